```python
import math
import jax, jax.numpy as jnp
from jax import lax
import numpy as np

D_MODEL = 1024
BATCH = 2
SEQ = 8192
DEPTH = 2

GRID_W = 64
A_HEADS = 8
A_HEAD_DIM = 64
DIL_CFG = ((128, 1), (512, 4), (2048, 16))
A_PAD = 1024
A_QB = 64
N_BUCKETS = 32
T5_MAX_DIST = 1024
B_HEADS = 4
B_DK = 128
B_DV = 128
B_CHUNK = 64
C_GROUPS = 4
C_WIDTH = 128
D_Q_HEADS = 8
D_KV_HEADS = 2
D_HEAD_DIM = 64
D_QB = 128
ROPE_THETA = 10000.0
FFN_HIDDEN = ((8 * D_MODEL + 2) // 3 + 255) // 256 * 256
A_W = A_HEADS * A_HEAD_DIM
B_WK = B_HEADS * B_DK
B_WV = B_HEADS * B_DV
AB_IN = 3 * A_W + 3 * B_WK + 2 * B_WV
AB_OUT = A_W + B_WV
C_W = C_GROUPS * C_WIDTH
DQ_W = D_Q_HEADS * D_HEAD_DIM
DKV_W = D_KV_HEADS * D_HEAD_DIM
CD_IN = C_W + DQ_W + 2 * DKV_W
CD_OUT = C_W + DQ_W
NEG_INF = -1e30

kernel_name = 'hybrid_dilated_hgrn2_fnet_gqa_encoder'


def rms_norm(x, gain, eps=1e-6):
    xf = x.astype(jnp.float32)
    y = xf * lax.rsqrt(jnp.mean(xf * xf, axis=-1, keepdims=True) + eps)
    return (y * gain.astype(jnp.float32)).astype(x.dtype)


def modulate(h, shift, scale):
    return h * (1.0 + scale[:, None, :]) + shift[:, None, :]


def split_cols(z, sizes):
    outs, start = [], 0
    for s in sizes:
        outs.append(z[..., start:start + s])
        start += s
    return outs


def t5_buckets(rel):
    half = N_BUCKETS // 2
    max_exact = half // 2
    n = np.abs(rel)
    large = max_exact + (np.log(np.maximum(n, 1) / max_exact) / np.log(T5_MAX_DIST / max_exact)
                         * (half - max_exact)).astype(np.int32)
    large = np.minimum(large, half - 1)
    return (np.where(rel > 0, half, 0) + np.where(n < max_exact, n, large)).astype(np.int32)


def dilated_offsets():
    offs = []
    for w, d in DIL_CFG:
        r = (w // 2) // d
        offs.append(np.arange(-r, r + 1) * d)
    return np.concatenate(offs).astype(np.int32)


def dilated_attention(q, k, v, bias_table):
    bsz, seq, nh, dh = q.shape
    f32 = jnp.float32
    n_br = len(DIL_CFG)
    offs = dilated_offsets()
    bias = bias_table[t5_buckets(offs)].astype(f32).T
    local = np.arange(A_QB)[:, None] + offs[None, :]
    gather_idx = local + A_PAD
    kv = jnp.pad(jnp.concatenate([k, v], axis=-1), ((0, 0), (A_PAD, A_PAD), (0, 0), (0, 0)))
    scale = dh ** -0.5

    def block(bi):
        q0 = bi * A_QB
        qb = lax.dynamic_slice_in_dim(q, q0, A_QB, axis=1).astype(f32)
        kv_win = lax.dynamic_slice_in_dim(kv, q0, A_QB + 2 * A_PAD, axis=1)
        kv_g = kv_win[:, gather_idx].astype(f32)
        kg, vg = kv_g[..., :dh], kv_g[..., dh:]
        logits = jnp.einsum('bqhd,bqjhd->bqhj', qb, kg) * scale + bias[None, None]
        pos = q0 + local
        valid = (pos >= 0) & (pos < seq)
        logits = jnp.where(valid[None, :, None, :], logits, NEG_INF)
        logits = logits.reshape(bsz, A_QB, nh, n_br, -1)
        m = jnp.max(logits, axis=-1, keepdims=True)
        p = jnp.exp(logits - m)
        den = jnp.sum(p, axis=-1)
        vg = vg.reshape(bsz, A_QB, n_br, -1, nh, dh)
        o_br = jnp.einsum('bqhrj,bqrjhd->bqhrd', p, vg) / den[..., None]
        m = m[..., 0]
        wgt = den * jnp.exp(m - jnp.max(m, axis=-1, keepdims=True))
        wgt = wgt / jnp.sum(wgt, axis=-1, keepdims=True)
        return jnp.einsum('bqhr,bqhrd->bqhd', wgt, o_br).astype(q.dtype)

    out = lax.map(block, jnp.arange(seq // A_QB))
    return out.transpose(1, 0, 2, 3, 4).reshape(bsz, seq, nh, dh)


def hgrn2_scan(q, k, log_f, v):
    bsz, seq, nh, dk = q.shape
    dv = v.shape[-1]
    nc = seq // B_CHUNK

    def to_chunks(a):
        return a.reshape(bsz, nc, B_CHUNK, nh, a.shape[-1]).transpose(1, 0, 3, 2, 4)

    causal = np.tril(np.ones((B_CHUNK, B_CHUNK), dtype=bool))[:, :, None]

    def step(state, inp):
        qc, kc, gc, vc = inp
        b = jnp.cumsum(gc, axis=2)
        o_inter = jnp.einsum('bhtk,bhkv->bhtv', qc * jnp.exp(b), state)
        diff = b[:, :, :, None, :] - b[:, :, None, :, :]
        decay = jnp.exp(jnp.where(causal, diff, -jnp.inf))
        attn = jnp.einsum('bhtk,bhsk,bhtsk->bhts', qc, kc, decay)
        o_intra = jnp.einsum('bhts,bhsv->bhtv', attn, vc)
        b_last = b[:, :, -1:, :]
        new_state = (jnp.exp(b_last[:, :, 0, :, None]) * state
                     + jnp.einsum('bhsk,bhsv->bhkv', kc * jnp.exp(b_last - b), vc))
        return new_state, o_inter + o_intra

    state0 = jnp.zeros((bsz, nh, dk, dv), jnp.float32)
    _, o = lax.scan(step, state0, (to_chunks(q), to_chunks(k), to_chunks(log_f), to_chunks(v)))
    return o.transpose(1, 0, 3, 2, 4).reshape(bsz, seq, nh, dv)


def hgrn2_mixer(q, f_fwd, f_bwd, i, g, lb, out_norm):
    bsz, seq, _ = q.shape
    f32 = jnp.float32
    shp_k = (bsz, seq, B_HEADS, B_DK)
    qf = q.astype(f32).reshape(shp_k) * B_DK ** -0.5
    vf = i.astype(f32).reshape(bsz, seq, B_HEADS, B_DV)
    lb = lb.reshape(2, B_HEADS, B_DK)

    def gates(fpre, lbd):
        fa = lbd + (1.0 - lbd) * jax.nn.sigmoid(fpre.astype(f32).reshape(shp_k))
        return 1.0 - fa, jnp.log(fa)

    k_f, lf_f = gates(f_fwd, lb[0])
    k_b, lf_b = gates(f_bwd, lb[1])
    o_fwd = hgrn2_scan(qf, k_f, lf_f, vf)
    o_bwd = jnp.flip(hgrn2_scan(jnp.flip(qf, 1), jnp.flip(k_b, 1), jnp.flip(lf_b, 1), jnp.flip(vf, 1)), 1)
    o = rms_norm(o_fwd + o_bwd, out_norm) * jax.nn.silu(g.astype(f32).reshape(bsz, seq, B_HEADS, B_DV))
    return o.reshape(bsz, seq, B_WV).astype(q.dtype)


def fourier_mixer(u):
    bsz, seq, _ = u.shape
    ug = u.astype(jnp.float32).reshape(bsz, seq, C_GROUPS, C_WIDTH)
    y = jnp.fft.fftn(ug, axes=(1, 3), norm='ortho').real
    return y.reshape(bsz, seq, C_W).astype(u.dtype)


def axial_rope_tables(seq):
    rows = seq // GRID_W
    row = jnp.repeat(jnp.arange(rows, dtype=jnp.float32), GRID_W)
    col = jnp.tile(jnp.arange(GRID_W, dtype=jnp.float32), rows)
    axis_dim = D_HEAD_DIM // 2
    inv_freq = jnp.power(ROPE_THETA, -jnp.arange(0, axis_dim, 2, dtype=jnp.float32) / axis_dim)
    ang_r = row[:, None] * inv_freq[None, :]
    ang_c = col[:, None] * inv_freq[None, :]
    return (jnp.cos(ang_r), jnp.sin(ang_r), jnp.cos(ang_c), jnp.sin(ang_c))


def rotate(x, cos, sin):
    m = x.shape[-1] // 2
    x1, x2 = x[..., :m], x[..., m:]
    cos = cos[None, :, None, :]
    sin = sin[None, :, None, :]
    return jnp.concatenate([x1 * cos - x2 * sin, x1 * sin + x2 * cos], axis=-1)


def apply_axial_rope(x, tabs):
    cr, sr, cc, sc = tabs
    half = x.shape[-1] // 2
    xf = x.astype(jnp.float32)
    out = jnp.concatenate([rotate(xf[..., :half], cr, sr), rotate(xf[..., half:], cc, sc)], axis=-1)
    return out.astype(x.dtype)


def gqa_attention(q, k, v):
    bsz, seq, hq, dh = q.shape
    hkv = k.shape[2]
    rep = hq // hkv
    f32 = jnp.float32
    qb = q.reshape(bsz, seq // D_QB, D_QB, hkv, rep, dh).transpose(1, 0, 2, 3, 4, 5)
    kf = k.astype(f32)
    vf = v.astype(f32)
    scale = dh ** -0.5

    def block(qblk):
        s = jnp.einsum('bqgrd,bkgd->bgrqk', qblk.astype(f32), kf) * scale
        p = jax.nn.softmax(s, axis=-1)
        return jnp.einsum('bgrqk,bkgd->bqgrd', p, vf).astype(q.dtype)

    o = lax.map(block, qb)
    return o.transpose(1, 0, 2, 3, 4, 5).reshape(bsz, seq, hq * dh)


def setup_inputs(seed: int = 0) -> dict:
    key = jax.random.key(seed)
    ks = jax.random.split(key, 16)
    f32 = jnp.float32
    n_even = (DEPTH + 1) // 2
    n_odd = DEPTH // 2

    def nrm(k, shape, scale):
        return jax.random.normal(k, shape, f32) * scale

    return {
        'x': nrm(ks[0], (BATCH, SEQ, D_MODEL), 1.0),
        'c': nrm(ks[1], (BATCH, D_MODEL), 1.0),
        't5_bias': nrm(ks[2], (N_BUCKETS, A_HEADS), 0.5),
        'hgrn_lb_logits': nrm(ks[3], (DEPTH + 1, 2, B_WK), 0.5),
        'ada_w': nrm(ks[4], (DEPTH, D_MODEL, 6 * D_MODEL), 0.5 * D_MODEL ** -0.5),
        'ada_b': nrm(ks[5], (DEPTH, 6 * D_MODEL), 0.02),
        'norm_gains': 1.0 + nrm(ks[6], (DEPTH, 4, D_MODEL), 0.02),
        'ab_w_in': nrm(ks[7], (n_even, D_MODEL, AB_IN), D_MODEL ** -0.5),
        'ab_w_out': nrm(ks[8], (n_even, AB_OUT, D_MODEL), AB_OUT ** -0.5),
        'hgrn_out_norm': 1.0 + nrm(ks[9], (n_even, B_DV), 0.02),
        'cd_w_in': nrm(ks[10], (n_odd, D_MODEL, CD_IN), D_MODEL ** -0.5),
        'cd_w_out': nrm(ks[11], (n_odd, CD_OUT, D_MODEL), CD_OUT ** -0.5),
        'qk_norm': 1.0 + nrm(ks[12], (n_odd, 2, D_HEAD_DIM), 0.02),
        'ffn_w_in': nrm(ks[13], (DEPTH, D_MODEL, 2 * FFN_HIDDEN), D_MODEL ** -0.5),
        'ffn_w_out': nrm(ks[14], (DEPTH, FFN_HIDDEN, D_MODEL), FFN_HIDDEN ** -0.5),
    }


def reference(x, c, t5_bias, hgrn_lb_logits, ada_w, ada_b, norm_gains, ab_w_in, ab_w_out,
              hgrn_out_norm, cd_w_in, cd_w_out, qk_norm, ffn_w_in, ffn_w_out):
    bsz, seq, _ = x.shape
    lb_all = jnp.cumsum(jax.nn.softmax(hgrn_lb_logits.astype(jnp.float32), axis=0), axis=0)
    rope = axial_rope_tables(seq)
    cond = jax.nn.silu(c)
    for layer in range(DEPTH):
        mod = cond @ ada_w[layer] + ada_b[layer]
        sh_m, sc_m, g_m, sh_f, sc_f, g_f = jnp.split(mod, 6, axis=-1)
        h = modulate(rms_norm(x, norm_gains[layer, 0]), sh_m, sc_m)
        j = layer // 2
        if layer % 2 == 0:
            z = h @ ab_w_in[j]
            qa, ka, va, qh, ffw, fbw, ih, gh = split_cols(
                z, (A_W, A_W, A_W, B_WK, B_WK, B_WK, B_WV, B_WV))
            hs = (bsz, seq, A_HEADS, A_HEAD_DIM)
            a_out = dilated_attention(qa.reshape(hs), ka.reshape(hs), va.reshape(hs), t5_bias)
            b_out = hgrn2_mixer(qh, ffw, fbw, ih, gh, lb_all[layer], hgrn_out_norm[j])
            y = jnp.concatenate([a_out.reshape(bsz, seq, A_W), b_out], axis=-1) @ ab_w_out[j]
        else:
            z = h @ cd_w_in[j]
            uc, qd, kd, vd = split_cols(z, (C_W, DQ_W, DKV_W, DKV_W))
            c_out = fourier_mixer(uc)
            qd = apply_axial_rope(rms_norm(qd.reshape(bsz, seq, D_Q_HEADS, D_HEAD_DIM), qk_norm[j, 0]), rope)
            kd = apply_axial_rope(rms_norm(kd.reshape(bsz, seq, D_KV_HEADS, D_HEAD_DIM), qk_norm[j, 1]), rope)
            d_out = gqa_attention(qd, kd, vd.reshape(bsz, seq, D_KV_HEADS, D_HEAD_DIM))
            y = jnp.concatenate([c_out, d_out], axis=-1) @ cd_w_out[j]
        x = x + g_m[:, None, :] * rms_norm(y, norm_gains[layer, 1])
        h = modulate(rms_norm(x, norm_gains[layer, 2]), sh_f, sc_f)
        gt, up = jnp.split(h @ ffn_w_in[layer], 2, axis=-1)
        y = (jax.nn.silu(gt) * up) @ ffn_w_out[layer]
        x = x + g_f[:, None, :] * rms_norm(y, norm_gains[layer, 3])
    return x
```

```python
import functools

import numpy as np
import jax
import jax.numpy as jnp
from jax import lax
from jax.experimental import pallas as pl
from jax.experimental.pallas import tpu as pltpu

F32 = jnp.float32
BF16 = jnp.bfloat16
LANES = 128
VMEM_LIMIT_BYTES = 56 * 2**20
NEG_INF = -1e30
EPS = 1e-6

GRID_W = 64
A_HEADS = 8
A_HEAD_DIM = 64
DIL_CFG = ((128, 1), (512, 4), (2048, 16))
N_BUCKETS = 32
T5_MAX_DIST = 1024
B_HEADS = 4
B_DK = 128
B_DV = 128
C_GROUPS = 4
C_WIDTH = 128
D_Q_HEADS = 8
D_KV_HEADS = 2
D_HEAD_DIM = 64
ROPE_THETA = 10000.0

DIL_TQ = 128
HGRN_CHUNK = 64
HGRN_SUB = 16
HGRN_BLOCK = 256
FFT_N1 = 64
FFT_N2 = 128


def _cparams(*sem):
    return pltpu.CompilerParams(dimension_semantics=sem, vmem_limit_bytes=VMEM_LIMIT_BYTES)


def _const_spec(shape):
    nd = len(shape)
    return pl.BlockSpec(shape, lambda *_: (0,) * nd, pipeline_mode=pl.Buffered(1))


def _sigmoid(x):
    return 1.0 / (1.0 + jnp.exp(-x))


def _dot(a, b):
    return jnp.dot(a.astype(BF16), b.astype(BF16), preferred_element_type=F32)


def _dot_nt(a, b):
    return lax.dot_general(a.astype(BF16), b.astype(BF16), (((1,), (1,)), ((), ())),
                           preferred_element_type=F32)


def _split2(a):
    hi = a.astype(BF16)
    lo = (a - hi.astype(F32)).astype(BF16)
    return hi, lo


def _split3(a):
    a1 = a.astype(BF16)
    r = a - a1.astype(F32)
    a2 = r.astype(BF16)
    a3 = (r - a2.astype(F32)).astype(BF16)
    return a1, a2, a3


def _dot_tab(tab_hi, tab_lo, x, *, tab_left):
    x_hi, x_lo = _split2(x)
    if tab_left:
        d = lambda t, v: jnp.dot(t, v, preferred_element_type=F32)
    else:
        d = lambda t, v: jnp.dot(v, t, preferred_element_type=F32)
    return d(tab_hi, x_hi) + (d(tab_hi, x_lo) + d(tab_lo, x_hi))


def _rms(x, gain):
    ms = jnp.mean(x * x, axis=-1, keepdims=True)
    return x * lax.rsqrt(ms + EPS) * gain


def _np_split2(t):
    t = np.asarray(t, np.float32)
    hi = jnp.asarray(t, F32).astype(BF16)
    lo = (jnp.asarray(t, F32) - hi.astype(F32)).astype(BF16)
    return hi, lo


def _mod_kernel(c_ref, w_ref, b_ref, o_ref):
    c = c_ref[...]
    o_ref[0] = _dot(c * _sigmoid(c), w_ref[0]) + b_ref[0]


def _ada_mod(c, ada_w, ada_b):
    depth, d, n6 = ada_w.shape
    bsz = c.shape[0]
    rows = 8
    cp = jnp.zeros((rows, d), F32).at[:bsz].set(c)
    tn = n6 // 4
    out = pl.pallas_call(
        _mod_kernel,
        out_shape=jax.ShapeDtypeStruct((depth, rows, n6), F32),
        grid=(depth, n6 // tn),
        in_specs=[pl.BlockSpec((rows, d), lambda l, j: (0, 0)),
                  pl.BlockSpec((1, d, tn), lambda l, j: (l, 0, j)),
                  pl.BlockSpec((1, 1, tn), lambda l, j: (l, 0, j))],
        out_specs=pl.BlockSpec((1, rows, tn), lambda l, j: (l, 0, j)),
        compiler_params=_cparams("parallel", "parallel"),
        name="ada_mod",
    )(cp, ada_w, ada_b.reshape(depth, 1, n6))
    return out[:, :bsz]


def _inproj_kernel(x_ref, gain_ref, sc_ref, sh_ref, w_ref, *o_refs):
    h = _rms(x_ref[0], gain_ref[...]) * (1.0 + sc_ref[0]) + sh_ref[0]
    z = _dot(h, w_ref[...])
    off = 0
    for o_ref in o_refs:
        n = o_ref.shape[-1]
        o_ref[0] = z[:, off:off + n].astype(o_ref.dtype)
        off += n


def _inproj(x, gain, sc, sh, w_bf16, splits, tm):
    bsz, seq, d = x.shape
    n = w_bf16.shape[1]
    assert sum(splits) == n
    vec = pl.BlockSpec((1, 1, d), lambda b, i: (b, 0, 0))
    return pl.pallas_call(
        _inproj_kernel,
        out_shape=tuple(jax.ShapeDtypeStruct((bsz, seq, s), F32) for s in splits),
        grid=(bsz, seq // tm),
        in_specs=[pl.BlockSpec((1, tm, d), lambda b, i: (b, i, 0)),
                  _const_spec((1, d)), vec, vec, _const_spec((d, n))],
        out_specs=tuple(pl.BlockSpec((1, tm, s), lambda b, i: (b, i, 0)) for s in splits),
        compiler_params=_cparams("parallel", "parallel"),
        name="inproj",
    )(x, gain.reshape(1, d), sc.reshape(bsz, 1, d), sh.reshape(bsz, 1, d), w_bf16)


def _t5_buckets(rel):
    half = N_BUCKETS // 2
    max_exact = half // 2
    n = np.abs(rel)
    large = max_exact + (np.log(np.maximum(n, 1) / max_exact) / np.log(T5_MAX_DIST / max_exact)
                         * (half - max_exact)).astype(np.int32)
    large = np.minimum(large, half - 1)
    return (np.where(rel > 0, half, 0) + np.where(n < max_exact, n, large)).astype(np.int32)


def _dil_bias(t5_bias, window, dil, tq):
    half = (window // 2) // dil
    assert half == tq // 2
    rel = np.arange(2 * tq)[None, :] - half - np.arange(tq)[:, None]
    inside = np.abs(rel) <= half
    buckets = _t5_buckets(np.where(inside, rel, 0) * dil)
    bias = jnp.transpose(t5_bias.astype(F32)[buckets], (2, 0, 1))
    return jnp.where(jnp.asarray(inside)[None], bias, NEG_INF)


def _dil_kernel(*refs, first, last, class_len, tq):
    if first:
        (q_ref, kp_ref, kc_ref, kn_ref, vp_ref, vc_ref, vn_ref, bias_ref,
         acc_o, m_o, l_o) = refs
    elif last:
        (q_ref, kp_ref, kc_ref, kn_ref, vp_ref, vc_ref, vn_ref, bias_ref,
         acc_i, m_i, l_i, out_o) = refs
    else:
        (q_ref, kp_ref, kc_ref, kn_ref, vp_ref, vc_ref, vn_ref, bias_ref,
         acc_i, m_i, l_i, acc_o, m_o, l_o) = refs
    i = pl.program_id(2)
    hq = tq // 2
    q = q_ref[0] * (A_HEAD_DIM ** -0.5)
    k3 = jnp.concatenate([kp_ref[0, hq:, :], kc_ref[0], kn_ref[0, :hq, :]], axis=0).astype(BF16)
    v3 = jnp.concatenate([vp_ref[0, hq:, :], vc_ref[0], vn_ref[0, :hq, :]], axis=0).astype(BF16)
    kpos = i * tq - hq + lax.broadcasted_iota(jnp.int32, (1, 2 * tq), 1)
    valid = jnp.logical_and(kpos >= 0, kpos < class_len)
    lane = lax.broadcasted_iota(jnp.int32, (1, LANES), 1)
    lo = lane < A_HEAD_DIM
    if not first:
        m_prev_all = m_i[0]
        l_prev_all = l_i[0]
    m_new_all = jnp.zeros((tq, LANES), F32)
    l_new_all = jnp.zeros((tq, LANES), F32)
    for j in range(A_HEADS // 2):
        cols = slice(j * LANES, (j + 1) * LANES)
        qj = q[:, cols]
        kj = k3[:, cols]
        vj = v3[:, cols]
        halves = []
        for half in range(2):
            h = 2 * j + half
            sel = lo if half == 0 else jnp.logical_not(lo)
            s = _dot_nt(jnp.where(sel, qj, 0.0), kj) + bias_ref[h]
            s = jnp.where(valid, s, NEG_INF)
            m_cur = jnp.max(s, axis=-1, keepdims=True)
            if first:
                m_new = m_cur
                p = jnp.exp(s - m_new)
                l_new = jnp.sum(p, axis=-1, keepdims=True)
                acc = _dot(p, vj)
            else:
                m_prev = m_prev_all[:, h:h + 1]
                l_prev = l_prev_all[:, h:h + 1]
                m_new = jnp.maximum(m_prev, m_cur)
                alpha = jnp.exp(m_prev - m_new)
                p = jnp.exp(s - m_new)
                l_new = alpha * l_prev + jnp.sum(p, axis=-1, keepdims=True)
                acc = alpha * acc_i[0, :, cols] + _dot(p, vj)
            if last:
                acc = acc / l_new
            else:
                m_new_all = jnp.where(lane == h, m_new, m_new_all)
                l_new_all = jnp.where(lane == h, l_new, l_new_all)
            halves.append(acc)
        merged = jnp.where(lo, halves[0], halves[1])
        if last:
            out_o[0, :, cols] = merged
        else:
            acc_o[0, :, cols] = merged
    if not last:
        m_o[0] = m_new_all
        l_o[0] = l_new_all


def _dilated_attention(z, t5_bias):
    bsz, seq, width = z.shape
    aw = A_HEADS * A_HEAD_DIM
    nblk = width // aw
    tq = DIL_TQ
    state = ()
    for bi, (window, dil) in enumerate(DIL_CFG):
        first, last = bi == 0, bi == len(DIL_CFG) - 1
        cl = seq // dil
        nq = cl // tq
        zv = z.reshape(bsz, cl, dil * width)

        def zspec(col, shift, nblk=nblk, nq=nq):
            return pl.BlockSpec(
                (1, tq, aw),
                lambda b, r, i: (b, jnp.clip(i + shift, 0, nq - 1), r * nblk + col))

        acc_spec = pl.BlockSpec((1, tq, aw), lambda b, r, i: (b, i, r))
        st_spec = pl.BlockSpec((1, tq, LANES), lambda b, r, i: (b, i, r))
        acc_shape = jax.ShapeDtypeStruct((bsz, cl, dil * aw), F32)
        st_shape = jax.ShapeDtypeStruct((bsz, cl, dil * LANES), F32)
        in_specs = [zspec(0, 0), zspec(1, -1), zspec(1, 0), zspec(1, 1),
                    zspec(2, -1), zspec(2, 0), zspec(2, 1),
                    _const_spec((A_HEADS, tq, 2 * tq))]
        args = [zv] * 7 + [_dil_bias(t5_bias, window, dil, tq)]
        if not first:
            in_specs += [acc_spec, st_spec, st_spec]
            acc, m, l = state
            args += [acc.reshape(bsz, cl, dil * aw), m.reshape(bsz, cl, dil * LANES),
                     l.reshape(bsz, cl, dil * LANES)]
        if last:
            out_shape, out_specs = acc_shape, acc_spec
        else:
            out_shape, out_specs = (acc_shape, st_shape, st_shape), (acc_spec, st_spec, st_spec)
        res = pl.pallas_call(
            functools.partial(_dil_kernel, first=first, last=last, class_len=cl, tq=tq),
            out_shape=out_shape,
            grid=(bsz, dil, nq),
            in_specs=in_specs,
            out_specs=out_specs,
            compiler_params=_cparams("parallel", "parallel", "parallel"),
            name=f"dilated_d{dil}",
        )(*args)
        if last:
            return res.reshape(bsz, seq, aw)
        state = tuple(r.reshape(bsz, seq, -1) for r in res)


def _hgrn_consts(reverse):
    c, s = HGRN_CHUNK, HGRN_SUB
    t = np.arange(c)
    tri = (t[None, :] >= t[:, None]) if reverse else (t[None, :] <= t[:, None])
    tt, tp, uu = np.meshgrid(np.arange(s), np.arange(s), np.arange(s), indexing="ij")
    keep = (uu >= tt) if reverse else (uu <= tt)
    sel = ((tp == tt) & keep).reshape(s, s * s)
    return (jnp.asarray(tri, F32).astype(BF16), jnp.asarray(sel, F32).astype(BF16))


def _hgrn_chunk(q, f, v, lb, st, tri, sel, ones, reverse):
    c, sub = HGRN_CHUNK, HGRN_SUB
    nsub = c // sub
    qs = q * (B_DK ** -0.5)
    fa = lb + (1.0 - lb) * _sigmoid(f)
    kk = 1.0 - fa
    g1, g2, g3 = _split3(jnp.log(fa))
    dtri = lambda x: jnp.dot(tri, x, preferred_element_type=F32)
    b = dtri(g1) + (dtri(g2) + dtri(g3))
    btot = b[0:1] if reverse else b[c - 1:c]
    vb = v.astype(BF16)
    out = _dot_nt(qs * jnp.exp(b), st)
    khat = (kk * jnp.exp(btot - b)).astype(BF16)
    st_new = st * jnp.exp(btot) + jnp.dot(v.T.astype(BF16), khat, preferred_element_type=F32)
    row = lax.broadcasted_iota(jnp.int32, (c, 1), 0)
    a_rows, d_rows = [], []
    for i in range(nsub):
        r0, r1 = i * sub, (i + 1) * sub
        bi, qi, ki, vi = b[r0:r1], qs[r0:r1], kk[r0:r1], v[r0:r1]
        xs = []
        for t in range(sub):
            dlt = jnp.minimum(bi[t:t + 1] - bi, 0.0)
            xs.append(jnp.exp(dlt) * qi[t:t + 1] * ki)
        y = _dot(jnp.concatenate(xs, axis=0), ones)
        d_rows.append(jnp.dot(sel, (y * jnp.concatenate([vi] * sub, axis=0)).astype(BF16),
                              preferred_element_type=F32))
        has_earlier = (i < nsub - 1) if reverse else (i > 0)
        if has_earlier:
            ref = b[r1:r1 + 1] if reverse else b[r0 - 1:r0]
            earlier = (row >= r1) if reverse else (row < r0)
            qt = qi * jnp.exp(bi - ref)
            kt = jnp.where(earlier, kk * jnp.exp(jnp.minimum(ref - b, 0.0)), 0.0)
            a_rows.append(_dot_nt(qt, kt))
        else:
            a_rows.append(jnp.zeros((sub, c), F32))
    out = out + jnp.concatenate(d_rows, axis=0) + _dot(jnp.concatenate(a_rows, axis=0), vb)
    return out, st_new


def _hgrn_lb(lg_ref, layer):
    lg = [lg_ref[l, 0] for l in range(lg_ref.shape[0])]
    mx = functools.reduce(jnp.maximum, lg)
    e = [jnp.exp(x - mx) for x in lg]
    return functools.reduce(jnp.add, e[:layer + 1]) / functools.reduce(jnp.add, e)


def _hgrn_kernel(qf_ref, ff_ref, vf_ref, qb_ref, fb_ref, vb_ref, lgf_ref, lgb_ref,
                 tril_ref, selt_ref, triu_ref, selu_ref, ones_ref,
                 of_ref, ob_ref, sf_sc, sb_sc, *, layer):
    @pl.when(pl.program_id(2) == 0)
    def _():
        sf_sc[...] = jnp.zeros_like(sf_sc)
        sb_sc[...] = jnp.zeros_like(sb_sc)

    c = HGRN_CHUNK
    nch = HGRN_BLOCK // c
    ones = ones_ref[...]
    lbf = _hgrn_lb(lgf_ref, layer)
    lbb = _hgrn_lb(lgb_ref, layer)
    for ch in range(nch):
        rows = pl.ds(ch * c, c)
        o, st = _hgrn_chunk(qf_ref[0, rows, :], ff_ref[0, rows, :], vf_ref[0, rows, :], lbf,
                            sf_sc[...], tril_ref[...], selt_ref[...], ones, False)
        of_ref[0, rows, :] = o
        sf_sc[...] = st
    for ch in reversed(range(nch)):
        rows = pl.ds(ch * c, c)
        o, st = _hgrn_chunk(qb_ref[0, rows, :], fb_ref[0, rows, :], vb_ref[0, rows, :], lbb,
                            sb_sc[...], triu_ref[...], selu_ref[...], ones, True)
        ob_ref[0, rows, :] = o
        sb_sc[...] = st


def _hgrn(z, lb_logits, layer, col0):
    bsz, seq, _ = z.shape
    t = HGRN_BLOCK
    nb = seq // t
    c0 = col0 // LANES
    hw = B_HEADS

    def zspec(group, rev):
        return pl.BlockSpec(
            (1, t, LANES),
            lambda b, h, j: (b, (nb - 1 - j) if rev else j, c0 + group * hw + h))

    def lgspec(direction):
        return pl.BlockSpec((lb_logits.shape[0], 1, 1, LANES),
                            lambda b, h, j: (0, direction * hw + h, 0, 0))

    tril, selt = _hgrn_consts(False)
    triu, selu = _hgrn_consts(True)
    ones = jnp.ones((LANES, LANES), BF16)
    lg = lb_logits.astype(F32).reshape(lb_logits.shape[0], 2 * hw, 1, LANES)
    o_shape = jax.ShapeDtypeStruct((bsz, seq, hw * B_DV), F32)
    return pl.pallas_call(
        functools.partial(_hgrn_kernel, layer=layer),
        out_shape=(o_shape, o_shape),
        grid=(bsz, hw, nb),
        in_specs=[zspec(0, False), zspec(1, False), zspec(3, False),
                  zspec(0, True), zspec(2, True), zspec(3, True),
                  lgspec(0), lgspec(1),
                  _const_spec(tril.shape), _const_spec(selt.shape),
                  _const_spec(triu.shape), _const_spec(selu.shape), _const_spec(ones.shape)],
        out_specs=(pl.BlockSpec((1, t, LANES), lambda b, h, j: (b, j, h)),
                   pl.BlockSpec((1, t, LANES), lambda b, h, j: (b, nb - 1 - j, h))),
        scratch_shapes=[pltpu.VMEM((B_DV, B_DK), F32), pltpu.VMEM((B_DV, B_DK), F32)],
        compiler_params=_cparams("parallel", "parallel", "arbitrary"),
        name="hgrn",
    )(z, z, z, z, z, z, lg, lg, tril, selt, triu, selu, ones)


def _residual_epilogue(y, x_ref, gain_ref, gate_ref, o_ref):
    o_ref[0] = x_ref[0] + gate_ref[0] * _rms(y, gain_ref[...])


def _out0_kernel(a_ref, of_ref, ob_ref, g_ref, on_ref, w_ref, x_ref, gain_ref, gate_ref, o_ref):
    o = of_ref[0] + ob_ref[0]
    g = g_ref[0]
    parts = [_rms(o[:, h * B_DV:(h + 1) * B_DV], on_ref[...]) for h in range(B_HEADS)]
    bn = jnp.concatenate(parts, axis=-1) * (g * _sigmoid(g))
    na = a_ref.shape[-1]
    y = _dot(a_ref[0], w_ref[:na, :]) + _dot(bn, w_ref[na:, :])
    _residual_epilogue(y, x_ref, gain_ref, gate_ref, o_ref)


def _out1_kernel(c_ref, d_ref, w_ref, x_ref, gain_ref, gate_ref, o_ref):
    nc = c_ref.shape[-1]
    y = _dot(c_ref[0], w_ref[:nc, :]) + _dot(d_ref[0], w_ref[nc:, :])
    _residual_epilogue(y, x_ref, gain_ref, gate_ref, o_ref)


def _row_spec(tm, width, col=0):
    return pl.BlockSpec((1, tm, width), lambda b, i: (b, i, col))


def _out0(a_out, o_f, o_b, z, g_col, out_norm, w_bf16, x, gain, gate, tm):
    bsz, seq, d = x.shape
    wv = B_HEADS * B_DV
    vec = pl.BlockSpec((1, 1, d), lambda b, i: (b, 0, 0))
    return pl.pallas_call(
        _out0_kernel,
        out_shape=jax.ShapeDtypeStruct(x.shape, F32),
        grid=(bsz, seq // tm),
        in_specs=[_row_spec(tm, a_out.shape[-1]), _row_spec(tm, wv), _row_spec(tm, wv),
                  _row_spec(tm, wv, g_col // wv), _const_spec((1, B_DV)),
                  _const_spec(w_bf16.shape), _row_spec(tm, d), _const_spec((1, d)), vec],
        out_specs=_row_spec(tm, d),
        compiler_params=_cparams("parallel", "parallel"),
        name="out0",
    )(a_out, o_f, o_b, z, out_norm.reshape(1, B_DV), w_bf16, x, gain.reshape(1, d),
      gate.reshape(bsz, 1, d))


def _out1(c_out, d_out, w_bf16, x, gain, gate, tm):
    bsz, seq, d = x.shape
    vec = pl.BlockSpec((1, 1, d), lambda b, i: (b, 0, 0))
    return pl.pallas_call(
        _out1_kernel,
        out_shape=jax.ShapeDtypeStruct(x.shape, F32),
        grid=(bsz, seq // tm),
        in_specs=[_row_spec(tm, c_out.shape[-1]), _row_spec(tm, d_out.shape[-1]),
                  _const_spec(w_bf16.shape), _row_spec(tm, d), _const_spec((1, d)), vec],
        out_specs=_row_spec(tm, d),
        compiler_params=_cparams("parallel", "parallel"),
        name="out1",
    )(c_out, d_out, w_bf16, x, gain.reshape(1, d), gate.reshape(bsz, 1, d))


def _ffn_kernel(x_ref, g1_ref, sc_ref, sh_ref, wi_ref, wo_ref, g2_ref, gate_ref, o_ref, *, nchunk):
    x = x_ref[0]
    h = (_rms(x, g1_ref[...]) * (1.0 + sc_ref[0]) + sh_ref[0]).astype(BF16)
    hidden = wo_ref.shape[0]
    ck = hidden // nchunk
    y = None
    for c in range(nchunk):
        gt = jnp.dot(h, wi_ref[:, c * ck:(c + 1) * ck], preferred_element_type=F32)
        up = jnp.dot(h, wi_ref[:, hidden + c * ck:hidden + (c + 1) * ck], preferred_element_type=F32)
        part = _dot(gt * _sigmoid(gt) * up, wo_ref[c * ck:(c + 1) * ck, :])
        y = part if y is None else y + part
    o_ref[0] = x + gate_ref[0] * _rms(y, g2_ref[...])


def _ffn(x, g1, sc, sh, wi_bf16, wo_bf16, g2, gate, tm):
    bsz, seq, d = x.shape
    vec = pl.BlockSpec((1, 1, d), lambda b, i: (b, 0, 0))
    hidden = wo_bf16.shape[0]
    nchunk = 2 if (hidden // 2) % LANES == 0 else 1
    return pl.pallas_call(
        functools.partial(_ffn_kernel, nchunk=nchunk),
        out_shape=jax.ShapeDtypeStruct(x.shape, F32),
        grid=(bsz, seq // tm),
        in_specs=[_row_spec(tm, d), _const_spec((1, d)), vec, vec,
                  _const_spec(wi_bf16.shape), _const_spec(wo_bf16.shape),
                  _const_spec((1, d)), vec],
        out_specs=_row_spec(tm, d),
        compiler_params=_cparams("parallel", "parallel"),
        name="ffn",
    )(x, g1.reshape(1, d), sc.reshape(bsz, 1, d), sh.reshape(bsz, 1, d), wi_bf16, wo_bf16,
      g2.reshape(1, d), gate.reshape(bsz, 1, d))


def _fft1_kernel(th_ref, tl_ref, u_ref, p_ref):
    p_ref[0] = _dot_tab(th_ref[...], tl_ref[...], u_ref[0], tab_left=True)


def _fft2_kernel(pr_ref, pi_ref, tc_ref, ts_ref, fh_ref, fl_ref, wh_ref, wl_ref, o_ref, *, scale):
    tc = tc_ref[0]
    ts = ts_ref[0]
    qr, qi = [], []
    for g in range(C_GROUPS):
        cols = slice(g * C_WIDTH, (g + 1) * C_WIDTH)
        pr = pr_ref[0, 0, 0, :, cols]
        pim = pi_ref[0, 0, 0, :, cols]
        qr.append(pr * tc + pim * ts)
        qi.append(pim * tc - pr * ts)
    q = jnp.concatenate([jnp.concatenate(qr, axis=1), jnp.concatenate(qi, axis=1)], axis=0)
    xx = _dot_tab(fh_ref[...], fl_ref[...], q, tab_left=True)
    n2 = xx.shape[0] // 2
    for g in range(C_GROUPS):
        cols = slice(g * C_WIDTH, (g + 1) * C_WIDTH)
        xg = jnp.concatenate([xx[:n2, cols], xx[n2:, cols]], axis=1)
        o_ref[0, :, cols] = _dot_tab(wh_ref[...], wl_ref[...], xg, tab_left=False) * scale


def _fourier_mixer(u):
    bsz, seq, cw = u.shape
    n2 = FFT_N2
    n1 = seq // n2
    assert n1 * n2 == seq and cw == C_GROUPS * C_WIDTH
    a1 = 2.0 * np.pi * np.outer(np.arange(n1), np.arange(n1)) / n1
    f1 = np.concatenate([np.cos(a1), -np.sin(a1)], axis=0)
    a2 = 2.0 * np.pi * np.outer(np.arange(n2), np.arange(n2)) / n2
    c2, s2 = np.cos(a2), np.sin(a2)
    f2 = np.block([[c2, s2], [-s2, c2]])
    aw = 2.0 * np.pi * np.outer(np.arange(C_WIDTH), np.arange(C_WIDTH)) / C_WIDTH
    fw = np.concatenate([np.cos(aw), np.sin(aw)], axis=0)
    at = 2.0 * np.pi * np.outer(np.arange(n1), np.arange(n2)) / seq
    tw_c = jnp.asarray(np.repeat(np.cos(at)[:, :, None], C_WIDTH, axis=2), F32)
    tw_s = jnp.asarray(np.repeat(np.sin(at)[:, :, None], C_WIDTH, axis=2), F32)
    f1h, f1l = _np_split2(f1)
    f2h, f2l = _np_split2(f2)
    fwh, fwl = _np_split2(fw)

    ncol = n2 * cw
    tn = 8192
    p = pl.pallas_call(
        _fft1_kernel,
        out_shape=jax.ShapeDtypeStruct((bsz, 2 * n1, ncol), F32),
        grid=(bsz, ncol // tn),
        in_specs=[_const_spec(f1h.shape), _const_spec(f1l.shape),
                  pl.BlockSpec((1, n1, tn), lambda b, j: (b, 0, j))],
        out_specs=pl.BlockSpec((1, 2 * n1, tn), lambda b, j: (b, 0, j)),
        compiler_params=_cparams("parallel", "parallel"),
        name="fft1",
    )(f1h, f1l, u.reshape(bsz, n1, ncol))
    p5 = p.reshape(bsz, 2, n1, n2, cw)
    tw_spec = pl.BlockSpec((1, n2, C_WIDTH), lambda b, k: (k, 0, 0))
    y = pl.pallas_call(
        functools.partial(_fft2_kernel, scale=float(1.0 / np.sqrt(seq * C_WIDTH))),
        out_shape=jax.ShapeDtypeStruct((bsz, n2, n1 * cw), F32),
        grid=(bsz, n1),
        in_specs=[pl.BlockSpec((1, 1, 1, n2, cw), lambda b, k: (b, 0, k, 0, 0)),
                  pl.BlockSpec((1, 1, 1, n2, cw), lambda b, k: (b, 1, k, 0, 0)),
                  tw_spec, tw_spec,
                  _const_spec(f2h.shape), _const_spec(f2l.shape),
                  _const_spec(fwh.shape), _const_spec(fwl.shape)],
        out_specs=pl.BlockSpec((1, n2, cw), lambda b, k: (b, 0, k)),
        compiler_params=_cparams("parallel", "parallel"),
        name="fft2",
    )(p5, p5, tw_c, tw_s, f2h, f2l, fwh, fwl)
    return y.reshape(bsz, seq, cw)


def _head_perm():
    rep = D_Q_HEADS // D_KV_HEADS
    cols = []
    for j in range(rep):
        for g in range(D_KV_HEADS):
            h = g * rep + j
            cols.extend(range(h * D_HEAD_DIM, (h + 1) * D_HEAD_DIM))
    return np.asarray(cols, np.int32)


def _rope_tables(seq):
    rows = seq // GRID_W
    row = jnp.repeat(jnp.arange(rows, dtype=F32), GRID_W)
    col = jnp.tile(jnp.arange(GRID_W, dtype=F32), rows)
    axis_dim = D_HEAD_DIM // 2
    inv_freq = jnp.power(ROPE_THETA, -jnp.arange(0, axis_dim, 2, dtype=F32) / axis_dim)
    ang_r = row[:, None] * inv_freq[None, :]
    ang_c = col[:, None] * inv_freq[None, :]
    cr, sr, cc, sc = jnp.cos(ang_r), jnp.sin(ang_r), jnp.cos(ang_c), jnp.sin(ang_c)
    cos = jnp.concatenate([cr, cr, cc, cc], axis=1)
    sin = jnp.concatenate([-sr, sr, -sc, sc], axis=1)
    reps = LANES // D_HEAD_DIM
    return jnp.tile(cos, (1, reps)), jnp.tile(sin, (1, reps))


def _qkprep_kernel(q_ref, k_ref, v_ref, cos_ref, sin_ref, bd_h_ref, bd_l_ref, gq_ref, gk_ref,
                   qo_ref, ko_ref, vo_ref):
    cos = cos_ref[...]
    sin = sin_ref[...]
    quarter = D_HEAD_DIM // 4
    lane = lax.broadcasted_iota(jnp.int32, (1, LANES), 1)
    first_of_pair = (lane // quarter) % 2 == 0

    def norm_rope(x, gain, scale):
        ms = _dot_tab(bd_h_ref[...], bd_l_ref[...], x * x, tab_left=False)
        xn = x * lax.rsqrt(ms + EPS) * gain
        partner = jnp.where(first_of_pair, pltpu.roll(xn, LANES - quarter, 1),
                            pltpu.roll(xn, quarter, 1))
        return ((xn * cos + partner * sin) * scale).astype(BF16)

    for j in range(q_ref.shape[-1] // LANES):
        cols = slice(j * LANES, (j + 1) * LANES)
        qo_ref[0, :, cols] = norm_rope(q_ref[0, :, cols], gq_ref[...], D_HEAD_DIM ** -0.5)
    ko_ref[0] = norm_rope(k_ref[0], gk_ref[...], 1.0)
    vo_ref[0] = v_ref[0].astype(BF16)


def _qkprep(qkv, qk_norm_j, tm):
    bsz, seq, _ = qkv.shape
    qw = D_Q_HEADS * D_HEAD_DIM
    kw = D_KV_HEADS * D_HEAD_DIM
    assert kw == LANES
    cos, sin = _rope_tables(seq)
    bd = np.kron(np.eye(LANES // D_HEAD_DIM), np.full((D_HEAD_DIM, D_HEAD_DIM), 1.0 / D_HEAD_DIM))
    bd_h, bd_l = _np_split2(bd)
    reps = LANES // D_HEAD_DIM
    gq = jnp.tile(qk_norm_j[0].astype(F32), reps).reshape(1, LANES)
    gk = jnp.tile(qk_norm_j[1].astype(F32), reps).reshape(1, LANES)
    tab = pl.BlockSpec((tm, LANES), lambda b, i: (i, 0))
    return pl.pallas_call(
        _qkprep_kernel,
        out_shape=(jax.ShapeDtypeStruct((bsz, seq, qw), BF16),
                   jax.ShapeDtypeStruct((bsz, seq, kw), BF16),
                   jax.ShapeDtypeStruct((bsz, seq, kw), BF16)),
        grid=(bsz, seq // tm),
        in_specs=[_row_spec(tm, qw, 0), _row_spec(tm, kw, qw // kw), _row_spec(tm, kw, qw // kw + 1),
                  tab, tab, _const_spec(bd_h.shape), _const_spec(bd_l.shape),
                  _const_spec((1, LANES)), _const_spec((1, LANES))],
        out_specs=(_row_spec(tm, qw), _row_spec(tm, kw), _row_spec(tm, kw)),
        compiler_params=_cparams("parallel", "parallel"),
        name="qkprep",
    )(qkv, qkv, qkv, cos, sin, bd_h, bd_l, gq, gk)


def _flash_kernel(q_ref, k_ref, v_ref, o_ref, m_sc, l_sc, acc_sc):
    kv = pl.program_id(2)

    @pl.when(kv == 0)
    def _():
        m_sc[...] = jnp.full_like(m_sc, -jnp.inf)
        l_sc[...] = jnp.zeros_like(l_sc)
        acc_sc[...] = jnp.zeros_like(acc_sc)

    k = k_ref[0]
    v = v_ref[0]
    lane = lax.broadcasted_iota(jnp.int32, (1, LANES), 1)
    lo = lane < D_HEAD_DIM
    nblk = q_ref.shape[-1] // LANES
    for j in range(nblk):
        cols = slice(j * LANES, (j + 1) * LANES)
        qj = q_ref[0, :, cols]
        alphas, pvs = [], []
        for g in range(D_KV_HEADS):
            idx = j * D_KV_HEADS + g
            sel = lo if g == 0 else jnp.logical_not(lo)
            s = lax.dot_general(jnp.where(sel, qj, jnp.zeros_like(qj)), k, (((1,), (1,)), ((), ())),
                                preferred_element_type=F32)
            m_prev = m_sc[idx]
            m_new = jnp.maximum(m_prev, jnp.max(s, axis=-1, keepdims=True))
            alpha = jnp.exp(m_prev - m_new)
            p = jnp.exp(s - m_new[:, 0:1])
            l_sc[idx] = alpha * l_sc[idx] + jnp.sum(p, axis=-1, keepdims=True)
            m_sc[idx] = m_new
            alphas.append(alpha)
            pvs.append(jnp.dot(p.astype(BF16), v, preferred_element_type=F32))
        acc_sc[:, cols] = (jnp.where(lo, alphas[0], alphas[1]) * acc_sc[:, cols]
                           + jnp.where(lo, pvs[0], pvs[1]))

    @pl.when(kv == pl.num_programs(2) - 1)
    def _():
        for j in range(nblk):
            cols = slice(j * LANES, (j + 1) * LANES)
            l = jnp.where(lo, l_sc[j * D_KV_HEADS], l_sc[j * D_KV_HEADS + 1])
            o_ref[0, :, cols] = (acc_sc[:, cols] / l).astype(o_ref.dtype)


def _flash(q, k, v, tq, tk):
    bsz, seq, qw = q.shape
    kw = k.shape[-1]
    return pl.pallas_call(
        _flash_kernel,
        out_shape=jax.ShapeDtypeStruct((bsz, seq, qw), F32),
        grid=(bsz, seq // tq, seq // tk),
        in_specs=[pl.BlockSpec((1, tq, qw), lambda b, i, j: (b, i, 0)),
                  pl.BlockSpec((1, tk, kw), lambda b, i, j: (b, j, 0)),
                  pl.BlockSpec((1, tk, kw), lambda b, i, j: (b, j, 0))],
        out_specs=pl.BlockSpec((1, tq, qw), lambda b, i, j: (b, i, 0)),
        scratch_shapes=[pltpu.VMEM((D_Q_HEADS, tq, LANES), F32),
                        pltpu.VMEM((D_Q_HEADS, tq, LANES), F32),
                        pltpu.VMEM((tq, qw), F32)],
        compiler_params=_cparams("parallel", "parallel", "arbitrary"),
        name="flash",
    )(q, k, v)


def kernel(x, c, t5_bias, hgrn_lb_logits, ada_w, ada_b, norm_gains, ab_w_in, ab_w_out,
           hgrn_out_norm, cd_w_in, cd_w_out, qk_norm, ffn_w_in, ffn_w_out):
    bsz, seq, d = x.shape
    depth = ada_w.shape[0]
    mod = _ada_mod(c.astype(F32), ada_w, ada_b)
    perm = _head_perm()
    aw = A_HEADS * A_HEAD_DIM
    cw = C_GROUPS * C_WIDTH
    qw = D_Q_HEADS * D_HEAD_DIM
    tm_in = min(256, seq)
    tm = min(512, seq)
    for layer in range(depth):
        sh_m, sc_m, g_m, sh_f, sc_f, g_f = [mod[layer, :, i * d:(i + 1) * d] for i in range(6)]
        gains = norm_gains[layer]
        j = layer // 2
        if layer % 2 == 0:
            w_in = ab_w_in[j].astype(BF16)
            (z,) = _inproj(x, gains[0], sc_m, sh_m, w_in, (w_in.shape[1],), tm_in)
            a_out = _dilated_attention(z, t5_bias)
            o_f, o_b = _hgrn(z, hgrn_lb_logits, layer, 3 * aw)
            g_col = 3 * aw + 3 * B_HEADS * B_DK + B_HEADS * B_DV
            x = _out0(a_out, o_f, o_b, z, g_col, hgrn_out_norm[j], ab_w_out[j].astype(BF16),
                      x, gains[1], g_m, tm)
        else:
            w_full = cd_w_in[j]
            w_in = jnp.concatenate([w_full[:, :cw], w_full[:, cw:cw + qw][:, perm],
                                    w_full[:, cw + qw:]], axis=1).astype(BF16)
            u, qkv = _inproj(x, gains[0], sc_m, sh_m, w_in, (cw, w_in.shape[1] - cw), tm)
            c_out = _fourier_mixer(u)
            qn, kn, vn = _qkprep(qkv, qk_norm[j], tm)
            d_out = _flash(qn, kn, vn, tm, tm)
            w_out_full = cd_w_out[j]
            w_out = jnp.concatenate([w_out_full[:cw], w_out_full[cw:][perm]], axis=0).astype(BF16)
            x = _out1(c_out, d_out, w_out, x, gains[1], g_m, tm)
        x = _ffn(x, gains[2], sc_f, sh_f, ffn_w_in[layer].astype(BF16), ffn_w_out[layer].astype(BF16),
                 gains[3], g_f, tm)
    return x
```

```python
import functools

import numpy as np
import jax
import jax.numpy as jnp
from jax import lax
from jax.experimental import pallas as pl
from jax.experimental.pallas import tpu as pltpu

F32 = jnp.float32
BF16 = jnp.bfloat16
LANES = 128
VMEM_LIMIT_BYTES = 56 * 2**20
NEG_INF = -1e30
EPS = 1e-6

GRID_W = 64
A_HEADS = 8
A_HEAD_DIM = 64
DIL_CFG = ((128, 1), (512, 4), (2048, 16))
N_BUCKETS = 32
T5_MAX_DIST = 1024
B_HEADS = 4
B_DK = 128
B_DV = 128
C_GROUPS = 4
C_WIDTH = 128
D_Q_HEADS = 8
D_KV_HEADS = 2
D_HEAD_DIM = 64
ROPE_THETA = 10000.0

DIL_TQ = 128
HGRN_BLOCK = 256
FLASH_TQ = 1024
FLASH_TK = 512
FLASH_KEY_UNIT = 128
FLASH_QUERY_UNIT = 256
FFT_N2 = 128
LOG2E = 1.4426950408889634


def _cparams(*sem):
    return pltpu.CompilerParams(dimension_semantics=sem, vmem_limit_bytes=VMEM_LIMIT_BYTES)


def _const_spec(shape):
    nd = len(shape)
    return pl.BlockSpec(shape, lambda *_: (0,) * nd, pipeline_mode=pl.Buffered(1))


def _sigmoid(x):
    return 1.0 / (1.0 + jnp.exp(-x))


def _dot(a, b):
    return jnp.dot(a.astype(BF16), b.astype(BF16), preferred_element_type=F32)


def _dot_nt(a, b):
    return lax.dot_general(a.astype(BF16), b.astype(BF16), (((1,), (1,)), ((), ())),
                           preferred_element_type=F32)


def _split2(a):
    hi = a.astype(BF16)
    lo = (a - hi.astype(F32)).astype(BF16)
    return hi, lo


def _split3(a):
    a1 = a.astype(BF16)
    r = a - a1.astype(F32)
    a2 = r.astype(BF16)
    a3 = (r - a2.astype(F32)).astype(BF16)
    return a1, a2, a3


def _dot_tab(tab_hi, tab_lo, x, *, tab_left):
    x_hi, x_lo = _split2(x)
    if tab_left:
        d = lambda t, v: jnp.dot(t, v, preferred_element_type=F32)
    else:
        d = lambda t, v: jnp.dot(v, t, preferred_element_type=F32)
    return d(tab_hi, x_hi) + (d(tab_hi, x_lo) + d(tab_lo, x_hi))


def _rms(x, gain):
    ms = jnp.mean(x * x, axis=-1, keepdims=True)
    return x * lax.rsqrt(ms + EPS) * gain


def _np_split2(t):
    t = np.asarray(t, np.float32)
    hi = jnp.asarray(t, F32).astype(BF16)
    lo = (jnp.asarray(t, F32) - hi.astype(F32)).astype(BF16)
    return hi, lo


def _mod_kernel(c_ref, w_ref, b_ref, o_ref):
    c = c_ref[...]
    o_ref[0] = _dot(c * _sigmoid(c), w_ref[0]) + b_ref[0]


def _ada_mod(c, ada_w, ada_b):
    depth, d, n6 = ada_w.shape
    bsz = c.shape[0]
    rows = 8
    cp = jnp.zeros((rows, d), F32).at[:bsz].set(c)
    tn = n6 // 4
    out = pl.pallas_call(
        _mod_kernel,
        out_shape=jax.ShapeDtypeStruct((depth, rows, n6), F32),
        grid=(depth, n6 // tn),
        in_specs=[pl.BlockSpec((rows, d), lambda l, j: (0, 0)),
                  pl.BlockSpec((1, d, tn), lambda l, j: (l, 0, j)),
                  pl.BlockSpec((1, 1, tn), lambda l, j: (l, 0, j))],
        out_specs=pl.BlockSpec((1, rows, tn), lambda l, j: (l, 0, j)),
        compiler_params=_cparams("parallel", "parallel"),
        name="ada_mod",
    )(cp, ada_w, ada_b.reshape(depth, 1, n6))
    return out[:, :bsz]


def _inproj_kernel(x_ref, gain_ref, sc_ref, sh_ref, w_ref, *o_refs):
    h = _rms(x_ref[0], gain_ref[...]) * (1.0 + sc_ref[0]) + sh_ref[0]
    z = _dot(h, w_ref[...])
    off = 0
    for o_ref in o_refs:
        n = o_ref.shape[-1]
        o_ref[0] = z[:, off:off + n].astype(o_ref.dtype)
        off += n


def _inproj(x, gain, sc, sh, w_bf16, splits, tm):
    bsz, seq, d = x.shape
    n = w_bf16.shape[1]
    assert sum(splits) == n
    vec = pl.BlockSpec((1, 1, d), lambda b, i: (b, 0, 0))
    return pl.pallas_call(
        _inproj_kernel,
        out_shape=tuple(jax.ShapeDtypeStruct((bsz, seq, s), F32) for s in splits),
        grid=(bsz, seq // tm),
        in_specs=[pl.BlockSpec((1, tm, d), lambda b, i: (b, i, 0)),
                  _const_spec((1, d)), vec, vec, _const_spec((d, n))],
        out_specs=tuple(pl.BlockSpec((1, tm, s), lambda b, i: (b, i, 0)) for s in splits),
        compiler_params=_cparams("parallel", "parallel"),
        name="inproj",
    )(x, gain.reshape(1, d), sc.reshape(bsz, 1, d), sh.reshape(bsz, 1, d), w_bf16)


def _t5_buckets(rel):
    half = N_BUCKETS // 2
    max_exact = half // 2
    n = np.abs(rel)
    large = max_exact + (np.log(np.maximum(n, 1) / max_exact) / np.log(T5_MAX_DIST / max_exact)
                         * (half - max_exact)).astype(np.int32)
    large = np.minimum(large, half - 1)
    return (np.where(rel > 0, half, 0) + np.where(n < max_exact, n, large)).astype(np.int32)


def _dil_bias(t5_bias, window, dil, tq):
    half = (window // 2) // dil
    assert half == tq // 2
    rel = np.arange(2 * tq)[None, :] - half - np.arange(tq)[:, None]
    inside = np.abs(rel) <= half
    buckets = _t5_buckets(np.where(inside, rel, 0) * dil)
    onehot =jnp.asarray(np.eye(N_BUCKETS, dtype=np.float32)[buckets])
    bias = jnp.einsum("qkn,nh->hqk", onehot, t5_bias.astype(F32), precision=lax.Precision.HIGHEST)
    return jnp.where(jnp.asarray(inside)[None], bias, NEG_INF)


def _dil_kernel(*refs, first, last, class_len, tq):
    if first:
        (q_ref, kp_ref, kc_ref, kn_ref, vp_ref, vc_ref, vn_ref, bias_ref,
         acc_o, m_o, l_o) = refs
    elif last:
        (q_ref, kp_ref, kc_ref, kn_ref, vp_ref, vc_ref, vn_ref, bias_ref,
         acc_i, m_i, l_i, out_o) = refs
    else:
        (q_ref, kp_ref, kc_ref, kn_ref, vp_ref, vc_ref, vn_ref, bias_ref,
         acc_i, m_i, l_i, acc_o, m_o, l_o) = refs
    i = pl.program_id(2)
    hq = tq // 2
    q = q_ref[0] * (A_HEAD_DIM ** -0.5)
    k3 = jnp.concatenate([kp_ref[0, hq:, :], kc_ref[0], kn_ref[0, :hq, :]], axis=0).astype(BF16)
    v3 = jnp.concatenate([vp_ref[0, hq:, :], vc_ref[0], vn_ref[0, :hq, :]], axis=0).astype(BF16)
    kpos = i * tq - hq + lax.broadcasted_iota(jnp.int32, (1, 2 * tq), 1)
    valid = jnp.logical_and(kpos >= 0, kpos < class_len)
    lane = lax.broadcasted_iota(jnp.int32, (1, LANES), 1)
    lo = lane < A_HEAD_DIM
    if not first:
        m_prev_all = m_i[0]
        l_prev_all = l_i[0]
    m_new_all = jnp.zeros((tq, LANES), F32)
    l_new_all = jnp.zeros((tq, LANES), F32)
    for j in range(A_HEADS // 2):
        cols = slice(j * LANES, (j + 1) * LANES)
        qj = q[:, cols]
        kj = k3[:, cols]
        vj = v3[:, cols]
        halves = []
        for half in range(2):
            h = 2 * j + half
            sel = lo if half == 0 else jnp.logical_not(lo)
            s = _dot_nt(jnp.where(sel, qj, 0.0), kj) + bias_ref[h]
            s = jnp.where(valid, s, NEG_INF)
            m_cur = jnp.max(s, axis=-1, keepdims=True)
            if first:
                m_new = m_cur
                p = jnp.exp(s - m_new)
                l_new = jnp.sum(p, axis=-1, keepdims=True)
                acc = _dot(p, vj)
            else:
                m_prev = m_prev_all[:, h:h + 1]
                l_prev = l_prev_all[:, h:h + 1]
                m_new = jnp.maximum(m_prev, m_cur)
                alpha = jnp.exp(m_prev - m_new)
                p = jnp.exp(s - m_new)
                l_new = alpha * l_prev + jnp.sum(p, axis=-1, keepdims=True)
                acc = alpha * acc_i[0, :, cols] + _dot(p, vj)
            if last:
                acc = acc / l_new
            else:
                m_new_all = jnp.where(lane == h, m_new, m_new_all)
                l_new_all = jnp.where(lane == h, l_new, l_new_all)
            halves.append(acc)
        merged = jnp.where(lo, halves[0], halves[1])
        if last:
            out_o[0, :, cols] = merged
        else:
            acc_o[0, :, cols] = merged
    if not last:
        m_o[0] = m_new_all
        l_o[0] = l_new_all


def _dilated_attention(z, t5_bias):
    bsz, seq, width = z.shape
    aw = A_HEADS * A_HEAD_DIM
    nblk = width // aw
    tq = DIL_TQ
    state = ()
    for bi, (window, dil) in enumerate(DIL_CFG):
        first, last = bi == 0, bi == len(DIL_CFG) - 1
        cl = seq // dil
        nq = cl // tq
        zv = z.reshape(bsz, cl, dil * width)

        def zspec(col, shift, nblk=nblk, nq=nq):
            return pl.BlockSpec(
                (1, tq, aw),
                lambda b, r, i: (b, jnp.clip(i + shift, 0, nq - 1), r * nblk + col))

        acc_spec = pl.BlockSpec((1, tq, aw), lambda b, r, i: (b, i, r))
        st_spec = pl.BlockSpec((1, tq, LANES), lambda b, r, i: (b, i, r))
        acc_shape = jax.ShapeDtypeStruct((bsz, cl, dil * aw), F32)
        st_shape = jax.ShapeDtypeStruct((bsz, cl, dil * LANES), F32)
        in_specs = [zspec(0, 0), zspec(1, -1), zspec(1, 0), zspec(1, 1),
                    zspec(2, -1), zspec(2, 0), zspec(2, 1),
                    _const_spec((A_HEADS, tq, 2 * tq))]
        args = [zv] * 7 + [_dil_bias(t5_bias, window, dil, tq)]
        if not first:
            in_specs += [acc_spec, st_spec, st_spec]
            acc, m, l = state
            args += [acc.reshape(bsz, cl, dil * aw), m.reshape(bsz, cl, dil * LANES),
                     l.reshape(bsz, cl, dil * LANES)]
        if last:
            out_shape, out_specs = acc_shape, acc_spec
        else:
            out_shape, out_specs = (acc_shape, st_shape, st_shape), (acc_spec, st_spec, st_spec)
        res = pl.pallas_call(
            functools.partial(_dil_kernel, first=first, last=last, class_len=cl, tq=tq),
            out_shape=out_shape,
            grid=(bsz, dil, nq),
            in_specs=in_specs,
            out_specs=out_specs,
            compiler_params=_cparams("parallel", "parallel", "parallel"),
            name=f"dilated_d{dil}",
        )(*args)
        if last:
            return res.reshape(bsz, seq, aw)
        state = tuple(r.reshape(bsz, seq, -1) for r in res)


def _hgrn_consts(reverse):
    t = HGRN_BLOCK
    r = np.arange(t)
    u = r[None, :]
    row = r[:, None]
    nmats = [(u >= row) if reverse else (u <= row)]
    masks = []
    m = t // 2
    while m >= 1:
        grp = r // (2 * m)
        in_first = (r % (2 * m)) < m
        same = grp[:, None] == grp[None, :]
        if reverse:
            beta = (grp * 2 * m + m)[:, None]
            n = np.where(in_first[:, None], (u >= row) & (u < beta), (u >= beta) & (u < row))
            mask = same & in_first[:, None] & ~in_first[None, :]
        else:
            beta = (grp * 2 * m + m - 1)[:, None]
            n = np.where(in_first[:, None], (u > row) & (u <= beta), (u > beta) & (u <= row))
            mask = same & ~in_first[:, None] & in_first[None, :]
        nmats.append(n)
        masks.append(mask)
        m //= 2
    masks.append(np.eye(t, dtype=bool))
    nmat = jnp.asarray(np.concatenate(nmats, axis=0), F32).astype(BF16)
    return nmat, jnp.asarray(np.stack(masks), F32)


def _hgrn_block(q, f, v, lb, st, nmat_ref, mask_ref, reverse):
    t = HGRN_BLOCK
    nlev = mask_ref.shape[0] - 1
    qs = q * (B_DK ** -0.5)
    fa = lb + (1.0 - lb) * _sigmoid(f)
    kk = 1.0 - fa
    g_hi, g_lo = _split2(jnp.log(fa))
    ex = jnp.dot(nmat_ref[...], jnp.concatenate([g_hi, g_lo], axis=1), preferred_element_type=F32)
    ex = ex[:, :B_DK] + ex[:, B_DK:]
    b = ex[:t]
    btot = b[0:1] if reverse else b[t - 1:t]
    a = mask_ref[nlev] * _dot_nt(qs, kk)
    for l in range(nlev):
        e = jnp.exp(ex[(l + 1) * t:(l + 2) * t])
        a = a + mask_ref[l] * _dot_nt(qs * e, kk * e)
    out = _dot(a, v) + _dot_nt(qs * jnp.exp(b), st)
    khat = (kk * jnp.exp(btot - b)).astype(BF16)
    st_new = st * jnp.exp(btot) + jnp.dot(v.T.astype(BF16), khat, preferred_element_type=F32)
    return out, st_new


def _hgrn_lb(lg_ref, layer):
    lg = [lg_ref[l, 0] for l in range(lg_ref.shape[0])]
    mx = functools.reduce(jnp.maximum, lg)
    e = [jnp.exp(x - mx) for x in lg]
    return functools.reduce(jnp.add, e[:layer + 1]) / functools.reduce(jnp.add, e)


def _hgrn_kernel(qf_ref, ff_ref, vf_ref, qb_ref, fb_ref, vb_ref, lgf_ref, lgb_ref,
                 nf_ref, mf_ref, nb_ref, mb_ref, of_ref, ob_ref, sf_sc, sb_sc, *, layer):
    @pl.when(pl.program_id(2) == 0)
    def _():
        sf_sc[...] = jnp.zeros_like(sf_sc)
        sb_sc[...] = jnp.zeros_like(sb_sc)

    o, st = _hgrn_block(qf_ref[0], ff_ref[0], vf_ref[0], _hgrn_lb(lgf_ref, layer), sf_sc[...],
                        nf_ref, mf_ref, False)
    of_ref[0] = o
    sf_sc[...] = st
    o, st = _hgrn_block(qb_ref[0], fb_ref[0], vb_ref[0], _hgrn_lb(lgb_ref, layer), sb_sc[...],
                        nb_ref, mb_ref, True)
    ob_ref[0] = o
    sb_sc[...] = st


def _hgrn(z, lb_logits, layer, col0):
    bsz, seq, _ = z.shape
    t = HGRN_BLOCK
    nb = seq // t
    c0 = col0 // LANES
    hw = B_HEADS

    def zspec(group, rev):
        return pl.BlockSpec(
            (1, t, LANES),
            lambda b, h, j: (b, (nb - 1 - j) if rev else j, c0 + group * hw + h))

    def lgspec(direction):
        return pl.BlockSpec((lb_logits.shape[0], 1, 1, LANES),
                            lambda b, h, j: (0, direction * hw + h, 0, 0))

    nf, mf = _hgrn_consts(False)
    nbw, mbw = _hgrn_consts(True)
    lg = lb_logits.astype(F32).reshape(lb_logits.shape[0], 2 * hw, 1, LANES)
    o_shape = jax.ShapeDtypeStruct((bsz, seq, hw * B_DV), F32)
    return pl.pallas_call(
        functools.partial(_hgrn_kernel, layer=layer),
        out_shape=(o_shape, o_shape),
        grid=(bsz, hw, nb),
        in_specs=[zspec(0, False), zspec(1, False), zspec(3, False),
                  zspec(0, True), zspec(2, True), zspec(3, True),
                  lgspec(0), lgspec(1),
                  _const_spec(nf.shape), _const_spec(mf.shape),
                  _const_spec(nbw.shape), _const_spec(mbw.shape)],
        out_specs=(pl.BlockSpec((1, t, LANES), lambda b, h, j: (b, j, h)),
                   pl.BlockSpec((1, t, LANES), lambda b, h, j: (b, nb - 1 - j, h))),
        scratch_shapes=[pltpu.VMEM((B_DV, B_DK), F32), pltpu.VMEM((B_DV, B_DK), F32)],
        compiler_params=_cparams("parallel", "parallel", "arbitrary"),
        name="hgrn",
    )(z, z, z, z, z, z, lg, lg, nf, mf, nbw, mbw)


def _residual_epilogue(y, x_ref, gain_ref, gate_ref, o_ref):
    o_ref[0] = x_ref[0] + gate_ref[0] * _rms(y, gain_ref[...])


def _out0_kernel(a_ref, of_ref, ob_ref, g_ref, on_ref, w_ref, x_ref, gain_ref, gate_ref, o_ref):
    o = of_ref[0] + ob_ref[0]
    g = g_ref[0]
    parts = [_rms(o[:, h * B_DV:(h + 1) * B_DV], on_ref[...]) for h in range(B_HEADS)]
    bn = jnp.concatenate(parts, axis=-1) * (g * _sigmoid(g))
    na = a_ref.shape[-1]
    y = _dot(a_ref[0], w_ref[:na, :]) + _dot(bn, w_ref[na:, :])
    _residual_epilogue(y, x_ref, gain_ref, gate_ref, o_ref)


def _out1_kernel(c_ref, d_ref, w_ref, x_ref, gain_ref, gate_ref, o_ref):
    nc = c_ref.shape[-1]
    y = _dot(c_ref[0], w_ref[:nc, :]) + _dot(d_ref[0], w_ref[nc:, :])
    _residual_epilogue(y, x_ref, gain_ref, gate_ref, o_ref)


def _row_spec(tm, width, col=0):
    return pl.BlockSpec((1, tm, width), lambda b, i: (b, i, col))


def _out0(a_out, o_f, o_b, z, g_col, out_norm, w_bf16, x, gain, gate, tm):
    bsz, seq, d = x.shape
    wv = B_HEADS * B_DV
    vec = pl.BlockSpec((1, 1, d), lambda b, i: (b, 0, 0))
    return pl.pallas_call(
        _out0_kernel,
        out_shape=jax.ShapeDtypeStruct(x.shape, F32),
        grid=(bsz, seq // tm),
        in_specs=[_row_spec(tm, a_out.shape[-1]), _row_spec(tm, wv), _row_spec(tm, wv),
                  _row_spec(tm, wv, g_col // wv), _const_spec((1, B_DV)),
                  _const_spec(w_bf16.shape), _row_spec(tm, d), _const_spec((1, d)), vec],
        out_specs=_row_spec(tm, d),
        compiler_params=_cparams("parallel", "parallel"),
        name="out0",
    )(a_out, o_f, o_b, z, out_norm.reshape(1, B_DV), w_bf16, x, gain.reshape(1, d),
      gate.reshape(bsz, 1, d))


def _out1(c_out, d_out, w_bf16, x, gain, gate, tm):
    bsz, seq, d = x.shape
    vec = pl.BlockSpec((1, 1, d), lambda b, i: (b, 0, 0))
    return pl.pallas_call(
        _out1_kernel,
        out_shape=jax.ShapeDtypeStruct(x.shape, F32),
        grid=(bsz, seq // tm),
        in_specs=[_row_spec(tm, c_out.shape[-1]), _row_spec(tm, d_out.shape[-1]),
                  _const_spec(w_bf16.shape), _row_spec(tm, d), _const_spec((1, d)), vec],
        out_specs=_row_spec(tm, d),
        compiler_params=_cparams("parallel", "parallel"),
        name="out1",
    )(c_out, d_out, w_bf16, x, gain.reshape(1, d), gate.reshape(bsz, 1, d))


def _ffn_kernel(x_ref, g1_ref, sc_ref, sh_ref, wi_ref, wo_ref, g2_ref, gate_ref, o_ref, *, nchunk):
    x = x_ref[0]
    h = (_rms(x, g1_ref[...]) * (1.0 + sc_ref[0]) + sh_ref[0]).astype(BF16)
    hidden = wo_ref.shape[0]
    ck = hidden // nchunk
    y = None
    for c in range(nchunk):
        gt = jnp.dot(h, wi_ref[:, c * ck:(c + 1) * ck], preferred_element_type=F32)
        up = jnp.dot(h, wi_ref[:, hidden + c * ck:hidden + (c + 1) * ck], preferred_element_type=F32)
        part = _dot(gt * _sigmoid(gt) * up, wo_ref[c * ck:(c + 1) * ck, :])
        y = part if y is None else y + part
    o_ref[0] = x + gate_ref[0] * _rms(y, g2_ref[...])


def _ffn(x, g1, sc, sh, wi_bf16, wo_bf16, g2, gate, tm):
    bsz, seq, d = x.shape
    vec = pl.BlockSpec((1, 1, d), lambda b, i: (b, 0, 0))
    hidden = wo_bf16.shape[0]
    nchunk = 2 if (hidden // 2) % LANES == 0 else 1
    return pl.pallas_call(
        functools.partial(_ffn_kernel, nchunk=nchunk),
        out_shape=jax.ShapeDtypeStruct(x.shape, F32),
        grid=(bsz, seq // tm),
        in_specs=[_row_spec(tm, d), _const_spec((1, d)), vec, vec,
                  _const_spec(wi_bf16.shape), _const_spec(wo_bf16.shape),
                  _const_spec((1, d)), vec],
        out_specs=_row_spec(tm, d),
        compiler_params=_cparams("parallel", "parallel"),
        name="ffn",
    )(x, g1.reshape(1, d), sc.reshape(bsz, 1, d), sh.reshape(bsz, 1, d), wi_bf16, wo_bf16,
      g2.reshape(1, d), gate.reshape(bsz, 1, d))


def _fft1_kernel(th_ref, tl_ref, u_ref, p_ref):
    p_ref[0] = _dot_tab(th_ref[...], tl_ref[...], u_ref[0], tab_left=True)


def _fft2_kernel(pr_ref, pi_ref, tc_ref, ts_ref, fh_ref, fl_ref, wh_ref, wl_ref, o_ref, *, scale):
    tc = tc_ref[0]
    ts = ts_ref[0]
    qr, qi = [], []
    for g in range(C_GROUPS):
        cols = slice(g * C_WIDTH, (g + 1) * C_WIDTH)
        pr = pr_ref[0, 0, 0, :, cols]
        pim = pi_ref[0, 0, 0, :, cols]
        qr.append(pr * tc + pim * ts)
        qi.append(pim * tc - pr * ts)
    q = jnp.concatenate([jnp.concatenate(qr, axis=1), jnp.concatenate(qi, axis=1)], axis=0)
    xx = _dot_tab(fh_ref[...], fl_ref[...], q, tab_left=True)
    n2 = xx.shape[0] // 2
    for g in range(C_GROUPS):
        cols = slice(g * C_WIDTH, (g + 1) * C_WIDTH)
        xg = jnp.concatenate([xx[:n2, cols], xx[n2:, cols]], axis=1)
        o_ref[0, :, cols] = _dot_tab(wh_ref[...], wl_ref[...], xg, tab_left=False) * scale


def _fourier_mixer(u):
    bsz, seq, cw = u.shape
    n2 = FFT_N2
    n1 = seq // n2
    assert n1 * n2 == seq and cw == C_GROUPS * C_WIDTH
    a1 = 2.0 * np.pi * np.outer(np.arange(n1), np.arange(n1)) / n1
    f1 = np.concatenate([np.cos(a1), -np.sin(a1)], axis=0)
    a2 = 2.0 * np.pi * np.outer(np.arange(n2), np.arange(n2)) / n2
    c2, s2 = np.cos(a2), np.sin(a2)
    f2 = np.block([[c2, s2], [-s2, c2]])
    aw = 2.0 * np.pi * np.outer(np.arange(C_WIDTH), np.arange(C_WIDTH)) / C_WIDTH
    fw = np.concatenate([np.cos(aw), np.sin(aw)], axis=0)
    at = 2.0 * np.pi * np.outer(np.arange(n1), np.arange(n2)) / seq
    tw_c = jnp.asarray(np.repeat(np.cos(at)[:, :, None], C_WIDTH, axis=2), F32)
    tw_s = jnp.asarray(np.repeat(np.sin(at)[:, :, None], C_WIDTH, axis=2), F32)
    f1h, f1l = _np_split2(f1)
    f2h, f2l = _np_split2(f2)
    fwh, fwl = _np_split2(fw)

    ncol = n2 * cw
    tn = 8192
    p = pl.pallas_call(
        _fft1_kernel,
        out_shape=jax.ShapeDtypeStruct((bsz, 2 * n1, ncol), F32),
        grid=(bsz, ncol // tn),
        in_specs=[_const_spec(f1h.shape), _const_spec(f1l.shape),
                  pl.BlockSpec((1, n1, tn), lambda b, j: (b, 0, j))],
        out_specs=pl.BlockSpec((1, 2 * n1, tn), lambda b, j: (b, 0, j)),
        compiler_params=_cparams("parallel", "parallel"),
        name="fft1",
    )(f1h, f1l, u.reshape(bsz, n1, ncol))
    p5 = p.reshape(bsz, 2, n1, n2, cw)
    tw_spec = pl.BlockSpec((1, n2, C_WIDTH), lambda b, k: (k, 0, 0))
    y = pl.pallas_call(
        functools.partial(_fft2_kernel, scale=float(1.0 / np.sqrt(seq * C_WIDTH))),
        out_shape=jax.ShapeDtypeStruct((bsz, n2, n1 * cw), F32),
        grid=(bsz, n1),
        in_specs=[pl.BlockSpec((1, 1, 1, n2, cw), lambda b, k: (b, 0, k, 0, 0)),
                  pl.BlockSpec((1, 1, 1, n2, cw), lambda b, k: (b, 1, k, 0, 0)),
                  tw_spec, tw_spec,
                  _const_spec(f2h.shape), _const_spec(f2l.shape),
                  _const_spec(fwh.shape), _const_spec(fwl.shape)],
        out_specs=pl.BlockSpec((1, n2, cw), lambda b, k: (b, 0, k)),
        compiler_params=_cparams("parallel", "parallel"),
        name="fft2",
    )(p5, p5, tw_c, tw_s, f2h, f2l, fwh, fwl)
    return y.reshape(bsz, seq, cw)


def _head_perm():
    rep = D_Q_HEADS // D_KV_HEADS
    cols = []
    for j in range(rep):
        for g in range(D_KV_HEADS):
            h = g * rep + j
            cols.extend(range(h * D_HEAD_DIM, (h + 1) * D_HEAD_DIM))
    return np.asarray(cols, np.int32)


def _rope_tables(seq):
    rows = seq // GRID_W
    row = jnp.repeat(jnp.arange(rows, dtype=F32), GRID_W)
    col = jnp.tile(jnp.arange(GRID_W, dtype=F32), rows)
    axis_dim = D_HEAD_DIM // 2
    inv_freq = jnp.power(ROPE_THETA, -jnp.arange(0, axis_dim, 2, dtype=F32) / axis_dim)
    ang_r = row[:, None] * inv_freq[None, :]
    ang_c = col[:, None] * inv_freq[None, :]
    cr, sr, cc, sc = jnp.cos(ang_r), jnp.sin(ang_r), jnp.cos(ang_c), jnp.sin(ang_c)
    cos = jnp.concatenate([cr, cr, cc, cc], axis=1)
    sin = jnp.concatenate([-sr, sr, -sc, sc], axis=1)
    reps = LANES // D_HEAD_DIM
    return jnp.tile(cos, (1, reps)), jnp.tile(sin, (1, reps))


def _qkprep_kernel(q_ref, k_ref, v_ref, cos_ref, sin_ref, bd_h_ref, bd_l_ref, gq_ref, gk_ref,
                   qo_ref, ko_ref, vo_ref):
    cos = cos_ref[...]
    sin = sin_ref[...]
    quarter = D_HEAD_DIM // 4
    lane = lax.broadcasted_iota(jnp.int32, (1, LANES), 1)
    first_of_pair = (lane // quarter) % 2 == 0

    def norm_rope(x, gain, scale):
        ms = _dot_tab(bd_h_ref[...], bd_l_ref[...], x * x, tab_left=False)
        xn = x * lax.rsqrt(ms + EPS) * gain
        partner = jnp.where(first_of_pair, pltpu.roll(xn, LANES - quarter, 1),
                            pltpu.roll(xn, quarter, 1))
        return ((xn * cos + partner * sin) * scale).astype(BF16)

    for j in range(q_ref.shape[-1] // LANES):
        cols = slice(j * LANES, (j + 1) * LANES)
        qo_ref[0, :, cols] = norm_rope(q_ref[0, :, cols], gq_ref[...], D_HEAD_DIM ** -0.5 * LOG2E)
    ko_ref[0] = norm_rope(k_ref[0], gk_ref[...], 1.0)
    vo_ref[0] = v_ref[0].T.astype(BF16)


def _qkprep(qkv, qk_norm_j, tm):
    bsz, seq, _ = qkv.shape
    qw = D_Q_HEADS * D_HEAD_DIM
    kw = D_KV_HEADS * D_HEAD_DIM
    assert kw == LANES
    cos, sin = _rope_tables(seq)
    bd = np.kron(np.eye(LANES // D_HEAD_DIM), np.full((D_HEAD_DIM, D_HEAD_DIM), 1.0 / D_HEAD_DIM))
    bd_h, bd_l = _np_split2(bd)
    reps = LANES // D_HEAD_DIM
    gq = jnp.tile(qk_norm_j[0].astype(F32), reps).reshape(1, LANES)
    gk = jnp.tile(qk_norm_j[1].astype(F32), reps).reshape(1, LANES)
    tab = pl.BlockSpec((tm, LANES), lambda b, i: (i, 0))
    return pl.pallas_call(
        _qkprep_kernel,
        out_shape=(jax.ShapeDtypeStruct((bsz, seq, qw), BF16),
                   jax.ShapeDtypeStruct((bsz, seq, kw), BF16),
                   jax.ShapeDtypeStruct((bsz, kw, seq), BF16)),
        grid=(bsz, seq // tm),
        in_specs=[_row_spec(tm, qw, 0), _row_spec(tm, kw, qw // kw), _row_spec(tm, kw, qw // kw + 1),
                  tab, tab, _const_spec(bd_h.shape), _const_spec(bd_l.shape),
                  _const_spec((1, LANES)), _const_spec((1, LANES))],
        out_specs=(_row_spec(tm, qw), _row_spec(tm, kw),
                   pl.BlockSpec((1, kw, tm), lambda b, i: (b, 0, i))),
        compiler_params=_cparams("parallel", "parallel"),
        name="qkprep",
    )(qkv, qkv, qkv, cos, sin, bd_h, bd_l, gq, gk)


def _flash_kernel(q_ref, k_ref, vt_ref, o_ref, m_sc, l_sc, acc_sc, s_sc):
    kv = pl.program_id(2)

    @pl.when(kv == 0)
    def _():
        m_sc[...] = jnp.full_like(m_sc, -jnp.inf)
        l_sc[...] = jnp.zeros_like(l_sc)
        acc_sc[...] = jnp.zeros_like(acc_sc)

    lane = lax.broadcasted_iota(jnp.int32, (1, LANES), 1)
    lo = lane < D_HEAD_DIM
    nblk = q_ref.shape[-1] // LANES
    tq, tk = q_ref.shape[1], k_ref.shape[1]
    ku, qu = FLASH_KEY_UNIT, FLASH_QUERY_UNIT
    k = k_ref[0]
    nheads = nblk * D_KV_HEADS

    def logits_pass(idx):
        j, g = divmod(idx, D_KV_HEADS)
        qj = q_ref[0, :, j * LANES:(j + 1) * LANES]
        sel = lo if g == 0 else jnp.logical_not(lo)
        qm = jnp.where(sel, qj, jnp.zeros_like(qj))
        mcs = []
        for c in range(tq // qu):
            s = lax.dot_general(k, qm[c * qu:(c + 1) * qu], (((1,), (1,)), ((), ())),
                                preferred_element_type=F32)
            s_sc[idx, :, c * qu:(c + 1) * qu] = s
            mcs.append(jnp.max(s, axis=0, keepdims=True))
        m_prev = m_sc[idx, 0:1, :]
        m_new = jnp.maximum(m_prev, jnp.concatenate(mcs, axis=1))
        m_sc[idx, 0:1, :] = m_new
        return m_new, jnp.exp2(m_prev - m_new)

    def value_pass(idx, m_new, alpha):
        j, g = divmod(idx, D_KV_HEADS)
        rows = slice(g * D_HEAD_DIM, (g + 1) * D_HEAD_DIM)
        lsums = []
        for c in range(tq // qu):
            qcols = slice(c * qu, (c + 1) * qu)
            pv = None
            lsum = None
            for u in range(tk // ku):
                keys = slice(u * ku, (u + 1) * ku)
                p = jnp.exp2(s_sc[idx, keys, qcols] - m_new[:, qcols])
                ps = jnp.sum(p, axis=0, keepdims=True)
                d = jnp.dot(vt_ref[0, rows, keys], p.astype(BF16), preferred_element_type=F32)
                pv = d if pv is None else pv + d
                lsum = ps if lsum is None else lsum + ps
            acc_sc[j, rows, qcols] = alpha[:, qcols] * acc_sc[j, rows, qcols] + pv
            lsums.append(lsum)
        l_sc[idx, 0:1, :] = alpha * l_sc[idx, 0:1, :] + jnp.concatenate(lsums, axis=1)

    stats = logits_pass(0)
    for idx in range(nheads):
        nxt = logits_pass(idx + 1) if idx + 1 < nheads else None
        value_pass(idx, *stats)
        stats = nxt

    @pl.when(kv == pl.num_programs(2) - 1)
    def _():
        for j in range(nblk):
            l = jnp.concatenate(
                [jnp.broadcast_to(l_sc[j * D_KV_HEADS + g, 0:1, :], (D_HEAD_DIM, tq))
                 for g in range(D_KV_HEADS)], axis=0)
            o_ref[0, :, j * LANES:(j + 1) * LANES] = (acc_sc[j] / l).T.astype(o_ref.dtype)


def _flash(q, k, vt, tq, tk):
    bsz, seq, qw = q.shape
    kw = k.shape[-1]
    return pl.pallas_call(
        _flash_kernel,
        out_shape=jax.ShapeDtypeStruct((bsz, seq, qw), F32),
        grid=(bsz, seq // tq, seq // tk),
        in_specs=[pl.BlockSpec((1, tq, qw), lambda b, i, j: (b, i, 0)),
                  pl.BlockSpec((1, tk, kw), lambda b, i, j: (b, j, 0)),
                  pl.BlockSpec((1, kw, tk), lambda b, i, j: (b, 0, j))],
        out_specs=pl.BlockSpec((1, tq, qw), lambda b, i, j: (b, i, 0)),
        scratch_shapes=[pltpu.VMEM((D_Q_HEADS, 8, tq), F32),
                        pltpu.VMEM((D_Q_HEADS, 8, tq), F32),
                        pltpu.VMEM((qw // LANES, LANES, tq), F32),
                        pltpu.VMEM((D_Q_HEADS, tk, tq), F32)],
        compiler_params=_cparams("parallel", "parallel", "arbitrary"),
        name="flash",
    )(q, k, vt)


def kernel(x, c, t5_bias, hgrn_lb_logits, ada_w, ada_b, norm_gains, ab_w_in, ab_w_out,
           hgrn_out_norm, cd_w_in, cd_w_out, qk_norm, ffn_w_in, ffn_w_out):
    bsz, seq, d = x.shape
    depth = ada_w.shape[0]
    mod = _ada_mod(c.astype(F32), ada_w, ada_b)
    perm = _head_perm()
    aw = A_HEADS * A_HEAD_DIM
    cw = C_GROUPS * C_WIDTH
    qw = D_Q_HEADS * D_HEAD_DIM
    tm_in = min(256, seq)
    tm = min(512, seq)
    for layer in range(depth):
        sh_m, sc_m, g_m, sh_f, sc_f, g_f = [mod[layer, :, i * d:(i + 1) * d] for i in range(6)]
        gains = norm_gains[layer]
        j = layer // 2
        if layer % 2 == 0:
            w_in = ab_w_in[j].astype(BF16)
            (z,) = _inproj(x, gains[0], sc_m, sh_m, w_in, (w_in.shape[1],), tm_in)
            a_out = _dilated_attention(z, t5_bias)
            o_f, o_b = _hgrn(z, hgrn_lb_logits, layer, 3 * aw)
            g_col = 3 * aw + 3 * B_HEADS * B_DK + B_HEADS * B_DV
            x = _out0(a_out, o_f, o_b, z, g_col, hgrn_out_norm[j], ab_w_out[j].astype(BF16),
                      x, gains[1], g_m, tm)
        else:
            w_full = cd_w_in[j]
            w_in = jnp.concatenate([w_full[:, :cw], w_full[:, cw:cw + qw][:, perm],
                                    w_full[:, cw + qw:]], axis=1).astype(BF16)
            u, qkv = _inproj(x, gains[0], sc_m, sh_m, w_in, (cw, w_in.shape[1] - cw), tm)
            c_out = _fourier_mixer(u)
            qn, kn, vn = _qkprep(qkv, qk_norm[j], tm)
            d_out = _flash(qn, kn, vn, min(FLASH_TQ, seq), min(FLASH_TK, seq))
            w_out_full = cd_w_out[j]
            w_out = jnp.concatenate([w_out_full[:cw], w_out_full[cw:][perm]], axis=0).astype(BF16)
            x = _out1(c_out, d_out, w_out, x, gains[1], g_m, tm)
        x = _ffn(x, gains[2], sc_f, sh_f, ffn_w_in[layer].astype(BF16), ffn_w_out[layer].astype(BF16),
                 gains[3], g_f, tm)
    return x
```

```python
import functools

import numpy as np
import jax
import jax.numpy as jnp
from jax import lax
from jax.experimental import pallas as pl
from jax.experimental.pallas import tpu as pltpu

F32 = jnp.float32
BF16 = jnp.bfloat16
LANES = 128
VMEM_LIMIT_BYTES = 56 * 2**20
NEG_INF = -1e30
EPS = 1e-6

GRID_W = 64
A_HEADS = 8
A_HEAD_DIM = 64
DIL_CFG = ((128, 1), (512, 4), (2048, 16))
N_BUCKETS = 32
T5_MAX_DIST = 1024
B_HEADS = 4
B_DK = 128
B_DV = 128
C_GROUPS = 4
C_WIDTH = 128
D_Q_HEADS = 8
D_KV_HEADS = 2
D_HEAD_DIM = 64
ROPE_THETA = 10000.0

DIL_TQ = 128
DIL_TILE = 256
HGRN_BLOCK = 256
FLASH_TQ = 1024
FLASH_TK = 512
FLASH_KEY_UNIT = 128
FLASH_QUERY_UNIT = 256
FFT_N2 = 128
LOG2E = 1.4426950408889634


def _cparams(*sem):
    return pltpu.CompilerParams(dimension_semantics=sem, vmem_limit_bytes=VMEM_LIMIT_BYTES)


def _const_spec(shape):
    nd = len(shape)
    return pl.BlockSpec(shape, lambda *_: (0,) * nd, pipeline_mode=pl.Buffered(1))


def _sigmoid(x):
    return 1.0 / (1.0 + jnp.exp(-x))


def _dot(a, b):
    return jnp.dot(a.astype(BF16), b.astype(BF16), preferred_element_type=F32)


def _dot_nt(a, b):
    return lax.dot_general(a.astype(BF16), b.astype(BF16), (((1,), (1,)), ((), ())),
                           preferred_element_type=F32)


def _split2(a):
    hi = a.astype(BF16)
    lo = (a - hi.astype(F32)).astype(BF16)
    return hi, lo


def _split3(a):
    a1 = a.astype(BF16)
    r = a - a1.astype(F32)
    a2 = r.astype(BF16)
    a3 = (r - a2.astype(F32)).astype(BF16)
    return a1, a2, a3


def _dot_tab(tab_hi, tab_lo, x, *, tab_left):
    x_hi, x_lo = _split2(x)
    if tab_left:
        d = lambda t, v: jnp.dot(t, v, preferred_element_type=F32)
    else:
        d = lambda t, v: jnp.dot(v, t, preferred_element_type=F32)
    return d(tab_hi, x_hi) + (d(tab_hi, x_lo) + d(tab_lo, x_hi))


def _rms(x, gain):
    ms = jnp.mean(x * x, axis=-1, keepdims=True)
    return x * lax.rsqrt(ms + EPS) * gain


def _np_split2(t):
    t = np.asarray(t, np.float32)
    hi = jnp.asarray(t, F32).astype(BF16)
    lo = (jnp.asarray(t, F32) - hi.astype(F32)).astype(BF16)
    return hi, lo


def _mod_kernel(c_ref, w_ref, b_ref, o_ref):
    c = c_ref[...]
    o_ref[0] = _dot(c * _sigmoid(c), w_ref[0]) + b_ref[0]


def _ada_mod(c, ada_w, ada_b):
    depth, d, n6 = ada_w.shape
    bsz = c.shape[0]
    rows = 8
    cp = jnp.zeros((rows, d), F32).at[:bsz].set(c)
    tn = n6 // 4
    out = pl.pallas_call(
        _mod_kernel,
        out_shape=jax.ShapeDtypeStruct((depth, rows, n6), F32),
        grid=(depth, n6 // tn),
        in_specs=[pl.BlockSpec((rows, d), lambda l, j: (0, 0)),
                  pl.BlockSpec((1, d, tn), lambda l, j: (l, 0, j)),
                  pl.BlockSpec((1, 1, tn), lambda l, j: (l, 0, j))],
        out_specs=pl.BlockSpec((1, rows, tn), lambda l, j: (l, 0, j)),
        compiler_params=_cparams("parallel", "parallel"),
        name="ada_mod",
    )(cp, ada_w, ada_b.reshape(depth, 1, n6))
    return out[:, :bsz]


def _inproj_kernel(x_ref, gain_ref, sc_ref, sh_ref, w_ref, *o_refs):
    h = _rms(x_ref[0], gain_ref[...]) * (1.0 + sc_ref[0]) + sh_ref[0]
    z = _dot(h, w_ref[...])
    off = 0
    for o_ref in o_refs:
        n = o_ref.shape[-1]
        o_ref[0] = z[:, off:off + n].astype(o_ref.dtype)
        off += n


def _inproj(x, gain, sc, sh, w_bf16, splits, tm):
    bsz, seq, d = x.shape
    n = w_bf16.shape[1]
    assert sum(splits) == n
    vec = pl.BlockSpec((1, 1, d), lambda b, i: (b, 0, 0))
    return pl.pallas_call(
        _inproj_kernel,
        out_shape=tuple(jax.ShapeDtypeStruct((bsz, seq, s), F32) for s in splits),
        grid=(bsz, seq // tm),
        in_specs=[pl.BlockSpec((1, tm, d), lambda b, i: (b, i, 0)),
                  _const_spec((1, d)), vec, vec, _const_spec((d, n))],
        out_specs=tuple(pl.BlockSpec((1, tm, s), lambda b, i: (b, i, 0)) for s in splits),
        compiler_params=_cparams("parallel", "parallel"),
        name="inproj",
    )(x, gain.reshape(1, d), sc.reshape(bsz, 1, d), sh.reshape(bsz, 1, d), w_bf16)


def _inproj_cm_kernel(x_ref, gain_ref, sc_ref, sh_ref, w_ref, *refs):
    cm_refs, rest_ref, zs_sc = refs[:-2], refs[-2], refs[-1]
    h = _rms(x_ref[0], gain_ref[...]) * (1.0 + sc_ref[0]) + sh_ref[0]
    z = _dot(h, w_ref[...])
    nblk, tm, _ = zs_sc.shape
    rest_ref[0] = z[:, nblk * LANES:]
    for c in range(nblk):
        zs_sc[c] = z[:, c * LANES:(c + 1) * LANES]
    for cm_ref, (_, dil) in zip(cm_refs, DIL_CFG):
        for r in range(dil):
            for c in range(nblk):
                cm_ref[0, r, :, c * LANES:(c + 1) * LANES] = (
                    zs_sc[c, pl.ds(r, tm // dil, stride=dil), :].astype(BF16))


def _inproj_cm(x, gain, sc, sh, w_bf16, na, tm):
    bsz, seq, d = x.shape
    n = w_bf16.shape[1]
    vec = pl.BlockSpec((1, 1, d), lambda b, i: (b, 0, 0))
    dils = [dl for _, dl in DIL_CFG]
    return pl.pallas_call(
        _inproj_cm_kernel,
        out_shape=tuple([jax.ShapeDtypeStruct((bsz, dl, seq // dl, na), BF16) for dl in dils]
                        + [jax.ShapeDtypeStruct((bsz, seq, n - na), F32)]),
        grid=(bsz, seq // tm),
        in_specs=[pl.BlockSpec((1, tm, d), lambda b, i: (b, i, 0)),
                  _const_spec((1, d)), vec, vec, _const_spec((d, n))],
        out_specs=tuple([pl.BlockSpec((1, dl, tm // dl, na), lambda b, i: (b, 0, i, 0)) for dl in dils]
                        + [pl.BlockSpec((1, tm, n - na), lambda b, i: (b, i, 0))]),
        scratch_shapes=[pltpu.VMEM((na // LANES, tm, LANES), F32)],
        compiler_params=_cparams("parallel", "parallel"),
        name="inproj_cm",
    )(x, gain.reshape(1, d), sc.reshape(bsz, 1, d), sh.reshape(bsz, 1, d), w_bf16)


def _t5_buckets(rel):
    half = N_BUCKETS // 2
    max_exact = half // 2
    n = np.abs(rel)
    large = max_exact + (np.log(np.maximum(n, 1) / max_exact) / np.log(T5_MAX_DIST / max_exact)
                         * (half - max_exact)).astype(np.int32)
    large = np.minimum(large, half - 1)
    return (np.where(rel > 0, half, 0) + np.where(n < max_exact, n, large)).astype(np.int32)


def _dil_bias(t5_bias, window, dil, tq):
    half = (window // 2) // dil
    assert half == tq // 2
    rel = np.arange(2 * tq)[None, :] - half - np.arange(tq)[:, None]
    inside = np.abs(rel) <= half
    buckets = _t5_buckets(np.where(inside, rel, 0) * dil)
    onehot =jnp.asarray(np.eye(N_BUCKETS, dtype=np.float32)[buckets])
    bias = jnp.einsum("qkn,nh->hqk", onehot, t5_bias.astype(F32), precision=lax.Precision.HIGHEST)
    return jnp.where(jnp.asarray(inside)[None], bias, NEG_INF)


def _dil_kernel(q_ref, kp_ref, kc_ref, kn_ref, vp_ref, vc_ref, vn_ref, bias_ref, o_ref, lse_ref,
                *, class_len):
    i = pl.program_id(2)
    sub, hq, tile = DIL_TQ, DIL_TQ // 2, DIL_TILE
    kwin = jnp.concatenate([kp_ref[0, 0], kc_ref[0, 0], kn_ref[0, 0]], axis=0)
    vwin = jnp.concatenate([vp_ref[0, 0], vc_ref[0, 0], vn_ref[0, 0]], axis=0)
    lane = lax.broadcasted_iota(jnp.int32, (1, LANES), 1)
    lo = lane < A_HEAD_DIM
    scale = jnp.asarray(A_HEAD_DIM ** -0.5, BF16)
    for jt in range(tile // sub):
        qrows = slice(jt * sub, (jt + 1) * sub)
        krows = slice(jt * sub, jt * sub + 2 * sub)
        kpos = i * tile + jt * sub - hq + lax.broadcasted_iota(jnp.int32, (1, 2 * sub), 1)
        valid = jnp.logical_and(kpos >= 0, kpos < class_len)
        lse_all = jnp.zeros((sub, LANES), F32)
        for j in range(A_HEADS // 2):
            cols = slice(j * LANES, (j + 1) * LANES)
            qj = q_ref[0, 0, qrows, cols] * scale
            kj = kwin[krows, cols]
            vj = vwin[krows, cols]
            halves = []
            for half in range(2):
                h = 2 * j + half
                sel = lo if half == 0 else jnp.logical_not(lo)
                s = lax.dot_general(jnp.where(sel, qj, jnp.zeros_like(qj)), kj,
                                    (((1,), (1,)), ((), ())), preferred_element_type=F32)
                s = jnp.where(valid, s + bias_ref[h], NEG_INF)
                m = jnp.max(s, axis=-1, keepdims=True)
                p = jnp.exp(s - m)
                l = jnp.sum(p, axis=-1, keepdims=True)
                halves.append(jnp.dot(p.astype(BF16), vj, preferred_element_type=F32) / l)
                lse_all = jnp.where(lane == h, m + jnp.log(l), lse_all)
            o_ref[0, 0, qrows, cols] = jnp.where(lo, halves[0], halves[1])
        lse_ref[0, 0, qrows, :] = lse_all


def _dilated_branch(qkv_cm, t5_bias, window, dil):
    bsz, _, cl, width = qkv_cm.shape
    aw = A_HEADS * A_HEAD_DIM
    tile, hq = DIL_TILE, DIL_TQ // 2
    nt = cl // tile
    per = tile // hq
    nh = cl // hq

    def cur(col):
        return pl.BlockSpec((1, 1, tile, aw), lambda b, r, i: (b, r, i, col))

    def prev(col):
        return pl.BlockSpec((1, 1, hq, aw), lambda b, r, i: (b, r, jnp.maximum(i * per - 1, 0), col))

    def nxt(col):
        return pl.BlockSpec((1, 1, hq, aw),
                            lambda b, r, i: (b, r, jnp.minimum((i + 1) * per, nh - 1), col))

    return pl.pallas_call(
        functools.partial(_dil_kernel, class_len=cl),
        out_shape=(jax.ShapeDtypeStruct((bsz, dil, cl, aw), F32),
                   jax.ShapeDtypeStruct((bsz, dil, cl, LANES), F32)),
        grid=(bsz, dil, nt),
        in_specs=[cur(0), prev(1), cur(1), nxt(1), prev(2), cur(2), nxt(2),
                  _const_spec((A_HEADS, DIL_TQ, 2 * DIL_TQ))],
        out_specs=(pl.BlockSpec((1, 1, tile, aw), lambda b, r, i: (b, r, i, 0)),
                   pl.BlockSpec((1, 1, tile, LANES), lambda b, r, i: (b, r, i, 0))),
        compiler_params=_cparams("parallel", "parallel", "parallel"),
        name=f"dilated_d{dil}",
    )(*([qkv_cm] * 7), _dil_bias(t5_bias, window, dil, DIL_TQ))


def _dilmerge_kernel(*refs):
    nbr = len(DIL_CFG)
    a_refs, l_refs = refs[:nbr], refs[nbr:2 * nbr]
    e_ref, o_ref = refs[2 * nbr], refs[2 * nbr + 1]
    a_scs, l_scs = refs[2 * nbr + 2:3 * nbr + 2], refs[3 * nbr + 2:4 * nbr + 2]
    tm = o_ref.shape[1]
    accs, lses = [], []
    for (_, dil), a_ref, l_ref, a_sc, l_sc in zip(DIL_CFG, a_refs, l_refs, a_scs, l_scs):
        nblk = a_sc.shape[0]
        for r in range(dil):
            rows = pl.ds(r, tm // dil, stride=dil)
            for c in range(nblk):
                a_sc[c, rows, :] = a_ref[0, r, :, c * LANES:(c + 1) * LANES]
            l_sc[rows, :] = l_ref[0, r]
        accs.append(jnp.concatenate([a_sc[c] for c in range(nblk)], axis=1))
        lses.append(l_sc[...])
    mx = functools.reduce(jnp.maximum, lses)
    ws = [jnp.exp(x - mx) for x in lses]
    tot = functools.reduce(jnp.add, ws)
    out = None
    for w, a in zip(ws, accs):
        w_hi, w_lo = _split2(w / tot)
        wide = (jnp.dot(w_hi, e_ref[...], preferred_element_type=F32)
                + jnp.dot(w_lo, e_ref[...], preferred_element_type=F32))
        out = wide * a if out is None else out + wide * a
    o_ref[0] = out


def _dilated_merge(branch_outs, tm):
    bsz, _, _, aw = branch_outs[0][0].shape
    seq = branch_outs[0][0].shape[1] * branch_outs[0][0].shape[2]
    expand = np.zeros((LANES, aw), np.float32)
    for h in range(A_HEADS):
        expand[h, h * A_HEAD_DIM:(h + 1) * A_HEAD_DIM] = 1.0
    dils = [d for _, d in DIL_CFG]
    in_specs = ([pl.BlockSpec((1, d, tm // d, aw), lambda b, i: (b, 0, i, 0)) for d in dils]
                + [pl.BlockSpec((1, d, tm // d, LANES), lambda b, i: (b, 0, i, 0)) for d in dils]
                + [_const_spec(expand.shape)])
    return pl.pallas_call(
        _dilmerge_kernel,
        out_shape=jax.ShapeDtypeStruct((bsz, seq, aw), F32),
        grid=(bsz, seq // tm),
        in_specs=in_specs,
        out_specs=pl.BlockSpec((1, tm, aw), lambda b, i: (b, i, 0)),
        scratch_shapes=([pltpu.VMEM((aw // LANES, tm, LANES), F32) for _ in dils]
                        + [pltpu.VMEM((tm, LANES), F32) for _ in dils]),
        compiler_params=_cparams("parallel", "parallel"),
        name="dilmerge",
    )(*[o for o, _ in branch_outs], *[l for _, l in branch_outs], jnp.asarray(expand, BF16))


def _hgrn_consts(reverse):
    t = HGRN_BLOCK
    r = np.arange(t)
    u = r[None, :]
    row = r[:, None]
    nmats = [(u >= row) if reverse else (u <= row)]
    masks = []
    m = t // 2
    while m >= 1:
        grp = r // (2 * m)
        in_first = (r % (2 * m)) < m
        same = grp[:, None] == grp[None, :]
        if reverse:
            beta = (grp * 2 * m + m)[:, None]
            n = np.where(in_first[:, None], (u >= row) & (u < beta), (u >= beta) & (u < row))
            mask = same & in_first[:, None] & ~in_first[None, :]
        else:
            beta = (grp * 2 * m + m - 1)[:, None]
            n = np.where(in_first[:, None], (u > row) & (u <= beta), (u > beta) & (u <= row))
            mask = same & ~in_first[:, None] & in_first[None, :]
        nmats.append(n)
        masks.append(mask)
        m //= 2
    masks.append(np.eye(t, dtype=bool))
    nmat = jnp.asarray(np.concatenate(nmats, axis=0), F32).astype(BF16)
    return nmat, jnp.asarray(np.stack(masks), F32)


def _hgrn_block(q, f, v, lb, st, nmat_ref, mask_ref, reverse):
    t = HGRN_BLOCK
    nlev = mask_ref.shape[0] - 1
    qs = q * (B_DK ** -0.5)
    fa = lb + (1.0 - lb) * _sigmoid(f)
    kk = 1.0 - fa
    g_hi, g_lo = _split2(jnp.log(fa))
    ex = jnp.dot(nmat_ref[...], jnp.concatenate([g_hi, g_lo], axis=1), preferred_element_type=F32)
    ex = ex[:, :B_DK] + ex[:, B_DK:]
    b = ex[:t]
    btot = b[0:1] if reverse else b[t - 1:t]
    a = mask_ref[nlev] * _dot_nt(qs, kk)
    for l in range(nlev):
        e = jnp.exp(ex[(l + 1) * t:(l + 2) * t])
        a = a + mask_ref[l] * _dot_nt(qs * e, kk * e)
    out = _dot(a, v) + _dot_nt(qs * jnp.exp(b), st)
    khat = (kk * jnp.exp(btot - b)).astype(BF16)
    st_new = st * jnp.exp(btot) + jnp.dot(v.T.astype(BF16), khat, preferred_element_type=F32)
    return out, st_new


def _hgrn_lb(lg_ref, layer):
    lg = [lg_ref[l, 0] for l in range(lg_ref.shape[0])]
    mx = functools.reduce(jnp.maximum, lg)
    e = [jnp.exp(x - mx) for x in lg]
    return functools.reduce(jnp.add, e[:layer + 1]) / functools.reduce(jnp.add, e)


def _hgrn_kernel(qf_ref, ff_ref, vf_ref, qb_ref, fb_ref, vb_ref, lgf_ref, lgb_ref,
                 nf_ref, mf_ref, nb_ref, mb_ref, of_ref, ob_ref, sf_sc, sb_sc, *, layer):
    @pl.when(pl.program_id(2) == 0)
    def _():
        sf_sc[...] = jnp.zeros_like(sf_sc)
        sb_sc[...] = jnp.zeros_like(sb_sc)

    o, st = _hgrn_block(qf_ref[0], ff_ref[0], vf_ref[0], _hgrn_lb(lgf_ref, layer), sf_sc[...],
                        nf_ref, mf_ref, False)
    of_ref[0] = o
    sf_sc[...] = st
    o, st = _hgrn_block(qb_ref[0], fb_ref[0], vb_ref[0], _hgrn_lb(lgb_ref, layer), sb_sc[...],
                        nb_ref, mb_ref, True)
    ob_ref[0] = o
    sb_sc[...] = st


def _hgrn(z, lb_logits, layer, col0):
    bsz, seq, _ = z.shape
    t = HGRN_BLOCK
    nb = seq // t
    c0 = col0 // LANES
    hw = B_HEADS

    def zspec(group, rev):
        return pl.BlockSpec(
            (1, t, LANES),
            lambda b, h, j: (b, (nb - 1 - j) if rev else j, c0 + group * hw + h))

    def lgspec(direction):
        return pl.BlockSpec((lb_logits.shape[0], 1, 1, LANES),
                            lambda b, h, j: (0, direction * hw + h, 0, 0))

    nf, mf = _hgrn_consts(False)
    nbw, mbw = _hgrn_consts(True)
    lg = lb_logits.astype(F32).reshape(lb_logits.shape[0], 2 * hw, 1, LANES)
    o_shape = jax.ShapeDtypeStruct((bsz, seq, hw * B_DV), F32)
    return pl.pallas_call(
        functools.partial(_hgrn_kernel, layer=layer),
        out_shape=(o_shape, o_shape),
        grid=(bsz, hw, nb),
        in_specs=[zspec(0, False), zspec(1, False), zspec(3, False),
                  zspec(0, True), zspec(2, True), zspec(3, True),
                  lgspec(0), lgspec(1),
                  _const_spec(nf.shape), _const_spec(mf.shape),
                  _const_spec(nbw.shape), _const_spec(mbw.shape)],
        out_specs=(pl.BlockSpec((1, t, LANES), lambda b, h, j: (b, j, h)),
                   pl.BlockSpec((1, t, LANES), lambda b, h, j: (b, nb - 1 - j, h))),
        scratch_shapes=[pltpu.VMEM((B_DV, B_DK), F32), pltpu.VMEM((B_DV, B_DK), F32)],
        compiler_params=_cparams("parallel", "parallel", "arbitrary"),
        name="hgrn",
    )(z, z, z, z, z, z, lg, lg, nf, mf, nbw, mbw)


def _residual_epilogue(y, x_ref, gain_ref, gate_ref, o_ref):
    o_ref[0] = x_ref[0] + gate_ref[0] * _rms(y, gain_ref[...])


def _out0_kernel(a_ref, of_ref, ob_ref, g_ref, on_ref, w_ref, x_ref, gain_ref, gate_ref, o_ref):
    o = of_ref[0] + ob_ref[0]
    g = g_ref[0]
    parts = [_rms(o[:, h * B_DV:(h + 1) * B_DV], on_ref[...]) for h in range(B_HEADS)]
    bn = jnp.concatenate(parts, axis=-1) * (g * _sigmoid(g))
    na = a_ref.shape[-1]
    y = _dot(a_ref[0], w_ref[:na, :]) + _dot(bn, w_ref[na:, :])
    _residual_epilogue(y, x_ref, gain_ref, gate_ref, o_ref)


def _out1_kernel(c_ref, d_ref, w_ref, x_ref, gain_ref, gate_ref, o_ref):
    nc = c_ref.shape[-1]
    y = _dot(c_ref[0], w_ref[:nc, :]) + _dot(d_ref[0], w_ref[nc:, :])
    _residual_epilogue(y, x_ref, gain_ref, gate_ref, o_ref)


def _row_spec(tm, width, col=0):
    return pl.BlockSpec((1, tm, width), lambda b, i: (b, i, col))


def _out0(a_out, o_f, o_b, z, g_col, out_norm, w_bf16, x, gain, gate, tm):
    bsz, seq, d = x.shape
    wv = B_HEADS * B_DV
    vec = pl.BlockSpec((1, 1, d), lambda b, i: (b, 0, 0))
    return pl.pallas_call(
        _out0_kernel,
        out_shape=jax.ShapeDtypeStruct(x.shape, F32),
        grid=(bsz, seq // tm),
        in_specs=[_row_spec(tm, a_out.shape[-1]), _row_spec(tm, wv), _row_spec(tm, wv),
                  _row_spec(tm, wv, g_col // wv), _const_spec((1, B_DV)),
                  _const_spec(w_bf16.shape), _row_spec(tm, d), _const_spec((1, d)), vec],
        out_specs=_row_spec(tm, d),
        compiler_params=_cparams("parallel", "parallel"),
        name="out0",
    )(a_out, o_f, o_b, z, out_norm.reshape(1, B_DV), w_bf16, x, gain.reshape(1, d),
      gate.reshape(bsz, 1, d))


def _out1(c_out, d_out, w_bf16, x, gain, gate, tm):
    bsz, seq, d = x.shape
    vec = pl.BlockSpec((1, 1, d), lambda b, i: (b, 0, 0))
    return pl.pallas_call(
        _out1_kernel,
        out_shape=jax.ShapeDtypeStruct(x.shape, F32),
        grid=(bsz, seq // tm),
        in_specs=[_row_spec(tm, c_out.shape[-1]), _row_spec(tm, d_out.shape[-1]),
                  _const_spec(w_bf16.shape), _row_spec(tm, d), _const_spec((1, d)), vec],
        out_specs=_row_spec(tm, d),
        compiler_params=_cparams("parallel", "parallel"),
        name="out1",
    )(c_out, d_out, w_bf16, x, gain.reshape(1, d), gate.reshape(bsz, 1, d))


def _ffn_kernel(x_ref, g1_ref, sc_ref, sh_ref, wi_ref, wo_ref, g2_ref, gate_ref, o_ref, *, nchunk):
    x = x_ref[0]
    h = (_rms(x, g1_ref[...]) * (1.0 + sc_ref[0]) + sh_ref[0]).astype(BF16)
    hidden = wo_ref.shape[0]
    ck = hidden // nchunk
    y = None
    for c in range(nchunk):
        gt = jnp.dot(h, wi_ref[:, c * ck:(c + 1) * ck], preferred_element_type=F32)
        up = jnp.dot(h, wi_ref[:, hidden + c * ck:hidden + (c + 1) * ck], preferred_element_type=F32)
        part = _dot(gt * _sigmoid(gt) * up, wo_ref[c * ck:(c + 1) * ck, :])
        y = part if y is None else y + part
    o_ref[0] = x + gate_ref[0] * _rms(y, g2_ref[...])


def _ffn(x, g1, sc, sh, wi_bf16, wo_bf16, g2, gate, tm):
    bsz, seq, d = x.shape
    vec = pl.BlockSpec((1, 1, d), lambda b, i: (b, 0, 0))
    hidden = wo_bf16.shape[0]
    nchunk = 2 if (hidden // 2) % LANES == 0 else 1
    return pl.pallas_call(
        functools.partial(_ffn_kernel, nchunk=nchunk),
        out_shape=jax.ShapeDtypeStruct(x.shape, F32),
        grid=(bsz, seq // tm),
        in_specs=[_row_spec(tm, d), _const_spec((1, d)), vec, vec,
                  _const_spec(wi_bf16.shape), _const_spec(wo_bf16.shape),
                  _const_spec((1, d)), vec],
        out_specs=_row_spec(tm, d),
        compiler_params=_cparams("parallel", "parallel"),
        name="ffn",
    )(x, g1.reshape(1, d), sc.reshape(bsz, 1, d), sh.reshape(bsz, 1, d), wi_bf16, wo_bf16,
      g2.reshape(1, d), gate.reshape(bsz, 1, d))


def _fft1_kernel(th_ref, tl_ref, u_ref, p_ref):
    p_ref[0] = _dot_tab(th_ref[...], tl_ref[...], u_ref[0], tab_left=True)


def _fft2_kernel(pr_ref, pi_ref, tc_ref, ts_ref, fh_ref, fl_ref, wh_ref, wl_ref, o_ref, *, scale):
    tc = tc_ref[0]
    ts = ts_ref[0]
    qr, qi = [], []
    for g in range(C_GROUPS):
        cols = slice(g * C_WIDTH, (g + 1) * C_WIDTH)
        pr = pr_ref[0, 0, 0, :, cols]
        pim = pi_ref[0, 0, 0, :, cols]
        qr.append(pr * tc + pim * ts)
        qi.append(pim * tc - pr * ts)
    q = jnp.concatenate([jnp.concatenate(qr, axis=1), jnp.concatenate(qi, axis=1)], axis=0)
    xx = _dot_tab(fh_ref[...], fl_ref[...], q, tab_left=True)
    n2 = xx.shape[0] // 2
    for g in range(C_GROUPS):
        cols = slice(g * C_WIDTH, (g + 1) * C_WIDTH)
        xg = jnp.concatenate([xx[:n2, cols], xx[n2:, cols]], axis=1)
        o_ref[0, :, cols] = _dot_tab(wh_ref[...], wl_ref[...], xg, tab_left=False) * scale


def _fourier_mixer(u):
    bsz, seq, cw = u.shape
    n2 = FFT_N2
    n1 = seq // n2
    assert n1 * n2 == seq and cw == C_GROUPS * C_WIDTH
    a1 = 2.0 * np.pi * np.outer(np.arange(n1), np.arange(n1)) / n1
    f1 = np.concatenate([np.cos(a1), -np.sin(a1)], axis=0)
    a2 = 2.0 * np.pi * np.outer(np.arange(n2), np.arange(n2)) / n2
    c2, s2 = np.cos(a2), np.sin(a2)
    f2 = np.block([[c2, s2], [-s2, c2]])
    aw = 2.0 * np.pi * np.outer(np.arange(C_WIDTH), np.arange(C_WIDTH)) / C_WIDTH
    fw = np.concatenate([np.cos(aw), np.sin(aw)], axis=0)
    at = 2.0 * np.pi * np.outer(np.arange(n1), np.arange(n2)) / seq
    tw_c = jnp.asarray(np.repeat(np.cos(at)[:, :, None], C_WIDTH, axis=2), F32)
    tw_s = jnp.asarray(np.repeat(np.sin(at)[:, :, None], C_WIDTH, axis=2), F32)
    f1h, f1l = _np_split2(f1)
    f2h, f2l = _np_split2(f2)
    fwh, fwl = _np_split2(fw)

    ncol = n2 * cw
    tn = 8192
    p = pl.pallas_call(
        _fft1_kernel,
        out_shape=jax.ShapeDtypeStruct((bsz, 2 * n1, ncol), F32),
        grid=(bsz, ncol // tn),
        in_specs=[_const_spec(f1h.shape), _const_spec(f1l.shape),
                  pl.BlockSpec((1, n1, tn), lambda b, j: (b, 0, j))],
        out_specs=pl.BlockSpec((1, 2 * n1, tn), lambda b, j: (b, 0, j)),
        compiler_params=_cparams("parallel", "parallel"),
        name="fft1",
    )(f1h, f1l, u.reshape(bsz, n1, ncol))
    p5 = p.reshape(bsz, 2, n1, n2, cw)
    tw_spec = pl.BlockSpec((1, n2, C_WIDTH), lambda b, k: (k, 0, 0))
    y = pl.pallas_call(
        functools.partial(_fft2_kernel, scale=float(1.0 / np.sqrt(seq * C_WIDTH))),
        out_shape=jax.ShapeDtypeStruct((bsz, n2, n1 * cw), F32),
        grid=(bsz, n1),
        in_specs=[pl.BlockSpec((1, 1, 1, n2, cw), lambda b, k: (b, 0, k, 0, 0)),
                  pl.BlockSpec((1, 1, 1, n2, cw), lambda b, k: (b, 1, k, 0, 0)),
                  tw_spec, tw_spec,
                  _const_spec(f2h.shape), _const_spec(f2l.shape),
                  _const_spec(fwh.shape), _const_spec(fwl.shape)],
        out_specs=pl.BlockSpec((1, n2, cw), lambda b, k: (b, 0, k)),
        compiler_params=_cparams("parallel", "parallel"),
        name="fft2",
    )(p5, p5, tw_c, tw_s, f2h, f2l, fwh, fwl)
    return y.reshape(bsz, seq, cw)


def _head_perm():
    rep = D_Q_HEADS // D_KV_HEADS
    cols = []
    for j in range(rep):
        for g in range(D_KV_HEADS):
            h = g * rep + j
            cols.extend(range(h * D_HEAD_DIM, (h + 1) * D_HEAD_DIM))
    return np.asarray(cols, np.int32)


def _rope_tables(seq):
    rows = seq // GRID_W
    row = jnp.repeat(jnp.arange(rows, dtype=F32), GRID_W)
    col = jnp.tile(jnp.arange(GRID_W, dtype=F32), rows)
    axis_dim = D_HEAD_DIM // 2
    inv_freq = jnp.power(ROPE_THETA, -jnp.arange(0, axis_dim, 2, dtype=F32) / axis_dim)
    ang_r = row[:, None] * inv_freq[None, :]
    ang_c = col[:, None] * inv_freq[None, :]
    cr, sr, cc, sc = jnp.cos(ang_r), jnp.sin(ang_r), jnp.cos(ang_c), jnp.sin(ang_c)
    cos = jnp.concatenate([cr, cr, cc, cc], axis=1)
    sin = jnp.concatenate([-sr, sr, -sc, sc], axis=1)
    reps = LANES // D_HEAD_DIM
    return jnp.tile(cos, (1, reps)), jnp.tile(sin, (1, reps))


def _qkprep_kernel(q_ref, k_ref, v_ref, cos_ref, sin_ref, bd_h_ref, bd_l_ref, gq_ref, gk_ref,
                   qo_ref, ko_ref, vo_ref):
    cos = cos_ref[...]
    sin = sin_ref[...]
    quarter = D_HEAD_DIM // 4
    lane = lax.broadcasted_iota(jnp.int32, (1, LANES), 1)
    first_of_pair = (lane // quarter) % 2 == 0

    def norm_rope(x, gain, scale):
        ms = _dot_tab(bd_h_ref[...], bd_l_ref[...], x * x, tab_left=False)
        xn = x * lax.rsqrt(ms + EPS) * gain
        partner = jnp.where(first_of_pair, pltpu.roll(xn, LANES - quarter, 1),
                            pltpu.roll(xn, quarter, 1))
        return ((xn * cos + partner * sin) * scale).astype(BF16)

    for j in range(q_ref.shape[-1] // LANES):
        cols = slice(j * LANES, (j + 1) * LANES)
        qo_ref[0, :, cols] = norm_rope(q_ref[0, :, cols], gq_ref[...], D_HEAD_DIM ** -0.5 * LOG2E)
    ko_ref[0] = norm_rope(k_ref[0], gk_ref[...], 1.0)
    vo_ref[0] = v_ref[0].T.astype(BF16)


def _qkprep(qkv, qk_norm_j, tm):
    bsz, seq, _ = qkv.shape
    qw = D_Q_HEADS * D_HEAD_DIM
    kw = D_KV_HEADS * D_HEAD_DIM
    assert kw == LANES
    cos, sin = _rope_tables(seq)
    bd = np.kron(np.eye(LANES // D_HEAD_DIM), np.full((D_HEAD_DIM, D_HEAD_DIM), 1.0 / D_HEAD_DIM))
    bd_h, bd_l = _np_split2(bd)
    reps = LANES // D_HEAD_DIM
    gq = jnp.tile(qk_norm_j[0].astype(F32), reps).reshape(1, LANES)
    gk = jnp.tile(qk_norm_j[1].astype(F32), reps).reshape(1, LANES)
    tab = pl.BlockSpec((tm, LANES), lambda b, i: (i, 0))
    return pl.pallas_call(
        _qkprep_kernel,
        out_shape=(jax.ShapeDtypeStruct((bsz, seq, qw), BF16),
                   jax.ShapeDtypeStruct((bsz, seq, kw), BF16),
                   jax.ShapeDtypeStruct((bsz, kw, seq), BF16)),
        grid=(bsz, seq // tm),
        in_specs=[_row_spec(tm, qw, 0), _row_spec(tm, kw, qw // kw), _row_spec(tm, kw, qw // kw + 1),
                  tab, tab, _const_spec(bd_h.shape), _const_spec(bd_l.shape),
                  _const_spec((1, LANES)), _const_spec((1, LANES))],
        out_specs=(_row_spec(tm, qw), _row_spec(tm, kw),
                   pl.BlockSpec((1, kw, tm), lambda b, i: (b, 0, i))),
        compiler_params=_cparams("parallel", "parallel"),
        name="qkprep",
    )(qkv, qkv, qkv, cos, sin, bd_h, bd_l, gq, gk)


def _flash_kernel(q_ref, k_ref, vt_ref, o_ref, m_sc, l_sc, acc_sc, s_sc):
    kv = pl.program_id(2)

    @pl.when(kv == 0)
    def _():
        m_sc[...] = jnp.full_like(m_sc, -jnp.inf)
        l_sc[...] = jnp.zeros_like(l_sc)
        acc_sc[...] = jnp.zeros_like(acc_sc)

    lane = lax.broadcasted_iota(jnp.int32, (1, LANES), 1)
    lo = lane < D_HEAD_DIM
    nblk = q_ref.shape[-1] // LANES
    tq, tk = q_ref.shape[1], k_ref.shape[1]
    ku, qu = FLASH_KEY_UNIT, FLASH_QUERY_UNIT
    k = k_ref[0]
    nheads = nblk * D_KV_HEADS

    def logits_pass(idx):
        j, g = divmod(idx, D_KV_HEADS)
        qj = q_ref[0, :, j * LANES:(j + 1) * LANES]
        sel = lo if g == 0 else jnp.logical_not(lo)
        qm = jnp.where(sel, qj, jnp.zeros_like(qj))
        mcs = []
        for c in range(tq // qu):
            s = lax.dot_general(k, qm[c * qu:(c + 1) * qu], (((1,), (1,)), ((), ())),
                                preferred_element_type=F32)
            s_sc[idx, :, c * qu:(c + 1) * qu] = s
            mcs.append(jnp.max(s, axis=0, keepdims=True))
        m_prev = m_sc[idx, 0:1, :]
        m_new = jnp.maximum(m_prev, jnp.concatenate(mcs, axis=1))
        m_sc[idx, 0:1, :] = m_new
        return m_new, jnp.exp2(m_prev - m_new)

    def value_pass(idx, m_new, alpha):
        j, g = divmod(idx, D_KV_HEADS)
        rows = slice(g * D_HEAD_DIM, (g + 1) * D_HEAD_DIM)
        lsums = []
        for c in range(tq // qu):
            qcols = slice(c * qu, (c + 1) * qu)
            pv = None
            lsum = None
            for u in range(tk // ku):
                keys = slice(u * ku, (u + 1) * ku)
                p = jnp.exp2(s_sc[idx, keys, qcols] - m_new[:, qcols])
                ps = jnp.sum(p, axis=0, keepdims=True)
                d = jnp.dot(vt_ref[0, rows, keys], p.astype(BF16), preferred_element_type=F32)
                pv = d if pv is None else pv + d
                lsum = ps if lsum is None else lsum + ps
            acc_sc[j, rows, qcols] = alpha[:, qcols] * acc_sc[j, rows, qcols] + pv
            lsums.append(lsum)
        l_sc[idx, 0:1, :] = alpha * l_sc[idx, 0:1, :] + jnp.concatenate(lsums, axis=1)

    stats = logits_pass(0)
    for idx in range(nheads):
        nxt = logits_pass(idx + 1) if idx + 1 < nheads else None
        value_pass(idx, *stats)
        stats = nxt

    @pl.when(kv == pl.num_programs(2) - 1)
    def _():
        for j in range(nblk):
            l = jnp.concatenate(
                [jnp.broadcast_to(l_sc[j * D_KV_HEADS + g, 0:1, :], (D_HEAD_DIM, tq))
                 for g in range(D_KV_HEADS)], axis=0)
            o_ref[0, :, j * LANES:(j + 1) * LANES] = (acc_sc[j] / l).T.astype(o_ref.dtype)


def _flash(q, k, vt, tq, tk):
    bsz, seq, qw = q.shape
    kw = k.shape[-1]
    return pl.pallas_call(
        _flash_kernel,
        out_shape=jax.ShapeDtypeStruct((bsz, seq, qw), F32),
        grid=(bsz, seq // tq, seq // tk),
        in_specs=[pl.BlockSpec((1, tq, qw), lambda b, i, j: (b, i, 0)),
                  pl.BlockSpec((1, tk, kw), lambda b, i, j: (b, j, 0)),
                  pl.BlockSpec((1, kw, tk), lambda b, i, j: (b, 0, j))],
        out_specs=pl.BlockSpec((1, tq, qw), lambda b, i, j: (b, i, 0)),
        scratch_shapes=[pltpu.VMEM((D_Q_HEADS, 8, tq), F32),
                        pltpu.VMEM((D_Q_HEADS, 8, tq), F32),
                        pltpu.VMEM((qw // LANES, LANES, tq), F32),
                        pltpu.VMEM((D_Q_HEADS, tk, tq), F32)],
        compiler_params=_cparams("parallel", "parallel", "arbitrary"),
        name="flash",
    )(q, k, vt)


def kernel(x, c, t5_bias, hgrn_lb_logits, ada_w, ada_b, norm_gains, ab_w_in, ab_w_out,
           hgrn_out_norm, cd_w_in, cd_w_out, qk_norm, ffn_w_in, ffn_w_out):
    bsz, seq, d = x.shape
    depth = ada_w.shape[0]
    mod = _ada_mod(c.astype(F32), ada_w, ada_b)
    perm = _head_perm()
    aw = A_HEADS * A_HEAD_DIM
    cw = C_GROUPS * C_WIDTH
    qw = D_Q_HEADS * D_HEAD_DIM
    tm_in = min(256, seq)
    tm = min(512, seq)
    for layer in range(depth):
        sh_m, sc_m, g_m, sh_f, sc_f, g_f = [mod[layer, :, i * d:(i + 1) * d] for i in range(6)]
        gains = norm_gains[layer]
        j = layer // 2
        if layer % 2 == 0:
            w_in = ab_w_in[j].astype(BF16)
            *qkv_cm, z = _inproj_cm(x, gains[0], sc_m, sh_m, w_in, 3 * aw, tm_in)
            branches = [_dilated_branch(cm, t5_bias, window, dil)
                        for cm, (window, dil) in zip(qkv_cm, DIL_CFG)]
            a_out = _dilated_merge(branches, tm)
            o_f, o_b = _hgrn(z, hgrn_lb_logits, layer, 0)
            g_col = 3 * B_HEADS * B_DK + B_HEADS * B_DV
            x = _out0(a_out, o_f, o_b, z, g_col, hgrn_out_norm[j], ab_w_out[j].astype(BF16),
                      x, gains[1], g_m, tm)
        else:
            w_full = cd_w_in[j]
            w_in = jnp.concatenate([w_full[:, :cw], w_full[:, cw:cw + qw][:, perm],
                                    w_full[:, cw + qw:]], axis=1).astype(BF16)
            u, qkv = _inproj(x, gains[0], sc_m, sh_m, w_in, (cw, w_in.shape[1] - cw), tm)
            c_out = _fourier_mixer(u)
            qn, kn, vn = _qkprep(qkv, qk_norm[j], tm)
            d_out = _flash(qn, kn, vn, min(FLASH_TQ, seq), min(FLASH_TK, seq))
            w_out_full = cd_w_out[j]
            w_out = jnp.concatenate([w_out_full[:cw], w_out_full[cw:][perm]], axis=0).astype(BF16)
            x = _out1(c_out, d_out, w_out, x, gains[1], g_m, tm)
        x = _ffn(x, gains[2], sc_f, sh_f, ffn_w_in[layer].astype(BF16), ffn_w_out[layer].astype(BF16),
                 gains[3], g_f, tm)
    return x
```

```python
import functools

import numpy as np
import jax
import jax.numpy as jnp
from jax import lax
from jax.experimental import pallas as pl
from jax.experimental.pallas import tpu as pltpu

F32 = jnp.float32
BF16 = jnp.bfloat16
LANES = 128
VMEM_LIMIT_BYTES = 56 * 2**20
NEG_INF = -1e30
EPS = 1e-6

GRID_W = 64
A_HEADS = 8
A_HEAD_DIM = 64
DIL_CFG = ((128, 1), (512, 4), (2048, 16))
N_BUCKETS = 32
T5_MAX_DIST = 1024
B_HEADS = 4
B_DK = 128
B_DV = 128
C_GROUPS = 4
C_WIDTH = 128
D_Q_HEADS = 8
D_KV_HEADS = 2
D_HEAD_DIM = 64
ROPE_THETA = 10000.0

DIL_TQ = 128
DIL_TILE = 256
HGRN_BLOCK = 256
FLASH_TQ = 1024
FLASH_TK = 512
FLASH_KEY_UNIT = 128
FLASH_QUERY_UNIT = 256
FLASH_ONES = 16
FFT_N2 = 128
LOG2E = 1.4426950408889634


def _cparams(*sem):
    return pltpu.CompilerParams(dimension_semantics=sem, vmem_limit_bytes=VMEM_LIMIT_BYTES)


def _const_spec(shape):
    nd = len(shape)
    return pl.BlockSpec(shape, lambda *_: (0,) * nd, pipeline_mode=pl.Buffered(1))


def _sigmoid(x):
    return 1.0 / (1.0 + jnp.exp(-x))


def _dot(a, b):
    return jnp.dot(a.astype(BF16), b.astype(BF16), preferred_element_type=F32)


def _dot_nt(a, b):
    return lax.dot_general(a.astype(BF16), b.astype(BF16), (((1,), (1,)), ((), ())),
                           preferred_element_type=F32)


def _split2(a):
    hi = a.astype(BF16)
    lo = (a - hi.astype(F32)).astype(BF16)
    return hi, lo


def _split3(a):
    a1 = a.astype(BF16)
    r = a - a1.astype(F32)
    a2 = r.astype(BF16)
    a3 = (r - a2.astype(F32)).astype(BF16)
    return a1, a2, a3


def _dot_tab(tab_hi, tab_lo, x, *, tab_left):
    x_hi, x_lo = _split2(x)
    if tab_left:
        d = lambda t, v: jnp.dot(t, v, preferred_element_type=F32)
    else:
        d = lambda t, v: jnp.dot(v, t, preferred_element_type=F32)
    return d(tab_hi, x_hi) + (d(tab_hi, x_lo) + d(tab_lo, x_hi))


def _rms(x, gain):
    ms = jnp.mean(x * x, axis=-1, keepdims=True)
    return x * lax.rsqrt(ms + EPS) * gain


def _np_split2(t):
    t = np.asarray(t, np.float32)
    hi = jnp.asarray(t, F32).astype(BF16)
    lo = (jnp.asarray(t, F32) - hi.astype(F32)).astype(BF16)
    return hi, lo


def _mod_kernel(c_ref, w_ref, b_ref, o_ref):
    c = c_ref[...]
    o_ref[0] = _dot(c * _sigmoid(c), w_ref[0]) + b_ref[0]


def _ada_mod(c, ada_w, ada_b):
    depth, d, n6 = ada_w.shape
    bsz = c.shape[0]
    rows = 8
    cp = jnp.zeros((rows, d), F32).at[:bsz].set(c)
    tn = n6 // 4
    out = pl.pallas_call(
        _mod_kernel,
        out_shape=jax.ShapeDtypeStruct((depth, rows, n6), F32),
        grid=(depth, n6 // tn),
        in_specs=[pl.BlockSpec((rows, d), lambda l, j: (0, 0)),
                  pl.BlockSpec((1, d, tn), lambda l, j: (l, 0, j)),
                  pl.BlockSpec((1, 1, tn), lambda l, j: (l, 0, j))],
        out_specs=pl.BlockSpec((1, rows, tn), lambda l, j: (l, 0, j)),
        compiler_params=_cparams("parallel", "parallel"),
        name="ada_mod",
    )(cp, ada_w, ada_b.reshape(depth, 1, n6))
    return out[:, :bsz]


def _inproj_kernel(x_ref, gain_ref, sc_ref, sh_ref, w_ref, *o_refs):
    h = _rms(x_ref[0], gain_ref[...]) * (1.0 + sc_ref[0]) + sh_ref[0]
    z = _dot(h, w_ref[...])
    off = 0
    for o_ref in o_refs:
        n = o_ref.shape[-1]
        o_ref[0] = z[:, off:off + n].astype(o_ref.dtype)
        off += n


def _inproj(x, gain, sc, sh, w_bf16, splits, tm):
    bsz, seq, d = x.shape
    n = w_bf16.shape[1]
    assert sum(splits) == n
    vec = pl.BlockSpec((1, 1, d), lambda b, i: (b, 0, 0))
    return pl.pallas_call(
        _inproj_kernel,
        out_shape=tuple(jax.ShapeDtypeStruct((bsz, seq, s), F32) for s in splits),
        grid=(bsz, seq // tm),
        in_specs=[pl.BlockSpec((1, tm, d), lambda b, i: (b, i, 0)),
                  _const_spec((1, d)), vec, vec, _const_spec((d, n))],
        out_specs=tuple(pl.BlockSpec((1, tm, s), lambda b, i: (b, i, 0)) for s in splits),
        compiler_params=_cparams("parallel", "parallel"),
        name="inproj",
    )(x, gain.reshape(1, d), sc.reshape(bsz, 1, d), sh.reshape(bsz, 1, d), w_bf16)


def _inproj_cm_kernel(x_ref, gain_ref, sc_ref, sh_ref, w_ref, *refs):
    cm_refs, rest_ref, zs_sc = refs[:-2], refs[-2], refs[-1]
    h = _rms(x_ref[0], gain_ref[...]) * (1.0 + sc_ref[0]) + sh_ref[0]
    z = _dot(h, w_ref[...])
    nblk, tm, _ = zs_sc.shape
    rest_ref[0] = z[:, nblk * LANES:]
    for c in range(nblk):
        zs_sc[c] = z[:, c * LANES:(c + 1) * LANES]
    for cm_ref, (_, dil) in zip(cm_refs, DIL_CFG):
        for r in range(dil):
            for c in range(nblk):
                cm_ref[0, r, :, c * LANES:(c + 1) * LANES] = (
                    zs_sc[c, pl.ds(r, tm // dil, stride=dil), :].astype(BF16))


def _inproj_cm(x, gain, sc, sh, w_bf16, na, tm):
    bsz, seq, d = x.shape
    n = w_bf16.shape[1]
    vec = pl.BlockSpec((1, 1, d), lambda b, i: (b, 0, 0))
    dils = [dl for _, dl in DIL_CFG]
    return pl.pallas_call(
        _inproj_cm_kernel,
        out_shape=tuple([jax.ShapeDtypeStruct((bsz, dl, seq // dl, na), BF16) for dl in dils]
                        + [jax.ShapeDtypeStruct((bsz, seq, n - na), F32)]),
        grid=(bsz, seq // tm),
        in_specs=[pl.BlockSpec((1, tm, d), lambda b, i: (b, i, 0)),
                  _const_spec((1, d)), vec, vec, _const_spec((d, n))],
        out_specs=tuple([pl.BlockSpec((1, dl, tm // dl, na), lambda b, i: (b, 0, i, 0)) for dl in dils]
                        + [pl.BlockSpec((1, tm, n - na), lambda b, i: (b, i, 0))]),
        scratch_shapes=[pltpu.VMEM((na // LANES, tm, LANES), F32)],
        compiler_params=_cparams("parallel", "parallel"),
        name="inproj_cm",
    )(x, gain.reshape(1, d), sc.reshape(bsz, 1, d), sh.reshape(bsz, 1, d), w_bf16)


def _t5_buckets(rel):
    half = N_BUCKETS // 2
    max_exact = half // 2
    n = np.abs(rel)
    large = max_exact + (np.log(np.maximum(n, 1) / max_exact) / np.log(T5_MAX_DIST / max_exact)
                         * (half - max_exact)).astype(np.int32)
    large = np.minimum(large, half - 1)
    return (np.where(rel > 0, half, 0) + np.where(n < max_exact, n, large)).astype(np.int32)


def _dil_bias(t5_bias, window, dil, tq):
    half = (window // 2) // dil
    assert half == tq // 2
    rel = np.arange(2 * tq)[None, :] - half - np.arange(tq)[:, None]
    inside = np.abs(rel) <= half
    buckets = _t5_buckets(np.where(inside, rel, 0) * dil)
    onehot =jnp.asarray(np.eye(N_BUCKETS, dtype=np.float32)[buckets])
    bias = jnp.einsum("qkn,nh->hqk", onehot, t5_bias.astype(F32), precision=lax.Precision.HIGHEST)
    return jnp.where(jnp.asarray(inside)[None], bias, NEG_INF)


def _dil_kernel(q_ref, kp_ref, kc_ref, kn_ref, vp_ref, vc_ref, vn_ref, bias_ref, o_ref, lse_ref,
                *, class_len):
    i = pl.program_id(2)
    sub, hq, tile = DIL_TQ, DIL_TQ // 2, DIL_TILE
    kwin = jnp.concatenate([kp_ref[0, 0], kc_ref[0, 0], kn_ref[0, 0]], axis=0)
    vwin = jnp.concatenate([vp_ref[0, 0], vc_ref[0, 0], vn_ref[0, 0]], axis=0)
    lane = lax.broadcasted_iota(jnp.int32, (1, LANES), 1)
    lo = lane < A_HEAD_DIM
    scale = jnp.asarray(A_HEAD_DIM ** -0.5, BF16)
    for jt in range(tile // sub):
        qrows = slice(jt * sub, (jt + 1) * sub)
        krows = slice(jt * sub, jt * sub + 2 * sub)
        kpos = i * tile + jt * sub - hq + lax.broadcasted_iota(jnp.int32, (1, 2 * sub), 1)
        valid = jnp.logical_and(kpos >= 0, kpos < class_len)
        lse_all = jnp.zeros((sub, LANES), F32)
        for j in range(A_HEADS // 2):
            cols = slice(j * LANES, (j + 1) * LANES)
            qj = q_ref[0, 0, qrows, cols] * scale
            kj = kwin[krows, cols]
            vj = vwin[krows, cols]
            halves = []
            for half in range(2):
                h = 2 * j + half
                sel = lo if half == 0 else jnp.logical_not(lo)
                s = lax.dot_general(jnp.where(sel, qj, jnp.zeros_like(qj)), kj,
                                    (((1,), (1,)), ((), ())), preferred_element_type=F32)
                s = jnp.where(valid, s + bias_ref[h], NEG_INF)
                m = jnp.max(s, axis=-1, keepdims=True)
                p = jnp.exp(s - m)
                l = jnp.sum(p, axis=-1, keepdims=True)
                halves.append(jnp.dot(p.astype(BF16), vj, preferred_element_type=F32) / l)
                lse_all = jnp.where(lane == h, m + jnp.log(l), lse_all)
            o_ref[0, 0, qrows, cols] = jnp.where(lo, halves[0], halves[1])
        lse_ref[0, 0, qrows, :] = lse_all


def _dilated_branch(qkv_cm, t5_bias, window, dil):
    bsz, _, cl, width = qkv_cm.shape
    aw = A_HEADS * A_HEAD_DIM
    tile, hq = DIL_TILE, DIL_TQ // 2
    nt = cl // tile
    per = tile // hq
    nh = cl // hq

    def cur(col):
        return pl.BlockSpec((1, 1, tile, aw), lambda b, r, i: (b, r, i, col))

    def prev(col):
        return pl.BlockSpec((1, 1, hq, aw), lambda b, r, i: (b, r, jnp.maximum(i * per - 1, 0), col))

    def nxt(col):
        return pl.BlockSpec((1, 1, hq, aw),
                            lambda b, r, i: (b, r, jnp.minimum((i + 1) * per, nh - 1), col))

    return pl.pallas_call(
        functools.partial(_dil_kernel, class_len=cl),
        out_shape=(jax.ShapeDtypeStruct((bsz, dil, cl, aw), F32),
                   jax.ShapeDtypeStruct((bsz, dil, cl, LANES), F32)),
        grid=(bsz, dil, nt),
        in_specs=[cur(0), prev(1), cur(1), nxt(1), prev(2), cur(2), nxt(2),
                  _const_spec((A_HEADS, DIL_TQ, 2 * DIL_TQ))],
        out_specs=(pl.BlockSpec((1, 1, tile, aw), lambda b, r, i: (b, r, i, 0)),
                   pl.BlockSpec((1, 1, tile, LANES), lambda b, r, i: (b, r, i, 0))),
        compiler_params=_cparams("parallel", "parallel", "parallel"),
        name=f"dilated_d{dil}",
    )(*([qkv_cm] * 7), _dil_bias(t5_bias, window, dil, DIL_TQ))


def _dilmerge_kernel(*refs):
    nbr = len(DIL_CFG)
    a_refs, l_refs = refs[:nbr], refs[nbr:2 * nbr]
    e_ref, o_ref = refs[2 * nbr], refs[2 * nbr + 1]
    a_scs, l_scs = refs[2 * nbr + 2:3 * nbr + 2], refs[3 * nbr + 2:4 * nbr + 2]
    tm = o_ref.shape[1]
    accs, lses = [], []
    for (_, dil), a_ref, l_ref, a_sc, l_sc in zip(DIL_CFG, a_refs, l_refs, a_scs, l_scs):
        nblk = a_sc.shape[0]
        for r in range(dil):
            rows = pl.ds(r, tm // dil, stride=dil)
            for c in range(nblk):
                a_sc[c, rows, :] = a_ref[0, r, :, c * LANES:(c + 1) * LANES]
            l_sc[rows, :] = l_ref[0, r]
        accs.append(jnp.concatenate([a_sc[c] for c in range(nblk)], axis=1))
        lses.append(l_sc[...])
    mx = functools.reduce(jnp.maximum, lses)
    ws = [jnp.exp(x - mx) for x in lses]
    tot = functools.reduce(jnp.add, ws)
    out = None
    for w, a in zip(ws, accs):
        w_hi, w_lo = _split2(w / tot)
        wide = (jnp.dot(w_hi, e_ref[...], preferred_element_type=F32)
                + jnp.dot(w_lo, e_ref[...], preferred_element_type=F32))
        out = wide * a if out is None else out + wide * a
    o_ref[0] = out.astype(o_ref.dtype)


def _dilated_merge(branch_outs, tm):
    bsz, _, _, aw = branch_outs[0][0].shape
    seq = branch_outs[0][0].shape[1] * branch_outs[0][0].shape[2]
    expand = np.zeros((LANES, aw), np.float32)
    for h in range(A_HEADS):
        expand[h, h * A_HEAD_DIM:(h + 1) * A_HEAD_DIM] = 1.0
    dils = [d for _, d in DIL_CFG]
    in_specs = ([pl.BlockSpec((1, d, tm // d, aw), lambda b, i: (b, 0, i, 0)) for d in dils]
                + [pl.BlockSpec((1, d, tm // d, LANES), lambda b, i: (b, 0, i, 0)) for d in dils]
                + [_const_spec(expand.shape)])
    return pl.pallas_call(
        _dilmerge_kernel,
        out_shape=jax.ShapeDtypeStruct((bsz, seq, aw), BF16),
        grid=(bsz, seq // tm),
        in_specs=in_specs,
        out_specs=pl.BlockSpec((1, tm, aw), lambda b, i: (b, i, 0)),
        scratch_shapes=([pltpu.VMEM((aw // LANES, tm, LANES), F32) for _ in dils]
                        + [pltpu.VMEM((tm, LANES), F32) for _ in dils]),
        compiler_params=_cparams("parallel", "parallel"),
        name="dilmerge",
    )(*[o for o, _ in branch_outs], *[l for _, l in branch_outs], jnp.asarray(expand, BF16))


def _hgrn_consts(reverse):
    t = HGRN_BLOCK
    r = np.arange(t)
    u = r[None, :]
    row = r[:, None]
    nmats = [(u >= row) if reverse else (u <= row)]
    masks = []
    m = t // 2
    while m >= 1:
        grp = r // (2 * m)
        in_first = (r % (2 * m)) < m
        same = grp[:, None] == grp[None, :]
        if reverse:
            beta = (grp * 2 * m + m)[:, None]
            n = np.where(in_first[:, None], (u >= row) & (u < beta), (u >= beta) & (u < row))
            mask = same & in_first[:, None] & ~in_first[None, :]
        else:
            beta = (grp * 2 * m + m - 1)[:, None]
            n = np.where(in_first[:, None], (u > row) & (u <= beta), (u > beta) & (u <= row))
            mask = same & ~in_first[:, None] & in_first[None, :]
        nmats.append(n)
        masks.append(mask)
        m //= 2
    masks.append(np.eye(t, dtype=bool))
    nmat = jnp.asarray(np.concatenate(nmats, axis=0), F32).astype(BF16)
    return nmat, jnp.asarray(np.stack(masks), F32)


def _hgrn_block(q, f, v, lb, st, nmat_ref, mask_ref, reverse):
    t = HGRN_BLOCK
    nlev = mask_ref.shape[0] - 1
    qs = q * (B_DK ** -0.5)
    fa = lb + (1.0 - lb) * _sigmoid(f)
    kk = 1.0 - fa
    g_hi, g_lo = _split2(jnp.log(fa))
    ex = jnp.dot(nmat_ref[...], jnp.concatenate([g_hi, g_lo], axis=1), preferred_element_type=F32)
    ex = ex[:, :B_DK] + ex[:, B_DK:]
    b = ex[:t]
    btot = b[0:1] if reverse else b[t - 1:t]
    a = mask_ref[nlev] * _dot_nt(qs, kk)
    for l in range(nlev):
        e = jnp.exp(ex[(l + 1) * t:(l + 2) * t])
        a = a + mask_ref[l] * _dot_nt(qs * e, kk * e)
    out = _dot(a, v) + _dot_nt(qs * jnp.exp(b), st)
    khat = (kk * jnp.exp(btot - b)).astype(BF16)
    st_new = st * jnp.exp(btot) + jnp.dot(v.T.astype(BF16), khat, preferred_element_type=F32)
    return out, st_new


def _hgrn_lb(lg_ref, layer):
    lg = [lg_ref[l, 0] for l in range(lg_ref.shape[0])]
    mx = functools.reduce(jnp.maximum, lg)
    e = [jnp.exp(x - mx) for x in lg]
    return functools.reduce(jnp.add, e[:layer + 1]) / functools.reduce(jnp.add, e)


def _hgrn_kernel(qf_ref, ff_ref, vf_ref, qb_ref, fb_ref, vb_ref, lgf_ref, lgb_ref,
                 nf_ref, mf_ref, nb_ref, mb_ref, of_ref, ob_ref, sf_sc, sb_sc, *, layer):
    @pl.when(pl.program_id(2) == 0)
    def _():
        sf_sc[...] = jnp.zeros_like(sf_sc)
        sb_sc[...] = jnp.zeros_like(sb_sc)

    o, st = _hgrn_block(qf_ref[0], ff_ref[0], vf_ref[0], _hgrn_lb(lgf_ref, layer), sf_sc[...],
                        nf_ref, mf_ref, False)
    of_ref[0] = o
    sf_sc[...] = st
    o, st = _hgrn_block(qb_ref[0], fb_ref[0], vb_ref[0], _hgrn_lb(lgb_ref, layer), sb_sc[...],
                        nb_ref, mb_ref, True)
    ob_ref[0] = o
    sb_sc[...] = st


def _hgrn(z, lb_logits, layer, col0):
    bsz, seq, _ = z.shape
    t = HGRN_BLOCK
    nb = seq // t
    c0 = col0 // LANES
    hw = B_HEADS

    def zspec(group, rev):
        return pl.BlockSpec(
            (1, t, LANES),
            lambda b, h, j: (b, (nb - 1 - j) if rev else j, c0 + group * hw + h))

    def lgspec(direction):
        return pl.BlockSpec((lb_logits.shape[0], 1, 1, LANES),
                            lambda b, h, j: (0, direction * hw + h, 0, 0))

    nf, mf = _hgrn_consts(False)
    nbw, mbw = _hgrn_consts(True)
    lg = lb_logits.astype(F32).reshape(lb_logits.shape[0], 2 * hw, 1, LANES)
    o_shape = jax.ShapeDtypeStruct((bsz, seq, hw * B_DV), F32)
    return pl.pallas_call(
        functools.partial(_hgrn_kernel, layer=layer),
        out_shape=(o_shape, o_shape),
        grid=(bsz, hw, nb),
        in_specs=[zspec(0, False), zspec(1, False), zspec(3, False),
                  zspec(0, True), zspec(2, True), zspec(3, True),
                  lgspec(0), lgspec(1),
                  _const_spec(nf.shape), _const_spec(mf.shape),
                  _const_spec(nbw.shape), _const_spec(mbw.shape)],
        out_specs=(pl.BlockSpec((1, t, LANES), lambda b, h, j: (b, j, h)),
                   pl.BlockSpec((1, t, LANES), lambda b, h, j: (b, nb - 1 - j, h))),
        scratch_shapes=[pltpu.VMEM((B_DV, B_DK), F32), pltpu.VMEM((B_DV, B_DK), F32)],
        compiler_params=_cparams("parallel", "parallel", "arbitrary"),
        name="hgrn",
    )(z, z, z, z, z, z, lg, lg, nf, mf, nbw, mbw)


def _residual_epilogue(y, x_ref, gain_ref, gate_ref, o_ref):
    o_ref[0] = x_ref[0] + gate_ref[0] * _rms(y, gain_ref[...])


def _out0_kernel(a_ref, of_ref, ob_ref, g_ref, on_ref, w_ref, x_ref, gain_ref, gate_ref, o_ref):
    o = of_ref[0] + ob_ref[0]
    g = g_ref[0]
    parts = [_rms(o[:, h * B_DV:(h + 1) * B_DV], on_ref[...]) for h in range(B_HEADS)]
    bn = jnp.concatenate(parts, axis=-1) * (g * _sigmoid(g))
    na = a_ref.shape[-1]
    y = _dot(a_ref[0], w_ref[:na, :]) + _dot(bn, w_ref[na:, :])
    _residual_epilogue(y, x_ref, gain_ref, gate_ref, o_ref)


def _out1_kernel(c_ref, d_ref, w_ref, x_ref, gain_ref, gate_ref, o_ref):
    nc = c_ref.shape[-1]
    y = _dot(c_ref[0], w_ref[:nc, :]) + _dot(d_ref[0], w_ref[nc:, :])
    _residual_epilogue(y, x_ref, gain_ref, gate_ref, o_ref)


def _row_spec(tm, width, col=0):
    return pl.BlockSpec((1, tm, width), lambda b, i: (b, i, col))


def _out0(a_out, o_f, o_b, z, g_col, out_norm, w_bf16, x, gain, gate, tm):
    bsz, seq, d = x.shape
    wv = B_HEADS * B_DV
    vec = pl.BlockSpec((1, 1, d), lambda b, i: (b, 0, 0))
    return pl.pallas_call(
        _out0_kernel,
        out_shape=jax.ShapeDtypeStruct(x.shape, F32),
        grid=(bsz, seq // tm),
        in_specs=[_row_spec(tm, a_out.shape[-1]), _row_spec(tm, wv), _row_spec(tm, wv),
                  _row_spec(tm, wv, g_col // wv), _const_spec((1, B_DV)),
                  _const_spec(w_bf16.shape), _row_spec(tm, d), _const_spec((1, d)), vec],
        out_specs=_row_spec(tm, d),
        compiler_params=_cparams("parallel", "parallel"),
        name="out0",
    )(a_out, o_f, o_b, z, out_norm.reshape(1, B_DV), w_bf16, x, gain.reshape(1, d),
      gate.reshape(bsz, 1, d))


def _out1(c_out, d_out, w_bf16, x, gain, gate, tm):
    bsz, seq, d = x.shape
    vec = pl.BlockSpec((1, 1, d), lambda b, i: (b, 0, 0))
    return pl.pallas_call(
        _out1_kernel,
        out_shape=jax.ShapeDtypeStruct(x.shape, F32),
        grid=(bsz, seq // tm),
        in_specs=[_row_spec(tm, c_out.shape[-1]), _row_spec(tm, d_out.shape[-1]),
                  _const_spec(w_bf16.shape), _row_spec(tm, d), _const_spec((1, d)), vec],
        out_specs=_row_spec(tm, d),
        compiler_params=_cparams("parallel", "parallel"),
        name="out1",
    )(c_out, d_out, w_bf16, x, gain.reshape(1, d), gate.reshape(bsz, 1, d))


def _ffn_kernel(x_ref, g1_ref, sc_ref, sh_ref, wi_ref, wo_ref, g2_ref, gate_ref, o_ref, *, nchunk):
    x = x_ref[0]
    h = (_rms(x, g1_ref[...]) * (1.0 + sc_ref[0]) + sh_ref[0]).astype(BF16)
    hidden = wo_ref.shape[0]
    ck = hidden // nchunk
    y = None
    for c in range(nchunk):
        gt = jnp.dot(h, wi_ref[:, c * ck:(c + 1) * ck], preferred_element_type=F32)
        up = jnp.dot(h, wi_ref[:, hidden + c * ck:hidden + (c + 1) * ck], preferred_element_type=F32)
        part = _dot(gt * _sigmoid(gt) * up, wo_ref[c * ck:(c + 1) * ck, :])
        y = part if y is None else y + part
    o_ref[0] = x + gate_ref[0] * _rms(y, g2_ref[...])


def _ffn(x, g1, sc, sh, wi_bf16, wo_bf16, g2, gate, tm):
    bsz, seq, d = x.shape
    vec = pl.BlockSpec((1, 1, d), lambda b, i: (b, 0, 0))
    hidden = wo_bf16.shape[0]
    nchunk = 2 if (hidden // 2) % LANES == 0 else 1
    return pl.pallas_call(
        functools.partial(_ffn_kernel, nchunk=nchunk),
        out_shape=jax.ShapeDtypeStruct(x.shape, F32),
        grid=(bsz, seq // tm),
        in_specs=[_row_spec(tm, d), _const_spec((1, d)), vec, vec,
                  _const_spec(wi_bf16.shape), _const_spec(wo_bf16.shape),
                  _const_spec((1, d)), vec],
        out_specs=_row_spec(tm, d),
        compiler_params=_cparams("parallel", "parallel"),
        name="ffn",
    )(x, g1.reshape(1, d), sc.reshape(bsz, 1, d), sh.reshape(bsz, 1, d), wi_bf16, wo_bf16,
      g2.reshape(1, d), gate.reshape(bsz, 1, d))


def _fft1_kernel(th_ref, tl_ref, u_ref, p_ref):
    p_ref[0] = _dot_tab(th_ref[...], tl_ref[...], u_ref[0], tab_left=True)


def _fft2_kernel(pr_ref, pi_ref, tc_ref, ts_ref, fh_ref, fl_ref, wh_ref, wl_ref, o_ref, *, scale):
    tc = tc_ref[0]
    ts = ts_ref[0]
    qr, qi = [], []
    for g in range(C_GROUPS):
        cols = slice(g * C_WIDTH, (g + 1) * C_WIDTH)
        pr = pr_ref[0, 0, 0, :, cols]
        pim = pi_ref[0, 0, 0, :, cols]
        qr.append(pr * tc + pim * ts)
        qi.append(pim * tc - pr * ts)
    q = jnp.concatenate([jnp.concatenate(qr, axis=1), jnp.concatenate(qi, axis=1)], axis=0)
    xx = _dot_tab(fh_ref[...], fl_ref[...], q, tab_left=True)
    n2 = xx.shape[0] // 2
    for g in range(C_GROUPS):
        cols = slice(g * C_WIDTH, (g + 1) * C_WIDTH)
        xg = jnp.concatenate([xx[:n2, cols], xx[n2:, cols]], axis=1)
        o_ref[0, :, cols] = (_dot_tab(wh_ref[...], wl_ref[...], xg, tab_left=False)
                             * scale).astype(o_ref.dtype)


def _fourier_mixer(u):
    bsz, seq, cw = u.shape
    n2 = FFT_N2
    n1 = seq // n2
    assert n1 * n2 == seq and cw == C_GROUPS * C_WIDTH
    a1 = 2.0 * np.pi * np.outer(np.arange(n1), np.arange(n1)) / n1
    f1 = np.concatenate([np.cos(a1), -np.sin(a1)], axis=0)
    a2 = 2.0 * np.pi * np.outer(np.arange(n2), np.arange(n2)) / n2
    c2, s2 = np.cos(a2), np.sin(a2)
    f2 = np.block([[c2, s2], [-s2, c2]])
    aw = 2.0 * np.pi * np.outer(np.arange(C_WIDTH), np.arange(C_WIDTH)) / C_WIDTH
    fw = np.concatenate([np.cos(aw), np.sin(aw)], axis=0)
    at = 2.0 * np.pi * np.outer(np.arange(n1), np.arange(n2)) / seq
    tw_c = jnp.asarray(np.repeat(np.cos(at)[:, :, None], C_WIDTH, axis=2), F32)
    tw_s = jnp.asarray(np.repeat(np.sin(at)[:, :, None], C_WIDTH, axis=2), F32)
    f1h, f1l = _np_split2(f1)
    f2h, f2l = _np_split2(f2)
    fwh, fwl = _np_split2(fw)

    ncol = n2 * cw
    tn = 8192
    p = pl.pallas_call(
        _fft1_kernel,
        out_shape=jax.ShapeDtypeStruct((bsz, 2 * n1, ncol), F32),
        grid=(bsz, ncol // tn),
        in_specs=[_const_spec(f1h.shape), _const_spec(f1l.shape),
                  pl.BlockSpec((1, n1, tn), lambda b, j: (b, 0, j))],
        out_specs=pl.BlockSpec((1, 2 * n1, tn), lambda b, j: (b, 0, j)),
        compiler_params=_cparams("parallel", "parallel"),
        name="fft1",
    )(f1h, f1l, u.reshape(bsz, n1, ncol))
    p5 = p.reshape(bsz, 2, n1, n2, cw)
    tw_spec = pl.BlockSpec((1, n2, C_WIDTH), lambda b, k: (k, 0, 0))
    y = pl.pallas_call(
        functools.partial(_fft2_kernel, scale=float(1.0 / np.sqrt(seq * C_WIDTH))),
        out_shape=jax.ShapeDtypeStruct((bsz, n2, n1 * cw), BF16),
        grid=(bsz, n1),
        in_specs=[pl.BlockSpec((1, 1, 1, n2, cw), lambda b, k: (b, 0, k, 0, 0)),
                  pl.BlockSpec((1, 1, 1, n2, cw), lambda b, k: (b, 1, k, 0, 0)),
                  tw_spec, tw_spec,
                  _const_spec(f2h.shape), _const_spec(f2l.shape),
                  _const_spec(fwh.shape), _const_spec(fwl.shape)],
        out_specs=pl.BlockSpec((1, n2, cw), lambda b, k: (b, 0, k)),
        compiler_params=_cparams("parallel", "parallel"),
        name="fft2",
    )(p5, p5, tw_c, tw_s, f2h, f2l, fwh, fwl)
    return y.reshape(bsz, seq, cw)


def _head_perm():
    rep = D_Q_HEADS // D_KV_HEADS
    cols = []
    for j in range(rep):
        for g in range(D_KV_HEADS):
            h = g * rep + j
            cols.extend(range(h * D_HEAD_DIM, (h + 1) * D_HEAD_DIM))
    return np.asarray(cols, np.int32)


def _rope_tables(seq):
    rows = seq // GRID_W
    row = jnp.repeat(jnp.arange(rows, dtype=F32), GRID_W)
    col = jnp.tile(jnp.arange(GRID_W, dtype=F32), rows)
    axis_dim = D_HEAD_DIM // 2
    inv_freq = jnp.power(ROPE_THETA, -jnp.arange(0, axis_dim, 2, dtype=F32) / axis_dim)
    ang_r = row[:, None] * inv_freq[None, :]
    ang_c = col[:, None] * inv_freq[None, :]
    cr, sr, cc, sc = jnp.cos(ang_r), jnp.sin(ang_r), jnp.cos(ang_c), jnp.sin(ang_c)
    cos = jnp.concatenate([cr, cr, cc, cc], axis=1)
    sin = jnp.concatenate([-sr, sr, -sc, sc], axis=1)
    reps = LANES // D_HEAD_DIM
    return jnp.tile(cos, (1, reps)), jnp.tile(sin, (1, reps))


def _qkprep_kernel(q_ref, k_ref, v_ref, cos_ref, sin_ref, bd_h_ref, bd_l_ref, gq_ref, gk_ref,
                   qo_ref, ko_ref, vo_ref):
    cos = cos_ref[...]
    sin = sin_ref[...]
    quarter = D_HEAD_DIM // 4
    lane = lax.broadcasted_iota(jnp.int32, (1, LANES), 1)
    first_of_pair = (lane // quarter) % 2 == 0

    def norm_rope(x, gain, scale):
        ms = _dot_tab(bd_h_ref[...], bd_l_ref[...], x * x, tab_left=False)
        xn = x * lax.rsqrt(ms + EPS) * gain
        partner = jnp.where(first_of_pair, pltpu.roll(xn, LANES - quarter, 1),
                            pltpu.roll(xn, quarter, 1))
        return ((xn * cos + partner * sin) * scale).astype(BF16)

    for j in range(q_ref.shape[-1] // LANES):
        cols = slice(j * LANES, (j + 1) * LANES)
        qo_ref[0, :, cols] = norm_rope(q_ref[0, :, cols], gq_ref[...], D_HEAD_DIM ** -0.5 * LOG2E)
    ko_ref[0] = norm_rope(k_ref[0], gk_ref[...], 1.0)
    vt = v_ref[0].T
    ones = jnp.ones((FLASH_ONES, vt.shape[1]), F32)
    vo_ref[0] = jnp.concatenate(
        [piece for g in range(D_KV_HEADS)
         for piece in (vt[g * D_HEAD_DIM:(g + 1) * D_HEAD_DIM], ones)], axis=0).astype(BF16)


def _qkprep(qkv, qk_norm_j, tm):
    bsz, seq, _ = qkv.shape
    qw = D_Q_HEADS * D_HEAD_DIM
    kw = D_KV_HEADS * D_HEAD_DIM
    assert kw == LANES
    vrows = D_KV_HEADS * (D_HEAD_DIM + FLASH_ONES)
    cos, sin = _rope_tables(seq)
    bd = np.kron(np.eye(LANES // D_HEAD_DIM), np.full((D_HEAD_DIM, D_HEAD_DIM), 1.0 / D_HEAD_DIM))
    bd_h, bd_l = _np_split2(bd)
    reps = LANES // D_HEAD_DIM
    gq = jnp.tile(qk_norm_j[0].astype(F32), reps).reshape(1, LANES)
    gk = jnp.tile(qk_norm_j[1].astype(F32), reps).reshape(1, LANES)
    tab = pl.BlockSpec((tm, LANES), lambda b, i: (i, 0))
    return pl.pallas_call(
        _qkprep_kernel,
        out_shape=(jax.ShapeDtypeStruct((bsz, seq, qw), BF16),
                   jax.ShapeDtypeStruct((bsz, seq, kw), BF16),
                   jax.ShapeDtypeStruct((bsz, vrows, seq), BF16)),
        grid=(bsz, seq // tm),
        in_specs=[_row_spec(tm, qw, 0), _row_spec(tm, kw, qw // kw), _row_spec(tm, kw, qw // kw + 1),
                  tab, tab, _const_spec(bd_h.shape), _const_spec(bd_l.shape),
                  _const_spec((1, LANES)), _const_spec((1, LANES))],
        out_specs=(_row_spec(tm, qw), _row_spec(tm, kw),
                   pl.BlockSpec((1, vrows, tm), lambda b, i: (b, 0, i))),
        compiler_params=_cparams("parallel", "parallel"),
        name="qkprep",
    )(qkv, qkv, qkv, cos, sin, bd_h, bd_l, gq, gk)


def _flash_kernel(q_ref, k_ref, vt_ref, o_ref, m_sc, acc_sc, s_sc):
    kv = pl.program_id(2)

    @pl.when(kv == 0)
    def _():
        m_sc[...] = jnp.full_like(m_sc, -jnp.inf)
        acc_sc[...] = jnp.zeros_like(acc_sc)

    lane = lax.broadcasted_iota(jnp.int32, (1, LANES), 1)
    lo = lane < D_HEAD_DIM
    nblk = q_ref.shape[-1] // LANES
    tq, tk = q_ref.shape[1], k_ref.shape[1]
    ku, qu = FLASH_KEY_UNIT, FLASH_QUERY_UNIT
    grows = D_HEAD_DIM + FLASH_ONES
    nheads = nblk * D_KV_HEADS
    k = k_ref[0]

    def logits_pass(idx):
        j, g = divmod(idx, D_KV_HEADS)
        qj = q_ref[0, :, j * LANES:(j + 1) * LANES]
        sel = lo if g == 0 else jnp.logical_not(lo)
        qm = jnp.where(sel, qj, jnp.zeros_like(qj))
        mcs = []
        for c in range(tq // qu):
            s = lax.dot_general(k, qm[c * qu:(c + 1) * qu], (((1,), (1,)), ((), ())),
                                preferred_element_type=F32)
            s_sc[idx, :, c * qu:(c + 1) * qu] = s
            mcs.append(jnp.max(s, axis=0, keepdims=True))
        m_prev = m_sc[idx, 0:1, :]
        m_new = jnp.maximum(m_prev, jnp.concatenate(mcs, axis=1))
        m_sc[idx, 0:1, :] = m_new
        return m_new, jnp.exp2(m_prev - m_new)

    def value_pass(idx, m_new, alpha):
        j, g = divmod(idx, D_KV_HEADS)
        rows = slice(g * grows, (g + 1) * grows)
        for c in range(tq // qu):
            qcols = slice(c * qu, (c + 1) * qu)
            pv = None
            for u in range(tk // ku):
                keys = slice(u * ku, (u + 1) * ku)
                p = jnp.exp2(s_sc[idx, keys, qcols] - m_new[:, qcols])
                d = jnp.dot(vt_ref[0, rows, keys], p.astype(BF16), preferred_element_type=F32)
                pv = d if pv is None else pv + d
            acc_sc[j, rows, qcols] = alpha[:, qcols] * acc_sc[j, rows, qcols] + pv

    stats = logits_pass(0)
    for idx in range(nheads):
        nxt = logits_pass(idx + 1) if idx + 1 < nheads else None
        value_pass(idx, *stats)
        stats = nxt

    @pl.when(kv == pl.num_programs(2) - 1)
    def _():
        for j in range(nblk):
            parts = []
            for g in range(D_KV_HEADS):
                num = acc_sc[j, g * grows:g * grows + D_HEAD_DIM, :]
                den = acc_sc[j, g * grows + D_HEAD_DIM:g * grows + D_HEAD_DIM + 1, :]
                parts.append(num / den)
            o_ref[0, :, j * LANES:(j + 1) * LANES] = (
                jnp.concatenate(parts, axis=0).T.astype(o_ref.dtype))


def _flash(q, k, vt, tq, tk):
    bsz, seq, qw = q.shape
    kw = k.shape[-1]
    vrows = vt.shape[1]
    return pl.pallas_call(
        _flash_kernel,
        out_shape=jax.ShapeDtypeStruct((bsz, seq, qw), BF16),
        grid=(bsz, seq // tq, seq // tk),
        in_specs=[pl.BlockSpec((1, tq, qw), lambda b, i, j: (b, i, 0)),
                  pl.BlockSpec((1, tk, kw), lambda b, i, j: (b, j, 0)),
                  pl.BlockSpec((1, vrows, tk), lambda b, i, j: (b, 0, j))],
        out_specs=pl.BlockSpec((1, tq, qw), lambda b, i, j: (b, i, 0)),
        scratch_shapes=[pltpu.VMEM((D_Q_HEADS, 8, tq), F32),
                        pltpu.VMEM((qw // LANES, vrows, tq), F32),
                        pltpu.VMEM((D_Q_HEADS, tk, tq), F32)],
        compiler_params=_cparams("parallel", "parallel", "arbitrary"),
        name="flash",
    )(q, k, vt)


def kernel(x, c, t5_bias, hgrn_lb_logits, ada_w, ada_b, norm_gains, ab_w_in, ab_w_out,
           hgrn_out_norm, cd_w_in, cd_w_out, qk_norm, ffn_w_in, ffn_w_out):
    bsz, seq, d = x.shape
    depth = ada_w.shape[0]
    mod = _ada_mod(c.astype(F32), ada_w, ada_b)
    perm = _head_perm()
    aw = A_HEADS * A_HEAD_DIM
    cw = C_GROUPS * C_WIDTH
    qw = D_Q_HEADS * D_HEAD_DIM
    tm_in = min(512, seq)
    tm = min(512, seq)
    for layer in range(depth):
        sh_m, sc_m, g_m, sh_f, sc_f, g_f = [mod[layer, :, i * d:(i + 1) * d] for i in range(6)]
        gains = norm_gains[layer]
        j = layer // 2
        if layer % 2 == 0:
            w_in = ab_w_in[j].astype(BF16)
            *qkv_cm, z = _inproj_cm(x, gains[0], sc_m, sh_m, w_in, 3 * aw, tm_in)
            branches = [_dilated_branch(cm, t5_bias, window, dil)
                        for cm, (window, dil) in zip(qkv_cm, DIL_CFG)]
            a_out = _dilated_merge(branches, tm)
            o_f, o_b = _hgrn(z, hgrn_lb_logits, layer, 0)
            g_col = 3 * B_HEADS * B_DK + B_HEADS * B_DV
            x = _out0(a_out, o_f, o_b, z, g_col, hgrn_out_norm[j], ab_w_out[j].astype(BF16),
                      x, gains[1], g_m, tm)
        else:
            w_full = cd_w_in[j]
            w_in = jnp.concatenate([w_full[:, :cw], w_full[:, cw:cw + qw][:, perm],
                                    w_full[:, cw + qw:]], axis=1).astype(BF16)
            u, qkv = _inproj(x, gains[0], sc_m, sh_m, w_in, (cw, w_in.shape[1] - cw), tm)
            c_out = _fourier_mixer(u)
            qn, kn, vn = _qkprep(qkv, qk_norm[j], tm)
            d_out = _flash(qn, kn, vn, min(FLASH_TQ, seq), min(FLASH_TK, seq))
            w_out_full = cd_w_out[j]
            w_out = jnp.concatenate([w_out_full[:cw], w_out_full[cw:][perm]], axis=0).astype(BF16)
            x = _out1(c_out, d_out, w_out, x, gains[1], g_m, tm)
        x = _ffn(x, gains[2], sc_f, sh_f, ffn_w_in[layer].astype(BF16), ffn_w_out[layer].astype(BF16),
                 gains[3], g_f, tm)
    return x
```

```python
import functools

import numpy as np
import jax
import jax.numpy as jnp
from jax import lax
from jax.experimental import pallas as pl
from jax.experimental.pallas import tpu as pltpu

F32 = jnp.float32
BF16 = jnp.bfloat16
LANES = 128
VMEM_LIMIT_BYTES = 56 * 2**20
NEG_INF = -1e30
EPS = 1e-6

GRID_W = 64
A_HEADS = 8
A_HEAD_DIM = 64
DIL_CFG = ((128, 1), (512, 4), (2048, 16))
N_BUCKETS = 32
T5_MAX_DIST = 1024
B_HEADS = 4
B_DK = 128
B_DV = 128
C_GROUPS = 4
C_WIDTH = 128
D_Q_HEADS = 8
D_KV_HEADS = 2
D_HEAD_DIM = 64
ROPE_THETA = 10000.0

DIL_TQ = 128
DIL_TILE = 256
HGRN_BLOCK = 256
FLASH_TQ = 1024
FLASH_TK = 512
FLASH_KEY_UNIT = 128
FLASH_QUERY_UNIT = 256
FLASH_ONES = 16
FFT_N2 = 128
FFT_BATCH = 8
FFT_PAD = 8
LOG2E = 1.4426950408889634


def _cparams(*sem):
    return pltpu.CompilerParams(dimension_semantics=sem, vmem_limit_bytes=VMEM_LIMIT_BYTES)


def _const_spec(shape):
    nd = len(shape)
    return pl.BlockSpec(shape, lambda *_: (0,) * nd, pipeline_mode=pl.Buffered(1))


def _sigmoid(x):
    return 1.0 / (1.0 + jnp.exp(-x))


def _dot(a, b):
    return jnp.dot(a.astype(BF16), b.astype(BF16), preferred_element_type=F32)


def _dot_nt(a, b):
    return lax.dot_general(a.astype(BF16), b.astype(BF16), (((1,), (1,)), ((), ())),
                           preferred_element_type=F32)


def _split2(a):
    hi = a.astype(BF16)
    lo = (a - hi.astype(F32)).astype(BF16)
    return hi, lo


def _split3(a):
    a1 = a.astype(BF16)
    r = a - a1.astype(F32)
    a2 = r.astype(BF16)
    a3 = (r - a2.astype(F32)).astype(BF16)
    return a1, a2, a3


def _dot_tab(tab_hi, tab_lo, x, *, tab_left):
    x_hi, x_lo = _split2(x)
    if tab_left:
        d = lambda t, v: jnp.dot(t, v, preferred_element_type=F32)
    else:
        d = lambda t, v: jnp.dot(v, t, preferred_element_type=F32)
    return d(tab_hi, x_hi) + (d(tab_hi, x_lo) + d(tab_lo, x_hi))


def _rms(x, gain):
    ms = jnp.mean(x * x, axis=-1, keepdims=True)
    return x * lax.rsqrt(ms + EPS) * gain


def _np_split2(t):
    t = np.asarray(t, np.float32)
    hi = jnp.asarray(t, F32).astype(BF16)
    lo = (jnp.asarray(t, F32) - hi.astype(F32)).astype(BF16)
    return hi, lo


def _mod_kernel(c_ref, w_ref, b_ref, o_ref):
    c = c_ref[...]
    o_ref[0] = _dot(c * _sigmoid(c), w_ref[0]) + b_ref[0]


def _ada_mod(c, ada_w, ada_b):
    depth, d, n6 = ada_w.shape
    bsz = c.shape[0]
    rows = 8
    cp = jnp.zeros((rows, d), F32).at[:bsz].set(c)
    tn = n6 // 4
    out = pl.pallas_call(
        _mod_kernel,
        out_shape=jax.ShapeDtypeStruct((depth, rows, n6), F32),
        grid=(depth, n6 // tn),
        in_specs=[pl.BlockSpec((rows, d), lambda l, j: (0, 0)),
                  pl.BlockSpec((1, d, tn), lambda l, j: (l, 0, j)),
                  pl.BlockSpec((1, 1, tn), lambda l, j: (l, 0, j))],
        out_specs=pl.BlockSpec((1, rows, tn), lambda l, j: (l, 0, j)),
        compiler_params=_cparams("parallel", "parallel"),
        name="ada_mod",
    )(cp, ada_w, ada_b.reshape(depth, 1, n6))
    return out[:, :bsz]


def _inproj_kernel(x_ref, gain_ref, sc_ref, sh_ref, w_ref, *o_refs):
    h = _rms(x_ref[0], gain_ref[...]) * (1.0 + sc_ref[0]) + sh_ref[0]
    z = _dot(h, w_ref[...])
    off = 0
    for o_ref in o_refs:
        if len(o_ref.shape) == 4:
            for g in range(o_ref.shape[1]):
                o_ref[0, g] = z[:, off:off + LANES]
                off += LANES
        else:
            n = o_ref.shape[-1]
            o_ref[0] = z[:, off:off + n].astype(o_ref.dtype)
            off += n


def _inproj(x, gain, sc, sh, w_bf16, splits, tm):
    bsz, seq, d = x.shape
    n = w_bf16.shape[1]
    assert sum(splits) == n
    vec = pl.BlockSpec((1, 1, d), lambda b, i: (b, 0, 0))
    g0 = splits[0] // LANES
    return pl.pallas_call(
        _inproj_kernel,
        out_shape=tuple([jax.ShapeDtypeStruct((bsz, g0, seq, LANES), F32)]
                        + [jax.ShapeDtypeStruct((bsz, seq, s), F32) for s in splits[1:]]),
        grid=(bsz, seq // tm),
        in_specs=[pl.BlockSpec((1, tm, d), lambda b, i: (b, i, 0)),
                  _const_spec((1, d)), vec, vec, _const_spec((d, n))],
        out_specs=tuple([pl.BlockSpec((1, g0, tm, LANES), lambda b, i: (b, 0, i, 0))]
                        + [pl.BlockSpec((1, tm, s), lambda b, i: (b, i, 0)) for s in splits[1:]]),
        compiler_params=_cparams("parallel", "parallel"),
        name="inproj",
    )(x, gain.reshape(1, d), sc.reshape(bsz, 1, d), sh.reshape(bsz, 1, d), w_bf16)


def _inproj_cm_kernel(x_ref, gain_ref, sc_ref, sh_ref, w_ref, *refs):
    cm_refs, rest_ref, zs_sc = refs[:-2], refs[-2], refs[-1]
    h = _rms(x_ref[0], gain_ref[...]) * (1.0 + sc_ref[0]) + sh_ref[0]
    z = _dot(h, w_ref[...])
    nblk, tm, _ = zs_sc.shape
    rest_ref[0] = z[:, nblk * LANES:]
    for c in range(nblk):
        zs_sc[c] = z[:, c * LANES:(c + 1) * LANES]
    for cm_ref, (_, dil) in zip(cm_refs, DIL_CFG):
        for r in range(dil):
            for c in range(nblk):
                cm_ref[0, r, :, c * LANES:(c + 1) * LANES] = (
                    zs_sc[c, pl.ds(r, tm // dil, stride=dil), :].astype(BF16))


def _inproj_cm(x, gain, sc, sh, w_bf16, na, tm):
    bsz, seq, d = x.shape
    n = w_bf16.shape[1]
    vec = pl.BlockSpec((1, 1, d), lambda b, i: (b, 0, 0))
    dils = [dl for _, dl in DIL_CFG]
    return pl.pallas_call(
        _inproj_cm_kernel,
        out_shape=tuple([jax.ShapeDtypeStruct((bsz, dl, seq // dl, na), BF16) for dl in dils]
                        + [jax.ShapeDtypeStruct((bsz, seq, n - na), F32)]),
        grid=(bsz, seq // tm),
        in_specs=[pl.BlockSpec((1, tm, d), lambda b, i: (b, i, 0)),
                  _const_spec((1, d)), vec, vec, _const_spec((d, n))],
        out_specs=tuple([pl.BlockSpec((1, dl, tm // dl, na), lambda b, i: (b, 0, i, 0)) for dl in dils]
                        + [pl.BlockSpec((1, tm, n - na), lambda b, i: (b, i, 0))]),
        scratch_shapes=[pltpu.VMEM((na // LANES, tm, LANES), F32)],
        compiler_params=_cparams("parallel", "parallel"),
        name="inproj_cm",
    )(x, gain.reshape(1, d), sc.reshape(bsz, 1, d), sh.reshape(bsz, 1, d), w_bf16)


def _t5_buckets(rel):
    half = N_BUCKETS // 2
    max_exact = half // 2
    n = np.abs(rel)
    large = max_exact + (np.log(np.maximum(n, 1) / max_exact) / np.log(T5_MAX_DIST / max_exact)
                         * (half - max_exact)).astype(np.int32)
    large = np.minimum(large, half - 1)
    return (np.where(rel > 0, half, 0) + np.where(n < max_exact, n, large)).astype(np.int32)


def _dil_bias(t5_bias, window, dil, tq):
    half = (window // 2) // dil
    assert half == tq // 2
    rel = np.arange(2 * tq)[None, :] - half - np.arange(tq)[:, None]
    inside = np.abs(rel) <= half
    buckets = _t5_buckets(np.where(inside, rel, 0) * dil)
    onehot =jnp.asarray(np.eye(N_BUCKETS, dtype=np.float32)[buckets])
    bias = jnp.einsum("qkn,nh->hqk", onehot, t5_bias.astype(F32), precision=lax.Precision.HIGHEST)
    return jnp.where(jnp.asarray(inside)[None], bias, NEG_INF)


def _dil_kernel(q_ref, kp_ref, kc_ref, kn_ref, vp_ref, vc_ref, vn_ref, bias_ref, o_ref, lse_ref,
                *, class_len):
    i = pl.program_id(2)
    sub, hq, tile = DIL_TQ, DIL_TQ // 2, DIL_TILE
    kwin = jnp.concatenate([kp_ref[0, 0], kc_ref[0, 0], kn_ref[0, 0]], axis=0)
    vwin = jnp.concatenate([vp_ref[0, 0], vc_ref[0, 0], vn_ref[0, 0]], axis=0)
    lane = lax.broadcasted_iota(jnp.int32, (1, LANES), 1)
    lo = lane < A_HEAD_DIM
    scale = jnp.asarray(A_HEAD_DIM ** -0.5, BF16)
    for jt in range(tile // sub):
        qrows = slice(jt * sub, (jt + 1) * sub)
        krows = slice(jt * sub, jt * sub + 2 * sub)
        kpos = i * tile + jt * sub - hq + lax.broadcasted_iota(jnp.int32, (1, 2 * sub), 1)
        valid = jnp.logical_and(kpos >= 0, kpos < class_len)
        lse_all = jnp.zeros((sub, LANES), F32)
        for j in range(A_HEADS // 2):
            cols = slice(j * LANES, (j + 1) * LANES)
            qj = q_ref[0, 0, qrows, cols] * scale
            kj = kwin[krows, cols]
            vj = vwin[krows, cols]
            halves = []
            for half in range(2):
                h = 2 * j + half
                sel = lo if half == 0 else jnp.logical_not(lo)
                s = lax.dot_general(jnp.where(sel, qj, jnp.zeros_like(qj)), kj,
                                    (((1,), (1,)), ((), ())), preferred_element_type=F32)
                s = jnp.where(valid, s + bias_ref[h], NEG_INF)
                m = jnp.max(s, axis=-1, keepdims=True)
                p = jnp.exp(s - m)
                l = jnp.sum(p, axis=-1, keepdims=True)
                halves.append(jnp.dot(p.astype(BF16), vj, preferred_element_type=F32) / l)
                lse_all = jnp.where(lane == h, m + jnp.log(l), lse_all)
            o_ref[0, 0, qrows, cols] = jnp.where(lo, halves[0], halves[1])
        lse_ref[0, 0, qrows, :] = lse_all


def _dilated_branch(qkv_cm, t5_bias, window, dil):
    bsz, _, cl, width = qkv_cm.shape
    aw = A_HEADS * A_HEAD_DIM
    tile, hq = DIL_TILE, DIL_TQ // 2
    nt = cl // tile
    per = tile // hq
    nh = cl // hq

    def cur(col):
        return pl.BlockSpec((1, 1, tile, aw), lambda b, r, i: (b, r, i, col))

    def prev(col):
        return pl.BlockSpec((1, 1, hq, aw), lambda b, r, i: (b, r, jnp.maximum(i * per - 1, 0), col))

    def nxt(col):
        return pl.BlockSpec((1, 1, hq, aw),
                            lambda b, r, i: (b, r, jnp.minimum((i + 1) * per, nh - 1), col))

    return pl.pallas_call(
        functools.partial(_dil_kernel, class_len=cl),
        out_shape=(jax.ShapeDtypeStruct((bsz, dil, cl, aw), F32),
                   jax.ShapeDtypeStruct((bsz, dil, cl, LANES), F32)),
        grid=(bsz, dil, nt),
        in_specs=[cur(0), prev(1), cur(1), nxt(1), prev(2), cur(2), nxt(2),
                  _const_spec((A_HEADS, DIL_TQ, 2 * DIL_TQ))],
        out_specs=(pl.BlockSpec((1, 1, tile, aw), lambda b, r, i: (b, r, i, 0)),
                   pl.BlockSpec((1, 1, tile, LANES), lambda b, r, i: (b, r, i, 0))),
        compiler_params=_cparams("parallel", "parallel", "parallel"),
        name=f"dilated_d{dil}",
    )(*([qkv_cm] * 7), _dil_bias(t5_bias, window, dil, DIL_TQ))


def _dilmerge_kernel(*refs):
    nbr = len(DIL_CFG)
    a_refs, l_refs = refs[:nbr], refs[nbr:2 * nbr]
    e_ref, o_ref = refs[2 * nbr], refs[2 * nbr + 1]
    a_scs, l_scs = refs[2 * nbr + 2:3 * nbr + 2], refs[3 * nbr + 2:4 * nbr + 2]
    tm = o_ref.shape[1]
    accs, lses = [], []
    for (_, dil), a_ref, l_ref, a_sc, l_sc in zip(DIL_CFG, a_refs, l_refs, a_scs, l_scs):
        nblk = a_sc.shape[0]
        for r in range(dil):
            rows = pl.ds(r, tm // dil, stride=dil)
            for c in range(nblk):
                a_sc[c, rows, :] = a_ref[0, r, :, c * LANES:(c + 1) * LANES]
            l_sc[rows, :] = l_ref[0, r]
        accs.append(jnp.concatenate([a_sc[c] for c in range(nblk)], axis=1))
        lses.append(l_sc[...])
    mx = functools.reduce(jnp.maximum, lses)
    ws = [jnp.exp(x - mx) for x in lses]
    tot = functools.reduce(jnp.add, ws)
    out = None
    for w, a in zip(ws, accs):
        w_hi, w_lo = _split2(w / tot)
        wide = (jnp.dot(w_hi, e_ref[...], preferred_element_type=F32)
                + jnp.dot(w_lo, e_ref[...], preferred_element_type=F32))
        out = wide * a if out is None else out + wide * a
    o_ref[0] = out.astype(o_ref.dtype)


def _dilated_merge(branch_outs, tm):
    bsz, _, _, aw = branch_outs[0][0].shape
    seq = branch_outs[0][0].shape[1] * branch_outs[0][0].shape[2]
    expand = np.zeros((LANES, aw), np.float32)
    for h in range(A_HEADS):
        expand[h, h * A_HEAD_DIM:(h + 1) * A_HEAD_DIM] = 1.0
    dils = [d for _, d in DIL_CFG]
    in_specs = ([pl.BlockSpec((1, d, tm // d, aw), lambda b, i: (b, 0, i, 0)) for d in dils]
                + [pl.BlockSpec((1, d, tm // d, LANES), lambda b, i: (b, 0, i, 0)) for d in dils]
                + [_const_spec(expand.shape)])
    return pl.pallas_call(
        _dilmerge_kernel,
        out_shape=jax.ShapeDtypeStruct((bsz, seq, aw), BF16),
        grid=(bsz, seq // tm),
        in_specs=in_specs,
        out_specs=pl.BlockSpec((1, tm, aw), lambda b, i: (b, i, 0)),
        scratch_shapes=([pltpu.VMEM((aw // LANES, tm, LANES), F32) for _ in dils]
                        + [pltpu.VMEM((tm, LANES), F32) for _ in dils]),
        compiler_params=_cparams("parallel", "parallel"),
        name="dilmerge",
    )(*[o for o, _ in branch_outs], *[l for _, l in branch_outs], jnp.asarray(expand, BF16))


def _hgrn_consts(reverse):
    t = HGRN_BLOCK
    r = np.arange(t)
    u = r[None, :]
    row = r[:, None]
    nmats = [(u >= row) if reverse else (u <= row)]
    masks = []
    m = t // 2
    while m >= 1:
        grp = r // (2 * m)
        in_first = (r % (2 * m)) < m
        same = grp[:, None] == grp[None, :]
        if reverse:
            beta = (grp * 2 * m + m)[:, None]
            n = np.where(in_first[:, None], (u >= row) & (u < beta), (u >= beta) & (u < row))
            mask = same & in_first[:, None] & ~in_first[None, :]
        else:
            beta = (grp * 2 * m + m - 1)[:, None]
            n = np.where(in_first[:, None], (u > row) & (u <= beta), (u > beta) & (u <= row))
            mask = same & ~in_first[:, None] & in_first[None, :]
        nmats.append(n)
        masks.append(mask)
        m //= 2
    masks.append(np.eye(t, dtype=bool))
    nmat = jnp.asarray(np.concatenate(nmats, axis=0), F32).astype(BF16)
    return nmat, jnp.asarray(np.stack(masks), F32)


def _hgrn_block(q, f, v, lb, st, nmat_ref, mask_ref, reverse):
    t = HGRN_BLOCK
    nlev = mask_ref.shape[0] - 1
    qs = q * (B_DK ** -0.5)
    fa = lb + (1.0 - lb) * _sigmoid(f)
    kk = 1.0 - fa
    g_hi, g_lo = _split2(jnp.log(fa))
    ex = jnp.dot(nmat_ref[...], jnp.concatenate([g_hi, g_lo], axis=1), preferred_element_type=F32)
    ex = ex[:, :B_DK] + ex[:, B_DK:]
    b = ex[:t]
    btot = b[0:1] if reverse else b[t - 1:t]
    a = mask_ref[nlev] * _dot_nt(qs, kk)
    for l in range(nlev):
        e = jnp.exp(ex[(l + 1) * t:(l + 2) * t])
        a = a + mask_ref[l] * _dot_nt(qs * e, kk * e)
    out = _dot(a, v) + _dot_nt(qs * jnp.exp(b), st)
    khat = (kk * jnp.exp(btot - b)).astype(BF16)
    st_new = st * jnp.exp(btot) + jnp.dot(v.T.astype(BF16), khat, preferred_element_type=F32)
    return out, st_new


def _hgrn_lb(lg_ref, layer):
    lg = [lg_ref[l, 0] for l in range(lg_ref.shape[0])]
    mx = functools.reduce(jnp.maximum, lg)
    e = [jnp.exp(x - mx) for x in lg]
    return functools.reduce(jnp.add, e[:layer + 1]) / functools.reduce(jnp.add, e)


def _hgrn_kernel(qf_ref, ff_ref, vf_ref, qb_ref, fb_ref, vb_ref, lgf_ref, lgb_ref,
                 nf_ref, mf_ref, nb_ref, mb_ref, of_ref, ob_ref, sf_sc, sb_sc, *, layer):
    @pl.when(pl.program_id(2) == 0)
    def _():
        sf_sc[...] = jnp.zeros_like(sf_sc)
        sb_sc[...] = jnp.zeros_like(sb_sc)

    o, st = _hgrn_block(qf_ref[0], ff_ref[0], vf_ref[0], _hgrn_lb(lgf_ref, layer), sf_sc[...],
                        nf_ref, mf_ref, False)
    of_ref[0] = o
    sf_sc[...] = st
    o, st = _hgrn_block(qb_ref[0], fb_ref[0], vb_ref[0], _hgrn_lb(lgb_ref, layer), sb_sc[...],
                        nb_ref, mb_ref, True)
    ob_ref[0] = o
    sb_sc[...] = st


def _hgrn(z, lb_logits, layer, col0):
    bsz, seq, _ = z.shape
    t = HGRN_BLOCK
    nb = seq // t
    c0 = col0 // LANES
    hw = B_HEADS

    def zspec(group, rev):
        return pl.BlockSpec(
            (1, t, LANES),
            lambda b, h, j: (b, (nb - 1 - j) if rev else j, c0 + group * hw + h))

    def lgspec(direction):
        return pl.BlockSpec((lb_logits.shape[0], 1, 1, LANES),
                            lambda b, h, j: (0, direction * hw + h, 0, 0))

    nf, mf = _hgrn_consts(False)
    nbw, mbw = _hgrn_consts(True)
    lg = lb_logits.astype(F32).reshape(lb_logits.shape[0], 2 * hw, 1, LANES)
    o_shape = jax.ShapeDtypeStruct((bsz, seq, hw * B_DV), F32)
    return pl.pallas_call(
        functools.partial(_hgrn_kernel, layer=layer),
        out_shape=(o_shape, o_shape),
        grid=(bsz, hw, nb),
        in_specs=[zspec(0, False), zspec(1, False), zspec(3, False),
                  zspec(0, True), zspec(2, True), zspec(3, True),
                  lgspec(0), lgspec(1),
                  _const_spec(nf.shape), _const_spec(mf.shape),
                  _const_spec(nbw.shape), _const_spec(mbw.shape)],
        out_specs=(pl.BlockSpec((1, t, LANES), lambda b, h, j: (b, j, h)),
                   pl.BlockSpec((1, t, LANES), lambda b, h, j: (b, nb - 1 - j, h))),
        scratch_shapes=[pltpu.VMEM((B_DV, B_DK), F32), pltpu.VMEM((B_DV, B_DK), F32)],
        compiler_params=_cparams("parallel", "parallel", "arbitrary"),
        name="hgrn",
    )(z, z, z, z, z, z, lg, lg, nf, mf, nbw, mbw)


def _residual_epilogue(y, x_ref, gain_ref, gate_ref, o_ref):
    o_ref[0] = x_ref[0] + gate_ref[0] * _rms(y, gain_ref[...])


def _out0_kernel(a_ref, of_ref, ob_ref, g_ref, on_ref, w_ref, x_ref, gain_ref, gate_ref, o_ref):
    o = of_ref[0] + ob_ref[0]
    g = g_ref[0]
    parts = [_rms(o[:, h * B_DV:(h + 1) * B_DV], on_ref[...]) for h in range(B_HEADS)]
    bn = jnp.concatenate(parts, axis=-1) * (g * _sigmoid(g))
    na = a_ref.shape[-1]
    y = _dot(a_ref[0], w_ref[:na, :]) + _dot(bn, w_ref[na:, :])
    _residual_epilogue(y, x_ref, gain_ref, gate_ref, o_ref)


def _out1_kernel(c_ref, d_ref, w_ref, x_ref, gain_ref, gate_ref, o_ref):
    c = jnp.concatenate([c_ref[0, g] for g in range(c_ref.shape[1])], axis=1)
    nc = c.shape[-1]
    y = _dot(c, w_ref[:nc, :]) + _dot(d_ref[0], w_ref[nc:, :])
    _residual_epilogue(y, x_ref, gain_ref, gate_ref, o_ref)


def _row_spec(tm, width, col=0):
    return pl.BlockSpec((1, tm, width), lambda b, i: (b, i, col))


def _out0(a_out, o_f, o_b, z, g_col, out_norm, w_bf16, x, gain, gate, tm):
    bsz, seq, d = x.shape
    wv = B_HEADS * B_DV
    vec = pl.BlockSpec((1, 1, d), lambda b, i: (b, 0, 0))
    return pl.pallas_call(
        _out0_kernel,
        out_shape=jax.ShapeDtypeStruct(x.shape, F32),
        grid=(bsz, seq // tm),
        in_specs=[_row_spec(tm, a_out.shape[-1]), _row_spec(tm, wv), _row_spec(tm, wv),
                  _row_spec(tm, wv, g_col // wv), _const_spec((1, B_DV)),
                  _const_spec(w_bf16.shape), _row_spec(tm, d), _const_spec((1, d)), vec],
        out_specs=_row_spec(tm, d),
        compiler_params=_cparams("parallel", "parallel"),
        name="out0",
    )(a_out, o_f, o_b, z, out_norm.reshape(1, B_DV), w_bf16, x, gain.reshape(1, d),
      gate.reshape(bsz, 1, d))


def _out1(c_out, d_out, w_bf16, x, gain, gate, tm):
    bsz, seq, d = x.shape
    vec = pl.BlockSpec((1, 1, d), lambda b, i: (b, 0, 0))
    return pl.pallas_call(
        _out1_kernel,
        out_shape=jax.ShapeDtypeStruct(x.shape, F32),
        grid=(bsz, seq // tm),
        in_specs=[pl.BlockSpec((1, c_out.shape[1], tm, c_out.shape[3]), lambda b, i: (b, 0, i, 0)),
                  _row_spec(tm, d_out.shape[-1]),
                  _const_spec(w_bf16.shape), _row_spec(tm, d), _const_spec((1, d)), vec],
        out_specs=_row_spec(tm, d),
        compiler_params=_cparams("parallel", "parallel"),
        name="out1",
    )(c_out, d_out, w_bf16, x, gain.reshape(1, d), gate.reshape(bsz, 1, d))


def _ffn_kernel(x_ref, g1_ref, sc_ref, sh_ref, wi_ref, wo_ref, g2_ref, gate_ref, o_ref, *, nchunk):
    x = x_ref[0]
    h = (_rms(x, g1_ref[...]) * (1.0 + sc_ref[0]) + sh_ref[0]).astype(BF16)
    hidden = wo_ref.shape[0]
    ck = hidden // nchunk
    y = None
    for c in range(nchunk):
        gt = jnp.dot(h, wi_ref[:, c * ck:(c + 1) * ck], preferred_element_type=F32)
        up = jnp.dot(h, wi_ref[:, hidden + c * ck:hidden + (c + 1) * ck], preferred_element_type=F32)
        part = _dot(gt * _sigmoid(gt) * up, wo_ref[c * ck:(c + 1) * ck, :])
        y = part if y is None else y + part
    o_ref[0] = x + gate_ref[0] * _rms(y, g2_ref[...])


def _ffn(x, g1, sc, sh, wi_bf16, wo_bf16, g2, gate, tm):
    bsz, seq, d = x.shape
    vec = pl.BlockSpec((1, 1, d), lambda b, i: (b, 0, 0))
    hidden = wo_bf16.shape[0]
    nchunk = 2 if (hidden // 2) % LANES == 0 else 1
    return pl.pallas_call(
        functools.partial(_ffn_kernel, nchunk=nchunk),
        out_shape=jax.ShapeDtypeStruct(x.shape, F32),
        grid=(bsz, seq // tm),
        in_specs=[_row_spec(tm, d), _const_spec((1, d)), vec, vec,
                  _const_spec(wi_bf16.shape), _const_spec(wo_bf16.shape),
                  _const_spec((1, d)), vec],
        out_specs=_row_spec(tm, d),
        compiler_params=_cparams("parallel", "parallel"),
        name="ffn",
    )(x, g1.reshape(1, d), sc.reshape(bsz, 1, d), sh.reshape(bsz, 1, d), wi_bf16, wo_bf16,
      g2.reshape(1, d), gate.reshape(bsz, 1, d))


def _fft_kernel(u_ref, f1_ref, twc_ref, tws_ref, f2_ref, fw_ref, o_ref, u_sc, p_sc, y_sc,
                *, scale, n1, n2):
    pu = n2 + FFT_PAD
    pp = 2 * n1 + FFT_PAD
    py = n1 + FFT_PAD
    f1 = f1_ref[...].astype(BF16)
    f2 = f2_ref[...].astype(BF16)
    fw = fw_ref[...].astype(BF16)
    for i1 in range(n1):
        u_sc[i1 * pu:i1 * pu + n2, :] = u_ref[0, 0, i1 * n2:(i1 + 1) * n2, :]

    nb = FFT_BATCH

    def stage1(blk, carry):
        i2s = [blk * nb + j for j in range(nb)]
        x = jnp.concatenate([u_sc[pl.ds(i2, n1, stride=pu), :] for i2 in i2s], axis=1)
        p = jnp.dot(f1, x.astype(BF16), preferred_element_type=F32)
        for j, i2 in enumerate(i2s):
            p_sc[pl.ds(pl.multiple_of(i2 * pp, 8), 2 * n1), :] = p[:, j * C_WIDTH:(j + 1) * C_WIDTH]
        return carry

    lax.fori_loop(0, n2 // nb, stage1, 0)

    def stage2(blk, carry):
        k1s = [blk * nb + j for j in range(nb)]
        qr, qi = [], []
        for k1 in k1s:
            tc = twc_ref[k1]
            ts = tws_ref[k1]
            pr = p_sc[pl.ds(k1, n2, stride=pp), :]
            pim = p_sc[pl.ds(n1 + k1, n2, stride=pp), :]
            qr.append(pr * tc + pim * ts)
            qi.append(pim * tc - pr * ts)
        q = jnp.concatenate([jnp.concatenate(qr, axis=1), jnp.concatenate(qi, axis=1)], axis=0)
        xx = jnp.dot(f2, q.astype(BF16), preferred_element_type=F32)
        xg = jnp.concatenate(
            [jnp.concatenate([xx[:n2, j * C_WIDTH:(j + 1) * C_WIDTH],
                              xx[n2:, j * C_WIDTH:(j + 1) * C_WIDTH]], axis=1) for j in range(nb)],
            axis=0)
        y = jnp.dot(xg.astype(BF16), fw, preferred_element_type=F32) * scale
        for j, k1 in enumerate(k1s):
            y_sc[pl.ds(k1, n2, stride=py), :] = y[j * n2:(j + 1) * n2]
        return carry

    lax.fori_loop(0, n1 // nb, stage2, 0)
    for k2 in range(n2):
        o_ref[0, 0, k2 * n1:(k2 + 1) * n1, :] = y_sc[k2 * py:k2 * py + n1, :]


def _fourier_mixer(u):
    bsz, ngroups, seq, width = u.shape
    n2 = FFT_N2
    n1 = seq // n2
    assert n1 * n2 == seq and width == C_WIDTH and n1 % 8 == 0
    a1 = 2.0 * np.pi * np.outer(np.arange(n1), np.arange(n1)) / n1
    f1 = np.concatenate([np.cos(a1), -np.sin(a1)], axis=0)
    a2 = 2.0 * np.pi * np.outer(np.arange(n2), np.arange(n2)) / n2
    c2, s2 = np.cos(a2), np.sin(a2)
    f2 = np.block([[c2, s2], [-s2, c2]])
    aw = 2.0 * np.pi * np.outer(np.arange(C_WIDTH), np.arange(C_WIDTH)) / C_WIDTH
    fw = np.concatenate([np.cos(aw), np.sin(aw)], axis=0)
    at = np.repeat((2.0 * np.pi * np.outer(np.arange(n1), np.arange(n2)) / seq)[:, :, None],
                   C_WIDTH, axis=2)
    consts = (jnp.asarray(f1, F32), jnp.asarray(np.cos(at), F32), jnp.asarray(np.sin(at), F32),
              jnp.asarray(f2, F32), jnp.asarray(fw, F32))
    blk = pl.BlockSpec((1, 1, seq, C_WIDTH), lambda b, g: (b, g, 0, 0))
    return pl.pallas_call(
        functools.partial(_fft_kernel, scale=float(1.0 / np.sqrt(seq * C_WIDTH)), n1=n1, n2=n2),
        out_shape=jax.ShapeDtypeStruct(u.shape, F32),
        grid=(bsz, ngroups),
        in_specs=[blk] + [_const_spec(c.shape) for c in consts],
        out_specs=blk,
        scratch_shapes=[pltpu.VMEM((n1 * (n2 + FFT_PAD), C_WIDTH), F32),
                        pltpu.VMEM((n2 * (2 * n1 + FFT_PAD), C_WIDTH), F32),
                        pltpu.VMEM((n2 * (n1 + FFT_PAD), C_WIDTH), F32)],
        compiler_params=_cparams("parallel", "parallel"),
        name="fft",
    )(u, *consts)


def _head_perm():
    rep = D_Q_HEADS // D_KV_HEADS
    cols = []
    for j in range(rep):
        for g in range(D_KV_HEADS):
            h = g * rep + j
            cols.extend(range(h * D_HEAD_DIM, (h + 1) * D_HEAD_DIM))
    return np.asarray(cols, np.int32)


def _rope_tables(seq):
    rows = seq // GRID_W
    row = jnp.repeat(jnp.arange(rows, dtype=F32), GRID_W)
    col = jnp.tile(jnp.arange(GRID_W, dtype=F32), rows)
    axis_dim = D_HEAD_DIM // 2
    inv_freq = jnp.power(ROPE_THETA, -jnp.arange(0, axis_dim, 2, dtype=F32) / axis_dim)
    ang_r = row[:, None] * inv_freq[None, :]
    ang_c = col[:, None] * inv_freq[None, :]
    cr, sr, cc, sc = jnp.cos(ang_r), jnp.sin(ang_r), jnp.cos(ang_c), jnp.sin(ang_c)
    cos = jnp.concatenate([cr, cr, cc, cc], axis=1)
    sin = jnp.concatenate([-sr, sr, -sc, sc], axis=1)
    reps = LANES // D_HEAD_DIM
    return jnp.tile(cos, (1, reps)), jnp.tile(sin, (1, reps))


def _qkprep_kernel(q_ref, k_ref, v_ref, cos_ref, sin_ref, bd_h_ref, bd_l_ref, gq_ref, gk_ref,
                   qo_ref, ko_ref, vo_ref):
    cos = cos_ref[...]
    sin = sin_ref[...]
    quarter = D_HEAD_DIM // 4
    lane = lax.broadcasted_iota(jnp.int32, (1, LANES), 1)
    first_of_pair = (lane // quarter) % 2 == 0

    def norm_rope(x, gain, scale):
        ms = _dot_tab(bd_h_ref[...], bd_l_ref[...], x * x, tab_left=False)
        xn = x * lax.rsqrt(ms + EPS) * gain
        partner = jnp.where(first_of_pair, pltpu.roll(xn, LANES - quarter, 1),
                            pltpu.roll(xn, quarter, 1))
        return ((xn * cos + partner * sin) * scale).astype(BF16)

    for j in range(q_ref.shape[-1] // LANES):
        cols = slice(j * LANES, (j + 1) * LANES)
        qo_ref[0, :, cols] = norm_rope(q_ref[0, :, cols], gq_ref[...], D_HEAD_DIM ** -0.5 * LOG2E)
    ko_ref[0] = norm_rope(k_ref[0], gk_ref[...], 1.0)
    vt = v_ref[0].T
    ones = jnp.ones((FLASH_ONES, vt.shape[1]), F32)
    vo_ref[0] = jnp.concatenate(
        [piece for g in range(D_KV_HEADS)
         for piece in (vt[g * D_HEAD_DIM:(g + 1) * D_HEAD_DIM], ones)], axis=0).astype(BF16)


def _qkprep(qkv, qk_norm_j, tm):
    bsz, seq, _ = qkv.shape
    qw = D_Q_HEADS * D_HEAD_DIM
    kw = D_KV_HEADS * D_HEAD_DIM
    assert kw == LANES
    vrows = D_KV_HEADS * (D_HEAD_DIM + FLASH_ONES)
    cos, sin = _rope_tables(seq)
    bd = np.kron(np.eye(LANES // D_HEAD_DIM), np.full((D_HEAD_DIM, D_HEAD_DIM), 1.0 / D_HEAD_DIM))
    bd_h, bd_l = _np_split2(bd)
    reps = LANES // D_HEAD_DIM
    gq = jnp.tile(qk_norm_j[0].astype(F32), reps).reshape(1, LANES)
    gk = jnp.tile(qk_norm_j[1].astype(F32), reps).reshape(1, LANES)
    tab = pl.BlockSpec((tm, LANES), lambda b, i: (i, 0))
    return pl.pallas_call(
        _qkprep_kernel,
        out_shape=(jax.ShapeDtypeStruct((bsz, seq, qw), BF16),
                   jax.ShapeDtypeStruct((bsz, seq, kw), BF16),
                   jax.ShapeDtypeStruct((bsz, vrows, seq), BF16)),
        grid=(bsz, seq // tm),
        in_specs=[_row_spec(tm, qw, 0), _row_spec(tm, kw, qw // kw), _row_spec(tm, kw, qw // kw + 1),
                  tab, tab, _const_spec(bd_h.shape), _const_spec(bd_l.shape),
                  _const_spec((1, LANES)), _const_spec((1, LANES))],
        out_specs=(_row_spec(tm, qw), _row_spec(tm, kw),
                   pl.BlockSpec((1, vrows, tm), lambda b, i: (b, 0, i))),
        compiler_params=_cparams("parallel", "parallel"),
        name="qkprep",
    )(qkv, qkv, qkv, cos, sin, bd_h, bd_l, gq, gk)


def _flash_kernel(q_ref, k_ref, vt_ref, o_ref, m_sc, acc_sc, s_sc):
    kv = pl.program_id(2)

    @pl.when(kv == 0)
    def _():
        m_sc[...] = jnp.full_like(m_sc, -jnp.inf)
        acc_sc[...] = jnp.zeros_like(acc_sc)

    lane = lax.broadcasted_iota(jnp.int32, (1, LANES), 1)
    lo = lane < D_HEAD_DIM
    nblk = q_ref.shape[-1] // LANES
    tq, tk = q_ref.shape[1], k_ref.shape[1]
    ku, qu = FLASH_KEY_UNIT, FLASH_QUERY_UNIT
    grows = D_HEAD_DIM + FLASH_ONES
    nheads = nblk * D_KV_HEADS
    k = k_ref[0]

    def logits_pass(idx):
        j, g = divmod(idx, D_KV_HEADS)
        qj = q_ref[0, :, j * LANES:(j + 1) * LANES]
        sel = lo if g == 0 else jnp.logical_not(lo)
        qm = jnp.where(sel, qj, jnp.zeros_like(qj))
        mcs = []
        for c in range(tq // qu):
            s = lax.dot_general(k, qm[c * qu:(c + 1) * qu], (((1,), (1,)), ((), ())),
                                preferred_element_type=F32)
            s_sc[idx, :, c * qu:(c + 1) * qu] = s
            mcs.append(jnp.max(s, axis=0, keepdims=True))
        m_prev = m_sc[idx, 0:1, :]
        m_new = jnp.maximum(m_prev, jnp.concatenate(mcs, axis=1))
        m_sc[idx, 0:1, :] = m_new
        return m_new, jnp.exp2(m_prev - m_new)

    def value_pass(idx, m_new, alpha):
        j, g = divmod(idx, D_KV_HEADS)
        rows = slice(g * grows, (g + 1) * grows)
        for c in range(tq // qu):
            qcols = slice(c * qu, (c + 1) * qu)
            pv = None
            for u in range(tk // ku):
                keys = slice(u * ku, (u + 1) * ku)
                p = jnp.exp2(s_sc[idx, keys, qcols] - m_new[:, qcols])
                d = jnp.dot(vt_ref[0, rows, keys], p.astype(BF16), preferred_element_type=F32)
                pv = d if pv is None else pv + d
            acc_sc[j, rows, qcols] = alpha[:, qcols] * acc_sc[j, rows, qcols] + pv

    stats = logits_pass(0)
    for idx in range(nheads):
        nxt = logits_pass(idx + 1) if idx + 1 < nheads else None
        value_pass(idx, *stats)
        stats = nxt

    @pl.when(kv == pl.num_programs(2) - 1)
    def _():
        for j in range(nblk):
            parts = []
            for g in range(D_KV_HEADS):
                num = acc_sc[j, g * grows:g * grows + D_HEAD_DIM, :]
                den = acc_sc[j, g * grows + D_HEAD_DIM:g * grows + D_HEAD_DIM + 1, :]
                parts.append(num / den)
            o_ref[0, :, j * LANES:(j + 1) * LANES] = (
                jnp.concatenate(parts, axis=0).T.astype(o_ref.dtype))


def _flash(q, k, vt, tq, tk):
    bsz, seq, qw = q.shape
    kw = k.shape[-1]
    vrows = vt.shape[1]
    return pl.pallas_call(
        _flash_kernel,
        out_shape=jax.ShapeDtypeStruct((bsz, seq, qw), BF16),
        grid=(bsz, seq // tq, seq // tk),
        in_specs=[pl.BlockSpec((1, tq, qw), lambda b, i, j: (b, i, 0)),
                  pl.BlockSpec((1, tk, kw), lambda b, i, j: (b, j, 0)),
                  pl.BlockSpec((1, vrows, tk), lambda b, i, j: (b, 0, j))],
        out_specs=pl.BlockSpec((1, tq, qw), lambda b, i, j: (b, i, 0)),
        scratch_shapes=[pltpu.VMEM((D_Q_HEADS, 8, tq), F32),
                        pltpu.VMEM((qw // LANES, vrows, tq), F32),
                        pltpu.VMEM((D_Q_HEADS, tk, tq), F32)],
        compiler_params=_cparams("parallel", "parallel", "arbitrary"),
        name="flash",
    )(q, k, vt)


def kernel(x, c, t5_bias, hgrn_lb_logits, ada_w, ada_b, norm_gains, ab_w_in, ab_w_out,
           hgrn_out_norm, cd_w_in, cd_w_out, qk_norm, ffn_w_in, ffn_w_out):
    bsz, seq, d = x.shape
    depth = ada_w.shape[0]
    mod = _ada_mod(c.astype(F32), ada_w, ada_b)
    perm = _head_perm()
    aw = A_HEADS * A_HEAD_DIM
    cw = C_GROUPS * C_WIDTH
    qw = D_Q_HEADS * D_HEAD_DIM
    tm_in = min(512, seq)
    tm = min(512, seq)
    for layer in range(depth):
        sh_m, sc_m, g_m, sh_f, sc_f, g_f = [mod[layer, :, i * d:(i + 1) * d] for i in range(6)]
        gains = norm_gains[layer]
        j = layer // 2
        if layer % 2 == 0:
            w_in = ab_w_in[j].astype(BF16)
            *qkv_cm, z = _inproj_cm(x, gains[0], sc_m, sh_m, w_in, 3 * aw, tm_in)
            branches = [_dilated_branch(cm, t5_bias, window, dil)
                        for cm, (window, dil) in zip(qkv_cm, DIL_CFG)]
            a_out = _dilated_merge(branches, tm)
            o_f, o_b = _hgrn(z, hgrn_lb_logits, layer, 0)
            g_col = 3 * B_HEADS * B_DK + B_HEADS * B_DV
            x = _out0(a_out, o_f, o_b, z, g_col, hgrn_out_norm[j], ab_w_out[j].astype(BF16),
                      x, gains[1], g_m, tm)
        else:
            w_full = cd_w_in[j]
            w_in = jnp.concatenate([w_full[:, :cw], w_full[:, cw:cw + qw][:, perm],
                                    w_full[:, cw + qw:]], axis=1).astype(BF16)
            u, qkv = _inproj(x, gains[0], sc_m, sh_m, w_in, (cw, w_in.shape[1] - cw), tm)
            c_out = _fourier_mixer(u)
            qn, kn, vn = _qkprep(qkv, qk_norm[j], tm)
            d_out = _flash(qn, kn, vn, min(FLASH_TQ, seq), min(FLASH_TK, seq))
            w_out_full = cd_w_out[j]
            w_out = jnp.concatenate([w_out_full[:cw], w_out_full[cw:][perm]], axis=0).astype(BF16)
            x = _out1(c_out, d_out, w_out, x, gains[1], g_m, tm)
        x = _ffn(x, gains[2], sc_f, sh_f, ffn_w_in[layer].astype(BF16), ffn_w_out[layer].astype(BF16),
                 gains[3], g_f, tm)
    return x
```

```python
import functools

import numpy as np
import jax
import jax.numpy as jnp
from jax import lax
from jax.experimental import pallas as pl
from jax.experimental.pallas import tpu as pltpu

F32 = jnp.float32
BF16 = jnp.bfloat16
LANES = 128
VMEM_LIMIT_BYTES = 56 * 2**20
NEG_INF = -1e30
EPS = 1e-6

GRID_W = 64
A_HEADS = 8
A_HEAD_DIM = 64
DIL_CFG = ((128, 1), (512, 4), (2048, 16))
N_BUCKETS = 32
T5_MAX_DIST = 1024
B_HEADS = 4
B_DK = 128
B_DV = 128
C_GROUPS = 4
C_WIDTH = 128
D_Q_HEADS = 8
D_KV_HEADS = 2
D_HEAD_DIM = 64
ROPE_THETA = 10000.0

DIL_TQ = 128
DIL_TILE = 256
HGRN_BLOCK = 256
HGRN_BCAST_MIN = 8
FLASH_TQ = 1024
FLASH_TK = 1024
FLASH_KEY_UNIT = 256
FLASH_QUERY_UNIT = 256
FLASH_ONES = 16
FFT_N2 = 128
FFT_BATCH = 8
FFT_PAD = 8
LOG2E = 1.4426950408889634


def _cparams(*sem):
    return pltpu.CompilerParams(dimension_semantics=sem, vmem_limit_bytes=VMEM_LIMIT_BYTES)


def _const_spec(shape):
    nd = len(shape)
    return pl.BlockSpec(shape, lambda *_: (0,) * nd, pipeline_mode=pl.Buffered(1))


def _sigmoid(x):
    return 1.0 / (1.0 + jnp.exp(-x))


def _dot(a, b):
    return jnp.dot(a.astype(BF16), b.astype(BF16), preferred_element_type=F32)


def _dot_nt(a, b):
    return lax.dot_general(a.astype(BF16), b.astype(BF16), (((1,), (1,)), ((), ())),
                           preferred_element_type=F32)


def _split2(a):
    hi = a.astype(BF16)
    lo = (a - hi.astype(F32)).astype(BF16)
    return hi, lo


def _split3(a):
    a1 = a.astype(BF16)
    r = a - a1.astype(F32)
    a2 = r.astype(BF16)
    a3 = (r - a2.astype(F32)).astype(BF16)
    return a1, a2, a3


def _dot_tab(tab_hi, tab_lo, x, *, tab_left):
    x_hi, x_lo = _split2(x)
    if tab_left:
        d = lambda t, v: jnp.dot(t, v, preferred_element_type=F32)
    else:
        d = lambda t, v: jnp.dot(v, t, preferred_element_type=F32)
    return d(tab_hi, x_hi) + (d(tab_hi, x_lo) + d(tab_lo, x_hi))


def _rms(x, gain):
    ms = jnp.mean(x * x, axis=-1, keepdims=True)
    return x * lax.rsqrt(ms + EPS) * gain


def _np_split2(t):
    t = np.asarray(t, np.float32)
    hi = jnp.asarray(t, F32).astype(BF16)
    lo = (jnp.asarray(t, F32) - hi.astype(F32)).astype(BF16)
    return hi, lo


def _mod_kernel(c_ref, w_ref, b_ref, o_ref):
    c = c_ref[...]
    o_ref[0] = _dot(c * _sigmoid(c), w_ref[0]) + b_ref[0]


def _ada_mod(c, ada_w, ada_b):
    depth, d, n6 = ada_w.shape
    bsz = c.shape[0]
    rows = 8
    cp = jnp.zeros((rows, d), F32).at[:bsz].set(c)
    tn = n6 // 4
    out = pl.pallas_call(
        _mod_kernel,
        out_shape=jax.ShapeDtypeStruct((depth, rows, n6), F32),
        grid=(depth, n6 // tn),
        in_specs=[pl.BlockSpec((rows, d), lambda l, j: (0, 0)),
                  pl.BlockSpec((1, d, tn), lambda l, j: (l, 0, j)),
                  pl.BlockSpec((1, 1, tn), lambda l, j: (l, 0, j))],
        out_specs=pl.BlockSpec((1, rows, tn), lambda l, j: (l, 0, j)),
        compiler_params=_cparams("parallel", "parallel"),
        name="ada_mod",
    )(cp, ada_w, ada_b.reshape(depth, 1, n6))
    return out[:, :bsz]


def _inproj_kernel(x_ref, gain_ref, sc_ref, sh_ref, w_ref, *o_refs):
    h = _rms(x_ref[0], gain_ref[...]) * (1.0 + sc_ref[0]) + sh_ref[0]
    z = _dot(h, w_ref[...])
    off = 0
    for o_ref in o_refs:
        if len(o_ref.shape) == 4:
            for g in range(o_ref.shape[1]):
                o_ref[0, g] = z[:, off:off + LANES]
                off += LANES
        else:
            n = o_ref.shape[-1]
            o_ref[0] = z[:, off:off + n].astype(o_ref.dtype)
            off += n


def _inproj(x, gain, sc, sh, w_bf16, splits, tm):
    bsz, seq, d = x.shape
    n = w_bf16.shape[1]
    assert sum(splits) == n
    vec = pl.BlockSpec((1, 1, d), lambda b, i: (b, 0, 0))
    g0 = splits[0] // LANES
    return pl.pallas_call(
        _inproj_kernel,
        out_shape=tuple([jax.ShapeDtypeStruct((bsz, g0, seq, LANES), F32)]
                        + [jax.ShapeDtypeStruct((bsz, seq, s), F32) for s in splits[1:]]),
        grid=(bsz, seq // tm),
        in_specs=[pl.BlockSpec((1, tm, d), lambda b, i: (b, i, 0)),
                  _const_spec((1, d)), vec, vec, _const_spec((d, n))],
        out_specs=tuple([pl.BlockSpec((1, g0, tm, LANES), lambda b, i: (b, 0, i, 0))]
                        + [pl.BlockSpec((1, tm, s), lambda b, i: (b, i, 0)) for s in splits[1:]]),
        compiler_params=_cparams("parallel", "parallel"),
        name="inproj",
    )(x, gain.reshape(1, d), sc.reshape(bsz, 1, d), sh.reshape(bsz, 1, d), w_bf16)


def _inproj_cm_kernel(x_ref, gain_ref, sc_ref, sh_ref, w_ref, *refs):
    cm_refs, rest_ref, zs_sc = refs[:-2], refs[-2], refs[-1]
    h = _rms(x_ref[0], gain_ref[...]) * (1.0 + sc_ref[0]) + sh_ref[0]
    z = _dot(h, w_ref[...])
    nblk, tm, _ = zs_sc.shape
    rest_ref[0] = z[:, nblk * LANES:]
    for c in range(nblk):
        zs_sc[c] = z[:, c * LANES:(c + 1) * LANES]
    for cm_ref, (_, dil) in zip(cm_refs, DIL_CFG):
        for r in range(dil):
            for c in range(nblk):
                cm_ref[0, r, :, c * LANES:(c + 1) * LANES] = (
                    zs_sc[c, pl.ds(r, tm // dil, stride=dil), :].astype(BF16))


def _inproj_cm(x, gain, sc, sh, w_bf16, na, tm):
    bsz, seq, d = x.shape
    n = w_bf16.shape[1]
    vec = pl.BlockSpec((1, 1, d), lambda b, i: (b, 0, 0))
    dils = [dl for _, dl in DIL_CFG]
    return pl.pallas_call(
        _inproj_cm_kernel,
        out_shape=tuple([jax.ShapeDtypeStruct((bsz, dl, seq // dl, na), BF16) for dl in dils]
                        + [jax.ShapeDtypeStruct((bsz, seq, n - na), F32)]),
        grid=(bsz, seq // tm),
        in_specs=[pl.BlockSpec((1, tm, d), lambda b, i: (b, i, 0)),
                  _const_spec((1, d)), vec, vec, _const_spec((d, n))],
        out_specs=tuple([pl.BlockSpec((1, dl, tm // dl, na), lambda b, i: (b, 0, i, 0)) for dl in dils]
                        + [pl.BlockSpec((1, tm, n - na), lambda b, i: (b, i, 0))]),
        scratch_shapes=[pltpu.VMEM((na // LANES, tm, LANES), F32)],
        compiler_params=_cparams("parallel", "parallel"),
        name="inproj_cm",
    )(x, gain.reshape(1, d), sc.reshape(bsz, 1, d), sh.reshape(bsz, 1, d), w_bf16)


def _t5_buckets(rel):
    half = N_BUCKETS // 2
    max_exact = half // 2
    n = np.abs(rel)
    large = max_exact + (np.log(np.maximum(n, 1) / max_exact) / np.log(T5_MAX_DIST / max_exact)
                         * (half - max_exact)).astype(np.int32)
    large = np.minimum(large, half - 1)
    return (np.where(rel > 0, half, 0) + np.where(n < max_exact, n, large)).astype(np.int32)


def _dil_bias(t5_bias, window, dil, tq):
    half = (window // 2) // dil
    assert half == tq // 2
    rel = np.arange(2 * tq)[None, :] - half - np.arange(tq)[:, None]
    inside = np.abs(rel) <= half
    buckets = _t5_buckets(np.where(inside, rel, 0) * dil)
    onehot =jnp.asarray(np.eye(N_BUCKETS, dtype=np.float32)[buckets])
    bias = jnp.einsum("qkn,nh->hqk", onehot, t5_bias.astype(F32), precision=lax.Precision.HIGHEST)
    return jnp.where(jnp.asarray(inside)[None], bias, NEG_INF)


def _dil_kernel(q_ref, kp_ref, kc_ref, kn_ref, vp_ref, vc_ref, vn_ref, bias_ref, o_ref, lse_ref,
                *, class_len):
    i = pl.program_id(2)
    sub, hq, tile = DIL_TQ, DIL_TQ // 2, DIL_TILE
    kwin = jnp.concatenate([kp_ref[0, 0], kc_ref[0, 0], kn_ref[0, 0]], axis=0)
    vwin = jnp.concatenate([vp_ref[0, 0], vc_ref[0, 0], vn_ref[0, 0]], axis=0)
    lane = lax.broadcasted_iota(jnp.int32, (1, LANES), 1)
    lo = lane < A_HEAD_DIM
    scale = jnp.asarray(A_HEAD_DIM ** -0.5, BF16)
    for jt in range(tile // sub):
        qrows = slice(jt * sub, (jt + 1) * sub)
        krows = slice(jt * sub, jt * sub + 2 * sub)
        kpos = i * tile + jt * sub - hq + lax.broadcasted_iota(jnp.int32, (1, 2 * sub), 1)
        valid = jnp.logical_and(kpos >= 0, kpos < class_len)
        lse_all = jnp.zeros((sub, LANES), F32)
        for j in range(A_HEADS // 2):
            cols = slice(j * LANES, (j + 1) * LANES)
            qj = q_ref[0, 0, qrows, cols] * scale
            kj = kwin[krows, cols]
            vj = vwin[krows, cols]
            halves = []
            for half in range(2):
                h = 2 * j + half
                sel = lo if half == 0 else jnp.logical_not(lo)
                s = lax.dot_general(jnp.where(sel, qj, jnp.zeros_like(qj)), kj,
                                    (((1,), (1,)), ((), ())), preferred_element_type=F32)
                s = jnp.where(valid, s + bias_ref[h], NEG_INF)
                m = jnp.max(s, axis=-1, keepdims=True)
                p = jnp.exp(s - m)
                l = jnp.sum(p, axis=-1, keepdims=True)
                halves.append(jnp.dot(p.astype(BF16), vj, preferred_element_type=F32) / l)
                lse_all = jnp.where(lane == h, m + jnp.log(l), lse_all)
            o_ref[0, 0, qrows, cols] = jnp.where(lo, halves[0], halves[1])
        lse_ref[0, 0, qrows, :] = lse_all


def _dilated_branch(qkv_cm, t5_bias, window, dil):
    bsz, _, cl, width = qkv_cm.shape
    aw = A_HEADS * A_HEAD_DIM
    tile, hq = DIL_TILE, DIL_TQ // 2
    nt = cl // tile
    per = tile // hq
    nh = cl // hq

    def cur(col):
        return pl.BlockSpec((1, 1, tile, aw), lambda b, r, i: (b, r, i, col))

    def prev(col):
        return pl.BlockSpec((1, 1, hq, aw), lambda b, r, i: (b, r, jnp.maximum(i * per - 1, 0), col))

    def nxt(col):
        return pl.BlockSpec((1, 1, hq, aw),
                            lambda b, r, i: (b, r, jnp.minimum((i + 1) * per, nh - 1), col))

    return pl.pallas_call(
        functools.partial(_dil_kernel, class_len=cl),
        out_shape=(jax.ShapeDtypeStruct((bsz, dil, cl, aw), F32),
                   jax.ShapeDtypeStruct((bsz, dil, cl, LANES), F32)),
        grid=(bsz, dil, nt),
        in_specs=[cur(0), prev(1), cur(1), nxt(1), prev(2), cur(2), nxt(2),
                  _const_spec((A_HEADS, DIL_TQ, 2 * DIL_TQ))],
        out_specs=(pl.BlockSpec((1, 1, tile, aw), lambda b, r, i: (b, r, i, 0)),
                   pl.BlockSpec((1, 1, tile, LANES), lambda b, r, i: (b, r, i, 0))),
        compiler_params=_cparams("parallel", "parallel", "parallel"),
        name=f"dilated_d{dil}",
    )(*([qkv_cm] * 7), _dil_bias(t5_bias, window, dil, DIL_TQ))


def _dilmerge_kernel(*refs):
    nbr = len(DIL_CFG)
    a_refs, l_refs = refs[:nbr], refs[nbr:2 * nbr]
    e_ref, o_ref = refs[2 * nbr], refs[2 * nbr + 1]
    a_scs, l_scs = refs[2 * nbr + 2:3 * nbr + 2], refs[3 * nbr + 2:4 * nbr + 2]
    tm = o_ref.shape[1]
    accs, lses = [], []
    for (_, dil), a_ref, l_ref, a_sc, l_sc in zip(DIL_CFG, a_refs, l_refs, a_scs, l_scs):
        nblk = a_sc.shape[0]
        for r in range(dil):
            rows = pl.ds(r, tm // dil, stride=dil)
            for c in range(nblk):
                a_sc[c, rows, :] = a_ref[0, r, :, c * LANES:(c + 1) * LANES]
            l_sc[rows, :] = l_ref[0, r]
        accs.append(jnp.concatenate([a_sc[c] for c in range(nblk)], axis=1))
        lses.append(l_sc[...])
    mx = functools.reduce(jnp.maximum, lses)
    ws = [jnp.exp(x - mx) for x in lses]
    tot = functools.reduce(jnp.add, ws)
    out = None
    for w, a in zip(ws, accs):
        w_hi, w_lo = _split2(w / tot)
        wide = (jnp.dot(w_hi, e_ref[...], preferred_element_type=F32)
                + jnp.dot(w_lo, e_ref[...], preferred_element_type=F32))
        out = wide * a if out is None else out + wide * a
    o_ref[0] = out.astype(o_ref.dtype)


def _dilated_merge(branch_outs, tm):
    bsz, _, _, aw = branch_outs[0][0].shape
    seq = branch_outs[0][0].shape[1] * branch_outs[0][0].shape[2]
    expand = np.zeros((LANES, aw), np.float32)
    for h in range(A_HEADS):
        expand[h, h * A_HEAD_DIM:(h + 1) * A_HEAD_DIM] = 1.0
    dils = [d for _, d in DIL_CFG]
    in_specs = ([pl.BlockSpec((1, d, tm // d, aw), lambda b, i: (b, 0, i, 0)) for d in dils]
                + [pl.BlockSpec((1, d, tm // d, LANES), lambda b, i: (b, 0, i, 0)) for d in dils]
                + [_const_spec(expand.shape)])
    return pl.pallas_call(
        _dilmerge_kernel,
        out_shape=jax.ShapeDtypeStruct((bsz, seq, aw), BF16),
        grid=(bsz, seq // tm),
        in_specs=in_specs,
        out_specs=pl.BlockSpec((1, tm, aw), lambda b, i: (b, i, 0)),
        scratch_shapes=([pltpu.VMEM((aw // LANES, tm, LANES), F32) for _ in dils]
                        + [pltpu.VMEM((tm, LANES), F32) for _ in dils]),
        compiler_params=_cparams("parallel", "parallel"),
        name="dilmerge",
    )(*[o for o, _ in branch_outs], *[l for _, l in branch_outs], jnp.asarray(expand, BF16))


def _hgrn_consts(reverse):
    t = HGRN_BLOCK
    r = np.arange(t)
    u = r[None, :]
    row = r[:, None]
    nmats = [(u >= row) if reverse else (u <= row)]
    masks = []
    m = t // 2
    while m >= 1:
        grp = r // (2 * m)
        in_first = (r % (2 * m)) < m
        same = grp[:, None] == grp[None, :]
        if reverse:
            beta = (grp * 2 * m + m)[:, None]
            n = np.where(in_first[:, None], (u >= row) & (u < beta), (u >= beta) & (u < row))
            mask = same & in_first[:, None] & ~in_first[None, :]
        else:
            beta = (grp * 2 * m + m - 1)[:, None]
            n = np.where(in_first[:, None], (u > row) & (u <= beta), (u > beta) & (u <= row))
            mask = same & ~in_first[:, None] & in_first[None, :]
        if m < HGRN_BCAST_MIN:
            nmats.append(n)
        masks.append(mask)
        m //= 2
    masks.append(np.eye(t, dtype=bool))
    nmat = jnp.asarray(np.concatenate(nmats, axis=0), F32).astype(BF16)
    return nmat, jnp.asarray(np.stack(masks), F32)


def _hgrn_block(q, f, v, lb, st, nmat_ref, mask_ref, reverse):
    t = HGRN_BLOCK
    nlev = mask_ref.shape[0] - 1
    qs = q * (B_DK ** -0.5)
    fa = lb + (1.0 - lb) * _sigmoid(f)
    kk = 1.0 - fa
    g_hi, g_lo = _split2(jnp.log(fa))
    ex = jnp.dot(nmat_ref[...], jnp.concatenate([g_hi, g_lo], axis=1), preferred_element_type=F32)
    ex = ex[:, :B_DK] + ex[:, B_DK:]
    b = ex[:t]
    btot = b[0:1] if reverse else b[t - 1:t]
    a = mask_ref[nlev] * _dot_nt(qs, kk)
    fine = 1
    for l in range(nlev):
        m = t >> (l + 1)
        if m >= HGRN_BCAST_MIN:
            ref = jnp.concatenate(
                [jnp.broadcast_to(b[beta:beta + 1], (2 * m, B_DK))
                 for beta in range(m if reverse else m - 1, t, 2 * m)], axis=0)
            e = jnp.exp(-jnp.abs(b - ref))
        else:
            e = jnp.exp(ex[fine * t:(fine + 1) * t])
            fine += 1
        a = a + mask_ref[l] * _dot_nt(qs * e, kk * e)
    out = _dot(a, v) + _dot_nt(qs * jnp.exp(b), st)
    khat = (kk * jnp.exp(btot - b)).astype(BF16)
    st_new = st * jnp.exp(btot) + jnp.dot(v.T.astype(BF16), khat, preferred_element_type=F32)
    return out, st_new


def _hgrn_lb(lg_ref, layer):
    lg = [lg_ref[l, 0] for l in range(lg_ref.shape[0])]
    mx = functools.reduce(jnp.maximum, lg)
    e = [jnp.exp(x - mx) for x in lg]
    return functools.reduce(jnp.add, e[:layer + 1]) / functools.reduce(jnp.add, e)


def _hgrn_kernel(qf_ref, ff_ref, vf_ref, qb_ref, fb_ref, vb_ref, lgf_ref, lgb_ref,
                 nf_ref, mf_ref, nb_ref, mb_ref, of_ref, ob_ref, sf_sc, sb_sc, *, layer):
    @pl.when(pl.program_id(2) == 0)
    def _():
        sf_sc[...] = jnp.zeros_like(sf_sc)
        sb_sc[...] = jnp.zeros_like(sb_sc)

    o, st = _hgrn_block(qf_ref[0], ff_ref[0], vf_ref[0], _hgrn_lb(lgf_ref, layer), sf_sc[...],
                        nf_ref, mf_ref, False)
    of_ref[0] = o
    sf_sc[...] = st
    o, st = _hgrn_block(qb_ref[0], fb_ref[0], vb_ref[0], _hgrn_lb(lgb_ref, layer), sb_sc[...],
                        nb_ref, mb_ref, True)
    ob_ref[0] = o
    sb_sc[...] = st


def _hgrn(z, lb_logits, layer, col0):
    bsz, seq, _ = z.shape
    t = HGRN_BLOCK
    nb = seq // t
    c0 = col0 // LANES
    hw = B_HEADS

    def zspec(group, rev):
        return pl.BlockSpec(
            (1, t, LANES),
            lambda b, h, j: (b, (nb - 1 - j) if rev else j, c0 + group * hw + h))

    def lgspec(direction):
        return pl.BlockSpec((lb_logits.shape[0], 1, 1, LANES),
                            lambda b, h, j: (0, direction * hw + h, 0, 0))

    nf, mf = _hgrn_consts(False)
    nbw, mbw = _hgrn_consts(True)
    lg = lb_logits.astype(F32).reshape(lb_logits.shape[0], 2 * hw, 1, LANES)
    o_shape = jax.ShapeDtypeStruct((bsz, seq, hw * B_DV), F32)
    return pl.pallas_call(
        functools.partial(_hgrn_kernel, layer=layer),
        out_shape=(o_shape, o_shape),
        grid=(bsz, hw, nb),
        in_specs=[zspec(0, False), zspec(1, False), zspec(3, False),
                  zspec(0, True), zspec(2, True), zspec(3, True),
                  lgspec(0), lgspec(1),
                  _const_spec(nf.shape), _const_spec(mf.shape),
                  _const_spec(nbw.shape), _const_spec(mbw.shape)],
        out_specs=(pl.BlockSpec((1, t, LANES), lambda b, h, j: (b, j, h)),
                   pl.BlockSpec((1, t, LANES), lambda b, h, j: (b, nb - 1 - j, h))),
        scratch_shapes=[pltpu.VMEM((B_DV, B_DK), F32), pltpu.VMEM((B_DV, B_DK), F32)],
        compiler_params=_cparams("parallel", "parallel", "arbitrary"),
        name="hgrn",
    )(z, z, z, z, z, z, lg, lg, nf, mf, nbw, mbw)


def _residual_epilogue(y, x_ref, gain_ref, gate_ref, o_ref):
    o_ref[0] = x_ref[0] + gate_ref[0] * _rms(y, gain_ref[...])


def _out0_kernel(a_ref, of_ref, ob_ref, g_ref, on_ref, w_ref, x_ref, gain_ref, gate_ref, o_ref):
    o = of_ref[0] + ob_ref[0]
    g = g_ref[0]
    parts = [_rms(o[:, h * B_DV:(h + 1) * B_DV], on_ref[...]) for h in range(B_HEADS)]
    bn = jnp.concatenate(parts, axis=-1) * (g * _sigmoid(g))
    na = a_ref.shape[-1]
    y = _dot(a_ref[0], w_ref[:na, :]) + _dot(bn, w_ref[na:, :])
    _residual_epilogue(y, x_ref, gain_ref, gate_ref, o_ref)


def _out1_kernel(c_ref, d_ref, w_ref, x_ref, gain_ref, gate_ref, o_ref):
    c = jnp.concatenate([c_ref[0, g] for g in range(c_ref.shape[1])], axis=1)
    nc = c.shape[-1]
    y = _dot(c, w_ref[:nc, :]) + _dot(d_ref[0], w_ref[nc:, :])
    _residual_epilogue(y, x_ref, gain_ref, gate_ref, o_ref)


def _row_spec(tm, width, col=0):
    return pl.BlockSpec((1, tm, width), lambda b, i: (b, i, col))


def _out0(a_out, o_f, o_b, z, g_col, out_norm, w_bf16, x, gain, gate, tm):
    bsz, seq, d = x.shape
    wv = B_HEADS * B_DV
    vec = pl.BlockSpec((1, 1, d), lambda b, i: (b, 0, 0))
    return pl.pallas_call(
        _out0_kernel,
        out_shape=jax.ShapeDtypeStruct(x.shape, F32),
        grid=(bsz, seq // tm),
        in_specs=[_row_spec(tm, a_out.shape[-1]), _row_spec(tm, wv), _row_spec(tm, wv),
                  _row_spec(tm, wv, g_col // wv), _const_spec((1, B_DV)),
                  _const_spec(w_bf16.shape), _row_spec(tm, d), _const_spec((1, d)), vec],
        out_specs=_row_spec(tm, d),
        compiler_params=_cparams("parallel", "parallel"),
        name="out0",
    )(a_out, o_f, o_b, z, out_norm.reshape(1, B_DV), w_bf16, x, gain.reshape(1, d),
      gate.reshape(bsz, 1, d))


def _out1(c_out, d_out, w_bf16, x, gain, gate, tm):
    bsz, seq, d = x.shape
    vec = pl.BlockSpec((1, 1, d), lambda b, i: (b, 0, 0))
    return pl.pallas_call(
        _out1_kernel,
        out_shape=jax.ShapeDtypeStruct(x.shape, F32),
        grid=(bsz, seq // tm),
        in_specs=[pl.BlockSpec((1, c_out.shape[1], tm, c_out.shape[3]), lambda b, i: (b, 0, i, 0)),
                  _row_spec(tm, d_out.shape[-1]),
                  _const_spec(w_bf16.shape), _row_spec(tm, d), _const_spec((1, d)), vec],
        out_specs=_row_spec(tm, d),
        compiler_params=_cparams("parallel", "parallel"),
        name="out1",
    )(c_out, d_out, w_bf16, x, gain.reshape(1, d), gate.reshape(bsz, 1, d))


def _ffn_kernel(x_ref, g1_ref, sc_ref, sh_ref, wi_ref, wo_ref, g2_ref, gate_ref, o_ref, *, nchunk):
    x = x_ref[0]
    h = (_rms(x, g1_ref[...]) * (1.0 + sc_ref[0]) + sh_ref[0]).astype(BF16)
    hidden = wo_ref.shape[0]
    ck = hidden // nchunk
    y = None
    for c in range(nchunk):
        gt = jnp.dot(h, wi_ref[:, c * ck:(c + 1) * ck], preferred_element_type=F32)
        up = jnp.dot(h, wi_ref[:, hidden + c * ck:hidden + (c + 1) * ck], preferred_element_type=F32)
        part = _dot(gt * _sigmoid(gt) * up, wo_ref[c * ck:(c + 1) * ck, :])
        y = part if y is None else y + part
    o_ref[0] = x + gate_ref[0] * _rms(y, g2_ref[...])


def _ffn(x, g1, sc, sh, wi_bf16, wo_bf16, g2, gate, tm):
    bsz, seq, d = x.shape
    vec = pl.BlockSpec((1, 1, d), lambda b, i: (b, 0, 0))
    hidden = wo_bf16.shape[0]
    nchunk = 2 if (hidden // 2) % LANES == 0 else 1
    return pl.pallas_call(
        functools.partial(_ffn_kernel, nchunk=nchunk),
        out_shape=jax.ShapeDtypeStruct(x.shape, F32),
        grid=(bsz, seq // tm),
        in_specs=[_row_spec(tm, d), _const_spec((1, d)), vec, vec,
                  _const_spec(wi_bf16.shape), _const_spec(wo_bf16.shape),
                  _const_spec((1, d)), vec],
        out_specs=_row_spec(tm, d),
        compiler_params=_cparams("parallel", "parallel"),
        name="ffn",
    )(x, g1.reshape(1, d), sc.reshape(bsz, 1, d), sh.reshape(bsz, 1, d), wi_bf16, wo_bf16,
      g2.reshape(1, d), gate.reshape(bsz, 1, d))


def _fft_kernel(u_ref, f1_ref, twc_ref, tws_ref, f2_ref, fw_ref, o_ref, u_sc, p_sc, y_sc,
                *, scale, n1, n2):
    pu = n2 + FFT_PAD
    pp = 2 * n1 + FFT_PAD
    py = n1 + FFT_PAD
    f1 = f1_ref[...].astype(BF16)
    f2 = f2_ref[...].astype(BF16)
    fw = fw_ref[...].astype(BF16)
    for i1 in range(n1):
        u_sc[i1 * pu:i1 * pu + n2, :] = u_ref[0, 0, i1 * n2:(i1 + 1) * n2, :]

    nb = FFT_BATCH

    def stage1(blk, carry):
        i2s = [blk * nb + j for j in range(nb)]
        x = jnp.concatenate([u_sc[pl.ds(i2, n1, stride=pu), :] for i2 in i2s], axis=1)
        p = jnp.dot(f1, x.astype(BF16), preferred_element_type=F32)
        for j, i2 in enumerate(i2s):
            p_sc[pl.ds(pl.multiple_of(i2 * pp, 8), 2 * n1), :] = p[:, j * C_WIDTH:(j + 1) * C_WIDTH]
        return carry

    lax.fori_loop(0, n2 // nb, stage1, 0)

    def stage2(blk, carry):
        k1s = [blk * nb + j for j in range(nb)]
        qr, qi = [], []
        for k1 in k1s:
            tc = twc_ref[k1]
            ts = tws_ref[k1]
            pr = p_sc[pl.ds(k1, n2, stride=pp), :]
            pim = p_sc[pl.ds(n1 + k1, n2, stride=pp), :]
            qr.append(pr * tc + pim * ts)
            qi.append(pim * tc - pr * ts)
        q = jnp.concatenate([jnp.concatenate(qr, axis=1), jnp.concatenate(qi, axis=1)], axis=0)
        xx = jnp.dot(f2, q.astype(BF16), preferred_element_type=F32)
        xg = jnp.concatenate(
            [jnp.concatenate([xx[:n2, j * C_WIDTH:(j + 1) * C_WIDTH],
                              xx[n2:, j * C_WIDTH:(j + 1) * C_WIDTH]], axis=1) for j in range(nb)],
            axis=0)
        y = jnp.dot(xg.astype(BF16), fw, preferred_element_type=F32) * scale
        for j, k1 in enumerate(k1s):
            y_sc[pl.ds(k1, n2, stride=py), :] = y[j * n2:(j + 1) * n2]
        return carry

    lax.fori_loop(0, n1 // nb, stage2, 0)
    for k2 in range(n2):
        o_ref[0, 0, k2 * n1:(k2 + 1) * n1, :] = y_sc[k2 * py:k2 * py + n1, :]


def _fourier_mixer(u):
    bsz, ngroups, seq, width = u.shape
    n2 = FFT_N2
    n1 = seq // n2
    assert n1 * n2 == seq and width == C_WIDTH and n1 % 8 == 0
    a1 = 2.0 * np.pi * np.outer(np.arange(n1), np.arange(n1)) / n1
    f1 = np.concatenate([np.cos(a1), -np.sin(a1)], axis=0)
    a2 = 2.0 * np.pi * np.outer(np.arange(n2), np.arange(n2)) / n2
    c2, s2 = np.cos(a2), np.sin(a2)
    f2 = np.block([[c2, s2], [-s2, c2]])
    aw = 2.0 * np.pi * np.outer(np.arange(C_WIDTH), np.arange(C_WIDTH)) / C_WIDTH
    fw = np.concatenate([np.cos(aw), np.sin(aw)], axis=0)
    at = np.repeat((2.0 * np.pi * np.outer(np.arange(n1), np.arange(n2)) / seq)[:, :, None],
                   C_WIDTH, axis=2)
    consts = (jnp.asarray(f1, F32), jnp.asarray(np.cos(at), F32), jnp.asarray(np.sin(at), F32),
              jnp.asarray(f2, F32), jnp.asarray(fw, F32))
    blk = pl.BlockSpec((1, 1, seq, C_WIDTH), lambda b, g: (b, g, 0, 0))
    return pl.pallas_call(
        functools.partial(_fft_kernel, scale=float(1.0 / np.sqrt(seq * C_WIDTH)), n1=n1, n2=n2),
        out_shape=jax.ShapeDtypeStruct(u.shape, F32),
        grid=(bsz, ngroups),
        in_specs=[blk] + [_const_spec(c.shape) for c in consts],
        out_specs=blk,
        scratch_shapes=[pltpu.VMEM((n1 * (n2 + FFT_PAD), C_WIDTH), F32),
                        pltpu.VMEM((n2 * (2 * n1 + FFT_PAD), C_WIDTH), F32),
                        pltpu.VMEM((n2 * (n1 + FFT_PAD), C_WIDTH), F32)],
        compiler_params=_cparams("parallel", "parallel"),
        name="fft",
    )(u, *consts)


def _head_perm():
    rep = D_Q_HEADS // D_KV_HEADS
    cols = []
    for j in range(rep):
        for g in range(D_KV_HEADS):
            h = g * rep + j
            cols.extend(range(h * D_HEAD_DIM, (h + 1) * D_HEAD_DIM))
    return np.asarray(cols, np.int32)


def _rope_tables(seq):
    rows = seq // GRID_W
    row = jnp.repeat(jnp.arange(rows, dtype=F32), GRID_W)
    col = jnp.tile(jnp.arange(GRID_W, dtype=F32), rows)
    axis_dim = D_HEAD_DIM // 2
    inv_freq = jnp.power(ROPE_THETA, -jnp.arange(0, axis_dim, 2, dtype=F32) / axis_dim)
    ang_r = row[:, None] * inv_freq[None, :]
    ang_c = col[:, None] * inv_freq[None, :]
    cr, sr, cc, sc = jnp.cos(ang_r), jnp.sin(ang_r), jnp.cos(ang_c), jnp.sin(ang_c)
    cos = jnp.concatenate([cr, cr, cc, cc], axis=1)
    sin = jnp.concatenate([-sr, sr, -sc, sc], axis=1)
    reps = LANES // D_HEAD_DIM
    return jnp.tile(cos, (1, reps)), jnp.tile(sin, (1, reps))


def _qkprep_kernel(q_ref, k_ref, v_ref, cos_ref, sin_ref, bd_h_ref, bd_l_ref, gq_ref, gk_ref,
                   qo_ref, ko_ref, vo_ref):
    cos = cos_ref[...]
    sin = sin_ref[...]
    quarter = D_HEAD_DIM // 4
    lane = lax.broadcasted_iota(jnp.int32, (1, LANES), 1)
    first_of_pair = (lane // quarter) % 2 == 0

    def norm_rope(x, gain, scale):
        ms = _dot_tab(bd_h_ref[...], bd_l_ref[...], x * x, tab_left=False)
        xn = x * lax.rsqrt(ms + EPS) * gain
        partner = jnp.where(first_of_pair, pltpu.roll(xn, LANES - quarter, 1),
                            pltpu.roll(xn, quarter, 1))
        return ((xn * cos + partner * sin) * scale).astype(BF16)

    for j in range(q_ref.shape[-1] // LANES):
        cols = slice(j * LANES, (j + 1) * LANES)
        qo_ref[0, :, cols] = norm_rope(q_ref[0, :, cols], gq_ref[...], D_HEAD_DIM ** -0.5 * LOG2E)
    ko_ref[0] = norm_rope(k_ref[0], gk_ref[...], 1.0)
    vt = v_ref[0].T
    ones = jnp.ones((FLASH_ONES, vt.shape[1]), F32)
    vo_ref[0] = jnp.concatenate(
        [piece for g in range(D_KV_HEADS)
         for piece in (vt[g * D_HEAD_DIM:(g + 1) * D_HEAD_DIM], ones)], axis=0).astype(BF16)


def _qkprep(qkv, qk_norm_j, tm):
    bsz, seq, _ = qkv.shape
    qw = D_Q_HEADS * D_HEAD_DIM
    kw = D_KV_HEADS * D_HEAD_DIM
    assert kw == LANES
    vrows = D_KV_HEADS * (D_HEAD_DIM + FLASH_ONES)
    cos, sin = _rope_tables(seq)
    bd = np.kron(np.eye(LANES // D_HEAD_DIM), np.full((D_HEAD_DIM, D_HEAD_DIM), 1.0 / D_HEAD_DIM))
    bd_h, bd_l = _np_split2(bd)
    reps = LANES // D_HEAD_DIM
    gq = jnp.tile(qk_norm_j[0].astype(F32), reps).reshape(1, LANES)
    gk = jnp.tile(qk_norm_j[1].astype(F32), reps).reshape(1, LANES)
    tab = pl.BlockSpec((tm, LANES), lambda b, i: (i, 0))
    return pl.pallas_call(
        _qkprep_kernel,
        out_shape=(jax.ShapeDtypeStruct((bsz, seq, qw), BF16),
                   jax.ShapeDtypeStruct((bsz, seq, kw), BF16),
                   jax.ShapeDtypeStruct((bsz, vrows, seq), BF16)),
        grid=(bsz, seq // tm),
        in_specs=[_row_spec(tm, qw, 0), _row_spec(tm, kw, qw // kw), _row_spec(tm, kw, qw // kw + 1),
                  tab, tab, _const_spec(bd_h.shape), _const_spec(bd_l.shape),
                  _const_spec((1, LANES)), _const_spec((1, LANES))],
        out_specs=(_row_spec(tm, qw), _row_spec(tm, kw),
                   pl.BlockSpec((1, vrows, tm), lambda b, i: (b, 0, i))),
        compiler_params=_cparams("parallel", "parallel"),
        name="qkprep",
    )(qkv, qkv, qkv, cos, sin, bd_h, bd_l, gq, gk)


def _flash_kernel(q_ref, k_ref, vt_ref, o_ref, m_sc, acc_sc, s_sc):
    kv = pl.program_id(2)

    @pl.when(kv == 0)
    def _():
        m_sc[...] = jnp.full_like(m_sc, -jnp.inf)
        acc_sc[...] = jnp.zeros_like(acc_sc)

    lane = lax.broadcasted_iota(jnp.int32, (1, LANES), 1)
    lo = lane < D_HEAD_DIM
    nblk = q_ref.shape[-1] // LANES
    tq, tk = q_ref.shape[1], k_ref.shape[1]
    ku, qu = FLASH_KEY_UNIT, FLASH_QUERY_UNIT
    grows = D_HEAD_DIM + FLASH_ONES
    nheads = nblk * D_KV_HEADS
    k = k_ref[0]

    def logits_pass(idx):
        j, g = divmod(idx, D_KV_HEADS)
        qj = q_ref[0, :, j * LANES:(j + 1) * LANES]
        sel = lo if g == 0 else jnp.logical_not(lo)
        qm = jnp.where(sel, qj, jnp.zeros_like(qj))
        mcs = []
        for c in range(tq // qu):
            s = lax.dot_general(k, qm[c * qu:(c + 1) * qu], (((1,), (1,)), ((), ())),
                                preferred_element_type=F32)
            s_sc[idx, :, c * qu:(c + 1) * qu] = s
            mcs.append(jnp.max(s, axis=0, keepdims=True))
        m_prev = m_sc[idx, 0:1, :]
        m_new = jnp.maximum(m_prev, jnp.concatenate(mcs, axis=1))
        m_sc[idx, 0:1, :] = m_new
        return m_new, jnp.exp2(m_prev - m_new)

    def value_pass(idx, m_new, alpha):
        j, g = divmod(idx, D_KV_HEADS)
        rows = slice(g * grows, (g + 1) * grows)
        for c in range(tq // qu):
            qcols = slice(c * qu, (c + 1) * qu)
            pv = None
            for u in range(tk // ku):
                keys = slice(u * ku, (u + 1) * ku)
                p = jnp.exp2(s_sc[idx, keys, qcols] - m_new[:, qcols])
                d = jnp.dot(vt_ref[0, rows, keys], p.astype(BF16), preferred_element_type=F32)
                pv = d if pv is None else pv + d
            acc_sc[j, rows, qcols] = alpha[:, qcols] * acc_sc[j, rows, qcols] + pv

    stats = logits_pass(0)
    for idx in range(nheads):
        nxt = logits_pass(idx + 1) if idx + 1 < nheads else None
        value_pass(idx, *stats)
        stats = nxt

    @pl.when(kv == pl.num_programs(2) - 1)
    def _():
        for j in range(nblk):
            parts = []
            for g in range(D_KV_HEADS):
                num = acc_sc[j, g * grows:g * grows + D_HEAD_DIM, :]
                den = acc_sc[j, g * grows + D_HEAD_DIM:g * grows + D_HEAD_DIM + 1, :]
                parts.append(num / den)
            o_ref[0, :, j * LANES:(j + 1) * LANES] = (
                jnp.concatenate(parts, axis=0).T.astype(o_ref.dtype))


def _flash(q, k, vt, tq, tk):
    bsz, seq, qw = q.shape
    kw = k.shape[-1]
    vrows = vt.shape[1]
    return pl.pallas_call(
        _flash_kernel,
        out_shape=jax.ShapeDtypeStruct((bsz, seq, qw), BF16),
        grid=(bsz, seq // tq, seq // tk),
        in_specs=[pl.BlockSpec((1, tq, qw), lambda b, i, j: (b, i, 0)),
                  pl.BlockSpec((1, tk, kw), lambda b, i, j: (b, j, 0)),
                  pl.BlockSpec((1, vrows, tk), lambda b, i, j: (b, 0, j))],
        out_specs=pl.BlockSpec((1, tq, qw), lambda b, i, j: (b, i, 0)),
        scratch_shapes=[pltpu.VMEM((D_Q_HEADS, 8, tq), F32),
                        pltpu.VMEM((qw // LANES, vrows, tq), F32),
                        pltpu.VMEM((D_Q_HEADS, tk, tq), F32)],
        compiler_params=_cparams("parallel", "parallel", "arbitrary"),
        name="flash",
    )(q, k, vt)


def kernel(x, c, t5_bias, hgrn_lb_logits, ada_w, ada_b, norm_gains, ab_w_in, ab_w_out,
           hgrn_out_norm, cd_w_in, cd_w_out, qk_norm, ffn_w_in, ffn_w_out):
    bsz, seq, d = x.shape
    depth = ada_w.shape[0]
    mod = _ada_mod(c.astype(F32), ada_w, ada_b)
    perm = _head_perm()
    aw = A_HEADS * A_HEAD_DIM
    cw = C_GROUPS * C_WIDTH
    qw = D_Q_HEADS * D_HEAD_DIM
    tm_in = min(512, seq)
    tm = min(512, seq)
    for layer in range(depth):
        sh_m, sc_m, g_m, sh_f, sc_f, g_f = [mod[layer, :, i * d:(i + 1) * d] for i in range(6)]
        gains = norm_gains[layer]
        j = layer // 2
        if layer % 2 == 0:
            w_in = ab_w_in[j].astype(BF16)
            *qkv_cm, z = _inproj_cm(x, gains[0], sc_m, sh_m, w_in, 3 * aw, tm_in)
            branches = [_dilated_branch(cm, t5_bias, window, dil)
                        for cm, (window, dil) in zip(qkv_cm, DIL_CFG)]
            a_out = _dilated_merge(branches, tm)
            o_f, o_b = _hgrn(z, hgrn_lb_logits, layer, 0)
            g_col = 3 * B_HEADS * B_DK + B_HEADS * B_DV
            x = _out0(a_out, o_f, o_b, z, g_col, hgrn_out_norm[j], ab_w_out[j].astype(BF16),
                      x, gains[1], g_m, tm)
        else:
            w_full = cd_w_in[j]
            w_in = jnp.concatenate([w_full[:, :cw], w_full[:, cw:cw + qw][:, perm],
                                    w_full[:, cw + qw:]], axis=1).astype(BF16)
            u, qkv = _inproj(x, gains[0], sc_m, sh_m, w_in, (cw, w_in.shape[1] - cw), tm)
            c_out = _fourier_mixer(u)
            qn, kn, vn = _qkprep(qkv, qk_norm[j], tm)
            d_out = _flash(qn, kn, vn, min(FLASH_TQ, seq), min(FLASH_TK, seq))
            w_out_full = cd_w_out[j]
            w_out = jnp.concatenate([w_out_full[:cw], w_out_full[cw:][perm]], axis=0).astype(BF16)
            x = _out1(c_out, d_out, w_out, x, gains[1], g_m, tm)
        x = _ffn(x, gains[2], sc_f, sh_f, ffn_w_in[layer].astype(BF16), ffn_w_out[layer].astype(BF16),
                 gains[3], g_f, tm)
    return x
```

```python
import functools

import numpy as np
import jax
import jax.numpy as jnp
from jax import lax
from jax.experimental import pallas as pl
from jax.experimental.pallas import tpu as pltpu

F32 = jnp.float32
BF16 = jnp.bfloat16
LANES = 128
MXU_WIDTH = 256
VMEM_LIMIT_BYTES = 56 * 2**20
NEG_INF = -1e30
EPS = 1e-6

GRID_W = 64
A_HEADS = 8
A_HEAD_DIM = 64
DIL_CFG = ((128, 1), (512, 4), (2048, 16))
N_BUCKETS = 32
T5_MAX_DIST = 1024
B_HEADS = 4
B_DK = 128
B_DV = 128
C_GROUPS = 4
C_WIDTH = 128
D_Q_HEADS = 8
D_KV_HEADS = 2
D_HEAD_DIM = 64
ROPE_THETA = 10000.0

DIL_TQ = 128
DIL_TILE = 256
HGRN_BLOCK = 256
HGRN_BCAST_MIN = 8
FLASH_TQ = 1024
FLASH_TK = 1024
FLASH_KEY_UNIT = 256
FLASH_QUERY_UNIT = 256
FLASH_AHEAD = 1
FLASH_ONES = 16
FFT_N2 = 128
FFT_BATCH = 8
FFT_PAD = 8
LOG2E = 1.4426950408889634


def _cparams(*sem):
    return pltpu.CompilerParams(dimension_semantics=sem, vmem_limit_bytes=VMEM_LIMIT_BYTES)


def _const_spec(shape):
    nd = len(shape)
    return pl.BlockSpec(shape, lambda *_: (0,) * nd, pipeline_mode=pl.Buffered(1))


def _sigmoid(x):
    return 1.0 / (1.0 + jnp.exp(-x))


def _dot(a, b):
    return jnp.dot(a.astype(BF16), b.astype(BF16), preferred_element_type=F32)


def _dot_nt(a, b):
    return lax.dot_general(a.astype(BF16), b.astype(BF16), (((1,), (1,)), ((), ())),
                           preferred_element_type=F32)


def _split2(a):
    hi = a.astype(BF16)
    lo = (a - hi.astype(F32)).astype(BF16)
    return hi, lo


def _split3(a):
    a1 = a.astype(BF16)
    r = a - a1.astype(F32)
    a2 = r.astype(BF16)
    a3 = (r - a2.astype(F32)).astype(BF16)
    return a1, a2, a3


def _dot_tab(tab_hi, tab_lo, x, *, tab_left):
    x_hi, x_lo = _split2(x)
    if tab_left:
        d = lambda t, v: jnp.dot(t, v, preferred_element_type=F32)
    else:
        d = lambda t, v: jnp.dot(v, t, preferred_element_type=F32)
    return d(tab_hi, x_hi) + (d(tab_hi, x_lo) + d(tab_lo, x_hi))


def _rms(x, gain):
    ms = jnp.mean(x * x, axis=-1, keepdims=True)
    return x * lax.rsqrt(ms + EPS) * gain


def _np_split2(t):
    t = np.asarray(t, np.float32)
    hi = jnp.asarray(t, F32).astype(BF16)
    lo = (jnp.asarray(t, F32) - hi.astype(F32)).astype(BF16)
    return hi, lo


def _mod_kernel(c_ref, w_ref, b_ref, o_ref):
    c = c_ref[...]
    o_ref[0] = _dot(c * _sigmoid(c), w_ref[0]) + b_ref[0]


def _ada_mod(c, ada_w, ada_b):
    depth, d, n6 = ada_w.shape
    bsz = c.shape[0]
    rows = 8
    cp = jnp.zeros((rows, d), F32).at[:bsz].set(c)
    tn = n6 // 4
    out = pl.pallas_call(
        _mod_kernel,
        out_shape=jax.ShapeDtypeStruct((depth, rows, n6), F32),
        grid=(depth, n6 // tn),
        in_specs=[pl.BlockSpec((rows, d), lambda l, j: (0, 0)),
                  pl.BlockSpec((1, d, tn), lambda l, j: (l, 0, j)),
                  pl.BlockSpec((1, 1, tn), lambda l, j: (l, 0, j))],
        out_specs=pl.BlockSpec((1, rows, tn), lambda l, j: (l, 0, j)),
        compiler_params=_cparams("parallel", "parallel"),
        name="ada_mod",
    )(cp, ada_w, ada_b.reshape(depth, 1, n6))
    return out[:, :bsz]


def _inproj_kernel(x_ref, gain_ref, sc_ref, sh_ref, w_ref, *o_refs):
    h = _rms(x_ref[0], gain_ref[...]) * (1.0 + sc_ref[0]) + sh_ref[0]
    z = _dot(h, w_ref[...])
    off = 0
    for o_ref in o_refs:
        if len(o_ref.shape) == 4:
            for g in range(o_ref.shape[1]):
                o_ref[0, g] = z[:, off:off + LANES]
                off += LANES
        else:
            n = o_ref.shape[-1]
            o_ref[0] = z[:, off:off + n].astype(o_ref.dtype)
            off += n


def _inproj(x, gain, sc, sh, w_bf16, splits, tm):
    bsz, seq, d = x.shape
    n = w_bf16.shape[1]
    assert sum(splits) == n
    vec = pl.BlockSpec((1, 1, d), lambda b, i: (b, 0, 0))
    g0 = splits[0] // LANES
    return pl.pallas_call(
        _inproj_kernel,
        out_shape=tuple([jax.ShapeDtypeStruct((bsz, g0, seq, LANES), F32)]
                        + [jax.ShapeDtypeStruct((bsz, seq, s), F32) for s in splits[1:]]),
        grid=(bsz, seq // tm),
        in_specs=[pl.BlockSpec((1, tm, d), lambda b, i: (b, i, 0)),
                  _const_spec((1, d)), vec, vec, _const_spec((d, n))],
        out_specs=tuple([pl.BlockSpec((1, g0, tm, LANES), lambda b, i: (b, 0, i, 0))]
                        + [pl.BlockSpec((1, tm, s), lambda b, i: (b, i, 0)) for s in splits[1:]]),
        compiler_params=_cparams("parallel", "parallel"),
        name="inproj",
    )(x, gain.reshape(1, d), sc.reshape(bsz, 1, d), sh.reshape(bsz, 1, d), w_bf16)


def _inproj_cm_kernel(x_ref, gain_ref, sc_ref, sh_ref, w_ref, *refs):
    cm_refs, rest_ref, zs_sc, zc_sc = refs[:-3], refs[-3], refs[-2], refs[-1]
    h = _rms(x_ref[0], gain_ref[...]) * (1.0 + sc_ref[0]) + sh_ref[0]
    z = _dot(h, w_ref[...])
    nblk, tm, _ = zs_sc.shape
    rest_ref[0] = z[:, nblk * LANES:]
    nq = A_HEADS * A_HEAD_DIM // LANES
    for c in range(nblk):
        blk = z[:, c * LANES:(c + 1) * LANES]
        zs_sc[c] = blk * (A_HEAD_DIM ** -0.5 * LOG2E) if c < nq else blk
    src, sd = zs_sc, 1
    for level, (cm_ref, (_, dil)) in enumerate(zip(cm_refs, DIL_CFG)):
        step, n = dil // sd, tm // dil
        keep = dil > 1 and level + 1 < len(DIL_CFG)
        for rs in range(sd):
            for cc in range(step):
                r = rs + sd * cc
                for c in range(nblk):
                    rows = src[c, pl.ds(rs * (tm // sd) + cc, n, stride=step), :]
                    cm_ref[0, r, :, c * LANES:(c + 1) * LANES] = rows.astype(BF16)
                    if keep:
                        zc_sc[c, r * n:(r + 1) * n, :] = rows
        if keep:
            src, sd = zc_sc, dil


def _inproj_cm(x, gain, sc, sh, w_bf16, na, tm):
    bsz, seq, d = x.shape
    n = w_bf16.shape[1]
    vec = pl.BlockSpec((1, 1, d), lambda b, i: (b, 0, 0))
    dils = [dl for _, dl in DIL_CFG]
    assert dils[0] == 1 and all(b % a == 0 for a, b in zip(dils, dils[1:]))
    return pl.pallas_call(
        _inproj_cm_kernel,
        out_shape=tuple([jax.ShapeDtypeStruct((bsz, dl, seq // dl, na), BF16) for dl in dils]
                        + [jax.ShapeDtypeStruct((bsz, seq, n - na), F32)]),
        grid=(bsz, seq // tm),
        in_specs=[pl.BlockSpec((1, tm, d), lambda b, i: (b, i, 0)),
                  _const_spec((1, d)), vec, vec, _const_spec((d, n))],
        out_specs=tuple([pl.BlockSpec((1, dl, tm // dl, na), lambda b, i: (b, 0, i, 0)) for dl in dils]
                        + [pl.BlockSpec((1, tm, n - na), lambda b, i: (b, i, 0))]),
        scratch_shapes=[pltpu.VMEM((na // LANES, tm, LANES), F32),
                        pltpu.VMEM((na // LANES, tm, LANES), F32)],
        compiler_params=_cparams("parallel", "parallel"),
        name="inproj_cm",
    )(x, gain.reshape(1, d), sc.reshape(bsz, 1, d), sh.reshape(bsz, 1, d), w_bf16)


def _t5_buckets(rel):
    half = N_BUCKETS // 2
    max_exact = half // 2
    n = np.abs(rel)
    large = max_exact + (np.log(np.maximum(n, 1) / max_exact) / np.log(T5_MAX_DIST / max_exact)
                         * (half - max_exact)).astype(np.int32)
    large = np.minimum(large, half - 1)
    return (np.where(rel > 0, half, 0) + np.where(n < max_exact, n, large)).astype(np.int32)


def _dil_bias(t5_bias, window, dil, tq):
    half = (window // 2) // dil
    assert half == tq // 2
    rel = np.arange(2 * tq)[None, :] - half - np.arange(tq)[:, None]
    inside = np.abs(rel) <= half
    buckets = _t5_buckets(np.where(inside, rel, 0) * dil)
    onehot =jnp.asarray(np.eye(N_BUCKETS, dtype=np.float32)[buckets])
    bias = jnp.einsum("qkn,nh->hqk", onehot, t5_bias.astype(F32), precision=lax.Precision.HIGHEST)
    return jnp.where(jnp.asarray(inside)[None], bias * LOG2E, NEG_INF)


def _dil_kernel(q_ref, kp_ref, kc_ref, kn_ref, vp_ref, vc_ref, vn_ref, bias_ref, o_ref, lse_ref,
                *, class_len):
    i = pl.program_id(2)
    sub, hq, tile = DIL_TQ, DIL_TQ // 2, DIL_TILE
    kwin = jnp.concatenate([kp_ref[0, 0], kc_ref[0, 0], kn_ref[0, 0]], axis=0)
    vwin = jnp.concatenate([vp_ref[0, 0], vc_ref[0, 0], vn_ref[0, 0]], axis=0)
    lane = lax.broadcasted_iota(jnp.int32, (1, LANES), 1)
    lo = lane < A_HEAD_DIM
    for jt in range(tile // sub):
        qrows = slice(jt * sub, (jt + 1) * sub)
        krows = slice(jt * sub, jt * sub + 2 * sub)
        kpos = i * tile + jt * sub - hq + lax.broadcasted_iota(jnp.int32, (1, 2 * sub), 1)
        valid = jnp.logical_and(kpos >= 0, kpos < class_len)
        lse_all = jnp.zeros((sub, LANES), F32)
        for j in range(A_HEADS // 2):
            cols = slice(j * LANES, (j + 1) * LANES)
            qj = q_ref[0, 0, qrows, cols]
            kj = kwin[krows, cols]
            vj = vwin[krows, cols]
            zero = jnp.zeros_like(qj)
            q2 = jnp.concatenate([jnp.where(lo, qj, zero), jnp.where(lo, zero, qj)], axis=0)
            s = lax.dot_general(q2, kj, (((1,), (1,)), ((), ())), preferred_element_type=F32)
            s = jnp.where(valid, s + bias_ref[j], NEG_INF)
            m = jnp.max(s, axis=-1, keepdims=True)
            p = jnp.exp2(s - m)
            l = jnp.sum(p, axis=-1, keepdims=True)
            o2 = jnp.dot(p.astype(BF16), vj, preferred_element_type=F32) * (1.0 / l)
            lse = m + jnp.log2(l)
            lse_all = jnp.where(lane == 2 * j, lse[:sub], lse_all)
            lse_all = jnp.where(lane == 2 * j + 1, lse[sub:], lse_all)
            o_ref[0, 0, qrows, cols] = jnp.where(lo, o2[:sub], o2[sub:])
        lse_ref[0, 0, qrows, :] = lse_all


def _dilated_branch(qkv_cm, t5_bias, window, dil):
    bsz, _, cl, width = qkv_cm.shape
    aw = A_HEADS * A_HEAD_DIM
    tile, hq = DIL_TILE, DIL_TQ // 2
    nt = cl // tile
    per = tile // hq
    nh = cl // hq

    def cur(col):
        return pl.BlockSpec((1, 1, tile, aw), lambda b, r, i: (b, r, i, col))

    def prev(col):
        return pl.BlockSpec((1, 1, hq, aw), lambda b, r, i: (b, r, jnp.maximum(i * per - 1, 0), col))

    def nxt(col):
        return pl.BlockSpec((1, 1, hq, aw),
                            lambda b, r, i: (b, r, jnp.minimum((i + 1) * per, nh - 1), col))

    return pl.pallas_call(
        functools.partial(_dil_kernel, class_len=cl),
        out_shape=(jax.ShapeDtypeStruct((bsz, dil, cl, aw), F32),
                   jax.ShapeDtypeStruct((bsz, dil, cl, LANES), F32)),
        grid=(bsz, dil, nt),
        in_specs=[cur(0), prev(1), cur(1), nxt(1), prev(2), cur(2), nxt(2),
                  _const_spec((A_HEADS // 2, 2 * DIL_TQ, 2 * DIL_TQ))],
        out_specs=(pl.BlockSpec((1, 1, tile, aw), lambda b, r, i: (b, r, i, 0)),
                   pl.BlockSpec((1, 1, tile, LANES), lambda b, r, i: (b, r, i, 0))),
        compiler_params=_cparams("parallel", "parallel", "parallel"),
        name=f"dilated_d{dil}",
    )(*([qkv_cm] * 7),
      _dil_bias(t5_bias, window, dil, DIL_TQ).reshape(A_HEADS // 2, 2 * DIL_TQ, 2 * DIL_TQ))


def _dilmerge_kernel(*refs):
    nbr = len(DIL_CFG)
    a_refs, l_refs = refs[:nbr], refs[nbr:2 * nbr]
    e_ref, o_ref = refs[2 * nbr], refs[2 * nbr + 1]
    a_scs, l_scs = refs[2 * nbr + 2:3 * nbr + 2], refs[3 * nbr + 2:4 * nbr + 2]
    tm = o_ref.shape[1]
    accs, lses = [], []
    for (_, dil), a_ref, l_ref, a_sc, l_sc in zip(DIL_CFG, a_refs, l_refs, a_scs, l_scs):
        nblk = a_sc.shape[0]
        for r in range(dil):
            rows = pl.ds(r, tm // dil, stride=dil)
            for c in range(nblk):
                a_sc[c, rows, :] = a_ref[0, r, :, c * LANES:(c + 1) * LANES]
            l_sc[rows, :] = l_ref[0, r]
        accs.append(jnp.concatenate([a_sc[c] for c in range(nblk)], axis=1))
        lses.append(l_sc[...])
    mx = functools.reduce(jnp.maximum, lses)
    ws = [jnp.exp2(x - mx) for x in lses]
    tot = functools.reduce(jnp.add, ws)
    out = None
    for w, a in zip(ws, accs):
        w_hi, w_lo = _split2(w / tot)
        wide = (jnp.dot(w_hi, e_ref[...], preferred_element_type=F32)
                + jnp.dot(w_lo, e_ref[...], preferred_element_type=F32))
        out = wide * a if out is None else out + wide * a
    o_ref[0] = out.astype(o_ref.dtype)


def _dilated_merge(branch_outs, tm):
    bsz, _, _, aw = branch_outs[0][0].shape
    seq = branch_outs[0][0].shape[1] * branch_outs[0][0].shape[2]
    expand = np.zeros((LANES, aw), np.float32)
    for h in range(A_HEADS):
        expand[h, h * A_HEAD_DIM:(h + 1) * A_HEAD_DIM] = 1.0
    dils = [d for _, d in DIL_CFG]
    in_specs = ([pl.BlockSpec((1, d, tm // d, aw), lambda b, i: (b, 0, i, 0)) for d in dils]
                + [pl.BlockSpec((1, d, tm // d, LANES), lambda b, i: (b, 0, i, 0)) for d in dils]
                + [_const_spec(expand.shape)])
    return pl.pallas_call(
        _dilmerge_kernel,
        out_shape=jax.ShapeDtypeStruct((bsz, seq, aw), BF16),
        grid=(bsz, seq // tm),
        in_specs=in_specs,
        out_specs=pl.BlockSpec((1, tm, aw), lambda b, i: (b, i, 0)),
        scratch_shapes=([pltpu.VMEM((aw // LANES, tm, LANES), F32) for _ in dils]
                        + [pltpu.VMEM((tm, LANES), F32) for _ in dils]),
        compiler_params=_cparams("parallel", "parallel"),
        name="dilmerge",
    )(*[o for o, _ in branch_outs], *[l for _, l in branch_outs], jnp.asarray(expand, BF16))


def _hgrn_consts(reverse):
    t = HGRN_BLOCK
    r = np.arange(t)
    u = r[None, :]
    row = r[:, None]
    nmats = [(u >= row) if reverse else (u <= row)]
    masks = []
    m = t // 2
    while m >= 1:
        grp = r // (2 * m)
        in_first = (r % (2 * m)) < m
        same = grp[:, None] == grp[None, :]
        if reverse:
            beta = (grp * 2 * m + m)[:, None]
            n = np.where(in_first[:, None], (u >= row) & (u < beta), (u >= beta) & (u < row))
            mask = same & in_first[:, None] & ~in_first[None, :]
        else:
            beta = (grp * 2 * m + m - 1)[:, None]
            n = np.where(in_first[:, None], (u > row) & (u <= beta), (u > beta) & (u <= row))
            mask = same & ~in_first[:, None] & in_first[None, :]
        if m < HGRN_BCAST_MIN:
            nmats.append(n)
        masks.append(mask)
        m //= 2
    masks.append(np.eye(t, dtype=bool))
    nmat = jnp.asarray(np.concatenate(nmats, axis=0), F32).astype(BF16)
    return nmat, jnp.asarray(np.stack(masks), F32)


def _hgrn_block(q, f, v, lb, st, nmat_ref, mask_ref, reverse):
    t = HGRN_BLOCK
    nlev = mask_ref.shape[0] - 1
    qs = q * (B_DK ** -0.5)
    fa = lb + (1.0 - lb) * _sigmoid(f)
    kk = 1.0 - fa
    g_hi, g_lo = _split2(jnp.log(fa))
    ex = jnp.dot(nmat_ref[...], jnp.concatenate([g_hi, g_lo], axis=1), preferred_element_type=F32)
    ex = ex[:, :B_DK] + ex[:, B_DK:]
    b = ex[:t]
    btot = b[0:1] if reverse else b[t - 1:t]
    a = mask_ref[nlev] * _dot_nt(qs, kk)
    fine = 1
    for l in range(nlev):
        m = t >> (l + 1)
        if m >= HGRN_BCAST_MIN:
            ref = jnp.concatenate(
                [jnp.broadcast_to(b[beta:beta + 1], (2 * m, B_DK))
                 for beta in range(m if reverse else m - 1, t, 2 * m)], axis=0)
            e = jnp.exp(-jnp.abs(b - ref))
        else:
            e = jnp.exp(ex[fine * t:(fine + 1) * t])
            fine += 1
        a = a + mask_ref[l] * _dot_nt(qs * e, kk * e)
    out = _dot(a, v) + _dot_nt(qs * jnp.exp(b), st)
    khat = (kk * jnp.exp(btot - b)).astype(BF16)
    st_new = st * jnp.exp(btot) + jnp.dot(v.T.astype(BF16), khat, preferred_element_type=F32)
    return out, st_new


def _hgrn_lb(lg_ref, layer):
    lg = [lg_ref[l, 0] for l in range(lg_ref.shape[0])]
    mx = functools.reduce(jnp.maximum, lg)
    e = [jnp.exp(x - mx) for x in lg]
    return functools.reduce(jnp.add, e[:layer + 1]) / functools.reduce(jnp.add, e)


def _hgrn_kernel(qf_ref, ff_ref, vf_ref, qb_ref, fb_ref, vb_ref, lgf_ref, lgb_ref,
                 nf_ref, mf_ref, nb_ref, mb_ref, of_ref, ob_ref, sf_sc, sb_sc, *, layer):
    @pl.when(pl.program_id(2) == 0)
    def _():
        sf_sc[...] = jnp.zeros_like(sf_sc)
        sb_sc[...] = jnp.zeros_like(sb_sc)

    o, st = _hgrn_block(qf_ref[0], ff_ref[0], vf_ref[0], _hgrn_lb(lgf_ref, layer), sf_sc[...],
                        nf_ref, mf_ref, False)
    of_ref[0] = o
    sf_sc[...] = st
    o, st = _hgrn_block(qb_ref[0], fb_ref[0], vb_ref[0], _hgrn_lb(lgb_ref, layer), sb_sc[...],
                        nb_ref, mb_ref, True)
    ob_ref[0] = o
    sb_sc[...] = st


def _hgrn(z, lb_logits, layer, col0):
    bsz, seq, _ = z.shape
    t = HGRN_BLOCK
    nb = seq // t
    c0 = col0 // LANES
    hw = B_HEADS

    def zspec(group, rev):
        return pl.BlockSpec(
            (1, t, LANES),
            lambda b, h, j: (b, (nb - 1 - j) if rev else j, c0 + group * hw + h))

    def lgspec(direction):
        return pl.BlockSpec((lb_logits.shape[0], 1, 1, LANES),
                            lambda b, h, j: (0, direction * hw + h, 0, 0))

    nf, mf = _hgrn_consts(False)
    nbw, mbw = _hgrn_consts(True)
    lg = lb_logits.astype(F32).reshape(lb_logits.shape[0], 2 * hw, 1, LANES)
    o_shape = jax.ShapeDtypeStruct((bsz, seq, hw * B_DV), F32)
    return pl.pallas_call(
        functools.partial(_hgrn_kernel, layer=layer),
        out_shape=(o_shape, o_shape),
        grid=(bsz, hw, nb),
        in_specs=[zspec(0, False), zspec(1, False), zspec(3, False),
                  zspec(0, True), zspec(2, True), zspec(3, True),
                  lgspec(0), lgspec(1),
                  _const_spec(nf.shape), _const_spec(mf.shape),
                  _const_spec(nbw.shape), _const_spec(mbw.shape)],
        out_specs=(pl.BlockSpec((1, t, LANES), lambda b, h, j: (b, j, h)),
                   pl.BlockSpec((1, t, LANES), lambda b, h, j: (b, nb - 1 - j, h))),
        scratch_shapes=[pltpu.VMEM((B_DV, B_DK), F32), pltpu.VMEM((B_DV, B_DK), F32)],
        compiler_params=_cparams("parallel", "parallel", "arbitrary"),
        name="hgrn",
    )(z, z, z, z, z, z, lg, lg, nf, mf, nbw, mbw)


def _residual_epilogue(y, x_ref, gain_ref, gate_ref, o_ref):
    o_ref[0] = x_ref[0] + gate_ref[0] * _rms(y, gain_ref[...])


def _out0_kernel(a_ref, of_ref, ob_ref, g_ref, on_ref, w_ref, x_ref, gain_ref, gate_ref, o_ref):
    o = of_ref[0] + ob_ref[0]
    g = g_ref[0]
    parts = [_rms(o[:, h * B_DV:(h + 1) * B_DV], on_ref[...]) for h in range(B_HEADS)]
    bn = jnp.concatenate(parts, axis=-1) * (g * _sigmoid(g))
    na = a_ref.shape[-1]
    y = _dot(a_ref[0], w_ref[:na, :]) + _dot(bn, w_ref[na:, :])
    _residual_epilogue(y, x_ref, gain_ref, gate_ref, o_ref)


def _out1_kernel(c_ref, d_ref, w_ref, x_ref, gain_ref, gate_ref, o_ref):
    c = jnp.concatenate([c_ref[0, g] for g in range(c_ref.shape[1])], axis=1)
    nc = c.shape[-1]
    y = _dot(c, w_ref[:nc, :]) + _dot(d_ref[0], w_ref[nc:, :])
    _residual_epilogue(y, x_ref, gain_ref, gate_ref, o_ref)


def _row_spec(tm, width, col=0):
    return pl.BlockSpec((1, tm, width), lambda b, i: (b, i, col))


def _out0(a_out, o_f, o_b, z, g_col, out_norm, w_bf16, x, gain, gate, tm):
    bsz, seq, d = x.shape
    wv = B_HEADS * B_DV
    vec = pl.BlockSpec((1, 1, d), lambda b, i: (b, 0, 0))
    return pl.pallas_call(
        _out0_kernel,
        out_shape=jax.ShapeDtypeStruct(x.shape, F32),
        grid=(bsz, seq // tm),
        in_specs=[_row_spec(tm, a_out.shape[-1]), _row_spec(tm, wv), _row_spec(tm, wv),
                  _row_spec(tm, wv, g_col // wv), _const_spec((1, B_DV)),
                  _const_spec(w_bf16.shape), _row_spec(tm, d), _const_spec((1, d)), vec],
        out_specs=_row_spec(tm, d),
        compiler_params=_cparams("parallel", "parallel"),
        name="out0",
    )(a_out, o_f, o_b, z, out_norm.reshape(1, B_DV), w_bf16, x, gain.reshape(1, d),
      gate.reshape(bsz, 1, d))


def _out1(c_out, d_out, w_bf16, x, gain, gate, tm):
    bsz, seq, d = x.shape
    vec = pl.BlockSpec((1, 1, d), lambda b, i: (b, 0, 0))
    return pl.pallas_call(
        _out1_kernel,
        out_shape=jax.ShapeDtypeStruct(x.shape, F32),
        grid=(bsz, seq // tm),
        in_specs=[pl.BlockSpec((1, c_out.shape[1], tm, c_out.shape[3]), lambda b, i: (b, 0, i, 0)),
                  _row_spec(tm, d_out.shape[-1]),
                  _const_spec(w_bf16.shape), _row_spec(tm, d), _const_spec((1, d)), vec],
        out_specs=_row_spec(tm, d),
        compiler_params=_cparams("parallel", "parallel"),
        name="out1",
    )(c_out, d_out, w_bf16, x, gain.reshape(1, d), gate.reshape(bsz, 1, d))


def _ffn_kernel(x_ref, g1_ref, sc_ref, sh_ref, wi_ref, wo_ref, g2_ref, gate_ref, o_ref, *, nchunk):
    x = x_ref[0]
    h = (_rms(x, g1_ref[...]) * (1.0 + sc_ref[0]) + sh_ref[0]).astype(BF16)
    hidden = wo_ref.shape[0]
    ck = hidden // nchunk
    y = None
    for c in range(nchunk):
        gt = jnp.dot(h, wi_ref[:, c * ck:(c + 1) * ck], preferred_element_type=F32)
        up = jnp.dot(h, wi_ref[:, hidden + c * ck:hidden + (c + 1) * ck], preferred_element_type=F32)
        part = _dot(gt * _sigmoid(gt) * up, wo_ref[c * ck:(c + 1) * ck, :])
        y = part if y is None else y + part
    o_ref[0] = x + gate_ref[0] * _rms(y, g2_ref[...])


def _ffn(x, g1, sc, sh, wi_bf16, wo_bf16, g2, gate, tm):
    bsz, seq, d = x.shape
    vec = pl.BlockSpec((1, 1, d), lambda b, i: (b, 0, 0))
    hidden = wo_bf16.shape[0]
    nchunk = 2 if (hidden // 2) % MXU_WIDTH == 0 else 1
    return pl.pallas_call(
        functools.partial(_ffn_kernel, nchunk=nchunk),
        out_shape=jax.ShapeDtypeStruct(x.shape, F32),
        grid=(bsz, seq // tm),
        in_specs=[_row_spec(tm, d), _const_spec((1, d)), vec, vec,
                  _const_spec(wi_bf16.shape), _const_spec(wo_bf16.shape),
                  _const_spec((1, d)), vec],
        out_specs=_row_spec(tm, d),
        compiler_params=_cparams("parallel", "parallel"),
        name="ffn",
    )(x, g1.reshape(1, d), sc.reshape(bsz, 1, d), sh.reshape(bsz, 1, d), wi_bf16, wo_bf16,
      g2.reshape(1, d), gate.reshape(bsz, 1, d))


def _fft_kernel(u_ref, f1_ref, twc_ref, tws_ref, f2_ref, fw_ref, o_ref, u_sc, p_sc, y_sc,
                *, scale, n1, n2):
    pu = n2 + FFT_PAD
    pp = 2 * n1 + FFT_PAD
    py = n1 + FFT_PAD
    f1 = f1_ref[...].astype(BF16)
    f2 = f2_ref[...].astype(BF16)
    fw = fw_ref[...].astype(BF16)
    for i1 in range(n1):
        u_sc[i1 * pu:i1 * pu + n2, :] = u_ref[0, 0, i1 * n2:(i1 + 1) * n2, :]

    nb = FFT_BATCH

    def stage1(blk, carry):
        i2s = [blk * nb + j for j in range(nb)]
        x = jnp.concatenate([u_sc[pl.ds(i2, n1, stride=pu), :] for i2 in i2s], axis=1)
        p = jnp.dot(f1, x.astype(BF16), preferred_element_type=F32)
        for j, i2 in enumerate(i2s):
            p_sc[pl.ds(pl.multiple_of(i2 * pp, 8), 2 * n1), :] = p[:, j * C_WIDTH:(j + 1) * C_WIDTH]
        return carry

    lax.fori_loop(0, n2 // nb, stage1, 0)

    def stage2(blk, carry):
        k1s = [blk * nb + j for j in range(nb)]
        qr, qi = [], []
        for k1 in k1s:
            tc = twc_ref[k1]
            ts = tws_ref[k1]
            pr = p_sc[pl.ds(k1, n2, stride=pp), :]
            pim = p_sc[pl.ds(n1 + k1, n2, stride=pp), :]
            qr.append(pr * tc + pim * ts)
            qi.append(pim * tc - pr * ts)
        q = jnp.concatenate([jnp.concatenate(qr, axis=1), jnp.concatenate(qi, axis=1)], axis=0)
        xx = jnp.dot(f2, q.astype(BF16), preferred_element_type=F32)
        xg = jnp.concatenate(
            [jnp.concatenate([xx[:n2, j * C_WIDTH:(j + 1) * C_WIDTH],
                              xx[n2:, j * C_WIDTH:(j + 1) * C_WIDTH]], axis=1) for j in range(nb)],
            axis=0)
        y = jnp.dot(xg.astype(BF16), fw, preferred_element_type=F32) * scale
        for j, k1 in enumerate(k1s):
            y_sc[pl.ds(k1, n2, stride=py), :] = y[j * n2:(j + 1) * n2]
        return carry

    lax.fori_loop(0, n1 // nb, stage2, 0)
    for k2 in range(n2):
        o_ref[0, 0, k2 * n1:(k2 + 1) * n1, :] = y_sc[k2 * py:k2 * py + n1, :]


def _fourier_mixer(u):
    bsz, ngroups, seq, width = u.shape
    n2 = FFT_N2
    n1 = seq // n2
    assert n1 * n2 == seq and width == C_WIDTH and n1 % 8 == 0
    a1 = 2.0 * np.pi * np.outer(np.arange(n1), np.arange(n1)) / n1
    f1 = np.concatenate([np.cos(a1), -np.sin(a1)], axis=0)
    a2 = 2.0 * np.pi * np.outer(np.arange(n2), np.arange(n2)) / n2
    c2, s2 = np.cos(a2), np.sin(a2)
    f2 = np.block([[c2, s2], [-s2, c2]])
    aw = 2.0 * np.pi * np.outer(np.arange(C_WIDTH), np.arange(C_WIDTH)) / C_WIDTH
    fw = np.concatenate([np.cos(aw), np.sin(aw)], axis=0)
    at = np.repeat((2.0 * np.pi * np.outer(np.arange(n1), np.arange(n2)) / seq)[:, :, None],
                   C_WIDTH, axis=2)
    consts = (jnp.asarray(f1, F32), jnp.asarray(np.cos(at), F32), jnp.asarray(np.sin(at), F32),
              jnp.asarray(f2, F32), jnp.asarray(fw, F32))
    blk = pl.BlockSpec((1, 1, seq, C_WIDTH), lambda b, g: (b, g, 0, 0))
    return pl.pallas_call(
        functools.partial(_fft_kernel, scale=float(1.0 / np.sqrt(seq * C_WIDTH)), n1=n1, n2=n2),
        out_shape=jax.ShapeDtypeStruct(u.shape, F32),
        grid=(bsz, ngroups),
        in_specs=[blk] + [_const_spec(c.shape) for c in consts],
        out_specs=blk,
        scratch_shapes=[pltpu.VMEM((n1 * (n2 + FFT_PAD), C_WIDTH), F32),
                        pltpu.VMEM((n2 * (2 * n1 + FFT_PAD), C_WIDTH), F32),
                        pltpu.VMEM((n2 * (n1 + FFT_PAD), C_WIDTH), F32)],
        compiler_params=_cparams("parallel", "parallel"),
        name="fft",
    )(u, *consts)


def _head_perm():
    rep = D_Q_HEADS // D_KV_HEADS
    cols = []
    for j in range(rep):
        for g in range(D_KV_HEADS):
            h = g * rep + j
            cols.extend(range(h * D_HEAD_DIM, (h + 1) * D_HEAD_DIM))
    return np.asarray(cols, np.int32)


def _rope_tables(seq):
    rows = seq // GRID_W
    row = jnp.repeat(jnp.arange(rows, dtype=F32), GRID_W)
    col = jnp.tile(jnp.arange(GRID_W, dtype=F32), rows)
    axis_dim = D_HEAD_DIM // 2
    inv_freq = jnp.power(ROPE_THETA, -jnp.arange(0, axis_dim, 2, dtype=F32) / axis_dim)
    ang_r = row[:, None] * inv_freq[None, :]
    ang_c = col[:, None] * inv_freq[None, :]
    cr, sr, cc, sc = jnp.cos(ang_r), jnp.sin(ang_r), jnp.cos(ang_c), jnp.sin(ang_c)
    cos = jnp.concatenate([cr, cr, cc, cc], axis=1)
    sin = jnp.concatenate([-sr, sr, -sc, sc], axis=1)
    reps = LANES // D_HEAD_DIM
    return jnp.tile(cos, (1, reps)), jnp.tile(sin, (1, reps))


def _qkprep_kernel(q_ref, k_ref, v_ref, cos_ref, sin_ref, bd_h_ref, bd_l_ref, gq_ref, gk_ref,
                   qo_ref, ko_ref, vo_ref):
    cos = cos_ref[...]
    sin = sin_ref[...]
    quarter = D_HEAD_DIM // 4
    lane = lax.broadcasted_iota(jnp.int32, (1, LANES), 1)
    first_of_pair = (lane // quarter) % 2 == 0

    def norm_rope(x, gain, scale):
        ms = _dot_tab(bd_h_ref[...], bd_l_ref[...], x * x, tab_left=False)
        xn = x * lax.rsqrt(ms + EPS) * gain
        partner = jnp.where(first_of_pair, pltpu.roll(xn, LANES - quarter, 1),
                            pltpu.roll(xn, quarter, 1))
        return ((xn * cos + partner * sin) * scale).astype(BF16)

    for j in range(q_ref.shape[-1] // LANES):
        cols = slice(j * LANES, (j + 1) * LANES)
        qo_ref[0, :, cols] = norm_rope(q_ref[0, :, cols], gq_ref[...], D_HEAD_DIM ** -0.5 * LOG2E)
    ko_ref[0] = norm_rope(k_ref[0], gk_ref[...], 1.0)
    vt = v_ref[0].T
    ones = jnp.ones((FLASH_ONES, vt.shape[1]), F32)
    vo_ref[0] = jnp.concatenate(
        [piece for g in range(D_KV_HEADS)
         for piece in (vt[g * D_HEAD_DIM:(g + 1) * D_HEAD_DIM], ones)], axis=0).astype(BF16)


def _qkprep(qkv, qk_norm_j, tm):
    bsz, seq, _ = qkv.shape
    qw = D_Q_HEADS * D_HEAD_DIM
    kw = D_KV_HEADS * D_HEAD_DIM
    assert kw == LANES
    vrows = D_KV_HEADS * (D_HEAD_DIM + FLASH_ONES)
    cos, sin = _rope_tables(seq)
    bd = np.kron(np.eye(LANES // D_HEAD_DIM), np.full((D_HEAD_DIM, D_HEAD_DIM), 1.0 / D_HEAD_DIM))
    bd_h, bd_l = _np_split2(bd)
    reps = LANES // D_HEAD_DIM
    gq = jnp.tile(qk_norm_j[0].astype(F32), reps).reshape(1, LANES)
    gk = jnp.tile(qk_norm_j[1].astype(F32), reps).reshape(1, LANES)
    tab = pl.BlockSpec((tm, LANES), lambda b, i: (i, 0))
    return pl.pallas_call(
        _qkprep_kernel,
        out_shape=(jax.ShapeDtypeStruct((bsz, seq, qw), BF16),
                   jax.ShapeDtypeStruct((bsz, seq, kw), BF16),
                   jax.ShapeDtypeStruct((bsz, vrows, seq), BF16)),
        grid=(bsz, seq // tm),
        in_specs=[_row_spec(tm, qw, 0), _row_spec(tm, kw, qw // kw), _row_spec(tm, kw, qw // kw + 1),
                  tab, tab, _const_spec(bd_h.shape), _const_spec(bd_l.shape),
                  _const_spec((1, LANES)), _const_spec((1, LANES))],
        out_specs=(_row_spec(tm, qw), _row_spec(tm, kw),
                   pl.BlockSpec((1, vrows, tm), lambda b, i: (b, 0, i))),
        compiler_params=_cparams("parallel", "parallel"),
        name="qkprep",
    )(qkv, qkv, qkv, cos, sin, bd_h, bd_l, gq, gk)


def _flash_kernel(q_ref, k_ref, vt_ref, o_ref, m_sc, acc_sc, s_sc):
    kv = pl.program_id(2)

    @pl.when(kv == 0)
    def _():
        m_sc[...] = jnp.full_like(m_sc, -jnp.inf)
        acc_sc[...] = jnp.zeros_like(acc_sc)

    lane = lax.broadcasted_iota(jnp.int32, (1, LANES), 1)
    lo = lane < D_HEAD_DIM
    nblk = q_ref.shape[-1] // LANES
    tq, tk = q_ref.shape[1], k_ref.shape[1]
    ku, qu = FLASH_KEY_UNIT, FLASH_QUERY_UNIT
    grows = D_HEAD_DIM + FLASH_ONES
    nheads = nblk * D_KV_HEADS
    k = k_ref[0]

    nchunk = tq // qu

    def logits_chunk(idx, c):
        j, g = divmod(idx, D_KV_HEADS)
        qj = q_ref[0, c * qu:(c + 1) * qu, j * LANES:(j + 1) * LANES]
        sel = lo if g == 0 else jnp.logical_not(lo)
        s = lax.dot_general(k, jnp.where(sel, qj, jnp.zeros_like(qj)), (((1,), (1,)), ((), ())),
                            preferred_element_type=F32)
        s_sc[idx, :, c * qu:(c + 1) * qu] = s
        return jnp.max(s, axis=0, keepdims=True)

    def finish_logits(idx, mcs):
        m_prev = m_sc[idx, 0:1, :]
        m_new = jnp.maximum(m_prev, jnp.concatenate(mcs, axis=1))
        m_sc[idx, 0:1, :] = m_new
        return m_new, jnp.exp2(m_prev - m_new)

    def value_chunk(idx, c, m_new, alpha):
        j, g = divmod(idx, D_KV_HEADS)
        rows = slice(g * grows, (g + 1) * grows)
        qcols = slice(c * qu, (c + 1) * qu)
        pv = None
        for u in range(tk // ku):
            keys = slice(u * ku, (u + 1) * ku)
            p = jnp.exp2(s_sc[idx, keys, qcols] - m_new[:, qcols])
            d = jnp.dot(vt_ref[0, rows, keys], p.astype(BF16), preferred_element_type=F32)
            pv = d if pv is None else pv + d
        acc_sc[j, rows, qcols] = alpha[:, qcols] * acc_sc[j, rows, qcols] + pv

    def logits_pass(idx):
        return finish_logits(idx, [logits_chunk(idx, c) for c in range(nchunk)])

    pending = [logits_pass(i) for i in range(min(FLASH_AHEAD, nheads))]
    for idx in range(nheads):
        if idx + FLASH_AHEAD < nheads:
            pending.append(logits_pass(idx + FLASH_AHEAD))
        stats = pending.pop(0)
        for c in range(nchunk):
            value_chunk(idx, c, *stats)

    @pl.when(kv == pl.num_programs(2) - 1)
    def _():
        for j in range(nblk):
            parts = []
            for g in range(D_KV_HEADS):
                num = acc_sc[j, g * grows:g * grows + D_HEAD_DIM, :]
                den = acc_sc[j, g * grows + D_HEAD_DIM:g * grows + D_HEAD_DIM + 1, :]
                parts.append(num / den)
            o_ref[0, :, j * LANES:(j + 1) * LANES] = (
                jnp.concatenate(parts, axis=0).T.astype(o_ref.dtype))


def _flash(q, k, vt, tq, tk):
    bsz, seq, qw = q.shape
    kw = k.shape[-1]
    vrows = vt.shape[1]
    return pl.pallas_call(
        _flash_kernel,
        out_shape=jax.ShapeDtypeStruct((bsz, seq, qw), BF16),
        grid=(bsz, seq // tq, seq // tk),
        in_specs=[pl.BlockSpec((1, tq, qw), lambda b, i, j: (b, i, 0)),
                  pl.BlockSpec((1, tk, kw), lambda b, i, j: (b, j, 0)),
                  pl.BlockSpec((1, vrows, tk), lambda b, i, j: (b, 0, j))],
        out_specs=pl.BlockSpec((1, tq, qw), lambda b, i, j: (b, i, 0)),
        scratch_shapes=[pltpu.VMEM((D_Q_HEADS, 8, tq), F32),
                        pltpu.VMEM((qw // LANES, vrows, tq), F32),
                        pltpu.VMEM((D_Q_HEADS, tk, tq), F32)],
        compiler_params=_cparams("parallel", "parallel", "arbitrary"),
        name="flash",
    )(q, k, vt)


def kernel(x, c, t5_bias, hgrn_lb_logits, ada_w, ada_b, norm_gains, ab_w_in, ab_w_out,
           hgrn_out_norm, cd_w_in, cd_w_out, qk_norm, ffn_w_in, ffn_w_out):
    bsz, seq, d = x.shape
    depth = ada_w.shape[0]
    mod = _ada_mod(c.astype(F32), ada_w, ada_b)
    perm = _head_perm()
    aw = A_HEADS * A_HEAD_DIM
    cw = C_GROUPS * C_WIDTH
    qw = D_Q_HEADS * D_HEAD_DIM
    tm_in = min(512, seq)
    tm = min(512, seq)
    for layer in range(depth):
        sh_m, sc_m, g_m, sh_f, sc_f, g_f = [mod[layer, :, i * d:(i + 1) * d] for i in range(6)]
        gains = norm_gains[layer]
        j = layer // 2
        if layer % 2 == 0:
            w_in = ab_w_in[j].astype(BF16)
            *qkv_cm, z = _inproj_cm(x, gains[0], sc_m, sh_m, w_in, 3 * aw, tm_in)
            branches = [_dilated_branch(cm, t5_bias, window, dil)
                        for cm, (window, dil) in zip(qkv_cm, DIL_CFG)]
            a_out = _dilated_merge(branches, tm)
            o_f, o_b = _hgrn(z, hgrn_lb_logits, layer, 0)
            g_col = 3 * B_HEADS * B_DK + B_HEADS * B_DV
            x = _out0(a_out, o_f, o_b, z, g_col, hgrn_out_norm[j], ab_w_out[j].astype(BF16),
                      x, gains[1], g_m, tm)
        else:
            w_full = cd_w_in[j]
            w_in = jnp.concatenate([w_full[:, :cw], w_full[:, cw:cw + qw][:, perm],
                                    w_full[:, cw + qw:]], axis=1).astype(BF16)
            u, qkv = _inproj(x, gains[0], sc_m, sh_m, w_in, (cw, w_in.shape[1] - cw), tm)
            c_out = _fourier_mixer(u)
            qn, kn, vn = _qkprep(qkv, qk_norm[j], tm)
            d_out = _flash(qn, kn, vn, min(FLASH_TQ, seq), min(FLASH_TK, seq))
            w_out_full = cd_w_out[j]
            w_out = jnp.concatenate([w_out_full[:cw], w_out_full[cw:][perm]], axis=0).astype(BF16)
            x = _out1(c_out, d_out, w_out, x, gains[1], g_m, tm)
        x = _ffn(x, gains[2], sc_f, sh_f, ffn_w_in[layer].astype(BF16), ffn_w_out[layer].astype(BF16),
                 gains[3], g_f, tm)
    return x
```

```python
import functools

import numpy as np
import jax
import jax.numpy as jnp
from jax import lax
from jax.experimental import pallas as pl
from jax.experimental.pallas import tpu as pltpu

F32 = jnp.float32
BF16 = jnp.bfloat16
LANES = 128
MXU_WIDTH = 256
VMEM_LIMIT_BYTES = 56 * 2**20
NEG_INF = -1e30
EPS = 1e-6

GRID_W = 64
A_HEADS = 8
A_HEAD_DIM = 64
DIL_CFG = ((128, 1), (512, 4), (2048, 16))
N_BUCKETS = 32
T5_MAX_DIST = 1024
B_HEADS = 4
B_DK = 128
B_DV = 128
C_GROUPS = 4
C_WIDTH = 128
D_Q_HEADS = 8
D_KV_HEADS = 2
D_HEAD_DIM = 64
ROPE_THETA = 10000.0

DIL_TQ = 128
DIL_TILE = 256
HGRN_BLOCK = 256
HGRN_SUB = 256
HGRN_BCAST_MIN = 8
FLASH_TQ = 1024
FLASH_TK = 1024
FLASH_KEY_UNIT = 256
FLASH_QUERY_UNIT = 256
FLASH_AHEAD = 1
FLASH_ONES = 16
FFT_N2 = 128
FFT_BATCH = 8
FFT_PAD = 8
LOG2E = 1.4426950408889634


def _cparams(*sem):
    return pltpu.CompilerParams(dimension_semantics=sem, vmem_limit_bytes=VMEM_LIMIT_BYTES)


def _const_spec(shape):
    nd = len(shape)
    return pl.BlockSpec(shape, lambda *_: (0,) * nd, pipeline_mode=pl.Buffered(1))


def _sigmoid(x):
    return 1.0 / (1.0 + jnp.exp(-x))


def _dot(a, b):
    return jnp.dot(a.astype(BF16), b.astype(BF16), preferred_element_type=F32)


def _dot_nt(a, b):
    return lax.dot_general(a.astype(BF16), b.astype(BF16), (((1,), (1,)), ((), ())),
                           preferred_element_type=F32)


def _split2(a):
    hi = a.astype(BF16)
    lo = (a - hi.astype(F32)).astype(BF16)
    return hi, lo


def _split3(a):
    a1 = a.astype(BF16)
    r = a - a1.astype(F32)
    a2 = r.astype(BF16)
    a3 = (r - a2.astype(F32)).astype(BF16)
    return a1, a2, a3


def _dot_tab(tab_hi, tab_lo, x, *, tab_left):
    x_hi, x_lo = _split2(x)
    if tab_left:
        d = lambda t, v: jnp.dot(t, v, preferred_element_type=F32)
    else:
        d = lambda t, v: jnp.dot(v, t, preferred_element_type=F32)
    return d(tab_hi, x_hi) + (d(tab_hi, x_lo) + d(tab_lo, x_hi))


def _rms(x, gain):
    ms = jnp.mean(x * x, axis=-1, keepdims=True)
    return x * lax.rsqrt(ms + EPS) * gain


def _np_split2(t):
    t = np.asarray(t, np.float32)
    hi = jnp.asarray(t, F32).astype(BF16)
    lo = (jnp.asarray(t, F32) - hi.astype(F32)).astype(BF16)
    return hi, lo


def _mod_kernel(c_ref, w_ref, b_ref, o_ref):
    c = c_ref[...]
    o_ref[0] = _dot(c * _sigmoid(c), w_ref[0]) + b_ref[0]


def _ada_mod(c, ada_w, ada_b):
    depth, d, n6 = ada_w.shape
    bsz = c.shape[0]
    rows = 8
    cp = jnp.zeros((rows, d), F32).at[:bsz].set(c)
    tn = n6 // 4
    out = pl.pallas_call(
        _mod_kernel,
        out_shape=jax.ShapeDtypeStruct((depth, rows, n6), F32),
        grid=(depth, n6 // tn),
        in_specs=[pl.BlockSpec((rows, d), lambda l, j: (0, 0)),
                  pl.BlockSpec((1, d, tn), lambda l, j: (l, 0, j)),
                  pl.BlockSpec((1, 1, tn), lambda l, j: (l, 0, j))],
        out_specs=pl.BlockSpec((1, rows, tn), lambda l, j: (l, 0, j)),
        compiler_params=_cparams("parallel", "parallel"),
        name="ada_mod",
    )(cp, ada_w, ada_b.reshape(depth, 1, n6))
    return out[:, :bsz]


def _inproj_cm_kernel(x_ref, gain_ref, sc_ref, sh_ref, w_ref, *refs):
    cm_refs, rest_ref, zs_sc, zc_sc = refs[:-3], refs[-3], refs[-2], refs[-1]
    h = _rms(x_ref[0], gain_ref[...]) * (1.0 + sc_ref[0]) + sh_ref[0]
    z = _dot(h, w_ref[...])
    nblk, tm, _ = zs_sc.shape
    rest_ref[0] = z[:, nblk * LANES:]
    nq = A_HEADS * A_HEAD_DIM // LANES
    for c in range(nblk):
        blk = z[:, c * LANES:(c + 1) * LANES]
        zs_sc[c] = blk * (A_HEAD_DIM ** -0.5 * LOG2E) if c < nq else blk
    src, sd = zs_sc, 1
    for level, (cm_ref, (_, dil)) in enumerate(zip(cm_refs, DIL_CFG)):
        step, n = dil // sd, tm // dil
        keep = dil > 1 and level + 1 < len(DIL_CFG)
        for rs in range(sd):
            for cc in range(step):
                r = rs + sd * cc
                for c in range(nblk):
                    rows = src[c, pl.ds(rs * (tm // sd) + cc, n, stride=step), :]
                    cm_ref[0, r, :, c * LANES:(c + 1) * LANES] = rows.astype(BF16)
                    if keep:
                        zc_sc[c, r * n:(r + 1) * n, :] = rows
        if keep:
            src, sd = zc_sc, dil


def _inproj_cm(x, gain, sc, sh, w_bf16, na, tm):
    bsz, seq, d = x.shape
    n = w_bf16.shape[1]
    vec = pl.BlockSpec((1, 1, d), lambda b, i: (b, 0, 0))
    dils = [dl for _, dl in DIL_CFG]
    assert dils[0] == 1 and all(b % a == 0 for a, b in zip(dils, dils[1:]))
    return pl.pallas_call(
        _inproj_cm_kernel,
        out_shape=tuple([jax.ShapeDtypeStruct((bsz, dl, seq // dl, na), BF16) for dl in dils]
                        + [jax.ShapeDtypeStruct((bsz, seq, n - na), F32)]),
        grid=(bsz, seq // tm),
        in_specs=[pl.BlockSpec((1, tm, d), lambda b, i: (b, i, 0)),
                  _const_spec((1, d)), vec, vec, _const_spec((d, n))],
        out_specs=tuple([pl.BlockSpec((1, dl, tm // dl, na), lambda b, i: (b, 0, i, 0)) for dl in dils]
                        + [pl.BlockSpec((1, tm, n - na), lambda b, i: (b, i, 0))]),
        scratch_shapes=[pltpu.VMEM((na // LANES, tm, LANES), F32),
                        pltpu.VMEM((na // LANES, tm, LANES), F32)],
        compiler_params=_cparams("parallel", "parallel"),
        name="inproj_cm",
    )(x, gain.reshape(1, d), sc.reshape(bsz, 1, d), sh.reshape(bsz, 1, d), w_bf16)


def _t5_buckets(rel):
    half = N_BUCKETS // 2
    max_exact = half // 2
    n = np.abs(rel)
    large = max_exact + (np.log(np.maximum(n, 1) / max_exact) / np.log(T5_MAX_DIST / max_exact)
                         * (half - max_exact)).astype(np.int32)
    large = np.minimum(large, half - 1)
    return (np.where(rel > 0, half, 0) + np.where(n < max_exact, n, large)).astype(np.int32)


def _dil_bias(t5_bias, window, dil, tq):
    half = (window // 2) // dil
    assert half == tq // 2
    rel = np.arange(2 * tq)[None, :] - half - np.arange(tq)[:, None]
    inside = np.abs(rel) <= half
    buckets = _t5_buckets(np.where(inside, rel, 0) * dil)
    onehot =jnp.asarray(np.eye(N_BUCKETS, dtype=np.float32)[buckets])
    bias = jnp.einsum("qkn,nh->hqk", onehot, t5_bias.astype(F32), precision=lax.Precision.HIGHEST)
    return jnp.where(jnp.asarray(inside)[None], bias * LOG2E, NEG_INF)


def _dil_kernel(q_ref, kp_ref, kc_ref, kn_ref, vp_ref, vc_ref, vn_ref, bias_ref, o_ref, lse_ref,
                *, class_len):
    i = pl.program_id(2)
    sub, hq, tile = DIL_TQ, DIL_TQ // 2, DIL_TILE
    kwin = jnp.concatenate([kp_ref[0, 0], kc_ref[0, 0], kn_ref[0, 0]], axis=0)
    vwin = jnp.concatenate([vp_ref[0, 0], vc_ref[0, 0], vn_ref[0, 0]], axis=0)
    lane = lax.broadcasted_iota(jnp.int32, (1, LANES), 1)
    lo = lane < A_HEAD_DIM
    for jt in range(tile // sub):
        qrows = slice(jt * sub, (jt + 1) * sub)
        krows = slice(jt * sub, jt * sub + 2 * sub)
        kpos = i * tile + jt * sub - hq + lax.broadcasted_iota(jnp.int32, (1, 2 * sub), 1)
        valid = jnp.logical_and(kpos >= 0, kpos < class_len)
        lse_all = jnp.zeros((sub, LANES), F32)
        for j in range(A_HEADS // 2):
            cols = slice(j * LANES, (j + 1) * LANES)
            qj = q_ref[0, 0, qrows, cols]
            kj = kwin[krows, cols]
            vj = vwin[krows, cols]
            zero = jnp.zeros_like(qj)
            q2 = jnp.concatenate([jnp.where(lo, qj, zero), jnp.where(lo, zero, qj)], axis=0)
            s = lax.dot_general(q2, kj, (((1,), (1,)), ((), ())), preferred_element_type=F32)
            s = jnp.where(valid, s + bias_ref[j], NEG_INF)
            m = jnp.max(s, axis=-1, keepdims=True)
            p = jnp.exp2(s - m)
            l = jnp.sum(p, axis=-1, keepdims=True)
            o2 = jnp.dot(p.astype(BF16), vj, preferred_element_type=F32) * (1.0 / l)
            lse = m + jnp.log2(l)
            lse_all = jnp.where(lane == 2 * j, lse[:sub], lse_all)
            lse_all = jnp.where(lane == 2 * j + 1, lse[sub:], lse_all)
            o_ref[0, 0, qrows, cols] = jnp.where(lo, o2[:sub], o2[sub:])
        lse_ref[0, 0, qrows, :] = lse_all


def _dilated_branch(qkv_cm, t5_bias, window, dil):
    bsz, _, cl, width = qkv_cm.shape
    aw = A_HEADS * A_HEAD_DIM
    tile, hq = DIL_TILE, DIL_TQ // 2
    nt = cl // tile
    per = tile // hq
    nh = cl // hq

    def cur(col):
        return pl.BlockSpec((1, 1, tile, aw), lambda b, r, i: (b, r, i, col))

    def prev(col):
        return pl.BlockSpec((1, 1, hq, aw), lambda b, r, i: (b, r, jnp.maximum(i * per - 1, 0), col))

    def nxt(col):
        return pl.BlockSpec((1, 1, hq, aw),
                            lambda b, r, i: (b, r, jnp.minimum((i + 1) * per, nh - 1), col))

    return pl.pallas_call(
        functools.partial(_dil_kernel, class_len=cl),
        out_shape=(jax.ShapeDtypeStruct((bsz, dil, cl, aw), F32),
                   jax.ShapeDtypeStruct((bsz, dil, cl, LANES), F32)),
        grid=(bsz, dil, nt),
        in_specs=[cur(0), prev(1), cur(1), nxt(1), prev(2), cur(2), nxt(2),
                  _const_spec((A_HEADS // 2, 2 * DIL_TQ, 2 * DIL_TQ))],
        out_specs=(pl.BlockSpec((1, 1, tile, aw), lambda b, r, i: (b, r, i, 0)),
                   pl.BlockSpec((1, 1, tile, LANES), lambda b, r, i: (b, r, i, 0))),
        compiler_params=_cparams("parallel", "parallel", "parallel"),
        name=f"dilated_d{dil}",
    )(*([qkv_cm] * 7),
      _dil_bias(t5_bias, window, dil, DIL_TQ).reshape(A_HEADS // 2, 2 * DIL_TQ, 2 * DIL_TQ))


def _dilmerge_kernel(*refs):
    nbr = len(DIL_CFG)
    a_refs, l_refs = refs[:nbr], refs[nbr:2 * nbr]
    e_ref, o_ref = refs[2 * nbr], refs[2 * nbr + 1]
    a_scs, l_scs = refs[2 * nbr + 2:3 * nbr + 2], refs[3 * nbr + 2:4 * nbr + 2]
    tm = o_ref.shape[1]
    accs, lses = [], []
    for (_, dil), a_ref, l_ref, a_sc, l_sc in zip(DIL_CFG, a_refs, l_refs, a_scs, l_scs):
        nblk = a_sc.shape[0]
        for r in range(dil):
            rows = pl.ds(r, tm // dil, stride=dil)
            for c in range(nblk):
                a_sc[c, rows, :] = a_ref[0, r, :, c * LANES:(c + 1) * LANES]
            l_sc[rows, :] = l_ref[0, r]
        accs.append(jnp.concatenate([a_sc[c] for c in range(nblk)], axis=1))
        lses.append(l_sc[...])
    mx = functools.reduce(jnp.maximum, lses)
    ws = [jnp.exp2(x - mx) for x in lses]
    tot = functools.reduce(jnp.add, ws)
    out = None
    for w, a in zip(ws, accs):
        w_hi, w_lo = _split2(w / tot)
        wide = (jnp.dot(w_hi, e_ref[...], preferred_element_type=F32)
                + jnp.dot(w_lo, e_ref[...], preferred_element_type=F32))
        out = wide * a if out is None else out + wide * a
    o_ref[0] = out.astype(o_ref.dtype)


def _dilated_merge(branch_outs, tm):
    bsz, _, _, aw = branch_outs[0][0].shape
    seq = branch_outs[0][0].shape[1] * branch_outs[0][0].shape[2]
    expand = np.zeros((LANES, aw), np.float32)
    for h in range(A_HEADS):
        expand[h, h * A_HEAD_DIM:(h + 1) * A_HEAD_DIM] = 1.0
    dils = [d for _, d in DIL_CFG]
    in_specs = ([pl.BlockSpec((1, d, tm // d, aw), lambda b, i: (b, 0, i, 0)) for d in dils]
                + [pl.BlockSpec((1, d, tm // d, LANES), lambda b, i: (b, 0, i, 0)) for d in dils]
                + [_const_spec(expand.shape)])
    return pl.pallas_call(
        _dilmerge_kernel,
        out_shape=jax.ShapeDtypeStruct((bsz, seq, aw), BF16),
        grid=(bsz, seq // tm),
        in_specs=in_specs,
        out_specs=pl.BlockSpec((1, tm, aw), lambda b, i: (b, i, 0)),
        scratch_shapes=([pltpu.VMEM((aw // LANES, tm, LANES), F32) for _ in dils]
                        + [pltpu.VMEM((tm, LANES), F32) for _ in dils]),
        compiler_params=_cparams("parallel", "parallel"),
        name="dilmerge",
    )(*[o for o, _ in branch_outs], *[l for _, l in branch_outs], jnp.asarray(expand, BF16))


def _hgrn_consts(reverse):
    t = HGRN_SUB
    r = np.arange(t)
    u = r[None, :]
    row = r[:, None]
    nmats = [(u >= row) if reverse else (u <= row)]
    masks = []
    m = t // 2
    while m >= 1:
        grp = r // (2 * m)
        in_first = (r % (2 * m)) < m
        same = grp[:, None] == grp[None, :]
        if reverse:
            beta = (grp * 2 * m + m)[:, None]
            n = np.where(in_first[:, None], (u >= row) & (u < beta), (u >= beta) & (u < row))
            mask = same & in_first[:, None] & ~in_first[None, :]
        else:
            beta = (grp * 2 * m + m - 1)[:, None]
            n = np.where(in_first[:, None], (u > row) & (u <= beta), (u > beta) & (u <= row))
            mask = same & ~in_first[:, None] & in_first[None, :]
        if m < HGRN_BCAST_MIN:
            nmats.append(n)
        masks.append(mask)
        m //= 2
    masks.append(np.eye(t, dtype=bool))
    nmat = jnp.asarray(np.concatenate(nmats, axis=0), F32).astype(BF16)
    return nmat, jnp.asarray(np.stack(masks), F32)


def _hgrn_block(q, f, v, lb, st, nmat_ref, mask_ref, reverse):
    t = HGRN_SUB
    nlev = mask_ref.shape[0] - 1
    qs = q * (B_DK ** -0.5)
    fa = lb + (1.0 - lb) * _sigmoid(f)
    kk = 1.0 - fa
    g_hi, g_lo = _split2(jnp.log(fa))
    ex = jnp.dot(nmat_ref[...], jnp.concatenate([g_hi, g_lo], axis=1), preferred_element_type=F32)
    ex = ex[:, :B_DK] + ex[:, B_DK:]
    b = ex[:t]
    btot = b[0:1] if reverse else b[t - 1:t]
    a = mask_ref[nlev] * _dot_nt(qs, kk)
    fine = 1
    for l in range(nlev):
        m = t >> (l + 1)
        if m >= HGRN_BCAST_MIN:
            ref = jnp.concatenate(
                [jnp.broadcast_to(b[beta:beta + 1], (2 * m, B_DK))
                 for beta in range(m if reverse else m - 1, t, 2 * m)], axis=0)
            e = jnp.exp(-jnp.abs(b - ref))
        else:
            e = jnp.exp(ex[fine * t:(fine + 1) * t])
            fine += 1
        a = a + mask_ref[l] * _dot_nt(qs * e, kk * e)
    out = _dot(a, v) + _dot_nt(qs * jnp.exp(b), st)
    khat = (kk * jnp.exp(btot - b)).astype(BF16)
    st_new = st * jnp.exp(btot) + jnp.dot(v.T.astype(BF16), khat, preferred_element_type=F32)
    return out, st_new


def _hgrn_lb(lg_ref, layer):
    lg = [lg_ref[l, 0] for l in range(lg_ref.shape[0])]
    mx = functools.reduce(jnp.maximum, lg)
    e = [jnp.exp(x - mx) for x in lg]
    return functools.reduce(jnp.add, e[:layer + 1]) / functools.reduce(jnp.add, e)


def _hgrn_kernel(qf_ref, ff_ref, vf_ref, qb_ref, fb_ref, vb_ref, lgf_ref, lgb_ref,
                 nf_ref, mf_ref, nb_ref, mb_ref, of_ref, ob_ref, sf_sc, sb_sc, *, layer):
    @pl.when(pl.program_id(2) == 0)
    def _():
        sf_sc[...] = jnp.zeros_like(sf_sc)
        sb_sc[...] = jnp.zeros_like(sb_sc)

    nsub = HGRN_BLOCK // HGRN_SUB
    chains = ((qf_ref, ff_ref, vf_ref, lgf_ref, sf_sc, nf_ref, mf_ref, of_ref, False),
              (qb_ref, fb_ref, vb_ref, lgb_ref, sb_sc, nb_ref, mb_ref, ob_ref, True))
    for q_ref, f_ref, v_ref, lg_ref, st_sc, n_ref, m_ref, o_ref, reverse in chains:
        lb = _hgrn_lb(lg_ref, layer)
        st = st_sc[...]
        for sub in (reversed(range(nsub)) if reverse else range(nsub)):
            rows = pl.ds(sub * HGRN_SUB, HGRN_SUB)
            o, st = _hgrn_block(q_ref[0, rows, :], f_ref[0, rows, :], v_ref[0, rows, :], lb, st,
                                n_ref, m_ref, reverse)
            o_ref[0, rows, :] = o.astype(o_ref.dtype)
        st_sc[...] = st


def _hgrn(z, lb_logits, layer, col0):
    bsz, seq, _ = z.shape
    t = HGRN_BLOCK
    nb = seq // t
    c0 = col0 // LANES
    hw = B_HEADS

    def zspec(group, rev):
        return pl.BlockSpec(
            (1, t, LANES),
            lambda b, h, j: (b, (nb - 1 - j) if rev else j, c0 + group * hw + h))

    def lgspec(direction):
        return pl.BlockSpec((lb_logits.shape[0], 1, 1, LANES),
                            lambda b, h, j: (0, direction * hw + h, 0, 0))

    nf, mf = _hgrn_consts(False)
    nbw, mbw = _hgrn_consts(True)
    lg = lb_logits.astype(F32).reshape(lb_logits.shape[0], 2 * hw, 1, LANES)
    o_shape = jax.ShapeDtypeStruct((bsz, seq, hw * B_DV), BF16)
    return pl.pallas_call(
        functools.partial(_hgrn_kernel, layer=layer),
        out_shape=(o_shape, o_shape),
        grid=(bsz, hw, nb),
        in_specs=[zspec(0, False), zspec(1, False), zspec(3, False),
                  zspec(0, True), zspec(2, True), zspec(3, True),
                  lgspec(0), lgspec(1),
                  _const_spec(nf.shape), _const_spec(mf.shape),
                  _const_spec(nbw.shape), _const_spec(mbw.shape)],
        out_specs=(pl.BlockSpec((1, t, LANES), lambda b, h, j: (b, j, h)),
                   pl.BlockSpec((1, t, LANES), lambda b, h, j: (b, nb - 1 - j, h))),
        scratch_shapes=[pltpu.VMEM((B_DV, B_DK), F32), pltpu.VMEM((B_DV, B_DK), F32)],
        compiler_params=_cparams("parallel", "parallel", "arbitrary"),
        name="hgrn",
    )(z, z, z, z, z, z, lg, lg, nf, mf, nbw, mbw)


def _residual_epilogue(y, x_ref, gain_ref, gate_ref, o_ref):
    o_ref[0] = x_ref[0] + gate_ref[0] * _rms(y, gain_ref[...])


def _out0_kernel(a_ref, of_ref, ob_ref, g_ref, on_ref, w_ref, x_ref, gain_ref, gate_ref, o_ref):
    o = of_ref[0].astype(F32) + ob_ref[0].astype(F32)
    g = g_ref[0]
    parts = [_rms(o[:, h * B_DV:(h + 1) * B_DV], on_ref[...]) for h in range(B_HEADS)]
    bn = jnp.concatenate(parts, axis=-1) * (g * _sigmoid(g))
    na = a_ref.shape[-1]
    y = _dot(a_ref[0], w_ref[:na, :]) + _dot(bn, w_ref[na:, :])
    _residual_epilogue(y, x_ref, gain_ref, gate_ref, o_ref)


def _out1_kernel(c_ref, d_ref, w_ref, x_ref, gain_ref, gate_ref, o_ref):
    c = jnp.concatenate([c_ref[0, g] for g in range(c_ref.shape[1])], axis=1)
    nc = c.shape[-1]
    y = _dot(c, w_ref[:nc, :]) + _dot(d_ref[0], w_ref[nc:, :])
    _residual_epilogue(y, x_ref, gain_ref, gate_ref, o_ref)


def _row_spec(tm, width, col=0):
    return pl.BlockSpec((1, tm, width), lambda b, i: (b, i, col))


def _out0(a_out, o_f, o_b, z, g_col, out_norm, w_bf16, x, gain, gate, tm):
    bsz, seq, d = x.shape
    wv = B_HEADS * B_DV
    vec = pl.BlockSpec((1, 1, d), lambda b, i: (b, 0, 0))
    return pl.pallas_call(
        _out0_kernel,
        out_shape=jax.ShapeDtypeStruct(x.shape, F32),
        grid=(bsz, seq // tm),
        in_specs=[_row_spec(tm, a_out.shape[-1]), _row_spec(tm, wv), _row_spec(tm, wv),
                  _row_spec(tm, wv, g_col // wv), _const_spec((1, B_DV)),
                  _const_spec(w_bf16.shape), _row_spec(tm, d), _const_spec((1, d)), vec],
        out_specs=_row_spec(tm, d),
        compiler_params=_cparams("parallel", "parallel"),
        name="out0",
    )(a_out, o_f, o_b, z, out_norm.reshape(1, B_DV), w_bf16, x, gain.reshape(1, d),
      gate.reshape(bsz, 1, d))


def _out1(c_out, d_out, w_bf16, x, gain, gate, tm):
    bsz, seq, d = x.shape
    vec = pl.BlockSpec((1, 1, d), lambda b, i: (b, 0, 0))
    return pl.pallas_call(
        _out1_kernel,
        out_shape=jax.ShapeDtypeStruct(x.shape, F32),
        grid=(bsz, seq // tm),
        in_specs=[pl.BlockSpec((1, c_out.shape[1], tm, c_out.shape[3]), lambda b, i: (b, 0, i, 0)),
                  _row_spec(tm, d_out.shape[-1]),
                  _const_spec(w_bf16.shape), _row_spec(tm, d), _const_spec((1, d)), vec],
        out_specs=_row_spec(tm, d),
        compiler_params=_cparams("parallel", "parallel"),
        name="out1",
    )(c_out, d_out, w_bf16, x, gain.reshape(1, d), gate.reshape(bsz, 1, d))


def _ffn_kernel(x_ref, g1_ref, sc_ref, sh_ref, wi_ref, wo_ref, g2_ref, gate_ref, o_ref, *, nchunk):
    x = x_ref[0]
    h = (_rms(x, g1_ref[...]) * (1.0 + sc_ref[0]) + sh_ref[0]).astype(BF16)
    hidden = wo_ref.shape[0]
    ck = hidden // nchunk
    y = None
    for c in range(nchunk):
        gt = jnp.dot(h, wi_ref[:, c * ck:(c + 1) * ck], preferred_element_type=F32)
        up = jnp.dot(h, wi_ref[:, hidden + c * ck:hidden + (c + 1) * ck], preferred_element_type=F32)
        part = _dot(gt * _sigmoid(gt) * up, wo_ref[c * ck:(c + 1) * ck, :])
        y = part if y is None else y + part
    o_ref[0] = x + gate_ref[0] * _rms(y, g2_ref[...])


def _ffn(x, g1, sc, sh, wi_bf16, wo_bf16, g2, gate, tm):
    bsz, seq, d = x.shape
    vec = pl.BlockSpec((1, 1, d), lambda b, i: (b, 0, 0))
    hidden = wo_bf16.shape[0]
    nchunk = 2 if (hidden // 2) % MXU_WIDTH == 0 else 1
    return pl.pallas_call(
        functools.partial(_ffn_kernel, nchunk=nchunk),
        out_shape=jax.ShapeDtypeStruct(x.shape, F32),
        grid=(bsz, seq // tm),
        in_specs=[_row_spec(tm, d), _const_spec((1, d)), vec, vec,
                  _const_spec(wi_bf16.shape), _const_spec(wo_bf16.shape),
                  _const_spec((1, d)), vec],
        out_specs=_row_spec(tm, d),
        compiler_params=_cparams("parallel", "parallel"),
        name="ffn",
    )(x, g1.reshape(1, d), sc.reshape(bsz, 1, d), sh.reshape(bsz, 1, d), wi_bf16, wo_bf16,
      g2.reshape(1, d), gate.reshape(bsz, 1, d))


def _fft_kernel(u_ref, f1_ref, twc_ref, tws_ref, f2_ref, fw_ref, o_ref, u_sc, p_sc, y_sc,
                *, scale, n1, n2):
    pu = n2 + FFT_PAD
    pp = 2 * n1 + FFT_PAD
    py = n1 + FFT_PAD
    f1 = f1_ref[...].astype(BF16)
    f2 = f2_ref[...].astype(BF16)
    fw = fw_ref[...].astype(BF16)
    for i1 in range(n1):
        u_sc[i1 * pu:i1 * pu + n2, :] = u_ref[0, 0, i1 * n2:(i1 + 1) * n2, :]

    nb = FFT_BATCH

    def stage1(blk, carry):
        i2s = [blk * nb + j for j in range(nb)]
        x = jnp.concatenate([u_sc[pl.ds(i2, n1, stride=pu), :] for i2 in i2s], axis=1)
        p = jnp.dot(f1, x.astype(BF16), preferred_element_type=F32)
        for j, i2 in enumerate(i2s):
            p_sc[pl.ds(pl.multiple_of(i2 * pp, 8), 2 * n1), :] = p[:, j * C_WIDTH:(j + 1) * C_WIDTH]
        return carry

    lax.fori_loop(0, n2 // nb, stage1, 0)

    def stage2(blk, carry):
        k1s = [blk * nb + j for j in range(nb)]
        qr, qi = [], []
        for k1 in k1s:
            tc = twc_ref[k1]
            ts = tws_ref[k1]
            pr = p_sc[pl.ds(k1, n2, stride=pp), :]
            pim = p_sc[pl.ds(n1 + k1, n2, stride=pp), :]
            qr.append(pr * tc + pim * ts)
            qi.append(pim * tc - pr * ts)
        q = jnp.concatenate([jnp.concatenate(qr, axis=1), jnp.concatenate(qi, axis=1)], axis=0)
        xx = jnp.dot(f2, q.astype(BF16), preferred_element_type=F32)
        xg = jnp.concatenate(
            [jnp.concatenate([xx[:n2, j * C_WIDTH:(j + 1) * C_WIDTH],
                              xx[n2:, j * C_WIDTH:(j + 1) * C_WIDTH]], axis=1) for j in range(nb)],
            axis=0)
        y = jnp.dot(xg.astype(BF16), fw, preferred_element_type=F32) * scale
        for j, k1 in enumerate(k1s):
            y_sc[pl.ds(k1, n2, stride=py), :] = y[j * n2:(j + 1) * n2]
        return carry

    lax.fori_loop(0, n1 // nb, stage2, 0)
    for k2 in range(n2):
        o_ref[0, 0, k2 * n1:(k2 + 1) * n1, :] = y_sc[k2 * py:k2 * py + n1, :]


def _fourier_mixer(u):
    bsz, ngroups, seq, width = u.shape
    n2 = FFT_N2
    n1 = seq // n2
    assert n1 * n2 == seq and width == C_WIDTH and n1 % 8 == 0
    a1 = 2.0 * np.pi * np.outer(np.arange(n1), np.arange(n1)) / n1
    f1 = np.concatenate([np.cos(a1), -np.sin(a1)], axis=0)
    a2 = 2.0 * np.pi * np.outer(np.arange(n2), np.arange(n2)) / n2
    c2, s2 = np.cos(a2), np.sin(a2)
    f2 = np.block([[c2, s2], [-s2, c2]])
    aw = 2.0 * np.pi * np.outer(np.arange(C_WIDTH), np.arange(C_WIDTH)) / C_WIDTH
    fw = np.concatenate([np.cos(aw), np.sin(aw)], axis=0)
    at = np.repeat((2.0 * np.pi * np.outer(np.arange(n1), np.arange(n2)) / seq)[:, :, None],
                   C_WIDTH, axis=2)
    consts = (jnp.asarray(f1, F32), jnp.asarray(np.cos(at), F32), jnp.asarray(np.sin(at), F32),
              jnp.asarray(f2, F32), jnp.asarray(fw, F32))
    blk = pl.BlockSpec((1, 1, seq, C_WIDTH), lambda b, g: (b, g, 0, 0))
    return pl.pallas_call(
        functools.partial(_fft_kernel, scale=float(1.0 / np.sqrt(seq * C_WIDTH)), n1=n1, n2=n2),
        out_shape=jax.ShapeDtypeStruct(u.shape, F32),
        grid=(bsz, ngroups),
        in_specs=[blk] + [_const_spec(c.shape) for c in consts],
        out_specs=blk,
        scratch_shapes=[pltpu.VMEM((n1 * (n2 + FFT_PAD), C_WIDTH), F32),
                        pltpu.VMEM((n2 * (2 * n1 + FFT_PAD), C_WIDTH), F32),
                        pltpu.VMEM((n2 * (n1 + FFT_PAD), C_WIDTH), F32)],
        compiler_params=_cparams("parallel", "parallel"),
        name="fft",
    )(u, *consts)


def _head_perm():
    rep = D_Q_HEADS // D_KV_HEADS
    cols = []
    for j in range(rep):
        for g in range(D_KV_HEADS):
            h = g * rep + j
            cols.extend(range(h * D_HEAD_DIM, (h + 1) * D_HEAD_DIM))
    return np.asarray(cols, np.int32)


def _rope_tables(seq):
    rows = seq // GRID_W
    row = jnp.repeat(jnp.arange(rows, dtype=F32), GRID_W)
    col = jnp.tile(jnp.arange(GRID_W, dtype=F32), rows)
    axis_dim = D_HEAD_DIM // 2
    inv_freq = jnp.power(ROPE_THETA, -jnp.arange(0, axis_dim, 2, dtype=F32) / axis_dim)
    ang_r = row[:, None] * inv_freq[None, :]
    ang_c = col[:, None] * inv_freq[None, :]
    cr, sr, cc, sc = jnp.cos(ang_r), jnp.sin(ang_r), jnp.cos(ang_c), jnp.sin(ang_c)
    cos = jnp.concatenate([cr, cr, cc, cc], axis=1)
    sin = jnp.concatenate([-sr, sr, -sc, sc], axis=1)
    reps = LANES // D_HEAD_DIM
    return jnp.tile(cos, (1, reps)), jnp.tile(sin, (1, reps))


def _inproj_qk_kernel(x_ref, gain_ref, sc_ref, sh_ref, w_ref, cos_ref, sin_ref, bd_h_ref, bd_l_ref,
                      gq_ref, gk_ref, u_ref, qo_ref, ko_ref, vo_ref):
    h = _rms(x_ref[0], gain_ref[...]) * (1.0 + sc_ref[0]) + sh_ref[0]
    z = _dot(h, w_ref[...])
    ngrp = u_ref.shape[1]
    for g in range(ngrp):
        u_ref[0, g] = z[:, g * LANES:(g + 1) * LANES]
    cos = cos_ref[...]
    sin = sin_ref[...]
    quarter = D_HEAD_DIM // 4
    lane = lax.broadcasted_iota(jnp.int32, (1, LANES), 1)
    first_of_pair = (lane // quarter) % 2 == 0

    def norm_rope(x, gain, scale):
        ms = _dot_tab(bd_h_ref[...], bd_l_ref[...], x * x, tab_left=False)
        xn = x * lax.rsqrt(ms + EPS) * gain
        partner = jnp.where(first_of_pair, pltpu.roll(xn, LANES - quarter, 1),
                            pltpu.roll(xn, quarter, 1))
        return ((xn * cos + partner * sin) * scale).astype(BF16)

    nq = qo_ref.shape[-1] // LANES
    for j in range(nq):
        qo_ref[0, :, j * LANES:(j + 1) * LANES] = norm_rope(
            z[:, (ngrp + j) * LANES:(ngrp + j + 1) * LANES], gq_ref[...], D_HEAD_DIM ** -0.5 * LOG2E)
    ko_ref[0] = norm_rope(z[:, (ngrp + nq) * LANES:(ngrp + nq + 1) * LANES], gk_ref[...], 1.0)
    vt = z[:, (ngrp + nq + 1) * LANES:(ngrp + nq + 2) * LANES].T
    ones = jnp.ones((FLASH_ONES, vt.shape[1]), F32)
    vo_ref[0] = jnp.concatenate(
        [piece for g in range(D_KV_HEADS)
         for piece in (vt[g * D_HEAD_DIM:(g + 1) * D_HEAD_DIM], ones)], axis=0).astype(BF16)


def _inproj_qk(x, gain, sc, sh, w_bf16, qk_norm_j, tm):
    bsz, seq, d = x.shape
    n = w_bf16.shape[1]
    qw = D_Q_HEADS * D_HEAD_DIM
    kw = D_KV_HEADS * D_HEAD_DIM
    assert kw == LANES and n == C_GROUPS * C_WIDTH + qw + 2 * kw
    vrows = D_KV_HEADS * (D_HEAD_DIM + FLASH_ONES)
    cos, sin = _rope_tables(seq)
    bd = np.kron(np.eye(LANES // D_HEAD_DIM), np.full((D_HEAD_DIM, D_HEAD_DIM), 1.0 / D_HEAD_DIM))
    bd_h, bd_l = _np_split2(bd)
    reps = LANES // D_HEAD_DIM
    gq = jnp.tile(qk_norm_j[0].astype(F32), reps).reshape(1, LANES)
    gk = jnp.tile(qk_norm_j[1].astype(F32), reps).reshape(1, LANES)
    tab = pl.BlockSpec((tm, LANES), lambda b, i: (i, 0))
    vec = pl.BlockSpec((1, 1, d), lambda b, i: (b, 0, 0))
    return pl.pallas_call(
        _inproj_qk_kernel,
        out_shape=(jax.ShapeDtypeStruct((bsz, C_GROUPS, seq, C_WIDTH), F32),
                   jax.ShapeDtypeStruct((bsz, seq, qw), BF16),
                   jax.ShapeDtypeStruct((bsz, seq, kw), BF16),
                   jax.ShapeDtypeStruct((bsz, vrows, seq), BF16)),
        grid=(bsz, seq // tm),
        in_specs=[pl.BlockSpec((1, tm, d), lambda b, i: (b, i, 0)),
                  _const_spec((1, d)), vec, vec, _const_spec((d, n)),
                  tab, tab, _const_spec(bd_h.shape), _const_spec(bd_l.shape),
                  _const_spec((1, LANES)), _const_spec((1, LANES))],
        out_specs=(pl.BlockSpec((1, C_GROUPS, tm, C_WIDTH), lambda b, i: (b, 0, i, 0)),
                   _row_spec(tm, qw), _row_spec(tm, kw),
                   pl.BlockSpec((1, vrows, tm), lambda b, i: (b, 0, i))),
        compiler_params=_cparams("parallel", "parallel"),
        name="inproj_qk",
    )(x, gain.reshape(1, d), sc.reshape(bsz, 1, d), sh.reshape(bsz, 1, d), w_bf16,
      cos, sin, bd_h, bd_l, gq, gk)


def _flash_kernel(q_ref, k_ref, vt_ref, o_ref, m_sc, acc_sc, s_sc):
    kv = pl.program_id(2)

    @pl.when(kv == 0)
    def _():
        m_sc[...] = jnp.full_like(m_sc, -jnp.inf)
        acc_sc[...] = jnp.zeros_like(acc_sc)

    lane = lax.broadcasted_iota(jnp.int32, (1, LANES), 1)
    lo = lane < D_HEAD_DIM
    nblk = q_ref.shape[-1] // LANES
    tq, tk = q_ref.shape[1], k_ref.shape[1]
    ku, qu = FLASH_KEY_UNIT, FLASH_QUERY_UNIT
    grows = D_HEAD_DIM + FLASH_ONES
    nheads = nblk * D_KV_HEADS
    k = k_ref[0]

    nchunk = tq // qu

    def logits_chunk(idx, c):
        j, g = divmod(idx, D_KV_HEADS)
        qj = q_ref[0, c * qu:(c + 1) * qu, j * LANES:(j + 1) * LANES]
        sel = lo if g == 0 else jnp.logical_not(lo)
        s = lax.dot_general(k, jnp.where(sel, qj, jnp.zeros_like(qj)), (((1,), (1,)), ((), ())),
                            preferred_element_type=F32)
        s_sc[idx, :, c * qu:(c + 1) * qu] = s
        return jnp.max(s, axis=0, keepdims=True)

    def finish_logits(idx, mcs):
        m_prev = m_sc[idx, 0:1, :]
        m_new = jnp.maximum(m_prev, jnp.concatenate(mcs, axis=1))
        m_sc[idx, 0:1, :] = m_new
        return m_new, jnp.exp2(m_prev - m_new)

    def value_chunk(idx, c, m_new, alpha):
        j, g = divmod(idx, D_KV_HEADS)
        rows = slice(g * grows, (g + 1) * grows)
        qcols = slice(c * qu, (c + 1) * qu)
        pv = None
        for u in range(tk // ku):
            keys = slice(u * ku, (u + 1) * ku)
            p = jnp.exp2(s_sc[idx, keys, qcols] - m_new[:, qcols])
            d = jnp.dot(vt_ref[0, rows, keys], p.astype(BF16), preferred_element_type=F32)
            pv = d if pv is None else pv + d
        acc_sc[j, rows, qcols] = alpha[:, qcols] * acc_sc[j, rows, qcols] + pv

    def logits_pass(idx):
        return finish_logits(idx, [logits_chunk(idx, c) for c in range(nchunk)])

    pending = [logits_pass(i) for i in range(min(FLASH_AHEAD, nheads))]
    for idx in range(nheads):
        if idx + FLASH_AHEAD < nheads:
            pending.append(logits_pass(idx + FLASH_AHEAD))
        stats = pending.pop(0)
        for c in range(nchunk):
            value_chunk(idx, c, *stats)

    @pl.when(kv == pl.num_programs(2) - 1)
    def _():
        for j in range(nblk):
            parts = []
            for g in range(D_KV_HEADS):
                num = acc_sc[j, g * grows:g * grows + D_HEAD_DIM, :]
                den = acc_sc[j, g * grows + D_HEAD_DIM:g * grows + D_HEAD_DIM + 1, :]
                parts.append(num / den)
            o_ref[0, :, j * LANES:(j + 1) * LANES] = (
                jnp.concatenate(parts, axis=0).T.astype(o_ref.dtype))


def _flash(q, k, vt, tq, tk):
    bsz, seq, qw = q.shape
    kw = k.shape[-1]
    vrows = vt.shape[1]
    return pl.pallas_call(
        _flash_kernel,
        out_shape=jax.ShapeDtypeStruct((bsz, seq, qw), BF16),
        grid=(bsz, seq // tq, seq // tk),
        in_specs=[pl.BlockSpec((1, tq, qw), lambda b, i, j: (b, i, 0)),
                  pl.BlockSpec((1, tk, kw), lambda b, i, j: (b, j, 0)),
                  pl.BlockSpec((1, vrows, tk), lambda b, i, j: (b, 0, j))],
        out_specs=pl.BlockSpec((1, tq, qw), lambda b, i, j: (b, i, 0)),
        scratch_shapes=[pltpu.VMEM((D_Q_HEADS, 8, tq), F32),
                        pltpu.VMEM((qw // LANES, vrows, tq), F32),
                        pltpu.VMEM((D_Q_HEADS, tk, tq), F32)],
        compiler_params=_cparams("parallel", "parallel", "arbitrary"),
        name="flash",
    )(q, k, vt)


def kernel(x, c, t5_bias, hgrn_lb_logits, ada_w, ada_b, norm_gains, ab_w_in, ab_w_out,
           hgrn_out_norm, cd_w_in, cd_w_out, qk_norm, ffn_w_in, ffn_w_out):
    bsz, seq, d = x.shape
    depth = ada_w.shape[0]
    mod = _ada_mod(c.astype(F32), ada_w, ada_b)
    perm = _head_perm()
    aw = A_HEADS * A_HEAD_DIM
    cw = C_GROUPS * C_WIDTH
    qw = D_Q_HEADS * D_HEAD_DIM
    tm_in = min(512, seq)
    tm = min(512, seq)
    for layer in range(depth):
        sh_m, sc_m, g_m, sh_f, sc_f, g_f = [mod[layer, :, i * d:(i + 1) * d] for i in range(6)]
        gains = norm_gains[layer]
        j = layer // 2
        if layer % 2 == 0:
            w_in = ab_w_in[j].astype(BF16)
            *qkv_cm, z = _inproj_cm(x, gains[0], sc_m, sh_m, w_in, 3 * aw, tm_in)
            branches = [_dilated_branch(cm, t5_bias, window, dil)
                        for cm, (window, dil) in zip(qkv_cm, DIL_CFG)]
            a_out = _dilated_merge(branches, tm)
            o_f, o_b = _hgrn(z, hgrn_lb_logits, layer, 0)
            g_col = 3 * B_HEADS * B_DK + B_HEADS * B_DV
            x = _out0(a_out, o_f, o_b, z, g_col, hgrn_out_norm[j], ab_w_out[j].astype(BF16),
                      x, gains[1], g_m, tm)
        else:
            w_full = cd_w_in[j]
            w_in = jnp.concatenate([w_full[:, :cw], w_full[:, cw:cw + qw][:, perm],
                                    w_full[:, cw + qw:]], axis=1).astype(BF16)
            u, qn, kn, vn = _inproj_qk(x, gains[0], sc_m, sh_m, w_in, qk_norm[j], tm)
            c_out = _fourier_mixer(u)
            d_out = _flash(qn, kn, vn, min(FLASH_TQ, seq), min(FLASH_TK, seq))
            w_out_full = cd_w_out[j]
            w_out = jnp.concatenate([w_out_full[:cw], w_out_full[cw:][perm]], axis=0).astype(BF16)
            x = _out1(c_out, d_out, w_out, x, gains[1], g_m, tm)
        x = _ffn(x, gains[2], sc_f, sh_f, ffn_w_in[layer].astype(BF16), ffn_w_out[layer].astype(BF16),
                 gains[3], g_f, tm)
    return x
```

```python
import functools

import numpy as np
import jax
import jax.numpy as jnp
from jax import lax
from jax.experimental import pallas as pl
from jax.experimental.pallas import tpu as pltpu

F32 = jnp.float32
BF16 = jnp.bfloat16
LANES = 128
MXU_WIDTH = 256
VMEM_LIMIT_BYTES = 56 * 2**20
NEG_INF = -1e30
EPS = 1e-6

GRID_W = 64
A_HEADS = 8
A_HEAD_DIM = 64
DIL_CFG = ((128, 1), (512, 4), (2048, 16))
N_BUCKETS = 32
T5_MAX_DIST = 1024
B_HEADS = 4
B_DK = 128
B_DV = 128
C_GROUPS = 4
C_WIDTH = 128
D_Q_HEADS = 8
D_KV_HEADS = 2
D_HEAD_DIM = 64
ROPE_THETA = 10000.0

DIL_TQ = 128
DIL_TILE = 512
HGRN_BLOCK = 256
HGRN_SUB = 256
HGRN_BCAST_MIN = 8
FLASH_TQ = 1024
FLASH_TK = 1024
FLASH_KEY_UNIT = 256
FLASH_QUERY_UNIT = 256
FLASH_AHEAD = 1
FLASH_ONES = 16
FFT_N2 = 128
FFT_BATCH = 8
FFT_PAD = 8
LOG2E = 1.4426950408889634


def _cparams(*sem):
    return pltpu.CompilerParams(dimension_semantics=sem, vmem_limit_bytes=VMEM_LIMIT_BYTES)


def _const_spec(shape):
    nd = len(shape)
    return pl.BlockSpec(shape, lambda *_: (0,) * nd, pipeline_mode=pl.Buffered(1))


def _sigmoid(x):
    return 1.0 / (1.0 + jnp.exp(-x))


def _dot(a, b):
    return jnp.dot(a.astype(BF16), b.astype(BF16), preferred_element_type=F32)


def _dot_nt(a, b):
    return lax.dot_general(a.astype(BF16), b.astype(BF16), (((1,), (1,)), ((), ())),
                           preferred_element_type=F32)


def _split2(a):
    hi = a.astype(BF16)
    lo = (a - hi.astype(F32)).astype(BF16)
    return hi, lo


def _split3(a):
    a1 = a.astype(BF16)
    r = a - a1.astype(F32)
    a2 = r.astype(BF16)
    a3 = (r - a2.astype(F32)).astype(BF16)
    return a1, a2, a3


def _dot_tab(tab_hi, tab_lo, x, *, tab_left):
    x_hi, x_lo = _split2(x)
    if tab_left:
        d = lambda t, v: jnp.dot(t, v, preferred_element_type=F32)
    else:
        d = lambda t, v: jnp.dot(v, t, preferred_element_type=F32)
    return d(tab_hi, x_hi) + (d(tab_hi, x_lo) + d(tab_lo, x_hi))


def _rms(x, gain):
    ms = jnp.mean(x * x, axis=-1, keepdims=True)
    return x * lax.rsqrt(ms + EPS) * gain


def _np_split2(t):
    t = np.asarray(t, np.float32)
    hi = jnp.asarray(t, F32).astype(BF16)
    lo = (jnp.asarray(t, F32) - hi.astype(F32)).astype(BF16)
    return hi, lo


def _mod_kernel(c_ref, w_ref, b_ref, o_ref):
    c = c_ref[...]
    o_ref[0] = _dot(c * _sigmoid(c), w_ref[0]) + b_ref[0]


def _ada_mod(c, ada_w, ada_b):
    depth, d, n6 = ada_w.shape
    bsz = c.shape[0]
    rows = 8
    cp = jnp.zeros((rows, d), F32).at[:bsz].set(c)
    tn = n6 // 4
    out = pl.pallas_call(
        _mod_kernel,
        out_shape=jax.ShapeDtypeStruct((depth, rows, n6), F32),
        grid=(depth, n6 // tn),
        in_specs=[pl.BlockSpec((rows, d), lambda l, j: (0, 0)),
                  pl.BlockSpec((1, d, tn), lambda l, j: (l, 0, j)),
                  pl.BlockSpec((1, 1, tn), lambda l, j: (l, 0, j))],
        out_specs=pl.BlockSpec((1, rows, tn), lambda l, j: (l, 0, j)),
        compiler_params=_cparams("parallel", "parallel"),
        name="ada_mod",
    )(cp, ada_w, ada_b.reshape(depth, 1, n6))
    return out[:, :bsz]


def _inproj_cm_kernel(x_ref, gain_ref, sc_ref, sh_ref, w_ref, *refs):
    cm_refs, rest_ref, zs_sc, zc_sc = refs[:-3], refs[-3], refs[-2], refs[-1]
    h = _rms(x_ref[0], gain_ref[...]) * (1.0 + sc_ref[0]) + sh_ref[0]
    z = _dot(h, w_ref[...])
    nblk, tm, _ = zs_sc.shape
    rest_ref[0] = z[:, nblk * LANES:]
    nq = A_HEADS * A_HEAD_DIM // LANES
    for c in range(nblk):
        blk = z[:, c * LANES:(c + 1) * LANES]
        zs_sc[c] = blk * (A_HEAD_DIM ** -0.5 * LOG2E) if c < nq else blk
    src, sd = zs_sc, 1
    for level, (cm_ref, (_, dil)) in enumerate(zip(cm_refs, DIL_CFG)):
        step, n = dil // sd, tm // dil
        keep = dil > 1 and level + 1 < len(DIL_CFG)
        for rs in range(sd):
            for cc in range(step):
                r = rs + sd * cc
                for c in range(nblk):
                    rows = src[c, pl.ds(rs * (tm // sd) + cc, n, stride=step), :]
                    cm_ref[0, r, :, c * LANES:(c + 1) * LANES] = rows.astype(BF16)
                    if keep:
                        zc_sc[c, r * n:(r + 1) * n, :] = rows
        if keep:
            src, sd = zc_sc, dil


def _inproj_cm(x, gain, sc, sh, w_bf16, na, tm):
    bsz, seq, d = x.shape
    n = w_bf16.shape[1]
    vec = pl.BlockSpec((1, 1, d), lambda b, i: (b, 0, 0))
    dils = [dl for _, dl in DIL_CFG]
    assert dils[0] == 1 and all(b % a == 0 for a, b in zip(dils, dils[1:]))
    return pl.pallas_call(
        _inproj_cm_kernel,
        out_shape=tuple([jax.ShapeDtypeStruct((bsz, dl, seq // dl, na), BF16) for dl in dils]
                        + [jax.ShapeDtypeStruct((bsz, seq, n - na), F32)]),
        grid=(bsz, seq // tm),
        in_specs=[pl.BlockSpec((1, tm, d), lambda b, i: (b, i, 0)),
                  _const_spec((1, d)), vec, vec, _const_spec((d, n))],
        out_specs=tuple([pl.BlockSpec((1, dl, tm // dl, na), lambda b, i: (b, 0, i, 0)) for dl in dils]
                        + [pl.BlockSpec((1, tm, n - na), lambda b, i: (b, i, 0))]),
        scratch_shapes=[pltpu.VMEM((na // LANES, tm, LANES), F32),
                        pltpu.VMEM((na // LANES, tm, LANES), F32)],
        compiler_params=_cparams("parallel", "parallel"),
        name="inproj_cm",
    )(x, gain.reshape(1, d), sc.reshape(bsz, 1, d), sh.reshape(bsz, 1, d), w_bf16)


def _t5_buckets(rel):
    half = N_BUCKETS // 2
    max_exact = half // 2
    n = np.abs(rel)
    large = max_exact + (np.log(np.maximum(n, 1) / max_exact) / np.log(T5_MAX_DIST / max_exact)
                         * (half - max_exact)).astype(np.int32)
    large = np.minimum(large, half - 1)
    return (np.where(rel > 0, half, 0) + np.where(n < max_exact, n, large)).astype(np.int32)


def _dil_bias(t5_bias, window, dil, tq):
    half = (window // 2) // dil
    assert half == tq // 2
    rel = np.arange(2 * tq)[None, :] - half - np.arange(tq)[:, None]
    inside = np.abs(rel) <= half
    buckets = _t5_buckets(np.where(inside, rel, 0) * dil)
    onehot =jnp.asarray(np.eye(N_BUCKETS, dtype=np.float32)[buckets])
    bias = jnp.einsum("qkn,nh->hqk", onehot, t5_bias.astype(F32), precision=lax.Precision.HIGHEST)
    return jnp.where(jnp.asarray(inside)[None], bias * LOG2E, NEG_INF)


def _dil_kernel(q_ref, kp_ref, kc_ref, kn_ref, vp_ref, vc_ref, vn_ref, bias_ref, o_ref, lse_ref,
                *, class_len):
    i = pl.program_id(2)
    sub, hq, tile = DIL_TQ, DIL_TQ // 2, DIL_TILE
    kwin = jnp.concatenate([kp_ref[0, 0], kc_ref[0, 0], kn_ref[0, 0]], axis=0)
    vwin = jnp.concatenate([vp_ref[0, 0], vc_ref[0, 0], vn_ref[0, 0]], axis=0)
    lane = lax.broadcasted_iota(jnp.int32, (1, LANES), 1)
    lo = lane < A_HEAD_DIM
    nblk = A_HEADS // 2
    units = [(jt, j) for jt in range(tile // sub) for j in range(nblk)]
    logits = []
    for jt, j in units:
        cols = slice(j * LANES, (j + 1) * LANES)
        qj = q_ref[0, 0, jt * sub:(jt + 1) * sub, cols]
        zero = jnp.zeros_like(qj)
        q2 = jnp.concatenate([jnp.where(lo, qj, zero), jnp.where(lo, zero, qj)], axis=0)
        s = lax.dot_general(q2, kwin[jt * sub:jt * sub + 2 * sub, cols], (((1,), (1,)), ((), ())),
                            preferred_element_type=F32)
        kpos = i * tile + jt * sub - hq + lax.broadcasted_iota(jnp.int32, (1, 2 * sub), 1)
        valid = jnp.logical_and(kpos >= 0, kpos < class_len)
        logits.append(jnp.where(valid, s + bias_ref[j], NEG_INF))
    s_all = jnp.concatenate(logits, axis=0)
    m = jnp.max(s_all, axis=-1, keepdims=True)
    p32 = jnp.exp2(s_all - m)
    l = jnp.sum(p32, axis=-1, keepdims=True)
    p = p32.astype(BF16)
    rinv = 1.0 / l
    lse = m + jnp.log2(l)
    for jt in range(tile // sub):
        lse_all = jnp.zeros((sub, LANES), F32)
        for j in range(nblk):
            cols = slice(j * LANES, (j + 1) * LANES)
            r0 = (jt * nblk + j) * 2 * sub
            o2 = jnp.dot(p[r0:r0 + 2 * sub], vwin[jt * sub:jt * sub + 2 * sub, cols],
                         preferred_element_type=F32) * rinv[r0:r0 + 2 * sub]
            lse_all = jnp.where(lane == 2 * j, lse[r0:r0 + sub], lse_all)
            lse_all = jnp.where(lane == 2 * j + 1, lse[r0 + sub:r0 + 2 * sub], lse_all)
            o_ref[0, 0, jt * sub:(jt + 1) * sub, cols] = (
                jnp.where(lo, o2[:sub], o2[sub:]).astype(o_ref.dtype))
        lse_ref[0, 0, jt * sub:(jt + 1) * sub, :] = lse_all


def _dilated_branch(qkv_cm, t5_bias, window, dil):
    bsz, _, cl, width = qkv_cm.shape
    aw = A_HEADS * A_HEAD_DIM
    tile, hq = DIL_TILE, DIL_TQ // 2
    nt = cl // tile
    per = tile // hq
    nh = cl // hq

    def cur(col):
        return pl.BlockSpec((1, 1, tile, aw), lambda b, r, i: (b, r, i, col))

    def prev(col):
        return pl.BlockSpec((1, 1, hq, aw), lambda b, r, i: (b, r, jnp.maximum(i * per - 1, 0), col))

    def nxt(col):
        return pl.BlockSpec((1, 1, hq, aw),
                            lambda b, r, i: (b, r, jnp.minimum((i + 1) * per, nh - 1), col))

    return pl.pallas_call(
        functools.partial(_dil_kernel, class_len=cl),
        out_shape=(jax.ShapeDtypeStruct((bsz, dil, cl, aw), BF16),
                   jax.ShapeDtypeStruct((bsz, dil, cl, LANES), F32)),
        grid=(bsz, dil, nt),
        in_specs=[cur(0), prev(1), cur(1), nxt(1), prev(2), cur(2), nxt(2),
                  _const_spec((A_HEADS // 2, 2 * DIL_TQ, 2 * DIL_TQ))],
        out_specs=(pl.BlockSpec((1, 1, tile, aw), lambda b, r, i: (b, r, i, 0)),
                   pl.BlockSpec((1, 1, tile, LANES), lambda b, r, i: (b, r, i, 0))),
        compiler_params=_cparams("parallel", "parallel", "parallel"),
        name=f"dilated_d{dil}",
    )(*([qkv_cm] * 7),
      _dil_bias(t5_bias, window, dil, DIL_TQ).reshape(A_HEADS // 2, 2 * DIL_TQ, 2 * DIL_TQ))


def _dilmerge_kernel(*refs):
    nbr = len(DIL_CFG)
    a_refs, l_refs = refs[:nbr], refs[nbr:2 * nbr]
    e_ref, o_ref = refs[2 * nbr], refs[2 * nbr + 1]
    a_scs, l_scs = refs[2 * nbr + 2:3 * nbr + 2], refs[3 * nbr + 2:4 * nbr + 2]
    tm = o_ref.shape[1]
    accs, lses = [], []
    for (_, dil), a_ref, l_ref, a_sc, l_sc in zip(DIL_CFG, a_refs, l_refs, a_scs, l_scs):
        nblk = a_sc.shape[0]
        for r in range(dil):
            rows = pl.ds(r, tm // dil, stride=dil)
            for c in range(nblk):
                a_sc[c, rows, :] = a_ref[0, r, :, c * LANES:(c + 1) * LANES].astype(F32)
            l_sc[rows, :] = l_ref[0, r]
        accs.append(jnp.concatenate([a_sc[c] for c in range(nblk)], axis=1))
        lses.append(l_sc[...])
    mx = functools.reduce(jnp.maximum, lses)
    ws = [jnp.exp2(x - mx) for x in lses]
    tot = functools.reduce(jnp.add, ws)
    out = None
    for w, a in zip(ws, accs):
        w_hi, w_lo = _split2(w / tot)
        wide = (jnp.dot(w_hi, e_ref[...], preferred_element_type=F32)
                + jnp.dot(w_lo, e_ref[...], preferred_element_type=F32))
        out = wide * a if out is None else out + wide * a
    o_ref[0] = out.astype(o_ref.dtype)


def _dilated_merge(branch_outs, tm):
    bsz, _, _, aw = branch_outs[0][0].shape
    seq = branch_outs[0][0].shape[1] * branch_outs[0][0].shape[2]
    expand = np.zeros((LANES, aw), np.float32)
    for h in range(A_HEADS):
        expand[h, h * A_HEAD_DIM:(h + 1) * A_HEAD_DIM] = 1.0
    dils = [d for _, d in DIL_CFG]
    in_specs = ([pl.BlockSpec((1, d, tm // d, aw), lambda b, i: (b, 0, i, 0)) for d in dils]
                + [pl.BlockSpec((1, d, tm // d, LANES), lambda b, i: (b, 0, i, 0)) for d in dils]
                + [_const_spec(expand.shape)])
    return pl.pallas_call(
        _dilmerge_kernel,
        out_shape=jax.ShapeDtypeStruct((bsz, seq, aw), BF16),
        grid=(bsz, seq // tm),
        in_specs=in_specs,
        out_specs=pl.BlockSpec((1, tm, aw), lambda b, i: (b, i, 0)),
        scratch_shapes=([pltpu.VMEM((aw // LANES, tm, LANES), F32) for _ in dils]
                        + [pltpu.VMEM((tm, LANES), F32) for _ in dils]),
        compiler_params=_cparams("parallel", "parallel"),
        name="dilmerge",
    )(*[o for o, _ in branch_outs], *[l for _, l in branch_outs], jnp.asarray(expand, BF16))


def _hgrn_consts(reverse):
    t = HGRN_SUB
    r = np.arange(t)
    u = r[None, :]
    row = r[:, None]
    nmats = [(u >= row) if reverse else (u <= row)]
    masks = []
    m = t // 2
    while m >= 1:
        grp = r // (2 * m)
        in_first = (r % (2 * m)) < m
        same = grp[:, None] == grp[None, :]
        if reverse:
            beta = (grp * 2 * m + m)[:, None]
            n = np.where(in_first[:, None], (u >= row) & (u < beta), (u >= beta) & (u < row))
            mask = same & in_first[:, None] & ~in_first[None, :]
        else:
            beta = (grp * 2 * m + m - 1)[:, None]
            n = np.where(in_first[:, None], (u > row) & (u <= beta), (u > beta) & (u <= row))
            mask = same & ~in_first[:, None] & in_first[None, :]
        if m < HGRN_BCAST_MIN:
            nmats.append(n)
        masks.append(mask)
        m //= 2
    masks.append(np.eye(t, dtype=bool))
    nmat = jnp.asarray(np.concatenate(nmats, axis=0), F32).astype(BF16)
    return nmat, jnp.asarray(np.stack(masks), F32)


def _hgrn_block(q, f, v, lb, st, nmat_ref, mask_ref, reverse):
    t = HGRN_SUB
    nlev = mask_ref.shape[0] - 1
    qs = q * (B_DK ** -0.5)
    fa = lb + (1.0 - lb) * _sigmoid(f)
    kk = 1.0 - fa
    g_hi, g_lo = _split2(jnp.log(fa))
    ex = jnp.dot(nmat_ref[...], jnp.concatenate([g_hi, g_lo], axis=1), preferred_element_type=F32)
    ex = ex[:, :B_DK] + ex[:, B_DK:]
    b = ex[:t]
    btot = b[0:1] if reverse else b[t - 1:t]
    a = mask_ref[nlev] * _dot_nt(qs, kk)
    fine = 1
    for l in range(nlev):
        m = t >> (l + 1)
        if m >= HGRN_BCAST_MIN:
            ref = jnp.concatenate(
                [jnp.broadcast_to(b[beta:beta + 1], (2 * m, B_DK))
                 for beta in range(m if reverse else m - 1, t, 2 * m)], axis=0)
            e = jnp.exp(-jnp.abs(b - ref))
        else:
            e = jnp.exp(ex[fine * t:(fine + 1) * t])
            fine += 1
        a = a + mask_ref[l] * _dot_nt(qs * e, kk * e)
    out = _dot(a, v) + _dot_nt(qs * jnp.exp(b), st)
    khat = (kk * jnp.exp(btot - b)).astype(BF16)
    st_new = st * jnp.exp(btot) + jnp.dot(v.T.astype(BF16), khat, preferred_element_type=F32)
    return out, st_new


def _hgrn_lb(lg_ref, layer):
    lg = [lg_ref[l, 0] for l in range(lg_ref.shape[0])]
    mx = functools.reduce(jnp.maximum, lg)
    e = [jnp.exp(x - mx) for x in lg]
    return functools.reduce(jnp.add, e[:layer + 1]) / functools.reduce(jnp.add, e)


def _hgrn_kernel(qf_ref, ff_ref, vf_ref, qb_ref, fb_ref, vb_ref, lgf_ref, lgb_ref,
                 nf_ref, mf_ref, nb_ref, mb_ref, of_ref, ob_ref, sf_sc, sb_sc, *, layer):
    @pl.when(pl.program_id(2) == 0)
    def _():
        sf_sc[...] = jnp.zeros_like(sf_sc)
        sb_sc[...] = jnp.zeros_like(sb_sc)

    nsub = HGRN_BLOCK // HGRN_SUB
    chains = ((qf_ref, ff_ref, vf_ref, lgf_ref, sf_sc, nf_ref, mf_ref, of_ref, False),
              (qb_ref, fb_ref, vb_ref, lgb_ref, sb_sc, nb_ref, mb_ref, ob_ref, True))
    for q_ref, f_ref, v_ref, lg_ref, st_sc, n_ref, m_ref, o_ref, reverse in chains:
        lb = _hgrn_lb(lg_ref, layer)
        st = st_sc[...]
        for sub in (reversed(range(nsub)) if reverse else range(nsub)):
            rows = pl.ds(sub * HGRN_SUB, HGRN_SUB)
            o, st = _hgrn_block(q_ref[0, rows, :], f_ref[0, rows, :], v_ref[0, rows, :], lb, st,
                                n_ref, m_ref, reverse)
            o_ref[0, rows, :] = o.astype(o_ref.dtype)
        st_sc[...] = st


def _hgrn(z, lb_logits, layer, col0):
    bsz, seq, _ = z.shape
    t = HGRN_BLOCK
    nb = seq // t
    c0 = col0 // LANES
    hw = B_HEADS

    def zspec(group, rev):
        return pl.BlockSpec(
            (1, t, LANES),
            lambda b, h, j: (b, (nb - 1 - j) if rev else j, c0 + group * hw + h))

    def lgspec(direction):
        return pl.BlockSpec((lb_logits.shape[0], 1, 1, LANES),
                            lambda b, h, j: (0, direction * hw + h, 0, 0))

    nf, mf = _hgrn_consts(False)
    nbw, mbw = _hgrn_consts(True)
    lg = lb_logits.astype(F32).reshape(lb_logits.shape[0], 2 * hw, 1, LANES)
    o_shape = jax.ShapeDtypeStruct((bsz, seq, hw * B_DV), BF16)
    return pl.pallas_call(
        functools.partial(_hgrn_kernel, layer=layer),
        out_shape=(o_shape, o_shape),
        grid=(bsz, hw, nb),
        in_specs=[zspec(0, False), zspec(1, False), zspec(3, False),
                  zspec(0, True), zspec(2, True), zspec(3, True),
                  lgspec(0), lgspec(1),
                  _const_spec(nf.shape), _const_spec(mf.shape),
                  _const_spec(nbw.shape), _const_spec(mbw.shape)],
        out_specs=(pl.BlockSpec((1, t, LANES), lambda b, h, j: (b, j, h)),
                   pl.BlockSpec((1, t, LANES), lambda b, h, j: (b, nb - 1 - j, h))),
        scratch_shapes=[pltpu.VMEM((B_DV, B_DK), F32), pltpu.VMEM((B_DV, B_DK), F32)],
        compiler_params=_cparams("parallel", "parallel", "arbitrary"),
        name="hgrn",
    )(z, z, z, z, z, z, lg, lg, nf, mf, nbw, mbw)


def _residual_epilogue(y, x_ref, gain_ref, gate_ref, o_ref):
    o_ref[0] = x_ref[0] + gate_ref[0] * _rms(y, gain_ref[...])


def _out0_kernel(a_ref, of_ref, ob_ref, g_ref, on_ref, w_ref, x_ref, gain_ref, gate_ref, o_ref):
    o = of_ref[0].astype(F32) + ob_ref[0].astype(F32)
    g = g_ref[0]
    parts = [_rms(o[:, h * B_DV:(h + 1) * B_DV], on_ref[...]) for h in range(B_HEADS)]
    bn = jnp.concatenate(parts, axis=-1) * (g * _sigmoid(g))
    na = a_ref.shape[-1]
    y = _dot(a_ref[0], w_ref[:na, :]) + _dot(bn, w_ref[na:, :])
    _residual_epilogue(y, x_ref, gain_ref, gate_ref, o_ref)


def _out1_kernel(c_ref, d_ref, w_ref, x_ref, gain_ref, gate_ref, o_ref):
    c = jnp.concatenate([c_ref[0, g] for g in range(c_ref.shape[1])], axis=1)
    nc = c.shape[-1]
    y = _dot(c, w_ref[:nc, :]) + _dot(d_ref[0], w_ref[nc:, :])
    _residual_epilogue(y, x_ref, gain_ref, gate_ref, o_ref)


def _row_spec(tm, width, col=0):
    return pl.BlockSpec((1, tm, width), lambda b, i: (b, i, col))


def _out0(a_out, o_f, o_b, z, g_col, out_norm, w_bf16, x, gain, gate, tm):
    bsz, seq, d = x.shape
    wv = B_HEADS * B_DV
    vec = pl.BlockSpec((1, 1, d), lambda b, i: (b, 0, 0))
    return pl.pallas_call(
        _out0_kernel,
        out_shape=jax.ShapeDtypeStruct(x.shape, F32),
        grid=(bsz, seq // tm),
        in_specs=[_row_spec(tm, a_out.shape[-1]), _row_spec(tm, wv), _row_spec(tm, wv),
                  _row_spec(tm, wv, g_col // wv), _const_spec((1, B_DV)),
                  _const_spec(w_bf16.shape), _row_spec(tm, d), _const_spec((1, d)), vec],
        out_specs=_row_spec(tm, d),
        compiler_params=_cparams("parallel", "parallel"),
        name="out0",
    )(a_out, o_f, o_b, z, out_norm.reshape(1, B_DV), w_bf16, x, gain.reshape(1, d),
      gate.reshape(bsz, 1, d))


def _out1(c_out, d_out, w_bf16, x, gain, gate, tm):
    bsz, seq, d = x.shape
    vec = pl.BlockSpec((1, 1, d), lambda b, i: (b, 0, 0))
    return pl.pallas_call(
        _out1_kernel,
        out_shape=jax.ShapeDtypeStruct(x.shape, F32),
        grid=(bsz, seq // tm),
        in_specs=[pl.BlockSpec((1, c_out.shape[1], tm, c_out.shape[3]), lambda b, i: (b, 0, i, 0)),
                  _row_spec(tm, d_out.shape[-1]),
                  _const_spec(w_bf16.shape), _row_spec(tm, d), _const_spec((1, d)), vec],
        out_specs=_row_spec(tm, d),
        compiler_params=_cparams("parallel", "parallel"),
        name="out1",
    )(c_out, d_out, w_bf16, x, gain.reshape(1, d), gate.reshape(bsz, 1, d))


def _ffn_kernel(x_ref, g1_ref, sc_ref, sh_ref, wi_ref, wo_ref, g2_ref, gate_ref, o_ref, *, nchunk):
    x = x_ref[0]
    h = (_rms(x, g1_ref[...]) * (1.0 + sc_ref[0]) + sh_ref[0]).astype(BF16)
    hidden = wo_ref.shape[0]
    ck = hidden // nchunk
    y = None
    for c in range(nchunk):
        gt = jnp.dot(h, wi_ref[:, c * ck:(c + 1) * ck], preferred_element_type=F32)
        up = jnp.dot(h, wi_ref[:, hidden + c * ck:hidden + (c + 1) * ck], preferred_element_type=F32)
        part = _dot(gt * _sigmoid(gt) * up, wo_ref[c * ck:(c + 1) * ck, :])
        y = part if y is None else y + part
    o_ref[0] = x + gate_ref[0] * _rms(y, g2_ref[...])


def _ffn(x, g1, sc, sh, wi_bf16, wo_bf16, g2, gate, tm):
    bsz, seq, d = x.shape
    vec = pl.BlockSpec((1, 1, d), lambda b, i: (b, 0, 0))
    hidden = wo_bf16.shape[0]
    nchunk = 2 if (hidden // 2) % MXU_WIDTH == 0 else 1
    return pl.pallas_call(
        functools.partial(_ffn_kernel, nchunk=nchunk),
        out_shape=jax.ShapeDtypeStruct(x.shape, F32),
        grid=(bsz, seq // tm),
        in_specs=[_row_spec(tm, d), _const_spec((1, d)), vec, vec,
                  _const_spec(wi_bf16.shape), _const_spec(wo_bf16.shape),
                  _const_spec((1, d)), vec],
        out_specs=_row_spec(tm, d),
        compiler_params=_cparams("parallel", "parallel"),
        name="ffn",
    )(x, g1.reshape(1, d), sc.reshape(bsz, 1, d), sh.reshape(bsz, 1, d), wi_bf16, wo_bf16,
      g2.reshape(1, d), gate.reshape(bsz, 1, d))


def _fft_kernel(u_ref, f1_ref, twc_ref, tws_ref, f2_ref, fw_ref, o_ref, u_sc, p_sc, y_sc,
                *, scale, n1, n2):
    pu = n2 + FFT_PAD
    pp = 2 * n1 + FFT_PAD
    py = n1 + FFT_PAD
    f1 = f1_ref[...].astype(BF16)
    f2 = f2_ref[...].astype(BF16)
    fw = fw_ref[...].astype(BF16)
    for i1 in range(n1):
        u_sc[i1 * pu:i1 * pu + n2, :] = u_ref[0, 0, i1 * n2:(i1 + 1) * n2, :]

    nb = FFT_BATCH

    def stage1(blk, carry):
        i2s = [blk * nb + j for j in range(nb)]
        x = jnp.concatenate([u_sc[pl.ds(i2, n1, stride=pu), :] for i2 in i2s], axis=1)
        p = jnp.dot(f1, x.astype(BF16), preferred_element_type=F32)
        for j, i2 in enumerate(i2s):
            p_sc[pl.ds(pl.multiple_of(i2 * pp, 8), 2 * n1), :] = p[:, j * C_WIDTH:(j + 1) * C_WIDTH]
        return carry

    lax.fori_loop(0, n2 // nb, stage1, 0)

    def stage2(blk, carry):
        k1s = [blk * nb + j for j in range(nb)]
        qr, qi = [], []
        for k1 in k1s:
            tc = twc_ref[k1]
            ts = tws_ref[k1]
            pr = p_sc[pl.ds(k1, n2, stride=pp), :]
            pim = p_sc[pl.ds(n1 + k1, n2, stride=pp), :]
            qr.append(pr * tc + pim * ts)
            qi.append(pim * tc - pr * ts)
        q = jnp.concatenate([jnp.concatenate(qr, axis=1), jnp.concatenate(qi, axis=1)], axis=0)
        xx = jnp.dot(f2, q.astype(BF16), preferred_element_type=F32)
        xg = jnp.concatenate(
            [jnp.concatenate([xx[:n2, j * C_WIDTH:(j + 1) * C_WIDTH],
                              xx[n2:, j * C_WIDTH:(j + 1) * C_WIDTH]], axis=1) for j in range(nb)],
            axis=0)
        y = jnp.dot(xg.astype(BF16), fw, preferred_element_type=F32) * scale
        for j, k1 in enumerate(k1s):
            y_sc[pl.ds(k1, n2, stride=py), :] = y[j * n2:(j + 1) * n2]
        return carry

    lax.fori_loop(0, n1 // nb, stage2, 0)
    for k2 in range(n2):
        o_ref[0, 0, k2 * n1:(k2 + 1) * n1, :] = y_sc[k2 * py:k2 * py + n1, :]


def _fourier_mixer(u):
    bsz, ngroups, seq, width = u.shape
    n2 = FFT_N2
    n1 = seq // n2
    assert n1 * n2 == seq and width == C_WIDTH and n1 % 8 == 0
    a1 = 2.0 * np.pi * np.outer(np.arange(n1), np.arange(n1)) / n1
    f1 = np.concatenate([np.cos(a1), -np.sin(a1)], axis=0)
    a2 = 2.0 * np.pi * np.outer(np.arange(n2), np.arange(n2)) / n2
    c2, s2 = np.cos(a2), np.sin(a2)
    f2 = np.block([[c2, s2], [-s2, c2]])
    aw = 2.0 * np.pi * np.outer(np.arange(C_WIDTH), np.arange(C_WIDTH)) / C_WIDTH
    fw = np.concatenate([np.cos(aw), np.sin(aw)], axis=0)
    at = np.repeat((2.0 * np.pi * np.outer(np.arange(n1), np.arange(n2)) / seq)[:, :, None],
                   C_WIDTH, axis=2)
    consts = (jnp.asarray(f1, F32), jnp.asarray(np.cos(at), F32), jnp.asarray(np.sin(at), F32),
              jnp.asarray(f2, F32), jnp.asarray(fw, F32))
    blk = pl.BlockSpec((1, 1, seq, C_WIDTH), lambda b, g: (b, g, 0, 0))
    return pl.pallas_call(
        functools.partial(_fft_kernel, scale=float(1.0 / np.sqrt(seq * C_WIDTH)), n1=n1, n2=n2),
        out_shape=jax.ShapeDtypeStruct(u.shape, F32),
        grid=(bsz, ngroups),
        in_specs=[blk] + [_const_spec(c.shape) for c in consts],
        out_specs=blk,
        scratch_shapes=[pltpu.VMEM((n1 * (n2 + FFT_PAD), C_WIDTH), F32),
                        pltpu.VMEM((n2 * (2 * n1 + FFT_PAD), C_WIDTH), F32),
                        pltpu.VMEM((n2 * (n1 + FFT_PAD), C_WIDTH), F32)],
        compiler_params=_cparams("parallel", "parallel"),
        name="fft",
    )(u, *consts)


def _head_perm():
    rep = D_Q_HEADS // D_KV_HEADS
    cols = []
    for j in range(rep):
        for g in range(D_KV_HEADS):
            h = g * rep + j
            cols.extend(range(h * D_HEAD_DIM, (h + 1) * D_HEAD_DIM))
    return np.asarray(cols, np.int32)


def _rope_tables(seq):
    rows = seq // GRID_W
    row = jnp.repeat(jnp.arange(rows, dtype=F32), GRID_W)
    col = jnp.tile(jnp.arange(GRID_W, dtype=F32), rows)
    axis_dim = D_HEAD_DIM // 2
    inv_freq = jnp.power(ROPE_THETA, -jnp.arange(0, axis_dim, 2, dtype=F32) / axis_dim)
    ang_r = row[:, None] * inv_freq[None, :]
    ang_c = col[:, None] * inv_freq[None, :]
    cr, sr, cc, sc = jnp.cos(ang_r), jnp.sin(ang_r), jnp.cos(ang_c), jnp.sin(ang_c)
    cos = jnp.concatenate([cr, cr, cc, cc], axis=1)
    sin = jnp.concatenate([-sr, sr, -sc, sc], axis=1)
    reps = LANES // D_HEAD_DIM
    return jnp.tile(cos, (1, reps)), jnp.tile(sin, (1, reps))


def _inproj_qk_kernel(x_ref, gain_ref, sc_ref, sh_ref, w_ref, cos_ref, sin_ref, bd_h_ref, bd_l_ref,
                      gq_ref, gk_ref, u_ref, qo_ref, ko_ref, vo_ref):
    h = _rms(x_ref[0], gain_ref[...]) * (1.0 + sc_ref[0]) + sh_ref[0]
    z = _dot(h, w_ref[...])
    ngrp = u_ref.shape[1]
    for g in range(ngrp):
        u_ref[0, g] = z[:, g * LANES:(g + 1) * LANES]
    cos = cos_ref[...]
    sin = sin_ref[...]
    quarter = D_HEAD_DIM // 4
    lane = lax.broadcasted_iota(jnp.int32, (1, LANES), 1)
    first_of_pair = (lane // quarter) % 2 == 0

    def norm_rope(x, gain, scale):
        ms = _dot_tab(bd_h_ref[...], bd_l_ref[...], x * x, tab_left=False)
        xn = x * lax.rsqrt(ms + EPS) * gain
        partner = jnp.where(first_of_pair, pltpu.roll(xn, LANES - quarter, 1),
                            pltpu.roll(xn, quarter, 1))
        return ((xn * cos + partner * sin) * scale).astype(BF16)

    nq = qo_ref.shape[-1] // LANES
    for j in range(nq):
        qo_ref[0, :, j * LANES:(j + 1) * LANES] = norm_rope(
            z[:, (ngrp + j) * LANES:(ngrp + j + 1) * LANES], gq_ref[...], D_HEAD_DIM ** -0.5 * LOG2E)
    ko_ref[0] = norm_rope(z[:, (ngrp + nq) * LANES:(ngrp + nq + 1) * LANES], gk_ref[...], 1.0)
    vt = z[:, (ngrp + nq + 1) * LANES:(ngrp + nq + 2) * LANES].T
    ones = jnp.ones((FLASH_ONES, vt.shape[1]), F32)
    vo_ref[0] = jnp.concatenate(
        [piece for g in range(D_KV_HEADS)
         for piece in (vt[g * D_HEAD_DIM:(g + 1) * D_HEAD_DIM], ones)], axis=0).astype(BF16)


def _inproj_qk(x, gain, sc, sh, w_bf16, qk_norm_j, tm):
    bsz, seq, d = x.shape
    n = w_bf16.shape[1]
    qw = D_Q_HEADS * D_HEAD_DIM
    kw = D_KV_HEADS * D_HEAD_DIM
    assert kw == LANES and n == C_GROUPS * C_WIDTH + qw + 2 * kw
    vrows = D_KV_HEADS * (D_HEAD_DIM + FLASH_ONES)
    cos, sin = _rope_tables(seq)
    bd = np.kron(np.eye(LANES // D_HEAD_DIM), np.full((D_HEAD_DIM, D_HEAD_DIM), 1.0 / D_HEAD_DIM))
    bd_h, bd_l = _np_split2(bd)
    reps = LANES // D_HEAD_DIM
    gq = jnp.tile(qk_norm_j[0].astype(F32), reps).reshape(1, LANES)
    gk = jnp.tile(qk_norm_j[1].astype(F32), reps).reshape(1, LANES)
    tab = pl.BlockSpec((tm, LANES), lambda b, i: (i, 0))
    vec = pl.BlockSpec((1, 1, d), lambda b, i: (b, 0, 0))
    return pl.pallas_call(
        _inproj_qk_kernel,
        out_shape=(jax.ShapeDtypeStruct((bsz, C_GROUPS, seq, C_WIDTH), F32),
                   jax.ShapeDtypeStruct((bsz, seq, qw), BF16),
                   jax.ShapeDtypeStruct((bsz, seq, kw), BF16),
                   jax.ShapeDtypeStruct((bsz, vrows, seq), BF16)),
        grid=(bsz, seq // tm),
        in_specs=[pl.BlockSpec((1, tm, d), lambda b, i: (b, i, 0)),
                  _const_spec((1, d)), vec, vec, _const_spec((d, n)),
                  tab, tab, _const_spec(bd_h.shape), _const_spec(bd_l.shape),
                  _const_spec((1, LANES)), _const_spec((1, LANES))],
        out_specs=(pl.BlockSpec((1, C_GROUPS, tm, C_WIDTH), lambda b, i: (b, 0, i, 0)),
                   _row_spec(tm, qw), _row_spec(tm, kw),
                   pl.BlockSpec((1, vrows, tm), lambda b, i: (b, 0, i))),
        compiler_params=_cparams("parallel", "parallel"),
        name="inproj_qk",
    )(x, gain.reshape(1, d), sc.reshape(bsz, 1, d), sh.reshape(bsz, 1, d), w_bf16,
      cos, sin, bd_h, bd_l, gq, gk)


def _flash_kernel(q_ref, k_ref, vt_ref, o_ref, m_sc, acc_sc, s_sc):
    kv = pl.program_id(2)

    @pl.when(kv == 0)
    def _():
        m_sc[...] = jnp.full_like(m_sc, -jnp.inf)
        acc_sc[...] = jnp.zeros_like(acc_sc)

    lane = lax.broadcasted_iota(jnp.int32, (1, LANES), 1)
    lo = lane < D_HEAD_DIM
    nblk = q_ref.shape[-1] // LANES
    tq, tk = q_ref.shape[1], k_ref.shape[1]
    ku, qu = FLASH_KEY_UNIT, FLASH_QUERY_UNIT
    grows = D_HEAD_DIM + FLASH_ONES
    nheads = nblk * D_KV_HEADS
    k = k_ref[0]

    nchunk = tq // qu

    def logits_chunk(idx, c):
        j, g = divmod(idx, D_KV_HEADS)
        qj = q_ref[0, c * qu:(c + 1) * qu, j * LANES:(j + 1) * LANES]
        sel = lo if g == 0 else jnp.logical_not(lo)
        s = lax.dot_general(k, jnp.where(sel, qj, jnp.zeros_like(qj)), (((1,), (1,)), ((), ())),
                            preferred_element_type=F32)
        s_sc[idx, :, c * qu:(c + 1) * qu] = s
        return jnp.max(s, axis=0, keepdims=True)

    def finish_logits(idx, mcs):
        m_prev = m_sc[idx, 0:1, :]
        m_new = jnp.maximum(m_prev, jnp.concatenate(mcs, axis=1))
        m_sc[idx, 0:1, :] = m_new
        return m_new, jnp.exp2(m_prev - m_new)

    def value_chunk(idx, c, m_new, alpha):
        j, g = divmod(idx, D_KV_HEADS)
        rows = slice(g * grows, (g + 1) * grows)
        qcols = slice(c * qu, (c + 1) * qu)
        pv = None
        for u in range(tk // ku):
            keys = slice(u * ku, (u + 1) * ku)
            p = jnp.exp2(s_sc[idx, keys, qcols] - m_new[:, qcols])
            d = jnp.dot(vt_ref[0, rows, keys], p.astype(BF16), preferred_element_type=F32)
            pv = d if pv is None else pv + d
        acc_sc[j, rows, qcols] = alpha[:, qcols] * acc_sc[j, rows, qcols] + pv

    def logits_pass(idx):
        return finish_logits(idx, [logits_chunk(idx, c) for c in range(nchunk)])

    pending = [logits_pass(i) for i in range(min(FLASH_AHEAD, nheads))]
    for idx in range(nheads):
        if idx + FLASH_AHEAD < nheads:
            pending.append(logits_pass(idx + FLASH_AHEAD))
        stats = pending.pop(0)
        for c in range(nchunk):
            value_chunk(idx, c, *stats)

    @pl.when(kv == pl.num_programs(2) - 1)
    def _():
        for j in range(nblk):
            parts = []
            for g in range(D_KV_HEADS):
                num = acc_sc[j, g * grows:g * grows + D_HEAD_DIM, :]
                den = acc_sc[j, g * grows + D_HEAD_DIM:g * grows + D_HEAD_DIM + 1, :]
                parts.append(num / den)
            o_ref[0, :, j * LANES:(j + 1) * LANES] = (
                jnp.concatenate(parts, axis=0).T.astype(o_ref.dtype))


def _flash(q, k, vt, tq, tk):
    bsz, seq, qw = q.shape
    kw = k.shape[-1]
    vrows = vt.shape[1]
    return pl.pallas_call(
        _flash_kernel,
        out_shape=jax.ShapeDtypeStruct((bsz, seq, qw), BF16),
        grid=(bsz, seq // tq, seq // tk),
        in_specs=[pl.BlockSpec((1, tq, qw), lambda b, i, j: (b, i, 0)),
                  pl.BlockSpec((1, tk, kw), lambda b, i, j: (b, j, 0)),
                  pl.BlockSpec((1, vrows, tk), lambda b, i, j: (b, 0, j))],
        out_specs=pl.BlockSpec((1, tq, qw), lambda b, i, j: (b, i, 0)),
        scratch_shapes=[pltpu.VMEM((D_Q_HEADS, 8, tq), F32),
                        pltpu.VMEM((qw // LANES, vrows, tq), F32),
                        pltpu.VMEM((D_Q_HEADS, tk, tq), F32)],
        compiler_params=_cparams("parallel", "parallel", "arbitrary"),
        name="flash",
    )(q, k, vt)


def kernel(x, c, t5_bias, hgrn_lb_logits, ada_w, ada_b, norm_gains, ab_w_in, ab_w_out,
           hgrn_out_norm, cd_w_in, cd_w_out, qk_norm, ffn_w_in, ffn_w_out):
    bsz, seq, d = x.shape
    depth = ada_w.shape[0]
    mod = _ada_mod(c.astype(F32), ada_w, ada_b)
    perm = _head_perm()
    aw = A_HEADS * A_HEAD_DIM
    cw = C_GROUPS * C_WIDTH
    qw = D_Q_HEADS * D_HEAD_DIM
    tm_in = min(512, seq)
    tm = min(512, seq)
    for layer in range(depth):
        sh_m, sc_m, g_m, sh_f, sc_f, g_f = [mod[layer, :, i * d:(i + 1) * d] for i in range(6)]
        gains = norm_gains[layer]
        j = layer // 2
        if layer % 2 == 0:
            w_in = ab_w_in[j].astype(BF16)
            *qkv_cm, z = _inproj_cm(x, gains[0], sc_m, sh_m, w_in, 3 * aw, tm_in)
            branches = [_dilated_branch(cm, t5_bias, window, dil)
                        for cm, (window, dil) in zip(qkv_cm, DIL_CFG)]
            a_out = _dilated_merge(branches, tm)
            o_f, o_b = _hgrn(z, hgrn_lb_logits, layer, 0)
            g_col = 3 * B_HEADS * B_DK + B_HEADS * B_DV
            x = _out0(a_out, o_f, o_b, z, g_col, hgrn_out_norm[j], ab_w_out[j].astype(BF16),
                      x, gains[1], g_m, tm)
        else:
            w_full = cd_w_in[j]
            w_in = jnp.concatenate([w_full[:, :cw], w_full[:, cw:cw + qw][:, perm],
                                    w_full[:, cw + qw:]], axis=1).astype(BF16)
            u, qn, kn, vn = _inproj_qk(x, gains[0], sc_m, sh_m, w_in, qk_norm[j], tm)
            c_out = _fourier_mixer(u)
            d_out = _flash(qn, kn, vn, min(FLASH_TQ, seq), min(FLASH_TK, seq))
            w_out_full = cd_w_out[j]
            w_out = jnp.concatenate([w_out_full[:cw], w_out_full[cw:][perm]], axis=0).astype(BF16)
            x = _out1(c_out, d_out, w_out, x, gains[1], g_m, tm)
        x = _ffn(x, gains[2], sc_f, sh_f, ffn_w_in[layer].astype(BF16), ffn_w_out[layer].astype(BF16),
                 gains[3], g_f, tm)
    return x
```

```python
import functools

import numpy as np
import jax
import jax.numpy as jnp
from jax import lax
from jax.experimental import pallas as pl
from jax.experimental.pallas import tpu as pltpu

F32 = jnp.float32
BF16 = jnp.bfloat16
LANES = 128
MXU_WIDTH = 256
VMEM_LIMIT_BYTES = 56 * 2**20
NEG_INF = -1e30
EPS = 1e-6

GRID_W = 64
A_HEADS = 8
A_HEAD_DIM = 64
DIL_CFG = ((128, 1), (512, 4), (2048, 16))
N_BUCKETS = 32
T5_MAX_DIST = 1024
B_HEADS = 4
B_DK = 128
B_DV = 128
C_GROUPS = 4
C_WIDTH = 128
D_Q_HEADS = 8
D_KV_HEADS = 2
D_HEAD_DIM = 64
ROPE_THETA = 10000.0

DIL_TQ = 128
DIL_TILE = 512
HGRN_BLOCK = 1024
HGRN_SUB = 256
HGRN_BCAST_MIN = 8
FLASH_TQ = 1024
FLASH_TK = 1024
FLASH_KEY_UNIT = 256
FLASH_QUERY_UNIT = 256
FLASH_AHEAD = 1
FLASH_ONES = 16
FFT_N2 = 128
FFT_BATCH = 8
FFT_PAD = 8
LOG2E = 1.4426950408889634


def _cparams(*sem):
    return pltpu.CompilerParams(dimension_semantics=sem, vmem_limit_bytes=VMEM_LIMIT_BYTES)


def _const_spec(shape):
    nd = len(shape)
    return pl.BlockSpec(shape, lambda *_: (0,) * nd, pipeline_mode=pl.Buffered(1))


def _sigmoid(x):
    return 1.0 / (1.0 + jnp.exp(-x))


def _dot(a, b):
    return jnp.dot(a.astype(BF16), b.astype(BF16), preferred_element_type=F32)


def _dot_nt(a, b):
    return lax.dot_general(a.astype(BF16), b.astype(BF16), (((1,), (1,)), ((), ())),
                           preferred_element_type=F32)


def _split2(a):
    hi = a.astype(BF16)
    lo = (a - hi.astype(F32)).astype(BF16)
    return hi, lo


def _split3(a):
    a1 = a.astype(BF16)
    r = a - a1.astype(F32)
    a2 = r.astype(BF16)
    a3 = (r - a2.astype(F32)).astype(BF16)
    return a1, a2, a3


def _dot_tab(tab_hi, tab_lo, x, *, tab_left):
    x_hi, x_lo = _split2(x)
    if tab_left:
        d = lambda t, v: jnp.dot(t, v, preferred_element_type=F32)
    else:
        d = lambda t, v: jnp.dot(v, t, preferred_element_type=F32)
    return d(tab_hi, x_hi) + (d(tab_hi, x_lo) + d(tab_lo, x_hi))


def _rms(x, gain):
    ms = jnp.mean(x * x, axis=-1, keepdims=True)
    return x * lax.rsqrt(ms + EPS) * gain


def _np_split2(t):
    t = np.asarray(t, np.float32)
    hi = jnp.asarray(t, F32).astype(BF16)
    lo = (jnp.asarray(t, F32) - hi.astype(F32)).astype(BF16)
    return hi, lo


def _mod_kernel(c_ref, w_ref, b_ref, o_ref):
    c = c_ref[...]
    o_ref[0] = _dot(c * _sigmoid(c), w_ref[0]) + b_ref[0]


def _ada_mod(c, ada_w, ada_b):
    depth, d, n6 = ada_w.shape
    bsz = c.shape[0]
    rows = 8
    cp = jnp.zeros((rows, d), F32).at[:bsz].set(c)
    tn = n6 // 4
    out = pl.pallas_call(
        _mod_kernel,
        out_shape=jax.ShapeDtypeStruct((depth, rows, n6), F32),
        grid=(depth, n6 // tn),
        in_specs=[pl.BlockSpec((rows, d), lambda l, j: (0, 0)),
                  pl.BlockSpec((1, d, tn), lambda l, j: (l, 0, j)),
                  pl.BlockSpec((1, 1, tn), lambda l, j: (l, 0, j))],
        out_specs=pl.BlockSpec((1, rows, tn), lambda l, j: (l, 0, j)),
        compiler_params=_cparams("parallel", "parallel"),
        name="ada_mod",
    )(cp, ada_w, ada_b.reshape(depth, 1, n6))
    return out[:, :bsz]


def _inproj_cm_kernel(x_ref, gain_ref, sc_ref, sh_ref, w_ref, *refs):
    cm_refs, rest_ref, zs_sc, zc_sc = refs[:-3], refs[-3], refs[-2], refs[-1]
    h = _rms(x_ref[0], gain_ref[...]) * (1.0 + sc_ref[0]) + sh_ref[0]
    z = _dot(h, w_ref[...])
    nblk, tm, _ = zs_sc.shape
    rest_ref[0] = z[:, nblk * LANES:]
    nq = A_HEADS * A_HEAD_DIM // LANES
    for c in range(nblk):
        blk = z[:, c * LANES:(c + 1) * LANES]
        zs_sc[c] = blk * (A_HEAD_DIM ** -0.5 * LOG2E) if c < nq else blk
    src, sd = zs_sc, 1
    for level, (cm_ref, (_, dil)) in enumerate(zip(cm_refs, DIL_CFG)):
        step, n = dil // sd, tm // dil
        keep = dil > 1 and level + 1 < len(DIL_CFG)
        for rs in range(sd):
            for cc in range(step):
                r = rs + sd * cc
                for c in range(nblk):
                    rows = src[c, pl.ds(rs * (tm // sd) + cc, n, stride=step), :]
                    cm_ref[0, r, :, c * LANES:(c + 1) * LANES] = rows.astype(BF16)
                    if keep:
                        zc_sc[c, r * n:(r + 1) * n, :] = rows
        if keep:
            src, sd = zc_sc, dil


def _inproj_cm(x, gain, sc, sh, w_bf16, na, tm):
    bsz, seq, d = x.shape
    n = w_bf16.shape[1]
    vec = pl.BlockSpec((1, 1, d), lambda b, i: (b, 0, 0))
    dils = [dl for _, dl in DIL_CFG]
    assert dils[0] == 1 and all(b % a == 0 for a, b in zip(dils, dils[1:]))
    return pl.pallas_call(
        _inproj_cm_kernel,
        out_shape=tuple([jax.ShapeDtypeStruct((bsz, dl, seq // dl, na), BF16) for dl in dils]
                        + [jax.ShapeDtypeStruct((bsz, seq, n - na), F32)]),
        grid=(bsz, seq // tm),
        in_specs=[pl.BlockSpec((1, tm, d), lambda b, i: (b, i, 0)),
                  _const_spec((1, d)), vec, vec, _const_spec((d, n))],
        out_specs=tuple([pl.BlockSpec((1, dl, tm // dl, na), lambda b, i: (b, 0, i, 0)) for dl in dils]
                        + [pl.BlockSpec((1, tm, n - na), lambda b, i: (b, i, 0))]),
        scratch_shapes=[pltpu.VMEM((na // LANES, tm, LANES), F32),
                        pltpu.VMEM((na // LANES, tm, LANES), F32)],
        compiler_params=_cparams("parallel", "parallel"),
        name="inproj_cm",
    )(x, gain.reshape(1, d), sc.reshape(bsz, 1, d), sh.reshape(bsz, 1, d), w_bf16)


def _t5_buckets(rel):
    half = N_BUCKETS // 2
    max_exact = half // 2
    n = np.abs(rel)
    large = max_exact + (np.log(np.maximum(n, 1) / max_exact) / np.log(T5_MAX_DIST / max_exact)
                         * (half - max_exact)).astype(np.int32)
    large = np.minimum(large, half - 1)
    return (np.where(rel > 0, half, 0) + np.where(n < max_exact, n, large)).astype(np.int32)


def _dil_bias(t5_bias, window, dil, tq):
    half = (window // 2) // dil
    assert half == tq // 2
    rel = np.arange(2 * tq)[None, :] - half - np.arange(tq)[:, None]
    inside = np.abs(rel) <= half
    buckets = _t5_buckets(np.where(inside, rel, 0) * dil)
    onehot =jnp.asarray(np.eye(N_BUCKETS, dtype=np.float32)[buckets])
    bias = jnp.einsum("qkn,nh->hqk", onehot, t5_bias.astype(F32), precision=lax.Precision.HIGHEST)
    return jnp.where(jnp.asarray(inside)[None], bias * LOG2E, NEG_INF)


def _dil_kernel(q_ref, kp_ref, kc_ref, kn_ref, vp_ref, vc_ref, vn_ref, bias_ref, o_ref, lse_ref,
                *, class_len):
    i = pl.program_id(2)
    sub, hq, tile = DIL_TQ, DIL_TQ // 2, q_ref.shape[2]
    kwin = jnp.concatenate([kp_ref[0, 0], kc_ref[0, 0], kn_ref[0, 0]], axis=0)
    vwin = jnp.concatenate([vp_ref[0, 0], vc_ref[0, 0], vn_ref[0, 0]], axis=0)
    lane = lax.broadcasted_iota(jnp.int32, (1, LANES), 1)
    lo = lane < A_HEAD_DIM
    nblk = A_HEADS // 2
    units = [(jt, j) for jt in range(tile // sub) for j in range(nblk)]
    logits = []
    for jt, j in units:
        cols = slice(j * LANES, (j + 1) * LANES)
        qj = q_ref[0, 0, jt * sub:(jt + 1) * sub, cols]
        zero = jnp.zeros_like(qj)
        q2 = jnp.concatenate([jnp.where(lo, qj, zero), jnp.where(lo, zero, qj)], axis=0)
        s = lax.dot_general(q2, kwin[jt * sub:jt * sub + 2 * sub, cols], (((1,), (1,)), ((), ())),
                            preferred_element_type=F32)
        kpos = i * tile + jt * sub - hq + lax.broadcasted_iota(jnp.int32, (1, 2 * sub), 1)
        valid = jnp.logical_and(kpos >= 0, kpos < class_len)
        logits.append(jnp.where(valid, s + bias_ref[j], NEG_INF))
    s_all = jnp.concatenate(logits, axis=0)
    m = jnp.max(s_all, axis=-1, keepdims=True)
    p32 = jnp.exp2(s_all - m)
    l = jnp.sum(p32, axis=-1, keepdims=True)
    p = p32.astype(BF16)
    rinv = 1.0 / l
    lse = m + jnp.log2(l)
    for jt in range(tile // sub):
        lse_all = jnp.zeros((sub, LANES), F32)
        for j in range(nblk):
            cols = slice(j * LANES, (j + 1) * LANES)
            r0 = (jt * nblk + j) * 2 * sub
            o2 = jnp.dot(p[r0:r0 + 2 * sub], vwin[jt * sub:jt * sub + 2 * sub, cols],
                         preferred_element_type=F32) * rinv[r0:r0 + 2 * sub]
            lse_all = jnp.where(lane == 2 * j, lse[r0:r0 + sub], lse_all)
            lse_all = jnp.where(lane == 2 * j + 1, lse[r0 + sub:r0 + 2 * sub], lse_all)
            o_ref[0, 0, jt * sub:(jt + 1) * sub, cols] = (
                jnp.where(lo, o2[:sub], o2[sub:]).astype(o_ref.dtype))
        lse_ref[0, 0, jt * sub:(jt + 1) * sub, :] = lse_all


def _dilated_branch(qkv_cm, t5_bias, window, dil):
    bsz, _, cl, width = qkv_cm.shape
    aw = A_HEADS * A_HEAD_DIM
    tile, hq = min(DIL_TILE, cl), DIL_TQ // 2
    nt = cl // tile
    per = tile // hq
    nh = cl // hq

    def cur(col):
        return pl.BlockSpec((1, 1, tile, aw), lambda b, r, i: (b, r, i, col))

    def prev(col):
        return pl.BlockSpec((1, 1, hq, aw), lambda b, r, i: (b, r, jnp.maximum(i * per - 1, 0), col))

    def nxt(col):
        return pl.BlockSpec((1, 1, hq, aw),
                            lambda b, r, i: (b, r, jnp.minimum((i + 1) * per, nh - 1), col))

    return pl.pallas_call(
        functools.partial(_dil_kernel, class_len=cl),
        out_shape=(jax.ShapeDtypeStruct((bsz, dil, cl, aw), BF16),
                   jax.ShapeDtypeStruct((bsz, dil, cl, LANES), F32)),
        grid=(bsz, dil, nt),
        in_specs=[cur(0), prev(1), cur(1), nxt(1), prev(2), cur(2), nxt(2),
                  _const_spec((A_HEADS // 2, 2 * DIL_TQ, 2 * DIL_TQ))],
        out_specs=(pl.BlockSpec((1, 1, tile, aw), lambda b, r, i: (b, r, i, 0)),
                   pl.BlockSpec((1, 1, tile, LANES), lambda b, r, i: (b, r, i, 0))),
        compiler_params=_cparams("parallel", "parallel", "parallel"),
        name=f"dilated_d{dil}",
    )(*([qkv_cm] * 7),
      _dil_bias(t5_bias, window, dil, DIL_TQ).reshape(A_HEADS // 2, 2 * DIL_TQ, 2 * DIL_TQ))


def _dilmerge_kernel(*refs):
    nbr = len(DIL_CFG)
    a_refs, l_refs = refs[:nbr], refs[nbr:2 * nbr]
    e_ref, o_ref = refs[2 * nbr], refs[2 * nbr + 1]
    a_scs, l_scs = refs[2 * nbr + 2:3 * nbr + 2], refs[3 * nbr + 2:4 * nbr + 2]
    tm = o_ref.shape[1]
    accs, lses = [], []
    for (_, dil), a_ref, l_ref, a_sc, l_sc in zip(DIL_CFG, a_refs, l_refs, a_scs, l_scs):
        nblk = a_sc.shape[0]
        for r in range(dil):
            rows = pl.ds(r, tm // dil, stride=dil)
            for c in range(nblk):
                a_sc[c, rows, :] = a_ref[0, r, :, c * LANES:(c + 1) * LANES].astype(F32)
            l_sc[rows, :] = l_ref[0, r]
        accs.append(jnp.concatenate([a_sc[c] for c in range(nblk)], axis=1))
        lses.append(l_sc[...])
    mx = functools.reduce(jnp.maximum, lses)
    ws = [jnp.exp2(x - mx) for x in lses]
    tot = functools.reduce(jnp.add, ws)
    out = None
    for w, a in zip(ws, accs):
        w_hi, w_lo = _split2(w / tot)
        wide = (jnp.dot(w_hi, e_ref[...], preferred_element_type=F32)
                + jnp.dot(w_lo, e_ref[...], preferred_element_type=F32))
        out = wide * a if out is None else out + wide * a
    o_ref[0] = out.astype(o_ref.dtype)


def _dilated_merge(branch_outs, tm):
    bsz, _, _, aw = branch_outs[0][0].shape
    seq = branch_outs[0][0].shape[1] * branch_outs[0][0].shape[2]
    expand = np.zeros((LANES, aw), np.float32)
    for h in range(A_HEADS):
        expand[h, h * A_HEAD_DIM:(h + 1) * A_HEAD_DIM] = 1.0
    dils = [d for _, d in DIL_CFG]
    in_specs = ([pl.BlockSpec((1, d, tm // d, aw), lambda b, i: (b, 0, i, 0)) for d in dils]
                + [pl.BlockSpec((1, d, tm // d, LANES), lambda b, i: (b, 0, i, 0)) for d in dils]
                + [_const_spec(expand.shape)])
    return pl.pallas_call(
        _dilmerge_kernel,
        out_shape=jax.ShapeDtypeStruct((bsz, seq, aw), BF16),
        grid=(bsz, seq // tm),
        in_specs=in_specs,
        out_specs=pl.BlockSpec((1, tm, aw), lambda b, i: (b, i, 0)),
        scratch_shapes=([pltpu.VMEM((aw // LANES, tm, LANES), F32) for _ in dils]
                        + [pltpu.VMEM((tm, LANES), F32) for _ in dils]),
        compiler_params=_cparams("parallel", "parallel"),
        name="dilmerge",
    )(*[o for o, _ in branch_outs], *[l for _, l in branch_outs], jnp.asarray(expand, BF16))


def _hgrn_consts(reverse):
    t = HGRN_SUB
    r = np.arange(t)
    u = r[None, :]
    row = r[:, None]
    nmats = [(u >= row) if reverse else (u <= row)]
    masks = []
    m = t // 2
    while m >= 1:
        grp = r // (2 * m)
        in_first = (r % (2 * m)) < m
        same = grp[:, None] == grp[None, :]
        if reverse:
            beta = (grp * 2 * m + m)[:, None]
            n = np.where(in_first[:, None], (u >= row) & (u < beta), (u >= beta) & (u < row))
            mask = same & in_first[:, None] & ~in_first[None, :]
        else:
            beta = (grp * 2 * m + m - 1)[:, None]
            n = np.where(in_first[:, None], (u > row) & (u <= beta), (u > beta) & (u <= row))
            mask = same & ~in_first[:, None] & in_first[None, :]
        if m < HGRN_BCAST_MIN:
            nmats.append(n)
        masks.append(mask)
        m //= 2
    masks.append(np.eye(t, dtype=bool))
    nmat = jnp.asarray(np.concatenate(nmats, axis=0), F32).astype(BF16)
    return nmat, jnp.asarray(np.stack(masks), F32)


def _hgrn_block(q, f, v, lb, st, nmat_ref, mask_ref, reverse):
    t = HGRN_SUB
    nlev = mask_ref.shape[0] - 1
    qs = q * (B_DK ** -0.5)
    fa = lb + (1.0 - lb) * _sigmoid(f)
    kk = 1.0 - fa
    g_hi, g_lo = _split2(jnp.log(fa))
    ex = jnp.dot(nmat_ref[...], jnp.concatenate([g_hi, g_lo], axis=1), preferred_element_type=F32)
    ex = ex[:, :B_DK] + ex[:, B_DK:]
    b = ex[:t]
    btot = b[0:1] if reverse else b[t - 1:t]
    a = mask_ref[nlev] * _dot_nt(qs, kk)
    fine = 1
    for l in range(nlev):
        m = t >> (l + 1)
        if m >= HGRN_BCAST_MIN:
            ref = jnp.concatenate(
                [jnp.broadcast_to(b[beta:beta + 1], (2 * m, B_DK))
                 for beta in range(m if reverse else m - 1, t, 2 * m)], axis=0)
            e = jnp.exp(-jnp.abs(b - ref))
        else:
            e = jnp.exp(ex[fine * t:(fine + 1) * t])
            fine += 1
        a = a + mask_ref[l] * _dot_nt(qs * e, kk * e)
    out = _dot(a, v) + _dot_nt(qs * jnp.exp(b), st)
    khat = (kk * jnp.exp(btot - b)).astype(BF16)
    st_new = st * jnp.exp(btot) + jnp.dot(v.T.astype(BF16), khat, preferred_element_type=F32)
    return out, st_new


def _hgrn_lb(lg_ref, layer):
    lg = [lg_ref[l, 0] for l in range(lg_ref.shape[0])]
    mx = functools.reduce(jnp.maximum, lg)
    e = [jnp.exp(x - mx) for x in lg]
    return functools.reduce(jnp.add, e[:layer + 1]) / functools.reduce(jnp.add, e)


def _hgrn_kernel(qf_ref, ff_ref, vf_ref, qb_ref, fb_ref, vb_ref, lgf_ref, lgb_ref,
                 nf_ref, mf_ref, nb_ref, mb_ref, of_ref, ob_ref, sf_sc, sb_sc, *, layer):
    @pl.when(pl.program_id(2) == 0)
    def _():
        sf_sc[...] = jnp.zeros_like(sf_sc)
        sb_sc[...] = jnp.zeros_like(sb_sc)

    nsub = HGRN_BLOCK // HGRN_SUB
    chains = ((qf_ref, ff_ref, vf_ref, lgf_ref, sf_sc, nf_ref, mf_ref, of_ref, False),
              (qb_ref, fb_ref, vb_ref, lgb_ref, sb_sc, nb_ref, mb_ref, ob_ref, True))
    for q_ref, f_ref, v_ref, lg_ref, st_sc, n_ref, m_ref, o_ref, reverse in chains:
        lb = _hgrn_lb(lg_ref, layer)
        st = st_sc[...]
        for sub in (reversed(range(nsub)) if reverse else range(nsub)):
            rows = pl.ds(sub * HGRN_SUB, HGRN_SUB)
            o, st = _hgrn_block(q_ref[0, rows, :], f_ref[0, rows, :], v_ref[0, rows, :], lb, st,
                                n_ref, m_ref, reverse)
            o_ref[0, rows, :] = o.astype(o_ref.dtype)
        st_sc[...] = st


def _hgrn(z, lb_logits, layer, col0):
    bsz, seq, _ = z.shape
    t = HGRN_BLOCK
    nb = seq // t
    c0 = col0 // LANES
    hw = B_HEADS

    def zspec(group, rev):
        return pl.BlockSpec(
            (1, t, LANES),
            lambda b, h, j: (b, (nb - 1 - j) if rev else j, c0 + group * hw + h))

    def lgspec(direction):
        return pl.BlockSpec((lb_logits.shape[0], 1, 1, LANES),
                            lambda b, h, j: (0, direction * hw + h, 0, 0))

    nf, mf = _hgrn_consts(False)
    nbw, mbw = _hgrn_consts(True)
    lg = lb_logits.astype(F32).reshape(lb_logits.shape[0], 2 * hw, 1, LANES)
    o_shape = jax.ShapeDtypeStruct((bsz, seq, hw * B_DV), BF16)
    return pl.pallas_call(
        functools.partial(_hgrn_kernel, layer=layer),
        out_shape=(o_shape, o_shape),
        grid=(bsz, hw, nb),
        in_specs=[zspec(0, False), zspec(1, False), zspec(3, False),
                  zspec(0, True), zspec(2, True), zspec(3, True),
                  lgspec(0), lgspec(1),
                  _const_spec(nf.shape), _const_spec(mf.shape),
                  _const_spec(nbw.shape), _const_spec(mbw.shape)],
        out_specs=(pl.BlockSpec((1, t, LANES), lambda b, h, j: (b, j, h)),
                   pl.BlockSpec((1, t, LANES), lambda b, h, j: (b, nb - 1 - j, h))),
        scratch_shapes=[pltpu.VMEM((B_DV, B_DK), F32), pltpu.VMEM((B_DV, B_DK), F32)],
        compiler_params=_cparams("parallel", "parallel", "arbitrary"),
        name="hgrn",
    )(z, z, z, z, z, z, lg, lg, nf, mf, nbw, mbw)


def _residual_epilogue(y, x_ref, gain_ref, gate_ref, o_ref):
    o_ref[0] = x_ref[0] + gate_ref[0] * _rms(y, gain_ref[...])


def _out0_kernel(a_ref, of_ref, ob_ref, g_ref, on_ref, w_ref, x_ref, gain_ref, gate_ref, o_ref):
    o = of_ref[0].astype(F32) + ob_ref[0].astype(F32)
    g = g_ref[0]
    parts = [_rms(o[:, h * B_DV:(h + 1) * B_DV], on_ref[...]) for h in range(B_HEADS)]
    bn = jnp.concatenate(parts, axis=-1) * (g * _sigmoid(g))
    na = a_ref.shape[-1]
    y = _dot(a_ref[0], w_ref[:na, :]) + _dot(bn, w_ref[na:, :])
    _residual_epilogue(y, x_ref, gain_ref, gate_ref, o_ref)


def _out1_kernel(c_ref, d_ref, w_ref, x_ref, gain_ref, gate_ref, o_ref):
    c = jnp.concatenate([c_ref[0, g] for g in range(c_ref.shape[1])], axis=1)
    nc = c.shape[-1]
    y = _dot(c, w_ref[:nc, :]) + _dot(d_ref[0], w_ref[nc:, :])
    _residual_epilogue(y, x_ref, gain_ref, gate_ref, o_ref)


def _row_spec(tm, width, col=0):
    return pl.BlockSpec((1, tm, width), lambda b, i: (b, i, col))


def _out0(a_out, o_f, o_b, z, g_col, out_norm, w_bf16, x, gain, gate, tm):
    bsz, seq, d = x.shape
    wv = B_HEADS * B_DV
    vec = pl.BlockSpec((1, 1, d), lambda b, i: (b, 0, 0))
    return pl.pallas_call(
        _out0_kernel,
        out_shape=jax.ShapeDtypeStruct(x.shape, F32),
        grid=(bsz, seq // tm),
        in_specs=[_row_spec(tm, a_out.shape[-1]), _row_spec(tm, wv), _row_spec(tm, wv),
                  _row_spec(tm, wv, g_col // wv), _const_spec((1, B_DV)),
                  _const_spec(w_bf16.shape), _row_spec(tm, d), _const_spec((1, d)), vec],
        out_specs=_row_spec(tm, d),
        compiler_params=_cparams("parallel", "parallel"),
        name="out0",
    )(a_out, o_f, o_b, z, out_norm.reshape(1, B_DV), w_bf16, x, gain.reshape(1, d),
      gate.reshape(bsz, 1, d))


def _out1(c_out, d_out, w_bf16, x, gain, gate, tm):
    bsz, seq, d = x.shape
    vec = pl.BlockSpec((1, 1, d), lambda b, i: (b, 0, 0))
    return pl.pallas_call(
        _out1_kernel,
        out_shape=jax.ShapeDtypeStruct(x.shape, F32),
        grid=(bsz, seq // tm),
        in_specs=[pl.BlockSpec((1, c_out.shape[1], tm, c_out.shape[3]), lambda b, i: (b, 0, i, 0)),
                  _row_spec(tm, d_out.shape[-1]),
                  _const_spec(w_bf16.shape), _row_spec(tm, d), _const_spec((1, d)), vec],
        out_specs=_row_spec(tm, d),
        compiler_params=_cparams("parallel", "parallel"),
        name="out1",
    )(c_out, d_out, w_bf16, x, gain.reshape(1, d), gate.reshape(bsz, 1, d))


def _ffn_kernel(x_ref, g1_ref, sc_ref, sh_ref, wi_ref, wo_ref, g2_ref, gate_ref, o_ref, *, nchunk):
    x = x_ref[0]
    h = (_rms(x, g1_ref[...]) * (1.0 + sc_ref[0]) + sh_ref[0]).astype(BF16)
    hidden = wo_ref.shape[0]
    ck = hidden // nchunk
    y = None
    for c in range(nchunk):
        gt = jnp.dot(h, wi_ref[:, c * ck:(c + 1) * ck], preferred_element_type=F32)
        up = jnp.dot(h, wi_ref[:, hidden + c * ck:hidden + (c + 1) * ck], preferred_element_type=F32)
        part = _dot(gt * _sigmoid(gt) * up, wo_ref[c * ck:(c + 1) * ck, :])
        y = part if y is None else y + part
    o_ref[0] = x + gate_ref[0] * _rms(y, g2_ref[...])


def _ffn(x, g1, sc, sh, wi_bf16, wo_bf16, g2, gate, tm):
    bsz, seq, d = x.shape
    vec = pl.BlockSpec((1, 1, d), lambda b, i: (b, 0, 0))
    hidden = wo_bf16.shape[0]
    nchunk = 2 if (hidden // 2) % MXU_WIDTH == 0 else 1
    return pl.pallas_call(
        functools.partial(_ffn_kernel, nchunk=nchunk),
        out_shape=jax.ShapeDtypeStruct(x.shape, F32),
        grid=(bsz, seq // tm),
        in_specs=[_row_spec(tm, d), _const_spec((1, d)), vec, vec,
                  _const_spec(wi_bf16.shape), _const_spec(wo_bf16.shape),
                  _const_spec((1, d)), vec],
        out_specs=_row_spec(tm, d),
        compiler_params=_cparams("parallel", "parallel"),
        name="ffn",
    )(x, g1.reshape(1, d), sc.reshape(bsz, 1, d), sh.reshape(bsz, 1, d), wi_bf16, wo_bf16,
      g2.reshape(1, d), gate.reshape(bsz, 1, d))


def _fft_kernel(u_ref, f1_ref, twc_ref, tws_ref, f2_ref, fw_ref, o_ref, u_sc, p_sc, y_sc,
                *, scale, n1, n2):
    pu = n2 + FFT_PAD
    pp = 2 * n1 + FFT_PAD
    py = n1 + FFT_PAD
    f1 = f1_ref[...].astype(BF16)
    f2 = f2_ref[...].astype(BF16)
    fw = fw_ref[...].astype(BF16)
    for i1 in range(n1):
        u_sc[i1 * pu:i1 * pu + n2, :] = u_ref[0, 0, i1 * n2:(i1 + 1) * n2, :]

    nb = FFT_BATCH

    def stage1(blk, carry):
        i2s = [blk * nb + j for j in range(nb)]
        x = jnp.concatenate([u_sc[pl.ds(i2, n1, stride=pu), :] for i2 in i2s], axis=1)
        p = jnp.dot(f1, x.astype(BF16), preferred_element_type=F32)
        for j, i2 in enumerate(i2s):
            p_sc[pl.ds(pl.multiple_of(i2 * pp, 8), 2 * n1), :] = p[:, j * C_WIDTH:(j + 1) * C_WIDTH]
        return carry

    lax.fori_loop(0, n2 // nb, stage1, 0)

    def stage2(blk, carry):
        k1s = [blk * nb + j for j in range(nb)]
        qr, qi = [], []
        for k1 in k1s:
            tc = twc_ref[k1]
            ts = tws_ref[k1]
            pr = p_sc[pl.ds(k1, n2, stride=pp), :]
            pim = p_sc[pl.ds(n1 + k1, n2, stride=pp), :]
            qr.append(pr * tc + pim * ts)
            qi.append(pim * tc - pr * ts)
        q = jnp.concatenate([jnp.concatenate(qr, axis=1), jnp.concatenate(qi, axis=1)], axis=0)
        xx = jnp.dot(f2, q.astype(BF16), preferred_element_type=F32)
        xg = jnp.concatenate(
            [jnp.concatenate([xx[:n2, j * C_WIDTH:(j + 1) * C_WIDTH],
                              xx[n2:, j * C_WIDTH:(j + 1) * C_WIDTH]], axis=1) for j in range(nb)],
            axis=0)
        y = jnp.dot(xg.astype(BF16), fw, preferred_element_type=F32) * scale
        for j, k1 in enumerate(k1s):
            y_sc[pl.ds(k1, n2, stride=py), :] = y[j * n2:(j + 1) * n2]
        return carry

    lax.fori_loop(0, n1 // nb, stage2, 0)
    for k2 in range(n2):
        o_ref[0, 0, k2 * n1:(k2 + 1) * n1, :] = y_sc[k2 * py:k2 * py + n1, :]


def _fourier_mixer(u):
    bsz, ngroups, seq, width = u.shape
    n2 = FFT_N2
    n1 = seq // n2
    assert n1 * n2 == seq and width == C_WIDTH and n1 % 8 == 0
    a1 = 2.0 * np.pi * np.outer(np.arange(n1), np.arange(n1)) / n1
    f1 = np.concatenate([np.cos(a1), -np.sin(a1)], axis=0)
    a2 = 2.0 * np.pi * np.outer(np.arange(n2), np.arange(n2)) / n2
    c2, s2 = np.cos(a2), np.sin(a2)
    f2 = np.block([[c2, s2], [-s2, c2]])
    aw = 2.0 * np.pi * np.outer(np.arange(C_WIDTH), np.arange(C_WIDTH)) / C_WIDTH
    fw = np.concatenate([np.cos(aw), np.sin(aw)], axis=0)
    at = np.repeat((2.0 * np.pi * np.outer(np.arange(n1), np.arange(n2)) / seq)[:, :, None],
                   C_WIDTH, axis=2)
    consts = (jnp.asarray(f1, F32), jnp.asarray(np.cos(at), F32), jnp.asarray(np.sin(at), F32),
              jnp.asarray(f2, F32), jnp.asarray(fw, F32))
    blk = pl.BlockSpec((1, 1, seq, C_WIDTH), lambda b, g: (b, g, 0, 0))
    return pl.pallas_call(
        functools.partial(_fft_kernel, scale=float(1.0 / np.sqrt(seq * C_WIDTH)), n1=n1, n2=n2),
        out_shape=jax.ShapeDtypeStruct(u.shape, F32),
        grid=(bsz, ngroups),
        in_specs=[blk] + [_const_spec(c.shape) for c in consts],
        out_specs=blk,
        scratch_shapes=[pltpu.VMEM((n1 * (n2 + FFT_PAD), C_WIDTH), F32),
                        pltpu.VMEM((n2 * (2 * n1 + FFT_PAD), C_WIDTH), F32),
                        pltpu.VMEM((n2 * (n1 + FFT_PAD), C_WIDTH), F32)],
        compiler_params=_cparams("parallel", "parallel"),
        name="fft",
    )(u, *consts)


def _head_perm():
    rep = D_Q_HEADS // D_KV_HEADS
    cols = []
    for j in range(rep):
        for g in range(D_KV_HEADS):
            h = g * rep + j
            cols.extend(range(h * D_HEAD_DIM, (h + 1) * D_HEAD_DIM))
    return np.asarray(cols, np.int32)


def _rope_tables(seq):
    rows = seq // GRID_W
    row = jnp.repeat(jnp.arange(rows, dtype=F32), GRID_W)
    col = jnp.tile(jnp.arange(GRID_W, dtype=F32), rows)
    axis_dim = D_HEAD_DIM // 2
    inv_freq = jnp.power(ROPE_THETA, -jnp.arange(0, axis_dim, 2, dtype=F32) / axis_dim)
    ang_r = row[:, None] * inv_freq[None, :]
    ang_c = col[:, None] * inv_freq[None, :]
    cr, sr, cc, sc = jnp.cos(ang_r), jnp.sin(ang_r), jnp.cos(ang_c), jnp.sin(ang_c)
    cos = jnp.concatenate([cr, cr, cc, cc], axis=1)
    sin = jnp.concatenate([-sr, sr, -sc, sc], axis=1)
    reps = LANES // D_HEAD_DIM
    return jnp.tile(cos, (1, reps)), jnp.tile(sin, (1, reps))


def _inproj_qk_kernel(x_ref, gain_ref, sc_ref, sh_ref, w_ref, cos_ref, sin_ref, bd_h_ref, bd_l_ref,
                      gq_ref, gk_ref, u_ref, qo_ref, ko_ref, vo_ref):
    h = _rms(x_ref[0], gain_ref[...]) * (1.0 + sc_ref[0]) + sh_ref[0]
    z = _dot(h, w_ref[...])
    ngrp = u_ref.shape[1]
    for g in range(ngrp):
        u_ref[0, g] = z[:, g * LANES:(g + 1) * LANES]
    cos = cos_ref[...]
    sin = sin_ref[...]
    quarter = D_HEAD_DIM // 4
    lane = lax.broadcasted_iota(jnp.int32, (1, LANES), 1)
    first_of_pair = (lane // quarter) % 2 == 0

    def norm_rope(x, gain, scale):
        ms = _dot_tab(bd_h_ref[...], bd_l_ref[...], x * x, tab_left=False)
        xn = x * lax.rsqrt(ms + EPS) * gain
        partner = jnp.where(first_of_pair, pltpu.roll(xn, LANES - quarter, 1),
                            pltpu.roll(xn, quarter, 1))
        return ((xn * cos + partner * sin) * scale).astype(BF16)

    nq = qo_ref.shape[-1] // LANES
    for j in range(nq):
        qo_ref[0, :, j * LANES:(j + 1) * LANES] = norm_rope(
            z[:, (ngrp + j) * LANES:(ngrp + j + 1) * LANES], gq_ref[...], D_HEAD_DIM ** -0.5 * LOG2E)
    ko_ref[0] = norm_rope(z[:, (ngrp + nq) * LANES:(ngrp + nq + 1) * LANES], gk_ref[...], 1.0)
    vt = z[:, (ngrp + nq + 1) * LANES:(ngrp + nq + 2) * LANES].T
    ones = jnp.ones((FLASH_ONES, vt.shape[1]), F32)
    vo_ref[0] = jnp.concatenate(
        [piece for g in range(D_KV_HEADS)
         for piece in (vt[g * D_HEAD_DIM:(g + 1) * D_HEAD_DIM], ones)], axis=0).astype(BF16)


def _inproj_qk(x, gain, sc, sh, w_bf16, qk_norm_j, tm):
    bsz, seq, d = x.shape
    n = w_bf16.shape[1]
    qw = D_Q_HEADS * D_HEAD_DIM
    kw = D_KV_HEADS * D_HEAD_DIM
    assert kw == LANES and n == C_GROUPS * C_WIDTH + qw + 2 * kw
    vrows = D_KV_HEADS * (D_HEAD_DIM + FLASH_ONES)
    cos, sin = _rope_tables(seq)
    bd = np.kron(np.eye(LANES // D_HEAD_DIM), np.full((D_HEAD_DIM, D_HEAD_DIM), 1.0 / D_HEAD_DIM))
    bd_h, bd_l = _np_split2(bd)
    reps = LANES // D_HEAD_DIM
    gq = jnp.tile(qk_norm_j[0].astype(F32), reps).reshape(1, LANES)
    gk = jnp.tile(qk_norm_j[1].astype(F32), reps).reshape(1, LANES)
    tab = pl.BlockSpec((tm, LANES), lambda b, i: (i, 0))
    vec = pl.BlockSpec((1, 1, d), lambda b, i: (b, 0, 0))
    return pl.pallas_call(
        _inproj_qk_kernel,
        out_shape=(jax.ShapeDtypeStruct((bsz, C_GROUPS, seq, C_WIDTH), F32),
                   jax.ShapeDtypeStruct((bsz, seq, qw), BF16),
                   jax.ShapeDtypeStruct((bsz, seq, kw), BF16),
                   jax.ShapeDtypeStruct((bsz, vrows, seq), BF16)),
        grid=(bsz, seq // tm),
        in_specs=[pl.BlockSpec((1, tm, d), lambda b, i: (b, i, 0)),
                  _const_spec((1, d)), vec, vec, _const_spec((d, n)),
                  tab, tab, _const_spec(bd_h.shape), _const_spec(bd_l.shape),
                  _const_spec((1, LANES)), _const_spec((1, LANES))],
        out_specs=(pl.BlockSpec((1, C_GROUPS, tm, C_WIDTH), lambda b, i: (b, 0, i, 0)),
                   _row_spec(tm, qw), _row_spec(tm, kw),
                   pl.BlockSpec((1, vrows, tm), lambda b, i: (b, 0, i))),
        compiler_params=_cparams("parallel", "parallel"),
        name="inproj_qk",
    )(x, gain.reshape(1, d), sc.reshape(bsz, 1, d), sh.reshape(bsz, 1, d), w_bf16,
      cos, sin, bd_h, bd_l, gq, gk)


def _flash_kernel(q_ref, k_ref, vt_ref, o_ref, m_sc, acc_sc, s_sc):
    kv = pl.program_id(2)

    @pl.when(kv == 0)
    def _():
        m_sc[...] = jnp.full_like(m_sc, -jnp.inf)
        acc_sc[...] = jnp.zeros_like(acc_sc)

    lane = lax.broadcasted_iota(jnp.int32, (1, LANES), 1)
    lo = lane < D_HEAD_DIM
    nblk = q_ref.shape[-1] // LANES
    tq, tk = q_ref.shape[1], k_ref.shape[1]
    ku, qu = FLASH_KEY_UNIT, FLASH_QUERY_UNIT
    grows = D_HEAD_DIM + FLASH_ONES
    nheads = nblk * D_KV_HEADS
    k = k_ref[0]

    nchunk = tq // qu

    def logits_chunk(idx, c):
        j, g = divmod(idx, D_KV_HEADS)
        qj = q_ref[0, c * qu:(c + 1) * qu, j * LANES:(j + 1) * LANES]
        sel = lo if g == 0 else jnp.logical_not(lo)
        s = lax.dot_general(k, jnp.where(sel, qj, jnp.zeros_like(qj)), (((1,), (1,)), ((), ())),
                            preferred_element_type=F32)
        s_sc[idx, :, c * qu:(c + 1) * qu] = s
        return jnp.max(s, axis=0, keepdims=True)

    def finish_logits(idx, mcs):
        m_prev = m_sc[idx, 0:1, :]
        m_new = jnp.maximum(m_prev, jnp.concatenate(mcs, axis=1))
        m_sc[idx, 0:1, :] = m_new
        return m_new, jnp.exp2(m_prev - m_new)

    def value_chunk(idx, c, m_new, alpha):
        j, g = divmod(idx, D_KV_HEADS)
        rows = slice(g * grows, (g + 1) * grows)
        qcols = slice(c * qu, (c + 1) * qu)
        pv = None
        for u in range(tk // ku):
            keys = slice(u * ku, (u + 1) * ku)
            p = jnp.exp2(s_sc[idx, keys, qcols] - m_new[:, qcols])
            d = jnp.dot(vt_ref[0, rows, keys], p.astype(BF16), preferred_element_type=F32)
            pv = d if pv is None else pv + d
        acc_sc[j, rows, qcols] = alpha[:, qcols] * acc_sc[j, rows, qcols] + pv

    def logits_pass(idx):
        return finish_logits(idx, [logits_chunk(idx, c) for c in range(nchunk)])

    pending = [logits_pass(i) for i in range(min(FLASH_AHEAD, nheads))]
    for idx in range(nheads):
        if idx + FLASH_AHEAD < nheads:
            pending.append(logits_pass(idx + FLASH_AHEAD))
        stats = pending.pop(0)
        for c in range(nchunk):
            value_chunk(idx, c, *stats)

    @pl.when(kv == pl.num_programs(2) - 1)
    def _():
        for j in range(nblk):
            parts = []
            for g in range(D_KV_HEADS):
                num = acc_sc[j, g * grows:g * grows + D_HEAD_DIM, :]
                den = acc_sc[j, g * grows + D_HEAD_DIM:g * grows + D_HEAD_DIM + 1, :]
                parts.append(num / den)
            o_ref[0, :, j * LANES:(j + 1) * LANES] = (
                jnp.concatenate(parts, axis=0).T.astype(o_ref.dtype))


def _flash(q, k, vt, tq, tk):
    bsz, seq, qw = q.shape
    kw = k.shape[-1]
    vrows = vt.shape[1]
    return pl.pallas_call(
        _flash_kernel,
        out_shape=jax.ShapeDtypeStruct((bsz, seq, qw), BF16),
        grid=(bsz, seq // tq, seq // tk),
        in_specs=[pl.BlockSpec((1, tq, qw), lambda b, i, j: (b, i, 0)),
                  pl.BlockSpec((1, tk, kw), lambda b, i, j: (b, j, 0)),
                  pl.BlockSpec((1, vrows, tk), lambda b, i, j: (b, 0, j))],
        out_specs=pl.BlockSpec((1, tq, qw), lambda b, i, j: (b, i, 0)),
        scratch_shapes=[pltpu.VMEM((D_Q_HEADS, 8, tq), F32),
                        pltpu.VMEM((qw // LANES, vrows, tq), F32),
                        pltpu.VMEM((D_Q_HEADS, tk, tq), F32)],
        compiler_params=_cparams("parallel", "parallel", "arbitrary"),
        name="flash",
    )(q, k, vt)


def kernel(x, c, t5_bias, hgrn_lb_logits, ada_w, ada_b, norm_gains, ab_w_in, ab_w_out,
           hgrn_out_norm, cd_w_in, cd_w_out, qk_norm, ffn_w_in, ffn_w_out):
    bsz, seq, d = x.shape
    depth = ada_w.shape[0]
    mod = _ada_mod(c.astype(F32), ada_w, ada_b)
    perm = _head_perm()
    aw = A_HEADS * A_HEAD_DIM
    cw = C_GROUPS * C_WIDTH
    qw = D_Q_HEADS * D_HEAD_DIM
    tm_in = min(512, seq)
    tm = min(512, seq)
    for layer in range(depth):
        sh_m, sc_m, g_m, sh_f, sc_f, g_f = [mod[layer, :, i * d:(i + 1) * d] for i in range(6)]
        gains = norm_gains[layer]
        j = layer // 2
        if layer % 2 == 0:
            w_in = ab_w_in[j].astype(BF16)
            *qkv_cm, z = _inproj_cm(x, gains[0], sc_m, sh_m, w_in, 3 * aw, tm_in)
            branches = [_dilated_branch(cm, t5_bias, window, dil)
                        for cm, (window, dil) in zip(qkv_cm, DIL_CFG)]
            a_out = _dilated_merge(branches, tm)
            o_f, o_b = _hgrn(z, hgrn_lb_logits, layer, 0)
            g_col = 3 * B_HEADS * B_DK + B_HEADS * B_DV
            x = _out0(a_out, o_f, o_b, z, g_col, hgrn_out_norm[j], ab_w_out[j].astype(BF16),
                      x, gains[1], g_m, tm)
        else:
            w_full = cd_w_in[j]
            w_in = jnp.concatenate([w_full[:, :cw], w_full[:, cw:cw + qw][:, perm],
                                    w_full[:, cw + qw:]], axis=1).astype(BF16)
            u, qn, kn, vn = _inproj_qk(x, gains[0], sc_m, sh_m, w_in, qk_norm[j], min(1024, seq))
            c_out = _fourier_mixer(u)
            d_out = _flash(qn, kn, vn, min(FLASH_TQ, seq), min(FLASH_TK, seq))
            w_out_full = cd_w_out[j]
            w_out = jnp.concatenate([w_out_full[:cw], w_out_full[cw:][perm]], axis=0).astype(BF16)
            x = _out1(c_out, d_out, w_out, x, gains[1], g_m, tm)
        x = _ffn(x, gains[2], sc_f, sh_f, ffn_w_in[layer].astype(BF16), ffn_w_out[layer].astype(BF16),
                 gains[3], g_f, tm)
    return x
```

```python
import functools

import numpy as np
import jax
import jax.numpy as jnp
from jax import lax
from jax.experimental import pallas as pl
from jax.experimental.pallas import tpu as pltpu

F32 = jnp.float32
BF16 = jnp.bfloat16
LANES = 128
MXU_WIDTH = 256
VMEM_LIMIT_BYTES = 56 * 2**20
NEG_INF = -1e30
EPS = 1e-6

GRID_W = 64
A_HEADS = 8
A_HEAD_DIM = 64
DIL_CFG = ((128, 1), (512, 4), (2048, 16))
N_BUCKETS = 32
T5_MAX_DIST = 1024
B_HEADS = 4
B_DK = 128
B_DV = 128
C_GROUPS = 4
C_WIDTH = 128
D_Q_HEADS = 8
D_KV_HEADS = 2
D_HEAD_DIM = 64
ROPE_THETA = 10000.0

DIL_TQ = 128
DIL_TILE = 512
HGRN_BLOCK = 1024
HGRN_SUB = 256
HGRN_BCAST_MIN = 8
FLASH_TQ = 1024
FLASH_TK = 1024
FLASH_KEY_UNIT = 256
FLASH_QUERY_UNIT = 256
FLASH_AHEAD = 1
FLASH_ONES = 16
FFT_N2 = 128
FFT_BATCH = 8
FFT_PAD = 8
LOG2E = 1.4426950408889634


def _cparams(*sem):
    return pltpu.CompilerParams(dimension_semantics=sem, vmem_limit_bytes=VMEM_LIMIT_BYTES)


def _const_spec(shape):
    nd = len(shape)
    return pl.BlockSpec(shape, lambda *_: (0,) * nd, pipeline_mode=pl.Buffered(1))


def _sigmoid(x):
    return 1.0 / (1.0 + jnp.exp(-x))


def _dot(a, b):
    return jnp.dot(a.astype(BF16), b.astype(BF16), preferred_element_type=F32)


def _dot_nt(a, b):
    return lax.dot_general(a.astype(BF16), b.astype(BF16), (((1,), (1,)), ((), ())),
                           preferred_element_type=F32)


def _split2(a):
    hi = a.astype(BF16)
    lo = (a - hi.astype(F32)).astype(BF16)
    return hi, lo


def _split3(a):
    a1 = a.astype(BF16)
    r = a - a1.astype(F32)
    a2 = r.astype(BF16)
    a3 = (r - a2.astype(F32)).astype(BF16)
    return a1, a2, a3


def _dot_tab(tab_hi, tab_lo, x, *, tab_left):
    x_hi, x_lo = _split2(x)
    if tab_left:
        d = lambda t, v: jnp.dot(t, v, preferred_element_type=F32)
    else:
        d = lambda t, v: jnp.dot(v, t, preferred_element_type=F32)
    return d(tab_hi, x_hi) + (d(tab_hi, x_lo) + d(tab_lo, x_hi))


def _rms(x, gain):
    ms = jnp.mean(x * x, axis=-1, keepdims=True)
    return x * lax.rsqrt(ms + EPS) * gain


def _np_split2(t):
    t = np.asarray(t, np.float32)
    hi = jnp.asarray(t, F32).astype(BF16)
    lo = (jnp.asarray(t, F32) - hi.astype(F32)).astype(BF16)
    return hi, lo


def _mod_kernel(c_ref, w_ref, b_ref, o_ref):
    c = c_ref[...]
    o_ref[0] = _dot(c * _sigmoid(c), w_ref[0]) + b_ref[0]


def _ada_mod(c, ada_w, ada_b):
    depth, d, n6 = ada_w.shape
    bsz = c.shape[0]
    rows = 8
    cp = jnp.zeros((rows, d), F32).at[:bsz].set(c)
    tn = n6 // 4
    out = pl.pallas_call(
        _mod_kernel,
        out_shape=jax.ShapeDtypeStruct((depth, rows, n6), F32),
        grid=(depth, n6 // tn),
        in_specs=[pl.BlockSpec((rows, d), lambda l, j: (0, 0)),
                  pl.BlockSpec((1, d, tn), lambda l, j: (l, 0, j)),
                  pl.BlockSpec((1, 1, tn), lambda l, j: (l, 0, j))],
        out_specs=pl.BlockSpec((1, rows, tn), lambda l, j: (l, 0, j)),
        compiler_params=_cparams("parallel", "parallel"),
        name="ada_mod",
    )(cp, ada_w, ada_b.reshape(depth, 1, n6))
    return out[:, :bsz]


def _inproj_cm_kernel(x_ref, gain_ref, sc_ref, sh_ref, w_ref, *refs):
    cm_refs, rest_ref, zs_sc, zc_sc = refs[:-3], refs[-3], refs[-2], refs[-1]
    h = _rms(x_ref[0], gain_ref[...]) * (1.0 + sc_ref[0]) + sh_ref[0]
    z = _dot(h, w_ref[...])
    nblk, tm, _ = zs_sc.shape
    rest_ref[0] = z[:, nblk * LANES:]
    nq = A_HEADS * A_HEAD_DIM // LANES
    for c in range(nblk):
        blk = z[:, c * LANES:(c + 1) * LANES]
        zs_sc[c] = blk * (A_HEAD_DIM ** -0.5 * LOG2E) if c < nq else blk
    src, sd = zs_sc, 1
    for level, (cm_ref, (_, dil)) in enumerate(zip(cm_refs, DIL_CFG)):
        step, n = dil // sd, tm // dil
        keep = dil > 1 and level + 1 < len(DIL_CFG)
        for rs in range(sd):
            for cc in range(step):
                r = rs + sd * cc
                for c in range(nblk):
                    rows = src[c, pl.ds(rs * (tm // sd) + cc, n, stride=step), :]
                    cm_ref[0, r, :, c * LANES:(c + 1) * LANES] = rows.astype(BF16)
                    if keep:
                        zc_sc[c, r * n:(r + 1) * n, :] = rows
        if keep:
            src, sd = zc_sc, dil


def _inproj_cm(x, gain, sc, sh, w_bf16, na, tm):
    bsz, seq, d = x.shape
    n = w_bf16.shape[1]
    vec = pl.BlockSpec((1, 1, d), lambda b, i: (b, 0, 0))
    dils = [dl for _, dl in DIL_CFG]
    assert dils[0] == 1 and all(b % a == 0 for a, b in zip(dils, dils[1:]))
    return pl.pallas_call(
        _inproj_cm_kernel,
        out_shape=tuple([jax.ShapeDtypeStruct((bsz, dl, seq // dl, na), BF16) for dl in dils]
                        + [jax.ShapeDtypeStruct((bsz, seq, n - na), F32)]),
        grid=(bsz, seq // tm),
        in_specs=[pl.BlockSpec((1, tm, d), lambda b, i: (b, i, 0)),
                  _const_spec((1, d)), vec, vec, _const_spec((d, n))],
        out_specs=tuple([pl.BlockSpec((1, dl, tm // dl, na), lambda b, i: (b, 0, i, 0)) for dl in dils]
                        + [pl.BlockSpec((1, tm, n - na), lambda b, i: (b, i, 0))]),
        scratch_shapes=[pltpu.VMEM((na // LANES, tm, LANES), F32),
                        pltpu.VMEM((na // LANES, tm, LANES), F32)],
        compiler_params=_cparams("parallel", "parallel"),
        name="inproj_cm",
    )(x, gain.reshape(1, d), sc.reshape(bsz, 1, d), sh.reshape(bsz, 1, d), w_bf16)


def _t5_buckets(rel):
    half = N_BUCKETS // 2
    max_exact = half // 2
    n = np.abs(rel)
    large = max_exact + (np.log(np.maximum(n, 1) / max_exact) / np.log(T5_MAX_DIST / max_exact)
                         * (half - max_exact)).astype(np.int32)
    large = np.minimum(large, half - 1)
    return (np.where(rel > 0, half, 0) + np.where(n < max_exact, n, large)).astype(np.int32)


def _dil_bias(t5_bias, window, dil, tq):
    half = (window // 2) // dil
    assert half == tq // 2
    rel = np.arange(2 * tq)[None, :] - half - np.arange(tq)[:, None]
    inside = np.abs(rel) <= half
    buckets = _t5_buckets(np.where(inside, rel, 0) * dil)
    onehot =jnp.asarray(np.eye(N_BUCKETS, dtype=np.float32)[buckets])
    bias = jnp.einsum("qkn,nh->hqk", onehot, t5_bias.astype(F32), precision=lax.Precision.HIGHEST)
    return jnp.where(jnp.asarray(inside)[None], bias * LOG2E, NEG_INF)


def _dil_kernel(q_ref, kp_ref, kc_ref, kn_ref, vp_ref, vc_ref, vn_ref, bias_ref, o_ref, lse_ref,
                *, class_len):
    i = pl.program_id(2)
    sub, hq, tile = DIL_TQ, DIL_TQ // 2, q_ref.shape[2]
    kwin = jnp.concatenate([kp_ref[0, 0], kc_ref[0, 0], kn_ref[0, 0]], axis=0)
    vwin = jnp.concatenate([vp_ref[0, 0], vc_ref[0, 0], vn_ref[0, 0]], axis=0)
    lane = lax.broadcasted_iota(jnp.int32, (1, LANES), 1)
    lo = lane < A_HEAD_DIM
    nblk = A_HEADS // 2
    units = [(jt, j) for jt in range(tile // sub) for j in range(nblk)]
    logits = []
    for jt, j in units:
        cols = slice(j * LANES, (j + 1) * LANES)
        qj = q_ref[0, 0, jt * sub:(jt + 1) * sub, cols]
        zero = jnp.zeros_like(qj)
        q2 = jnp.concatenate([jnp.where(lo, qj, zero), jnp.where(lo, zero, qj)], axis=0)
        s = lax.dot_general(q2, kwin[jt * sub:jt * sub + 2 * sub, cols], (((1,), (1,)), ((), ())),
                            preferred_element_type=F32)
        kpos = i * tile + jt * sub - hq + lax.broadcasted_iota(jnp.int32, (1, 2 * sub), 1)
        valid = jnp.logical_and(kpos >= 0, kpos < class_len)
        logits.append(jnp.where(valid, s + bias_ref[j], NEG_INF))
    s_all = jnp.concatenate(logits, axis=0)
    m = jnp.max(s_all, axis=-1, keepdims=True)
    p32 = jnp.exp2(s_all - m)
    l = jnp.sum(p32, axis=-1, keepdims=True)
    p = p32.astype(BF16)
    rinv = 1.0 / l
    lse = m + jnp.log2(l)
    for jt in range(tile // sub):
        lse_all = jnp.zeros((sub, LANES), F32)
        for j in range(nblk):
            cols = slice(j * LANES, (j + 1) * LANES)
            r0 = (jt * nblk + j) * 2 * sub
            o2 = jnp.dot(p[r0:r0 + 2 * sub], vwin[jt * sub:jt * sub + 2 * sub, cols],
                         preferred_element_type=F32) * rinv[r0:r0 + 2 * sub]
            lse_all = jnp.where(lane == 2 * j, lse[r0:r0 + sub], lse_all)
            lse_all = jnp.where(lane == 2 * j + 1, lse[r0 + sub:r0 + 2 * sub], lse_all)
            o_ref[0, 0, jt * sub:(jt + 1) * sub, cols] = (
                jnp.where(lo, o2[:sub], o2[sub:]).astype(o_ref.dtype))
        lse_ref[0, 0, jt * sub:(jt + 1) * sub, :] = lse_all


def _dilated_branch(qkv_cm, t5_bias, window, dil):
    bsz, _, cl, width = qkv_cm.shape
    aw = A_HEADS * A_HEAD_DIM
    tile, hq = min(DIL_TILE, cl), DIL_TQ // 2
    nt = cl // tile
    per = tile // hq
    nh = cl // hq

    def cur(col):
        return pl.BlockSpec((1, 1, tile, aw), lambda b, r, i: (b, r, i, col))

    def prev(col):
        return pl.BlockSpec((1, 1, hq, aw), lambda b, r, i: (b, r, jnp.maximum(i * per - 1, 0), col))

    def nxt(col):
        return pl.BlockSpec((1, 1, hq, aw),
                            lambda b, r, i: (b, r, jnp.minimum((i + 1) * per, nh - 1), col))

    return pl.pallas_call(
        functools.partial(_dil_kernel, class_len=cl),
        out_shape=(jax.ShapeDtypeStruct((bsz, dil, cl, aw), BF16),
                   jax.ShapeDtypeStruct((bsz, dil, cl, LANES), F32)),
        grid=(bsz, dil, nt),
        in_specs=[cur(0), prev(1), cur(1), nxt(1), prev(2), cur(2), nxt(2),
                  _const_spec((A_HEADS // 2, 2 * DIL_TQ, 2 * DIL_TQ))],
        out_specs=(pl.BlockSpec((1, 1, tile, aw), lambda b, r, i: (b, r, i, 0)),
                   pl.BlockSpec((1, 1, tile, LANES), lambda b, r, i: (b, r, i, 0))),
        compiler_params=_cparams("parallel", "parallel", "parallel"),
        name=f"dilated_d{dil}",
    )(*([qkv_cm] * 7),
      _dil_bias(t5_bias, window, dil, DIL_TQ).reshape(A_HEADS // 2, 2 * DIL_TQ, 2 * DIL_TQ))


def _dilmerge_kernel(*refs):
    nbr = len(DIL_CFG)
    a_refs, l_refs = refs[:nbr], refs[nbr:2 * nbr]
    e_ref, o_ref = refs[2 * nbr], refs[2 * nbr + 1]
    a_scs, l_scs = refs[2 * nbr + 2:3 * nbr + 2], refs[3 * nbr + 2:4 * nbr + 2]
    tm = o_ref.shape[1]
    accs, lses = [], []
    for (_, dil), a_ref, l_ref, a_sc, l_sc in zip(DIL_CFG, a_refs, l_refs, a_scs, l_scs):
        nblk = a_sc.shape[0]
        for r in range(dil):
            rows = pl.ds(r, tm // dil, stride=dil)
            for c in range(nblk):
                a_sc[c, rows, :] = a_ref[0, r, :, c * LANES:(c + 1) * LANES].astype(F32)
            l_sc[rows, :] = l_ref[0, r]
        accs.append(jnp.concatenate([a_sc[c] for c in range(nblk)], axis=1))
        lses.append(l_sc[...])
    mx = functools.reduce(jnp.maximum, lses)
    ws = [jnp.exp2(x - mx) for x in lses]
    tot = functools.reduce(jnp.add, ws)
    out = None
    for w, a in zip(ws, accs):
        w_hi, w_lo = _split2(w / tot)
        wide = (jnp.dot(w_hi, e_ref[...], preferred_element_type=F32)
                + jnp.dot(w_lo, e_ref[...], preferred_element_type=F32))
        out = wide * a if out is None else out + wide * a
    o_ref[0] = out.astype(o_ref.dtype)


def _dilated_merge(branch_outs, tm):
    bsz, _, _, aw = branch_outs[0][0].shape
    seq = branch_outs[0][0].shape[1] * branch_outs[0][0].shape[2]
    expand = np.zeros((LANES, aw), np.float32)
    for h in range(A_HEADS):
        expand[h, h * A_HEAD_DIM:(h + 1) * A_HEAD_DIM] = 1.0
    dils = [d for _, d in DIL_CFG]
    in_specs = ([pl.BlockSpec((1, d, tm // d, aw), lambda b, i: (b, 0, i, 0)) for d in dils]
                + [pl.BlockSpec((1, d, tm // d, LANES), lambda b, i: (b, 0, i, 0)) for d in dils]
                + [_const_spec(expand.shape)])
    return pl.pallas_call(
        _dilmerge_kernel,
        out_shape=jax.ShapeDtypeStruct((bsz, seq, aw), BF16),
        grid=(bsz, seq // tm),
        in_specs=in_specs,
        out_specs=pl.BlockSpec((1, tm, aw), lambda b, i: (b, i, 0)),
        scratch_shapes=([pltpu.VMEM((aw // LANES, tm, LANES), F32) for _ in dils]
                        + [pltpu.VMEM((tm, LANES), F32) for _ in dils]),
        compiler_params=_cparams("parallel", "parallel"),
        name="dilmerge",
    )(*[o for o, _ in branch_outs], *[l for _, l in branch_outs], jnp.asarray(expand, BF16))


def _hgrn_consts(reverse):
    t = HGRN_SUB
    r = np.arange(t)
    u = r[None, :]
    row = r[:, None]
    nmats = [(u >= row) if reverse else (u <= row)]
    masks = []
    m = t // 2
    while m >= 1:
        grp = r // (2 * m)
        in_first = (r % (2 * m)) < m
        same = grp[:, None] == grp[None, :]
        if reverse:
            beta = (grp * 2 * m + m)[:, None]
            n = np.where(in_first[:, None], (u >= row) & (u < beta), (u >= beta) & (u < row))
            mask = same & in_first[:, None] & ~in_first[None, :]
        else:
            beta = (grp * 2 * m + m - 1)[:, None]
            n = np.where(in_first[:, None], (u > row) & (u <= beta), (u > beta) & (u <= row))
            mask = same & ~in_first[:, None] & in_first[None, :]
        if m < HGRN_BCAST_MIN:
            nmats.append(n)
        masks.append(mask)
        m //= 2
    masks.append(np.eye(t, dtype=bool))
    nmat = jnp.asarray(np.concatenate(nmats, axis=0), F32).astype(BF16)
    return nmat, jnp.asarray(np.stack(masks), F32)


def _hgrn_block(q, f, v, lb, st, nmat_ref, mask_ref, reverse):
    t = HGRN_SUB
    nlev = mask_ref.shape[0] - 1
    qs = q * (B_DK ** -0.5)
    fa = lb + (1.0 - lb) * _sigmoid(f)
    kk = 1.0 - fa
    g_hi, g_lo = _split2(jnp.log(fa))
    ex = jnp.dot(nmat_ref[...], jnp.concatenate([g_hi, g_lo], axis=1), preferred_element_type=F32)
    ex = ex[:, :B_DK] + ex[:, B_DK:]
    b = ex[:t]
    btot = b[0:1] if reverse else b[t - 1:t]
    a = mask_ref[nlev] * _dot_nt(qs, kk)
    fine = 1
    for l in range(nlev):
        m = t >> (l + 1)
        if m >= HGRN_BCAST_MIN:
            ref = jnp.concatenate(
                [jnp.broadcast_to(b[beta:beta + 1], (2 * m, B_DK))
                 for beta in range(m if reverse else m - 1, t, 2 * m)], axis=0)
            e = jnp.exp(-jnp.abs(b - ref))
        else:
            e = jnp.exp(ex[fine * t:(fine + 1) * t])
            fine += 1
        a = a + mask_ref[l] * _dot_nt(qs * e, kk * e)
    out = _dot(a, v) + _dot_nt(qs * jnp.exp(b), st)
    khat = (kk * jnp.exp(btot - b)).astype(BF16)
    st_new = st * jnp.exp(btot) + jnp.dot(v.T.astype(BF16), khat, preferred_element_type=F32)
    return out, st_new


def _hgrn_lb(lg_ref, layer):
    lg = [lg_ref[l, 0] for l in range(lg_ref.shape[0])]
    mx = functools.reduce(jnp.maximum, lg)
    e = [jnp.exp(x - mx) for x in lg]
    return functools.reduce(jnp.add, e[:layer + 1]) / functools.reduce(jnp.add, e)


def _hgrn_kernel(qf_ref, ff_ref, vf_ref, qb_ref, fb_ref, vb_ref, lgf_ref, lgb_ref,
                 nf_ref, mf_ref, nb_ref, mb_ref, of_ref, ob_ref, sf_sc, sb_sc, *, layer):
    @pl.when(pl.program_id(2) == 0)
    def _():
        sf_sc[...] = jnp.zeros_like(sf_sc)
        sb_sc[...] = jnp.zeros_like(sb_sc)

    nsub = HGRN_BLOCK // HGRN_SUB
    chains = ((qf_ref, ff_ref, vf_ref, lgf_ref, sf_sc, nf_ref, mf_ref, of_ref, False),
              (qb_ref, fb_ref, vb_ref, lgb_ref, sb_sc, nb_ref, mb_ref, ob_ref, True))
    for q_ref, f_ref, v_ref, lg_ref, st_sc, n_ref, m_ref, o_ref, reverse in chains:
        lb = _hgrn_lb(lg_ref, layer)
        st = st_sc[...]
        for sub in (reversed(range(nsub)) if reverse else range(nsub)):
            rows = pl.ds(sub * HGRN_SUB, HGRN_SUB)
            o, st = _hgrn_block(q_ref[0, rows, :], f_ref[0, rows, :], v_ref[0, rows, :], lb, st,
                                n_ref, m_ref, reverse)
            o_ref[0, rows, :] = o.astype(o_ref.dtype)
        st_sc[...] = st


def _hgrn(z, lb_logits, layer, col0):
    bsz, seq, _ = z.shape
    t = HGRN_BLOCK
    nb = seq // t
    c0 = col0 // LANES
    hw = B_HEADS

    def zspec(group, rev):
        return pl.BlockSpec(
            (1, t, LANES),
            lambda b, h, j: (b, (nb - 1 - j) if rev else j, c0 + group * hw + h))

    def lgspec(direction):
        return pl.BlockSpec((lb_logits.shape[0], 1, 1, LANES),
                            lambda b, h, j: (0, direction * hw + h, 0, 0))

    nf, mf = _hgrn_consts(False)
    nbw, mbw = _hgrn_consts(True)
    lg = lb_logits.astype(F32).reshape(lb_logits.shape[0], 2 * hw, 1, LANES)
    o_shape = jax.ShapeDtypeStruct((bsz, seq, hw * B_DV), BF16)
    return pl.pallas_call(
        functools.partial(_hgrn_kernel, layer=layer),
        out_shape=(o_shape, o_shape),
        grid=(bsz, hw, nb),
        in_specs=[zspec(0, False), zspec(1, False), zspec(3, False),
                  zspec(0, True), zspec(2, True), zspec(3, True),
                  lgspec(0), lgspec(1),
                  _const_spec(nf.shape), _const_spec(mf.shape),
                  _const_spec(nbw.shape), _const_spec(mbw.shape)],
        out_specs=(pl.BlockSpec((1, t, LANES), lambda b, h, j: (b, j, h)),
                   pl.BlockSpec((1, t, LANES), lambda b, h, j: (b, nb - 1 - j, h))),
        scratch_shapes=[pltpu.VMEM((B_DV, B_DK), F32), pltpu.VMEM((B_DV, B_DK), F32)],
        compiler_params=_cparams("parallel", "parallel", "arbitrary"),
        name="hgrn",
    )(z, z, z, z, z, z, lg, lg, nf, mf, nbw, mbw)


def _mix0_y(a_ref, of_ref, ob_ref, g_ref, on_ref, w_ref):
    o = of_ref[0].astype(F32) + ob_ref[0].astype(F32)
    g = g_ref[0]
    parts = [_rms(o[:, h * B_DV:(h + 1) * B_DV], on_ref[...]) for h in range(B_HEADS)]
    bn = jnp.concatenate(parts, axis=-1) * (g * _sigmoid(g))
    na = a_ref.shape[-1]
    return _dot(a_ref[0], w_ref[:na, :]) + _dot(bn, w_ref[na:, :])


def _mix1_y(c_ref, d_ref, w_ref):
    c = jnp.concatenate([c_ref[0, g] for g in range(c_ref.shape[1])], axis=1)
    nc = c.shape[-1]
    return _dot(c, w_ref[:nc, :]) + _dot(d_ref[0], w_ref[nc:, :])


def _row_spec(tm, width, col=0):
    return pl.BlockSpec((1, tm, width), lambda b, i: (b, i, col))


def _tail_kernel(*refs, nmix, mix_fn, bounds):
    (x_ref, gm_ref, gatem_ref, g1_ref, sc_ref, sh_ref, wi_ref, wo_ref, g2_ref, gatef_ref,
     o_ref) = refs[nmix:]
    x1 = x_ref[0] + gatem_ref[0] * _rms(mix_fn(*refs[:nmix]), gm_ref[...])
    h = (_rms(x1, g1_ref[...]) * (1.0 + sc_ref[0]) + sh_ref[0]).astype(BF16)
    hidden = wo_ref.shape[0]
    y = None
    for c0, c1 in zip(bounds, bounds[1:]):
        gt = jnp.dot(h, wi_ref[:, c0:c1], preferred_element_type=F32)
        up = jnp.dot(h, wi_ref[:, hidden + c0:hidden + c1], preferred_element_type=F32)
        part = _dot(gt * _sigmoid(gt) * up, wo_ref[c0:c1, :])
        y = part if y is None else y + part
    o_ref[0] = x1 + gatef_ref[0] * _rms(y, g2_ref[...])


def _layer_tail(mix_fn, mix_args, mix_specs, x, gain_m, gate_m, g1, sc, sh, wi_bf16, wo_bf16,
                g2, gate_f, tm, name):
    bsz, seq, d = x.shape
    vec = pl.BlockSpec((1, 1, d), lambda b, i: (b, 0, 0))
    hidden = wo_bf16.shape[0]
    ntile = hidden // MXU_WIDTH
    assert ntile * MXU_WIDTH == hidden
    bounds = (0, (ntile + 1) // 2 * MXU_WIDTH, hidden)
    row = lambda v: v.reshape(1, d)
    per_batch = lambda v: v.reshape(bsz, 1, d)
    return pl.pallas_call(
        functools.partial(_tail_kernel, nmix=len(mix_args), mix_fn=mix_fn, bounds=bounds),
        out_shape=jax.ShapeDtypeStruct(x.shape, F32),
        grid=(bsz, seq // tm),
        in_specs=list(mix_specs) + [_row_spec(tm, d), _const_spec((1, d)), vec, _const_spec((1, d)),
                                    vec, vec, _const_spec(wi_bf16.shape), _const_spec(wo_bf16.shape),
                                    _const_spec((1, d)), vec],
        out_specs=_row_spec(tm, d),
        compiler_params=_cparams("parallel", "parallel"),
        name=name,
    )(*mix_args, x, row(gain_m), per_batch(gate_m), row(g1), per_batch(sc), per_batch(sh),
      wi_bf16, wo_bf16, row(g2), per_batch(gate_f))


def _tail0(a_out, o_f, o_b, z, g_col, out_norm, w_bf16, tm, **kw):
    wv = B_HEADS * B_DV
    specs = [_row_spec(tm, a_out.shape[-1]), _row_spec(tm, wv), _row_spec(tm, wv),
             _row_spec(tm, wv, g_col // wv), _const_spec((1, B_DV)), _const_spec(w_bf16.shape)]
    return _layer_tail(_mix0_y, (a_out, o_f, o_b, z, out_norm.reshape(1, B_DV), w_bf16), specs,
                       tm=tm, name="tail0", **kw)


def _tail1(c_out, d_out, w_bf16, tm, **kw):
    specs = [pl.BlockSpec((1, c_out.shape[1], tm, c_out.shape[3]), lambda b, i: (b, 0, i, 0)),
             _row_spec(tm, d_out.shape[-1]), _const_spec(w_bf16.shape)]
    return _layer_tail(_mix1_y, (c_out, d_out, w_bf16), specs, tm=tm, name="tail1", **kw)


def _fft_kernel(u_ref, f1_ref, twc_ref, tws_ref, f2_ref, fw_ref, o_ref, u_sc, p_sc, y_sc,
                *, scale, n1, n2):
    pu = n2 + FFT_PAD
    pp = 2 * n1 + FFT_PAD
    py = n1 + FFT_PAD
    f1 = f1_ref[...].astype(BF16)
    f2 = f2_ref[...].astype(BF16)
    fw = fw_ref[...].astype(BF16)
    for i1 in range(n1):
        u_sc[i1 * pu:i1 * pu + n2, :] = u_ref[0, 0, i1 * n2:(i1 + 1) * n2, :]

    nb = FFT_BATCH

    def stage1(blk, carry):
        i2s = [blk * nb + j for j in range(nb)]
        x = jnp.concatenate([u_sc[pl.ds(i2, n1, stride=pu), :] for i2 in i2s], axis=1)
        p = jnp.dot(f1, x.astype(BF16), preferred_element_type=F32)
        for j, i2 in enumerate(i2s):
            p_sc[pl.ds(pl.multiple_of(i2 * pp, 8), 2 * n1), :] = p[:, j * C_WIDTH:(j + 1) * C_WIDTH]
        return carry

    lax.fori_loop(0, n2 // nb, stage1, 0)

    def stage2(blk, carry):
        k1s = [blk * nb + j for j in range(nb)]
        qr, qi = [], []
        for k1 in k1s:
            tc = twc_ref[k1]
            ts = tws_ref[k1]
            pr = p_sc[pl.ds(k1, n2, stride=pp), :]
            pim = p_sc[pl.ds(n1 + k1, n2, stride=pp), :]
            qr.append(pr * tc + pim * ts)
            qi.append(pim * tc - pr * ts)
        q = jnp.concatenate([jnp.concatenate(qr, axis=1), jnp.concatenate(qi, axis=1)], axis=0)
        xx = jnp.dot(f2, q.astype(BF16), preferred_element_type=F32)
        xg = jnp.concatenate(
            [jnp.concatenate([xx[:n2, j * C_WIDTH:(j + 1) * C_WIDTH],
                              xx[n2:, j * C_WIDTH:(j + 1) * C_WIDTH]], axis=1) for j in range(nb)],
            axis=0)
        y = jnp.dot(xg.astype(BF16), fw, preferred_element_type=F32) * scale
        for j, k1 in enumerate(k1s):
            y_sc[pl.ds(k1, n2, stride=py), :] = y[j * n2:(j + 1) * n2]
        return carry

    lax.fori_loop(0, n1 // nb, stage2, 0)
    for k2 in range(n2):
        o_ref[0, 0, k2 * n1:(k2 + 1) * n1, :] = y_sc[k2 * py:k2 * py + n1, :]


def _fourier_mixer(u):
    bsz, ngroups, seq, width = u.shape
    n2 = FFT_N2
    n1 = seq // n2
    assert n1 * n2 == seq and width == C_WIDTH and n1 % 8 == 0
    a1 = 2.0 * np.pi * np.outer(np.arange(n1), np.arange(n1)) / n1
    f1 = np.concatenate([np.cos(a1), -np.sin(a1)], axis=0)
    a2 = 2.0 * np.pi * np.outer(np.arange(n2), np.arange(n2)) / n2
    c2, s2 = np.cos(a2), np.sin(a2)
    f2 = np.block([[c2, s2], [-s2, c2]])
    aw = 2.0 * np.pi * np.outer(np.arange(C_WIDTH), np.arange(C_WIDTH)) / C_WIDTH
    fw = np.concatenate([np.cos(aw), np.sin(aw)], axis=0)
    at = np.repeat((2.0 * np.pi * np.outer(np.arange(n1), np.arange(n2)) / seq)[:, :, None],
                   C_WIDTH, axis=2)
    consts = (jnp.asarray(f1, F32), jnp.asarray(np.cos(at), F32), jnp.asarray(np.sin(at), F32),
              jnp.asarray(f2, F32), jnp.asarray(fw, F32))
    blk = pl.BlockSpec((1, 1, seq, C_WIDTH), lambda b, g: (b, g, 0, 0))
    return pl.pallas_call(
        functools.partial(_fft_kernel, scale=float(1.0 / np.sqrt(seq * C_WIDTH)), n1=n1, n2=n2),
        out_shape=jax.ShapeDtypeStruct(u.shape, F32),
        grid=(bsz, ngroups),
        in_specs=[blk] + [_const_spec(c.shape) for c in consts],
        out_specs=blk,
        scratch_shapes=[pltpu.VMEM((n1 * (n2 + FFT_PAD), C_WIDTH), F32),
                        pltpu.VMEM((n2 * (2 * n1 + FFT_PAD), C_WIDTH), F32),
                        pltpu.VMEM((n2 * (n1 + FFT_PAD), C_WIDTH), F32)],
        compiler_params=_cparams("parallel", "parallel"),
        name="fft",
    )(u, *consts)


def _head_perm():
    rep = D_Q_HEADS // D_KV_HEADS
    cols = []
    for j in range(rep):
        for g in range(D_KV_HEADS):
            h = g * rep + j
            cols.extend(range(h * D_HEAD_DIM, (h + 1) * D_HEAD_DIM))
    return np.asarray(cols, np.int32)


def _rope_tables(seq):
    rows = seq // GRID_W
    row = jnp.repeat(jnp.arange(rows, dtype=F32), GRID_W)
    col = jnp.tile(jnp.arange(GRID_W, dtype=F32), rows)
    axis_dim = D_HEAD_DIM // 2
    inv_freq = jnp.power(ROPE_THETA, -jnp.arange(0, axis_dim, 2, dtype=F32) / axis_dim)
    ang_r = row[:, None] * inv_freq[None, :]
    ang_c = col[:, None] * inv_freq[None, :]
    cr, sr, cc, sc = jnp.cos(ang_r), jnp.sin(ang_r), jnp.cos(ang_c), jnp.sin(ang_c)
    cos = jnp.concatenate([cr, cr, cc, cc], axis=1)
    sin = jnp.concatenate([-sr, sr, -sc, sc], axis=1)
    reps = LANES // D_HEAD_DIM
    return jnp.tile(cos, (1, reps)), jnp.tile(sin, (1, reps))


def _inproj_qk_kernel(x_ref, gain_ref, sc_ref, sh_ref, w_ref, cos_ref, sin_ref, bd_h_ref, bd_l_ref,
                      gq_ref, gk_ref, u_ref, qo_ref, ko_ref, vo_ref):
    h = _rms(x_ref[0], gain_ref[...]) * (1.0 + sc_ref[0]) + sh_ref[0]
    z = _dot(h, w_ref[...])
    ngrp = u_ref.shape[1]
    for g in range(ngrp):
        u_ref[0, g] = z[:, g * LANES:(g + 1) * LANES]
    cos = cos_ref[...]
    sin = sin_ref[...]
    quarter = D_HEAD_DIM // 4
    lane = lax.broadcasted_iota(jnp.int32, (1, LANES), 1)
    first_of_pair = (lane // quarter) % 2 == 0

    def norm_rope(x, gain, scale):
        ms = _dot_tab(bd_h_ref[...], bd_l_ref[...], x * x, tab_left=False)
        xn = x * lax.rsqrt(ms + EPS) * gain
        partner = jnp.where(first_of_pair, pltpu.roll(xn, LANES - quarter, 1),
                            pltpu.roll(xn, quarter, 1))
        return ((xn * cos + partner * sin) * scale).astype(BF16)

    nq = qo_ref.shape[-1] // LANES
    for j in range(nq):
        qo_ref[0, :, j * LANES:(j + 1) * LANES] = norm_rope(
            z[:, (ngrp + j) * LANES:(ngrp + j + 1) * LANES], gq_ref[...], D_HEAD_DIM ** -0.5 * LOG2E)
    ko_ref[0] = norm_rope(z[:, (ngrp + nq) * LANES:(ngrp + nq + 1) * LANES], gk_ref[...], 1.0)
    vt = z[:, (ngrp + nq + 1) * LANES:(ngrp + nq + 2) * LANES].T
    ones = jnp.ones((FLASH_ONES, vt.shape[1]), F32)
    vo_ref[0] = jnp.concatenate(
        [piece for g in range(D_KV_HEADS)
         for piece in (vt[g * D_HEAD_DIM:(g + 1) * D_HEAD_DIM], ones)], axis=0).astype(BF16)


def _inproj_qk(x, gain, sc, sh, w_bf16, qk_norm_j, tm):
    bsz, seq, d = x.shape
    n = w_bf16.shape[1]
    qw = D_Q_HEADS * D_HEAD_DIM
    kw = D_KV_HEADS * D_HEAD_DIM
    assert kw == LANES and n == C_GROUPS * C_WIDTH + qw + 2 * kw
    vrows = D_KV_HEADS * (D_HEAD_DIM + FLASH_ONES)
    cos, sin = _rope_tables(seq)
    bd = np.kron(np.eye(LANES // D_HEAD_DIM), np.full((D_HEAD_DIM, D_HEAD_DIM), 1.0 / D_HEAD_DIM))
    bd_h, bd_l = _np_split2(bd)
    reps = LANES // D_HEAD_DIM
    gq = jnp.tile(qk_norm_j[0].astype(F32), reps).reshape(1, LANES)
    gk = jnp.tile(qk_norm_j[1].astype(F32), reps).reshape(1, LANES)
    tab = pl.BlockSpec((tm, LANES), lambda b, i: (i, 0))
    vec = pl.BlockSpec((1, 1, d), lambda b, i: (b, 0, 0))
    return pl.pallas_call(
        _inproj_qk_kernel,
        out_shape=(jax.ShapeDtypeStruct((bsz, C_GROUPS, seq, C_WIDTH), F32),
                   jax.ShapeDtypeStruct((bsz, seq, qw), BF16),
                   jax.ShapeDtypeStruct((bsz, seq, kw), BF16),
                   jax.ShapeDtypeStruct((bsz, vrows, seq), BF16)),
        grid=(bsz, seq // tm),
        in_specs=[pl.BlockSpec((1, tm, d), lambda b, i: (b, i, 0)),
                  _const_spec((1, d)), vec, vec, _const_spec((d, n)),
                  tab, tab, _const_spec(bd_h.shape), _const_spec(bd_l.shape),
                  _const_spec((1, LANES)), _const_spec((1, LANES))],
        out_specs=(pl.BlockSpec((1, C_GROUPS, tm, C_WIDTH), lambda b, i: (b, 0, i, 0)),
                   _row_spec(tm, qw), _row_spec(tm, kw),
                   pl.BlockSpec((1, vrows, tm), lambda b, i: (b, 0, i))),
        compiler_params=_cparams("parallel", "parallel"),
        name="inproj_qk",
    )(x, gain.reshape(1, d), sc.reshape(bsz, 1, d), sh.reshape(bsz, 1, d), w_bf16,
      cos, sin, bd_h, bd_l, gq, gk)


def _flash_kernel(q_ref, k_ref, vt_ref, o_ref, m_sc, acc_sc, s_sc):
    kv = pl.program_id(2)

    @pl.when(kv == 0)
    def _():
        m_sc[...] = jnp.full_like(m_sc, -jnp.inf)
        acc_sc[...] = jnp.zeros_like(acc_sc)

    lane = lax.broadcasted_iota(jnp.int32, (1, LANES), 1)
    lo = lane < D_HEAD_DIM
    nblk = q_ref.shape[-1] // LANES
    tq, tk = q_ref.shape[1], k_ref.shape[1]
    ku, qu = FLASH_KEY_UNIT, FLASH_QUERY_UNIT
    grows = D_HEAD_DIM + FLASH_ONES
    nheads = nblk * D_KV_HEADS
    k = k_ref[0]

    nchunk = tq // qu

    def logits_chunk(idx, c):
        j, g = divmod(idx, D_KV_HEADS)
        qj = q_ref[0, c * qu:(c + 1) * qu, j * LANES:(j + 1) * LANES]
        sel = lo if g == 0 else jnp.logical_not(lo)
        s = lax.dot_general(k, jnp.where(sel, qj, jnp.zeros_like(qj)), (((1,), (1,)), ((), ())),
                            preferred_element_type=F32)
        s_sc[idx, :, c * qu:(c + 1) * qu] = s
        return jnp.max(s, axis=0, keepdims=True)

    def finish_logits(idx, mcs):
        m_prev = m_sc[idx, 0:1, :]
        m_new = jnp.maximum(m_prev, jnp.concatenate(mcs, axis=1))
        m_sc[idx, 0:1, :] = m_new
        return m_new, jnp.exp2(m_prev - m_new)

    def value_chunk(idx, c, m_new, alpha):
        j, g = divmod(idx, D_KV_HEADS)
        rows = slice(g * grows, (g + 1) * grows)
        qcols = slice(c * qu, (c + 1) * qu)
        pv = None
        for u in range(tk // ku):
            keys = slice(u * ku, (u + 1) * ku)
            p = jnp.exp2(s_sc[idx, keys, qcols] - m_new[:, qcols])
            d = jnp.dot(vt_ref[0, rows, keys], p.astype(BF16), preferred_element_type=F32)
            pv = d if pv is None else pv + d
        acc_sc[j, rows, qcols] = alpha[:, qcols] * acc_sc[j, rows, qcols] + pv

    def logits_pass(idx):
        return finish_logits(idx, [logits_chunk(idx, c) for c in range(nchunk)])

    pending = [logits_pass(i) for i in range(min(FLASH_AHEAD, nheads))]
    for idx in range(nheads):
        if idx + FLASH_AHEAD < nheads:
            pending.append(logits_pass(idx + FLASH_AHEAD))
        stats = pending.pop(0)
        for c in range(nchunk):
            value_chunk(idx, c, *stats)

    @pl.when(kv == pl.num_programs(2) - 1)
    def _():
        for j in range(nblk):
            parts = []
            for g in range(D_KV_HEADS):
                num = acc_sc[j, g * grows:g * grows + D_HEAD_DIM, :]
                den = acc_sc[j, g * grows + D_HEAD_DIM:g * grows + D_HEAD_DIM + 1, :]
                parts.append(num / den)
            o_ref[0, :, j * LANES:(j + 1) * LANES] = (
                jnp.concatenate(parts, axis=0).T.astype(o_ref.dtype))


def _flash(q, k, vt, tq, tk):
    bsz, seq, qw = q.shape
    kw = k.shape[-1]
    vrows = vt.shape[1]
    return pl.pallas_call(
        _flash_kernel,
        out_shape=jax.ShapeDtypeStruct((bsz, seq, qw), BF16),
        grid=(bsz, seq // tq, seq // tk),
        in_specs=[pl.BlockSpec((1, tq, qw), lambda b, i, j: (b, i, 0)),
                  pl.BlockSpec((1, tk, kw), lambda b, i, j: (b, j, 0)),
                  pl.BlockSpec((1, vrows, tk), lambda b, i, j: (b, 0, j))],
        out_specs=pl.BlockSpec((1, tq, qw), lambda b, i, j: (b, i, 0)),
        scratch_shapes=[pltpu.VMEM((D_Q_HEADS, 8, tq), F32),
                        pltpu.VMEM((qw // LANES, vrows, tq), F32),
                        pltpu.VMEM((D_Q_HEADS, tk, tq), F32)],
        compiler_params=_cparams("parallel", "parallel", "arbitrary"),
        name="flash",
    )(q, k, vt)


def kernel(x, c, t5_bias, hgrn_lb_logits, ada_w, ada_b, norm_gains, ab_w_in, ab_w_out,
           hgrn_out_norm, cd_w_in, cd_w_out, qk_norm, ffn_w_in, ffn_w_out):
    bsz, seq, d = x.shape
    depth = ada_w.shape[0]
    mod = _ada_mod(c.astype(F32), ada_w, ada_b)
    perm = _head_perm()
    aw = A_HEADS * A_HEAD_DIM
    cw = C_GROUPS * C_WIDTH
    qw = D_Q_HEADS * D_HEAD_DIM
    tm_in = min(512, seq)
    tm = min(512, seq)
    for layer in range(depth):
        sh_m, sc_m, g_m, sh_f, sc_f, g_f = [mod[layer, :, i * d:(i + 1) * d] for i in range(6)]
        gains = norm_gains[layer]
        j = layer // 2
        tail = dict(x=x, gain_m=gains[1], gate_m=g_m, g1=gains[2], sc=sc_f, sh=sh_f,
                    wi_bf16=ffn_w_in[layer].astype(BF16), wo_bf16=ffn_w_out[layer].astype(BF16),
                    g2=gains[3], gate_f=g_f, tm=tm)
        if layer % 2 == 0:
            w_in = ab_w_in[j].astype(BF16)
            *qkv_cm, z = _inproj_cm(x, gains[0], sc_m, sh_m, w_in, 3 * aw, tm_in)
            branches = [_dilated_branch(cm, t5_bias, window, dil)
                        for cm, (window, dil) in zip(qkv_cm, DIL_CFG)]
            a_out = _dilated_merge(branches, tm)
            o_f, o_b = _hgrn(z, hgrn_lb_logits, layer, 0)
            g_col = 3 * B_HEADS * B_DK + B_HEADS * B_DV
            x = _tail0(a_out, o_f, o_b, z, g_col, hgrn_out_norm[j], ab_w_out[j].astype(BF16), **tail)
        else:
            w_full = cd_w_in[j]
            w_in = jnp.concatenate([w_full[:, :cw], w_full[:, cw:cw + qw][:, perm],
                                    w_full[:, cw + qw:]], axis=1).astype(BF16)
            u, qn, kn, vn = _inproj_qk(x, gains[0], sc_m, sh_m, w_in, qk_norm[j], min(1024, seq))
            c_out = _fourier_mixer(u)
            d_out = _flash(qn, kn, vn, min(FLASH_TQ, seq), min(FLASH_TK, seq))
            w_out_full = cd_w_out[j]
            w_out = jnp.concatenate([w_out_full[:cw], w_out_full[cw:][perm]], axis=0).astype(BF16)
            x = _tail1(c_out, d_out, w_out, **tail)
    return x
```

```python
import functools

import numpy as np
import jax
import jax.numpy as jnp
from jax import lax
from jax.experimental import pallas as pl
from jax.experimental.pallas import tpu as pltpu

F32 = jnp.float32
BF16 = jnp.bfloat16
LANES = 128
MXU_WIDTH = 256
VMEM_LIMIT_BYTES = 56 * 2**20
NEG_INF = -1e30
EPS = 1e-6

GRID_W = 64
A_HEADS = 8
A_HEAD_DIM = 64
DIL_CFG = ((128, 1), (512, 4), (2048, 16))
N_BUCKETS = 32
T5_MAX_DIST = 1024
B_HEADS = 4
B_DK = 128
B_DV = 128
C_GROUPS = 4
C_WIDTH = 128
D_Q_HEADS = 8
D_KV_HEADS = 2
D_HEAD_DIM = 64
ROPE_THETA = 10000.0

DIL_TQ = 128
DIL_TILE = 512
HGRN_BLOCK = 1024
HGRN_SUB = 256
HGRN_BCAST_MIN = 8
FLASH_TQ = 1024
FLASH_TK = 1024
FLASH_KEY_UNIT = 256
FLASH_QUERY_UNIT = 256
FLASH_AHEAD = 1
FLASH_ONES = 16
FFT_N2 = 128
FFT_BATCH = 8
FFT_PAD = 8
LOG2E = 1.4426950408889634


def _cparams(*sem):
    return pltpu.CompilerParams(dimension_semantics=sem, vmem_limit_bytes=VMEM_LIMIT_BYTES)


def _const_spec(shape):
    nd = len(shape)
    return pl.BlockSpec(shape, lambda *_: (0,) * nd, pipeline_mode=pl.Buffered(1))


def _sigmoid(x):
    return 1.0 / (1.0 + jnp.exp(-x))


def _dot(a, b):
    return jnp.dot(a.astype(BF16), b.astype(BF16), preferred_element_type=F32)


def _dot_nt(a, b):
    return lax.dot_general(a.astype(BF16), b.astype(BF16), (((1,), (1,)), ((), ())),
                           preferred_element_type=F32)


def _split2(a):
    hi = a.astype(BF16)
    lo = (a - hi.astype(F32)).astype(BF16)
    return hi, lo


def _split3(a):
    a1 = a.astype(BF16)
    r = a - a1.astype(F32)
    a2 = r.astype(BF16)
    a3 = (r - a2.astype(F32)).astype(BF16)
    return a1, a2, a3


def _dot_tab(tab_hi, tab_lo, x, *, tab_left):
    x_hi, x_lo = _split2(x)
    if tab_left:
        d = lambda t, v: jnp.dot(t, v, preferred_element_type=F32)
    else:
        d = lambda t, v: jnp.dot(v, t, preferred_element_type=F32)
    return d(tab_hi, x_hi) + (d(tab_hi, x_lo) + d(tab_lo, x_hi))


def _rms(x, gain):
    ms = jnp.mean(x * x, axis=-1, keepdims=True)
    return x * lax.rsqrt(ms + EPS) * gain


def _np_split2(t):
    t = np.asarray(t, np.float32)
    hi = jnp.asarray(t, F32).astype(BF16)
    lo = (jnp.asarray(t, F32) - hi.astype(F32)).astype(BF16)
    return hi, lo


def _mod_kernel(c_ref, w_ref, b_ref, o_ref):
    c = c_ref[...]
    o_ref[0] = _dot(c * _sigmoid(c), w_ref[0]) + b_ref[0]


def _ada_mod(c, ada_w, ada_b):
    depth, d, n6 = ada_w.shape
    bsz = c.shape[0]
    rows = 8
    cp = jnp.zeros((rows, d), F32).at[:bsz].set(c)
    tn = n6 // 4
    out = pl.pallas_call(
        _mod_kernel,
        out_shape=jax.ShapeDtypeStruct((depth, rows, n6), F32),
        grid=(depth, n6 // tn),
        in_specs=[pl.BlockSpec((rows, d), lambda l, j: (0, 0)),
                  pl.BlockSpec((1, d, tn), lambda l, j: (l, 0, j)),
                  pl.BlockSpec((1, 1, tn), lambda l, j: (l, 0, j))],
        out_specs=pl.BlockSpec((1, rows, tn), lambda l, j: (l, 0, j)),
        compiler_params=_cparams("parallel", "parallel"),
        name="ada_mod",
    )(cp, ada_w, ada_b.reshape(depth, 1, n6))
    return out[:, :bsz]


def _inproj_cm_kernel(x_ref, gain_ref, sc_ref, sh_ref, w_ref, *refs):
    cm_refs, rest_ref, zs_sc, zc_sc = refs[:-3], refs[-3], refs[-2], refs[-1]
    h = _rms(x_ref[0], gain_ref[...]) * (1.0 + sc_ref[0]) + sh_ref[0]
    z = _dot(h, w_ref[...])
    nblk, tm, _ = zs_sc.shape
    rest_ref[0] = z[:, nblk * LANES:]
    nq = A_HEADS * A_HEAD_DIM // LANES
    for c in range(nblk):
        blk = z[:, c * LANES:(c + 1) * LANES]
        zs_sc[c] = blk * (A_HEAD_DIM ** -0.5 * LOG2E) if c < nq else blk
    src, sd = zs_sc, 1
    for level, (cm_ref, (_, dil)) in enumerate(zip(cm_refs, DIL_CFG)):
        step, n = dil // sd, tm // dil
        keep = dil > 1 and level + 1 < len(DIL_CFG)
        for rs in range(sd):
            for cc in range(step):
                r = rs + sd * cc
                for c in range(nblk):
                    rows = src[c, pl.ds(rs * (tm // sd) + cc, n, stride=step), :]
                    cm_ref[0, r, :, c * LANES:(c + 1) * LANES] = rows.astype(BF16)
                    if keep:
                        zc_sc[c, r * n:(r + 1) * n, :] = rows
        if keep:
            src, sd = zc_sc, dil


def _inproj_cm(x, gain, sc, sh, w_bf16, na, tm):
    bsz, seq, d = x.shape
    n = w_bf16.shape[1]
    vec = pl.BlockSpec((1, 1, d), lambda b, i: (b, 0, 0))
    dils = [dl for _, dl in DIL_CFG]
    assert dils[0] == 1 and all(b % a == 0 for a, b in zip(dils, dils[1:]))
    return pl.pallas_call(
        _inproj_cm_kernel,
        out_shape=tuple([jax.ShapeDtypeStruct((bsz, dl, seq // dl, na), BF16) for dl in dils]
                        + [jax.ShapeDtypeStruct((bsz, seq, n - na), F32)]),
        grid=(bsz, seq // tm),
        in_specs=[pl.BlockSpec((1, tm, d), lambda b, i: (b, i, 0)),
                  _const_spec((1, d)), vec, vec, _const_spec((d, n))],
        out_specs=tuple([pl.BlockSpec((1, dl, tm // dl, na), lambda b, i: (b, 0, i, 0)) for dl in dils]
                        + [pl.BlockSpec((1, tm, n - na), lambda b, i: (b, i, 0))]),
        scratch_shapes=[pltpu.VMEM((na // LANES, tm, LANES), F32),
                        pltpu.VMEM((na // LANES, tm, LANES), F32)],
        compiler_params=_cparams("parallel", "parallel"),
        name="inproj_cm",
    )(x, gain.reshape(1, d), sc.reshape(bsz, 1, d), sh.reshape(bsz, 1, d), w_bf16)


def _t5_buckets(rel):
    half = N_BUCKETS // 2
    max_exact = half // 2
    n = np.abs(rel)
    large = max_exact + (np.log(np.maximum(n, 1) / max_exact) / np.log(T5_MAX_DIST / max_exact)
                         * (half - max_exact)).astype(np.int32)
    large = np.minimum(large, half - 1)
    return (np.where(rel > 0, half, 0) + np.where(n < max_exact, n, large)).astype(np.int32)


def _dil_bias(t5_bias, window, dil, tq):
    half = (window // 2) // dil
    assert half == tq // 2
    rel = np.arange(2 * tq)[None, :] - half - np.arange(tq)[:, None]
    inside = np.abs(rel) <= half
    buckets = _t5_buckets(np.where(inside, rel, 0) * dil)
    onehot =jnp.asarray(np.eye(N_BUCKETS, dtype=np.float32)[buckets])
    bias = jnp.einsum("qkn,nh->hqk", onehot, t5_bias.astype(F32), precision=lax.Precision.HIGHEST)
    return jnp.where(jnp.asarray(inside)[None], bias * LOG2E, NEG_INF)


def _dil_kernel(q_ref, kp_ref, kc_ref, kn_ref, vp_ref, vc_ref, vn_ref, bias_ref, o_ref, lse_ref,
                *, class_len):
    i = pl.program_id(2)
    sub, hq, tile = DIL_TQ, DIL_TQ // 2, q_ref.shape[2]
    kwin = jnp.concatenate([kp_ref[0, 0], kc_ref[0, 0], kn_ref[0, 0]], axis=0)
    vwin = jnp.concatenate([vp_ref[0, 0], vc_ref[0, 0], vn_ref[0, 0]], axis=0)
    lane = lax.broadcasted_iota(jnp.int32, (1, LANES), 1)
    lo = lane < A_HEAD_DIM
    nblk = A_HEADS // 2
    units = [(jt, j) for jt in range(tile // sub) for j in range(nblk)]
    logits = []
    for jt, j in units:
        cols = slice(j * LANES, (j + 1) * LANES)
        qj = q_ref[0, 0, jt * sub:(jt + 1) * sub, cols]
        zero = jnp.zeros_like(qj)
        q2 = jnp.concatenate([jnp.where(lo, qj, zero), jnp.where(lo, zero, qj)], axis=0)
        s = lax.dot_general(q2, kwin[jt * sub:jt * sub + 2 * sub, cols], (((1,), (1,)), ((), ())),
                            preferred_element_type=F32)
        kpos = i * tile + jt * sub - hq + lax.broadcasted_iota(jnp.int32, (1, 2 * sub), 1)
        valid = jnp.logical_and(kpos >= 0, kpos < class_len)
        logits.append(jnp.where(valid, s + bias_ref[j], NEG_INF))
    s_all = jnp.concatenate(logits, axis=0)
    m = jnp.max(s_all, axis=-1, keepdims=True)
    p32 = jnp.exp2(s_all - m)
    l = jnp.sum(p32, axis=-1, keepdims=True)
    p = p32.astype(BF16)
    rinv = 1.0 / l
    lse = m + jnp.log2(l)
    for jt in range(tile // sub):
        lse_all = jnp.zeros((sub, LANES), F32)
        for j in range(nblk):
            cols = slice(j * LANES, (j + 1) * LANES)
            r0 = (jt * nblk + j) * 2 * sub
            o2 = jnp.dot(p[r0:r0 + 2 * sub], vwin[jt * sub:jt * sub + 2 * sub, cols],
                         preferred_element_type=F32) * rinv[r0:r0 + 2 * sub]
            lse_all = jnp.where(lane == 2 * j, lse[r0:r0 + sub], lse_all)
            lse_all = jnp.where(lane == 2 * j + 1, lse[r0 + sub:r0 + 2 * sub], lse_all)
            o_ref[0, 0, jt * sub:(jt + 1) * sub, cols] = (
                jnp.where(lo, o2[:sub], o2[sub:]).astype(o_ref.dtype))
        lse_ref[0, 0, jt * sub:(jt + 1) * sub, :] = lse_all


def _dilated_branch(qkv_cm, t5_bias, window, dil):
    bsz, _, cl, width = qkv_cm.shape
    aw = A_HEADS * A_HEAD_DIM
    tile, hq = min(DIL_TILE, cl), DIL_TQ // 2
    nt = cl // tile
    per = tile // hq
    nh = cl // hq

    def cur(col):
        return pl.BlockSpec((1, 1, tile, aw), lambda b, r, i: (b, r, i, col))

    def prev(col):
        return pl.BlockSpec((1, 1, hq, aw), lambda b, r, i: (b, r, jnp.maximum(i * per - 1, 0), col))

    def nxt(col):
        return pl.BlockSpec((1, 1, hq, aw),
                            lambda b, r, i: (b, r, jnp.minimum((i + 1) * per, nh - 1), col))

    return pl.pallas_call(
        functools.partial(_dil_kernel, class_len=cl),
        out_shape=(jax.ShapeDtypeStruct((bsz, dil, cl, aw), BF16),
                   jax.ShapeDtypeStruct((bsz, dil, cl, LANES), F32)),
        grid=(bsz, dil, nt),
        in_specs=[cur(0), prev(1), cur(1), nxt(1), prev(2), cur(2), nxt(2),
                  _const_spec((A_HEADS // 2, 2 * DIL_TQ, 2 * DIL_TQ))],
        out_specs=(pl.BlockSpec((1, 1, tile, aw), lambda b, r, i: (b, r, i, 0)),
                   pl.BlockSpec((1, 1, tile, LANES), lambda b, r, i: (b, r, i, 0))),
        compiler_params=_cparams("parallel", "parallel", "parallel"),
        name=f"dilated_d{dil}",
    )(*([qkv_cm] * 7),
      _dil_bias(t5_bias, window, dil, DIL_TQ).reshape(A_HEADS // 2, 2 * DIL_TQ, 2 * DIL_TQ))


def _merge_branches(a_refs, l_refs, e_ref, a_scs, l_scs):
    tm = l_scs[0].shape[0]
    accs, lses = [], []
    for (_, dil), a_ref, l_ref, a_sc, l_sc in zip(DIL_CFG, a_refs, l_refs, a_scs, l_scs):
        nblk = a_sc.shape[0]
        for r in range(dil):
            rows = pl.ds(r, tm // dil, stride=dil)
            for c in range(nblk):
                a_sc[c, rows, :] = a_ref[0, r, :, c * LANES:(c + 1) * LANES].astype(F32)
            l_sc[rows, :] = l_ref[0, r]
        accs.append(jnp.concatenate([a_sc[c] for c in range(nblk)], axis=1))
        lses.append(l_sc[...])
    mx = functools.reduce(jnp.maximum, lses)
    ws = [jnp.exp2(x - mx) for x in lses]
    tot = functools.reduce(jnp.add, ws)
    out = None
    for w, a in zip(ws, accs):
        w_hi, w_lo = _split2(w / tot)
        wide = (jnp.dot(w_hi, e_ref[...], preferred_element_type=F32)
                + jnp.dot(w_lo, e_ref[...], preferred_element_type=F32))
        out = wide * a if out is None else out + wide * a
    return out


def _merge_operands(branch_outs, tm):
    aw = branch_outs[0][0].shape[-1]
    expand = np.zeros((LANES, aw), np.float32)
    for h in range(A_HEADS):
        expand[h, h * A_HEAD_DIM:(h + 1) * A_HEAD_DIM] = 1.0
    dils = [d for _, d in DIL_CFG]
    specs = ([pl.BlockSpec((1, d, tm // d, aw), lambda b, i: (b, 0, i, 0)) for d in dils]
             + [pl.BlockSpec((1, d, tm // d, LANES), lambda b, i: (b, 0, i, 0)) for d in dils]
             + [_const_spec(expand.shape)])
    arrays = [o for o, _ in branch_outs] + [l for _, l in branch_outs] + [jnp.asarray(expand, BF16)]
    scratch = ([pltpu.VMEM((aw // LANES, tm, LANES), F32) for _ in dils]
               + [pltpu.VMEM((tm, LANES), F32) for _ in dils])
    return arrays, specs, scratch


def _hgrn_consts(reverse):
    t = HGRN_SUB
    r = np.arange(t)
    u = r[None, :]
    row = r[:, None]
    nmats = [(u >= row) if reverse else (u <= row)]
    masks = []
    m = t // 2
    while m >= 1:
        grp = r // (2 * m)
        in_first = (r % (2 * m)) < m
        same = grp[:, None] == grp[None, :]
        if reverse:
            beta = (grp * 2 * m + m)[:, None]
            n = np.where(in_first[:, None], (u >= row) & (u < beta), (u >= beta) & (u < row))
            mask = same & in_first[:, None] & ~in_first[None, :]
        else:
            beta = (grp * 2 * m + m - 1)[:, None]
            n = np.where(in_first[:, None], (u > row) & (u <= beta), (u > beta) & (u <= row))
            mask = same & ~in_first[:, None] & in_first[None, :]
        if m < HGRN_BCAST_MIN:
            nmats.append(n)
        masks.append(mask)
        m //= 2
    masks.append(np.eye(t, dtype=bool))
    nmat = jnp.asarray(np.concatenate(nmats, axis=0), F32).astype(BF16)
    return nmat, jnp.asarray(np.stack(masks), F32)


def _hgrn_block(q, f, v, lb, st, nmat_ref, mask_ref, reverse):
    t = HGRN_SUB
    nlev = mask_ref.shape[0] - 1
    qs = q * (B_DK ** -0.5)
    fa = lb + (1.0 - lb) * _sigmoid(f)
    kk = 1.0 - fa
    g_hi, g_lo = _split2(jnp.log(fa))
    ex = jnp.dot(nmat_ref[...], jnp.concatenate([g_hi, g_lo], axis=1), preferred_element_type=F32)
    ex = ex[:, :B_DK] + ex[:, B_DK:]
    b = ex[:t]
    btot = b[0:1] if reverse else b[t - 1:t]
    a = mask_ref[nlev] * _dot_nt(qs, kk)
    fine = 1
    for l in range(nlev):
        m = t >> (l + 1)
        if m >= HGRN_BCAST_MIN:
            ref = jnp.concatenate(
                [jnp.broadcast_to(b[beta:beta + 1], (2 * m, B_DK))
                 for beta in range(m if reverse else m - 1, t, 2 * m)], axis=0)
            e = jnp.exp(-jnp.abs(b - ref))
        else:
            e = jnp.exp(ex[fine * t:(fine + 1) * t])
            fine += 1
        a = a + mask_ref[l] * _dot_nt(qs * e, kk * e)
    out = _dot(a, v) + _dot_nt(qs * jnp.exp(b), st)
    khat = (kk * jnp.exp(btot - b)).astype(BF16)
    st_new = st * jnp.exp(btot) + jnp.dot(v.T.astype(BF16), khat, preferred_element_type=F32)
    return out, st_new


def _hgrn_lb(lg_ref, layer):
    lg = [lg_ref[l, 0] for l in range(lg_ref.shape[0])]
    mx = functools.reduce(jnp.maximum, lg)
    e = [jnp.exp(x - mx) for x in lg]
    return functools.reduce(jnp.add, e[:layer + 1]) / functools.reduce(jnp.add, e)


def _hgrn_kernel(qf_ref, ff_ref, vf_ref, qb_ref, fb_ref, vb_ref, lgf_ref, lgb_ref,
                 nf_ref, mf_ref, nb_ref, mb_ref, of_ref, ob_ref, sf_sc, sb_sc, *, layer):
    @pl.when(pl.program_id(2) == 0)
    def _():
        sf_sc[...] = jnp.zeros_like(sf_sc)
        sb_sc[...] = jnp.zeros_like(sb_sc)

    nsub = HGRN_BLOCK // HGRN_SUB
    chains = ((qf_ref, ff_ref, vf_ref, lgf_ref, sf_sc, nf_ref, mf_ref, of_ref, False),
              (qb_ref, fb_ref, vb_ref, lgb_ref, sb_sc, nb_ref, mb_ref, ob_ref, True))
    for q_ref, f_ref, v_ref, lg_ref, st_sc, n_ref, m_ref, o_ref, reverse in chains:
        lb = _hgrn_lb(lg_ref, layer)
        st = st_sc[...]
        for sub in (reversed(range(nsub)) if reverse else range(nsub)):
            rows = pl.ds(sub * HGRN_SUB, HGRN_SUB)
            o, st = _hgrn_block(q_ref[0, rows, :], f_ref[0, rows, :], v_ref[0, rows, :], lb, st,
                                n_ref, m_ref, reverse)
            o_ref[0, rows, :] = o.astype(o_ref.dtype)
        st_sc[...] = st


def _hgrn(z, lb_logits, layer, col0):
    bsz, seq, _ = z.shape
    t = HGRN_BLOCK
    nb = seq // t
    c0 = col0 // LANES
    hw = B_HEADS

    def zspec(group, rev):
        return pl.BlockSpec(
            (1, t, LANES),
            lambda b, h, j: (b, (nb - 1 - j) if rev else j, c0 + group * hw + h))

    def lgspec(direction):
        return pl.BlockSpec((lb_logits.shape[0], 1, 1, LANES),
                            lambda b, h, j: (0, direction * hw + h, 0, 0))

    nf, mf = _hgrn_consts(False)
    nbw, mbw = _hgrn_consts(True)
    lg = lb_logits.astype(F32).reshape(lb_logits.shape[0], 2 * hw, 1, LANES)
    o_shape = jax.ShapeDtypeStruct((bsz, seq, hw * B_DV), BF16)
    return pl.pallas_call(
        functools.partial(_hgrn_kernel, layer=layer),
        out_shape=(o_shape, o_shape),
        grid=(bsz, hw, nb),
        in_specs=[zspec(0, False), zspec(1, False), zspec(3, False),
                  zspec(0, True), zspec(2, True), zspec(3, True),
                  lgspec(0), lgspec(1),
                  _const_spec(nf.shape), _const_spec(mf.shape),
                  _const_spec(nbw.shape), _const_spec(mbw.shape)],
        out_specs=(pl.BlockSpec((1, t, LANES), lambda b, h, j: (b, j, h)),
                   pl.BlockSpec((1, t, LANES), lambda b, h, j: (b, nb - 1 - j, h))),
        scratch_shapes=[pltpu.VMEM((B_DV, B_DK), F32), pltpu.VMEM((B_DV, B_DK), F32)],
        compiler_params=_cparams("parallel", "parallel", "arbitrary"),
        name="hgrn",
    )(z, z, z, z, z, z, lg, lg, nf, mf, nbw, mbw)


def _mix0_y(*refs):
    nbr = len(DIL_CFG)
    a_refs, l_refs, e_ref = refs[:nbr], refs[nbr:2 * nbr], refs[2 * nbr]
    of_ref, ob_ref, g_ref, on_ref, w_ref = refs[2 * nbr + 1:2 * nbr + 6]
    scratch = refs[2 * nbr + 6:]
    a = _merge_branches(a_refs, l_refs, e_ref, scratch[:nbr], scratch[nbr:])
    o = of_ref[0].astype(F32) + ob_ref[0].astype(F32)
    g = g_ref[0]
    parts = [_rms(o[:, h * B_DV:(h + 1) * B_DV], on_ref[...]) for h in range(B_HEADS)]
    bn = jnp.concatenate(parts, axis=-1) * (g * _sigmoid(g))
    na = a.shape[-1]
    return _dot(a, w_ref[:na, :]) + _dot(bn, w_ref[na:, :])


def _mix1_y(c_ref, d_ref, w_ref):
    c = jnp.concatenate([c_ref[0, g] for g in range(c_ref.shape[1])], axis=1)
    nc = c.shape[-1]
    return _dot(c, w_ref[:nc, :]) + _dot(d_ref[0], w_ref[nc:, :])


def _row_spec(tm, width, col=0):
    return pl.BlockSpec((1, tm, width), lambda b, i: (b, i, col))


def _tail_kernel(*refs, nmix, mix_fn, bounds):
    (x_ref, gm_ref, gatem_ref, g1_ref, sc_ref, sh_ref, wi_ref, wo_ref, g2_ref, gatef_ref,
     o_ref) = refs[nmix:nmix + 11]
    y_mix = mix_fn(*refs[:nmix], *refs[nmix + 11:])
    x1 = x_ref[0] + gatem_ref[0] * _rms(y_mix, gm_ref[...])
    h = (_rms(x1, g1_ref[...]) * (1.0 + sc_ref[0]) + sh_ref[0]).astype(BF16)
    hidden = wo_ref.shape[0]
    y = None
    for c0, c1 in zip(bounds, bounds[1:]):
        gt = jnp.dot(h, wi_ref[:, c0:c1], preferred_element_type=F32)
        up = jnp.dot(h, wi_ref[:, hidden + c0:hidden + c1], preferred_element_type=F32)
        part = _dot(gt * _sigmoid(gt) * up, wo_ref[c0:c1, :])
        y = part if y is None else y + part
    o_ref[0] = x1 + gatef_ref[0] * _rms(y, g2_ref[...])


def _layer_tail(mix_fn, mix_args, mix_specs, x, gain_m, gate_m, g1, sc, sh, wi_bf16, wo_bf16,
                g2, gate_f, tm, name, mix_scratch=()):
    bsz, seq, d = x.shape
    vec = pl.BlockSpec((1, 1, d), lambda b, i: (b, 0, 0))
    hidden = wo_bf16.shape[0]
    ntile = hidden // MXU_WIDTH
    assert ntile * MXU_WIDTH == hidden
    bounds = (0, (ntile + 1) // 2 * MXU_WIDTH, hidden)
    row = lambda v: v.reshape(1, d)
    per_batch = lambda v: v.reshape(bsz, 1, d)
    return pl.pallas_call(
        functools.partial(_tail_kernel, nmix=len(mix_args), mix_fn=mix_fn, bounds=bounds),
        out_shape=jax.ShapeDtypeStruct(x.shape, F32),
        grid=(bsz, seq // tm),
        in_specs=list(mix_specs) + [_row_spec(tm, d), _const_spec((1, d)), vec, _const_spec((1, d)),
                                    vec, vec, _const_spec(wi_bf16.shape), _const_spec(wo_bf16.shape),
                                    _const_spec((1, d)), vec],
        out_specs=_row_spec(tm, d),
        scratch_shapes=list(mix_scratch),
        compiler_params=_cparams("parallel", "parallel"),
        name=name,
    )(*mix_args, x, row(gain_m), per_batch(gate_m), row(g1), per_batch(sc), per_batch(sh),
      wi_bf16, wo_bf16, row(g2), per_batch(gate_f))


def _tail0(branches, o_f, o_b, z, g_col, out_norm, w_bf16, tm, **kw):
    wv = B_HEADS * B_DV
    m_arrays, m_specs, m_scratch = _merge_operands(branches, tm)
    specs = m_specs + [_row_spec(tm, wv), _row_spec(tm, wv), _row_spec(tm, wv, g_col // wv),
                       _const_spec((1, B_DV)), _const_spec(w_bf16.shape)]
    args = m_arrays + [o_f, o_b, z, out_norm.reshape(1, B_DV), w_bf16]
    return _layer_tail(_mix0_y, args, specs, tm=tm, name="tail0", mix_scratch=m_scratch, **kw)


def _tail1(c_out, d_out, w_bf16, tm, **kw):
    specs = [pl.BlockSpec((1, c_out.shape[1], tm, c_out.shape[3]), lambda b, i: (b, 0, i, 0)),
             _row_spec(tm, d_out.shape[-1]), _const_spec(w_bf16.shape)]
    return _layer_tail(_mix1_y, (c_out, d_out, w_bf16), specs, tm=tm, name="tail1", **kw)


def _fft_kernel(u_ref, f1_ref, twc_ref, tws_ref, f2_ref, fw_ref, o_ref, u_sc, p_sc, y_sc,
                *, scale, n1, n2):
    pu = n2 + FFT_PAD
    pp = 2 * n1 + FFT_PAD
    py = n1 + FFT_PAD
    f1 = f1_ref[...].astype(BF16)
    f2 = f2_ref[...].astype(BF16)
    fw = fw_ref[...].astype(BF16)
    for i1 in range(n1):
        u_sc[i1 * pu:i1 * pu + n2, :] = u_ref[0, 0, i1 * n2:(i1 + 1) * n2, :]

    nb = FFT_BATCH

    def stage1(blk, carry):
        i2s = [blk * nb + j for j in range(nb)]
        x = jnp.concatenate([u_sc[pl.ds(i2, n1, stride=pu), :] for i2 in i2s], axis=1)
        p = jnp.dot(f1, x.astype(BF16), preferred_element_type=F32)
        for j, i2 in enumerate(i2s):
            p_sc[pl.ds(pl.multiple_of(i2 * pp, 8), 2 * n1), :] = p[:, j * C_WIDTH:(j + 1) * C_WIDTH]
        return carry

    lax.fori_loop(0, n2 // nb, stage1, 0)

    def stage2(blk, carry):
        k1s = [blk * nb + j for j in range(nb)]
        qr, qi = [], []
        for k1 in k1s:
            tc = twc_ref[k1]
            ts = tws_ref[k1]
            pr = p_sc[pl.ds(k1, n2, stride=pp), :]
            pim = p_sc[pl.ds(n1 + k1, n2, stride=pp), :]
            qr.append(pr * tc + pim * ts)
            qi.append(pim * tc - pr * ts)
        q = jnp.concatenate([jnp.concatenate(qr, axis=1), jnp.concatenate(qi, axis=1)], axis=0)
        xx = jnp.dot(f2, q.astype(BF16), preferred_element_type=F32)
        xg = jnp.concatenate(
            [jnp.concatenate([xx[:n2, j * C_WIDTH:(j + 1) * C_WIDTH],
                              xx[n2:, j * C_WIDTH:(j + 1) * C_WIDTH]], axis=1) for j in range(nb)],
            axis=0)
        y = jnp.dot(xg.astype(BF16), fw, preferred_element_type=F32) * scale
        for j, k1 in enumerate(k1s):
            y_sc[pl.ds(k1, n2, stride=py), :] = y[j * n2:(j + 1) * n2]
        return carry

    lax.fori_loop(0, n1 // nb, stage2, 0)
    for k2 in range(n2):
        o_ref[0, 0, k2 * n1:(k2 + 1) * n1, :] = y_sc[k2 * py:k2 * py + n1, :]


def _fourier_mixer(u):
    bsz, ngroups, seq, width = u.shape
    n2 = FFT_N2
    n1 = seq // n2
    assert n1 * n2 == seq and width == C_WIDTH and n1 % 8 == 0
    a1 = 2.0 * np.pi * np.outer(np.arange(n1), np.arange(n1)) / n1
    f1 = np.concatenate([np.cos(a1), -np.sin(a1)], axis=0)
    a2 = 2.0 * np.pi * np.outer(np.arange(n2), np.arange(n2)) / n2
    c2, s2 = np.cos(a2), np.sin(a2)
    f2 = np.block([[c2, s2], [-s2, c2]])
    aw = 2.0 * np.pi * np.outer(np.arange(C_WIDTH), np.arange(C_WIDTH)) / C_WIDTH
    fw = np.concatenate([np.cos(aw), np.sin(aw)], axis=0)
    at = np.repeat((2.0 * np.pi * np.outer(np.arange(n1), np.arange(n2)) / seq)[:, :, None],
                   C_WIDTH, axis=2)
    consts = (jnp.asarray(f1, F32), jnp.asarray(np.cos(at), F32), jnp.asarray(np.sin(at), F32),
              jnp.asarray(f2, F32), jnp.asarray(fw, F32))
    blk = pl.BlockSpec((1, 1, seq, C_WIDTH), lambda b, g: (b, g, 0, 0))
    return pl.pallas_call(
        functools.partial(_fft_kernel, scale=float(1.0 / np.sqrt(seq * C_WIDTH)), n1=n1, n2=n2),
        out_shape=jax.ShapeDtypeStruct(u.shape, F32),
        grid=(bsz, ngroups),
        in_specs=[blk] + [_const_spec(c.shape) for c in consts],
        out_specs=blk,
        scratch_shapes=[pltpu.VMEM((n1 * (n2 + FFT_PAD), C_WIDTH), F32),
                        pltpu.VMEM((n2 * (2 * n1 + FFT_PAD), C_WIDTH), F32),
                        pltpu.VMEM((n2 * (n1 + FFT_PAD), C_WIDTH), F32)],
        compiler_params=_cparams("parallel", "parallel"),
        name="fft",
    )(u, *consts)


def _head_perm():
    rep = D_Q_HEADS // D_KV_HEADS
    cols = []
    for j in range(rep):
        for g in range(D_KV_HEADS):
            h = g * rep + j
            cols.extend(range(h * D_HEAD_DIM, (h + 1) * D_HEAD_DIM))
    return np.asarray(cols, np.int32)


def _rope_tables(seq):
    rows = seq // GRID_W
    row = jnp.repeat(jnp.arange(rows, dtype=F32), GRID_W)
    col = jnp.tile(jnp.arange(GRID_W, dtype=F32), rows)
    axis_dim = D_HEAD_DIM // 2
    inv_freq = jnp.power(ROPE_THETA, -jnp.arange(0, axis_dim, 2, dtype=F32) / axis_dim)
    ang_r = row[:, None] * inv_freq[None, :]
    ang_c = col[:, None] * inv_freq[None, :]
    cr, sr, cc, sc = jnp.cos(ang_r), jnp.sin(ang_r), jnp.cos(ang_c), jnp.sin(ang_c)
    cos = jnp.concatenate([cr, cr, cc, cc], axis=1)
    sin = jnp.concatenate([-sr, sr, -sc, sc], axis=1)
    reps = LANES // D_HEAD_DIM
    return jnp.tile(cos, (1, reps)), jnp.tile(sin, (1, reps))


def _inproj_qk_kernel(x_ref, gain_ref, sc_ref, sh_ref, w_ref, cos_ref, sin_ref, bd_h_ref, bd_l_ref,
                      gq_ref, gk_ref, u_ref, qo_ref, ko_ref, vo_ref):
    h = _rms(x_ref[0], gain_ref[...]) * (1.0 + sc_ref[0]) + sh_ref[0]
    z = _dot(h, w_ref[...])
    ngrp = u_ref.shape[1]
    for g in range(ngrp):
        u_ref[0, g] = z[:, g * LANES:(g + 1) * LANES]
    cos = cos_ref[...]
    sin = sin_ref[...]
    quarter = D_HEAD_DIM // 4
    lane = lax.broadcasted_iota(jnp.int32, (1, LANES), 1)
    first_of_pair = (lane // quarter) % 2 == 0

    def norm_rope(x, gain, scale):
        ms = _dot_tab(bd_h_ref[...], bd_l_ref[...], x * x, tab_left=False)
        xn = x * lax.rsqrt(ms + EPS) * gain
        partner = jnp.where(first_of_pair, pltpu.roll(xn, LANES - quarter, 1),
                            pltpu.roll(xn, quarter, 1))
        return ((xn * cos + partner * sin) * scale).astype(BF16)

    nq = qo_ref.shape[-1] // LANES
    for j in range(nq):
        qo_ref[0, :, j * LANES:(j + 1) * LANES] = norm_rope(
            z[:, (ngrp + j) * LANES:(ngrp + j + 1) * LANES], gq_ref[...], D_HEAD_DIM ** -0.5 * LOG2E)
    ko_ref[0] = norm_rope(z[:, (ngrp + nq) * LANES:(ngrp + nq + 1) * LANES], gk_ref[...], 1.0)
    vt = z[:, (ngrp + nq + 1) * LANES:(ngrp + nq + 2) * LANES].T
    ones = jnp.ones((FLASH_ONES, vt.shape[1]), F32)
    vo_ref[0] = jnp.concatenate(
        [piece for g in range(D_KV_HEADS)
         for piece in (vt[g * D_HEAD_DIM:(g + 1) * D_HEAD_DIM], ones)], axis=0).astype(BF16)


def _inproj_qk(x, gain, sc, sh, w_bf16, qk_norm_j, tm):
    bsz, seq, d = x.shape
    n = w_bf16.shape[1]
    qw = D_Q_HEADS * D_HEAD_DIM
    kw = D_KV_HEADS * D_HEAD_DIM
    assert kw == LANES and n == C_GROUPS * C_WIDTH + qw + 2 * kw
    vrows = D_KV_HEADS * (D_HEAD_DIM + FLASH_ONES)
    cos, sin = _rope_tables(seq)
    bd = np.kron(np.eye(LANES // D_HEAD_DIM), np.full((D_HEAD_DIM, D_HEAD_DIM), 1.0 / D_HEAD_DIM))
    bd_h, bd_l = _np_split2(bd)
    reps = LANES // D_HEAD_DIM
    gq = jnp.tile(qk_norm_j[0].astype(F32), reps).reshape(1, LANES)
    gk = jnp.tile(qk_norm_j[1].astype(F32), reps).reshape(1, LANES)
    tab = pl.BlockSpec((tm, LANES), lambda b, i: (i, 0))
    vec = pl.BlockSpec((1, 1, d), lambda b, i: (b, 0, 0))
    return pl.pallas_call(
        _inproj_qk_kernel,
        out_shape=(jax.ShapeDtypeStruct((bsz, C_GROUPS, seq, C_WIDTH), F32),
                   jax.ShapeDtypeStruct((bsz, seq, qw), BF16),
                   jax.ShapeDtypeStruct((bsz, seq, kw), BF16),
                   jax.ShapeDtypeStruct((bsz, vrows, seq), BF16)),
        grid=(bsz, seq // tm),
        in_specs=[pl.BlockSpec((1, tm, d), lambda b, i: (b, i, 0)),
                  _const_spec((1, d)), vec, vec, _const_spec((d, n)),
                  tab, tab, _const_spec(bd_h.shape), _const_spec(bd_l.shape),
                  _const_spec((1, LANES)), _const_spec((1, LANES))],
        out_specs=(pl.BlockSpec((1, C_GROUPS, tm, C_WIDTH), lambda b, i: (b, 0, i, 0)),
                   _row_spec(tm, qw), _row_spec(tm, kw),
                   pl.BlockSpec((1, vrows, tm), lambda b, i: (b, 0, i))),
        compiler_params=_cparams("parallel", "parallel"),
        name="inproj_qk",
    )(x, gain.reshape(1, d), sc.reshape(bsz, 1, d), sh.reshape(bsz, 1, d), w_bf16,
      cos, sin, bd_h, bd_l, gq, gk)


def _flash_kernel(q_ref, k_ref, vt_ref, o_ref, m_sc, acc_sc, s_sc):
    kv = pl.program_id(2)

    @pl.when(kv == 0)
    def _():
        m_sc[...] = jnp.full_like(m_sc, -jnp.inf)
        acc_sc[...] = jnp.zeros_like(acc_sc)

    lane = lax.broadcasted_iota(jnp.int32, (1, LANES), 1)
    lo = lane < D_HEAD_DIM
    nblk = q_ref.shape[-1] // LANES
    tq, tk = q_ref.shape[1], k_ref.shape[1]
    ku, qu = FLASH_KEY_UNIT, FLASH_QUERY_UNIT
    grows = D_HEAD_DIM + FLASH_ONES
    nheads = nblk * D_KV_HEADS
    k = k_ref[0]

    nchunk = tq // qu

    def logits_chunk(idx, c):
        j, g = divmod(idx, D_KV_HEADS)
        qj = q_ref[0, c * qu:(c + 1) * qu, j * LANES:(j + 1) * LANES]
        sel = lo if g == 0 else jnp.logical_not(lo)
        s = lax.dot_general(k, jnp.where(sel, qj, jnp.zeros_like(qj)), (((1,), (1,)), ((), ())),
                            preferred_element_type=F32)
        s_sc[idx, :, c * qu:(c + 1) * qu] = s
        return jnp.max(s, axis=0, keepdims=True)

    def finish_logits(idx, mcs):
        m_prev = m_sc[idx, 0:1, :]
        m_new = jnp.maximum(m_prev, jnp.concatenate(mcs, axis=1))
        m_sc[idx, 0:1, :] = m_new
        return m_new, jnp.exp2(m_prev - m_new)

    def value_chunk(idx, c, m_new, alpha):
        j, g = divmod(idx, D_KV_HEADS)
        rows = slice(g * grows, (g + 1) * grows)
        qcols = slice(c * qu, (c + 1) * qu)
        pv = None
        for u in range(tk // ku):
            keys = slice(u * ku, (u + 1) * ku)
            p = jnp.exp2(s_sc[idx, keys, qcols] - m_new[:, qcols])
            d = jnp.dot(vt_ref[0, rows, keys], p.astype(BF16), preferred_element_type=F32)
            pv = d if pv is None else pv + d
        acc_sc[j, rows, qcols] = alpha[:, qcols] * acc_sc[j, rows, qcols] + pv

    def logits_pass(idx):
        return finish_logits(idx, [logits_chunk(idx, c) for c in range(nchunk)])

    pending = [logits_pass(i) for i in range(min(FLASH_AHEAD, nheads))]
    for idx in range(nheads):
        if idx + FLASH_AHEAD < nheads:
            pending.append(logits_pass(idx + FLASH_AHEAD))
        stats = pending.pop(0)
        for c in range(nchunk):
            value_chunk(idx, c, *stats)

    @pl.when(kv == pl.num_programs(2) - 1)
    def _():
        for j in range(nblk):
            parts = []
            for g in range(D_KV_HEADS):
                num = acc_sc[j, g * grows:g * grows + D_HEAD_DIM, :]
                den = acc_sc[j, g * grows + D_HEAD_DIM:g * grows + D_HEAD_DIM + 1, :]
                parts.append(num / den)
            o_ref[0, :, j * LANES:(j + 1) * LANES] = (
                jnp.concatenate(parts, axis=0).T.astype(o_ref.dtype))


def _flash(q, k, vt, tq, tk):
    bsz, seq, qw = q.shape
    kw = k.shape[-1]
    vrows = vt.shape[1]
    return pl.pallas_call(
        _flash_kernel,
        out_shape=jax.ShapeDtypeStruct((bsz, seq, qw), BF16),
        grid=(bsz, seq // tq, seq // tk),
        in_specs=[pl.BlockSpec((1, tq, qw), lambda b, i, j: (b, i, 0)),
                  pl.BlockSpec((1, tk, kw), lambda b, i, j: (b, j, 0)),
                  pl.BlockSpec((1, vrows, tk), lambda b, i, j: (b, 0, j))],
        out_specs=pl.BlockSpec((1, tq, qw), lambda b, i, j: (b, i, 0)),
        scratch_shapes=[pltpu.VMEM((D_Q_HEADS, 8, tq), F32),
                        pltpu.VMEM((qw // LANES, vrows, tq), F32),
                        pltpu.VMEM((D_Q_HEADS, tk, tq), F32)],
        compiler_params=_cparams("parallel", "parallel", "arbitrary"),
        name="flash",
    )(q, k, vt)


def kernel(x, c, t5_bias, hgrn_lb_logits, ada_w, ada_b, norm_gains, ab_w_in, ab_w_out,
           hgrn_out_norm, cd_w_in, cd_w_out, qk_norm, ffn_w_in, ffn_w_out):
    bsz, seq, d = x.shape
    depth = ada_w.shape[0]
    mod = _ada_mod(c.astype(F32), ada_w, ada_b)
    perm = _head_perm()
    aw = A_HEADS * A_HEAD_DIM
    cw = C_GROUPS * C_WIDTH
    qw = D_Q_HEADS * D_HEAD_DIM
    tm_in = min(512, seq)
    tm = min(512, seq)
    for layer in range(depth):
        sh_m, sc_m, g_m, sh_f, sc_f, g_f = [mod[layer, :, i * d:(i + 1) * d] for i in range(6)]
        gains = norm_gains[layer]
        j = layer // 2
        tail = dict(x=x, gain_m=gains[1], gate_m=g_m, g1=gains[2], sc=sc_f, sh=sh_f,
                    wi_bf16=ffn_w_in[layer].astype(BF16), wo_bf16=ffn_w_out[layer].astype(BF16),
                    g2=gains[3], gate_f=g_f, tm=tm)
        if layer % 2 == 0:
            w_in = ab_w_in[j].astype(BF16)
            *qkv_cm, z = _inproj_cm(x, gains[0], sc_m, sh_m, w_in, 3 * aw, tm_in)
            branches = [_dilated_branch(cm, t5_bias, window, dil)
                        for cm, (window, dil) in zip(qkv_cm, DIL_CFG)]
            o_f, o_b = _hgrn(z, hgrn_lb_logits, layer, 0)
            g_col = 3 * B_HEADS * B_DK + B_HEADS * B_DV
            x = _tail0(branches, o_f, o_b, z, g_col, hgrn_out_norm[j], ab_w_out[j].astype(BF16), **tail)
        else:
            w_full = cd_w_in[j]
            w_in = jnp.concatenate([w_full[:, :cw], w_full[:, cw:cw + qw][:, perm],
                                    w_full[:, cw + qw:]], axis=1).astype(BF16)
            u, qn, kn, vn = _inproj_qk(x, gains[0], sc_m, sh_m, w_in, qk_norm[j], min(1024, seq))
            c_out = _fourier_mixer(u)
            d_out = _flash(qn, kn, vn, min(FLASH_TQ, seq), min(FLASH_TK, seq))
            w_out_full = cd_w_out[j]
            w_out = jnp.concatenate([w_out_full[:cw], w_out_full[cw:][perm]], axis=0).astype(BF16)
            x = _tail1(c_out, d_out, w_out, **tail)
    return x
```

```python
import functools

import numpy as np
import jax
import jax.numpy as jnp
from jax import lax
from jax.experimental import pallas as pl
from jax.experimental.pallas import tpu as pltpu

F32 = jnp.float32
BF16 = jnp.bfloat16
LANES = 128
MXU_WIDTH = 256
VMEM_LIMIT_BYTES = 56 * 2**20
NEG_INF = -1e30
EPS = 1e-6

GRID_W = 64
A_HEADS = 8
A_HEAD_DIM = 64
DIL_CFG = ((128, 1), (512, 4), (2048, 16))
N_BUCKETS = 32
T5_MAX_DIST = 1024
B_HEADS = 4
B_DK = 128
B_DV = 128
C_GROUPS = 4
C_WIDTH = 128
D_Q_HEADS = 8
D_KV_HEADS = 2
D_HEAD_DIM = 64
ROPE_THETA = 10000.0

DIL_TQ = 128
DIL_TILE = 512
HGRN_BLOCK = 1024
HGRN_SUB = 256
HGRN_BCAST_MIN = 8
FLASH_TQ = 1024
FLASH_TK = 1024
FLASH_KEY_UNIT = 256
FLASH_QUERY_UNIT = 256
FLASH_AHEAD = 1
FLASH_ONES = 16
FFT_N2 = 128
FFT_BATCH = 8
FFT_PAD = 8
LOG2E = 1.4426950408889634


def _cparams(*sem):
    return pltpu.CompilerParams(dimension_semantics=sem, vmem_limit_bytes=VMEM_LIMIT_BYTES)


def _const_spec(shape):
    nd = len(shape)
    return pl.BlockSpec(shape, lambda *_: (0,) * nd, pipeline_mode=pl.Buffered(1))


def _sigmoid(x):
    return 1.0 / (1.0 + jnp.exp(-x))


def _dot(a, b):
    return jnp.dot(a.astype(BF16), b.astype(BF16), preferred_element_type=F32)


def _dot_nt(a, b):
    return lax.dot_general(a.astype(BF16), b.astype(BF16), (((1,), (1,)), ((), ())),
                           preferred_element_type=F32)


def _split2(a):
    hi = a.astype(BF16)
    lo = (a - hi.astype(F32)).astype(BF16)
    return hi, lo


def _split3(a):
    a1 = a.astype(BF16)
    r = a - a1.astype(F32)
    a2 = r.astype(BF16)
    a3 = (r - a2.astype(F32)).astype(BF16)
    return a1, a2, a3


def _dot_tab(tab_hi, tab_lo, x, *, tab_left):
    x_hi, x_lo = _split2(x)
    if tab_left:
        d = lambda t, v: jnp.dot(t, v, preferred_element_type=F32)
    else:
        d = lambda t, v: jnp.dot(v, t, preferred_element_type=F32)
    return d(tab_hi, x_hi) + (d(tab_hi, x_lo) + d(tab_lo, x_hi))


def _rms(x, gain):
    ms = jnp.mean(x * x, axis=-1, keepdims=True)
    return x * lax.rsqrt(ms + EPS) * gain


def _np_split2(t):
    t = np.asarray(t, np.float32)
    hi = jnp.asarray(t, F32).astype(BF16)
    lo = (jnp.asarray(t, F32) - hi.astype(F32)).astype(BF16)
    return hi, lo


def _mod_kernel(c_ref, w_ref, b_ref, o_ref):
    c = c_ref[...]
    o_ref[0] = _dot(c * _sigmoid(c), w_ref[0]) + b_ref[0]


def _ada_mod(c, ada_w, ada_b):
    depth, d, n6 = ada_w.shape
    bsz = c.shape[0]
    rows = 8
    cp = jnp.zeros((rows, d), F32).at[:bsz].set(c)
    tn = n6 // 4
    out = pl.pallas_call(
        _mod_kernel,
        out_shape=jax.ShapeDtypeStruct((depth, rows, n6), F32),
        grid=(depth, n6 // tn),
        in_specs=[pl.BlockSpec((rows, d), lambda l, j: (0, 0)),
                  pl.BlockSpec((1, d, tn), lambda l, j: (l, 0, j)),
                  pl.BlockSpec((1, 1, tn), lambda l, j: (l, 0, j))],
        out_specs=pl.BlockSpec((1, rows, tn), lambda l, j: (l, 0, j)),
        compiler_params=_cparams("parallel", "parallel"),
        name="ada_mod",
    )(cp, ada_w, ada_b.reshape(depth, 1, n6))
    return out[:, :bsz]


def _inproj_cm_kernel(x_ref, gain_ref, sc_ref, sh_ref, w_ref, *refs):
    cm_refs, rest_ref, zs_sc, zc_sc = refs[:-3], refs[-3], refs[-2], refs[-1]
    h = _rms(x_ref[0], gain_ref[...]) * (1.0 + sc_ref[0]) + sh_ref[0]
    z = _dot(h, w_ref[...])
    nblk, tm, _ = zs_sc.shape
    rest_ref[0] = z[:, nblk * LANES:]
    nq = A_HEADS * A_HEAD_DIM // LANES
    for c in range(nblk):
        blk = z[:, c * LANES:(c + 1) * LANES]
        zs_sc[c] = blk * (A_HEAD_DIM ** -0.5 * LOG2E) if c < nq else blk
    src, sd = zs_sc, 1
    for level, (cm_ref, (_, dil)) in enumerate(zip(cm_refs, DIL_CFG)):
        step, n = dil // sd, tm // dil
        keep = dil > 1 and level + 1 < len(DIL_CFG)
        for rs in range(sd):
            for cc in range(step):
                r = rs + sd * cc
                for c in range(nblk):
                    rows = src[c, pl.ds(rs * (tm // sd) + cc, n, stride=step), :]
                    cm_ref[0, r, :, c * LANES:(c + 1) * LANES] = rows.astype(BF16)
                    if keep:
                        zc_sc[c, r * n:(r + 1) * n, :] = rows
        if keep:
            src, sd = zc_sc, dil


def _inproj_cm(x, gain, sc, sh, w_bf16, na, tm):
    bsz, seq, d = x.shape
    n = w_bf16.shape[1]
    vec = pl.BlockSpec((1, 1, d), lambda b, i: (b, 0, 0))
    dils = [dl for _, dl in DIL_CFG]
    assert dils[0] == 1 and all(b % a == 0 for a, b in zip(dils, dils[1:]))
    return pl.pallas_call(
        _inproj_cm_kernel,
        out_shape=tuple([jax.ShapeDtypeStruct((bsz, dl, seq // dl, na), BF16) for dl in dils]
                        + [jax.ShapeDtypeStruct((bsz, seq, n - na), F32)]),
        grid=(bsz, seq // tm),
        in_specs=[pl.BlockSpec((1, tm, d), lambda b, i: (b, i, 0)),
                  _const_spec((1, d)), vec, vec, _const_spec((d, n))],
        out_specs=tuple([pl.BlockSpec((1, dl, tm // dl, na), lambda b, i: (b, 0, i, 0)) for dl in dils]
                        + [pl.BlockSpec((1, tm, n - na), lambda b, i: (b, i, 0))]),
        scratch_shapes=[pltpu.VMEM((na // LANES, tm, LANES), F32),
                        pltpu.VMEM((na // LANES, tm, LANES), F32)],
        compiler_params=_cparams("parallel", "parallel"),
        name="inproj_cm",
    )(x, gain.reshape(1, d), sc.reshape(bsz, 1, d), sh.reshape(bsz, 1, d), w_bf16)


def _t5_buckets(rel):
    half = N_BUCKETS // 2
    max_exact = half // 2
    n = np.abs(rel)
    large = max_exact + (np.log(np.maximum(n, 1) / max_exact) / np.log(T5_MAX_DIST / max_exact)
                         * (half - max_exact)).astype(np.int32)
    large = np.minimum(large, half - 1)
    return (np.where(rel > 0, half, 0) + np.where(n < max_exact, n, large)).astype(np.int32)


def _dil_bias(t5_bias, window, dil, tq):
    half = (window // 2) // dil
    assert half == tq // 2
    rel = np.arange(2 * tq)[None, :] - half - np.arange(tq)[:, None]
    inside = np.abs(rel) <= half
    buckets = _t5_buckets(np.where(inside, rel, 0) * dil)
    onehot =jnp.asarray(np.eye(N_BUCKETS, dtype=np.float32)[buckets])
    bias = jnp.einsum("qkn,nh->hqk", onehot, t5_bias.astype(F32), precision=lax.Precision.HIGHEST)
    return jnp.where(jnp.asarray(inside)[None], bias * LOG2E, NEG_INF)


def _dil_kernel(q_ref, kp_ref, kc_ref, kn_ref, vp_ref, vc_ref, vn_ref, bias_ref, o_ref, lse_ref,
                *, class_len):
    i = pl.program_id(2)
    sub, hq, tile = DIL_TQ, DIL_TQ // 2, q_ref.shape[2]
    kwin = jnp.concatenate([kp_ref[0, 0], kc_ref[0, 0], kn_ref[0, 0]], axis=0)
    vwin = jnp.concatenate([vp_ref[0, 0], vc_ref[0, 0], vn_ref[0, 0]], axis=0)
    lane = lax.broadcasted_iota(jnp.int32, (1, LANES), 1)
    lo = lane < A_HEAD_DIM
    nblk = A_HEADS // 2
    units = [(jt, j) for jt in range(tile // sub) for j in range(nblk)]
    logits = []
    for jt, j in units:
        cols = slice(j * LANES, (j + 1) * LANES)
        qj = q_ref[0, 0, jt * sub:(jt + 1) * sub, cols]
        zero = jnp.zeros_like(qj)
        q2 = jnp.concatenate([jnp.where(lo, qj, zero), jnp.where(lo, zero, qj)], axis=0)
        s = lax.dot_general(q2, kwin[jt * sub:jt * sub + 2 * sub, cols], (((1,), (1,)), ((), ())),
                            preferred_element_type=F32)
        kpos = i * tile + jt * sub - hq + lax.broadcasted_iota(jnp.int32, (1, 2 * sub), 1)
        valid = jnp.logical_and(kpos >= 0, kpos < class_len)
        logits.append(jnp.where(valid, s + bias_ref[j], NEG_INF))
    s_all = jnp.concatenate(logits, axis=0)
    m = jnp.max(s_all, axis=-1, keepdims=True)
    p32 = jnp.exp2(s_all - m)
    l = jnp.sum(p32, axis=-1, keepdims=True)
    p = p32.astype(BF16)
    rinv = 1.0 / l
    lse = m + jnp.log2(l)
    for jt in range(tile // sub):
        lse_all = jnp.zeros((sub, LANES), F32)
        for j in range(nblk):
            cols = slice(j * LANES, (j + 1) * LANES)
            r0 = (jt * nblk + j) * 2 * sub
            o2 = jnp.dot(p[r0:r0 + 2 * sub], vwin[jt * sub:jt * sub + 2 * sub, cols],
                         preferred_element_type=F32) * rinv[r0:r0 + 2 * sub]
            lse_all = jnp.where(lane == 2 * j, lse[r0:r0 + sub], lse_all)
            lse_all = jnp.where(lane == 2 * j + 1, lse[r0 + sub:r0 + 2 * sub], lse_all)
            o_ref[0, 0, jt * sub:(jt + 1) * sub, cols] = (
                jnp.where(lo, o2[:sub], o2[sub:]).astype(o_ref.dtype))
        lse_ref[0, 0, jt * sub:(jt + 1) * sub, :] = lse_all


def _dilated_branch(qkv_cm, t5_bias, window, dil):
    bsz, _, cl, width = qkv_cm.shape
    aw = A_HEADS * A_HEAD_DIM
    tile, hq = min(DIL_TILE, cl), DIL_TQ // 2
    nt = cl // tile
    per = tile // hq
    nh = cl // hq

    def cur(col):
        return pl.BlockSpec((1, 1, tile, aw), lambda b, r, i: (b, r, i, col))

    def prev(col):
        return pl.BlockSpec((1, 1, hq, aw), lambda b, r, i: (b, r, jnp.maximum(i * per - 1, 0), col))

    def nxt(col):
        return pl.BlockSpec((1, 1, hq, aw),
                            lambda b, r, i: (b, r, jnp.minimum((i + 1) * per, nh - 1), col))

    return pl.pallas_call(
        functools.partial(_dil_kernel, class_len=cl),
        out_shape=(jax.ShapeDtypeStruct((bsz, dil, cl, aw), BF16),
                   jax.ShapeDtypeStruct((bsz, dil, cl, LANES), F32)),
        grid=(bsz, dil, nt),
        in_specs=[cur(0), prev(1), cur(1), nxt(1), prev(2), cur(2), nxt(2),
                  _const_spec((A_HEADS // 2, 2 * DIL_TQ, 2 * DIL_TQ))],
        out_specs=(pl.BlockSpec((1, 1, tile, aw), lambda b, r, i: (b, r, i, 0)),
                   pl.BlockSpec((1, 1, tile, LANES), lambda b, r, i: (b, r, i, 0))),
        compiler_params=_cparams("parallel", "parallel", "parallel"),
        name=f"dilated_d{dil}",
    )(*([qkv_cm] * 7),
      _dil_bias(t5_bias, window, dil, DIL_TQ).reshape(A_HEADS // 2, 2 * DIL_TQ, 2 * DIL_TQ))


def _merge_branches(a_refs, l_refs, e_ref, a_scs, l_scs):
    tm = l_scs[0].shape[0]
    accs, lses = [], []
    for (_, dil), a_ref, l_ref, a_sc, l_sc in zip(DIL_CFG, a_refs, l_refs, a_scs, l_scs):
        nblk = a_sc.shape[0]
        for r in range(dil):
            rows = pl.ds(r, tm // dil, stride=dil)
            for c in range(nblk):
                a_sc[c, rows, :] = a_ref[0, r, :, c * LANES:(c + 1) * LANES].astype(F32)
            l_sc[rows, :] = l_ref[0, r]
        accs.append(jnp.concatenate([a_sc[c] for c in range(nblk)], axis=1))
        lses.append(l_sc[...])
    mx = functools.reduce(jnp.maximum, lses)
    ws = [jnp.exp2(x - mx) for x in lses]
    tot = functools.reduce(jnp.add, ws)
    out = None
    for w, a in zip(ws, accs):
        w_hi, w_lo = _split2(w / tot)
        wide = (jnp.dot(w_hi, e_ref[...], preferred_element_type=F32)
                + jnp.dot(w_lo, e_ref[...], preferred_element_type=F32))
        out = wide * a if out is None else out + wide * a
    return out


def _merge_operands(branch_outs, tm):
    aw = branch_outs[0][0].shape[-1]
    expand = np.zeros((LANES, aw), np.float32)
    for h in range(A_HEADS):
        expand[h, h * A_HEAD_DIM:(h + 1) * A_HEAD_DIM] = 1.0
    dils = [d for _, d in DIL_CFG]
    specs = ([pl.BlockSpec((1, d, tm // d, aw), lambda b, i: (b, 0, i, 0)) for d in dils]
             + [pl.BlockSpec((1, d, tm // d, LANES), lambda b, i: (b, 0, i, 0)) for d in dils]
             + [_const_spec(expand.shape)])
    arrays = [o for o, _ in branch_outs] + [l for _, l in branch_outs] + [jnp.asarray(expand, BF16)]
    scratch = ([pltpu.VMEM((aw // LANES, tm, LANES), F32) for _ in dils]
               + [pltpu.VMEM((tm, LANES), F32) for _ in dils])
    return arrays, specs, scratch


def _hgrn_consts(reverse):
    t = HGRN_SUB
    r = np.arange(t)
    u = r[None, :]
    row = r[:, None]
    nmats = [(u >= row) if reverse else (u <= row)]
    masks = []
    m = t // 2
    while m >= 1:
        grp = r // (2 * m)
        in_first = (r % (2 * m)) < m
        same = grp[:, None] == grp[None, :]
        if reverse:
            beta = (grp * 2 * m + m)[:, None]
            n = np.where(in_first[:, None], (u >= row) & (u < beta), (u >= beta) & (u < row))
            mask = same & in_first[:, None] & ~in_first[None, :]
        else:
            beta = (grp * 2 * m + m - 1)[:, None]
            n = np.where(in_first[:, None], (u > row) & (u <= beta), (u > beta) & (u <= row))
            mask = same & ~in_first[:, None] & in_first[None, :]
        if m < HGRN_BCAST_MIN:
            nmats.append(n)
        masks.append(mask)
        m //= 2
    masks.append(np.eye(t, dtype=bool))
    nmat = jnp.asarray(np.concatenate(nmats, axis=0), F32).astype(BF16)
    return nmat, jnp.asarray(np.stack(masks), F32)


def _hgrn_block(q, f, v, lb, st, nmat_ref, mask_ref, reverse):
    t = HGRN_SUB
    nlev = mask_ref.shape[0] - 1
    qs = q * (B_DK ** -0.5)
    fa = lb + (1.0 - lb) * _sigmoid(f)
    kk = 1.0 - fa
    g_hi, g_lo = _split2(jnp.log(fa))
    ex = jnp.dot(nmat_ref[...], jnp.concatenate([g_hi, g_lo], axis=1), preferred_element_type=F32)
    ex = ex[:, :B_DK] + ex[:, B_DK:]
    b = ex[:t]
    btot = b[0:1] if reverse else b[t - 1:t]
    a = mask_ref[nlev] * _dot_nt(qs, kk)
    fine = 1
    for l in range(nlev):
        m = t >> (l + 1)
        if m >= HGRN_BCAST_MIN:
            ref = jnp.concatenate(
                [jnp.broadcast_to(b[beta:beta + 1], (2 * m, B_DK))
                 for beta in range(m if reverse else m - 1, t, 2 * m)], axis=0)
            e = jnp.exp(-jnp.abs(b - ref))
        else:
            e = jnp.exp(ex[fine * t:(fine + 1) * t])
            fine += 1
        a = a + mask_ref[l] * _dot_nt(qs * e, kk * e)
    out = _dot(a, v) + _dot_nt(qs * jnp.exp(b), st)
    khat = (kk * jnp.exp(btot - b)).astype(BF16)
    st_new = st * jnp.exp(btot) + jnp.dot(v.T.astype(BF16), khat, preferred_element_type=F32)
    return out, st_new


def _hgrn_lb(lg_ref, layer):
    lg = [lg_ref[l, 0] for l in range(lg_ref.shape[0])]
    mx = functools.reduce(jnp.maximum, lg)
    e = [jnp.exp(x - mx) for x in lg]
    return functools.reduce(jnp.add, e[:layer + 1]) / functools.reduce(jnp.add, e)


def _hgrn_kernel(qf_ref, ff_ref, vf_ref, qb_ref, fb_ref, vb_ref, lgf_ref, lgb_ref,
                 nf_ref, mf_ref, nb_ref, mb_ref, of_ref, ob_ref, sf_sc, sb_sc, *, layer):
    @pl.when(pl.program_id(2) == 0)
    def _():
        sf_sc[...] = jnp.zeros_like(sf_sc)
        sb_sc[...] = jnp.zeros_like(sb_sc)

    nsub = HGRN_BLOCK // HGRN_SUB
    chains = ((qf_ref, ff_ref, vf_ref, lgf_ref, sf_sc, nf_ref, mf_ref, of_ref, False),
              (qb_ref, fb_ref, vb_ref, lgb_ref, sb_sc, nb_ref, mb_ref, ob_ref, True))
    for q_ref, f_ref, v_ref, lg_ref, st_sc, n_ref, m_ref, o_ref, reverse in chains:
        lb = _hgrn_lb(lg_ref, layer)
        st = st_sc[...]
        for sub in (reversed(range(nsub)) if reverse else range(nsub)):
            rows = pl.ds(sub * HGRN_SUB, HGRN_SUB)
            o, st = _hgrn_block(q_ref[0, rows, :], f_ref[0, rows, :], v_ref[0, rows, :], lb, st,
                                n_ref, m_ref, reverse)
            o_ref[0, rows, :] = o.astype(o_ref.dtype)
        st_sc[...] = st


def _hgrn(z, lb_logits, layer, col0):
    bsz, seq, _ = z.shape
    t = HGRN_BLOCK
    nb = seq // t
    c0 = col0 // LANES
    hw = B_HEADS

    def zspec(group, rev):
        return pl.BlockSpec(
            (1, t, LANES),
            lambda b, h, j: (b, (nb - 1 - j) if rev else j, c0 + group * hw + h))

    def lgspec(direction):
        return pl.BlockSpec((lb_logits.shape[0], 1, 1, LANES),
                            lambda b, h, j: (0, direction * hw + h, 0, 0))

    nf, mf = _hgrn_consts(False)
    nbw, mbw = _hgrn_consts(True)
    lg = lb_logits.astype(F32).reshape(lb_logits.shape[0], 2 * hw, 1, LANES)
    o_shape = jax.ShapeDtypeStruct((bsz, seq, hw * B_DV), BF16)
    return pl.pallas_call(
        functools.partial(_hgrn_kernel, layer=layer),
        out_shape=(o_shape, o_shape),
        grid=(bsz, hw, nb),
        in_specs=[zspec(0, False), zspec(1, False), zspec(3, False),
                  zspec(0, True), zspec(2, True), zspec(3, True),
                  lgspec(0), lgspec(1),
                  _const_spec(nf.shape), _const_spec(mf.shape),
                  _const_spec(nbw.shape), _const_spec(mbw.shape)],
        out_specs=(pl.BlockSpec((1, t, LANES), lambda b, h, j: (b, j, h)),
                   pl.BlockSpec((1, t, LANES), lambda b, h, j: (b, nb - 1 - j, h))),
        scratch_shapes=[pltpu.VMEM((B_DV, B_DK), F32), pltpu.VMEM((B_DV, B_DK), F32)],
        compiler_params=_cparams("parallel", "parallel", "arbitrary"),
        name="hgrn",
    )(z, z, z, z, z, z, lg, lg, nf, mf, nbw, mbw)


def _mix0_y(*refs):
    nbr = len(DIL_CFG)
    a_refs, l_refs, e_ref = refs[:nbr], refs[nbr:2 * nbr], refs[2 * nbr]
    of_ref, ob_ref, g_ref, on_ref, w_ref = refs[2 * nbr + 1:2 * nbr + 6]
    scratch = refs[2 * nbr + 6:]
    a = _merge_branches(a_refs, l_refs, e_ref, scratch[:nbr], scratch[nbr:])
    o = of_ref[0].astype(F32) + ob_ref[0].astype(F32)
    g = g_ref[0]
    parts = [_rms(o[:, h * B_DV:(h + 1) * B_DV], on_ref[...]) for h in range(B_HEADS)]
    bn = jnp.concatenate(parts, axis=-1) * (g * _sigmoid(g))
    na = a.shape[-1]
    return _dot(a, w_ref[:na, :]) + _dot(bn, w_ref[na:, :])


def _mix1_y(c_ref, d_ref, w_ref):
    c = jnp.concatenate([c_ref[0, g] for g in range(c_ref.shape[1])], axis=1)
    nc = c.shape[-1]
    return _dot(c, w_ref[:nc, :]) + _dot(d_ref[0], w_ref[nc:, :])


def _row_spec(tm, width, col=0):
    return pl.BlockSpec((1, tm, width), lambda b, i: (b, i, col))


def _tail_kernel(*refs, nmix, mix_fn, bounds):
    (x_ref, gm_ref, gatem_ref, g1_ref, sc_ref, sh_ref, wi_ref, wo_ref, g2_ref, gatef_ref,
     o_ref) = refs[nmix:nmix + 11]
    y_mix = mix_fn(*refs[:nmix], *refs[nmix + 11:])
    x1 = x_ref[0] + gatem_ref[0] * _rms(y_mix, gm_ref[...])
    h = (_rms(x1, g1_ref[...]) * (1.0 + sc_ref[0]) + sh_ref[0]).astype(BF16)
    hidden = wo_ref.shape[0]
    y = None
    for c0, c1 in zip(bounds, bounds[1:]):
        gt = jnp.dot(h, wi_ref[:, c0:c1], preferred_element_type=F32)
        up = jnp.dot(h, wi_ref[:, hidden + c0:hidden + c1], preferred_element_type=F32)
        part = _dot(gt * _sigmoid(gt) * up, wo_ref[c0:c1, :])
        y = part if y is None else y + part
    o_ref[0] = x1 + gatef_ref[0] * _rms(y, g2_ref[...])


def _layer_tail(mix_fn, mix_args, mix_specs, x, gain_m, gate_m, g1, sc, sh, wi_bf16, wo_bf16, layer,
                g2, gate_f, tm, name, mix_scratch=()):
    bsz, seq, d = x.shape
    vec = pl.BlockSpec((1, 1, d), lambda b, i: (b, 0, 0))
    hidden = wo_bf16.shape[1]

    def layer_slab(w):
        return pl.BlockSpec((None,) + w.shape[1:], lambda b, i: (layer, 0, 0),
                            pipeline_mode=pl.Buffered(1))

    ntile = hidden // MXU_WIDTH
    assert ntile * MXU_WIDTH == hidden
    bounds = (0, (ntile + 1) // 2 * MXU_WIDTH, hidden)
    row = lambda v: v.reshape(1, d)
    per_batch = lambda v: v.reshape(bsz, 1, d)
    return pl.pallas_call(
        functools.partial(_tail_kernel, nmix=len(mix_args), mix_fn=mix_fn, bounds=bounds),
        out_shape=jax.ShapeDtypeStruct(x.shape, F32),
        grid=(bsz, seq // tm),
        in_specs=list(mix_specs) + [_row_spec(tm, d), _const_spec((1, d)), vec, _const_spec((1, d)),
                                    vec, vec, layer_slab(wi_bf16), layer_slab(wo_bf16),
                                    _const_spec((1, d)), vec],
        out_specs=_row_spec(tm, d),
        scratch_shapes=list(mix_scratch),
        compiler_params=_cparams("parallel", "parallel"),
        name=name,
    )(*mix_args, x, row(gain_m), per_batch(gate_m), row(g1), per_batch(sc), per_batch(sh),
      wi_bf16, wo_bf16, row(g2), per_batch(gate_f))


def _tail0(branches, o_f, o_b, z, g_col, out_norm, w_bf16, tm, **kw):
    wv = B_HEADS * B_DV
    m_arrays, m_specs, m_scratch = _merge_operands(branches, tm)
    specs = m_specs + [_row_spec(tm, wv), _row_spec(tm, wv), _row_spec(tm, wv, g_col // wv),
                       _const_spec((1, B_DV)), _const_spec(w_bf16.shape)]
    args = m_arrays + [o_f, o_b, z, out_norm.reshape(1, B_DV), w_bf16]
    return _layer_tail(_mix0_y, args, specs, tm=tm, name="tail0", mix_scratch=m_scratch, **kw)


def _tail1(c_out, d_out, w_bf16, tm, **kw):
    specs = [pl.BlockSpec((1, c_out.shape[1], tm, c_out.shape[3]), lambda b, i: (b, 0, i, 0)),
             _row_spec(tm, d_out.shape[-1]), _const_spec(w_bf16.shape)]
    return _layer_tail(_mix1_y, (c_out, d_out, w_bf16), specs, tm=tm, name="tail1", **kw)


def _fft_kernel(u_ref, f1_ref, twc_ref, tws_ref, f2_ref, fw_ref, o_ref, u_sc, p_sc, y_sc,
                *, scale, n1, n2):
    pu = n2 + FFT_PAD
    pp = 2 * n1 + FFT_PAD
    py = n1 + FFT_PAD
    f1 = f1_ref[...].astype(BF16)
    f2 = f2_ref[...].astype(BF16)
    fw = fw_ref[...].astype(BF16)
    for i1 in range(n1):
        u_sc[i1 * pu:i1 * pu + n2, :] = u_ref[0, 0, i1 * n2:(i1 + 1) * n2, :]

    nb = FFT_BATCH

    def stage1(blk, carry):
        i2s = [blk * nb + j for j in range(nb)]
        x = jnp.concatenate([u_sc[pl.ds(i2, n1, stride=pu), :] for i2 in i2s], axis=1)
        p = jnp.dot(f1, x.astype(BF16), preferred_element_type=F32)
        for j, i2 in enumerate(i2s):
            p_sc[pl.ds(pl.multiple_of(i2 * pp, 8), 2 * n1), :] = p[:, j * C_WIDTH:(j + 1) * C_WIDTH]
        return carry

    lax.fori_loop(0, n2 // nb, stage1, 0, unroll=2)

    def stage2(blk, carry):
        k1s = [blk * nb + j for j in range(nb)]
        qr, qi = [], []
        for k1 in k1s:
            tc = twc_ref[k1]
            ts = tws_ref[k1]
            pr = p_sc[pl.ds(k1, n2, stride=pp), :]
            pim = p_sc[pl.ds(n1 + k1, n2, stride=pp), :]
            qr.append(pr * tc + pim * ts)
            qi.append(pim * tc - pr * ts)
        q = jnp.concatenate([jnp.concatenate(qr, axis=1), jnp.concatenate(qi, axis=1)], axis=0)
        xx = jnp.dot(f2, q.astype(BF16), preferred_element_type=F32)
        xg = jnp.concatenate(
            [jnp.concatenate([xx[:n2, j * C_WIDTH:(j + 1) * C_WIDTH],
                              xx[n2:, j * C_WIDTH:(j + 1) * C_WIDTH]], axis=1) for j in range(nb)],
            axis=0)
        y = jnp.dot(xg.astype(BF16), fw, preferred_element_type=F32) * scale
        for j, k1 in enumerate(k1s):
            y_sc[pl.ds(k1, n2, stride=py), :] = y[j * n2:(j + 1) * n2]
        return carry

    lax.fori_loop(0, n1 // nb, stage2, 0, unroll=2)
    for k2 in range(n2):
        o_ref[0, 0, k2 * n1:(k2 + 1) * n1, :] = y_sc[k2 * py:k2 * py + n1, :]


def _fourier_mixer(u):
    bsz, ngroups, seq, width = u.shape
    n2 = FFT_N2
    n1 = seq // n2
    assert n1 * n2 == seq and width == C_WIDTH and n1 % 8 == 0
    a1 = 2.0 * np.pi * np.outer(np.arange(n1), np.arange(n1)) / n1
    f1 = np.concatenate([np.cos(a1), -np.sin(a1)], axis=0)
    a2 = 2.0 * np.pi * np.outer(np.arange(n2), np.arange(n2)) / n2
    c2, s2 = np.cos(a2), np.sin(a2)
    f2 = np.block([[c2, s2], [-s2, c2]])
    aw = 2.0 * np.pi * np.outer(np.arange(C_WIDTH), np.arange(C_WIDTH)) / C_WIDTH
    fw = np.concatenate([np.cos(aw), np.sin(aw)], axis=0)
    at = np.repeat((2.0 * np.pi * np.outer(np.arange(n1), np.arange(n2)) / seq)[:, :, None],
                   C_WIDTH, axis=2)
    consts = (jnp.asarray(f1, F32), jnp.asarray(np.cos(at), F32), jnp.asarray(np.sin(at), F32),
              jnp.asarray(f2, F32), jnp.asarray(fw, F32))
    blk = pl.BlockSpec((1, 1, seq, C_WIDTH), lambda b, g: (b, g, 0, 0))
    return pl.pallas_call(
        functools.partial(_fft_kernel, scale=float(1.0 / np.sqrt(seq * C_WIDTH)), n1=n1, n2=n2),
        out_shape=jax.ShapeDtypeStruct(u.shape, F32),
        grid=(bsz, ngroups),
        in_specs=[blk] + [_const_spec(c.shape) for c in consts],
        out_specs=blk,
        scratch_shapes=[pltpu.VMEM((n1 * (n2 + FFT_PAD), C_WIDTH), F32),
                        pltpu.VMEM((n2 * (2 * n1 + FFT_PAD), C_WIDTH), F32),
                        pltpu.VMEM((n2 * (n1 + FFT_PAD), C_WIDTH), F32)],
        compiler_params=_cparams("parallel", "parallel"),
        name="fft",
    )(u, *consts)


def _head_perm():
    rep = D_Q_HEADS // D_KV_HEADS
    cols = []
    for j in range(rep):
        for g in range(D_KV_HEADS):
            h = g * rep + j
            cols.extend(range(h * D_HEAD_DIM, (h + 1) * D_HEAD_DIM))
    return np.asarray(cols, np.int32)


def _rope_tables(seq):
    rows = seq // GRID_W
    row = jnp.repeat(jnp.arange(rows, dtype=F32), GRID_W)
    col = jnp.tile(jnp.arange(GRID_W, dtype=F32), rows)
    axis_dim = D_HEAD_DIM // 2
    inv_freq = jnp.power(ROPE_THETA, -jnp.arange(0, axis_dim, 2, dtype=F32) / axis_dim)
    ang_r = row[:, None] * inv_freq[None, :]
    ang_c = col[:, None] * inv_freq[None, :]
    cr, sr, cc, sc = jnp.cos(ang_r), jnp.sin(ang_r), jnp.cos(ang_c), jnp.sin(ang_c)
    cos = jnp.concatenate([cr, cr, cc, cc], axis=1)
    sin = jnp.concatenate([-sr, sr, -sc, sc], axis=1)
    reps = LANES // D_HEAD_DIM
    return jnp.tile(cos, (1, reps)), jnp.tile(sin, (1, reps))


def _inproj_qk_kernel(x_ref, gain_ref, sc_ref, sh_ref, w_ref, cos_ref, sin_ref, bd_h_ref, bd_l_ref,
                      gq_ref, gk_ref, u_ref, qo_ref, ko_ref, vo_ref):
    h = _rms(x_ref[0], gain_ref[...]) * (1.0 + sc_ref[0]) + sh_ref[0]
    z = _dot(h, w_ref[...])
    ngrp = u_ref.shape[1]
    for g in range(ngrp):
        u_ref[0, g] = z[:, g * LANES:(g + 1) * LANES]
    cos = cos_ref[...]
    sin = sin_ref[...]
    quarter = D_HEAD_DIM // 4
    lane = lax.broadcasted_iota(jnp.int32, (1, LANES), 1)
    first_of_pair = (lane // quarter) % 2 == 0

    def norm_rope(x, gain, scale):
        ms = _dot_tab(bd_h_ref[...], bd_l_ref[...], x * x, tab_left=False)
        xn = x * lax.rsqrt(ms + EPS) * gain
        partner = jnp.where(first_of_pair, pltpu.roll(xn, LANES - quarter, 1),
                            pltpu.roll(xn, quarter, 1))
        return ((xn * cos + partner * sin) * scale).astype(BF16)

    nq = qo_ref.shape[-1] // LANES
    for j in range(nq):
        qo_ref[0, :, j * LANES:(j + 1) * LANES] = norm_rope(
            z[:, (ngrp + j) * LANES:(ngrp + j + 1) * LANES], gq_ref[...], D_HEAD_DIM ** -0.5 * LOG2E)
    ko_ref[0] = norm_rope(z[:, (ngrp + nq) * LANES:(ngrp + nq + 1) * LANES], gk_ref[...], 1.0)
    vt = z[:, (ngrp + nq + 1) * LANES:(ngrp + nq + 2) * LANES].T
    ones = jnp.ones((FLASH_ONES, vt.shape[1]), F32)
    vo_ref[0] = jnp.concatenate(
        [piece for g in range(D_KV_HEADS)
         for piece in (vt[g * D_HEAD_DIM:(g + 1) * D_HEAD_DIM], ones)], axis=0).astype(BF16)


def _inproj_qk(x, gain, sc, sh, w_bf16, qk_norm_j, tm):
    bsz, seq, d = x.shape
    n = w_bf16.shape[1]
    qw = D_Q_HEADS * D_HEAD_DIM
    kw = D_KV_HEADS * D_HEAD_DIM
    assert kw == LANES and n == C_GROUPS * C_WIDTH + qw + 2 * kw
    vrows = D_KV_HEADS * (D_HEAD_DIM + FLASH_ONES)
    cos, sin = _rope_tables(seq)
    bd = np.kron(np.eye(LANES // D_HEAD_DIM), np.full((D_HEAD_DIM, D_HEAD_DIM), 1.0 / D_HEAD_DIM))
    bd_h, bd_l = _np_split2(bd)
    reps = LANES // D_HEAD_DIM
    gq = jnp.tile(qk_norm_j[0].astype(F32), reps).reshape(1, LANES)
    gk = jnp.tile(qk_norm_j[1].astype(F32), reps).reshape(1, LANES)
    tab = pl.BlockSpec((tm, LANES), lambda b, i: (i, 0))
    vec = pl.BlockSpec((1, 1, d), lambda b, i: (b, 0, 0))
    return pl.pallas_call(
        _inproj_qk_kernel,
        out_shape=(jax.ShapeDtypeStruct((bsz, C_GROUPS, seq, C_WIDTH), F32),
                   jax.ShapeDtypeStruct((bsz, seq, qw), BF16),
                   jax.ShapeDtypeStruct((bsz, seq, kw), BF16),
                   jax.ShapeDtypeStruct((bsz, vrows, seq), BF16)),
        grid=(bsz, seq // tm),
        in_specs=[pl.BlockSpec((1, tm, d), lambda b, i: (b, i, 0)),
                  _const_spec((1, d)), vec, vec, _const_spec((d, n)),
                  tab, tab, _const_spec(bd_h.shape), _const_spec(bd_l.shape),
                  _const_spec((1, LANES)), _const_spec((1, LANES))],
        out_specs=(pl.BlockSpec((1, C_GROUPS, tm, C_WIDTH), lambda b, i: (b, 0, i, 0)),
                   _row_spec(tm, qw), _row_spec(tm, kw),
                   pl.BlockSpec((1, vrows, tm), lambda b, i: (b, 0, i))),
        compiler_params=_cparams("parallel", "parallel"),
        name="inproj_qk",
    )(x, gain.reshape(1, d), sc.reshape(bsz, 1, d), sh.reshape(bsz, 1, d), w_bf16,
      cos, sin, bd_h, bd_l, gq, gk)


def _flash_kernel(q_ref, k_ref, vt_ref, o_ref, m_sc, acc_sc, s_sc):
    kv = pl.program_id(2)

    @pl.when(kv == 0)
    def _():
        m_sc[...] = jnp.full_like(m_sc, -jnp.inf)
        acc_sc[...] = jnp.zeros_like(acc_sc)

    lane = lax.broadcasted_iota(jnp.int32, (1, LANES), 1)
    lo = lane < D_HEAD_DIM
    nblk = q_ref.shape[-1] // LANES
    tq, tk = q_ref.shape[1], k_ref.shape[1]
    ku, qu = FLASH_KEY_UNIT, FLASH_QUERY_UNIT
    grows = D_HEAD_DIM + FLASH_ONES
    nheads = nblk * D_KV_HEADS
    k = k_ref[0]

    nchunk = tq // qu

    def logits_chunk(idx, c):
        j, g = divmod(idx, D_KV_HEADS)
        qj = q_ref[0, c * qu:(c + 1) * qu, j * LANES:(j + 1) * LANES]
        sel = lo if g == 0 else jnp.logical_not(lo)
        s = lax.dot_general(k, jnp.where(sel, qj, jnp.zeros_like(qj)), (((1,), (1,)), ((), ())),
                            preferred_element_type=F32)
        s_sc[idx, :, c * qu:(c + 1) * qu] = s
        return jnp.max(s, axis=0, keepdims=True)

    def finish_logits(idx, mcs):
        m_prev = m_sc[idx, 0:1, :]
        m_new = jnp.maximum(m_prev, jnp.concatenate(mcs, axis=1))
        m_sc[idx, 0:1, :] = m_new
        return m_new, jnp.exp2(m_prev - m_new)

    def value_chunk(idx, c, m_new, alpha):
        j, g = divmod(idx, D_KV_HEADS)
        rows = slice(g * grows, (g + 1) * grows)
        qcols = slice(c * qu, (c + 1) * qu)
        pv = None
        for u in range(tk // ku):
            keys = slice(u * ku, (u + 1) * ku)
            p = jnp.exp2(s_sc[idx, keys, qcols] - m_new[:, qcols])
            d = jnp.dot(vt_ref[0, rows, keys], p.astype(BF16), preferred_element_type=F32)
            pv = d if pv is None else pv + d
        acc_sc[j, rows, qcols] = alpha[:, qcols] * acc_sc[j, rows, qcols] + pv

    def logits_pass(idx):
        return finish_logits(idx, [logits_chunk(idx, c) for c in range(nchunk)])

    pending = [logits_pass(i) for i in range(min(FLASH_AHEAD, nheads))]
    for idx in range(nheads):
        if idx + FLASH_AHEAD < nheads:
            pending.append(logits_pass(idx + FLASH_AHEAD))
        stats = pending.pop(0)
        for c in range(nchunk):
            value_chunk(idx, c, *stats)

    @pl.when(kv == pl.num_programs(2) - 1)
    def _():
        for j in range(nblk):
            parts = []
            for g in range(D_KV_HEADS):
                num = acc_sc[j, g * grows:g * grows + D_HEAD_DIM, :]
                den = acc_sc[j, g * grows + D_HEAD_DIM:g * grows + D_HEAD_DIM + 1, :]
                parts.append(num / den)
            o_ref[0, :, j * LANES:(j + 1) * LANES] = (
                jnp.concatenate(parts, axis=0).T.astype(o_ref.dtype))


def _flash(q, k, vt, tq, tk):
    bsz, seq, qw = q.shape
    kw = k.shape[-1]
    vrows = vt.shape[1]
    return pl.pallas_call(
        _flash_kernel,
        out_shape=jax.ShapeDtypeStruct((bsz, seq, qw), BF16),
        grid=(bsz, seq // tq, seq // tk),
        in_specs=[pl.BlockSpec((1, tq, qw), lambda b, i, j: (b, i, 0)),
                  pl.BlockSpec((1, tk, kw), lambda b, i, j: (b, j, 0)),
                  pl.BlockSpec((1, vrows, tk), lambda b, i, j: (b, 0, j))],
        out_specs=pl.BlockSpec((1, tq, qw), lambda b, i, j: (b, i, 0)),
        scratch_shapes=[pltpu.VMEM((D_Q_HEADS, 8, tq), F32),
                        pltpu.VMEM((qw // LANES, vrows, tq), F32),
                        pltpu.VMEM((D_Q_HEADS, tk, tq), F32)],
        compiler_params=_cparams("parallel", "parallel", "arbitrary"),
        name="flash",
    )(q, k, vt)


def kernel(x, c, t5_bias, hgrn_lb_logits, ada_w, ada_b, norm_gains, ab_w_in, ab_w_out,
           hgrn_out_norm, cd_w_in, cd_w_out, qk_norm, ffn_w_in, ffn_w_out):
    bsz, seq, d = x.shape
    depth = ada_w.shape[0]
    mod = _ada_mod(c.astype(F32), ada_w, ada_b)
    perm = _head_perm()
    aw = A_HEADS * A_HEAD_DIM
    cw = C_GROUPS * C_WIDTH
    qw = D_Q_HEADS * D_HEAD_DIM
    tm_in = min(512, seq)
    tm = min(512, seq)
    ffn_wi = ffn_w_in.astype(BF16)
    ffn_wo = ffn_w_out.astype(BF16)
    for layer in range(depth):
        sh_m, sc_m, g_m, sh_f, sc_f, g_f = [mod[layer, :, i * d:(i + 1) * d] for i in range(6)]
        gains = norm_gains[layer]
        j = layer // 2
        tail = dict(x=x, gain_m=gains[1], gate_m=g_m, g1=gains[2], sc=sc_f, sh=sh_f,
                    wi_bf16=ffn_wi, wo_bf16=ffn_wo, layer=layer, g2=gains[3], gate_f=g_f, tm=tm)
        if layer % 2 == 0:
            w_in = ab_w_in[j].astype(BF16)
            *qkv_cm, z = _inproj_cm(x, gains[0], sc_m, sh_m, w_in, 3 * aw, tm_in)
            branches = [_dilated_branch(cm, t5_bias, window, dil)
                        for cm, (window, dil) in zip(qkv_cm, DIL_CFG)]
            o_f, o_b = _hgrn(z, hgrn_lb_logits, layer, 0)
            g_col = 3 * B_HEADS * B_DK + B_HEADS * B_DV
            x = _tail0(branches, o_f, o_b, z, g_col, hgrn_out_norm[j], ab_w_out[j].astype(BF16), **tail)
        else:
            w_full = cd_w_in[j]
            w_in = jnp.concatenate([w_full[:, :cw], w_full[:, cw:cw + qw][:, perm],
                                    w_full[:, cw + qw:]], axis=1).astype(BF16)
            u, qn, kn, vn = _inproj_qk(x, gains[0], sc_m, sh_m, w_in, qk_norm[j], min(1024, seq))
            c_out = _fourier_mixer(u)
            d_out = _flash(qn, kn, vn, min(FLASH_TQ, seq), min(FLASH_TK, seq))
            w_out_full = cd_w_out[j]
            w_out = jnp.concatenate([w_out_full[:cw], w_out_full[cw:][perm]], axis=0).astype(BF16)
            x = _tail1(c_out, d_out, w_out, **tail)
    return x
```

```python
import functools

import numpy as np
import jax
import jax.numpy as jnp
from jax import lax
from jax.experimental import pallas as pl
from jax.experimental.pallas import tpu as pltpu

F32 = jnp.float32
BF16 = jnp.bfloat16
LANES = 128
MXU_WIDTH = 256
VMEM_LIMIT_BYTES = 56 * 2**20
NEG_INF = -1e30
EPS = 1e-6

GRID_W = 64
A_HEADS = 8
A_HEAD_DIM = 64
DIL_CFG = ((128, 1), (512, 4), (2048, 16))
N_BUCKETS = 32
T5_MAX_DIST = 1024
B_HEADS = 4
B_DK = 128
B_DV = 128
C_GROUPS = 4
C_WIDTH = 128
D_Q_HEADS = 8
D_KV_HEADS = 2
D_HEAD_DIM = 64
ROPE_THETA = 10000.0

DIL_TQ = 128
DIL_TILE = 512
HGRN_BLOCK = 1024
HGRN_SUB = 256
HGRN_BCAST_MIN = 8
FLASH_TQ = 1024
FLASH_TK = 1024
FLASH_KEY_UNIT = 256
FLASH_QUERY_UNIT = 256
FLASH_AHEAD = 1
FLASH_ONES = 16
FFT_N2 = 128
FFT_BATCH = 8
FFT_PAD = 8
LOG2E = 1.4426950408889634


def _cparams(*sem):
    return pltpu.CompilerParams(dimension_semantics=sem, vmem_limit_bytes=VMEM_LIMIT_BYTES)


def _const_spec(shape):
    nd = len(shape)
    return pl.BlockSpec(shape, lambda *_: (0,) * nd, pipeline_mode=pl.Buffered(1))


def _sigmoid(x):
    return 1.0 / (1.0 + jnp.exp(-x))


def _dot(a, b):
    return jnp.dot(a.astype(BF16), b.astype(BF16), preferred_element_type=F32)


def _dot_nt(a, b):
    return lax.dot_general(a.astype(BF16), b.astype(BF16), (((1,), (1,)), ((), ())),
                           preferred_element_type=F32)


def _split2(a):
    hi = a.astype(BF16)
    lo = (a - hi.astype(F32)).astype(BF16)
    return hi, lo


def _split3(a):
    a1 = a.astype(BF16)
    r = a - a1.astype(F32)
    a2 = r.astype(BF16)
    a3 = (r - a2.astype(F32)).astype(BF16)
    return a1, a2, a3


def _dot_tab(tab_hi, tab_lo, x, *, tab_left):
    x_hi, x_lo = _split2(x)
    if tab_left:
        d = lambda t, v: jnp.dot(t, v, preferred_element_type=F32)
    else:
        d = lambda t, v: jnp.dot(v, t, preferred_element_type=F32)
    return d(tab_hi, x_hi) + (d(tab_hi, x_lo) + d(tab_lo, x_hi))


def _rms(x, gain):
    ms = jnp.mean(x * x, axis=-1, keepdims=True)
    return x * lax.rsqrt(ms + EPS) * gain


def _np_split2(t):
    t = np.asarray(t, np.float32)
    hi = jnp.asarray(t, F32).astype(BF16)
    lo = (jnp.asarray(t, F32) - hi.astype(F32)).astype(BF16)
    return hi, lo


def _mod_kernel(c_ref, w_ref, b_ref, o_ref):
    c = c_ref[...]
    o_ref[0] = _dot(c * _sigmoid(c), w_ref[0]) + b_ref[0]


def _ada_mod(c, ada_w, ada_b):
    depth, d, n6 = ada_w.shape
    bsz = c.shape[0]
    rows = 8
    cp = jnp.zeros((rows, d), F32).at[:bsz].set(c)
    tn = n6 // 4
    out = pl.pallas_call(
        _mod_kernel,
        out_shape=jax.ShapeDtypeStruct((depth, rows, n6), F32),
        grid=(depth, n6 // tn),
        in_specs=[pl.BlockSpec((rows, d), lambda l, j: (0, 0)),
                  pl.BlockSpec((1, d, tn), lambda l, j: (l, 0, j)),
                  pl.BlockSpec((1, 1, tn), lambda l, j: (l, 0, j))],
        out_specs=pl.BlockSpec((1, rows, tn), lambda l, j: (l, 0, j)),
        compiler_params=_cparams("parallel", "parallel"),
        name="ada_mod",
    )(cp, ada_w, ada_b.reshape(depth, 1, n6))
    return out[:, :bsz]


def _inproj_cm_kernel(x_ref, gain_ref, sc_ref, sh_ref, w_ref, *refs):
    cm_refs, rest_ref, zs_sc, zc_sc = refs[:-3], refs[-3], refs[-2], refs[-1]
    h = _rms(x_ref[0], gain_ref[...]) * (1.0 + sc_ref[0]) + sh_ref[0]
    z = _dot(h, w_ref[...])
    nblk, tm, _ = zs_sc.shape
    rest_ref[0] = z[:, nblk * LANES:]
    nq = A_HEADS * A_HEAD_DIM // LANES
    for c in range(nblk):
        blk = z[:, c * LANES:(c + 1) * LANES]
        zs_sc[c] = blk * (A_HEAD_DIM ** -0.5 * LOG2E) if c < nq else blk
    src, sd = zs_sc, 1
    for level, (cm_ref, (_, dil)) in enumerate(zip(cm_refs, DIL_CFG)):
        step, n = dil // sd, tm // dil
        keep = dil > 1 and level + 1 < len(DIL_CFG)
        for rs in range(sd):
            for cc in range(step):
                r = rs + sd * cc
                for c in range(nblk):
                    rows = src[c, pl.ds(rs * (tm // sd) + cc, n, stride=step), :]
                    cm_ref[0, r, :, c * LANES:(c + 1) * LANES] = rows.astype(BF16)
                    if keep:
                        zc_sc[c, r * n:(r + 1) * n, :] = rows
        if keep:
            src, sd = zc_sc, dil


def _inproj_cm(x, gain, sc, sh, w_bf16, na, tm):
    bsz, seq, d = x.shape
    n = w_bf16.shape[1]
    vec = pl.BlockSpec((1, 1, d), lambda b, i: (b, 0, 0))
    dils = [dl for _, dl in DIL_CFG]
    assert dils[0] == 1 and all(b % a == 0 for a, b in zip(dils, dils[1:]))
    return pl.pallas_call(
        _inproj_cm_kernel,
        out_shape=tuple([jax.ShapeDtypeStruct((bsz, dl, seq // dl, na), BF16) for dl in dils]
                        + [jax.ShapeDtypeStruct((bsz, seq, n - na), F32)]),
        grid=(bsz, seq // tm),
        in_specs=[pl.BlockSpec((1, tm, d), lambda b, i: (b, i, 0)),
                  _const_spec((1, d)), vec, vec, _const_spec((d, n))],
        out_specs=tuple([pl.BlockSpec((1, dl, tm // dl, na), lambda b, i: (b, 0, i, 0)) for dl in dils]
                        + [pl.BlockSpec((1, tm, n - na), lambda b, i: (b, i, 0))]),
        scratch_shapes=[pltpu.VMEM((na // LANES, tm, LANES), F32),
                        pltpu.VMEM((na // LANES, tm, LANES), F32)],
        compiler_params=_cparams("parallel", "parallel"),
        name="inproj_cm",
    )(x, gain.reshape(1, d), sc.reshape(bsz, 1, d), sh.reshape(bsz, 1, d), w_bf16)


def _t5_buckets(rel):
    half = N_BUCKETS // 2
    max_exact = half // 2
    n = np.abs(rel)
    large = max_exact + (np.log(np.maximum(n, 1) / max_exact) / np.log(T5_MAX_DIST / max_exact)
                         * (half - max_exact)).astype(np.int32)
    large = np.minimum(large, half - 1)
    return (np.where(rel > 0, half, 0) + np.where(n < max_exact, n, large)).astype(np.int32)


def _dil_bias(t5_bias, window, dil, tq):
    half = (window // 2) // dil
    assert half == tq // 2
    rel = np.arange(2 * tq)[None, :] - half - np.arange(tq)[:, None]
    inside = np.abs(rel) <= half
    buckets = _t5_buckets(np.where(inside, rel, 0) * dil)
    onehot =jnp.asarray(np.eye(N_BUCKETS, dtype=np.float32)[buckets])
    bias = jnp.einsum("qkn,nh->hqk", onehot, t5_bias.astype(F32), precision=lax.Precision.HIGHEST)
    return jnp.where(jnp.asarray(inside)[None], bias * LOG2E, NEG_INF)


def _dil_kernel(q_ref, kp_ref, kc_ref, kn_ref, vp_ref, vc_ref, vn_ref, bias_ref, o_ref, lse_ref,
                *, class_len):
    i = pl.program_id(2)
    sub, hq, tile = DIL_TQ, DIL_TQ // 2, q_ref.shape[2]
    kwin = jnp.concatenate([kp_ref[0, 0], kc_ref[0, 0], kn_ref[0, 0]], axis=0)
    vwin = jnp.concatenate([vp_ref[0, 0], vc_ref[0, 0], vn_ref[0, 0]], axis=0)
    lane = lax.broadcasted_iota(jnp.int32, (1, LANES), 1)
    lo = lane < A_HEAD_DIM
    nblk = A_HEADS // 2
    units = [(jt, j) for jt in range(tile // sub) for j in range(nblk)]
    logits = []
    for jt, j in units:
        cols = slice(j * LANES, (j + 1) * LANES)
        qj = q_ref[0, 0, jt * sub:(jt + 1) * sub, cols]
        zero = jnp.zeros_like(qj)
        q2 = jnp.concatenate([jnp.where(lo, qj, zero), jnp.where(lo, zero, qj)], axis=0)
        s = lax.dot_general(q2, kwin[jt * sub:jt * sub + 2 * sub, cols], (((1,), (1,)), ((), ())),
                            preferred_element_type=F32)
        kpos = i * tile + jt * sub - hq + lax.broadcasted_iota(jnp.int32, (1, 2 * sub), 1)
        valid = jnp.logical_and(kpos >= 0, kpos < class_len)
        logits.append(jnp.where(valid, s + bias_ref[j], NEG_INF))
    s_all = jnp.concatenate(logits, axis=0)
    m = jnp.max(s_all, axis=-1, keepdims=True)
    p32 = jnp.exp2(s_all - m)
    l = jnp.sum(p32, axis=-1, keepdims=True)
    p = p32.astype(BF16)
    rinv = 1.0 / l
    lse = m + jnp.log2(l)
    for jt in range(tile // sub):
        lse_all = jnp.zeros((sub, LANES), F32)
        for j in range(nblk):
            cols = slice(j * LANES, (j + 1) * LANES)
            r0 = (jt * nblk + j) * 2 * sub
            o2 = jnp.dot(p[r0:r0 + 2 * sub], vwin[jt * sub:jt * sub + 2 * sub, cols],
                         preferred_element_type=F32) * rinv[r0:r0 + 2 * sub]
            lse_all = jnp.where(lane == 2 * j, lse[r0:r0 + sub], lse_all)
            lse_all = jnp.where(lane == 2 * j + 1, lse[r0 + sub:r0 + 2 * sub], lse_all)
            o_ref[0, 0, jt * sub:(jt + 1) * sub, cols] = (
                jnp.where(lo, o2[:sub], o2[sub:]).astype(o_ref.dtype))
        lse_ref[0, 0, jt * sub:(jt + 1) * sub, :] = lse_all


def _dilated_branch(qkv_cm, t5_bias, window, dil):
    bsz, _, cl, width = qkv_cm.shape
    aw = A_HEADS * A_HEAD_DIM
    tile, hq = min(DIL_TILE, cl), DIL_TQ // 2
    nt = cl // tile
    per = tile // hq
    nh = cl // hq

    def cur(col):
        return pl.BlockSpec((1, 1, tile, aw), lambda b, r, i: (b, r, i, col))

    def prev(col):
        return pl.BlockSpec((1, 1, hq, aw), lambda b, r, i: (b, r, jnp.maximum(i * per - 1, 0), col))

    def nxt(col):
        return pl.BlockSpec((1, 1, hq, aw),
                            lambda b, r, i: (b, r, jnp.minimum((i + 1) * per, nh - 1), col))

    return pl.pallas_call(
        functools.partial(_dil_kernel, class_len=cl),
        out_shape=(jax.ShapeDtypeStruct((bsz, dil, cl, aw), BF16),
                   jax.ShapeDtypeStruct((bsz, dil, cl, LANES), F32)),
        grid=(bsz, dil, nt),
        in_specs=[cur(0), prev(1), cur(1), nxt(1), prev(2), cur(2), nxt(2),
                  _const_spec((A_HEADS // 2, 2 * DIL_TQ, 2 * DIL_TQ))],
        out_specs=(pl.BlockSpec((1, 1, tile, aw), lambda b, r, i: (b, r, i, 0)),
                   pl.BlockSpec((1, 1, tile, LANES), lambda b, r, i: (b, r, i, 0))),
        compiler_params=_cparams("parallel", "parallel", "parallel"),
        name=f"dilated_d{dil}",
    )(*([qkv_cm] * 7),
      _dil_bias(t5_bias, window, dil, DIL_TQ).reshape(A_HEADS // 2, 2 * DIL_TQ, 2 * DIL_TQ))


def _merge_branches(a_refs, l_refs, e_ref, a_scs, l_scs):
    tm = l_scs[0].shape[0]
    accs, lses = [], []
    for (_, dil), a_ref, l_ref, a_sc, l_sc in zip(DIL_CFG, a_refs, l_refs, a_scs, l_scs):
        nblk = a_sc.shape[0]
        for r in range(dil):
            rows = pl.ds(r, tm // dil, stride=dil)
            for c in range(nblk):
                a_sc[c, rows, :] = a_ref[0, r, :, c * LANES:(c + 1) * LANES].astype(F32)
            l_sc[rows, :] = l_ref[0, r]
        accs.append(jnp.concatenate([a_sc[c] for c in range(nblk)], axis=1))
        lses.append(l_sc[...])
    mx = functools.reduce(jnp.maximum, lses)
    ws = [jnp.exp2(x - mx) for x in lses]
    tot = functools.reduce(jnp.add, ws)
    out = None
    for w, a in zip(ws, accs):
        w_hi, w_lo = _split2(w / tot)
        wide = (jnp.dot(w_hi, e_ref[...], preferred_element_type=F32)
                + jnp.dot(w_lo, e_ref[...], preferred_element_type=F32))
        out = wide * a if out is None else out + wide * a
    return out


def _merge_operands(branch_outs, tm):
    aw = branch_outs[0][0].shape[-1]
    expand = np.zeros((LANES, aw), np.float32)
    for h in range(A_HEADS):
        expand[h, h * A_HEAD_DIM:(h + 1) * A_HEAD_DIM] = 1.0
    dils = [d for _, d in DIL_CFG]
    specs = ([pl.BlockSpec((1, d, tm // d, aw), lambda b, i: (b, 0, i, 0)) for d in dils]
             + [pl.BlockSpec((1, d, tm // d, LANES), lambda b, i: (b, 0, i, 0)) for d in dils]
             + [_const_spec(expand.shape)])
    arrays = [o for o, _ in branch_outs] + [l for _, l in branch_outs] + [jnp.asarray(expand, BF16)]
    scratch = ([pltpu.VMEM((aw // LANES, tm, LANES), F32) for _ in dils]
               + [pltpu.VMEM((tm, LANES), F32) for _ in dils])
    return arrays, specs, scratch


def _hgrn_consts(reverse):
    t = HGRN_SUB
    r = np.arange(t)
    u = r[None, :]
    row = r[:, None]
    nmats = [(u >= row) if reverse else (u <= row)]
    masks = []
    m = t // 2
    while m >= 1:
        grp = r // (2 * m)
        in_first = (r % (2 * m)) < m
        same = grp[:, None] == grp[None, :]
        if reverse:
            beta = (grp * 2 * m + m)[:, None]
            n = np.where(in_first[:, None], (u >= row) & (u < beta), (u >= beta) & (u < row))
            mask = same & in_first[:, None] & ~in_first[None, :]
        else:
            beta = (grp * 2 * m + m - 1)[:, None]
            n = np.where(in_first[:, None], (u > row) & (u <= beta), (u > beta) & (u <= row))
            mask = same & ~in_first[:, None] & in_first[None, :]
        if m < HGRN_BCAST_MIN:
            nmats.append(n)
        masks.append(mask)
        m //= 2
    masks.append(np.eye(t, dtype=bool))
    nmat = jnp.asarray(np.concatenate(nmats, axis=0), F32).astype(BF16)
    return nmat, jnp.asarray(np.stack(masks), F32)


def _hgrn_block(q, f, v, lb, st, nmat_ref, mask_ref, reverse):
    t = HGRN_SUB
    nlev = mask_ref.shape[0] - 1
    qs = q * (B_DK ** -0.5)
    fa = lb + (1.0 - lb) * _sigmoid(f)
    kk = 1.0 - fa
    g_hi, g_lo = _split2(jnp.log2(fa))
    ex = jnp.dot(nmat_ref[...], jnp.concatenate([g_hi, g_lo], axis=1), preferred_element_type=F32)
    ex = ex[:, :B_DK] + ex[:, B_DK:]
    b = ex[:t]
    btot = b[0:1] if reverse else b[t - 1:t]
    a = mask_ref[nlev] * _dot_nt(qs, kk)
    fine = 1
    for l in range(nlev):
        m = t >> (l + 1)
        if m >= HGRN_BCAST_MIN:
            ref = jnp.concatenate(
                [jnp.broadcast_to(b[beta:beta + 1], (2 * m, B_DK))
                 for beta in range(m if reverse else m - 1, t, 2 * m)], axis=0)
            e = jnp.exp2(-jnp.abs(b - ref))
        else:
            e = jnp.exp2(ex[fine * t:(fine + 1) * t])
            fine += 1
        a = a + mask_ref[l] * _dot_nt(qs * e, kk * e)
    out = _dot(a, v) + _dot_nt(qs * jnp.exp2(b), st)
    khat = (kk * jnp.exp2(btot - b)).astype(BF16)
    st_new = st * jnp.exp2(btot) + jnp.dot(v.T.astype(BF16), khat, preferred_element_type=F32)
    return out, st_new


def _hgrn_lb(lg_ref, layer):
    lg = [lg_ref[l, 0] for l in range(lg_ref.shape[0])]
    mx = functools.reduce(jnp.maximum, lg)
    e = [jnp.exp(x - mx) for x in lg]
    return functools.reduce(jnp.add, e[:layer + 1]) / functools.reduce(jnp.add, e)


def _hgrn_kernel(qf_ref, ff_ref, vf_ref, qb_ref, fb_ref, vb_ref, lgf_ref, lgb_ref,
                 nf_ref, mf_ref, nb_ref, mb_ref, of_ref, ob_ref, sf_sc, sb_sc, *, layer):
    @pl.when(pl.program_id(2) == 0)
    def _():
        sf_sc[...] = jnp.zeros_like(sf_sc)
        sb_sc[...] = jnp.zeros_like(sb_sc)

    nsub = HGRN_BLOCK // HGRN_SUB
    chains = ((qf_ref, ff_ref, vf_ref, lgf_ref, sf_sc, nf_ref, mf_ref, of_ref, False),
              (qb_ref, fb_ref, vb_ref, lgb_ref, sb_sc, nb_ref, mb_ref, ob_ref, True))
    for q_ref, f_ref, v_ref, lg_ref, st_sc, n_ref, m_ref, o_ref, reverse in chains:
        lb = _hgrn_lb(lg_ref, layer)
        st = st_sc[...]
        for sub in (reversed(range(nsub)) if reverse else range(nsub)):
            rows = pl.ds(sub * HGRN_SUB, HGRN_SUB)
            o, st = _hgrn_block(q_ref[0, rows, :], f_ref[0, rows, :], v_ref[0, rows, :], lb, st,
                                n_ref, m_ref, reverse)
            o_ref[0, rows, :] = o.astype(o_ref.dtype)
        st_sc[...] = st


def _hgrn(z, lb_logits, layer, col0):
    bsz, seq, _ = z.shape
    t = HGRN_BLOCK
    nb = seq // t
    c0 = col0 // LANES
    hw = B_HEADS

    def zspec(group, rev):
        return pl.BlockSpec(
            (1, t, LANES),
            lambda b, h, j: (b, (nb - 1 - j) if rev else j, c0 + group * hw + h))

    def lgspec(direction):
        return pl.BlockSpec((lb_logits.shape[0], 1, 1, LANES),
                            lambda b, h, j: (0, direction * hw + h, 0, 0))

    nf, mf = _hgrn_consts(False)
    nbw, mbw = _hgrn_consts(True)
    lg = lb_logits.astype(F32).reshape(lb_logits.shape[0], 2 * hw, 1, LANES)
    o_shape = jax.ShapeDtypeStruct((bsz, seq, hw * B_DV), BF16)
    return pl.pallas_call(
        functools.partial(_hgrn_kernel, layer=layer),
        out_shape=(o_shape, o_shape),
        grid=(bsz, hw, nb),
        in_specs=[zspec(0, False), zspec(1, False), zspec(3, False),
                  zspec(0, True), zspec(2, True), zspec(3, True),
                  lgspec(0), lgspec(1),
                  _const_spec(nf.shape), _const_spec(mf.shape),
                  _const_spec(nbw.shape), _const_spec(mbw.shape)],
        out_specs=(pl.BlockSpec((1, t, LANES), lambda b, h, j: (b, j, h)),
                   pl.BlockSpec((1, t, LANES), lambda b, h, j: (b, nb - 1 - j, h))),
        scratch_shapes=[pltpu.VMEM((B_DV, B_DK), F32), pltpu.VMEM((B_DV, B_DK), F32)],
        compiler_params=_cparams("parallel", "parallel", "arbitrary"),
        name="hgrn",
    )(z, z, z, z, z, z, lg, lg, nf, mf, nbw, mbw)


def _mix0_y(*refs):
    nbr = len(DIL_CFG)
    a_refs, l_refs, e_ref = refs[:nbr], refs[nbr:2 * nbr], refs[2 * nbr]
    of_ref, ob_ref, g_ref, on_ref, w_ref = refs[2 * nbr + 1:2 * nbr + 6]
    scratch = refs[2 * nbr + 6:]
    a = _merge_branches(a_refs, l_refs, e_ref, scratch[:nbr], scratch[nbr:])
    o = of_ref[0].astype(F32) + ob_ref[0].astype(F32)
    g = g_ref[0]
    parts = [_rms(o[:, h * B_DV:(h + 1) * B_DV], on_ref[...]) for h in range(B_HEADS)]
    bn = jnp.concatenate(parts, axis=-1) * (g * _sigmoid(g))
    na = a.shape[-1]
    return _dot(a, w_ref[:na, :]) + _dot(bn, w_ref[na:, :])


def _mix1_y(c_ref, d_ref, w_ref):
    c = jnp.concatenate([c_ref[0, g] for g in range(c_ref.shape[1])], axis=1)
    nc = c.shape[-1]
    return _dot(c, w_ref[:nc, :]) + _dot(d_ref[0], w_ref[nc:, :])


def _row_spec(tm, width, col=0):
    return pl.BlockSpec((1, tm, width), lambda b, i: (b, i, col))


def _tail_kernel(*refs, nmix, mix_fn, bounds):
    (x_ref, gm_ref, gatem_ref, g1_ref, sc_ref, sh_ref, wi_ref, wo_ref, g2_ref, gatef_ref,
     o_ref) = refs[nmix:nmix + 11]
    y_mix = mix_fn(*refs[:nmix], *refs[nmix + 11:])
    x1 = x_ref[0] + gatem_ref[0] * _rms(y_mix, gm_ref[...])
    h = (_rms(x1, g1_ref[...]) * (1.0 + sc_ref[0]) + sh_ref[0]).astype(BF16)
    hidden = wo_ref.shape[0]
    y = None
    for c0, c1 in zip(bounds, bounds[1:]):
        gt = jnp.dot(h, wi_ref[:, c0:c1], preferred_element_type=F32)
        up = jnp.dot(h, wi_ref[:, hidden + c0:hidden + c1], preferred_element_type=F32)
        part = _dot(gt * _sigmoid(gt) * up, wo_ref[c0:c1, :])
        y = part if y is None else y + part
    o_ref[0] = x1 + gatef_ref[0] * _rms(y, g2_ref[...])


def _layer_tail(mix_fn, mix_args, mix_specs, x, gain_m, gate_m, g1, sc, sh, wi_bf16, wo_bf16, layer,
                g2, gate_f, tm, name, mix_scratch=()):
    bsz, seq, d = x.shape
    vec = pl.BlockSpec((1, 1, d), lambda b, i: (b, 0, 0))
    hidden = wo_bf16.shape[1]

    def layer_slab(w):
        return pl.BlockSpec((None,) + w.shape[1:], lambda b, i: (layer, 0, 0),
                            pipeline_mode=pl.Buffered(1))

    ntile = hidden // MXU_WIDTH
    assert ntile * MXU_WIDTH == hidden
    bounds = (0, (ntile + 1) // 2 * MXU_WIDTH, hidden)
    row = lambda v: v.reshape(1, d)
    per_batch = lambda v: v.reshape(bsz, 1, d)
    return pl.pallas_call(
        functools.partial(_tail_kernel, nmix=len(mix_args), mix_fn=mix_fn, bounds=bounds),
        out_shape=jax.ShapeDtypeStruct(x.shape, F32),
        grid=(bsz, seq // tm),
        in_specs=list(mix_specs) + [_row_spec(tm, d), _const_spec((1, d)), vec, _const_spec((1, d)),
                                    vec, vec, layer_slab(wi_bf16), layer_slab(wo_bf16),
                                    _const_spec((1, d)), vec],
        out_specs=_row_spec(tm, d),
        scratch_shapes=list(mix_scratch),
        compiler_params=_cparams("parallel", "parallel"),
        name=name,
    )(*mix_args, x, row(gain_m), per_batch(gate_m), row(g1), per_batch(sc), per_batch(sh),
      wi_bf16, wo_bf16, row(g2), per_batch(gate_f))


def _tail0(branches, o_f, o_b, z, g_col, out_norm, w_bf16, tm, **kw):
    wv = B_HEADS * B_DV
    m_arrays, m_specs, m_scratch = _merge_operands(branches, tm)
    specs = m_specs + [_row_spec(tm, wv), _row_spec(tm, wv), _row_spec(tm, wv, g_col // wv),
                       _const_spec((1, B_DV)), _const_spec(w_bf16.shape)]
    args = m_arrays + [o_f, o_b, z, out_norm.reshape(1, B_DV), w_bf16]
    return _layer_tail(_mix0_y, args, specs, tm=tm, name="tail0", mix_scratch=m_scratch, **kw)


def _tail1(c_out, d_out, w_bf16, tm, **kw):
    specs = [pl.BlockSpec((1, c_out.shape[1], tm, c_out.shape[3]), lambda b, i: (b, 0, i, 0)),
             _row_spec(tm, d_out.shape[-1]), _const_spec(w_bf16.shape)]
    return _layer_tail(_mix1_y, (c_out, d_out, w_bf16), specs, tm=tm, name="tail1", **kw)


def _fft_kernel(u_ref, f1_ref, twc_ref, tws_ref, f2_ref, fw_ref, o_ref, u_sc, p_sc, y_sc,
                *, scale, n1, n2):
    pu = n2 + FFT_PAD
    pp = 2 * n1 + FFT_PAD
    py = n1 + FFT_PAD
    f1 = f1_ref[...].astype(BF16)
    f2 = f2_ref[...].astype(BF16)
    fw = fw_ref[...].astype(BF16)
    for i1 in range(n1):
        u_sc[i1 * pu:i1 * pu + n2, :] = u_ref[0, 0, i1 * n2:(i1 + 1) * n2, :]

    nb = FFT_BATCH

    def stage1(blk, carry):
        i2s = [blk * nb + j for j in range(nb)]
        x = jnp.concatenate([u_sc[pl.ds(i2, n1, stride=pu), :] for i2 in i2s], axis=1)
        p = jnp.dot(f1, x.astype(BF16), preferred_element_type=F32)
        for j, i2 in enumerate(i2s):
            p_sc[pl.ds(pl.multiple_of(i2 * pp, 8), 2 * n1), :] = p[:, j * C_WIDTH:(j + 1) * C_WIDTH]
        return carry

    lax.fori_loop(0, n2 // nb, stage1, 0, unroll=2)

    def stage2(blk, carry):
        k1s = [blk * nb + j for j in range(nb)]
        qr, qi = [], []
        for k1 in k1s:
            tc = twc_ref[k1]
            ts = tws_ref[k1]
            pr = p_sc[pl.ds(k1, n2, stride=pp), :]
            pim = p_sc[pl.ds(n1 + k1, n2, stride=pp), :]
            qr.append(pr * tc + pim * ts)
            qi.append(pim * tc - pr * ts)
        q = jnp.concatenate([jnp.concatenate(qr, axis=1), jnp.concatenate(qi, axis=1)], axis=0)
        xx = jnp.dot(f2, q.astype(BF16), preferred_element_type=F32)
        xg = jnp.concatenate(
            [jnp.concatenate([xx[:n2, j * C_WIDTH:(j + 1) * C_WIDTH],
                              xx[n2:, j * C_WIDTH:(j + 1) * C_WIDTH]], axis=1) for j in range(nb)],
            axis=0)
        y = jnp.dot(xg.astype(BF16), fw, preferred_element_type=F32) * scale
        for j, k1 in enumerate(k1s):
            y_sc[pl.ds(k1, n2, stride=py), :] = y[j * n2:(j + 1) * n2]
        return carry

    lax.fori_loop(0, n1 // nb, stage2, 0, unroll=2)
    for k2 in range(n2):
        o_ref[0, 0, k2 * n1:(k2 + 1) * n1, :] = y_sc[k2 * py:k2 * py + n1, :]


def _fourier_mixer(u):
    bsz, ngroups, seq, width = u.shape
    n2 = FFT_N2
    n1 = seq // n2
    assert n1 * n2 == seq and width == C_WIDTH and n1 % 8 == 0
    a1 = 2.0 * np.pi * np.outer(np.arange(n1), np.arange(n1)) / n1
    f1 = np.concatenate([np.cos(a1), -np.sin(a1)], axis=0)
    a2 = 2.0 * np.pi * np.outer(np.arange(n2), np.arange(n2)) / n2
    c2, s2 = np.cos(a2), np.sin(a2)
    f2 = np.block([[c2, s2], [-s2, c2]])
    aw = 2.0 * np.pi * np.outer(np.arange(C_WIDTH), np.arange(C_WIDTH)) / C_WIDTH
    fw = np.concatenate([np.cos(aw), np.sin(aw)], axis=0)
    at = np.repeat((2.0 * np.pi * np.outer(np.arange(n1), np.arange(n2)) / seq)[:, :, None],
                   C_WIDTH, axis=2)
    consts = (jnp.asarray(f1, F32), jnp.asarray(np.cos(at), F32), jnp.asarray(np.sin(at), F32),
              jnp.asarray(f2, F32), jnp.asarray(fw, F32))
    blk = pl.BlockSpec((1, 1, seq, C_WIDTH), lambda b, g: (b, g, 0, 0))
    return pl.pallas_call(
        functools.partial(_fft_kernel, scale=float(1.0 / np.sqrt(seq * C_WIDTH)), n1=n1, n2=n2),
        out_shape=jax.ShapeDtypeStruct(u.shape, F32),
        grid=(bsz, ngroups),
        in_specs=[blk] + [_const_spec(c.shape) for c in consts],
        out_specs=blk,
        scratch_shapes=[pltpu.VMEM((n1 * (n2 + FFT_PAD), C_WIDTH), F32),
                        pltpu.VMEM((n2 * (2 * n1 + FFT_PAD), C_WIDTH), F32),
                        pltpu.VMEM((n2 * (n1 + FFT_PAD), C_WIDTH), F32)],
        compiler_params=_cparams("parallel", "parallel"),
        name="fft",
    )(u, *consts)


def _head_perm():
    rep = D_Q_HEADS // D_KV_HEADS
    cols = []
    for j in range(rep):
        for g in range(D_KV_HEADS):
            h = g * rep + j
            cols.extend(range(h * D_HEAD_DIM, (h + 1) * D_HEAD_DIM))
    return np.asarray(cols, np.int32)


def _rope_tables(seq):
    rows = seq // GRID_W
    row = jnp.repeat(jnp.arange(rows, dtype=F32), GRID_W)
    col = jnp.tile(jnp.arange(GRID_W, dtype=F32), rows)
    axis_dim = D_HEAD_DIM // 2
    inv_freq = jnp.power(ROPE_THETA, -jnp.arange(0, axis_dim, 2, dtype=F32) / axis_dim)
    ang_r = row[:, None] * inv_freq[None, :]
    ang_c = col[:, None] * inv_freq[None, :]
    cr, sr, cc, sc = jnp.cos(ang_r), jnp.sin(ang_r), jnp.cos(ang_c), jnp.sin(ang_c)
    cos = jnp.concatenate([cr, cr, cc, cc], axis=1)
    sin = jnp.concatenate([-sr, sr, -sc, sc], axis=1)
    reps = LANES // D_HEAD_DIM
    return jnp.tile(cos, (1, reps)), jnp.tile(sin, (1, reps))


def _inproj_qk_kernel(x_ref, gain_ref, sc_ref, sh_ref, w_ref, cos_ref, sin_ref, bd_h_ref, bd_l_ref,
                      gq_ref, gk_ref, u_ref, qo_ref, ko_ref, vo_ref):
    h = _rms(x_ref[0], gain_ref[...]) * (1.0 + sc_ref[0]) + sh_ref[0]
    z = _dot(h, w_ref[...])
    ngrp = u_ref.shape[1]
    for g in range(ngrp):
        u_ref[0, g] = z[:, g * LANES:(g + 1) * LANES]
    cos = cos_ref[...]
    sin = sin_ref[...]
    quarter = D_HEAD_DIM // 4
    lane = lax.broadcasted_iota(jnp.int32, (1, LANES), 1)
    first_of_pair = (lane // quarter) % 2 == 0

    def norm_rope(x, gain, scale):
        ms = _dot_tab(bd_h_ref[...], bd_l_ref[...], x * x, tab_left=False)
        xn = x * lax.rsqrt(ms + EPS) * gain
        partner = jnp.where(first_of_pair, pltpu.roll(xn, LANES - quarter, 1),
                            pltpu.roll(xn, quarter, 1))
        return ((xn * cos + partner * sin) * scale).astype(BF16)

    nq = qo_ref.shape[-1] // LANES
    for j in range(nq):
        qo_ref[0, :, j * LANES:(j + 1) * LANES] = norm_rope(
            z[:, (ngrp + j) * LANES:(ngrp + j + 1) * LANES], gq_ref[...], D_HEAD_DIM ** -0.5 * LOG2E)
    ko_ref[0] = norm_rope(z[:, (ngrp + nq) * LANES:(ngrp + nq + 1) * LANES], gk_ref[...], 1.0)
    vt = z[:, (ngrp + nq + 1) * LANES:(ngrp + nq + 2) * LANES].T
    ones = jnp.ones((FLASH_ONES, vt.shape[1]), F32)
    vo_ref[0] = jnp.concatenate(
        [piece for g in range(D_KV_HEADS)
         for piece in (vt[g * D_HEAD_DIM:(g + 1) * D_HEAD_DIM], ones)], axis=0).astype(BF16)


def _inproj_qk(x, gain, sc, sh, w_bf16, qk_norm_j, tm):
    bsz, seq, d = x.shape
    n = w_bf16.shape[1]
    qw = D_Q_HEADS * D_HEAD_DIM
    kw = D_KV_HEADS * D_HEAD_DIM
    assert kw == LANES and n == C_GROUPS * C_WIDTH + qw + 2 * kw
    vrows = D_KV_HEADS * (D_HEAD_DIM + FLASH_ONES)
    cos, sin = _rope_tables(seq)
    bd = np.kron(np.eye(LANES // D_HEAD_DIM), np.full((D_HEAD_DIM, D_HEAD_DIM), 1.0 / D_HEAD_DIM))
    bd_h, bd_l = _np_split2(bd)
    reps = LANES // D_HEAD_DIM
    gq = jnp.tile(qk_norm_j[0].astype(F32), reps).reshape(1, LANES)
    gk = jnp.tile(qk_norm_j[1].astype(F32), reps).reshape(1, LANES)
    tab = pl.BlockSpec((tm, LANES), lambda b, i: (i, 0))
    vec = pl.BlockSpec((1, 1, d), lambda b, i: (b, 0, 0))
    return pl.pallas_call(
        _inproj_qk_kernel,
        out_shape=(jax.ShapeDtypeStruct((bsz, C_GROUPS, seq, C_WIDTH), F32),
                   jax.ShapeDtypeStruct((bsz, seq, qw), BF16),
                   jax.ShapeDtypeStruct((bsz, seq, kw), BF16),
                   jax.ShapeDtypeStruct((bsz, vrows, seq), BF16)),
        grid=(bsz, seq // tm),
        in_specs=[pl.BlockSpec((1, tm, d), lambda b, i: (b, i, 0)),
                  _const_spec((1, d)), vec, vec, _const_spec((d, n)),
                  tab, tab, _const_spec(bd_h.shape), _const_spec(bd_l.shape),
                  _const_spec((1, LANES)), _const_spec((1, LANES))],
        out_specs=(pl.BlockSpec((1, C_GROUPS, tm, C_WIDTH), lambda b, i: (b, 0, i, 0)),
                   _row_spec(tm, qw), _row_spec(tm, kw),
                   pl.BlockSpec((1, vrows, tm), lambda b, i: (b, 0, i))),
        compiler_params=_cparams("parallel", "parallel"),
        name="inproj_qk",
    )(x, gain.reshape(1, d), sc.reshape(bsz, 1, d), sh.reshape(bsz, 1, d), w_bf16,
      cos, sin, bd_h, bd_l, gq, gk)


def _flash_kernel(q_ref, k_ref, vt_ref, o_ref, m_sc, acc_sc, s_sc):
    kv = pl.program_id(2)

    @pl.when(kv == 0)
    def _():
        m_sc[...] = jnp.full_like(m_sc, -jnp.inf)
        acc_sc[...] = jnp.zeros_like(acc_sc)

    lane = lax.broadcasted_iota(jnp.int32, (1, LANES), 1)
    lo = lane < D_HEAD_DIM
    nblk = q_ref.shape[-1] // LANES
    tq, tk = q_ref.shape[1], k_ref.shape[1]
    ku, qu = FLASH_KEY_UNIT, FLASH_QUERY_UNIT
    grows = D_HEAD_DIM + FLASH_ONES
    nheads = nblk * D_KV_HEADS
    k = k_ref[0]

    nchunk = tq // qu

    def logits_chunk(idx, c):
        j, g = divmod(idx, D_KV_HEADS)
        qj = q_ref[0, c * qu:(c + 1) * qu, j * LANES:(j + 1) * LANES]
        sel = lo if g == 0 else jnp.logical_not(lo)
        s = lax.dot_general(k, jnp.where(sel, qj, jnp.zeros_like(qj)), (((1,), (1,)), ((), ())),
                            preferred_element_type=F32)
        s_sc[idx, :, c * qu:(c + 1) * qu] = s
        return jnp.max(s, axis=0, keepdims=True)

    def finish_logits(idx, mcs):
        m_prev = m_sc[idx, 0:1, :]
        m_new = jnp.maximum(m_prev, jnp.concatenate(mcs, axis=1))
        m_sc[idx, 0:1, :] = m_new
        return m_new, jnp.exp2(m_prev - m_new)

    def value_chunk(idx, c, m_new, alpha):
        j, g = divmod(idx, D_KV_HEADS)
        rows = slice(g * grows, (g + 1) * grows)
        qcols = slice(c * qu, (c + 1) * qu)
        pv = None
        for u in range(tk // ku):
            keys = slice(u * ku, (u + 1) * ku)
            p = jnp.exp2(s_sc[idx, keys, qcols] - m_new[:, qcols])
            d = jnp.dot(vt_ref[0, rows, keys], p.astype(BF16), preferred_element_type=F32)
            pv = d if pv is None else pv + d
        acc_sc[j, rows, qcols] = alpha[:, qcols] * acc_sc[j, rows, qcols] + pv

    def logits_pass(idx):
        return finish_logits(idx, [logits_chunk(idx, c) for c in range(nchunk)])

    pending = [logits_pass(i) for i in range(min(FLASH_AHEAD, nheads))]
    for idx in range(nheads):
        if idx + FLASH_AHEAD < nheads:
            pending.append(logits_pass(idx + FLASH_AHEAD))
        stats = pending.pop(0)
        for c in range(nchunk):
            value_chunk(idx, c, *stats)

    @pl.when(kv == pl.num_programs(2) - 1)
    def _():
        for j in range(nblk):
            parts = []
            for g in range(D_KV_HEADS):
                num = acc_sc[j, g * grows:g * grows + D_HEAD_DIM, :]
                den = acc_sc[j, g * grows + D_HEAD_DIM:g * grows + D_HEAD_DIM + 1, :]
                parts.append(num / den)
            o_ref[0, :, j * LANES:(j + 1) * LANES] = (
                jnp.concatenate(parts, axis=0).T.astype(o_ref.dtype))


def _flash(q, k, vt, tq, tk):
    bsz, seq, qw = q.shape
    kw = k.shape[-1]
    vrows = vt.shape[1]
    return pl.pallas_call(
        _flash_kernel,
        out_shape=jax.ShapeDtypeStruct((bsz, seq, qw), BF16),
        grid=(bsz, seq // tq, seq // tk),
        in_specs=[pl.BlockSpec((1, tq, qw), lambda b, i, j: (b, i, 0)),
                  pl.BlockSpec((1, tk, kw), lambda b, i, j: (b, j, 0)),
                  pl.BlockSpec((1, vrows, tk), lambda b, i, j: (b, 0, j))],
        out_specs=pl.BlockSpec((1, tq, qw), lambda b, i, j: (b, i, 0)),
        scratch_shapes=[pltpu.VMEM((D_Q_HEADS, 8, tq), F32),
                        pltpu.VMEM((qw // LANES, vrows, tq), F32),
                        pltpu.VMEM((D_Q_HEADS, tk, tq), F32)],
        compiler_params=_cparams("parallel", "parallel", "arbitrary"),
        name="flash",
    )(q, k, vt)


def kernel(x, c, t5_bias, hgrn_lb_logits, ada_w, ada_b, norm_gains, ab_w_in, ab_w_out,
           hgrn_out_norm, cd_w_in, cd_w_out, qk_norm, ffn_w_in, ffn_w_out):
    bsz, seq, d = x.shape
    depth = ada_w.shape[0]
    mod = _ada_mod(c.astype(F32), ada_w, ada_b)
    perm = _head_perm()
    aw = A_HEADS * A_HEAD_DIM
    cw = C_GROUPS * C_WIDTH
    qw = D_Q_HEADS * D_HEAD_DIM
    tm_in = min(512, seq)
    tm = min(512, seq)
    ffn_wi = ffn_w_in.astype(BF16)
    ffn_wo = ffn_w_out.astype(BF16)
    for layer in range(depth):
        sh_m, sc_m, g_m, sh_f, sc_f, g_f = [mod[layer, :, i * d:(i + 1) * d] for i in range(6)]
        gains = norm_gains[layer]
        j = layer // 2
        tail = dict(x=x, gain_m=gains[1], gate_m=g_m, g1=gains[2], sc=sc_f, sh=sh_f,
                    wi_bf16=ffn_wi, wo_bf16=ffn_wo, layer=layer, g2=gains[3], gate_f=g_f, tm=tm)
        if layer % 2 == 0:
            w_in = ab_w_in[j].astype(BF16)
            *qkv_cm, z = _inproj_cm(x, gains[0], sc_m, sh_m, w_in, 3 * aw, tm_in)
            branches = [_dilated_branch(cm, t5_bias, window, dil)
                        for cm, (window, dil) in zip(qkv_cm, DIL_CFG)]
            o_f, o_b = _hgrn(z, hgrn_lb_logits, layer, 0)
            g_col = 3 * B_HEADS * B_DK + B_HEADS * B_DV
            x = _tail0(branches, o_f, o_b, z, g_col, hgrn_out_norm[j], ab_w_out[j].astype(BF16), **tail)
        else:
            w_full = cd_w_in[j]
            w_in = jnp.concatenate([w_full[:, :cw], w_full[:, cw:cw + qw][:, perm],
                                    w_full[:, cw + qw:]], axis=1).astype(BF16)
            u, qn, kn, vn = _inproj_qk(x, gains[0], sc_m, sh_m, w_in, qk_norm[j], min(1024, seq))
            c_out = _fourier_mixer(u)
            d_out = _flash(qn, kn, vn, min(FLASH_TQ, seq), min(FLASH_TK, seq))
            w_out_full = cd_w_out[j]
            w_out = jnp.concatenate([w_out_full[:cw], w_out_full[cw:][perm]], axis=0).astype(BF16)
            x = _tail1(c_out, d_out, w_out, **tail)
    return x
```

```python
import functools

import numpy as np
import jax
import jax.numpy as jnp
from jax import lax
from jax.experimental import pallas as pl
from jax.experimental.pallas import tpu as pltpu

F32 = jnp.float32
BF16 = jnp.bfloat16
LANES = 128
MXU_WIDTH = 256
VMEM_LIMIT_BYTES = 56 * 2**20
NEG_INF = -1e30
EPS = 1e-6

GRID_W = 64
A_HEADS = 8
A_HEAD_DIM = 64
DIL_CFG = ((128, 1), (512, 4), (2048, 16))
N_BUCKETS = 32
T5_MAX_DIST = 1024
B_HEADS = 4
B_DK = 128
B_DV = 128
C_GROUPS = 4
C_WIDTH = 128
D_Q_HEADS = 8
D_KV_HEADS = 2
D_HEAD_DIM = 64
ROPE_THETA = 10000.0

DIL_TQ = 128
DIL_TILE = 512
HGRN_BLOCK = 1024
HGRN_SUB = 256
HGRN_BCAST_MIN = 8
FLASH_TQ = 1024
FLASH_TK = 1024
FLASH_KEY_UNIT = 256
FLASH_QUERY_UNIT = 1024
FLASH_AHEAD = 1
FLASH_ONES = 16
TAIL_BLOCK = 512
TAIL_SUB = 512
FFT_N2 = 128
FFT_BATCH = 8
FFT_PAD = 8
LOG2E = 1.4426950408889634


def _cparams(*sem):
    return pltpu.CompilerParams(dimension_semantics=sem, vmem_limit_bytes=VMEM_LIMIT_BYTES)


def _const_spec(shape):
    nd = len(shape)
    return pl.BlockSpec(shape, lambda *_: (0,) * nd, pipeline_mode=pl.Buffered(1))


def _sigmoid(x):
    return 1.0 / (1.0 + jnp.exp(-x))


def _dot(a, b):
    return jnp.dot(a.astype(BF16), b.astype(BF16), preferred_element_type=F32)


def _dot_nt(a, b):
    return lax.dot_general(a.astype(BF16), b.astype(BF16), (((1,), (1,)), ((), ())),
                           preferred_element_type=F32)


def _split2(a):
    hi = a.astype(BF16)
    lo = (a - hi.astype(F32)).astype(BF16)
    return hi, lo


def _split3(a):
    a1 = a.astype(BF16)
    r = a - a1.astype(F32)
    a2 = r.astype(BF16)
    a3 = (r - a2.astype(F32)).astype(BF16)
    return a1, a2, a3


def _dot_tab(tab_hi, tab_lo, x, *, tab_left):
    x_hi, x_lo = _split2(x)
    if tab_left:
        d = lambda t, v: jnp.dot(t, v, preferred_element_type=F32)
    else:
        d = lambda t, v: jnp.dot(v, t, preferred_element_type=F32)
    return d(tab_hi, x_hi) + (d(tab_hi, x_lo) + d(tab_lo, x_hi))


def _rms(x, gain):
    ms = jnp.mean(x * x, axis=-1, keepdims=True)
    return x * lax.rsqrt(ms + EPS) * gain


def _np_split2(t):
    t = np.asarray(t, np.float32)
    hi = jnp.asarray(t, F32).astype(BF16)
    lo = (jnp.asarray(t, F32) - hi.astype(F32)).astype(BF16)
    return hi, lo


def _mod_kernel(c_ref, w_ref, b_ref, o_ref):
    c = c_ref[...]
    o_ref[0] = _dot(c * _sigmoid(c), w_ref[0]) + b_ref[0]


def _ada_mod(c, ada_w, ada_b):
    depth, d, n6 = ada_w.shape
    bsz = c.shape[0]
    rows = 8
    cp = jnp.zeros((rows, d), F32).at[:bsz].set(c)
    tn = n6 // 4
    out = pl.pallas_call(
        _mod_kernel,
        out_shape=jax.ShapeDtypeStruct((depth, rows, n6), F32),
        grid=(depth, n6 // tn),
        in_specs=[pl.BlockSpec((rows, d), lambda l, j: (0, 0)),
                  pl.BlockSpec((1, d, tn), lambda l, j: (l, 0, j)),
                  pl.BlockSpec((1, 1, tn), lambda l, j: (l, 0, j))],
        out_specs=pl.BlockSpec((1, rows, tn), lambda l, j: (l, 0, j)),
        compiler_params=_cparams("parallel", "parallel"),
        name="ada_mod",
    )(cp, ada_w, ada_b.reshape(depth, 1, n6))
    return out[:, :bsz]


def _inproj_cm_kernel(x_ref, gain_ref, sc_ref, sh_ref, w_ref, *refs):
    cm_refs, rest_ref, zs_sc, zc_sc = refs[:-3], refs[-3], refs[-2], refs[-1]
    h = _rms(x_ref[0], gain_ref[...]) * (1.0 + sc_ref[0]) + sh_ref[0]
    z = _dot(h, w_ref[...])
    nblk, tm, _ = zs_sc.shape
    rest_ref[0] = z[:, nblk * LANES:]
    nq = A_HEADS * A_HEAD_DIM // LANES
    for c in range(nblk):
        blk = z[:, c * LANES:(c + 1) * LANES]
        zs_sc[c] = blk * (A_HEAD_DIM ** -0.5 * LOG2E) if c < nq else blk
    src, sd = zs_sc, 1
    for level, (cm_ref, (_, dil)) in enumerate(zip(cm_refs, DIL_CFG)):
        step, n = dil // sd, tm // dil
        keep = dil > 1 and level + 1 < len(DIL_CFG)
        for rs in range(sd):
            for cc in range(step):
                r = rs + sd * cc
                for c in range(nblk):
                    rows = src[c, pl.ds(rs * (tm // sd) + cc, n, stride=step), :]
                    cm_ref[0, r, :, c * LANES:(c + 1) * LANES] = rows.astype(BF16)
                    if keep:
                        zc_sc[c, r * n:(r + 1) * n, :] = rows
        if keep:
            src, sd = zc_sc, dil


def _inproj_cm(x, gain, sc, sh, w_bf16, na, tm):
    bsz, seq, d = x.shape
    n = w_bf16.shape[1]
    vec = pl.BlockSpec((1, 1, d), lambda b, i: (b, 0, 0))
    dils = [dl for _, dl in DIL_CFG]
    assert dils[0] == 1 and all(b % a == 0 for a, b in zip(dils, dils[1:]))
    return pl.pallas_call(
        _inproj_cm_kernel,
        out_shape=tuple([jax.ShapeDtypeStruct((bsz, dl, seq // dl, na), BF16) for dl in dils]
                        + [jax.ShapeDtypeStruct((bsz, seq, n - na), F32)]),
        grid=(bsz, seq // tm),
        in_specs=[pl.BlockSpec((1, tm, d), lambda b, i: (b, i, 0)),
                  _const_spec((1, d)), vec, vec, _const_spec((d, n))],
        out_specs=tuple([pl.BlockSpec((1, dl, tm // dl, na), lambda b, i: (b, 0, i, 0)) for dl in dils]
                        + [pl.BlockSpec((1, tm, n - na), lambda b, i: (b, i, 0))]),
        scratch_shapes=[pltpu.VMEM((na // LANES, tm, LANES), F32),
                        pltpu.VMEM((na // LANES, tm, LANES), F32)],
        compiler_params=_cparams("parallel", "parallel"),
        name="inproj_cm",
    )(x, gain.reshape(1, d), sc.reshape(bsz, 1, d), sh.reshape(bsz, 1, d), w_bf16)


def _t5_buckets(rel):
    half = N_BUCKETS // 2
    max_exact = half // 2
    n = np.abs(rel)
    large = max_exact + (np.log(np.maximum(n, 1) / max_exact) / np.log(T5_MAX_DIST / max_exact)
                         * (half - max_exact)).astype(np.int32)
    large = np.minimum(large, half - 1)
    return (np.where(rel > 0, half, 0) + np.where(n < max_exact, n, large)).astype(np.int32)


def _dil_bias(t5_bias, window, dil, tq):
    half = (window // 2) // dil
    assert half == tq // 2
    rel = np.arange(2 * tq)[None, :] - half - np.arange(tq)[:, None]
    inside = np.abs(rel) <= half
    buckets = _t5_buckets(np.where(inside, rel, 0) * dil)
    onehot =jnp.asarray(np.eye(N_BUCKETS, dtype=np.float32)[buckets])
    bias = jnp.einsum("qkn,nh->hqk", onehot, t5_bias.astype(F32), precision=lax.Precision.HIGHEST)
    return jnp.where(jnp.asarray(inside)[None], bias * LOG2E, NEG_INF)


def _dil_kernel(q_ref, kp_ref, kc_ref, kn_ref, vp_ref, vc_ref, vn_ref, bias_ref, o_ref, lse_ref,
                *, class_len):
    i = pl.program_id(2)
    sub, hq, tile = DIL_TQ, DIL_TQ // 2, q_ref.shape[2]
    kwin = jnp.concatenate([kp_ref[0, 0], kc_ref[0, 0], kn_ref[0, 0]], axis=0)
    vwin = jnp.concatenate([vp_ref[0, 0], vc_ref[0, 0], vn_ref[0, 0]], axis=0)
    lane = lax.broadcasted_iota(jnp.int32, (1, LANES), 1)
    lo = lane < A_HEAD_DIM
    nblk = A_HEADS // 2
    units = [(jt, j) for jt in range(tile // sub) for j in range(nblk)]
    logits = []
    for jt, j in units:
        cols = slice(j * LANES, (j + 1) * LANES)
        qj = q_ref[0, 0, jt * sub:(jt + 1) * sub, cols]
        zero = jnp.zeros_like(qj)
        q2 = jnp.concatenate([jnp.where(lo, qj, zero), jnp.where(lo, zero, qj)], axis=0)
        s = lax.dot_general(q2, kwin[jt * sub:jt * sub + 2 * sub, cols], (((1,), (1,)), ((), ())),
                            preferred_element_type=F32)
        kpos = i * tile + jt * sub - hq + lax.broadcasted_iota(jnp.int32, (1, 2 * sub), 1)
        valid = jnp.logical_and(kpos >= 0, kpos < class_len)
        logits.append(jnp.where(valid, s + bias_ref[j], NEG_INF))
    s_all = jnp.concatenate(logits, axis=0)
    m = jnp.max(s_all, axis=-1, keepdims=True)
    p32 = jnp.exp2(s_all - m)
    l = jnp.sum(p32, axis=-1, keepdims=True)
    p = p32.astype(BF16)
    rinv = 1.0 / l
    lse = m + jnp.log2(l)
    for jt in range(tile // sub):
        lse_all = jnp.zeros((sub, LANES), F32)
        for j in range(nblk):
            cols = slice(j * LANES, (j + 1) * LANES)
            r0 = (jt * nblk + j) * 2 * sub
            o2 = jnp.dot(p[r0:r0 + 2 * sub], vwin[jt * sub:jt * sub + 2 * sub, cols],
                         preferred_element_type=F32) * rinv[r0:r0 + 2 * sub]
            lse_all = jnp.where(lane == 2 * j, lse[r0:r0 + sub], lse_all)
            lse_all = jnp.where(lane == 2 * j + 1, lse[r0 + sub:r0 + 2 * sub], lse_all)
            o_ref[0, 0, jt * sub:(jt + 1) * sub, cols] = (
                jnp.where(lo, o2[:sub], o2[sub:]).astype(o_ref.dtype))
        lse_ref[0, 0, jt * sub:(jt + 1) * sub, :] = lse_all


def _dilated_branch(qkv_cm, t5_bias, window, dil):
    bsz, _, cl, width = qkv_cm.shape
    aw = A_HEADS * A_HEAD_DIM
    tile, hq = min(DIL_TILE, cl), DIL_TQ // 2
    nt = cl // tile
    per = tile // hq
    nh = cl // hq

    def cur(col):
        return pl.BlockSpec((1, 1, tile, aw), lambda b, r, i: (b, r, i, col))

    def prev(col):
        return pl.BlockSpec((1, 1, hq, aw), lambda b, r, i: (b, r, jnp.maximum(i * per - 1, 0), col))

    def nxt(col):
        return pl.BlockSpec((1, 1, hq, aw),
                            lambda b, r, i: (b, r, jnp.minimum((i + 1) * per, nh - 1), col))

    return pl.pallas_call(
        functools.partial(_dil_kernel, class_len=cl),
        out_shape=(jax.ShapeDtypeStruct((bsz, dil, cl, aw), BF16),
                   jax.ShapeDtypeStruct((bsz, dil, cl, LANES), F32)),
        grid=(bsz, dil, nt),
        in_specs=[cur(0), prev(1), cur(1), nxt(1), prev(2), cur(2), nxt(2),
                  _const_spec((A_HEADS // 2, 2 * DIL_TQ, 2 * DIL_TQ))],
        out_specs=(pl.BlockSpec((1, 1, tile, aw), lambda b, r, i: (b, r, i, 0)),
                   pl.BlockSpec((1, 1, tile, LANES), lambda b, r, i: (b, r, i, 0))),
        compiler_params=_cparams("parallel", "parallel", "parallel"),
        name=f"dilated_d{dil}",
    )(*([qkv_cm] * 7),
      _dil_bias(t5_bias, window, dil, DIL_TQ).reshape(A_HEADS // 2, 2 * DIL_TQ, 2 * DIL_TQ))


def _merge_branches(a_refs, l_refs, e_ref, a_scs, l_scs, sub):
    tm = l_scs[0].shape[0]
    accs, lses = [], []
    for (_, dil), a_ref, l_ref, a_sc, l_sc in zip(DIL_CFG, a_refs, l_refs, a_scs, l_scs):
        nblk = a_sc.shape[0]
        for r in range(dil):
            rows = pl.ds(r, tm // dil, stride=dil)
            src = pl.ds(sub * (tm // dil), tm // dil)
            for c in range(nblk):
                a_sc[c, rows, :] = a_ref[0, r, src, c * LANES:(c + 1) * LANES].astype(F32)
            l_sc[rows, :] = l_ref[0, r, src, :]
        accs.append(jnp.concatenate([a_sc[c] for c in range(nblk)], axis=1))
        lses.append(l_sc[...])
    mx = functools.reduce(jnp.maximum, lses)
    ws = [jnp.exp2(x - mx) for x in lses]
    tot = functools.reduce(jnp.add, ws)
    out = None
    for w, a in zip(ws, accs):
        w_hi, w_lo = _split2(w / tot)
        wide = (jnp.dot(w_hi, e_ref[...], preferred_element_type=F32)
                + jnp.dot(w_lo, e_ref[...], preferred_element_type=F32))
        out = wide * a if out is None else out + wide * a
    return out


def _merge_operands(branch_outs, tm):
    aw = branch_outs[0][0].shape[-1]
    expand = np.zeros((LANES, aw), np.float32)
    for h in range(A_HEADS):
        expand[h, h * A_HEAD_DIM:(h + 1) * A_HEAD_DIM] = 1.0
    dils = [d for _, d in DIL_CFG]
    specs = ([pl.BlockSpec((1, d, tm // d, aw), lambda b, i: (b, 0, i, 0)) for d in dils]
             + [pl.BlockSpec((1, d, tm // d, LANES), lambda b, i: (b, 0, i, 0)) for d in dils]
             + [_const_spec(expand.shape)])
    arrays = [o for o, _ in branch_outs] + [l for _, l in branch_outs] + [jnp.asarray(expand, BF16)]
    scratch = ([pltpu.VMEM((aw // LANES, TAIL_SUB, LANES), F32) for _ in dils]
               + [pltpu.VMEM((TAIL_SUB, LANES), F32) for _ in dils])
    return arrays, specs, scratch


def _hgrn_consts(reverse):
    t = HGRN_SUB
    r = np.arange(t)
    u = r[None, :]
    row = r[:, None]
    nmats = [(u >= row) if reverse else (u <= row)]
    masks = []
    m = t // 2
    while m >= 1:
        grp = r // (2 * m)
        in_first = (r % (2 * m)) < m
        same = grp[:, None] == grp[None, :]
        if reverse:
            beta = (grp * 2 * m + m)[:, None]
            n = np.where(in_first[:, None], (u >= row) & (u < beta), (u >= beta) & (u < row))
            mask = same & in_first[:, None] & ~in_first[None, :]
        else:
            beta = (grp * 2 * m + m - 1)[:, None]
            n = np.where(in_first[:, None], (u > row) & (u <= beta), (u > beta) & (u <= row))
            mask = same & ~in_first[:, None] & in_first[None, :]
        if m < HGRN_BCAST_MIN:
            nmats.append(n)
        masks.append(mask)
        m //= 2
    masks.append(np.eye(t, dtype=bool))
    nmat = jnp.asarray(np.concatenate(nmats, axis=0), F32).astype(BF16)
    return nmat, jnp.asarray(np.stack(masks), F32)


def _hgrn_block(q, f, v, lb, st, nmat_ref, mask_ref, reverse):
    t = HGRN_SUB
    nlev = mask_ref.shape[0] - 1
    qs = q * (B_DK ** -0.5)
    fa = lb + (1.0 - lb) * _sigmoid(f)
    kk = 1.0 - fa
    g_hi, g_lo = _split2(jnp.log2(fa))
    ex = jnp.dot(nmat_ref[...], jnp.concatenate([g_hi, g_lo], axis=1), preferred_element_type=F32)
    ex = ex[:, :B_DK] + ex[:, B_DK:]
    b = ex[:t]
    btot = b[0:1] if reverse else b[t - 1:t]
    a = mask_ref[nlev] * _dot_nt(qs, kk)
    fine = 1
    for l in range(nlev):
        m = t >> (l + 1)
        if m >= HGRN_BCAST_MIN:
            ref = jnp.concatenate(
                [jnp.broadcast_to(b[beta:beta + 1], (2 * m, B_DK))
                 for beta in range(m if reverse else m - 1, t, 2 * m)], axis=0)
            e = jnp.exp2(-jnp.abs(b - ref))
        else:
            e = jnp.exp2(ex[fine * t:(fine + 1) * t])
            fine += 1
        a = a + mask_ref[l] * _dot_nt(qs * e, kk * e)
    out = _dot(a, v) + _dot_nt(qs * jnp.exp2(b), st)
    khat = (kk * jnp.exp2(btot - b)).astype(BF16)
    st_new = st * jnp.exp2(btot) + jnp.dot(v.T.astype(BF16), khat, preferred_element_type=F32)
    return out, st_new


def _hgrn_lb(lg_ref, layer):
    lg = [lg_ref[l, 0] for l in range(lg_ref.shape[0])]
    mx = functools.reduce(jnp.maximum, lg)
    e = [jnp.exp(x - mx) for x in lg]
    return functools.reduce(jnp.add, e[:layer + 1]) / functools.reduce(jnp.add, e)


def _hgrn_kernel(qf_ref, ff_ref, vf_ref, qb_ref, fb_ref, vb_ref, lgf_ref, lgb_ref,
                 nf_ref, mf_ref, nb_ref, mb_ref, of_ref, ob_ref, sf_sc, sb_sc, *, layer):
    @pl.when(pl.program_id(2) == 0)
    def _():
        sf_sc[...] = jnp.zeros_like(sf_sc)
        sb_sc[...] = jnp.zeros_like(sb_sc)

    nsub = HGRN_BLOCK // HGRN_SUB
    chains = ((qf_ref, ff_ref, vf_ref, lgf_ref, sf_sc, nf_ref, mf_ref, of_ref, False),
              (qb_ref, fb_ref, vb_ref, lgb_ref, sb_sc, nb_ref, mb_ref, ob_ref, True))
    for q_ref, f_ref, v_ref, lg_ref, st_sc, n_ref, m_ref, o_ref, reverse in chains:
        lb = _hgrn_lb(lg_ref, layer)
        st = st_sc[...]
        for sub in (reversed(range(nsub)) if reverse else range(nsub)):
            rows = pl.ds(sub * HGRN_SUB, HGRN_SUB)
            o, st = _hgrn_block(q_ref[0, rows, :], f_ref[0, rows, :], v_ref[0, rows, :], lb, st,
                                n_ref, m_ref, reverse)
            o_ref[0, rows, :] = o.astype(o_ref.dtype)
        st_sc[...] = st


def _hgrn(z, lb_logits, layer, col0):
    bsz, seq, _ = z.shape
    t = HGRN_BLOCK
    nb = seq // t
    c0 = col0 // LANES
    hw = B_HEADS

    def zspec(group, rev):
        return pl.BlockSpec(
            (1, t, LANES),
            lambda b, h, j: (b, (nb - 1 - j) if rev else j, c0 + group * hw + h))

    def lgspec(direction):
        return pl.BlockSpec((lb_logits.shape[0], 1, 1, LANES),
                            lambda b, h, j: (0, direction * hw + h, 0, 0))

    nf, mf = _hgrn_consts(False)
    nbw, mbw = _hgrn_consts(True)
    lg = lb_logits.astype(F32).reshape(lb_logits.shape[0], 2 * hw, 1, LANES)
    o_shape = jax.ShapeDtypeStruct((bsz, seq, hw * B_DV), BF16)
    return pl.pallas_call(
        functools.partial(_hgrn_kernel, layer=layer),
        out_shape=(o_shape, o_shape),
        grid=(bsz, hw, nb),
        in_specs=[zspec(0, False), zspec(1, False), zspec(3, False),
                  zspec(0, True), zspec(2, True), zspec(3, True),
                  lgspec(0), lgspec(1),
                  _const_spec(nf.shape), _const_spec(mf.shape),
                  _const_spec(nbw.shape), _const_spec(mbw.shape)],
        out_specs=(pl.BlockSpec((1, t, LANES), lambda b, h, j: (b, j, h)),
                   pl.BlockSpec((1, t, LANES), lambda b, h, j: (b, nb - 1 - j, h))),
        scratch_shapes=[pltpu.VMEM((B_DV, B_DK), F32), pltpu.VMEM((B_DV, B_DK), F32)],
        compiler_params=_cparams("parallel", "parallel", "arbitrary"),
        name="hgrn",
    )(z, z, z, z, z, z, lg, lg, nf, mf, nbw, mbw)


def _mix0_y(sub, *refs):
    nbr = len(DIL_CFG)
    a_refs, l_refs, e_ref = refs[:nbr], refs[nbr:2 * nbr], refs[2 * nbr]
    of_ref, ob_ref, g_ref, on_ref, w_ref = refs[2 * nbr + 1:2 * nbr + 6]
    scratch = refs[2 * nbr + 6:]
    a = _merge_branches(a_refs, l_refs, e_ref, scratch[:nbr], scratch[nbr:], sub)
    rows = pl.ds(sub * TAIL_SUB, TAIL_SUB)
    o = of_ref[0, rows, :].astype(F32) + ob_ref[0, rows, :].astype(F32)
    g = g_ref[0, rows, :]
    parts = [_rms(o[:, h * B_DV:(h + 1) * B_DV], on_ref[...]) for h in range(B_HEADS)]
    bn = jnp.concatenate(parts, axis=-1) * (g * _sigmoid(g))
    na = a.shape[-1]
    return _dot(a, w_ref[:na, :]) + _dot(bn, w_ref[na:, :])


def _mix1_y(sub, c_ref, d_ref, w_ref):
    rows = pl.ds(sub * TAIL_SUB, TAIL_SUB)
    c = jnp.concatenate([c_ref[0, g, rows, :] for g in range(c_ref.shape[1])], axis=1)
    nc = c.shape[-1]
    return _dot(c, w_ref[:nc, :]) + _dot(d_ref[0, rows, :], w_ref[nc:, :])


def _row_spec(tm, width, col=0):
    return pl.BlockSpec((1, tm, width), lambda b, i: (b, i, col))


def _tail_kernel(*refs, nmix, mix_fn, bounds):
    (x_ref, gm_ref, gatem_ref, g1_ref, sc_ref, sh_ref, wi_ref, wo_ref, g2_ref, gatef_ref,
     o_ref) = refs[nmix:nmix + 11]
    hidden = wo_ref.shape[0]
    for sub in range(x_ref.shape[1] // TAIL_SUB):
        rows = pl.ds(sub * TAIL_SUB, TAIL_SUB)
        y_mix = mix_fn(sub, *refs[:nmix], *refs[nmix + 11:])
        x1 = x_ref[0, rows, :] + gatem_ref[0] * _rms(y_mix, gm_ref[...])
        h = (_rms(x1, g1_ref[...]) * (1.0 + sc_ref[0]) + sh_ref[0]).astype(BF16)
        y = None
        for c0, c1 in zip(bounds, bounds[1:]):
            gt = jnp.dot(h, wi_ref[:, c0:c1], preferred_element_type=F32)
            up = jnp.dot(h, wi_ref[:, hidden + c0:hidden + c1], preferred_element_type=F32)
            part = _dot(gt * _sigmoid(gt) * up, wo_ref[c0:c1, :])
            y = part if y is None else y + part
        o_ref[0, rows, :] = x1 + gatef_ref[0] * _rms(y, g2_ref[...])


def _layer_tail(mix_fn, mix_args, mix_specs, x, gain_m, gate_m, g1, sc, sh, wi_bf16, wo_bf16, layer,
                g2, gate_f, tm, name, mix_scratch=()):
    bsz, seq, d = x.shape
    vec = pl.BlockSpec((1, 1, d), lambda b, i: (b, 0, 0))
    hidden = wo_bf16.shape[1]

    def layer_slab(w):
        return pl.BlockSpec((None,) + w.shape[1:], lambda b, i: (layer, 0, 0),
                            pipeline_mode=pl.Buffered(1))

    ntile = hidden // MXU_WIDTH
    assert ntile * MXU_WIDTH == hidden
    bounds = (0, (ntile + 1) // 2 * MXU_WIDTH, hidden)
    row = lambda v: v.reshape(1, d)
    per_batch = lambda v: v.reshape(bsz, 1, d)
    return pl.pallas_call(
        functools.partial(_tail_kernel, nmix=len(mix_args), mix_fn=mix_fn, bounds=bounds),
        out_shape=jax.ShapeDtypeStruct(x.shape, F32),
        grid=(bsz, seq // tm),
        in_specs=list(mix_specs) + [_row_spec(tm, d), _const_spec((1, d)), vec, _const_spec((1, d)),
                                    vec, vec, layer_slab(wi_bf16), layer_slab(wo_bf16),
                                    _const_spec((1, d)), vec],
        out_specs=_row_spec(tm, d),
        scratch_shapes=list(mix_scratch),
        compiler_params=_cparams("parallel", "parallel"),
        name=name,
    )(*mix_args, x, row(gain_m), per_batch(gate_m), row(g1), per_batch(sc), per_batch(sh),
      wi_bf16, wo_bf16, row(g2), per_batch(gate_f))


def _tail0(branches, o_f, o_b, z, g_col, out_norm, w_bf16, tm, **kw):
    wv = B_HEADS * B_DV
    m_arrays, m_specs, m_scratch = _merge_operands(branches, tm)
    specs = m_specs + [_row_spec(tm, wv), _row_spec(tm, wv), _row_spec(tm, wv, g_col // wv),
                       _const_spec((1, B_DV)), _const_spec(w_bf16.shape)]
    args = m_arrays + [o_f, o_b, z, out_norm.reshape(1, B_DV), w_bf16]
    return _layer_tail(_mix0_y, args, specs, tm=tm, name="tail0", mix_scratch=m_scratch, **kw)


def _tail1(c_out, d_out, w_bf16, tm, **kw):
    specs = [pl.BlockSpec((1, c_out.shape[1], tm, c_out.shape[3]), lambda b, i: (b, 0, i, 0)),
             _row_spec(tm, d_out.shape[-1]), _const_spec(w_bf16.shape)]
    return _layer_tail(_mix1_y, (c_out, d_out, w_bf16), specs, tm=tm, name="tail1", **kw)


def _fft_kernel(u_ref, f1_ref, twc_ref, tws_ref, f2_ref, fw_ref, o_ref, u_sc, p_sc, y_sc,
                *, scale, n1, n2):
    pu = n2 + FFT_PAD
    pp = 2 * n1 + FFT_PAD
    py = n1 + FFT_PAD
    f1 = f1_ref[...].astype(BF16)
    f2 = f2_ref[...].astype(BF16)
    fw = fw_ref[...].astype(BF16)
    for i1 in range(n1):
        u_sc[i1 * pu:i1 * pu + n2, :] = u_ref[0, 0, i1 * n2:(i1 + 1) * n2, :]

    nb = FFT_BATCH

    def stage1(blk, carry):
        i2s = [blk * nb + j for j in range(nb)]
        x = jnp.concatenate([u_sc[pl.ds(i2, n1, stride=pu), :] for i2 in i2s], axis=1)
        p = jnp.dot(f1, x.astype(BF16), preferred_element_type=F32)
        for j, i2 in enumerate(i2s):
            p_sc[pl.ds(pl.multiple_of(i2 * pp, 8), 2 * n1), :] = p[:, j * C_WIDTH:(j + 1) * C_WIDTH]
        return carry

    lax.fori_loop(0, n2 // nb, stage1, 0, unroll=2)

    def stage2(blk, carry):
        k1s = [blk * nb + j for j in range(nb)]
        qr, qi = [], []
        for k1 in k1s:
            tc = twc_ref[k1]
            ts = tws_ref[k1]
            pr = p_sc[pl.ds(k1, n2, stride=pp), :]
            pim = p_sc[pl.ds(n1 + k1, n2, stride=pp), :]
            qr.append(pr * tc + pim * ts)
            qi.append(pim * tc - pr * ts)
        q = jnp.concatenate([jnp.concatenate(qr, axis=1), jnp.concatenate(qi, axis=1)], axis=0)
        xx = jnp.dot(f2, q.astype(BF16), preferred_element_type=F32)
        xg = jnp.concatenate(
            [jnp.concatenate([xx[:n2, j * C_WIDTH:(j + 1) * C_WIDTH],
                              xx[n2:, j * C_WIDTH:(j + 1) * C_WIDTH]], axis=1) for j in range(nb)],
            axis=0)
        y = jnp.dot(xg.astype(BF16), fw, preferred_element_type=F32) * scale
        for j, k1 in enumerate(k1s):
            y_sc[pl.ds(k1, n2, stride=py), :] = y[j * n2:(j + 1) * n2]
        return carry

    lax.fori_loop(0, n1 // nb, stage2, 0, unroll=2)
    for k2 in range(n2):
        o_ref[0, 0, k2 * n1:(k2 + 1) * n1, :] = y_sc[k2 * py:k2 * py + n1, :]


def _fourier_mixer(u):
    bsz, ngroups, seq, width = u.shape
    n2 = FFT_N2
    n1 = seq // n2
    assert n1 * n2 == seq and width == C_WIDTH and n1 % 8 == 0
    a1 = 2.0 * np.pi * np.outer(np.arange(n1), np.arange(n1)) / n1
    f1 = np.concatenate([np.cos(a1), -np.sin(a1)], axis=0)
    a2 = 2.0 * np.pi * np.outer(np.arange(n2), np.arange(n2)) / n2
    c2, s2 = np.cos(a2), np.sin(a2)
    f2 = np.block([[c2, s2], [-s2, c2]])
    aw = 2.0 * np.pi * np.outer(np.arange(C_WIDTH), np.arange(C_WIDTH)) / C_WIDTH
    fw = np.concatenate([np.cos(aw), np.sin(aw)], axis=0)
    at = np.repeat((2.0 * np.pi * np.outer(np.arange(n1), np.arange(n2)) / seq)[:, :, None],
                   C_WIDTH, axis=2)
    consts = (jnp.asarray(f1, F32), jnp.asarray(np.cos(at), F32), jnp.asarray(np.sin(at), F32),
              jnp.asarray(f2, F32), jnp.asarray(fw, F32))
    blk = pl.BlockSpec((1, 1, seq, C_WIDTH), lambda b, g: (b, g, 0, 0))
    return pl.pallas_call(
        functools.partial(_fft_kernel, scale=float(1.0 / np.sqrt(seq * C_WIDTH)), n1=n1, n2=n2),
        out_shape=jax.ShapeDtypeStruct(u.shape, F32),
        grid=(bsz, ngroups),
        in_specs=[blk] + [_const_spec(c.shape) for c in consts],
        out_specs=blk,
        scratch_shapes=[pltpu.VMEM((n1 * (n2 + FFT_PAD), C_WIDTH), F32),
                        pltpu.VMEM((n2 * (2 * n1 + FFT_PAD), C_WIDTH), F32),
                        pltpu.VMEM((n2 * (n1 + FFT_PAD), C_WIDTH), F32)],
        compiler_params=_cparams("parallel", "parallel"),
        name="fft",
    )(u, *consts)


def _head_perm():
    rep = D_Q_HEADS // D_KV_HEADS
    cols = []
    for j in range(rep):
        for g in range(D_KV_HEADS):
            h = g * rep + j
            cols.extend(range(h * D_HEAD_DIM, (h + 1) * D_HEAD_DIM))
    return np.asarray(cols, np.int32)


def _rope_tables(seq):
    rows = seq // GRID_W
    row = jnp.repeat(jnp.arange(rows, dtype=F32), GRID_W)
    col = jnp.tile(jnp.arange(GRID_W, dtype=F32), rows)
    axis_dim = D_HEAD_DIM // 2
    inv_freq = jnp.power(ROPE_THETA, -jnp.arange(0, axis_dim, 2, dtype=F32) / axis_dim)
    ang_r = row[:, None] * inv_freq[None, :]
    ang_c = col[:, None] * inv_freq[None, :]
    cr, sr, cc, sc = jnp.cos(ang_r), jnp.sin(ang_r), jnp.cos(ang_c), jnp.sin(ang_c)
    cos = jnp.concatenate([cr, cr, cc, cc], axis=1)
    sin = jnp.concatenate([-sr, sr, -sc, sc], axis=1)
    reps = LANES // D_HEAD_DIM
    return jnp.tile(cos, (1, reps)), jnp.tile(sin, (1, reps))


def _inproj_qk_kernel(x_ref, gain_ref, sc_ref, sh_ref, w_ref, cos_ref, sin_ref, bd_h_ref, bd_l_ref,
                      gq_ref, gk_ref, u_ref, qo_ref, ko_ref, vo_ref):
    h = _rms(x_ref[0], gain_ref[...]) * (1.0 + sc_ref[0]) + sh_ref[0]
    z = _dot(h, w_ref[...])
    ngrp = u_ref.shape[1]
    for g in range(ngrp):
        u_ref[0, g] = z[:, g * LANES:(g + 1) * LANES]
    cos = cos_ref[...]
    sin = sin_ref[...]
    quarter = D_HEAD_DIM // 4
    lane = lax.broadcasted_iota(jnp.int32, (1, LANES), 1)
    first_of_pair = (lane // quarter) % 2 == 0

    def norm_rope(x, gain, scale):
        ms = _dot_tab(bd_h_ref[...], bd_l_ref[...], x * x, tab_left=False)
        xn = x * lax.rsqrt(ms + EPS) * gain
        partner = jnp.where(first_of_pair, pltpu.roll(xn, LANES - quarter, 1),
                            pltpu.roll(xn, quarter, 1))
        return ((xn * cos + partner * sin) * scale).astype(BF16)

    nq = qo_ref.shape[-1] // LANES
    for j in range(nq):
        qo_ref[0, :, j * LANES:(j + 1) * LANES] = norm_rope(
            z[:, (ngrp + j) * LANES:(ngrp + j + 1) * LANES], gq_ref[...], D_HEAD_DIM ** -0.5 * LOG2E)
    ko_ref[0] = norm_rope(z[:, (ngrp + nq) * LANES:(ngrp + nq + 1) * LANES], gk_ref[...], 1.0)
    vt = z[:, (ngrp + nq + 1) * LANES:(ngrp + nq + 2) * LANES].T
    ones = jnp.ones((FLASH_ONES, vt.shape[1]), F32)
    vo_ref[0] = jnp.concatenate(
        [piece for g in range(D_KV_HEADS)
         for piece in (vt[g * D_HEAD_DIM:(g + 1) * D_HEAD_DIM], ones)], axis=0).astype(BF16)


def _inproj_qk(x, gain, sc, sh, w_bf16, qk_norm_j, tm):
    bsz, seq, d = x.shape
    n = w_bf16.shape[1]
    qw = D_Q_HEADS * D_HEAD_DIM
    kw = D_KV_HEADS * D_HEAD_DIM
    assert kw == LANES and n == C_GROUPS * C_WIDTH + qw + 2 * kw
    vrows = D_KV_HEADS * (D_HEAD_DIM + FLASH_ONES)
    cos, sin = _rope_tables(seq)
    bd = np.kron(np.eye(LANES // D_HEAD_DIM), np.full((D_HEAD_DIM, D_HEAD_DIM), 1.0 / D_HEAD_DIM))
    bd_h, bd_l = _np_split2(bd)
    reps = LANES // D_HEAD_DIM
    gq = jnp.tile(qk_norm_j[0].astype(F32), reps).reshape(1, LANES)
    gk = jnp.tile(qk_norm_j[1].astype(F32), reps).reshape(1, LANES)
    tab = pl.BlockSpec((tm, LANES), lambda b, i: (i, 0))
    vec = pl.BlockSpec((1, 1, d), lambda b, i: (b, 0, 0))
    return pl.pallas_call(
        _inproj_qk_kernel,
        out_shape=(jax.ShapeDtypeStruct((bsz, C_GROUPS, seq, C_WIDTH), F32),
                   jax.ShapeDtypeStruct((bsz, seq, qw), BF16),
                   jax.ShapeDtypeStruct((bsz, seq, kw), BF16),
                   jax.ShapeDtypeStruct((bsz, vrows, seq), BF16)),
        grid=(bsz, seq // tm),
        in_specs=[pl.BlockSpec((1, tm, d), lambda b, i: (b, i, 0)),
                  _const_spec((1, d)), vec, vec, _const_spec((d, n)),
                  tab, tab, _const_spec(bd_h.shape), _const_spec(bd_l.shape),
                  _const_spec((1, LANES)), _const_spec((1, LANES))],
        out_specs=(pl.BlockSpec((1, C_GROUPS, tm, C_WIDTH), lambda b, i: (b, 0, i, 0)),
                   _row_spec(tm, qw), _row_spec(tm, kw),
                   pl.BlockSpec((1, vrows, tm), lambda b, i: (b, 0, i))),
        compiler_params=_cparams("parallel", "parallel"),
        name="inproj_qk",
    )(x, gain.reshape(1, d), sc.reshape(bsz, 1, d), sh.reshape(bsz, 1, d), w_bf16,
      cos, sin, bd_h, bd_l, gq, gk)


def _flash_kernel(q_ref, k_ref, vt_ref, o_ref, m_sc, acc_sc, s_sc):
    kv = pl.program_id(2)

    @pl.when(kv == 0)
    def _():
        m_sc[...] = jnp.full_like(m_sc, -jnp.inf)
        acc_sc[...] = jnp.zeros_like(acc_sc)

    lane = lax.broadcasted_iota(jnp.int32, (1, LANES), 1)
    lo = lane < D_HEAD_DIM
    nblk = q_ref.shape[-1] // LANES
    tq, tk = q_ref.shape[1], k_ref.shape[1]
    ku, qu = FLASH_KEY_UNIT, FLASH_QUERY_UNIT
    grows = D_HEAD_DIM + FLASH_ONES
    nheads = nblk * D_KV_HEADS
    k = k_ref[0]

    nchunk = tq // qu

    def logits_chunk(idx, c):
        j, g = divmod(idx, D_KV_HEADS)
        qj = q_ref[0, c * qu:(c + 1) * qu, j * LANES:(j + 1) * LANES]
        sel = lo if g == 0 else jnp.logical_not(lo)
        s = lax.dot_general(k, jnp.where(sel, qj, jnp.zeros_like(qj)), (((1,), (1,)), ((), ())),
                            preferred_element_type=F32)
        s_sc[idx, :, c * qu:(c + 1) * qu] = s
        return jnp.max(s, axis=0, keepdims=True)

    def finish_logits(idx, mcs):
        m_prev = m_sc[idx, 0:1, :]
        m_new = jnp.maximum(m_prev, jnp.concatenate(mcs, axis=1))
        m_sc[idx, 0:1, :] = m_new
        return m_new, jnp.exp2(m_prev - m_new)

    def value_chunk(idx, c, m_new, alpha):
        j, g = divmod(idx, D_KV_HEADS)
        rows = slice(g * grows, (g + 1) * grows)
        qcols = slice(c * qu, (c + 1) * qu)
        pv = None
        for u in range(tk // ku):
            keys = slice(u * ku, (u + 1) * ku)
            p = jnp.exp2(s_sc[idx, keys, qcols] - m_new[:, qcols])
            d = jnp.dot(vt_ref[0, rows, keys], p.astype(BF16), preferred_element_type=F32)
            pv = d if pv is None else pv + d
        acc_sc[j, rows, qcols] = alpha[:, qcols] * acc_sc[j, rows, qcols] + pv

    def logits_pass(idx):
        return finish_logits(idx, [logits_chunk(idx, c) for c in range(nchunk)])

    pending = [logits_pass(i) for i in range(min(FLASH_AHEAD, nheads))]
    for idx in range(nheads):
        if idx + FLASH_AHEAD < nheads:
            pending.append(logits_pass(idx + FLASH_AHEAD))
        stats = pending.pop(0)
        for c in range(nchunk):
            value_chunk(idx, c, *stats)

    @pl.when(kv == pl.num_programs(2) - 1)
    def _():
        for j in range(nblk):
            parts = []
            for g in range(D_KV_HEADS):
                num = acc_sc[j, g * grows:g * grows + D_HEAD_DIM, :]
                den = acc_sc[j, g * grows + D_HEAD_DIM:g * grows + D_HEAD_DIM + 1, :]
                parts.append(num / den)
            o_ref[0, :, j * LANES:(j + 1) * LANES] = (
                jnp.concatenate(parts, axis=0).T.astype(o_ref.dtype))


def _flash(q, k, vt, tq, tk):
    bsz, seq, qw = q.shape
    kw = k.shape[-1]
    vrows = vt.shape[1]
    return pl.pallas_call(
        _flash_kernel,
        out_shape=jax.ShapeDtypeStruct((bsz, seq, qw), BF16),
        grid=(bsz, seq // tq, seq // tk),
        in_specs=[pl.BlockSpec((1, tq, qw), lambda b, i, j: (b, i, 0)),
                  pl.BlockSpec((1, tk, kw), lambda b, i, j: (b, j, 0)),
                  pl.BlockSpec((1, vrows, tk), lambda b, i, j: (b, 0, j))],
        out_specs=pl.BlockSpec((1, tq, qw), lambda b, i, j: (b, i, 0)),
        scratch_shapes=[pltpu.VMEM((D_Q_HEADS, 8, tq), F32),
                        pltpu.VMEM((qw // LANES, vrows, tq), F32),
                        pltpu.VMEM((D_Q_HEADS, tk, tq), F32)],
        compiler_params=_cparams("parallel", "parallel", "arbitrary"),
        name="flash",
    )(q, k, vt)


def kernel(x, c, t5_bias, hgrn_lb_logits, ada_w, ada_b, norm_gains, ab_w_in, ab_w_out,
           hgrn_out_norm, cd_w_in, cd_w_out, qk_norm, ffn_w_in, ffn_w_out):
    bsz, seq, d = x.shape
    depth = ada_w.shape[0]
    mod = _ada_mod(c.astype(F32), ada_w, ada_b)
    perm = _head_perm()
    aw = A_HEADS * A_HEAD_DIM
    cw = C_GROUPS * C_WIDTH
    qw = D_Q_HEADS * D_HEAD_DIM
    tm_in = min(512, seq)
    ffn_wi = ffn_w_in.astype(BF16)
    ffn_wo = ffn_w_out.astype(BF16)
    for layer in range(depth):
        sh_m, sc_m, g_m, sh_f, sc_f, g_f = [mod[layer, :, i * d:(i + 1) * d] for i in range(6)]
        gains = norm_gains[layer]
        j = layer // 2
        tail = dict(x=x, gain_m=gains[1], gate_m=g_m, g1=gains[2], sc=sc_f, sh=sh_f,
                    wi_bf16=ffn_wi, wo_bf16=ffn_wo, layer=layer, g2=gains[3], gate_f=g_f)
        if layer % 2 == 0:
            w_in = ab_w_in[j].astype(BF16)
            *qkv_cm, z = _inproj_cm(x, gains[0], sc_m, sh_m, w_in, 3 * aw, tm_in)
            branches = [_dilated_branch(cm, t5_bias, window, dil)
                        for cm, (window, dil) in zip(qkv_cm, DIL_CFG)]
            o_f, o_b = _hgrn(z, hgrn_lb_logits, layer, 0)
            g_col = 3 * B_HEADS * B_DK + B_HEADS * B_DV
            x = _tail0(branches, o_f, o_b, z, g_col, hgrn_out_norm[j], ab_w_out[j].astype(BF16),
                       tm=min(TAIL_SUB, seq), **tail)
        else:
            w_full = cd_w_in[j]
            w_in = jnp.concatenate([w_full[:, :cw], w_full[:, cw:cw + qw][:, perm],
                                    w_full[:, cw + qw:]], axis=1).astype(BF16)
            u, qn, kn, vn = _inproj_qk(x, gains[0], sc_m, sh_m, w_in, qk_norm[j], min(1024, seq))
            c_out = _fourier_mixer(u)
            d_out = _flash(qn, kn, vn, min(FLASH_TQ, seq), min(FLASH_TK, seq))
            w_out_full = cd_w_out[j]
            w_out = jnp.concatenate([w_out_full[:cw], w_out_full[cw:][perm]], axis=0).astype(BF16)
            x = _tail1(c_out, d_out, w_out, tm=min(TAIL_BLOCK, seq), **tail)
    return x
```

```python
import functools

import numpy as np
import jax
import jax.numpy as jnp
from jax import lax
from jax.experimental import pallas as pl
from jax.experimental.pallas import tpu as pltpu

F32 = jnp.float32
BF16 = jnp.bfloat16
LANES = 128
MXU_WIDTH = 256
VMEM_LIMIT_BYTES = 56 * 2**20
NEG_INF = -1e30
EPS = 1e-6

GRID_W = 64
A_HEADS = 8
A_HEAD_DIM = 64
DIL_CFG = ((128, 1), (512, 4), (2048, 16))
N_BUCKETS = 32
T5_MAX_DIST = 1024
B_HEADS = 4
B_DK = 128
B_DV = 128
C_GROUPS = 4
C_WIDTH = 128
D_Q_HEADS = 8
D_KV_HEADS = 2
D_HEAD_DIM = 64
ROPE_THETA = 10000.0

DIL_TQ = 128
DIL_TILE = 512
HGRN_BLOCK = 1024
HGRN_SUB = 256
HGRN_BCAST_MIN = 8
FLASH_TQ = 1024
FLASH_TK = 2048
FLASH_LOGIT_BUFS = 4
FLASH_KEY_UNIT = 256
FLASH_QUERY_UNIT = 1024
FLASH_AHEAD = 1
FLASH_ONES = 16
TAIL_BLOCK = 512
TAIL_SUB = 512
FFT_N2 = 128
FFT_BATCH = 8
FFT_PAD = 8
LOG2E = 1.4426950408889634


def _cparams(*sem):
    return pltpu.CompilerParams(dimension_semantics=sem, vmem_limit_bytes=VMEM_LIMIT_BYTES)


def _const_spec(shape):
    nd = len(shape)
    return pl.BlockSpec(shape, lambda *_: (0,) * nd, pipeline_mode=pl.Buffered(1))


def _sigmoid(x):
    return 1.0 / (1.0 + jnp.exp(-x))


def _dot(a, b):
    return jnp.dot(a.astype(BF16), b.astype(BF16), preferred_element_type=F32)


def _dot_nt(a, b):
    return lax.dot_general(a.astype(BF16), b.astype(BF16), (((1,), (1,)), ((), ())),
                           preferred_element_type=F32)


def _split2(a):
    hi = a.astype(BF16)
    lo = (a - hi.astype(F32)).astype(BF16)
    return hi, lo


def _split3(a):
    a1 = a.astype(BF16)
    r = a - a1.astype(F32)
    a2 = r.astype(BF16)
    a3 = (r - a2.astype(F32)).astype(BF16)
    return a1, a2, a3


def _dot_tab(tab_hi, tab_lo, x, *, tab_left):
    x_hi, x_lo = _split2(x)
    if tab_left:
        d = lambda t, v: jnp.dot(t, v, preferred_element_type=F32)
    else:
        d = lambda t, v: jnp.dot(v, t, preferred_element_type=F32)
    return d(tab_hi, x_hi) + (d(tab_hi, x_lo) + d(tab_lo, x_hi))


def _rms(x, gain):
    ms = jnp.mean(x * x, axis=-1, keepdims=True)
    return x * lax.rsqrt(ms + EPS) * gain


def _np_split2(t):
    t = np.asarray(t, np.float32)
    hi = jnp.asarray(t, F32).astype(BF16)
    lo = (jnp.asarray(t, F32) - hi.astype(F32)).astype(BF16)
    return hi, lo


def _mod_kernel(c_ref, w_ref, b_ref, o_ref):
    c = c_ref[...]
    o_ref[0] = _dot(c * _sigmoid(c), w_ref[0]) + b_ref[0]


def _ada_mod(c, ada_w, ada_b):
    depth, d, n6 = ada_w.shape
    bsz = c.shape[0]
    rows = 8
    cp = jnp.zeros((rows, d), F32).at[:bsz].set(c)
    tn = n6 // 4
    out = pl.pallas_call(
        _mod_kernel,
        out_shape=jax.ShapeDtypeStruct((depth, rows, n6), F32),
        grid=(depth, n6 // tn),
        in_specs=[pl.BlockSpec((rows, d), lambda l, j: (0, 0)),
                  pl.BlockSpec((1, d, tn), lambda l, j: (l, 0, j)),
                  pl.BlockSpec((1, 1, tn), lambda l, j: (l, 0, j))],
        out_specs=pl.BlockSpec((1, rows, tn), lambda l, j: (l, 0, j)),
        compiler_params=_cparams("parallel", "parallel"),
        name="ada_mod",
    )(cp, ada_w, ada_b.reshape(depth, 1, n6))
    return out[:, :bsz]


def _inproj_cm_kernel(x_ref, gain_ref, sc_ref, sh_ref, w_ref, *refs):
    cm_refs, rest_ref, zs_sc, zc_sc = refs[:-3], refs[-3], refs[-2], refs[-1]
    h = _rms(x_ref[0], gain_ref[...]) * (1.0 + sc_ref[0]) + sh_ref[0]
    z = _dot(h, w_ref[...])
    nblk, tm, _ = zs_sc.shape
    rest_ref[0] = z[:, nblk * LANES:]
    nq = A_HEADS * A_HEAD_DIM // LANES
    for c in range(nblk):
        blk = z[:, c * LANES:(c + 1) * LANES]
        zs_sc[c] = blk * (A_HEAD_DIM ** -0.5 * LOG2E) if c < nq else blk
    src, sd = zs_sc, 1
    for level, (cm_ref, (_, dil)) in enumerate(zip(cm_refs, DIL_CFG)):
        step, n = dil // sd, tm // dil
        keep = dil > 1 and level + 1 < len(DIL_CFG)
        for rs in range(sd):
            for cc in range(step):
                r = rs + sd * cc
                for c in range(nblk):
                    rows = src[c, pl.ds(rs * (tm // sd) + cc, n, stride=step), :]
                    cm_ref[0, r, :, c * LANES:(c + 1) * LANES] = rows.astype(BF16)
                    if keep:
                        zc_sc[c, r * n:(r + 1) * n, :] = rows
        if keep:
            src, sd = zc_sc, dil


def _inproj_cm(x, gain, sc, sh, w_bf16, na, tm):
    bsz, seq, d = x.shape
    n = w_bf16.shape[1]
    vec = pl.BlockSpec((1, 1, d), lambda b, i: (b, 0, 0))
    dils = [dl for _, dl in DIL_CFG]
    assert dils[0] == 1 and all(b % a == 0 for a, b in zip(dils, dils[1:]))
    return pl.pallas_call(
        _inproj_cm_kernel,
        out_shape=tuple([jax.ShapeDtypeStruct((bsz, dl, seq // dl, na), BF16) for dl in dils]
                        + [jax.ShapeDtypeStruct((bsz, seq, n - na), F32)]),
        grid=(bsz, seq // tm),
        in_specs=[pl.BlockSpec((1, tm, d), lambda b, i: (b, i, 0)),
                  _const_spec((1, d)), vec, vec, _const_spec((d, n))],
        out_specs=tuple([pl.BlockSpec((1, dl, tm // dl, na), lambda b, i: (b, 0, i, 0)) for dl in dils]
                        + [pl.BlockSpec((1, tm, n - na), lambda b, i: (b, i, 0))]),
        scratch_shapes=[pltpu.VMEM((na // LANES, tm, LANES), F32),
                        pltpu.VMEM((na // LANES, tm, LANES), F32)],
        compiler_params=_cparams("parallel", "parallel"),
        name="inproj_cm",
    )(x, gain.reshape(1, d), sc.reshape(bsz, 1, d), sh.reshape(bsz, 1, d), w_bf16)


def _t5_buckets(rel):
    half = N_BUCKETS // 2
    max_exact = half // 2
    n = np.abs(rel)
    large = max_exact + (np.log(np.maximum(n, 1) / max_exact) / np.log(T5_MAX_DIST / max_exact)
                         * (half - max_exact)).astype(np.int32)
    large = np.minimum(large, half - 1)
    return (np.where(rel > 0, half, 0) + np.where(n < max_exact, n, large)).astype(np.int32)


def _dil_bias(t5_bias, window, dil, tq):
    half = (window // 2) // dil
    assert half == tq // 2
    rel = np.arange(2 * tq)[None, :] - half - np.arange(tq)[:, None]
    inside = np.abs(rel) <= half
    buckets = _t5_buckets(np.where(inside, rel, 0) * dil)
    onehot =jnp.asarray(np.eye(N_BUCKETS, dtype=np.float32)[buckets])
    bias = jnp.einsum("qkn,nh->hqk", onehot, t5_bias.astype(F32), precision=lax.Precision.HIGHEST)
    return jnp.where(jnp.asarray(inside)[None], bias * LOG2E, NEG_INF)


def _dil_kernel(q_ref, kp_ref, kc_ref, kn_ref, vp_ref, vc_ref, vn_ref, bias_ref, o_ref, lse_ref,
                *, class_len):
    i = pl.program_id(2)
    sub, hq, tile = DIL_TQ, DIL_TQ // 2, q_ref.shape[2]
    kwin = jnp.concatenate([kp_ref[0, 0], kc_ref[0, 0], kn_ref[0, 0]], axis=0)
    vwin = jnp.concatenate([vp_ref[0, 0], vc_ref[0, 0], vn_ref[0, 0]], axis=0)
    lane = lax.broadcasted_iota(jnp.int32, (1, LANES), 1)
    lo = lane < A_HEAD_DIM
    nblk = A_HEADS // 2
    units = [(jt, j) for jt in range(tile // sub) for j in range(nblk)]
    logits = []
    for jt, j in units:
        cols = slice(j * LANES, (j + 1) * LANES)
        qj = q_ref[0, 0, jt * sub:(jt + 1) * sub, cols]
        zero = jnp.zeros_like(qj)
        q2 = jnp.concatenate([jnp.where(lo, qj, zero), jnp.where(lo, zero, qj)], axis=0)
        s = lax.dot_general(q2, kwin[jt * sub:jt * sub + 2 * sub, cols], (((1,), (1,)), ((), ())),
                            preferred_element_type=F32)
        kpos = i * tile + jt * sub - hq + lax.broadcasted_iota(jnp.int32, (1, 2 * sub), 1)
        valid = jnp.logical_and(kpos >= 0, kpos < class_len)
        logits.append(jnp.where(valid, s + bias_ref[j], NEG_INF))
    s_all = jnp.concatenate(logits, axis=0)
    m = jnp.max(s_all, axis=-1, keepdims=True)
    p32 = jnp.exp2(s_all - m)
    l = jnp.sum(p32, axis=-1, keepdims=True)
    p = p32.astype(BF16)
    rinv = 1.0 / l
    lse = m + jnp.log2(l)
    for jt in range(tile // sub):
        lse_all = jnp.zeros((sub, LANES), F32)
        for j in range(nblk):
            cols = slice(j * LANES, (j + 1) * LANES)
            r0 = (jt * nblk + j) * 2 * sub
            o2 = jnp.dot(p[r0:r0 + 2 * sub], vwin[jt * sub:jt * sub + 2 * sub, cols],
                         preferred_element_type=F32) * rinv[r0:r0 + 2 * sub]
            lse_all = jnp.where(lane == 2 * j, lse[r0:r0 + sub], lse_all)
            lse_all = jnp.where(lane == 2 * j + 1, lse[r0 + sub:r0 + 2 * sub], lse_all)
            o_ref[0, 0, jt * sub:(jt + 1) * sub, cols] = (
                jnp.where(lo, o2[:sub], o2[sub:]).astype(o_ref.dtype))
        lse_ref[0, 0, jt * sub:(jt + 1) * sub, :] = lse_all


def _dilated_branch(qkv_cm, t5_bias, window, dil):
    bsz, _, cl, width = qkv_cm.shape
    aw = A_HEADS * A_HEAD_DIM
    tile, hq = min(DIL_TILE, cl), DIL_TQ // 2
    nt = cl // tile
    per = tile // hq
    nh = cl // hq

    def cur(col):
        return pl.BlockSpec((1, 1, tile, aw), lambda b, r, i: (b, r, i, col))

    def prev(col):
        return pl.BlockSpec((1, 1, hq, aw), lambda b, r, i: (b, r, jnp.maximum(i * per - 1, 0), col))

    def nxt(col):
        return pl.BlockSpec((1, 1, hq, aw),
                            lambda b, r, i: (b, r, jnp.minimum((i + 1) * per, nh - 1), col))

    return pl.pallas_call(
        functools.partial(_dil_kernel, class_len=cl),
        out_shape=(jax.ShapeDtypeStruct((bsz, dil, cl, aw), BF16),
                   jax.ShapeDtypeStruct((bsz, dil, cl, LANES), F32)),
        grid=(bsz, dil, nt),
        in_specs=[cur(0), prev(1), cur(1), nxt(1), prev(2), cur(2), nxt(2),
                  _const_spec((A_HEADS // 2, 2 * DIL_TQ, 2 * DIL_TQ))],
        out_specs=(pl.BlockSpec((1, 1, tile, aw), lambda b, r, i: (b, r, i, 0)),
                   pl.BlockSpec((1, 1, tile, LANES), lambda b, r, i: (b, r, i, 0))),
        compiler_params=_cparams("parallel", "parallel", "parallel"),
        name=f"dilated_d{dil}",
    )(*([qkv_cm] * 7),
      _dil_bias(t5_bias, window, dil, DIL_TQ).reshape(A_HEADS // 2, 2 * DIL_TQ, 2 * DIL_TQ))


def _merge_branches(a_refs, l_refs, e_ref, a_scs, l_scs, sub):
    tm = l_scs[0].shape[0]
    accs, lses = [], []
    for (_, dil), a_ref, l_ref, a_sc, l_sc in zip(DIL_CFG, a_refs, l_refs, a_scs, l_scs):
        nblk = a_sc.shape[0]
        for r in range(dil):
            rows = pl.ds(r, tm // dil, stride=dil)
            src = pl.ds(sub * (tm // dil), tm // dil)
            for c in range(nblk):
                a_sc[c, rows, :] = a_ref[0, r, src, c * LANES:(c + 1) * LANES].astype(F32)
            l_sc[rows, :] = l_ref[0, r, src, :]
        accs.append(jnp.concatenate([a_sc[c] for c in range(nblk)], axis=1))
        lses.append(l_sc[...])
    mx = functools.reduce(jnp.maximum, lses)
    ws = [jnp.exp2(x - mx) for x in lses]
    tot = functools.reduce(jnp.add, ws)
    out = None
    for w, a in zip(ws, accs):
        w_hi, w_lo = _split2(w / tot)
        wide = (jnp.dot(w_hi, e_ref[...], preferred_element_type=F32)
                + jnp.dot(w_lo, e_ref[...], preferred_element_type=F32))
        out = wide * a if out is None else out + wide * a
    return out


def _merge_operands(branch_outs, tm):
    aw = branch_outs[0][0].shape[-1]
    expand = np.zeros((LANES, aw), np.float32)
    for h in range(A_HEADS):
        expand[h, h * A_HEAD_DIM:(h + 1) * A_HEAD_DIM] = 1.0
    dils = [d for _, d in DIL_CFG]
    specs = ([pl.BlockSpec((1, d, tm // d, aw), lambda b, i: (b, 0, i, 0)) for d in dils]
             + [pl.BlockSpec((1, d, tm // d, LANES), lambda b, i: (b, 0, i, 0)) for d in dils]
             + [_const_spec(expand.shape)])
    arrays = [o for o, _ in branch_outs] + [l for _, l in branch_outs] + [jnp.asarray(expand, BF16)]
    scratch = ([pltpu.VMEM((aw // LANES, TAIL_SUB, LANES), F32) for _ in dils]
               + [pltpu.VMEM((TAIL_SUB, LANES), F32) for _ in dils])
    return arrays, specs, scratch


def _hgrn_consts(reverse):
    t = HGRN_SUB
    r = np.arange(t)
    u = r[None, :]
    row = r[:, None]
    nmats = [(u >= row) if reverse else (u <= row)]
    masks = []
    m = t // 2
    while m >= 1:
        grp = r // (2 * m)
        in_first = (r % (2 * m)) < m
        same = grp[:, None] == grp[None, :]
        if reverse:
            beta = (grp * 2 * m + m)[:, None]
            n = np.where(in_first[:, None], (u >= row) & (u < beta), (u >= beta) & (u < row))
            mask = same & in_first[:, None] & ~in_first[None, :]
        else:
            beta = (grp * 2 * m + m - 1)[:, None]
            n = np.where(in_first[:, None], (u > row) & (u <= beta), (u > beta) & (u <= row))
            mask = same & ~in_first[:, None] & in_first[None, :]
        if m < HGRN_BCAST_MIN:
            nmats.append(n)
        masks.append(mask)
        m //= 2
    masks.append(np.eye(t, dtype=bool))
    nmat = jnp.asarray(np.concatenate(nmats, axis=0), F32).astype(BF16)
    return nmat, jnp.asarray(np.stack(masks), F32)


def _hgrn_block(q, f, v, lb, st, nmat_ref, mask_ref, reverse):
    t = HGRN_SUB
    nlev = mask_ref.shape[0] - 1
    qs = q * (B_DK ** -0.5)
    fa = lb + (1.0 - lb) * _sigmoid(f)
    kk = 1.0 - fa
    g_hi, g_lo = _split2(jnp.log2(fa))
    ex = jnp.dot(nmat_ref[...], jnp.concatenate([g_hi, g_lo], axis=1), preferred_element_type=F32)
    ex = ex[:, :B_DK] + ex[:, B_DK:]
    b = ex[:t]
    btot = b[0:1] if reverse else b[t - 1:t]
    a = mask_ref[nlev] * _dot_nt(qs, kk)
    fine = 1
    for l in range(nlev):
        m = t >> (l + 1)
        if m >= HGRN_BCAST_MIN:
            ref = jnp.concatenate(
                [jnp.broadcast_to(b[beta:beta + 1], (2 * m, B_DK))
                 for beta in range(m if reverse else m - 1, t, 2 * m)], axis=0)
            e = jnp.exp2(-jnp.abs(b - ref))
        else:
            e = jnp.exp2(ex[fine * t:(fine + 1) * t])
            fine += 1
        a = a + mask_ref[l] * _dot_nt(qs * e, kk * e)
    out = _dot(a, v) + _dot_nt(qs * jnp.exp2(b), st)
    khat = (kk * jnp.exp2(btot - b)).astype(BF16)
    st_new = st * jnp.exp2(btot) + jnp.dot(v.T.astype(BF16), khat, preferred_element_type=F32)
    return out, st_new


def _hgrn_lb(lg_ref, layer):
    lg = [lg_ref[l, 0] for l in range(lg_ref.shape[0])]
    mx = functools.reduce(jnp.maximum, lg)
    e = [jnp.exp(x - mx) for x in lg]
    return functools.reduce(jnp.add, e[:layer + 1]) / functools.reduce(jnp.add, e)


def _hgrn_kernel(qf_ref, ff_ref, vf_ref, qb_ref, fb_ref, vb_ref, lgf_ref, lgb_ref,
                 nf_ref, mf_ref, nb_ref, mb_ref, of_ref, ob_ref, sf_sc, sb_sc, *, layer):
    @pl.when(pl.program_id(2) == 0)
    def _():
        sf_sc[...] = jnp.zeros_like(sf_sc)
        sb_sc[...] = jnp.zeros_like(sb_sc)

    nsub = HGRN_BLOCK // HGRN_SUB
    chains = ((qf_ref, ff_ref, vf_ref, lgf_ref, sf_sc, nf_ref, mf_ref, of_ref, False),
              (qb_ref, fb_ref, vb_ref, lgb_ref, sb_sc, nb_ref, mb_ref, ob_ref, True))
    for q_ref, f_ref, v_ref, lg_ref, st_sc, n_ref, m_ref, o_ref, reverse in chains:
        lb = _hgrn_lb(lg_ref, layer)
        st = st_sc[...]
        for sub in (reversed(range(nsub)) if reverse else range(nsub)):
            rows = pl.ds(sub * HGRN_SUB, HGRN_SUB)
            o, st = _hgrn_block(q_ref[0, rows, :], f_ref[0, rows, :], v_ref[0, rows, :], lb, st,
                                n_ref, m_ref, reverse)
            o_ref[0, rows, :] = o.astype(o_ref.dtype)
        st_sc[...] = st


def _hgrn(z, lb_logits, layer, col0):
    bsz, seq, _ = z.shape
    t = HGRN_BLOCK
    nb = seq // t
    c0 = col0 // LANES
    hw = B_HEADS

    def zspec(group, rev):
        return pl.BlockSpec(
            (1, t, LANES),
            lambda b, h, j: (b, (nb - 1 - j) if rev else j, c0 + group * hw + h))

    def lgspec(direction):
        return pl.BlockSpec((lb_logits.shape[0], 1, 1, LANES),
                            lambda b, h, j: (0, direction * hw + h, 0, 0))

    nf, mf = _hgrn_consts(False)
    nbw, mbw = _hgrn_consts(True)
    lg = lb_logits.astype(F32).reshape(lb_logits.shape[0], 2 * hw, 1, LANES)
    o_shape = jax.ShapeDtypeStruct((bsz, seq, hw * B_DV), BF16)
    return pl.pallas_call(
        functools.partial(_hgrn_kernel, layer=layer),
        out_shape=(o_shape, o_shape),
        grid=(bsz, hw, nb),
        in_specs=[zspec(0, False), zspec(1, False), zspec(3, False),
                  zspec(0, True), zspec(2, True), zspec(3, True),
                  lgspec(0), lgspec(1),
                  _const_spec(nf.shape), _const_spec(mf.shape),
                  _const_spec(nbw.shape), _const_spec(mbw.shape)],
        out_specs=(pl.BlockSpec((1, t, LANES), lambda b, h, j: (b, j, h)),
                   pl.BlockSpec((1, t, LANES), lambda b, h, j: (b, nb - 1 - j, h))),
        scratch_shapes=[pltpu.VMEM((B_DV, B_DK), F32), pltpu.VMEM((B_DV, B_DK), F32)],
        compiler_params=_cparams("parallel", "parallel", "arbitrary"),
        name="hgrn",
    )(z, z, z, z, z, z, lg, lg, nf, mf, nbw, mbw)


def _mix0_y(sub, *refs):
    nbr = len(DIL_CFG)
    a_refs, l_refs, e_ref = refs[:nbr], refs[nbr:2 * nbr], refs[2 * nbr]
    of_ref, ob_ref, g_ref, on_ref, w_ref = refs[2 * nbr + 1:2 * nbr + 6]
    scratch = refs[2 * nbr + 6:]
    a = _merge_branches(a_refs, l_refs, e_ref, scratch[:nbr], scratch[nbr:], sub)
    rows = pl.ds(sub * TAIL_SUB, TAIL_SUB)
    o = of_ref[0, rows, :].astype(F32) + ob_ref[0, rows, :].astype(F32)
    g = g_ref[0, rows, :]
    parts = [_rms(o[:, h * B_DV:(h + 1) * B_DV], on_ref[...]) for h in range(B_HEADS)]
    bn = jnp.concatenate(parts, axis=-1) * (g * _sigmoid(g))
    na = a.shape[-1]
    return _dot(a, w_ref[:na, :]) + _dot(bn, w_ref[na:, :])


def _mix1_y(sub, c_ref, d_ref, w_ref):
    rows = pl.ds(sub * TAIL_SUB, TAIL_SUB)
    c = jnp.concatenate([c_ref[0, g, rows, :] for g in range(c_ref.shape[1])], axis=1)
    nc = c.shape[-1]
    return _dot(c, w_ref[:nc, :]) + _dot(d_ref[0, rows, :], w_ref[nc:, :])


def _row_spec(tm, width, col=0):
    return pl.BlockSpec((1, tm, width), lambda b, i: (b, i, col))


def _tail_kernel(*refs, nmix, mix_fn, bounds):
    (x_ref, gm_ref, gatem_ref, g1_ref, sc_ref, sh_ref, wi_ref, wo_ref, g2_ref, gatef_ref,
     o_ref) = refs[nmix:nmix + 11]
    hidden = wo_ref.shape[0]
    for sub in range(x_ref.shape[1] // TAIL_SUB):
        rows = pl.ds(sub * TAIL_SUB, TAIL_SUB)
        y_mix = mix_fn(sub, *refs[:nmix], *refs[nmix + 11:])
        x1 = x_ref[0, rows, :] + gatem_ref[0] * _rms(y_mix, gm_ref[...])
        h = (_rms(x1, g1_ref[...]) * (1.0 + sc_ref[0]) + sh_ref[0]).astype(BF16)
        y = None
        for c0, c1 in zip(bounds, bounds[1:]):
            gt = jnp.dot(h, wi_ref[:, c0:c1], preferred_element_type=F32)
            up = jnp.dot(h, wi_ref[:, hidden + c0:hidden + c1], preferred_element_type=F32)
            part = _dot(gt * _sigmoid(gt) * up, wo_ref[c0:c1, :])
            y = part if y is None else y + part
        o_ref[0, rows, :] = x1 + gatef_ref[0] * _rms(y, g2_ref[...])


def _layer_tail(mix_fn, mix_args, mix_specs, x, gain_m, gate_m, g1, sc, sh, wi_bf16, wo_bf16, layer,
                g2, gate_f, tm, name, mix_scratch=()):
    bsz, seq, d = x.shape
    vec = pl.BlockSpec((1, 1, d), lambda b, i: (b, 0, 0))
    hidden = wo_bf16.shape[1]

    def layer_slab(w):
        return pl.BlockSpec((None,) + w.shape[1:], lambda b, i: (layer, 0, 0),
                            pipeline_mode=pl.Buffered(1))

    ntile = hidden // MXU_WIDTH
    assert ntile * MXU_WIDTH == hidden
    bounds = (0, (ntile + 1) // 2 * MXU_WIDTH, hidden)
    row = lambda v: v.reshape(1, d)
    per_batch = lambda v: v.reshape(bsz, 1, d)
    return pl.pallas_call(
        functools.partial(_tail_kernel, nmix=len(mix_args), mix_fn=mix_fn, bounds=bounds),
        out_shape=jax.ShapeDtypeStruct(x.shape, F32),
        grid=(bsz, seq // tm),
        in_specs=list(mix_specs) + [_row_spec(tm, d), _const_spec((1, d)), vec, _const_spec((1, d)),
                                    vec, vec, layer_slab(wi_bf16), layer_slab(wo_bf16),
                                    _const_spec((1, d)), vec],
        out_specs=_row_spec(tm, d),
        scratch_shapes=list(mix_scratch),
        compiler_params=_cparams("parallel", "parallel"),
        name=name,
    )(*mix_args, x, row(gain_m), per_batch(gate_m), row(g1), per_batch(sc), per_batch(sh),
      wi_bf16, wo_bf16, row(g2), per_batch(gate_f))


def _tail0(branches, o_f, o_b, z, g_col, out_norm, w_bf16, tm, **kw):
    wv = B_HEADS * B_DV
    m_arrays, m_specs, m_scratch = _merge_operands(branches, tm)
    specs = m_specs + [_row_spec(tm, wv), _row_spec(tm, wv), _row_spec(tm, wv, g_col // wv),
                       _const_spec((1, B_DV)), _const_spec(w_bf16.shape)]
    args = m_arrays + [o_f, o_b, z, out_norm.reshape(1, B_DV), w_bf16]
    return _layer_tail(_mix0_y, args, specs, tm=tm, name="tail0", mix_scratch=m_scratch, **kw)


def _tail1(c_out, d_out, w_bf16, tm, **kw):
    specs = [pl.BlockSpec((1, c_out.shape[1], tm, c_out.shape[3]), lambda b, i: (b, 0, i, 0)),
             _row_spec(tm, d_out.shape[-1]), _const_spec(w_bf16.shape)]
    return _layer_tail(_mix1_y, (c_out, d_out, w_bf16), specs, tm=tm, name="tail1", **kw)


def _fft_kernel(u_ref, f1_ref, twc_ref, tws_ref, f2_ref, fw_ref, o_ref, u_sc, p_sc, y_sc,
                *, scale, n1, n2):
    pu = n2 + FFT_PAD
    pp = 2 * n1 + FFT_PAD
    py = n1 + FFT_PAD
    f1 = f1_ref[...].astype(BF16)
    f2 = f2_ref[...].astype(BF16)
    fw = fw_ref[...].astype(BF16)
    for i1 in range(n1):
        u_sc[i1 * pu:i1 * pu + n2, :] = u_ref[0, 0, i1 * n2:(i1 + 1) * n2, :]

    nb = FFT_BATCH

    def stage1(blk, carry):
        i2s = [blk * nb + j for j in range(nb)]
        x = jnp.concatenate([u_sc[pl.ds(i2, n1, stride=pu), :] for i2 in i2s], axis=1)
        p = jnp.dot(f1, x.astype(BF16), preferred_element_type=F32)
        for j, i2 in enumerate(i2s):
            p_sc[pl.ds(pl.multiple_of(i2 * pp, 8), 2 * n1), :] = p[:, j * C_WIDTH:(j + 1) * C_WIDTH]
        return carry

    lax.fori_loop(0, n2 // nb, stage1, 0, unroll=2)

    def stage2(blk, carry):
        k1s = [blk * nb + j for j in range(nb)]
        qr, qi = [], []
        for k1 in k1s:
            tc = twc_ref[k1]
            ts = tws_ref[k1]
            pr = p_sc[pl.ds(k1, n2, stride=pp), :]
            pim = p_sc[pl.ds(n1 + k1, n2, stride=pp), :]
            qr.append(pr * tc + pim * ts)
            qi.append(pim * tc - pr * ts)
        q = jnp.concatenate([jnp.concatenate(qr, axis=1), jnp.concatenate(qi, axis=1)], axis=0)
        xx = jnp.dot(f2, q.astype(BF16), preferred_element_type=F32)
        xg = jnp.concatenate(
            [jnp.concatenate([xx[:n2, j * C_WIDTH:(j + 1) * C_WIDTH],
                              xx[n2:, j * C_WIDTH:(j + 1) * C_WIDTH]], axis=1) for j in range(nb)],
            axis=0)
        y = jnp.dot(xg.astype(BF16), fw, preferred_element_type=F32) * scale
        for j, k1 in enumerate(k1s):
            y_sc[pl.ds(k1, n2, stride=py), :] = y[j * n2:(j + 1) * n2]
        return carry

    lax.fori_loop(0, n1 // nb, stage2, 0, unroll=2)
    for k2 in range(n2):
        o_ref[0, 0, k2 * n1:(k2 + 1) * n1, :] = y_sc[k2 * py:k2 * py + n1, :]


def _fourier_mixer(u):
    bsz, ngroups, seq, width = u.shape
    n2 = FFT_N2
    n1 = seq // n2
    assert n1 * n2 == seq and width == C_WIDTH and n1 % 8 == 0
    a1 = 2.0 * np.pi * np.outer(np.arange(n1), np.arange(n1)) / n1
    f1 = np.concatenate([np.cos(a1), -np.sin(a1)], axis=0)
    a2 = 2.0 * np.pi * np.outer(np.arange(n2), np.arange(n2)) / n2
    c2, s2 = np.cos(a2), np.sin(a2)
    f2 = np.block([[c2, s2], [-s2, c2]])
    aw = 2.0 * np.pi * np.outer(np.arange(C_WIDTH), np.arange(C_WIDTH)) / C_WIDTH
    fw = np.concatenate([np.cos(aw), np.sin(aw)], axis=0)
    at = np.repeat((2.0 * np.pi * np.outer(np.arange(n1), np.arange(n2)) / seq)[:, :, None],
                   C_WIDTH, axis=2)
    consts = (jnp.asarray(f1, F32), jnp.asarray(np.cos(at), F32), jnp.asarray(np.sin(at), F32),
              jnp.asarray(f2, F32), jnp.asarray(fw, F32))
    blk = pl.BlockSpec((1, 1, seq, C_WIDTH), lambda b, g: (b, g, 0, 0))
    return pl.pallas_call(
        functools.partial(_fft_kernel, scale=float(1.0 / np.sqrt(seq * C_WIDTH)), n1=n1, n2=n2),
        out_shape=jax.ShapeDtypeStruct(u.shape, F32),
        grid=(bsz, ngroups),
        in_specs=[blk] + [_const_spec(c.shape) for c in consts],
        out_specs=blk,
        scratch_shapes=[pltpu.VMEM((n1 * (n2 + FFT_PAD), C_WIDTH), F32),
                        pltpu.VMEM((n2 * (2 * n1 + FFT_PAD), C_WIDTH), F32),
                        pltpu.VMEM((n2 * (n1 + FFT_PAD), C_WIDTH), F32)],
        compiler_params=_cparams("parallel", "parallel"),
        name="fft",
    )(u, *consts)


def _head_perm():
    rep = D_Q_HEADS // D_KV_HEADS
    cols = []
    for j in range(rep):
        for g in range(D_KV_HEADS):
            h = g * rep + j
            cols.extend(range(h * D_HEAD_DIM, (h + 1) * D_HEAD_DIM))
    return np.asarray(cols, np.int32)


def _rope_tables(seq):
    rows = seq // GRID_W
    row = jnp.repeat(jnp.arange(rows, dtype=F32), GRID_W)
    col = jnp.tile(jnp.arange(GRID_W, dtype=F32), rows)
    axis_dim = D_HEAD_DIM // 2
    inv_freq = jnp.power(ROPE_THETA, -jnp.arange(0, axis_dim, 2, dtype=F32) / axis_dim)
    ang_r = row[:, None] * inv_freq[None, :]
    ang_c = col[:, None] * inv_freq[None, :]
    cr, sr, cc, sc = jnp.cos(ang_r), jnp.sin(ang_r), jnp.cos(ang_c), jnp.sin(ang_c)
    cos = jnp.concatenate([cr, cr, cc, cc], axis=1)
    sin = jnp.concatenate([-sr, sr, -sc, sc], axis=1)
    reps = LANES // D_HEAD_DIM
    return jnp.tile(cos, (1, reps)), jnp.tile(sin, (1, reps))


def _inproj_qk_kernel(x_ref, gain_ref, sc_ref, sh_ref, w_ref, cos_ref, sin_ref, bd_h_ref, bd_l_ref,
                      gq_ref, gk_ref, u_ref, qo_ref, ko_ref, vo_ref):
    h = _rms(x_ref[0], gain_ref[...]) * (1.0 + sc_ref[0]) + sh_ref[0]
    z = _dot(h, w_ref[...])
    ngrp = u_ref.shape[1]
    for g in range(ngrp):
        u_ref[0, g] = z[:, g * LANES:(g + 1) * LANES]
    cos = cos_ref[...]
    sin = sin_ref[...]
    quarter = D_HEAD_DIM // 4
    lane = lax.broadcasted_iota(jnp.int32, (1, LANES), 1)
    first_of_pair = (lane // quarter) % 2 == 0

    def norm_rope(x, gain, scale):
        ms = _dot_tab(bd_h_ref[...], bd_l_ref[...], x * x, tab_left=False)
        xn = x * lax.rsqrt(ms + EPS) * gain
        partner = jnp.where(first_of_pair, pltpu.roll(xn, LANES - quarter, 1),
                            pltpu.roll(xn, quarter, 1))
        return ((xn * cos + partner * sin) * scale).astype(BF16)

    nq = qo_ref.shape[-1] // LANES
    for j in range(nq):
        qo_ref[0, :, j * LANES:(j + 1) * LANES] = norm_rope(
            z[:, (ngrp + j) * LANES:(ngrp + j + 1) * LANES], gq_ref[...], D_HEAD_DIM ** -0.5 * LOG2E)
    ko_ref[0] = norm_rope(z[:, (ngrp + nq) * LANES:(ngrp + nq + 1) * LANES], gk_ref[...], 1.0)
    vt = z[:, (ngrp + nq + 1) * LANES:(ngrp + nq + 2) * LANES].T
    ones = jnp.ones((FLASH_ONES, vt.shape[1]), F32)
    vo_ref[0] = jnp.concatenate(
        [piece for g in range(D_KV_HEADS)
         for piece in (vt[g * D_HEAD_DIM:(g + 1) * D_HEAD_DIM], ones)], axis=0).astype(BF16)


def _inproj_qk(x, gain, sc, sh, w_bf16, qk_norm_j, tm):
    bsz, seq, d = x.shape
    n = w_bf16.shape[1]
    qw = D_Q_HEADS * D_HEAD_DIM
    kw = D_KV_HEADS * D_HEAD_DIM
    assert kw == LANES and n == C_GROUPS * C_WIDTH + qw + 2 * kw
    vrows = D_KV_HEADS * (D_HEAD_DIM + FLASH_ONES)
    cos, sin = _rope_tables(seq)
    bd = np.kron(np.eye(LANES // D_HEAD_DIM), np.full((D_HEAD_DIM, D_HEAD_DIM), 1.0 / D_HEAD_DIM))
    bd_h, bd_l = _np_split2(bd)
    reps = LANES // D_HEAD_DIM
    gq = jnp.tile(qk_norm_j[0].astype(F32), reps).reshape(1, LANES)
    gk = jnp.tile(qk_norm_j[1].astype(F32), reps).reshape(1, LANES)
    tab = pl.BlockSpec((tm, LANES), lambda b, i: (i, 0))
    vec = pl.BlockSpec((1, 1, d), lambda b, i: (b, 0, 0))
    return pl.pallas_call(
        _inproj_qk_kernel,
        out_shape=(jax.ShapeDtypeStruct((bsz, C_GROUPS, seq, C_WIDTH), F32),
                   jax.ShapeDtypeStruct((bsz, seq, qw), BF16),
                   jax.ShapeDtypeStruct((bsz, seq, kw), BF16),
                   jax.ShapeDtypeStruct((bsz, vrows, seq), BF16)),
        grid=(bsz, seq // tm),
        in_specs=[pl.BlockSpec((1, tm, d), lambda b, i: (b, i, 0)),
                  _const_spec((1, d)), vec, vec, _const_spec((d, n)),
                  tab, tab, _const_spec(bd_h.shape), _const_spec(bd_l.shape),
                  _const_spec((1, LANES)), _const_spec((1, LANES))],
        out_specs=(pl.BlockSpec((1, C_GROUPS, tm, C_WIDTH), lambda b, i: (b, 0, i, 0)),
                   _row_spec(tm, qw), _row_spec(tm, kw),
                   pl.BlockSpec((1, vrows, tm), lambda b, i: (b, 0, i))),
        compiler_params=_cparams("parallel", "parallel"),
        name="inproj_qk",
    )(x, gain.reshape(1, d), sc.reshape(bsz, 1, d), sh.reshape(bsz, 1, d), w_bf16,
      cos, sin, bd_h, bd_l, gq, gk)


def _flash_kernel(q_ref, k_ref, vt_ref, o_ref, m_sc, acc_sc, s_sc):
    kv = pl.program_id(2)

    @pl.when(kv == 0)
    def _():
        m_sc[...] = jnp.full_like(m_sc, -jnp.inf)
        acc_sc[...] = jnp.zeros_like(acc_sc)

    lane = lax.broadcasted_iota(jnp.int32, (1, LANES), 1)
    lo = lane < D_HEAD_DIM
    nblk = q_ref.shape[-1] // LANES
    tq, tk = q_ref.shape[1], k_ref.shape[1]
    ku, qu = FLASH_KEY_UNIT, FLASH_QUERY_UNIT
    grows = D_HEAD_DIM + FLASH_ONES
    nheads = nblk * D_KV_HEADS
    k = k_ref[0]

    nchunk = tq // qu

    def logits_chunk(idx, c):
        j, g = divmod(idx, D_KV_HEADS)
        qj = q_ref[0, c * qu:(c + 1) * qu, j * LANES:(j + 1) * LANES]
        sel = lo if g == 0 else jnp.logical_not(lo)
        s = lax.dot_general(k, jnp.where(sel, qj, jnp.zeros_like(qj)), (((1,), (1,)), ((), ())),
                            preferred_element_type=F32)
        s_sc[idx % s_sc.shape[0], :, c * qu:(c + 1) * qu] = s
        return jnp.max(s, axis=0, keepdims=True)

    def finish_logits(idx, mcs):
        m_prev = m_sc[idx, 0:1, :]
        m_new = jnp.maximum(m_prev, jnp.concatenate(mcs, axis=1))
        m_sc[idx, 0:1, :] = m_new
        return m_new, jnp.exp2(m_prev - m_new)

    def value_chunk(idx, c, m_new, alpha):
        j, g = divmod(idx, D_KV_HEADS)
        rows = slice(g * grows, (g + 1) * grows)
        qcols = slice(c * qu, (c + 1) * qu)
        pv = None
        for u in range(tk // ku):
            keys = slice(u * ku, (u + 1) * ku)
            p = jnp.exp2(s_sc[idx % s_sc.shape[0], keys, qcols] - m_new[:, qcols])
            d = jnp.dot(vt_ref[0, rows, keys], p.astype(BF16), preferred_element_type=F32)
            pv = d if pv is None else pv + d
        acc_sc[j, rows, qcols] = alpha[:, qcols] * acc_sc[j, rows, qcols] + pv

    def logits_pass(idx):
        return finish_logits(idx, [logits_chunk(idx, c) for c in range(nchunk)])

    pending = [logits_pass(i) for i in range(min(FLASH_AHEAD, nheads))]
    for idx in range(nheads):
        if idx + FLASH_AHEAD < nheads:
            pending.append(logits_pass(idx + FLASH_AHEAD))
        stats = pending.pop(0)
        for c in range(nchunk):
            value_chunk(idx, c, *stats)

    @pl.when(kv == pl.num_programs(2) - 1)
    def _():
        for j in range(nblk):
            parts = []
            for g in range(D_KV_HEADS):
                num = acc_sc[j, g * grows:g * grows + D_HEAD_DIM, :]
                den = acc_sc[j, g * grows + D_HEAD_DIM:g * grows + D_HEAD_DIM + 1, :]
                parts.append(num / den)
            o_ref[0, :, j * LANES:(j + 1) * LANES] = (
                jnp.concatenate(parts, axis=0).T.astype(o_ref.dtype))


def _flash(q, k, vt, tq, tk):
    bsz, seq, qw = q.shape
    kw = k.shape[-1]
    vrows = vt.shape[1]
    return pl.pallas_call(
        _flash_kernel,
        out_shape=jax.ShapeDtypeStruct((bsz, seq, qw), BF16),
        grid=(bsz, seq // tq, seq // tk),
        in_specs=[pl.BlockSpec((1, tq, qw), lambda b, i, j: (b, i, 0)),
                  pl.BlockSpec((1, tk, kw), lambda b, i, j: (b, j, 0)),
                  pl.BlockSpec((1, vrows, tk), lambda b, i, j: (b, 0, j))],
        out_specs=pl.BlockSpec((1, tq, qw), lambda b, i, j: (b, i, 0)),
        scratch_shapes=[pltpu.VMEM((D_Q_HEADS, 8, tq), F32),
                        pltpu.VMEM((qw // LANES, vrows, tq), F32),
                        pltpu.VMEM((FLASH_LOGIT_BUFS, tk, tq), F32)],
        compiler_params=_cparams("parallel", "parallel", "arbitrary"),
        name="flash",
    )(q, k, vt)


def kernel(x, c, t5_bias, hgrn_lb_logits, ada_w, ada_b, norm_gains, ab_w_in, ab_w_out,
           hgrn_out_norm, cd_w_in, cd_w_out, qk_norm, ffn_w_in, ffn_w_out):
    bsz, seq, d = x.shape
    depth = ada_w.shape[0]
    mod = _ada_mod(c.astype(F32), ada_w, ada_b)
    perm = _head_perm()
    aw = A_HEADS * A_HEAD_DIM
    cw = C_GROUPS * C_WIDTH
    qw = D_Q_HEADS * D_HEAD_DIM
    tm_in = min(512, seq)
    ffn_wi = ffn_w_in.astype(BF16)
    ffn_wo = ffn_w_out.astype(BF16)
    for layer in range(depth):
        sh_m, sc_m, g_m, sh_f, sc_f, g_f = [mod[layer, :, i * d:(i + 1) * d] for i in range(6)]
        gains = norm_gains[layer]
        j = layer // 2
        tail = dict(x=x, gain_m=gains[1], gate_m=g_m, g1=gains[2], sc=sc_f, sh=sh_f,
                    wi_bf16=ffn_wi, wo_bf16=ffn_wo, layer=layer, g2=gains[3], gate_f=g_f)
        if layer % 2 == 0:
            w_in = ab_w_in[j].astype(BF16)
            *qkv_cm, z = _inproj_cm(x, gains[0], sc_m, sh_m, w_in, 3 * aw, tm_in)
            branches = [_dilated_branch(cm, t5_bias, window, dil)
                        for cm, (window, dil) in zip(qkv_cm, DIL_CFG)]
            o_f, o_b = _hgrn(z, hgrn_lb_logits, layer, 0)
            g_col = 3 * B_HEADS * B_DK + B_HEADS * B_DV
            x = _tail0(branches, o_f, o_b, z, g_col, hgrn_out_norm[j], ab_w_out[j].astype(BF16),
                       tm=min(TAIL_SUB, seq), **tail)
        else:
            w_full = cd_w_in[j]
            w_in = jnp.concatenate([w_full[:, :cw], w_full[:, cw:cw + qw][:, perm],
                                    w_full[:, cw + qw:]], axis=1).astype(BF16)
            u, qn, kn, vn = _inproj_qk(x, gains[0], sc_m, sh_m, w_in, qk_norm[j], min(1024, seq))
            c_out = _fourier_mixer(u)
            d_out = _flash(qn, kn, vn, min(FLASH_TQ, seq), min(FLASH_TK, seq))
            w_out_full = cd_w_out[j]
            w_out = jnp.concatenate([w_out_full[:cw], w_out_full[cw:][perm]], axis=0).astype(BF16)
            x = _tail1(c_out, d_out, w_out, tm=min(TAIL_BLOCK, seq), **tail)
    return x
```

```python
import functools

import numpy as np
import jax
import jax.numpy as jnp
from jax import lax
from jax.experimental import pallas as pl
from jax.experimental.pallas import tpu as pltpu

F32 = jnp.float32
BF16 = jnp.bfloat16
LANES = 128
MXU_WIDTH = 256
VMEM_LIMIT_BYTES = 56 * 2**20
NEG_INF = -1e30
EPS = 1e-6

GRID_W = 64
A_HEADS = 8
A_HEAD_DIM = 64
DIL_CFG = ((128, 1), (512, 4), (2048, 16))
N_BUCKETS = 32
T5_MAX_DIST = 1024
B_HEADS = 4
B_DK = 128
B_DV = 128
C_GROUPS = 4
C_WIDTH = 128
D_Q_HEADS = 8
D_KV_HEADS = 2
D_HEAD_DIM = 64
ROPE_THETA = 10000.0

DIL_TQ = 128
DIL_TILE = 512
HGRN_BLOCK = 1024
HGRN_SUB = 256
HGRN_BCAST_MIN = 8
FLASH_TQ = 1024
FLASH_TK = 4096
FLASH_LOGIT_BUFS = 2
FLASH_KEY_UNIT = 256
FLASH_QUERY_UNIT = 1024
FLASH_AHEAD = 1
FLASH_ONES = 16
TAIL_BLOCK = 512
TAIL_SUB = 512
FFT_N2 = 128
FFT_BATCH = 8
FFT_PAD = 8
LOG2E = 1.4426950408889634


def _cparams(*sem):
    return pltpu.CompilerParams(dimension_semantics=sem, vmem_limit_bytes=VMEM_LIMIT_BYTES)


def _const_spec(shape):
    nd = len(shape)
    return pl.BlockSpec(shape, lambda *_: (0,) * nd, pipeline_mode=pl.Buffered(1))


def _sigmoid(x):
    return 1.0 / (1.0 + jnp.exp(-x))


def _dot(a, b):
    return jnp.dot(a.astype(BF16), b.astype(BF16), preferred_element_type=F32)


def _dot_nt(a, b):
    return lax.dot_general(a.astype(BF16), b.astype(BF16), (((1,), (1,)), ((), ())),
                           preferred_element_type=F32)


def _split2(a):
    hi = a.astype(BF16)
    lo = (a - hi.astype(F32)).astype(BF16)
    return hi, lo


def _split3(a):
    a1 = a.astype(BF16)
    r = a - a1.astype(F32)
    a2 = r.astype(BF16)
    a3 = (r - a2.astype(F32)).astype(BF16)
    return a1, a2, a3


def _dot_tab(tab_hi, tab_lo, x, *, tab_left):
    x_hi, x_lo = _split2(x)
    if tab_left:
        d = lambda t, v: jnp.dot(t, v, preferred_element_type=F32)
    else:
        d = lambda t, v: jnp.dot(v, t, preferred_element_type=F32)
    return d(tab_hi, x_hi) + (d(tab_hi, x_lo) + d(tab_lo, x_hi))


def _rms(x, gain):
    ms = jnp.mean(x * x, axis=-1, keepdims=True)
    return x * lax.rsqrt(ms + EPS) * gain


def _np_split2(t):
    t = np.asarray(t, np.float32)
    hi = jnp.asarray(t, F32).astype(BF16)
    lo = (jnp.asarray(t, F32) - hi.astype(F32)).astype(BF16)
    return hi, lo


def _mod_kernel(c_ref, w_ref, b_ref, o_ref):
    c = c_ref[...]
    o_ref[0] = _dot(c * _sigmoid(c), w_ref[0]) + b_ref[0]


def _ada_mod(c, ada_w, ada_b):
    depth, d, n6 = ada_w.shape
    bsz = c.shape[0]
    rows = 8
    cp = jnp.zeros((rows, d), F32).at[:bsz].set(c)
    tn = n6 // 4
    out = pl.pallas_call(
        _mod_kernel,
        out_shape=jax.ShapeDtypeStruct((depth, rows, n6), F32),
        grid=(depth, n6 // tn),
        in_specs=[pl.BlockSpec((rows, d), lambda l, j: (0, 0)),
                  pl.BlockSpec((1, d, tn), lambda l, j: (l, 0, j)),
                  pl.BlockSpec((1, 1, tn), lambda l, j: (l, 0, j))],
        out_specs=pl.BlockSpec((1, rows, tn), lambda l, j: (l, 0, j)),
        compiler_params=_cparams("parallel", "parallel"),
        name="ada_mod",
    )(cp, ada_w, ada_b.reshape(depth, 1, n6))
    return out[:, :bsz]


def _inproj_cm_kernel(x_ref, gain_ref, sc_ref, sh_ref, w_ref, *refs):
    cm_refs, rest_ref, zs_sc, zc_sc = refs[:-3], refs[-3], refs[-2], refs[-1]
    h = _rms(x_ref[0], gain_ref[...]) * (1.0 + sc_ref[0]) + sh_ref[0]
    z = _dot(h, w_ref[...])
    nblk, tm, _ = zs_sc.shape
    rest_ref[0] = z[:, nblk * LANES:]
    nq = A_HEADS * A_HEAD_DIM // LANES
    for c in range(nblk):
        blk = z[:, c * LANES:(c + 1) * LANES]
        zs_sc[c] = blk * (A_HEAD_DIM ** -0.5 * LOG2E) if c < nq else blk
    src, sd = zs_sc, 1
    for level, (cm_ref, (_, dil)) in enumerate(zip(cm_refs, DIL_CFG)):
        step, n = dil // sd, tm // dil
        keep = dil > 1 and level + 1 < len(DIL_CFG)
        for rs in range(sd):
            for cc in range(step):
                r = rs + sd * cc
                for c in range(nblk):
                    rows = src[c, pl.ds(rs * (tm // sd) + cc, n, stride=step), :]
                    cm_ref[0, r, :, c * LANES:(c + 1) * LANES] = rows.astype(BF16)
                    if keep:
                        zc_sc[c, r * n:(r + 1) * n, :] = rows
        if keep:
            src, sd = zc_sc, dil


def _inproj_cm(x, gain, sc, sh, w_bf16, na, tm):
    bsz, seq, d = x.shape
    n = w_bf16.shape[1]
    vec = pl.BlockSpec((1, 1, d), lambda b, i: (b, 0, 0))
    dils = [dl for _, dl in DIL_CFG]
    assert dils[0] == 1 and all(b % a == 0 for a, b in zip(dils, dils[1:]))
    return pl.pallas_call(
        _inproj_cm_kernel,
        out_shape=tuple([jax.ShapeDtypeStruct((bsz, dl, seq // dl, na), BF16) for dl in dils]
                        + [jax.ShapeDtypeStruct((bsz, seq, n - na), F32)]),
        grid=(bsz, seq // tm),
        in_specs=[pl.BlockSpec((1, tm, d), lambda b, i: (b, i, 0)),
                  _const_spec((1, d)), vec, vec, _const_spec((d, n))],
        out_specs=tuple([pl.BlockSpec((1, dl, tm // dl, na), lambda b, i: (b, 0, i, 0)) for dl in dils]
                        + [pl.BlockSpec((1, tm, n - na), lambda b, i: (b, i, 0))]),
        scratch_shapes=[pltpu.VMEM((na // LANES, tm, LANES), F32),
                        pltpu.VMEM((na // LANES, tm, LANES), F32)],
        compiler_params=_cparams("parallel", "parallel"),
        name="inproj_cm",
    )(x, gain.reshape(1, d), sc.reshape(bsz, 1, d), sh.reshape(bsz, 1, d), w_bf16)


def _t5_buckets(rel):
    half = N_BUCKETS // 2
    max_exact = half // 2
    n = np.abs(rel)
    large = max_exact + (np.log(np.maximum(n, 1) / max_exact) / np.log(T5_MAX_DIST / max_exact)
                         * (half - max_exact)).astype(np.int32)
    large = np.minimum(large, half - 1)
    return (np.where(rel > 0, half, 0) + np.where(n < max_exact, n, large)).astype(np.int32)


def _dil_bias(t5_bias, window, dil, tq):
    half = (window // 2) // dil
    assert half == tq // 2
    rel = np.arange(2 * tq)[None, :] - half - np.arange(tq)[:, None]
    inside = np.abs(rel) <= half
    buckets = _t5_buckets(np.where(inside, rel, 0) * dil)
    onehot =jnp.asarray(np.eye(N_BUCKETS, dtype=np.float32)[buckets])
    bias = jnp.einsum("qkn,nh->hqk", onehot, t5_bias.astype(F32), precision=lax.Precision.HIGHEST)
    return jnp.where(jnp.asarray(inside)[None], bias * LOG2E, NEG_INF)


def _dil_kernel(q_ref, kp_ref, kc_ref, kn_ref, vp_ref, vc_ref, vn_ref, bias_ref, o_ref, lse_ref,
                *, class_len):
    i = pl.program_id(2)
    sub, hq, tile = DIL_TQ, DIL_TQ // 2, q_ref.shape[2]
    kwin = jnp.concatenate([kp_ref[0, 0], kc_ref[0, 0], kn_ref[0, 0]], axis=0)
    vwin = jnp.concatenate([vp_ref[0, 0], vc_ref[0, 0], vn_ref[0, 0]], axis=0)
    lane = lax.broadcasted_iota(jnp.int32, (1, LANES), 1)
    lo = lane < A_HEAD_DIM
    nblk = A_HEADS // 2
    units = [(jt, j) for jt in range(tile // sub) for j in range(nblk)]
    logits = []
    for jt, j in units:
        cols = slice(j * LANES, (j + 1) * LANES)
        qj = q_ref[0, 0, jt * sub:(jt + 1) * sub, cols]
        zero = jnp.zeros_like(qj)
        q2 = jnp.concatenate([jnp.where(lo, qj, zero), jnp.where(lo, zero, qj)], axis=0)
        s = lax.dot_general(q2, kwin[jt * sub:jt * sub + 2 * sub, cols], (((1,), (1,)), ((), ())),
                            preferred_element_type=F32)
        kpos = i * tile + jt * sub - hq + lax.broadcasted_iota(jnp.int32, (1, 2 * sub), 1)
        valid = jnp.logical_and(kpos >= 0, kpos < class_len)
        logits.append(jnp.where(valid, s + bias_ref[j], NEG_INF))
    s_all = jnp.concatenate(logits, axis=0)
    m = jnp.max(s_all, axis=-1, keepdims=True)
    p32 = jnp.exp2(s_all - m)
    l = jnp.sum(p32, axis=-1, keepdims=True)
    p = p32.astype(BF16)
    rinv = 1.0 / l
    lse = m + jnp.log2(l)
    for jt in range(tile // sub):
        lse_all = jnp.zeros((sub, LANES), F32)
        for j in range(nblk):
            cols = slice(j * LANES, (j + 1) * LANES)
            r0 = (jt * nblk + j) * 2 * sub
            o2 = jnp.dot(p[r0:r0 + 2 * sub], vwin[jt * sub:jt * sub + 2 * sub, cols],
                         preferred_element_type=F32) * rinv[r0:r0 + 2 * sub]
            lse_all = jnp.where(lane == 2 * j, lse[r0:r0 + sub], lse_all)
            lse_all = jnp.where(lane == 2 * j + 1, lse[r0 + sub:r0 + 2 * sub], lse_all)
            o_ref[0, 0, jt * sub:(jt + 1) * sub, cols] = (
                jnp.where(lo, o2[:sub], o2[sub:]).astype(o_ref.dtype))
        lse_ref[0, 0, jt * sub:(jt + 1) * sub, :] = lse_all


def _dilated_branch(qkv_cm, t5_bias, window, dil):
    bsz, _, cl, width = qkv_cm.shape
    aw = A_HEADS * A_HEAD_DIM
    tile, hq = min(DIL_TILE, cl), DIL_TQ // 2
    nt = cl // tile
    per = tile // hq
    nh = cl // hq

    def cur(col):
        return pl.BlockSpec((1, 1, tile, aw), lambda b, r, i: (b, r, i, col))

    def prev(col):
        return pl.BlockSpec((1, 1, hq, aw), lambda b, r, i: (b, r, jnp.maximum(i * per - 1, 0), col))

    def nxt(col):
        return pl.BlockSpec((1, 1, hq, aw),
                            lambda b, r, i: (b, r, jnp.minimum((i + 1) * per, nh - 1), col))

    return pl.pallas_call(
        functools.partial(_dil_kernel, class_len=cl),
        out_shape=(jax.ShapeDtypeStruct((bsz, dil, cl, aw), BF16),
                   jax.ShapeDtypeStruct((bsz, dil, cl, LANES), F32)),
        grid=(bsz, dil, nt),
        in_specs=[cur(0), prev(1), cur(1), nxt(1), prev(2), cur(2), nxt(2),
                  _const_spec((A_HEADS // 2, 2 * DIL_TQ, 2 * DIL_TQ))],
        out_specs=(pl.BlockSpec((1, 1, tile, aw), lambda b, r, i: (b, r, i, 0)),
                   pl.BlockSpec((1, 1, tile, LANES), lambda b, r, i: (b, r, i, 0))),
        compiler_params=_cparams("parallel", "parallel", "parallel"),
        name=f"dilated_d{dil}",
    )(*([qkv_cm] * 7),
      _dil_bias(t5_bias, window, dil, DIL_TQ).reshape(A_HEADS // 2, 2 * DIL_TQ, 2 * DIL_TQ))


def _merge_branches(a_refs, l_refs, e_ref, a_scs, l_scs, sub):
    tm = l_scs[0].shape[0]
    accs, lses = [], []
    for (_, dil), a_ref, l_ref, a_sc, l_sc in zip(DIL_CFG, a_refs, l_refs, a_scs, l_scs):
        nblk = a_sc.shape[0]
        for r in range(dil):
            rows = pl.ds(r, tm // dil, stride=dil)
            src = pl.ds(sub * (tm // dil), tm // dil)
            for c in range(nblk):
                a_sc[c, rows, :] = a_ref[0, r, src, c * LANES:(c + 1) * LANES].astype(F32)
            l_sc[rows, :] = l_ref[0, r, src, :]
        accs.append(jnp.concatenate([a_sc[c] for c in range(nblk)], axis=1))
        lses.append(l_sc[...])
    mx = functools.reduce(jnp.maximum, lses)
    ws = [jnp.exp2(x - mx) for x in lses]
    tot = functools.reduce(jnp.add, ws)
    out = None
    for w, a in zip(ws, accs):
        w_hi, w_lo = _split2(w / tot)
        wide = (jnp.dot(w_hi, e_ref[...], preferred_element_type=F32)
                + jnp.dot(w_lo, e_ref[...], preferred_element_type=F32))
        out = wide * a if out is None else out + wide * a
    return out


def _merge_operands(branch_outs, tm):
    aw = branch_outs[0][0].shape[-1]
    expand = np.zeros((LANES, aw), np.float32)
    for h in range(A_HEADS):
        expand[h, h * A_HEAD_DIM:(h + 1) * A_HEAD_DIM] = 1.0
    dils = [d for _, d in DIL_CFG]
    specs = ([pl.BlockSpec((1, d, tm // d, aw), lambda b, i: (b, 0, i, 0)) for d in dils]
             + [pl.BlockSpec((1, d, tm // d, LANES), lambda b, i: (b, 0, i, 0)) for d in dils]
             + [_const_spec(expand.shape)])
    arrays = [o for o, _ in branch_outs] + [l for _, l in branch_outs] + [jnp.asarray(expand, BF16)]
    scratch = ([pltpu.VMEM((aw // LANES, TAIL_SUB, LANES), F32) for _ in dils]
               + [pltpu.VMEM((TAIL_SUB, LANES), F32) for _ in dils])
    return arrays, specs, scratch


def _hgrn_consts(reverse):
    t = HGRN_SUB
    r = np.arange(t)
    u = r[None, :]
    row = r[:, None]
    nmats = [(u >= row) if reverse else (u <= row)]
    masks = []
    m = t // 2
    while m >= 1:
        grp = r // (2 * m)
        in_first = (r % (2 * m)) < m
        same = grp[:, None] == grp[None, :]
        if reverse:
            beta = (grp * 2 * m + m)[:, None]
            n = np.where(in_first[:, None], (u >= row) & (u < beta), (u >= beta) & (u < row))
            mask = same & in_first[:, None] & ~in_first[None, :]
        else:
            beta = (grp * 2 * m + m - 1)[:, None]
            n = np.where(in_first[:, None], (u > row) & (u <= beta), (u > beta) & (u <= row))
            mask = same & ~in_first[:, None] & in_first[None, :]
        if m < HGRN_BCAST_MIN:
            nmats.append(n)
        masks.append(mask)
        m //= 2
    masks.append(np.eye(t, dtype=bool))
    nmat = jnp.asarray(np.concatenate(nmats, axis=0), F32).astype(BF16)
    return nmat, jnp.asarray(np.stack(masks), F32)


def _hgrn_block(q, f, v, lb, st, nmat_ref, mask_ref, reverse):
    t = HGRN_SUB
    nlev = mask_ref.shape[0] - 1
    qs = q * (B_DK ** -0.5)
    fa = lb + (1.0 - lb) * _sigmoid(f)
    kk = 1.0 - fa
    g_hi, g_lo = _split2(jnp.log2(fa))
    ex = jnp.dot(nmat_ref[...], jnp.concatenate([g_hi, g_lo], axis=1), preferred_element_type=F32)
    ex = ex[:, :B_DK] + ex[:, B_DK:]
    b = ex[:t]
    btot = b[0:1] if reverse else b[t - 1:t]
    a = mask_ref[nlev] * _dot_nt(qs, kk)
    fine = 1
    for l in range(nlev):
        m = t >> (l + 1)
        if m >= HGRN_BCAST_MIN:
            ref = jnp.concatenate(
                [jnp.broadcast_to(b[beta:beta + 1], (2 * m, B_DK))
                 for beta in range(m if reverse else m - 1, t, 2 * m)], axis=0)
            e = jnp.exp2(-jnp.abs(b - ref))
        else:
            e = jnp.exp2(ex[fine * t:(fine + 1) * t])
            fine += 1
        a = a + mask_ref[l] * _dot_nt(qs * e, kk * e)
    out = _dot(a, v) + _dot_nt(qs * jnp.exp2(b), st)
    khat = (kk * jnp.exp2(btot - b)).astype(BF16)
    st_new = st * jnp.exp2(btot) + jnp.dot(v.T.astype(BF16), khat, preferred_element_type=F32)
    return out, st_new


def _hgrn_lb(lg_ref, layer):
    lg = [lg_ref[l, 0] for l in range(lg_ref.shape[0])]
    mx = functools.reduce(jnp.maximum, lg)
    e = [jnp.exp(x - mx) for x in lg]
    return functools.reduce(jnp.add, e[:layer + 1]) / functools.reduce(jnp.add, e)


def _hgrn_kernel(qf_ref, ff_ref, vf_ref, qb_ref, fb_ref, vb_ref, lgf_ref, lgb_ref,
                 nf_ref, mf_ref, nb_ref, mb_ref, of_ref, ob_ref, sf_sc, sb_sc, *, layer):
    @pl.when(pl.program_id(2) == 0)
    def _():
        sf_sc[...] = jnp.zeros_like(sf_sc)
        sb_sc[...] = jnp.zeros_like(sb_sc)

    nsub = HGRN_BLOCK // HGRN_SUB
    chains = ((qf_ref, ff_ref, vf_ref, lgf_ref, sf_sc, nf_ref, mf_ref, of_ref, False),
              (qb_ref, fb_ref, vb_ref, lgb_ref, sb_sc, nb_ref, mb_ref, ob_ref, True))
    for q_ref, f_ref, v_ref, lg_ref, st_sc, n_ref, m_ref, o_ref, reverse in chains:
        lb = _hgrn_lb(lg_ref, layer)
        st = st_sc[...]
        for sub in (reversed(range(nsub)) if reverse else range(nsub)):
            rows = pl.ds(sub * HGRN_SUB, HGRN_SUB)
            o, st = _hgrn_block(q_ref[0, rows, :], f_ref[0, rows, :], v_ref[0, rows, :], lb, st,
                                n_ref, m_ref, reverse)
            o_ref[0, rows, :] = o.astype(o_ref.dtype)
        st_sc[...] = st


def _hgrn(z, lb_logits, layer, col0):
    bsz, seq, _ = z.shape
    t = HGRN_BLOCK
    nb = seq // t
    c0 = col0 // LANES
    hw = B_HEADS

    def zspec(group, rev):
        return pl.BlockSpec(
            (1, t, LANES),
            lambda b, h, j: (b, (nb - 1 - j) if rev else j, c0 + group * hw + h))

    def lgspec(direction):
        return pl.BlockSpec((lb_logits.shape[0], 1, 1, LANES),
                            lambda b, h, j: (0, direction * hw + h, 0, 0))

    nf, mf = _hgrn_consts(False)
    nbw, mbw = _hgrn_consts(True)
    lg = lb_logits.astype(F32).reshape(lb_logits.shape[0], 2 * hw, 1, LANES)
    o_shape = jax.ShapeDtypeStruct((bsz, seq, hw * B_DV), BF16)
    return pl.pallas_call(
        functools.partial(_hgrn_kernel, layer=layer),
        out_shape=(o_shape, o_shape),
        grid=(bsz, hw, nb),
        in_specs=[zspec(0, False), zspec(1, False), zspec(3, False),
                  zspec(0, True), zspec(2, True), zspec(3, True),
                  lgspec(0), lgspec(1),
                  _const_spec(nf.shape), _const_spec(mf.shape),
                  _const_spec(nbw.shape), _const_spec(mbw.shape)],
        out_specs=(pl.BlockSpec((1, t, LANES), lambda b, h, j: (b, j, h)),
                   pl.BlockSpec((1, t, LANES), lambda b, h, j: (b, nb - 1 - j, h))),
        scratch_shapes=[pltpu.VMEM((B_DV, B_DK), F32), pltpu.VMEM((B_DV, B_DK), F32)],
        compiler_params=_cparams("parallel", "parallel", "arbitrary"),
        name="hgrn",
    )(z, z, z, z, z, z, lg, lg, nf, mf, nbw, mbw)


def _mix0_y(sub, *refs):
    nbr = len(DIL_CFG)
    a_refs, l_refs, e_ref = refs[:nbr], refs[nbr:2 * nbr], refs[2 * nbr]
    of_ref, ob_ref, g_ref, on_ref, w_ref = refs[2 * nbr + 1:2 * nbr + 6]
    scratch = refs[2 * nbr + 6:]
    a = _merge_branches(a_refs, l_refs, e_ref, scratch[:nbr], scratch[nbr:], sub)
    rows = pl.ds(sub * TAIL_SUB, TAIL_SUB)
    o = of_ref[0, rows, :].astype(F32) + ob_ref[0, rows, :].astype(F32)
    g = g_ref[0, rows, :]
    parts = [_rms(o[:, h * B_DV:(h + 1) * B_DV], on_ref[...]) for h in range(B_HEADS)]
    bn = jnp.concatenate(parts, axis=-1) * (g * _sigmoid(g))
    na = a.shape[-1]
    return _dot(a, w_ref[:na, :]) + _dot(bn, w_ref[na:, :])


def _mix1_y(sub, c_ref, d_ref, w_ref):
    rows = pl.ds(sub * TAIL_SUB, TAIL_SUB)
    c = jnp.concatenate([c_ref[0, g, rows, :] for g in range(c_ref.shape[1])], axis=1)
    nc = c.shape[-1]
    return _dot(c, w_ref[:nc, :]) + _dot(d_ref[0, rows, :], w_ref[nc:, :])


def _row_spec(tm, width, col=0):
    return pl.BlockSpec((1, tm, width), lambda b, i: (b, i, col))


def _tail_kernel(*refs, nmix, mix_fn, bounds):
    (x_ref, gm_ref, gatem_ref, g1_ref, sc_ref, sh_ref, wi_ref, wo_ref, g2_ref, gatef_ref,
     o_ref) = refs[nmix:nmix + 11]
    hidden = wo_ref.shape[0]
    for sub in range(x_ref.shape[1] // TAIL_SUB):
        rows = pl.ds(sub * TAIL_SUB, TAIL_SUB)
        y_mix = mix_fn(sub, *refs[:nmix], *refs[nmix + 11:])
        x1 = x_ref[0, rows, :] + gatem_ref[0] * _rms(y_mix, gm_ref[...])
        h = (_rms(x1, g1_ref[...]) * (1.0 + sc_ref[0]) + sh_ref[0]).astype(BF16)
        y = None
        for c0, c1 in zip(bounds, bounds[1:]):
            gt = jnp.dot(h, wi_ref[:, c0:c1], preferred_element_type=F32)
            up = jnp.dot(h, wi_ref[:, hidden + c0:hidden + c1], preferred_element_type=F32)
            part = _dot(gt * _sigmoid(gt) * up, wo_ref[c0:c1, :])
            y = part if y is None else y + part
        o_ref[0, rows, :] = x1 + gatef_ref[0] * _rms(y, g2_ref[...])


def _layer_tail(mix_fn, mix_args, mix_specs, x, gain_m, gate_m, g1, sc, sh, wi_bf16, wo_bf16, layer,
                g2, gate_f, tm, name, mix_scratch=()):
    bsz, seq, d = x.shape
    vec = pl.BlockSpec((1, 1, d), lambda b, i: (b, 0, 0))
    hidden = wo_bf16.shape[1]

    def layer_slab(w):
        return pl.BlockSpec((None,) + w.shape[1:], lambda b, i: (layer, 0, 0),
                            pipeline_mode=pl.Buffered(1))

    ntile = hidden // MXU_WIDTH
    assert ntile * MXU_WIDTH == hidden
    bounds = (0, (ntile + 1) // 2 * MXU_WIDTH, hidden)
    row = lambda v: v.reshape(1, d)
    per_batch = lambda v: v.reshape(bsz, 1, d)
    return pl.pallas_call(
        functools.partial(_tail_kernel, nmix=len(mix_args), mix_fn=mix_fn, bounds=bounds),
        out_shape=jax.ShapeDtypeStruct(x.shape, F32),
        grid=(bsz, seq // tm),
        in_specs=list(mix_specs) + [_row_spec(tm, d), _const_spec((1, d)), vec, _const_spec((1, d)),
                                    vec, vec, layer_slab(wi_bf16), layer_slab(wo_bf16),
                                    _const_spec((1, d)), vec],
        out_specs=_row_spec(tm, d),
        scratch_shapes=list(mix_scratch),
        compiler_params=_cparams("parallel", "parallel"),
        name=name,
    )(*mix_args, x, row(gain_m), per_batch(gate_m), row(g1), per_batch(sc), per_batch(sh),
      wi_bf16, wo_bf16, row(g2), per_batch(gate_f))


def _tail0(branches, o_f, o_b, z, g_col, out_norm, w_bf16, tm, **kw):
    wv = B_HEADS * B_DV
    m_arrays, m_specs, m_scratch = _merge_operands(branches, tm)
    specs = m_specs + [_row_spec(tm, wv), _row_spec(tm, wv), _row_spec(tm, wv, g_col // wv),
                       _const_spec((1, B_DV)), _const_spec(w_bf16.shape)]
    args = m_arrays + [o_f, o_b, z, out_norm.reshape(1, B_DV), w_bf16]
    return _layer_tail(_mix0_y, args, specs, tm=tm, name="tail0", mix_scratch=m_scratch, **kw)


def _tail1(c_out, d_out, w_bf16, tm, **kw):
    specs = [pl.BlockSpec((1, c_out.shape[1], tm, c_out.shape[3]), lambda b, i: (b, 0, i, 0)),
             _row_spec(tm, d_out.shape[-1]), _const_spec(w_bf16.shape)]
    return _layer_tail(_mix1_y, (c_out, d_out, w_bf16), specs, tm=tm, name="tail1", **kw)


def _fft_kernel(u_ref, f1_ref, twc_ref, tws_ref, f2_ref, fw_ref, o_ref, u_sc, p_sc, y_sc,
                *, scale, n1, n2):
    pu = n2 + FFT_PAD
    pp = 2 * n1 + FFT_PAD
    py = n1 + FFT_PAD
    f1 = f1_ref[...].astype(BF16)
    f2 = f2_ref[...].astype(BF16)
    fw = fw_ref[...].astype(BF16)
    for i1 in range(n1):
        u_sc[i1 * pu:i1 * pu + n2, :] = u_ref[0, 0, i1 * n2:(i1 + 1) * n2, :]

    nb = FFT_BATCH

    def stage1(blk, carry):
        i2s = [blk * nb + j for j in range(nb)]
        x = jnp.concatenate([u_sc[pl.ds(i2, n1, stride=pu), :] for i2 in i2s], axis=1)
        p = jnp.dot(f1, x.astype(BF16), preferred_element_type=F32)
        for j, i2 in enumerate(i2s):
            p_sc[pl.ds(pl.multiple_of(i2 * pp, 8), 2 * n1), :] = p[:, j * C_WIDTH:(j + 1) * C_WIDTH]
        return carry

    lax.fori_loop(0, n2 // nb, stage1, 0, unroll=2)

    def stage2(blk, carry):
        k1s = [blk * nb + j for j in range(nb)]
        qr, qi = [], []
        for k1 in k1s:
            tc = twc_ref[k1]
            ts = tws_ref[k1]
            pr = p_sc[pl.ds(k1, n2, stride=pp), :]
            pim = p_sc[pl.ds(n1 + k1, n2, stride=pp), :]
            qr.append(pr * tc + pim * ts)
            qi.append(pim * tc - pr * ts)
        q = jnp.concatenate([jnp.concatenate(qr, axis=1), jnp.concatenate(qi, axis=1)], axis=0)
        xx = jnp.dot(f2, q.astype(BF16), preferred_element_type=F32)
        xg = jnp.concatenate(
            [jnp.concatenate([xx[:n2, j * C_WIDTH:(j + 1) * C_WIDTH],
                              xx[n2:, j * C_WIDTH:(j + 1) * C_WIDTH]], axis=1) for j in range(nb)],
            axis=0)
        y = jnp.dot(xg.astype(BF16), fw, preferred_element_type=F32) * scale
        for j, k1 in enumerate(k1s):
            y_sc[pl.ds(k1, n2, stride=py), :] = y[j * n2:(j + 1) * n2]
        return carry

    lax.fori_loop(0, n1 // nb, stage2, 0, unroll=2)
    for k2 in range(n2):
        o_ref[0, 0, k2 * n1:(k2 + 1) * n1, :] = y_sc[k2 * py:k2 * py + n1, :]


def _fourier_mixer(u):
    bsz, ngroups, seq, width = u.shape
    n2 = FFT_N2
    n1 = seq // n2
    assert n1 * n2 == seq and width == C_WIDTH and n1 % 8 == 0
    a1 = 2.0 * np.pi * np.outer(np.arange(n1), np.arange(n1)) / n1
    f1 = np.concatenate([np.cos(a1), -np.sin(a1)], axis=0)
    a2 = 2.0 * np.pi * np.outer(np.arange(n2), np.arange(n2)) / n2
    c2, s2 = np.cos(a2), np.sin(a2)
    f2 = np.block([[c2, s2], [-s2, c2]])
    aw = 2.0 * np.pi * np.outer(np.arange(C_WIDTH), np.arange(C_WIDTH)) / C_WIDTH
    fw = np.concatenate([np.cos(aw), np.sin(aw)], axis=0)
    at = np.repeat((2.0 * np.pi * np.outer(np.arange(n1), np.arange(n2)) / seq)[:, :, None],
                   C_WIDTH, axis=2)
    consts = (jnp.asarray(f1, F32), jnp.asarray(np.cos(at), F32), jnp.asarray(np.sin(at), F32),
              jnp.asarray(f2, F32), jnp.asarray(fw, F32))
    blk = pl.BlockSpec((1, 1, seq, C_WIDTH), lambda b, g: (b, g, 0, 0))
    return pl.pallas_call(
        functools.partial(_fft_kernel, scale=float(1.0 / np.sqrt(seq * C_WIDTH)), n1=n1, n2=n2),
        out_shape=jax.ShapeDtypeStruct(u.shape, F32),
        grid=(bsz, ngroups),
        in_specs=[blk] + [_const_spec(c.shape) for c in consts],
        out_specs=blk,
        scratch_shapes=[pltpu.VMEM((n1 * (n2 + FFT_PAD), C_WIDTH), F32),
                        pltpu.VMEM((n2 * (2 * n1 + FFT_PAD), C_WIDTH), F32),
                        pltpu.VMEM((n2 * (n1 + FFT_PAD), C_WIDTH), F32)],
        compiler_params=_cparams("parallel", "parallel"),
        name="fft",
    )(u, *consts)


def _head_perm():
    rep = D_Q_HEADS // D_KV_HEADS
    cols = []
    for j in range(rep):
        for g in range(D_KV_HEADS):
            h = g * rep + j
            cols.extend(range(h * D_HEAD_DIM, (h + 1) * D_HEAD_DIM))
    return np.asarray(cols, np.int32)


def _rope_tables(seq):
    rows = seq // GRID_W
    row = jnp.repeat(jnp.arange(rows, dtype=F32), GRID_W)
    col = jnp.tile(jnp.arange(GRID_W, dtype=F32), rows)
    axis_dim = D_HEAD_DIM // 2
    inv_freq = jnp.power(ROPE_THETA, -jnp.arange(0, axis_dim, 2, dtype=F32) / axis_dim)
    ang_r = row[:, None] * inv_freq[None, :]
    ang_c = col[:, None] * inv_freq[None, :]
    cr, sr, cc, sc = jnp.cos(ang_r), jnp.sin(ang_r), jnp.cos(ang_c), jnp.sin(ang_c)
    cos = jnp.concatenate([cr, cr, cc, cc], axis=1)
    sin = jnp.concatenate([-sr, sr, -sc, sc], axis=1)
    reps = LANES // D_HEAD_DIM
    return jnp.tile(cos, (1, reps)), jnp.tile(sin, (1, reps))


def _inproj_qk_kernel(x_ref, gain_ref, sc_ref, sh_ref, w_ref, cos_ref, sin_ref, bd_h_ref, bd_l_ref,
                      gq_ref, gk_ref, u_ref, qo_ref, ko_ref, vo_ref):
    h = _rms(x_ref[0], gain_ref[...]) * (1.0 + sc_ref[0]) + sh_ref[0]
    z = _dot(h, w_ref[...])
    ngrp = u_ref.shape[1]
    for g in range(ngrp):
        u_ref[0, g] = z[:, g * LANES:(g + 1) * LANES]
    cos = cos_ref[...]
    sin = sin_ref[...]
    quarter = D_HEAD_DIM // 4
    lane = lax.broadcasted_iota(jnp.int32, (1, LANES), 1)
    first_of_pair = (lane // quarter) % 2 == 0

    def norm_rope(x, gain, scale):
        ms = _dot_tab(bd_h_ref[...], bd_l_ref[...], x * x, tab_left=False)
        xn = x * lax.rsqrt(ms + EPS) * gain
        partner = jnp.where(first_of_pair, pltpu.roll(xn, LANES - quarter, 1),
                            pltpu.roll(xn, quarter, 1))
        return ((xn * cos + partner * sin) * scale).astype(BF16)

    nq = qo_ref.shape[-1] // LANES
    for j in range(nq):
        qo_ref[0, :, j * LANES:(j + 1) * LANES] = norm_rope(
            z[:, (ngrp + j) * LANES:(ngrp + j + 1) * LANES], gq_ref[...], D_HEAD_DIM ** -0.5 * LOG2E)
    ko_ref[0] = norm_rope(z[:, (ngrp + nq) * LANES:(ngrp + nq + 1) * LANES], gk_ref[...], 1.0)
    vt = z[:, (ngrp + nq + 1) * LANES:(ngrp + nq + 2) * LANES].T
    ones = jnp.ones((FLASH_ONES, vt.shape[1]), F32)
    vo_ref[0] = jnp.concatenate(
        [piece for g in range(D_KV_HEADS)
         for piece in (vt[g * D_HEAD_DIM:(g + 1) * D_HEAD_DIM], ones)], axis=0).astype(BF16)


def _inproj_qk(x, gain, sc, sh, w_bf16, qk_norm_j, tm):
    bsz, seq, d = x.shape
    n = w_bf16.shape[1]
    qw = D_Q_HEADS * D_HEAD_DIM
    kw = D_KV_HEADS * D_HEAD_DIM
    assert kw == LANES and n == C_GROUPS * C_WIDTH + qw + 2 * kw
    vrows = D_KV_HEADS * (D_HEAD_DIM + FLASH_ONES)
    cos, sin = _rope_tables(seq)
    bd = np.kron(np.eye(LANES // D_HEAD_DIM), np.full((D_HEAD_DIM, D_HEAD_DIM), 1.0 / D_HEAD_DIM))
    bd_h, bd_l = _np_split2(bd)
    reps = LANES // D_HEAD_DIM
    gq = jnp.tile(qk_norm_j[0].astype(F32), reps).reshape(1, LANES)
    gk = jnp.tile(qk_norm_j[1].astype(F32), reps).reshape(1, LANES)
    tab = pl.BlockSpec((tm, LANES), lambda b, i: (i, 0))
    vec = pl.BlockSpec((1, 1, d), lambda b, i: (b, 0, 0))
    return pl.pallas_call(
        _inproj_qk_kernel,
        out_shape=(jax.ShapeDtypeStruct((bsz, C_GROUPS, seq, C_WIDTH), F32),
                   jax.ShapeDtypeStruct((bsz, seq, qw), BF16),
                   jax.ShapeDtypeStruct((bsz, seq, kw), BF16),
                   jax.ShapeDtypeStruct((bsz, vrows, seq), BF16)),
        grid=(bsz, seq // tm),
        in_specs=[pl.BlockSpec((1, tm, d), lambda b, i: (b, i, 0)),
                  _const_spec((1, d)), vec, vec, _const_spec((d, n)),
                  tab, tab, _const_spec(bd_h.shape), _const_spec(bd_l.shape),
                  _const_spec((1, LANES)), _const_spec((1, LANES))],
        out_specs=(pl.BlockSpec((1, C_GROUPS, tm, C_WIDTH), lambda b, i: (b, 0, i, 0)),
                   _row_spec(tm, qw), _row_spec(tm, kw),
                   pl.BlockSpec((1, vrows, tm), lambda b, i: (b, 0, i))),
        compiler_params=_cparams("parallel", "parallel"),
        name="inproj_qk",
    )(x, gain.reshape(1, d), sc.reshape(bsz, 1, d), sh.reshape(bsz, 1, d), w_bf16,
      cos, sin, bd_h, bd_l, gq, gk)


def _flash_kernel(q_ref, k_ref, vt_ref, o_ref, m_sc, acc_sc, s_sc):
    kv = pl.program_id(2)

    @pl.when(kv == 0)
    def _():
        m_sc[...] = jnp.full_like(m_sc, -jnp.inf)
        acc_sc[...] = jnp.zeros_like(acc_sc)

    lane = lax.broadcasted_iota(jnp.int32, (1, LANES), 1)
    lo = lane < D_HEAD_DIM
    nblk = q_ref.shape[-1] // LANES
    tq, tk = q_ref.shape[1], k_ref.shape[1]
    ku, qu = FLASH_KEY_UNIT, min(FLASH_QUERY_UNIT, tq)
    grows = D_HEAD_DIM + FLASH_ONES
    nheads = nblk * D_KV_HEADS
    k = k_ref[0]

    nchunk = tq // qu

    def logits_chunk(idx, c):
        j, g = divmod(idx, D_KV_HEADS)
        qj = q_ref[0, c * qu:(c + 1) * qu, j * LANES:(j + 1) * LANES]
        sel = lo if g == 0 else jnp.logical_not(lo)
        s = lax.dot_general(k, jnp.where(sel, qj, jnp.zeros_like(qj)), (((1,), (1,)), ((), ())),
                            preferred_element_type=F32)
        s_sc[idx % s_sc.shape[0], :, c * qu:(c + 1) * qu] = s
        return jnp.max(s, axis=0, keepdims=True)

    def finish_logits(idx, mcs):
        m_prev = m_sc[idx, 0:1, :]
        m_new = jnp.maximum(m_prev, jnp.concatenate(mcs, axis=1))
        m_sc[idx, 0:1, :] = m_new
        return m_new, jnp.exp2(m_prev - m_new)

    def value_chunk(idx, c, m_new, alpha):
        j, g = divmod(idx, D_KV_HEADS)
        rows = slice(g * grows, (g + 1) * grows)
        qcols = slice(c * qu, (c + 1) * qu)
        pv = None
        for u in range(tk // ku):
            keys = slice(u * ku, (u + 1) * ku)
            p = jnp.exp2(s_sc[idx % s_sc.shape[0], keys, qcols] - m_new[:, qcols])
            d = jnp.dot(vt_ref[0, rows, keys], p.astype(BF16), preferred_element_type=F32)
            pv = d if pv is None else pv + d
        acc_sc[j, rows, qcols] = alpha[:, qcols] * acc_sc[j, rows, qcols] + pv

    def logits_pass(idx):
        return finish_logits(idx, [logits_chunk(idx, c) for c in range(nchunk)])

    pending = [logits_pass(i) for i in range(min(FLASH_AHEAD, nheads))]
    for idx in range(nheads):
        if idx + FLASH_AHEAD < nheads:
            pending.append(logits_pass(idx + FLASH_AHEAD))
        stats = pending.pop(0)
        for c in range(nchunk):
            value_chunk(idx, c, *stats)

    @pl.when(kv == pl.num_programs(2) - 1)
    def _():
        for j in range(nblk):
            parts = []
            for g in range(D_KV_HEADS):
                num = acc_sc[j, g * grows:g * grows + D_HEAD_DIM, :]
                den = acc_sc[j, g * grows + D_HEAD_DIM:g * grows + D_HEAD_DIM + 1, :]
                parts.append(num / den)
            o_ref[0, :, j * LANES:(j + 1) * LANES] = (
                jnp.concatenate(parts, axis=0).T.astype(o_ref.dtype))


def _flash(q, k, vt, tq, tk):
    bsz, seq, qw = q.shape
    kw = k.shape[-1]
    vrows = vt.shape[1]
    return pl.pallas_call(
        _flash_kernel,
        out_shape=jax.ShapeDtypeStruct((bsz, seq, qw), BF16),
        grid=(bsz, seq // tq, seq // tk),
        in_specs=[pl.BlockSpec((1, tq, qw), lambda b, i, j: (b, i, 0)),
                  pl.BlockSpec((1, tk, kw), lambda b, i, j: (b, j, 0)),
                  pl.BlockSpec((1, vrows, tk), lambda b, i, j: (b, 0, j))],
        out_specs=pl.BlockSpec((1, tq, qw), lambda b, i, j: (b, i, 0)),
        scratch_shapes=[pltpu.VMEM((D_Q_HEADS, 8, tq), F32),
                        pltpu.VMEM((qw // LANES, vrows, tq), F32),
                        pltpu.VMEM((FLASH_LOGIT_BUFS, tk, tq), F32)],
        compiler_params=_cparams("parallel", "parallel", "arbitrary"),
        name="flash",
    )(q, k, vt)


def kernel(x, c, t5_bias, hgrn_lb_logits, ada_w, ada_b, norm_gains, ab_w_in, ab_w_out,
           hgrn_out_norm, cd_w_in, cd_w_out, qk_norm, ffn_w_in, ffn_w_out):
    bsz, seq, d = x.shape
    depth = ada_w.shape[0]
    mod = _ada_mod(c.astype(F32), ada_w, ada_b)
    perm = _head_perm()
    aw = A_HEADS * A_HEAD_DIM
    cw = C_GROUPS * C_WIDTH
    qw = D_Q_HEADS * D_HEAD_DIM
    tm_in = min(512, seq)
    ffn_wi = ffn_w_in.astype(BF16)
    ffn_wo = ffn_w_out.astype(BF16)
    for layer in range(depth):
        sh_m, sc_m, g_m, sh_f, sc_f, g_f = [mod[layer, :, i * d:(i + 1) * d] for i in range(6)]
        gains = norm_gains[layer]
        j = layer // 2
        tail = dict(x=x, gain_m=gains[1], gate_m=g_m, g1=gains[2], sc=sc_f, sh=sh_f,
                    wi_bf16=ffn_wi, wo_bf16=ffn_wo, layer=layer, g2=gains[3], gate_f=g_f)
        if layer % 2 == 0:
            w_in = ab_w_in[j].astype(BF16)
            *qkv_cm, z = _inproj_cm(x, gains[0], sc_m, sh_m, w_in, 3 * aw, tm_in)
            branches = [_dilated_branch(cm, t5_bias, window, dil)
                        for cm, (window, dil) in zip(qkv_cm, DIL_CFG)]
            o_f, o_b = _hgrn(z, hgrn_lb_logits, layer, 0)
            g_col = 3 * B_HEADS * B_DK + B_HEADS * B_DV
            x = _tail0(branches, o_f, o_b, z, g_col, hgrn_out_norm[j], ab_w_out[j].astype(BF16),
                       tm=min(TAIL_SUB, seq), **tail)
        else:
            w_full = cd_w_in[j]
            w_in = jnp.concatenate([w_full[:, :cw], w_full[:, cw:cw + qw][:, perm],
                                    w_full[:, cw + qw:]], axis=1).astype(BF16)
            u, qn, kn, vn = _inproj_qk(x, gains[0], sc_m, sh_m, w_in, qk_norm[j], min(1024, seq))
            c_out = _fourier_mixer(u)
            d_out = _flash(qn, kn, vn, min(FLASH_TQ, seq), min(FLASH_TK, seq))
            w_out_full = cd_w_out[j]
            w_out = jnp.concatenate([w_out_full[:cw], w_out_full[cw:][perm]], axis=0).astype(BF16)
            x = _tail1(c_out, d_out, w_out, tm=min(TAIL_BLOCK, seq), **tail)
    return x
```

```python
import functools

import numpy as np
import jax
import jax.numpy as jnp
from jax import lax
from jax.experimental import pallas as pl
from jax.experimental.pallas import tpu as pltpu

F32 = jnp.float32
BF16 = jnp.bfloat16
LANES = 128
MXU_WIDTH = 256
VMEM_LIMIT_BYTES = 56 * 2**20
NEG_INF = -1e30
EPS = 1e-6

GRID_W = 64
A_HEADS = 8
A_HEAD_DIM = 64
DIL_CFG = ((128, 1), (512, 4), (2048, 16))
N_BUCKETS = 32
T5_MAX_DIST = 1024
B_HEADS = 4
B_DK = 128
B_DV = 128
C_GROUPS = 4
C_WIDTH = 128
D_Q_HEADS = 8
D_KV_HEADS = 2
D_HEAD_DIM = 64
ROPE_THETA = 10000.0

DIL_TQ = 128
DIL_TILE = 512
HGRN_BLOCK = 1024
HGRN_SUB = 256
HGRN_BCAST_MIN = 8
FLASH_TQ = 1024
FLASH_TK = 4096
FLASH_LOGIT_BUFS = 2
FLASH_KEY_UNIT = 256
FLASH_QUERY_UNIT = 1024
FLASH_AHEAD = 1
FLASH_ONES = 16
TAIL_BLOCK = 512
TAIL_SUB = 512
FFT_N2 = 128
FFT_BATCH = 8
FFT_PAD = 8
LOG2E = 1.4426950408889634


def _cparams(*sem):
    return pltpu.CompilerParams(dimension_semantics=sem, vmem_limit_bytes=VMEM_LIMIT_BYTES)


def _const_spec(shape):
    nd = len(shape)
    return pl.BlockSpec(shape, lambda *_: (0,) * nd, pipeline_mode=pl.Buffered(1))


def _sigmoid(x):
    return 1.0 / (1.0 + jnp.exp(-x))


def _dot(a, b):
    return jnp.dot(a.astype(BF16), b.astype(BF16), preferred_element_type=F32)


def _dot_nt(a, b):
    return lax.dot_general(a.astype(BF16), b.astype(BF16), (((1,), (1,)), ((), ())),
                           preferred_element_type=F32)


def _split2(a):
    hi = a.astype(BF16)
    lo = (a - hi.astype(F32)).astype(BF16)
    return hi, lo


def _dot_tab(tab_hi, tab_lo, x, *, tab_left):
    x_hi, x_lo = _split2(x)
    if tab_left:
        d = lambda t, v: jnp.dot(t, v, preferred_element_type=F32)
    else:
        d = lambda t, v: jnp.dot(v, t, preferred_element_type=F32)
    return d(tab_hi, x_hi) + (d(tab_hi, x_lo) + d(tab_lo, x_hi))


def _rms(x, gain):
    ms = jnp.mean(x * x, axis=-1, keepdims=True)
    return x * lax.rsqrt(ms + EPS) * gain


def _np_split2(t):
    t = np.asarray(t, np.float32)
    hi = jnp.asarray(t, F32).astype(BF16)
    lo = (jnp.asarray(t, F32) - hi.astype(F32)).astype(BF16)
    return hi, lo


def _mod_kernel(c_ref, w_ref, b_ref, o_ref):
    c = c_ref[...]
    o_ref[0] = _dot(c * _sigmoid(c), w_ref[0]) + b_ref[0]


def _ada_mod(c, ada_w, ada_b):
    depth, d, n6 = ada_w.shape
    bsz = c.shape[0]
    rows = 8
    cp = jnp.zeros((rows, d), F32).at[:bsz].set(c)
    tn = n6 // 4
    out = pl.pallas_call(
        _mod_kernel,
        out_shape=jax.ShapeDtypeStruct((depth, rows, n6), F32),
        grid=(depth, n6 // tn),
        in_specs=[pl.BlockSpec((rows, d), lambda l, j: (0, 0)),
                  pl.BlockSpec((1, d, tn), lambda l, j: (l, 0, j)),
                  pl.BlockSpec((1, 1, tn), lambda l, j: (l, 0, j))],
        out_specs=pl.BlockSpec((1, rows, tn), lambda l, j: (l, 0, j)),
        compiler_params=_cparams("parallel", "parallel"),
        name="ada_mod",
    )(cp, ada_w, ada_b.reshape(depth, 1, n6))
    return out[:, :bsz]


def _inproj_cm_kernel(x_ref, gain_ref, sc_ref, sh_ref, w_ref, *refs):
    cm_refs, rest_ref, zs_sc, zc_sc = refs[:-3], refs[-3], refs[-2], refs[-1]
    h = _rms(x_ref[0], gain_ref[...]) * (1.0 + sc_ref[0]) + sh_ref[0]
    z = _dot(h, w_ref[...])
    nblk, tm, _ = zs_sc.shape
    rest_ref[0] = z[:, nblk * LANES:]
    nq = A_HEADS * A_HEAD_DIM // LANES
    for c in range(nblk):
        blk = z[:, c * LANES:(c + 1) * LANES]
        zs_sc[c] = blk * (A_HEAD_DIM ** -0.5 * LOG2E) if c < nq else blk
    src, sd = zs_sc, 1
    for level, (cm_ref, (_, dil)) in enumerate(zip(cm_refs, DIL_CFG)):
        step, n = dil // sd, tm // dil
        keep = dil > 1 and level + 1 < len(DIL_CFG)
        for rs in range(sd):
            for cc in range(step):
                r = rs + sd * cc
                for c in range(nblk):
                    rows = src[c, pl.ds(rs * (tm // sd) + cc, n, stride=step), :]
                    cm_ref[0, r, :, c * LANES:(c + 1) * LANES] = rows.astype(BF16)
                    if keep:
                        zc_sc[c, r * n:(r + 1) * n, :] = rows
        if keep:
            src, sd = zc_sc, dil


def _inproj_cm(x, gain, sc, sh, w_bf16, na, tm):
    bsz, seq, d = x.shape
    n = w_bf16.shape[1]
    vec = pl.BlockSpec((1, 1, d), lambda b, i: (b, 0, 0))
    dils = [dl for _, dl in DIL_CFG]
    assert dils[0] == 1 and all(b % a == 0 for a, b in zip(dils, dils[1:]))
    return pl.pallas_call(
        _inproj_cm_kernel,
        out_shape=tuple([jax.ShapeDtypeStruct((bsz, dl, seq // dl, na), BF16) for dl in dils]
                        + [jax.ShapeDtypeStruct((bsz, seq, n - na), F32)]),
        grid=(bsz, seq // tm),
        in_specs=[pl.BlockSpec((1, tm, d), lambda b, i: (b, i, 0)),
                  _const_spec((1, d)), vec, vec, _const_spec((d, n))],
        out_specs=tuple([pl.BlockSpec((1, dl, tm // dl, na), lambda b, i: (b, 0, i, 0)) for dl in dils]
                        + [pl.BlockSpec((1, tm, n - na), lambda b, i: (b, i, 0))]),
        scratch_shapes=[pltpu.VMEM((na // LANES, tm, LANES), F32),
                        pltpu.VMEM((na // LANES, tm, LANES), F32)],
        compiler_params=_cparams("parallel", "parallel"),
        name="inproj_cm",
    )(x, gain.reshape(1, d), sc.reshape(bsz, 1, d), sh.reshape(bsz, 1, d), w_bf16)


def _t5_buckets(rel):
    half = N_BUCKETS // 2
    max_exact = half // 2
    n = np.abs(rel)
    large = max_exact + (np.log(np.maximum(n, 1) / max_exact) / np.log(T5_MAX_DIST / max_exact)
                         * (half - max_exact)).astype(np.int32)
    large = np.minimum(large, half - 1)
    return (np.where(rel > 0, half, 0) + np.where(n < max_exact, n, large)).astype(np.int32)


def _dil_bias(t5_bias, window, dil, tq):
    half = (window // 2) // dil
    assert half == tq // 2
    rel = np.arange(2 * tq)[None, :] - half - np.arange(tq)[:, None]
    inside = np.abs(rel) <= half
    buckets = _t5_buckets(np.where(inside, rel, 0) * dil)
    onehot =jnp.asarray(np.eye(N_BUCKETS, dtype=np.float32)[buckets])
    bias = jnp.einsum("qkn,nh->hqk", onehot, t5_bias.astype(F32), precision=lax.Precision.HIGHEST)
    return jnp.where(jnp.asarray(inside)[None], bias * LOG2E, NEG_INF)


def _dil_kernel(q_ref, kp_ref, kc_ref, kn_ref, vp_ref, vc_ref, vn_ref, bias_ref, o_ref, lse_ref,
                *, class_len):
    i = pl.program_id(2)
    sub, hq, tile = DIL_TQ, DIL_TQ // 2, q_ref.shape[2]
    kwin = jnp.concatenate([kp_ref[0, 0], kc_ref[0, 0], kn_ref[0, 0]], axis=0)
    vwin = jnp.concatenate([vp_ref[0, 0], vc_ref[0, 0], vn_ref[0, 0]], axis=0)
    lane = lax.broadcasted_iota(jnp.int32, (1, LANES), 1)
    lo = lane < A_HEAD_DIM
    nblk = A_HEADS // 2
    units = [(jt, j) for jt in range(tile // sub) for j in range(nblk)]
    logits = []
    for jt, j in units:
        cols = slice(j * LANES, (j + 1) * LANES)
        qj = q_ref[0, 0, jt * sub:(jt + 1) * sub, cols]
        zero = jnp.zeros_like(qj)
        q2 = jnp.concatenate([jnp.where(lo, qj, zero), jnp.where(lo, zero, qj)], axis=0)
        s = lax.dot_general(q2, kwin[jt * sub:jt * sub + 2 * sub, cols], (((1,), (1,)), ((), ())),
                            preferred_element_type=F32)
        kpos = i * tile + jt * sub - hq + lax.broadcasted_iota(jnp.int32, (1, 2 * sub), 1)
        valid = jnp.logical_and(kpos >= 0, kpos < class_len)
        logits.append(jnp.where(valid, s + bias_ref[j], NEG_INF))
    s_all = jnp.concatenate(logits, axis=0)
    m = jnp.max(s_all, axis=-1, keepdims=True)
    p32 = jnp.exp2(s_all - m)
    l = jnp.sum(p32, axis=-1, keepdims=True)
    p = p32.astype(BF16)
    rinv = 1.0 / l
    lse = m + jnp.log2(l)
    for jt in range(tile // sub):
        lse_all = jnp.zeros((sub, LANES), F32)
        for j in range(nblk):
            cols = slice(j * LANES, (j + 1) * LANES)
            r0 = (jt * nblk + j) * 2 * sub
            o2 = jnp.dot(p[r0:r0 + 2 * sub], vwin[jt * sub:jt * sub + 2 * sub, cols],
                         preferred_element_type=F32) * rinv[r0:r0 + 2 * sub]
            lse_all = jnp.where(lane == 2 * j, lse[r0:r0 + sub], lse_all)
            lse_all = jnp.where(lane == 2 * j + 1, lse[r0 + sub:r0 + 2 * sub], lse_all)
            o_ref[0, 0, jt * sub:(jt + 1) * sub, cols] = (
                jnp.where(lo, o2[:sub], o2[sub:]).astype(o_ref.dtype))
        lse_ref[0, 0, jt * sub:(jt + 1) * sub, :] = lse_all


def _dilated_branch(qkv_cm, t5_bias, window, dil):
    bsz, _, cl, width = qkv_cm.shape
    aw = A_HEADS * A_HEAD_DIM
    tile, hq = min(DIL_TILE, cl), DIL_TQ // 2
    nt = cl // tile
    per = tile // hq
    nh = cl // hq

    def cur(col):
        return pl.BlockSpec((1, 1, tile, aw), lambda b, r, i: (b, r, i, col))

    def prev(col):
        return pl.BlockSpec((1, 1, hq, aw), lambda b, r, i: (b, r, jnp.maximum(i * per - 1, 0), col))

    def nxt(col):
        return pl.BlockSpec((1, 1, hq, aw),
                            lambda b, r, i: (b, r, jnp.minimum((i + 1) * per, nh - 1), col))

    return pl.pallas_call(
        functools.partial(_dil_kernel, class_len=cl),
        out_shape=(jax.ShapeDtypeStruct((bsz, dil, cl, aw), BF16),
                   jax.ShapeDtypeStruct((bsz, dil, cl, LANES), F32)),
        grid=(bsz, dil, nt),
        in_specs=[cur(0), prev(1), cur(1), nxt(1), prev(2), cur(2), nxt(2),
                  _const_spec((A_HEADS // 2, 2 * DIL_TQ, 2 * DIL_TQ))],
        out_specs=(pl.BlockSpec((1, 1, tile, aw), lambda b, r, i: (b, r, i, 0)),
                   pl.BlockSpec((1, 1, tile, LANES), lambda b, r, i: (b, r, i, 0))),
        compiler_params=_cparams("parallel", "parallel", "parallel"),
        name=f"dilated_d{dil}",
    )(*([qkv_cm] * 7),
      _dil_bias(t5_bias, window, dil, DIL_TQ).reshape(A_HEADS // 2, 2 * DIL_TQ, 2 * DIL_TQ))


def _merge_branches(a_refs, l_refs, e_ref, a_scs, l_scs, sub):
    tm = l_scs[0].shape[0]
    accs, lses = [], []
    for (_, dil), a_ref, l_ref, a_sc, l_sc in zip(DIL_CFG, a_refs, l_refs, a_scs, l_scs):
        nblk = a_sc.shape[0]
        for r in range(dil):
            rows = pl.ds(r, tm // dil, stride=dil)
            src = pl.ds(sub * (tm // dil), tm // dil)
            for c in range(nblk):
                a_sc[c, rows, :] = a_ref[0, r, src, c * LANES:(c + 1) * LANES].astype(F32)
            l_sc[rows, :] = l_ref[0, r, src, :]
        accs.append(jnp.concatenate([a_sc[c] for c in range(nblk)], axis=1))
        lses.append(l_sc[...])
    mx = functools.reduce(jnp.maximum, lses)
    ws = [jnp.exp2(x - mx) for x in lses]
    tot = functools.reduce(jnp.add, ws)
    out = None
    for w, a in zip(ws, accs):
        w_hi, w_lo = _split2(w / tot)
        wide = (jnp.dot(w_hi, e_ref[...], preferred_element_type=F32)
                + jnp.dot(w_lo, e_ref[...], preferred_element_type=F32))
        out = wide * a if out is None else out + wide * a
    return out


def _merge_operands(branch_outs, tm):
    aw = branch_outs[0][0].shape[-1]
    expand = np.zeros((LANES, aw), np.float32)
    for h in range(A_HEADS):
        expand[h, h * A_HEAD_DIM:(h + 1) * A_HEAD_DIM] = 1.0
    dils = [d for _, d in DIL_CFG]
    specs = ([pl.BlockSpec((1, d, tm // d, aw), lambda b, i: (b, 0, i, 0)) for d in dils]
             + [pl.BlockSpec((1, d, tm // d, LANES), lambda b, i: (b, 0, i, 0)) for d in dils]
             + [_const_spec(expand.shape)])
    arrays = [o for o, _ in branch_outs] + [l for _, l in branch_outs] + [jnp.asarray(expand, BF16)]
    scratch = ([pltpu.VMEM((aw // LANES, TAIL_SUB, LANES), F32) for _ in dils]
               + [pltpu.VMEM((TAIL_SUB, LANES), F32) for _ in dils])
    return arrays, specs, scratch


def _hgrn_consts(reverse):
    t = HGRN_SUB
    r = np.arange(t)
    u = r[None, :]
    row = r[:, None]
    nmats = [(u >= row) if reverse else (u <= row)]
    masks = []
    m = t // 2
    while m >= 1:
        grp = r // (2 * m)
        in_first = (r % (2 * m)) < m
        same = grp[:, None] == grp[None, :]
        if reverse:
            beta = (grp * 2 * m + m)[:, None]
            n = np.where(in_first[:, None], (u >= row) & (u < beta), (u >= beta) & (u < row))
            mask = same & in_first[:, None] & ~in_first[None, :]
        else:
            beta = (grp * 2 * m + m - 1)[:, None]
            n = np.where(in_first[:, None], (u > row) & (u <= beta), (u > beta) & (u <= row))
            mask = same & ~in_first[:, None] & in_first[None, :]
        if m < HGRN_BCAST_MIN:
            nmats.append(n)
        masks.append(mask)
        m //= 2
    masks.append(np.eye(t, dtype=bool))
    nmat = jnp.asarray(np.concatenate(nmats, axis=0), F32).astype(BF16)
    return nmat, jnp.asarray(np.stack(masks), F32)


def _hgrn_block(q, f, v, lb, st, nmat_ref, mask_ref, reverse):
    t = HGRN_SUB
    nlev = mask_ref.shape[0] - 1
    qs = q * (B_DK ** -0.5)
    fa = lb + (1.0 - lb) * _sigmoid(f)
    kk = 1.0 - fa
    g_hi, g_lo = _split2(jnp.log2(fa))
    ex = jnp.dot(nmat_ref[...], jnp.concatenate([g_hi, g_lo], axis=1), preferred_element_type=F32)
    ex = ex[:, :B_DK] + ex[:, B_DK:]
    b = ex[:t]
    btot = b[0:1] if reverse else b[t - 1:t]
    a = mask_ref[nlev] * _dot_nt(qs, kk)
    fine = 1
    for l in range(nlev):
        m = t >> (l + 1)
        if m >= HGRN_BCAST_MIN:
            ref = jnp.concatenate(
                [jnp.broadcast_to(b[beta:beta + 1], (2 * m, B_DK))
                 for beta in range(m if reverse else m - 1, t, 2 * m)], axis=0)
            e = jnp.exp2(-jnp.abs(b - ref))
        else:
            e = jnp.exp2(ex[fine * t:(fine + 1) * t])
            fine += 1
        a = a + mask_ref[l] * _dot_nt(qs * e, kk * e)
    out = _dot(a, v) + _dot_nt(qs * jnp.exp2(b), st)
    khat = (kk * jnp.exp2(btot - b)).astype(BF16)
    st_new = st * jnp.exp2(btot) + jnp.dot(v.T.astype(BF16), khat, preferred_element_type=F32)
    return out, st_new


def _hgrn_lb(lg_ref, layer):
    lg = [lg_ref[l, 0] for l in range(lg_ref.shape[0])]
    mx = functools.reduce(jnp.maximum, lg)
    e = [jnp.exp(x - mx) for x in lg]
    return functools.reduce(jnp.add, e[:layer + 1]) / functools.reduce(jnp.add, e)


def _hgrn_kernel(qf_ref, ff_ref, vf_ref, qb_ref, fb_ref, vb_ref, lgf_ref, lgb_ref,
                 nf_ref, mf_ref, nb_ref, mb_ref, of_ref, ob_ref, sf_sc, sb_sc, *, layer):
    @pl.when(pl.program_id(2) == 0)
    def _():
        sf_sc[...] = jnp.zeros_like(sf_sc)
        sb_sc[...] = jnp.zeros_like(sb_sc)

    nsub = HGRN_BLOCK // HGRN_SUB
    chains = ((qf_ref, ff_ref, vf_ref, lgf_ref, sf_sc, nf_ref, mf_ref, of_ref, False),
              (qb_ref, fb_ref, vb_ref, lgb_ref, sb_sc, nb_ref, mb_ref, ob_ref, True))
    for q_ref, f_ref, v_ref, lg_ref, st_sc, n_ref, m_ref, o_ref, reverse in chains:
        lb = _hgrn_lb(lg_ref, layer)
        st = st_sc[...]
        for sub in (reversed(range(nsub)) if reverse else range(nsub)):
            rows = pl.ds(sub * HGRN_SUB, HGRN_SUB)
            o, st = _hgrn_block(q_ref[0, rows, :], f_ref[0, rows, :], v_ref[0, rows, :], lb, st,
                                n_ref, m_ref, reverse)
            o_ref[0, rows, :] = o.astype(o_ref.dtype)
        st_sc[...] = st


def _hgrn(z, lb_logits, layer, col0):
    bsz, seq, _ = z.shape
    t = HGRN_BLOCK
    nb = seq // t
    c0 = col0 // LANES
    hw = B_HEADS

    def zspec(group, rev):
        return pl.BlockSpec(
            (1, t, LANES),
            lambda b, h, j: (b, (nb - 1 - j) if rev else j, c0 + group * hw + h))

    def lgspec(direction):
        return pl.BlockSpec((lb_logits.shape[0], 1, 1, LANES),
                            lambda b, h, j: (0, direction * hw + h, 0, 0))

    nf, mf = _hgrn_consts(False)
    nbw, mbw = _hgrn_consts(True)
    lg = lb_logits.astype(F32).reshape(lb_logits.shape[0], 2 * hw, 1, LANES)
    o_shape = jax.ShapeDtypeStruct((bsz, seq, hw * B_DV), BF16)
    return pl.pallas_call(
        functools.partial(_hgrn_kernel, layer=layer),
        out_shape=(o_shape, o_shape),
        grid=(bsz, hw, nb),
        in_specs=[zspec(0, False), zspec(1, False), zspec(3, False),
                  zspec(0, True), zspec(2, True), zspec(3, True),
                  lgspec(0), lgspec(1),
                  _const_spec(nf.shape), _const_spec(mf.shape),
                  _const_spec(nbw.shape), _const_spec(mbw.shape)],
        out_specs=(pl.BlockSpec((1, t, LANES), lambda b, h, j: (b, j, h)),
                   pl.BlockSpec((1, t, LANES), lambda b, h, j: (b, nb - 1 - j, h))),
        scratch_shapes=[pltpu.VMEM((B_DV, B_DK), F32), pltpu.VMEM((B_DV, B_DK), F32)],
        compiler_params=_cparams("parallel", "parallel", "arbitrary"),
        name="hgrn",
    )(z, z, z, z, z, z, lg, lg, nf, mf, nbw, mbw)


def _mix0_y(sub, *refs):
    nbr = len(DIL_CFG)
    a_refs, l_refs, e_ref = refs[:nbr], refs[nbr:2 * nbr], refs[2 * nbr]
    of_ref, ob_ref, g_ref, on_ref, w_ref = refs[2 * nbr + 1:2 * nbr + 6]
    scratch = refs[2 * nbr + 6:]
    a = _merge_branches(a_refs, l_refs, e_ref, scratch[:nbr], scratch[nbr:], sub)
    rows = pl.ds(sub * TAIL_SUB, TAIL_SUB)
    o = of_ref[0, rows, :].astype(F32) + ob_ref[0, rows, :].astype(F32)
    g = g_ref[0, rows, :]
    parts = [_rms(o[:, h * B_DV:(h + 1) * B_DV], on_ref[...]) for h in range(B_HEADS)]
    bn = jnp.concatenate(parts, axis=-1) * (g * _sigmoid(g))
    na = a.shape[-1]
    return _dot(a, w_ref[:na, :]) + _dot(bn, w_ref[na:, :])


def _mix1_y(sub, c_ref, d_ref, w_ref):
    rows = pl.ds(sub * TAIL_SUB, TAIL_SUB)
    c = jnp.concatenate([c_ref[0, g, rows, :] for g in range(c_ref.shape[1])], axis=1)
    nc = c.shape[-1]
    return _dot(c, w_ref[:nc, :]) + _dot(d_ref[0, rows, :], w_ref[nc:, :])


def _row_spec(tm, width, col=0):
    return pl.BlockSpec((1, tm, width), lambda b, i: (b, i, col))


def _tail_kernel(*refs, nmix, mix_fn, bounds):
    (x_ref, gm_ref, gatem_ref, g1_ref, sc_ref, sh_ref, wi_ref, wo_ref, g2_ref, gatef_ref,
     o_ref) = refs[nmix:nmix + 11]
    hidden = wo_ref.shape[0]
    for sub in range(x_ref.shape[1] // TAIL_SUB):
        rows = pl.ds(sub * TAIL_SUB, TAIL_SUB)
        y_mix = mix_fn(sub, *refs[:nmix], *refs[nmix + 11:])
        x1 = x_ref[0, rows, :] + gatem_ref[0] * _rms(y_mix, gm_ref[...])
        h = (_rms(x1, g1_ref[...]) * (1.0 + sc_ref[0]) + sh_ref[0]).astype(BF16)
        y = None
        for c0, c1 in zip(bounds, bounds[1:]):
            gt = jnp.dot(h, wi_ref[:, c0:c1], preferred_element_type=F32)
            up = jnp.dot(h, wi_ref[:, hidden + c0:hidden + c1], preferred_element_type=F32)
            part = _dot(gt * _sigmoid(gt) * up, wo_ref[c0:c1, :])
            y = part if y is None else y + part
        o_ref[0, rows, :] = x1 + gatef_ref[0] * _rms(y, g2_ref[...])


def _layer_tail(mix_fn, mix_args, mix_specs, x, gain_m, gate_m, g1, sc, sh, wi_bf16, wo_bf16, layer,
                g2, gate_f, tm, name, mix_scratch=()):
    bsz, seq, d = x.shape
    vec = pl.BlockSpec((1, 1, d), lambda b, i: (b, 0, 0))
    hidden = wo_bf16.shape[1]

    def layer_slab(w):
        return pl.BlockSpec((None,) + w.shape[1:], lambda b, i: (layer, 0, 0),
                            pipeline_mode=pl.Buffered(1))

    ntile = hidden // MXU_WIDTH
    assert ntile * MXU_WIDTH == hidden
    bounds = (0, (ntile + 1) // 2 * MXU_WIDTH, hidden)
    row = lambda v: v.reshape(1, d)
    per_batch = lambda v: v.reshape(bsz, 1, d)
    return pl.pallas_call(
        functools.partial(_tail_kernel, nmix=len(mix_args), mix_fn=mix_fn, bounds=bounds),
        out_shape=jax.ShapeDtypeStruct(x.shape, F32),
        grid=(bsz, seq // tm),
        in_specs=list(mix_specs) + [_row_spec(tm, d), _const_spec((1, d)), vec, _const_spec((1, d)),
                                    vec, vec, layer_slab(wi_bf16), layer_slab(wo_bf16),
                                    _const_spec((1, d)), vec],
        out_specs=_row_spec(tm, d),
        scratch_shapes=list(mix_scratch),
        compiler_params=_cparams("parallel", "parallel"),
        name=name,
    )(*mix_args, x, row(gain_m), per_batch(gate_m), row(g1), per_batch(sc), per_batch(sh),
      wi_bf16, wo_bf16, row(g2), per_batch(gate_f))


def _tail0(branches, o_f, o_b, z, g_col, out_norm, w_bf16, tm, **kw):
    wv = B_HEADS * B_DV
    m_arrays, m_specs, m_scratch = _merge_operands(branches, tm)
    specs = m_specs + [_row_spec(tm, wv), _row_spec(tm, wv), _row_spec(tm, wv, g_col // wv),
                       _const_spec((1, B_DV)), _const_spec(w_bf16.shape)]
    args = m_arrays + [o_f, o_b, z, out_norm.reshape(1, B_DV), w_bf16]
    return _layer_tail(_mix0_y, args, specs, tm=tm, name="tail0", mix_scratch=m_scratch, **kw)


def _tail1(c_out, d_out, w_bf16, tm, **kw):
    specs = [pl.BlockSpec((1, c_out.shape[1], tm, c_out.shape[3]), lambda b, i: (b, 0, i, 0)),
             _row_spec(tm, d_out.shape[-1]), _const_spec(w_bf16.shape)]
    return _layer_tail(_mix1_y, (c_out, d_out, w_bf16), specs, tm=tm, name="tail1", **kw)


def _fft_kernel(u_ref, f1_ref, twc_ref, tws_ref, f2_ref, fw_ref, o_ref, u_sc, p_sc, y_sc,
                *, scale, n1, n2):
    pu = n2 + FFT_PAD
    pp = 2 * n1 + FFT_PAD
    py = n1 + FFT_PAD
    f1 = f1_ref[...].astype(BF16)
    f2 = f2_ref[...].astype(BF16)
    fw = fw_ref[...].astype(BF16)
    for i1 in range(n1):
        u_sc[i1 * pu:i1 * pu + n2, :] = u_ref[0, 0, i1 * n2:(i1 + 1) * n2, :]

    nb = FFT_BATCH

    def stage1(blk, carry):
        i2s = [blk * nb + j for j in range(nb)]
        x = jnp.concatenate([u_sc[pl.ds(i2, n1, stride=pu), :] for i2 in i2s], axis=1)
        p = jnp.dot(f1, x.astype(BF16), preferred_element_type=F32)
        for j, i2 in enumerate(i2s):
            p_sc[pl.ds(pl.multiple_of(i2 * pp, 8), 2 * n1), :] = p[:, j * C_WIDTH:(j + 1) * C_WIDTH]
        return carry

    lax.fori_loop(0, n2 // nb, stage1, 0, unroll=2)

    def stage2(blk, carry):
        k1s = [blk * nb + j for j in range(nb)]
        qr, qi = [], []
        for k1 in k1s:
            tc = twc_ref[k1]
            ts = tws_ref[k1]
            pr = p_sc[pl.ds(k1, n2, stride=pp), :]
            pim = p_sc[pl.ds(n1 + k1, n2, stride=pp), :]
            qr.append(pr * tc + pim * ts)
            qi.append(pim * tc - pr * ts)
        q = jnp.concatenate([jnp.concatenate(qr, axis=1), jnp.concatenate(qi, axis=1)], axis=0)
        xx = jnp.dot(f2, q.astype(BF16), preferred_element_type=F32)
        xg = jnp.concatenate(
            [jnp.concatenate([xx[:n2, j * C_WIDTH:(j + 1) * C_WIDTH],
                              xx[n2:, j * C_WIDTH:(j + 1) * C_WIDTH]], axis=1) for j in range(nb)],
            axis=0)
        y = jnp.dot(xg.astype(BF16), fw, preferred_element_type=F32) * scale
        for j, k1 in enumerate(k1s):
            y_sc[pl.ds(k1, n2, stride=py), :] = y[j * n2:(j + 1) * n2]
        return carry

    lax.fori_loop(0, n1 // nb, stage2, 0, unroll=2)
    for k2 in range(n2):
        o_ref[0, 0, k2 * n1:(k2 + 1) * n1, :] = y_sc[k2 * py:k2 * py + n1, :]


def _fourier_mixer(u):
    bsz, ngroups, seq, width = u.shape
    n2 = FFT_N2
    n1 = seq // n2
    assert n1 * n2 == seq and width == C_WIDTH and n1 % 8 == 0
    a1 = 2.0 * np.pi * np.outer(np.arange(n1), np.arange(n1)) / n1
    f1 = np.concatenate([np.cos(a1), -np.sin(a1)], axis=0)
    a2 = 2.0 * np.pi * np.outer(np.arange(n2), np.arange(n2)) / n2
    c2, s2 = np.cos(a2), np.sin(a2)
    f2 = np.block([[c2, s2], [-s2, c2]])
    aw = 2.0 * np.pi * np.outer(np.arange(C_WIDTH), np.arange(C_WIDTH)) / C_WIDTH
    fw = np.concatenate([np.cos(aw), np.sin(aw)], axis=0)
    at = np.repeat((2.0 * np.pi * np.outer(np.arange(n1), np.arange(n2)) / seq)[:, :, None],
                   C_WIDTH, axis=2)
    consts = (jnp.asarray(f1, F32), jnp.asarray(np.cos(at), F32), jnp.asarray(np.sin(at), F32),
              jnp.asarray(f2, F32), jnp.asarray(fw, F32))
    blk = pl.BlockSpec((1, 1, seq, C_WIDTH), lambda b, g: (b, g, 0, 0))
    return pl.pallas_call(
        functools.partial(_fft_kernel, scale=float(1.0 / np.sqrt(seq * C_WIDTH)), n1=n1, n2=n2),
        out_shape=jax.ShapeDtypeStruct(u.shape, F32),
        grid=(bsz, ngroups),
        in_specs=[blk] + [_const_spec(c.shape) for c in consts],
        out_specs=blk,
        scratch_shapes=[pltpu.VMEM((n1 * (n2 + FFT_PAD), C_WIDTH), F32),
                        pltpu.VMEM((n2 * (2 * n1 + FFT_PAD), C_WIDTH), F32),
                        pltpu.VMEM((n2 * (n1 + FFT_PAD), C_WIDTH), F32)],
        compiler_params=_cparams("parallel", "parallel"),
        name="fft",
    )(u, *consts)


def _head_perm():
    rep = D_Q_HEADS // D_KV_HEADS
    cols = []
    for j in range(rep):
        for g in range(D_KV_HEADS):
            h = g * rep + j
            cols.extend(range(h * D_HEAD_DIM, (h + 1) * D_HEAD_DIM))
    return np.asarray(cols, np.int32)


def _rope_tables(seq):
    rows = seq // GRID_W
    row = jnp.repeat(jnp.arange(rows, dtype=F32), GRID_W)
    col = jnp.tile(jnp.arange(GRID_W, dtype=F32), rows)
    axis_dim = D_HEAD_DIM // 2
    inv_freq = jnp.power(ROPE_THETA, -jnp.arange(0, axis_dim, 2, dtype=F32) / axis_dim)
    ang_r = row[:, None] * inv_freq[None, :]
    ang_c = col[:, None] * inv_freq[None, :]
    cr, sr, cc, sc = jnp.cos(ang_r), jnp.sin(ang_r), jnp.cos(ang_c), jnp.sin(ang_c)
    cos = jnp.concatenate([cr, cr, cc, cc], axis=1)
    sin = jnp.concatenate([-sr, sr, -sc, sc], axis=1)
    reps = LANES // D_HEAD_DIM
    return jnp.tile(cos, (1, reps)), jnp.tile(sin, (1, reps))


def _inproj_qk_kernel(x_ref, gain_ref, sc_ref, sh_ref, w_ref, cos_ref, sin_ref, bd_h_ref, bd_l_ref,
                      gq_ref, gk_ref, u_ref, qo_ref, ko_ref, vo_ref):
    h = _rms(x_ref[0], gain_ref[...]) * (1.0 + sc_ref[0]) + sh_ref[0]
    z = _dot(h, w_ref[...])
    ngrp = u_ref.shape[1]
    for g in range(ngrp):
        u_ref[0, g] = z[:, g * LANES:(g + 1) * LANES]
    cos = cos_ref[...]
    sin = sin_ref[...]
    quarter = D_HEAD_DIM // 4
    lane = lax.broadcasted_iota(jnp.int32, (1, LANES), 1)
    first_of_pair = (lane // quarter) % 2 == 0

    def norm_rope(x, gain, scale):
        ms = _dot_tab(bd_h_ref[...], bd_l_ref[...], x * x, tab_left=False)
        xn = x * lax.rsqrt(ms + EPS) * gain
        partner = jnp.where(first_of_pair, pltpu.roll(xn, LANES - quarter, 1),
                            pltpu.roll(xn, quarter, 1))
        return ((xn * cos + partner * sin) * scale).astype(BF16)

    nq = qo_ref.shape[-1] // LANES
    for j in range(nq):
        qo_ref[0, :, j * LANES:(j + 1) * LANES] = norm_rope(
            z[:, (ngrp + j) * LANES:(ngrp + j + 1) * LANES], gq_ref[...], D_HEAD_DIM ** -0.5 * LOG2E)
    ko_ref[0] = norm_rope(z[:, (ngrp + nq) * LANES:(ngrp + nq + 1) * LANES], gk_ref[...], 1.0)
    vt = z[:, (ngrp + nq + 1) * LANES:(ngrp + nq + 2) * LANES].T
    ones = jnp.ones((FLASH_ONES, vt.shape[1]), F32)
    vo_ref[0] = jnp.concatenate(
        [piece for g in range(D_KV_HEADS)
         for piece in (vt[g * D_HEAD_DIM:(g + 1) * D_HEAD_DIM], ones)], axis=0).astype(BF16)


def _inproj_qk(x, gain, sc, sh, w_bf16, qk_norm_j, tm):
    bsz, seq, d = x.shape
    n = w_bf16.shape[1]
    qw = D_Q_HEADS * D_HEAD_DIM
    kw = D_KV_HEADS * D_HEAD_DIM
    assert kw == LANES and n == C_GROUPS * C_WIDTH + qw + 2 * kw
    vrows = D_KV_HEADS * (D_HEAD_DIM + FLASH_ONES)
    cos, sin = _rope_tables(seq)
    bd = np.kron(np.eye(LANES // D_HEAD_DIM), np.full((D_HEAD_DIM, D_HEAD_DIM), 1.0 / D_HEAD_DIM))
    bd_h, bd_l = _np_split2(bd)
    reps = LANES // D_HEAD_DIM
    gq = jnp.tile(qk_norm_j[0].astype(F32), reps).reshape(1, LANES)
    gk = jnp.tile(qk_norm_j[1].astype(F32), reps).reshape(1, LANES)
    tab = pl.BlockSpec((tm, LANES), lambda b, i: (i, 0))
    vec = pl.BlockSpec((1, 1, d), lambda b, i: (b, 0, 0))
    return pl.pallas_call(
        _inproj_qk_kernel,
        out_shape=(jax.ShapeDtypeStruct((bsz, C_GROUPS, seq, C_WIDTH), F32),
                   jax.ShapeDtypeStruct((bsz, seq, qw), BF16),
                   jax.ShapeDtypeStruct((bsz, seq, kw), BF16),
                   jax.ShapeDtypeStruct((bsz, vrows, seq), BF16)),
        grid=(bsz, seq // tm),
        in_specs=[pl.BlockSpec((1, tm, d), lambda b, i: (b, i, 0)),
                  _const_spec((1, d)), vec, vec, _const_spec((d, n)),
                  tab, tab, _const_spec(bd_h.shape), _const_spec(bd_l.shape),
                  _const_spec((1, LANES)), _const_spec((1, LANES))],
        out_specs=(pl.BlockSpec((1, C_GROUPS, tm, C_WIDTH), lambda b, i: (b, 0, i, 0)),
                   _row_spec(tm, qw), _row_spec(tm, kw),
                   pl.BlockSpec((1, vrows, tm), lambda b, i: (b, 0, i))),
        compiler_params=_cparams("parallel", "parallel"),
        name="inproj_qk",
    )(x, gain.reshape(1, d), sc.reshape(bsz, 1, d), sh.reshape(bsz, 1, d), w_bf16,
      cos, sin, bd_h, bd_l, gq, gk)


def _flash_kernel(q_ref, k_ref, vt_ref, o_ref, m_sc, acc_sc, s_sc):
    kv = pl.program_id(2)

    @pl.when(kv == 0)
    def _():
        m_sc[...] = jnp.full_like(m_sc, -jnp.inf)
        acc_sc[...] = jnp.zeros_like(acc_sc)

    lane = lax.broadcasted_iota(jnp.int32, (1, LANES), 1)
    lo = lane < D_HEAD_DIM
    nblk = q_ref.shape[-1] // LANES
    tq, tk = q_ref.shape[1], k_ref.shape[1]
    ku, qu = FLASH_KEY_UNIT, min(FLASH_QUERY_UNIT, tq)
    grows = D_HEAD_DIM + FLASH_ONES
    nheads = nblk * D_KV_HEADS
    k = k_ref[0]

    nchunk = tq // qu

    def logits_chunk(idx, c):
        j, g = divmod(idx, D_KV_HEADS)
        qj = q_ref[0, c * qu:(c + 1) * qu, j * LANES:(j + 1) * LANES]
        sel = lo if g == 0 else jnp.logical_not(lo)
        s = lax.dot_general(k, jnp.where(sel, qj, jnp.zeros_like(qj)), (((1,), (1,)), ((), ())),
                            preferred_element_type=F32)
        s_sc[idx % s_sc.shape[0], :, c * qu:(c + 1) * qu] = s
        return jnp.max(s, axis=0, keepdims=True)

    def finish_logits(idx, mcs):
        m_prev = m_sc[idx, 0:1, :]
        m_new = jnp.maximum(m_prev, jnp.concatenate(mcs, axis=1))
        m_sc[idx, 0:1, :] = m_new
        return m_new, jnp.exp2(m_prev - m_new)

    def value_chunk(idx, c, m_new, alpha):
        j, g = divmod(idx, D_KV_HEADS)
        rows = slice(g * grows, (g + 1) * grows)
        qcols = slice(c * qu, (c + 1) * qu)
        pv = None
        for u in range(tk // ku):
            keys = slice(u * ku, (u + 1) * ku)
            p = jnp.exp2(s_sc[idx % s_sc.shape[0], keys, qcols] - m_new[:, qcols])
            d = jnp.dot(vt_ref[0, rows, keys], p.astype(BF16), preferred_element_type=F32)
            pv = d if pv is None else pv + d
        acc_sc[j, rows, qcols] = alpha[:, qcols] * acc_sc[j, rows, qcols] + pv

    def logits_pass(idx):
        return finish_logits(idx, [logits_chunk(idx, c) for c in range(nchunk)])

    pending = [logits_pass(i) for i in range(min(FLASH_AHEAD, nheads))]
    for idx in range(nheads):
        if idx + FLASH_AHEAD < nheads:
            pending.append(logits_pass(idx + FLASH_AHEAD))
        stats = pending.pop(0)
        for c in range(nchunk):
            value_chunk(idx, c, *stats)

    @pl.when(kv == pl.num_programs(2) - 1)
    def _():
        for j in range(nblk):
            parts = []
            for g in range(D_KV_HEADS):
                num = acc_sc[j, g * grows:g * grows + D_HEAD_DIM, :]
                den = acc_sc[j, g * grows + D_HEAD_DIM:g * grows + D_HEAD_DIM + 1, :]
                parts.append(num / den)
            o_ref[0, :, j * LANES:(j + 1) * LANES] = (
                jnp.concatenate(parts, axis=0).T.astype(o_ref.dtype))


def _flash(q, k, vt, tq, tk):
    bsz, seq, qw = q.shape
    kw = k.shape[-1]
    vrows = vt.shape[1]
    assert FLASH_LOGIT_BUFS > FLASH_AHEAD
    return pl.pallas_call(
        _flash_kernel,
        out_shape=jax.ShapeDtypeStruct((bsz, seq, qw), BF16),
        grid=(bsz, seq // tq, seq // tk),
        in_specs=[pl.BlockSpec((1, tq, qw), lambda b, i, j: (b, i, 0)),
                  pl.BlockSpec((1, tk, kw), lambda b, i, j: (b, j, 0)),
                  pl.BlockSpec((1, vrows, tk), lambda b, i, j: (b, 0, j))],
        out_specs=pl.BlockSpec((1, tq, qw), lambda b, i, j: (b, i, 0)),
        scratch_shapes=[pltpu.VMEM((D_Q_HEADS, 8, tq), F32),
                        pltpu.VMEM((qw // LANES, vrows, tq), F32),
                        pltpu.VMEM((FLASH_LOGIT_BUFS, tk, tq), F32)],
        compiler_params=_cparams("parallel", "parallel", "arbitrary"),
        name="flash",
    )(q, k, vt)


def kernel(x, c, t5_bias, hgrn_lb_logits, ada_w, ada_b, norm_gains, ab_w_in, ab_w_out,
           hgrn_out_norm, cd_w_in, cd_w_out, qk_norm, ffn_w_in, ffn_w_out):
    bsz, seq, d = x.shape
    depth = ada_w.shape[0]
    mod = _ada_mod(c.astype(F32), ada_w, ada_b)
    perm = _head_perm()
    aw = A_HEADS * A_HEAD_DIM
    cw = C_GROUPS * C_WIDTH
    qw = D_Q_HEADS * D_HEAD_DIM
    tm_in = min(512, seq)
    ffn_wi = ffn_w_in.astype(BF16)
    ffn_wo = ffn_w_out.astype(BF16)
    for layer in range(depth):
        sh_m, sc_m, g_m, sh_f, sc_f, g_f = [mod[layer, :, i * d:(i + 1) * d] for i in range(6)]
        gains = norm_gains[layer]
        j = layer // 2
        tail = dict(x=x, gain_m=gains[1], gate_m=g_m, g1=gains[2], sc=sc_f, sh=sh_f,
                    wi_bf16=ffn_wi, wo_bf16=ffn_wo, layer=layer, g2=gains[3], gate_f=g_f)
        if layer % 2 == 0:
            w_in = ab_w_in[j].astype(BF16)
            *qkv_cm, z = _inproj_cm(x, gains[0], sc_m, sh_m, w_in, 3 * aw, tm_in)
            branches = [_dilated_branch(cm, t5_bias, window, dil)
                        for cm, (window, dil) in zip(qkv_cm, DIL_CFG)]
            o_f, o_b = _hgrn(z, hgrn_lb_logits, layer, 0)
            g_col = 3 * B_HEADS * B_DK + B_HEADS * B_DV
            x = _tail0(branches, o_f, o_b, z, g_col, hgrn_out_norm[j], ab_w_out[j].astype(BF16),
                       tm=min(TAIL_SUB, seq), **tail)
        else:
            w_full = cd_w_in[j]
            w_in = jnp.concatenate([w_full[:, :cw], w_full[:, cw:cw + qw][:, perm],
                                    w_full[:, cw + qw:]], axis=1).astype(BF16)
            u, qn, kn, vn = _inproj_qk(x, gains[0], sc_m, sh_m, w_in, qk_norm[j], min(1024, seq))
            c_out = _fourier_mixer(u)
            d_out = _flash(qn, kn, vn, min(FLASH_TQ, seq), min(FLASH_TK, seq))
            w_out_full = cd_w_out[j]
            w_out = jnp.concatenate([w_out_full[:cw], w_out_full[cw:][perm]], axis=0).astype(BF16)
            x = _tail1(c_out, d_out, w_out, tm=min(TAIL_BLOCK, seq), **tail)
    return x
```

```python
import functools

import numpy as np
import jax
import jax.numpy as jnp
from jax import lax
from jax.experimental import pallas as pl
from jax.experimental.pallas import tpu as pltpu

F32 = jnp.float32
BF16 = jnp.bfloat16
LANES = 128
MXU_WIDTH = 256
VMEM_LIMIT_BYTES = 56 * 2**20
NEG_INF = -1e30
EPS = 1e-6

GRID_W = 64
A_HEADS = 8
A_HEAD_DIM = 64
DIL_CFG = ((128, 1), (512, 4), (2048, 16))
N_BUCKETS = 32
T5_MAX_DIST = 1024
B_HEADS = 4
B_DK = 128
B_DV = 128
C_GROUPS = 4
C_WIDTH = 128
D_Q_HEADS = 8
D_KV_HEADS = 2
D_HEAD_DIM = 64
ROPE_THETA = 10000.0

DIL_TQ = 128
DIL_TILE = 512
HGRN_BLOCK = 1024
HGRN_SUB = 256
HGRN_BCAST_MIN = 8
FLASH_TQ = 512
FLASH_TK = 8192
FLASH_LOGIT_BUFS = 2
FLASH_KEY_UNIT = 256
FLASH_QUERY_UNIT = 1024
FLASH_AHEAD = 1
FLASH_ONES = 16
TAIL_BLOCK = 512
TAIL_SUB = 512
FFT_N2 = 128
FFT_BATCH = 8
FFT_PAD = 8
LOG2E = 1.4426950408889634


def _cparams(*sem):
    return pltpu.CompilerParams(dimension_semantics=sem, vmem_limit_bytes=VMEM_LIMIT_BYTES)


def _const_spec(shape):
    nd = len(shape)
    return pl.BlockSpec(shape, lambda *_: (0,) * nd, pipeline_mode=pl.Buffered(1))


def _sigmoid(x):
    return 1.0 / (1.0 + jnp.exp(-x))


def _dot(a, b):
    return jnp.dot(a.astype(BF16), b.astype(BF16), preferred_element_type=F32)


def _dot_nt(a, b):
    return lax.dot_general(a.astype(BF16), b.astype(BF16), (((1,), (1,)), ((), ())),
                           preferred_element_type=F32)


def _split2(a):
    hi = a.astype(BF16)
    lo = (a - hi.astype(F32)).astype(BF16)
    return hi, lo


def _dot_tab(tab_hi, tab_lo, x, *, tab_left):
    x_hi, x_lo = _split2(x)
    if tab_left:
        d = lambda t, v: jnp.dot(t, v, preferred_element_type=F32)
    else:
        d = lambda t, v: jnp.dot(v, t, preferred_element_type=F32)
    return d(tab_hi, x_hi) + (d(tab_hi, x_lo) + d(tab_lo, x_hi))


def _rms(x, gain):
    ms = jnp.mean(x * x, axis=-1, keepdims=True)
    return x * lax.rsqrt(ms + EPS) * gain


def _np_split2(t):
    t = np.asarray(t, np.float32)
    hi = jnp.asarray(t, F32).astype(BF16)
    lo = (jnp.asarray(t, F32) - hi.astype(F32)).astype(BF16)
    return hi, lo


def _mod_kernel(c_ref, w_ref, b_ref, o_ref):
    c = c_ref[...]
    o_ref[0] = _dot(c * _sigmoid(c), w_ref[0]) + b_ref[0]


def _ada_mod(c, ada_w, ada_b):
    depth, d, n6 = ada_w.shape
    bsz = c.shape[0]
    rows = 8
    cp = jnp.zeros((rows, d), F32).at[:bsz].set(c)
    tn = n6 // 4
    out = pl.pallas_call(
        _mod_kernel,
        out_shape=jax.ShapeDtypeStruct((depth, rows, n6), F32),
        grid=(depth, n6 // tn),
        in_specs=[pl.BlockSpec((rows, d), lambda l, j: (0, 0)),
                  pl.BlockSpec((1, d, tn), lambda l, j: (l, 0, j)),
                  pl.BlockSpec((1, 1, tn), lambda l, j: (l, 0, j))],
        out_specs=pl.BlockSpec((1, rows, tn), lambda l, j: (l, 0, j)),
        compiler_params=_cparams("parallel", "parallel"),
        name="ada_mod",
    )(cp, ada_w, ada_b.reshape(depth, 1, n6))
    return out[:, :bsz]


def _inproj_cm_kernel(x_ref, gain_ref, sc_ref, sh_ref, w_ref, *refs):
    cm_refs, rest_ref, zs_sc, zc_sc = refs[:-3], refs[-3], refs[-2], refs[-1]
    h = _rms(x_ref[0], gain_ref[...]) * (1.0 + sc_ref[0]) + sh_ref[0]
    z = _dot(h, w_ref[...])
    nblk, tm, _ = zs_sc.shape
    rest_ref[0] = z[:, nblk * LANES:]
    nq = A_HEADS * A_HEAD_DIM // LANES
    for c in range(nblk):
        blk = z[:, c * LANES:(c + 1) * LANES]
        zs_sc[c] = blk * (A_HEAD_DIM ** -0.5 * LOG2E) if c < nq else blk
    src, sd = zs_sc, 1
    for level, (cm_ref, (_, dil)) in enumerate(zip(cm_refs, DIL_CFG)):
        step, n = dil // sd, tm // dil
        keep = dil > 1 and level + 1 < len(DIL_CFG)
        for rs in range(sd):
            for cc in range(step):
                r = rs + sd * cc
                for c in range(nblk):
                    rows = src[c, pl.ds(rs * (tm // sd) + cc, n, stride=step), :]
                    cm_ref[0, r, :, c * LANES:(c + 1) * LANES] = rows.astype(BF16)
                    if keep:
                        zc_sc[c, r * n:(r + 1) * n, :] = rows
        if keep:
            src, sd = zc_sc, dil


def _inproj_cm(x, gain, sc, sh, w_bf16, na, tm):
    bsz, seq, d = x.shape
    n = w_bf16.shape[1]
    vec = pl.BlockSpec((1, 1, d), lambda b, i: (b, 0, 0))
    dils = [dl for _, dl in DIL_CFG]
    assert dils[0] == 1 and all(b % a == 0 for a, b in zip(dils, dils[1:]))
    return pl.pallas_call(
        _inproj_cm_kernel,
        out_shape=tuple([jax.ShapeDtypeStruct((bsz, dl, seq // dl, na), BF16) for dl in dils]
                        + [jax.ShapeDtypeStruct((bsz, seq, n - na), F32)]),
        grid=(bsz, seq // tm),
        in_specs=[pl.BlockSpec((1, tm, d), lambda b, i: (b, i, 0)),
                  _const_spec((1, d)), vec, vec, _const_spec((d, n))],
        out_specs=tuple([pl.BlockSpec((1, dl, tm // dl, na), lambda b, i: (b, 0, i, 0)) for dl in dils]
                        + [pl.BlockSpec((1, tm, n - na), lambda b, i: (b, i, 0))]),
        scratch_shapes=[pltpu.VMEM((na // LANES, tm, LANES), F32),
                        pltpu.VMEM((na // LANES, tm, LANES), F32)],
        compiler_params=_cparams("parallel", "parallel"),
        name="inproj_cm",
    )(x, gain.reshape(1, d), sc.reshape(bsz, 1, d), sh.reshape(bsz, 1, d), w_bf16)


def _t5_buckets(rel):
    half = N_BUCKETS // 2
    max_exact = half // 2
    n = np.abs(rel)
    large = max_exact + (np.log(np.maximum(n, 1) / max_exact) / np.log(T5_MAX_DIST / max_exact)
                         * (half - max_exact)).astype(np.int32)
    large = np.minimum(large, half - 1)
    return (np.where(rel > 0, half, 0) + np.where(n < max_exact, n, large)).astype(np.int32)


def _dil_bias(t5_bias, window, dil, tq):
    half = (window // 2) // dil
    assert half == tq // 2
    rel = np.arange(2 * tq)[None, :] - half - np.arange(tq)[:, None]
    inside = np.abs(rel) <= half
    buckets = _t5_buckets(np.where(inside, rel, 0) * dil)
    onehot =jnp.asarray(np.eye(N_BUCKETS, dtype=np.float32)[buckets])
    bias = jnp.einsum("qkn,nh->hqk", onehot, t5_bias.astype(F32), precision=lax.Precision.HIGHEST)
    return jnp.where(jnp.asarray(inside)[None], bias * LOG2E, NEG_INF)


def _dil_kernel(q_ref, kp_ref, kc_ref, kn_ref, vp_ref, vc_ref, vn_ref, bias_ref, o_ref, lse_ref,
                *, class_len):
    i = pl.program_id(2)
    sub, hq, tile = DIL_TQ, DIL_TQ // 2, q_ref.shape[2]
    kwin = jnp.concatenate([kp_ref[0, 0], kc_ref[0, 0], kn_ref[0, 0]], axis=0)
    vwin = jnp.concatenate([vp_ref[0, 0], vc_ref[0, 0], vn_ref[0, 0]], axis=0)
    lane = lax.broadcasted_iota(jnp.int32, (1, LANES), 1)
    lo = lane < A_HEAD_DIM
    nblk = A_HEADS // 2
    units = [(jt, j) for jt in range(tile // sub) for j in range(nblk)]
    logits = []
    for jt, j in units:
        cols = slice(j * LANES, (j + 1) * LANES)
        qj = q_ref[0, 0, jt * sub:(jt + 1) * sub, cols]
        zero = jnp.zeros_like(qj)
        q2 = jnp.concatenate([jnp.where(lo, qj, zero), jnp.where(lo, zero, qj)], axis=0)
        s = lax.dot_general(q2, kwin[jt * sub:jt * sub + 2 * sub, cols], (((1,), (1,)), ((), ())),
                            preferred_element_type=F32)
        kpos = i * tile + jt * sub - hq + lax.broadcasted_iota(jnp.int32, (1, 2 * sub), 1)
        valid = jnp.logical_and(kpos >= 0, kpos < class_len)
        logits.append(jnp.where(valid, s + bias_ref[j], NEG_INF))
    s_all = jnp.concatenate(logits, axis=0)
    m = jnp.max(s_all, axis=-1, keepdims=True)
    p32 = jnp.exp2(s_all - m)
    l = jnp.sum(p32, axis=-1, keepdims=True)
    p = p32.astype(BF16)
    rinv = 1.0 / l
    lse = m + jnp.log2(l)
    for jt in range(tile // sub):
        lse_all = jnp.zeros((sub, LANES), F32)
        for j in range(nblk):
            cols = slice(j * LANES, (j + 1) * LANES)
            r0 = (jt * nblk + j) * 2 * sub
            o2 = jnp.dot(p[r0:r0 + 2 * sub], vwin[jt * sub:jt * sub + 2 * sub, cols],
                         preferred_element_type=F32) * rinv[r0:r0 + 2 * sub]
            lse_all = jnp.where(lane == 2 * j, lse[r0:r0 + sub], lse_all)
            lse_all = jnp.where(lane == 2 * j + 1, lse[r0 + sub:r0 + 2 * sub], lse_all)
            o_ref[0, 0, jt * sub:(jt + 1) * sub, cols] = (
                jnp.where(lo, o2[:sub], o2[sub:]).astype(o_ref.dtype))
        lse_ref[0, 0, jt * sub:(jt + 1) * sub, :] = lse_all


def _dilated_branch(qkv_cm, t5_bias, window, dil):
    bsz, _, cl, width = qkv_cm.shape
    aw = A_HEADS * A_HEAD_DIM
    tile, hq = min(DIL_TILE, cl), DIL_TQ // 2
    nt = cl // tile
    per = tile // hq
    nh = cl // hq

    def cur(col):
        return pl.BlockSpec((1, 1, tile, aw), lambda b, r, i: (b, r, i, col))

    def prev(col):
        return pl.BlockSpec((1, 1, hq, aw), lambda b, r, i: (b, r, jnp.maximum(i * per - 1, 0), col))

    def nxt(col):
        return pl.BlockSpec((1, 1, hq, aw),
                            lambda b, r, i: (b, r, jnp.minimum((i + 1) * per, nh - 1), col))

    return pl.pallas_call(
        functools.partial(_dil_kernel, class_len=cl),
        out_shape=(jax.ShapeDtypeStruct((bsz, dil, cl, aw), BF16),
                   jax.ShapeDtypeStruct((bsz, dil, cl, LANES), F32)),
        grid=(bsz, dil, nt),
        in_specs=[cur(0), prev(1), cur(1), nxt(1), prev(2), cur(2), nxt(2),
                  _const_spec((A_HEADS // 2, 2 * DIL_TQ, 2 * DIL_TQ))],
        out_specs=(pl.BlockSpec((1, 1, tile, aw), lambda b, r, i: (b, r, i, 0)),
                   pl.BlockSpec((1, 1, tile, LANES), lambda b, r, i: (b, r, i, 0))),
        compiler_params=_cparams("parallel", "parallel", "parallel"),
        name=f"dilated_d{dil}",
    )(*([qkv_cm] * 7),
      _dil_bias(t5_bias, window, dil, DIL_TQ).reshape(A_HEADS // 2, 2 * DIL_TQ, 2 * DIL_TQ))


def _merge_branches(a_refs, l_refs, e_ref, a_scs, l_scs, sub):
    tm = l_scs[0].shape[0]
    accs, lses = [], []
    for (_, dil), a_ref, l_ref, a_sc, l_sc in zip(DIL_CFG, a_refs, l_refs, a_scs, l_scs):
        nblk = a_sc.shape[0]
        for r in range(dil):
            rows = pl.ds(r, tm // dil, stride=dil)
            src = pl.ds(sub * (tm // dil), tm // dil)
            for c in range(nblk):
                a_sc[c, rows, :] = a_ref[0, r, src, c * LANES:(c + 1) * LANES].astype(F32)
            l_sc[rows, :] = l_ref[0, r, src, :]
        accs.append(jnp.concatenate([a_sc[c] for c in range(nblk)], axis=1))
        lses.append(l_sc[...])
    mx = functools.reduce(jnp.maximum, lses)
    ws = [jnp.exp2(x - mx) for x in lses]
    tot = functools.reduce(jnp.add, ws)
    out = None
    for w, a in zip(ws, accs):
        w_hi, w_lo = _split2(w / tot)
        wide = (jnp.dot(w_hi, e_ref[...], preferred_element_type=F32)
                + jnp.dot(w_lo, e_ref[...], preferred_element_type=F32))
        out = wide * a if out is None else out + wide * a
    return out


def _merge_operands(branch_outs, tm):
    aw = branch_outs[0][0].shape[-1]
    expand = np.zeros((LANES, aw), np.float32)
    for h in range(A_HEADS):
        expand[h, h * A_HEAD_DIM:(h + 1) * A_HEAD_DIM] = 1.0
    dils = [d for _, d in DIL_CFG]
    specs = ([pl.BlockSpec((1, d, tm // d, aw), lambda b, i: (b, 0, i, 0)) for d in dils]
             + [pl.BlockSpec((1, d, tm // d, LANES), lambda b, i: (b, 0, i, 0)) for d in dils]
             + [_const_spec(expand.shape)])
    arrays = [o for o, _ in branch_outs] + [l for _, l in branch_outs] + [jnp.asarray(expand, BF16)]
    scratch = ([pltpu.VMEM((aw // LANES, TAIL_SUB, LANES), F32) for _ in dils]
               + [pltpu.VMEM((TAIL_SUB, LANES), F32) for _ in dils])
    return arrays, specs, scratch


def _hgrn_consts(reverse):
    t = HGRN_SUB
    r = np.arange(t)
    u = r[None, :]
    row = r[:, None]
    nmats = [(u >= row) if reverse else (u <= row)]
    masks = []
    m = t // 2
    while m >= 1:
        grp = r // (2 * m)
        in_first = (r % (2 * m)) < m
        same = grp[:, None] == grp[None, :]
        if reverse:
            beta = (grp * 2 * m + m)[:, None]
            n = np.where(in_first[:, None], (u >= row) & (u < beta), (u >= beta) & (u < row))
            mask = same & in_first[:, None] & ~in_first[None, :]
        else:
            beta = (grp * 2 * m + m - 1)[:, None]
            n = np.where(in_first[:, None], (u > row) & (u <= beta), (u > beta) & (u <= row))
            mask = same & ~in_first[:, None] & in_first[None, :]
        if m < HGRN_BCAST_MIN:
            nmats.append(n)
        masks.append(mask)
        m //= 2
    masks.append(np.eye(t, dtype=bool))
    nmat = jnp.asarray(np.concatenate(nmats, axis=0), F32).astype(BF16)
    return nmat, jnp.asarray(np.stack(masks), F32)


def _hgrn_block(q, f, v, lb, st, nmat_ref, mask_ref, reverse):
    t = HGRN_SUB
    nlev = mask_ref.shape[0] - 1
    qs = q * (B_DK ** -0.5)
    fa = lb + (1.0 - lb) * _sigmoid(f)
    kk = 1.0 - fa
    g_hi, g_lo = _split2(jnp.log2(fa))
    ex = jnp.dot(nmat_ref[...], jnp.concatenate([g_hi, g_lo], axis=1), preferred_element_type=F32)
    ex = ex[:, :B_DK] + ex[:, B_DK:]
    b = ex[:t]
    btot = b[0:1] if reverse else b[t - 1:t]
    a = mask_ref[nlev] * _dot_nt(qs, kk)
    fine = 1
    for l in range(nlev):
        m = t >> (l + 1)
        if m >= HGRN_BCAST_MIN:
            ref = jnp.concatenate(
                [jnp.broadcast_to(b[beta:beta + 1], (2 * m, B_DK))
                 for beta in range(m if reverse else m - 1, t, 2 * m)], axis=0)
            e = jnp.exp2(-jnp.abs(b - ref))
        else:
            e = jnp.exp2(ex[fine * t:(fine + 1) * t])
            fine += 1
        a = a + mask_ref[l] * _dot_nt(qs * e, kk * e)
    out = _dot(a, v) + _dot_nt(qs * jnp.exp2(b), st)
    khat = (kk * jnp.exp2(btot - b)).astype(BF16)
    st_new = st * jnp.exp2(btot) + jnp.dot(v.T.astype(BF16), khat, preferred_element_type=F32)
    return out, st_new


def _hgrn_lb(lg_ref, layer):
    lg = [lg_ref[l, 0] for l in range(lg_ref.shape[0])]
    mx = functools.reduce(jnp.maximum, lg)
    e = [jnp.exp(x - mx) for x in lg]
    return functools.reduce(jnp.add, e[:layer + 1]) / functools.reduce(jnp.add, e)


def _hgrn_kernel(qf_ref, ff_ref, vf_ref, qb_ref, fb_ref, vb_ref, lgf_ref, lgb_ref,
                 nf_ref, mf_ref, nb_ref, mb_ref, of_ref, ob_ref, sf_sc, sb_sc, *, layer):
    @pl.when(pl.program_id(2) == 0)
    def _():
        sf_sc[...] = jnp.zeros_like(sf_sc)
        sb_sc[...] = jnp.zeros_like(sb_sc)

    nsub = HGRN_BLOCK // HGRN_SUB
    chains = ((qf_ref, ff_ref, vf_ref, lgf_ref, sf_sc, nf_ref, mf_ref, of_ref, False),
              (qb_ref, fb_ref, vb_ref, lgb_ref, sb_sc, nb_ref, mb_ref, ob_ref, True))
    for q_ref, f_ref, v_ref, lg_ref, st_sc, n_ref, m_ref, o_ref, reverse in chains:
        lb = _hgrn_lb(lg_ref, layer)
        st = st_sc[...]
        for sub in (reversed(range(nsub)) if reverse else range(nsub)):
            rows = pl.ds(sub * HGRN_SUB, HGRN_SUB)
            o, st = _hgrn_block(q_ref[0, rows, :], f_ref[0, rows, :], v_ref[0, rows, :], lb, st,
                                n_ref, m_ref, reverse)
            o_ref[0, rows, :] = o.astype(o_ref.dtype)
        st_sc[...] = st


def _hgrn(z, lb_logits, layer, col0):
    bsz, seq, _ = z.shape
    t = HGRN_BLOCK
    nb = seq // t
    c0 = col0 // LANES
    hw = B_HEADS

    def zspec(group, rev):
        return pl.BlockSpec(
            (1, t, LANES),
            lambda b, h, j: (b, (nb - 1 - j) if rev else j, c0 + group * hw + h))

    def lgspec(direction):
        return pl.BlockSpec((lb_logits.shape[0], 1, 1, LANES),
                            lambda b, h, j: (0, direction * hw + h, 0, 0))

    nf, mf = _hgrn_consts(False)
    nbw, mbw = _hgrn_consts(True)
    lg = lb_logits.astype(F32).reshape(lb_logits.shape[0], 2 * hw, 1, LANES)
    o_shape = jax.ShapeDtypeStruct((bsz, seq, hw * B_DV), BF16)
    return pl.pallas_call(
        functools.partial(_hgrn_kernel, layer=layer),
        out_shape=(o_shape, o_shape),
        grid=(bsz, hw, nb),
        in_specs=[zspec(0, False), zspec(1, False), zspec(3, False),
                  zspec(0, True), zspec(2, True), zspec(3, True),
                  lgspec(0), lgspec(1),
                  _const_spec(nf.shape), _const_spec(mf.shape),
                  _const_spec(nbw.shape), _const_spec(mbw.shape)],
        out_specs=(pl.BlockSpec((1, t, LANES), lambda b, h, j: (b, j, h)),
                   pl.BlockSpec((1, t, LANES), lambda b, h, j: (b, nb - 1 - j, h))),
        scratch_shapes=[pltpu.VMEM((B_DV, B_DK), F32), pltpu.VMEM((B_DV, B_DK), F32)],
        compiler_params=_cparams("parallel", "parallel", "arbitrary"),
        name="hgrn",
    )(z, z, z, z, z, z, lg, lg, nf, mf, nbw, mbw)


def _mix0_y(sub, *refs):
    nbr = len(DIL_CFG)
    a_refs, l_refs, e_ref = refs[:nbr], refs[nbr:2 * nbr], refs[2 * nbr]
    of_ref, ob_ref, g_ref, on_ref, w_ref = refs[2 * nbr + 1:2 * nbr + 6]
    scratch = refs[2 * nbr + 6:]
    a = _merge_branches(a_refs, l_refs, e_ref, scratch[:nbr], scratch[nbr:], sub)
    rows = pl.ds(sub * TAIL_SUB, TAIL_SUB)
    o = of_ref[0, rows, :].astype(F32) + ob_ref[0, rows, :].astype(F32)
    g = g_ref[0, rows, :]
    parts = [_rms(o[:, h * B_DV:(h + 1) * B_DV], on_ref[...]) for h in range(B_HEADS)]
    bn = jnp.concatenate(parts, axis=-1) * (g * _sigmoid(g))
    na = a.shape[-1]
    return _dot(a, w_ref[:na, :]) + _dot(bn, w_ref[na:, :])


def _mix1_y(sub, c_ref, d_ref, w_ref):
    rows = pl.ds(sub * TAIL_SUB, TAIL_SUB)
    c = jnp.concatenate([c_ref[0, g, rows, :] for g in range(c_ref.shape[1])], axis=1)
    nc = c.shape[-1]
    return _dot(c, w_ref[:nc, :]) + _dot(d_ref[0, rows, :], w_ref[nc:, :])


def _row_spec(tm, width, col=0):
    return pl.BlockSpec((1, tm, width), lambda b, i: (b, i, col))


def _tail_kernel(*refs, nmix, mix_fn, bounds):
    (x_ref, gm_ref, gatem_ref, g1_ref, sc_ref, sh_ref, wi_ref, wo_ref, g2_ref, gatef_ref,
     o_ref) = refs[nmix:nmix + 11]
    hidden = wo_ref.shape[0]
    for sub in range(x_ref.shape[1] // TAIL_SUB):
        rows = pl.ds(sub * TAIL_SUB, TAIL_SUB)
        y_mix = mix_fn(sub, *refs[:nmix], *refs[nmix + 11:])
        x1 = x_ref[0, rows, :] + gatem_ref[0] * _rms(y_mix, gm_ref[...])
        h = (_rms(x1, g1_ref[...]) * (1.0 + sc_ref[0]) + sh_ref[0]).astype(BF16)
        y = None
        for c0, c1 in zip(bounds, bounds[1:]):
            gt = jnp.dot(h, wi_ref[:, c0:c1], preferred_element_type=F32)
            up = jnp.dot(h, wi_ref[:, hidden + c0:hidden + c1], preferred_element_type=F32)
            part = _dot(gt * _sigmoid(gt) * up, wo_ref[c0:c1, :])
            y = part if y is None else y + part
        o_ref[0, rows, :] = x1 + gatef_ref[0] * _rms(y, g2_ref[...])


def _layer_tail(mix_fn, mix_args, mix_specs, x, gain_m, gate_m, g1, sc, sh, wi_bf16, wo_bf16, layer,
                g2, gate_f, tm, name, mix_scratch=()):
    bsz, seq, d = x.shape
    vec = pl.BlockSpec((1, 1, d), lambda b, i: (b, 0, 0))
    hidden = wo_bf16.shape[1]

    def layer_slab(w):
        return pl.BlockSpec((None,) + w.shape[1:], lambda b, i: (layer, 0, 0),
                            pipeline_mode=pl.Buffered(1))

    ntile = hidden // MXU_WIDTH
    assert ntile * MXU_WIDTH == hidden
    bounds = (0, (ntile + 1) // 2 * MXU_WIDTH, hidden)
    row = lambda v: v.reshape(1, d)
    per_batch = lambda v: v.reshape(bsz, 1, d)
    return pl.pallas_call(
        functools.partial(_tail_kernel, nmix=len(mix_args), mix_fn=mix_fn, bounds=bounds),
        out_shape=jax.ShapeDtypeStruct(x.shape, F32),
        grid=(bsz, seq // tm),
        in_specs=list(mix_specs) + [_row_spec(tm, d), _const_spec((1, d)), vec, _const_spec((1, d)),
                                    vec, vec, layer_slab(wi_bf16), layer_slab(wo_bf16),
                                    _const_spec((1, d)), vec],
        out_specs=_row_spec(tm, d),
        scratch_shapes=list(mix_scratch),
        compiler_params=_cparams("parallel", "parallel"),
        name=name,
    )(*mix_args, x, row(gain_m), per_batch(gate_m), row(g1), per_batch(sc), per_batch(sh),
      wi_bf16, wo_bf16, row(g2), per_batch(gate_f))


def _tail0(branches, o_f, o_b, z, g_col, out_norm, w_bf16, tm, **kw):
    wv = B_HEADS * B_DV
    m_arrays, m_specs, m_scratch = _merge_operands(branches, tm)
    specs = m_specs + [_row_spec(tm, wv), _row_spec(tm, wv), _row_spec(tm, wv, g_col // wv),
                       _const_spec((1, B_DV)), _const_spec(w_bf16.shape)]
    args = m_arrays + [o_f, o_b, z, out_norm.reshape(1, B_DV), w_bf16]
    return _layer_tail(_mix0_y, args, specs, tm=tm, name="tail0", mix_scratch=m_scratch, **kw)


def _tail1(c_out, d_out, w_bf16, tm, **kw):
    specs = [pl.BlockSpec((1, c_out.shape[1], tm, c_out.shape[3]), lambda b, i: (b, 0, i, 0)),
             _row_spec(tm, d_out.shape[-1]), _const_spec(w_bf16.shape)]
    return _layer_tail(_mix1_y, (c_out, d_out, w_bf16), specs, tm=tm, name="tail1", **kw)


def _fft_kernel(u_ref, f1_ref, twc_ref, tws_ref, f2_ref, fw_ref, o_ref, u_sc, p_sc, y_sc,
                *, scale, n1, n2):
    pu = n2 + FFT_PAD
    pp = 2 * n1 + FFT_PAD
    py = n1 + FFT_PAD
    f1 = f1_ref[...].astype(BF16)
    f2 = f2_ref[...].astype(BF16)
    fw = fw_ref[...].astype(BF16)
    for i1 in range(n1):
        u_sc[i1 * pu:i1 * pu + n2, :] = u_ref[0, 0, i1 * n2:(i1 + 1) * n2, :]

    nb = FFT_BATCH

    def stage1(blk, carry):
        i2s = [blk * nb + j for j in range(nb)]
        x = jnp.concatenate([u_sc[pl.ds(i2, n1, stride=pu), :] for i2 in i2s], axis=1)
        p = jnp.dot(f1, x.astype(BF16), preferred_element_type=F32)
        for j, i2 in enumerate(i2s):
            p_sc[pl.ds(pl.multiple_of(i2 * pp, 8), 2 * n1), :] = p[:, j * C_WIDTH:(j + 1) * C_WIDTH]
        return carry

    lax.fori_loop(0, n2 // nb, stage1, 0, unroll=2)

    def stage2(blk, carry):
        k1s = [blk * nb + j for j in range(nb)]
        qr, qi = [], []
        for k1 in k1s:
            tc = twc_ref[k1]
            ts = tws_ref[k1]
            pr = p_sc[pl.ds(k1, n2, stride=pp), :]
            pim = p_sc[pl.ds(n1 + k1, n2, stride=pp), :]
            qr.append(pr * tc + pim * ts)
            qi.append(pim * tc - pr * ts)
        q = jnp.concatenate([jnp.concatenate(qr, axis=1), jnp.concatenate(qi, axis=1)], axis=0)
        xx = jnp.dot(f2, q.astype(BF16), preferred_element_type=F32)
        xg = jnp.concatenate(
            [jnp.concatenate([xx[:n2, j * C_WIDTH:(j + 1) * C_WIDTH],
                              xx[n2:, j * C_WIDTH:(j + 1) * C_WIDTH]], axis=1) for j in range(nb)],
            axis=0)
        y = jnp.dot(xg.astype(BF16), fw, preferred_element_type=F32) * scale
        for j, k1 in enumerate(k1s):
            y_sc[pl.ds(k1, n2, stride=py), :] = y[j * n2:(j + 1) * n2]
        return carry

    lax.fori_loop(0, n1 // nb, stage2, 0, unroll=2)
    for k2 in range(n2):
        o_ref[0, 0, k2 * n1:(k2 + 1) * n1, :] = y_sc[k2 * py:k2 * py + n1, :]


def _fourier_mixer(u):
    bsz, ngroups, seq, width = u.shape
    n2 = FFT_N2
    n1 = seq // n2
    assert n1 * n2 == seq and width == C_WIDTH and n1 % 8 == 0
    a1 = 2.0 * np.pi * np.outer(np.arange(n1), np.arange(n1)) / n1
    f1 = np.concatenate([np.cos(a1), -np.sin(a1)], axis=0)
    a2 = 2.0 * np.pi * np.outer(np.arange(n2), np.arange(n2)) / n2
    c2, s2 = np.cos(a2), np.sin(a2)
    f2 = np.block([[c2, s2], [-s2, c2]])
    aw = 2.0 * np.pi * np.outer(np.arange(C_WIDTH), np.arange(C_WIDTH)) / C_WIDTH
    fw = np.concatenate([np.cos(aw), np.sin(aw)], axis=0)
    at = np.repeat((2.0 * np.pi * np.outer(np.arange(n1), np.arange(n2)) / seq)[:, :, None],
                   C_WIDTH, axis=2)
    consts = (jnp.asarray(f1, F32), jnp.asarray(np.cos(at), F32), jnp.asarray(np.sin(at), F32),
              jnp.asarray(f2, F32), jnp.asarray(fw, F32))
    blk = pl.BlockSpec((1, 1, seq, C_WIDTH), lambda b, g: (b, g, 0, 0))
    return pl.pallas_call(
        functools.partial(_fft_kernel, scale=float(1.0 / np.sqrt(seq * C_WIDTH)), n1=n1, n2=n2),
        out_shape=jax.ShapeDtypeStruct(u.shape, F32),
        grid=(bsz, ngroups),
        in_specs=[blk] + [_const_spec(c.shape) for c in consts],
        out_specs=blk,
        scratch_shapes=[pltpu.VMEM((n1 * (n2 + FFT_PAD), C_WIDTH), F32),
                        pltpu.VMEM((n2 * (2 * n1 + FFT_PAD), C_WIDTH), F32),
                        pltpu.VMEM((n2 * (n1 + FFT_PAD), C_WIDTH), F32)],
        compiler_params=_cparams("parallel", "parallel"),
        name="fft",
    )(u, *consts)


def _head_perm():
    rep = D_Q_HEADS // D_KV_HEADS
    cols = []
    for j in range(rep):
        for g in range(D_KV_HEADS):
            h = g * rep + j
            cols.extend(range(h * D_HEAD_DIM, (h + 1) * D_HEAD_DIM))
    return np.asarray(cols, np.int32)


def _rope_tables(seq):
    rows = seq // GRID_W
    row = jnp.repeat(jnp.arange(rows, dtype=F32), GRID_W)
    col = jnp.tile(jnp.arange(GRID_W, dtype=F32), rows)
    axis_dim = D_HEAD_DIM // 2
    inv_freq = jnp.power(ROPE_THETA, -jnp.arange(0, axis_dim, 2, dtype=F32) / axis_dim)
    ang_r = row[:, None] * inv_freq[None, :]
    ang_c = col[:, None] * inv_freq[None, :]
    cr, sr, cc, sc = jnp.cos(ang_r), jnp.sin(ang_r), jnp.cos(ang_c), jnp.sin(ang_c)
    cos = jnp.concatenate([cr, cr, cc, cc], axis=1)
    sin = jnp.concatenate([-sr, sr, -sc, sc], axis=1)
    reps = LANES // D_HEAD_DIM
    return jnp.tile(cos, (1, reps)), jnp.tile(sin, (1, reps))


def _inproj_qk_kernel(x_ref, gain_ref, sc_ref, sh_ref, w_ref, cos_ref, sin_ref, bd_h_ref, bd_l_ref,
                      gq_ref, gk_ref, u_ref, qo_ref, ko_ref, vo_ref):
    h = _rms(x_ref[0], gain_ref[...]) * (1.0 + sc_ref[0]) + sh_ref[0]
    z = _dot(h, w_ref[...])
    ngrp = u_ref.shape[1]
    for g in range(ngrp):
        u_ref[0, g] = z[:, g * LANES:(g + 1) * LANES]
    cos = cos_ref[...]
    sin = sin_ref[...]
    quarter = D_HEAD_DIM // 4
    lane = lax.broadcasted_iota(jnp.int32, (1, LANES), 1)
    first_of_pair = (lane // quarter) % 2 == 0

    def norm_rope(x, gain, scale):
        ms = _dot_tab(bd_h_ref[...], bd_l_ref[...], x * x, tab_left=False)
        xn = x * lax.rsqrt(ms + EPS) * gain
        partner = jnp.where(first_of_pair, pltpu.roll(xn, LANES - quarter, 1),
                            pltpu.roll(xn, quarter, 1))
        return ((xn * cos + partner * sin) * scale).astype(BF16)

    nq = qo_ref.shape[-1] // LANES
    for j in range(nq):
        qo_ref[0, :, j * LANES:(j + 1) * LANES] = norm_rope(
            z[:, (ngrp + j) * LANES:(ngrp + j + 1) * LANES], gq_ref[...], D_HEAD_DIM ** -0.5 * LOG2E)
    ko_ref[0] = norm_rope(z[:, (ngrp + nq) * LANES:(ngrp + nq + 1) * LANES], gk_ref[...], 1.0)
    vt = z[:, (ngrp + nq + 1) * LANES:(ngrp + nq + 2) * LANES].T
    ones = jnp.ones((FLASH_ONES, vt.shape[1]), F32)
    vo_ref[0] = jnp.concatenate(
        [piece for g in range(D_KV_HEADS)
         for piece in (vt[g * D_HEAD_DIM:(g + 1) * D_HEAD_DIM], ones)], axis=0).astype(BF16)


def _inproj_qk(x, gain, sc, sh, w_bf16, qk_norm_j, tm):
    bsz, seq, d = x.shape
    n = w_bf16.shape[1]
    qw = D_Q_HEADS * D_HEAD_DIM
    kw = D_KV_HEADS * D_HEAD_DIM
    assert kw == LANES and n == C_GROUPS * C_WIDTH + qw + 2 * kw
    vrows = D_KV_HEADS * (D_HEAD_DIM + FLASH_ONES)
    cos, sin = _rope_tables(seq)
    bd = np.kron(np.eye(LANES // D_HEAD_DIM), np.full((D_HEAD_DIM, D_HEAD_DIM), 1.0 / D_HEAD_DIM))
    bd_h, bd_l = _np_split2(bd)
    reps = LANES // D_HEAD_DIM
    gq = jnp.tile(qk_norm_j[0].astype(F32), reps).reshape(1, LANES)
    gk = jnp.tile(qk_norm_j[1].astype(F32), reps).reshape(1, LANES)
    tab = pl.BlockSpec((tm, LANES), lambda b, i: (i, 0))
    vec = pl.BlockSpec((1, 1, d), lambda b, i: (b, 0, 0))
    return pl.pallas_call(
        _inproj_qk_kernel,
        out_shape=(jax.ShapeDtypeStruct((bsz, C_GROUPS, seq, C_WIDTH), F32),
                   jax.ShapeDtypeStruct((bsz, seq, qw), BF16),
                   jax.ShapeDtypeStruct((bsz, seq, kw), BF16),
                   jax.ShapeDtypeStruct((bsz, vrows, seq), BF16)),
        grid=(bsz, seq // tm),
        in_specs=[pl.BlockSpec((1, tm, d), lambda b, i: (b, i, 0)),
                  _const_spec((1, d)), vec, vec, _const_spec((d, n)),
                  tab, tab, _const_spec(bd_h.shape), _const_spec(bd_l.shape),
                  _const_spec((1, LANES)), _const_spec((1, LANES))],
        out_specs=(pl.BlockSpec((1, C_GROUPS, tm, C_WIDTH), lambda b, i: (b, 0, i, 0)),
                   _row_spec(tm, qw), _row_spec(tm, kw),
                   pl.BlockSpec((1, vrows, tm), lambda b, i: (b, 0, i))),
        compiler_params=_cparams("parallel", "parallel"),
        name="inproj_qk",
    )(x, gain.reshape(1, d), sc.reshape(bsz, 1, d), sh.reshape(bsz, 1, d), w_bf16,
      cos, sin, bd_h, bd_l, gq, gk)


def _flash_kernel(q_ref, k_ref, vt_ref, o_ref, m_sc, acc_sc, s_sc):
    kv = pl.program_id(2)

    @pl.when(kv == 0)
    def _():
        m_sc[...] = jnp.full_like(m_sc, -jnp.inf)
        acc_sc[...] = jnp.zeros_like(acc_sc)

    lane = lax.broadcasted_iota(jnp.int32, (1, LANES), 1)
    lo = lane < D_HEAD_DIM
    nblk = q_ref.shape[-1] // LANES
    tq, tk = q_ref.shape[1], k_ref.shape[1]
    ku, qu = FLASH_KEY_UNIT, min(FLASH_QUERY_UNIT, tq)
    grows = D_HEAD_DIM + FLASH_ONES
    nheads = nblk * D_KV_HEADS
    k = k_ref[0]

    nchunk = tq // qu

    def logits_chunk(idx, c):
        j, g = divmod(idx, D_KV_HEADS)
        qj = q_ref[0, c * qu:(c + 1) * qu, j * LANES:(j + 1) * LANES]
        sel = lo if g == 0 else jnp.logical_not(lo)
        s = lax.dot_general(k, jnp.where(sel, qj, jnp.zeros_like(qj)), (((1,), (1,)), ((), ())),
                            preferred_element_type=F32)
        s_sc[idx % s_sc.shape[0], :, c * qu:(c + 1) * qu] = s
        return jnp.max(s, axis=0, keepdims=True)

    def finish_logits(idx, mcs):
        m_prev = m_sc[idx, 0:1, :]
        m_new = jnp.maximum(m_prev, jnp.concatenate(mcs, axis=1))
        m_sc[idx, 0:1, :] = m_new
        return m_new, jnp.exp2(m_prev - m_new)

    def value_chunk(idx, c, m_new, alpha):
        j, g = divmod(idx, D_KV_HEADS)
        rows = slice(g * grows, (g + 1) * grows)
        qcols = slice(c * qu, (c + 1) * qu)
        pv = None
        for u in range(tk // ku):
            keys = slice(u * ku, (u + 1) * ku)
            p = jnp.exp2(s_sc[idx % s_sc.shape[0], keys, qcols] - m_new[:, qcols])
            d = jnp.dot(vt_ref[0, rows, keys], p.astype(BF16), preferred_element_type=F32)
            pv = d if pv is None else pv + d
        acc_sc[j, rows, qcols] = alpha[:, qcols] * acc_sc[j, rows, qcols] + pv

    def logits_pass(idx):
        return finish_logits(idx, [logits_chunk(idx, c) for c in range(nchunk)])

    pending = [logits_pass(i) for i in range(min(FLASH_AHEAD, nheads))]
    for idx in range(nheads):
        if idx + FLASH_AHEAD < nheads:
            pending.append(logits_pass(idx + FLASH_AHEAD))
        stats = pending.pop(0)
        for c in range(nchunk):
            value_chunk(idx, c, *stats)

    @pl.when(kv == pl.num_programs(2) - 1)
    def _():
        for j in range(nblk):
            parts = []
            for g in range(D_KV_HEADS):
                num = acc_sc[j, g * grows:g * grows + D_HEAD_DIM, :]
                den = acc_sc[j, g * grows + D_HEAD_DIM:g * grows + D_HEAD_DIM + 1, :]
                parts.append(num / den)
            o_ref[0, :, j * LANES:(j + 1) * LANES] = (
                jnp.concatenate(parts, axis=0).T.astype(o_ref.dtype))


def _flash(q, k, vt, tq, tk):
    bsz, seq, qw = q.shape
    kw = k.shape[-1]
    vrows = vt.shape[1]
    assert FLASH_LOGIT_BUFS > FLASH_AHEAD
    return pl.pallas_call(
        _flash_kernel,
        out_shape=jax.ShapeDtypeStruct((bsz, seq, qw), BF16),
        grid=(bsz, seq // tq, seq // tk),
        in_specs=[pl.BlockSpec((1, tq, qw), lambda b, i, j: (b, i, 0)),
                  pl.BlockSpec((1, tk, kw), lambda b, i, j: (b, j, 0)),
                  pl.BlockSpec((1, vrows, tk), lambda b, i, j: (b, 0, j))],
        out_specs=pl.BlockSpec((1, tq, qw), lambda b, i, j: (b, i, 0)),
        scratch_shapes=[pltpu.VMEM((D_Q_HEADS, 8, tq), F32),
                        pltpu.VMEM((qw // LANES, vrows, tq), F32),
                        pltpu.VMEM((FLASH_LOGIT_BUFS, tk, tq), F32)],
        compiler_params=_cparams("parallel", "parallel", "arbitrary"),
        name="flash",
    )(q, k, vt)


def kernel(x, c, t5_bias, hgrn_lb_logits, ada_w, ada_b, norm_gains, ab_w_in, ab_w_out,
           hgrn_out_norm, cd_w_in, cd_w_out, qk_norm, ffn_w_in, ffn_w_out):
    bsz, seq, d = x.shape
    depth = ada_w.shape[0]
    mod = _ada_mod(c.astype(F32), ada_w, ada_b)
    perm = _head_perm()
    aw = A_HEADS * A_HEAD_DIM
    cw = C_GROUPS * C_WIDTH
    qw = D_Q_HEADS * D_HEAD_DIM
    tm_in = min(512, seq)
    ffn_wi = ffn_w_in.astype(BF16)
    ffn_wo = ffn_w_out.astype(BF16)
    for layer in range(depth):
        sh_m, sc_m, g_m, sh_f, sc_f, g_f = [mod[layer, :, i * d:(i + 1) * d] for i in range(6)]
        gains = norm_gains[layer]
        j = layer // 2
        tail = dict(x=x, gain_m=gains[1], gate_m=g_m, g1=gains[2], sc=sc_f, sh=sh_f,
                    wi_bf16=ffn_wi, wo_bf16=ffn_wo, layer=layer, g2=gains[3], gate_f=g_f)
        if layer % 2 == 0:
            w_in = ab_w_in[j].astype(BF16)
            *qkv_cm, z = _inproj_cm(x, gains[0], sc_m, sh_m, w_in, 3 * aw, tm_in)
            branches = [_dilated_branch(cm, t5_bias, window, dil)
                        for cm, (window, dil) in zip(qkv_cm, DIL_CFG)]
            o_f, o_b = _hgrn(z, hgrn_lb_logits, layer, 0)
            g_col = 3 * B_HEADS * B_DK + B_HEADS * B_DV
            x = _tail0(branches, o_f, o_b, z, g_col, hgrn_out_norm[j], ab_w_out[j].astype(BF16),
                       tm=min(TAIL_SUB, seq), **tail)
        else:
            w_full = cd_w_in[j]
            w_in = jnp.concatenate([w_full[:, :cw], w_full[:, cw:cw + qw][:, perm],
                                    w_full[:, cw + qw:]], axis=1).astype(BF16)
            u, qn, kn, vn = _inproj_qk(x, gains[0], sc_m, sh_m, w_in, qk_norm[j], min(1024, seq))
            c_out = _fourier_mixer(u)
            d_out = _flash(qn, kn, vn, min(FLASH_TQ, seq), min(FLASH_TK, seq))
            w_out_full = cd_w_out[j]
            w_out = jnp.concatenate([w_out_full[:cw], w_out_full[cw:][perm]], axis=0).astype(BF16)
            x = _tail1(c_out, d_out, w_out, tm=min(TAIL_BLOCK, seq), **tail)
    return x
```

```python
import functools

import numpy as np
import jax
import jax.numpy as jnp
from jax import lax
from jax.experimental import pallas as pl
from jax.experimental.pallas import tpu as pltpu

F32 = jnp.float32
BF16 = jnp.bfloat16
LANES = 128
MXU_WIDTH = 256
VMEM_LIMIT_BYTES = 56 * 2**20
NEG_INF = -1e30
EPS = 1e-6

GRID_W = 64
A_HEADS = 8
A_HEAD_DIM = 64
DIL_CFG = ((128, 1), (512, 4), (2048, 16))
N_BUCKETS = 32
T5_MAX_DIST = 1024
B_HEADS = 4
B_DK = 128
B_DV = 128
C_GROUPS = 4
C_WIDTH = 128
D_Q_HEADS = 8
D_KV_HEADS = 2
D_HEAD_DIM = 64
ROPE_THETA = 10000.0

INPROJ_CM_ROWS = 512
INPROJ_QK_ROWS = 1024
DIL_TQ = 128
DIL_TILE = 512
HGRN_BLOCK = 1024
HGRN_SUB = 256
HGRN_BCAST_MIN = 8
FLASH_TQ = 1024
FLASH_TK = 4096
FLASH_LOGIT_BUFS = 2
FLASH_KEY_UNIT = 256
FLASH_QUERY_UNIT = 1024
FLASH_AHEAD = 1
FLASH_ONES = 16
TAIL_BLOCK = 512
TAIL_SUB = 512
FFT_N2 = 128
FFT_BATCH = 8
FFT_PAD = 8
LOG2E = 1.4426950408889634


def _cparams(*sem):
    return pltpu.CompilerParams(dimension_semantics=sem, vmem_limit_bytes=VMEM_LIMIT_BYTES)


def _const_spec(shape):
    nd = len(shape)
    return pl.BlockSpec(shape, lambda *_: (0,) * nd, pipeline_mode=pl.Buffered(1))


def _sigmoid(x):
    return 1.0 / (1.0 + jnp.exp(-x))


def _dot(a, b):
    return jnp.dot(a.astype(BF16), b.astype(BF16), preferred_element_type=F32)


def _dot_nt(a, b):
    return lax.dot_general(a.astype(BF16), b.astype(BF16), (((1,), (1,)), ((), ())),
                           preferred_element_type=F32)


def _split2(a):
    hi = a.astype(BF16)
    lo = (a - hi.astype(F32)).astype(BF16)
    return hi, lo


def _dot_tab(tab_hi, tab_lo, x, *, tab_left):
    x_hi, x_lo = _split2(x)
    if tab_left:
        d = lambda t, v: jnp.dot(t, v, preferred_element_type=F32)
    else:
        d = lambda t, v: jnp.dot(v, t, preferred_element_type=F32)
    return d(tab_hi, x_hi) + (d(tab_hi, x_lo) + d(tab_lo, x_hi))


def _rms(x, gain):
    ms = jnp.mean(x * x, axis=-1, keepdims=True)
    return x * lax.rsqrt(ms + EPS) * gain


def _np_split2(t):
    t = np.asarray(t, np.float32)
    hi = jnp.asarray(t, F32).astype(BF16)
    lo = (jnp.asarray(t, F32) - hi.astype(F32)).astype(BF16)
    return hi, lo


def _mod_kernel(c_ref, w_ref, b_ref, o_ref):
    c = c_ref[...]
    o_ref[0] = _dot(c * _sigmoid(c), w_ref[0]) + b_ref[0]


def _ada_mod(c, ada_w, ada_b):
    depth, d, n6 = ada_w.shape
    bsz = c.shape[0]
    rows = 8
    cp = jnp.zeros((rows, d), F32).at[:bsz].set(c)
    tn = n6 // 4
    out = pl.pallas_call(
        _mod_kernel,
        out_shape=jax.ShapeDtypeStruct((depth, rows, n6), F32),
        grid=(depth, n6 // tn),
        in_specs=[pl.BlockSpec((rows, d), lambda l, j: (0, 0)),
                  pl.BlockSpec((1, d, tn), lambda l, j: (l, 0, j)),
                  pl.BlockSpec((1, 1, tn), lambda l, j: (l, 0, j))],
        out_specs=pl.BlockSpec((1, rows, tn), lambda l, j: (l, 0, j)),
        compiler_params=_cparams("parallel", "parallel"),
        name="ada_mod",
    )(cp, ada_w, ada_b.reshape(depth, 1, n6))
    return out[:, :bsz]


def _inproj_cm_kernel(x_ref, gain_ref, sc_ref, sh_ref, w_ref, *refs):
    cm_refs, rest_ref, zs_sc, zc_sc = refs[:-3], refs[-3], refs[-2], refs[-1]
    h = _rms(x_ref[0], gain_ref[...]) * (1.0 + sc_ref[0]) + sh_ref[0]
    z = _dot(h, w_ref[...])
    nblk, tm, _ = zs_sc.shape
    rest_ref[0] = z[:, nblk * LANES:]
    nq = A_HEADS * A_HEAD_DIM // LANES
    for c in range(nblk):
        blk = z[:, c * LANES:(c + 1) * LANES]
        zs_sc[c] = blk * (A_HEAD_DIM ** -0.5 * LOG2E) if c < nq else blk
    src, sd = zs_sc, 1
    for level, (cm_ref, (_, dil)) in enumerate(zip(cm_refs, DIL_CFG)):
        step, n = dil // sd, tm // dil
        keep = dil > 1 and level + 1 < len(DIL_CFG)
        for rs in range(sd):
            for cc in range(step):
                r = rs + sd * cc
                for c in range(nblk):
                    rows = src[c, pl.ds(rs * (tm // sd) + cc, n, stride=step), :]
                    cm_ref[0, r, :, c * LANES:(c + 1) * LANES] = rows.astype(BF16)
                    if keep:
                        zc_sc[c, r * n:(r + 1) * n, :] = rows
        if keep:
            src, sd = zc_sc, dil


def _inproj_cm(x, gain, sc, sh, w_bf16, na, tm):
    bsz, seq, d = x.shape
    n = w_bf16.shape[1]
    vec = pl.BlockSpec((1, 1, d), lambda b, i: (b, 0, 0))
    dils = [dl for _, dl in DIL_CFG]
    assert dils[0] == 1 and all(b % a == 0 for a, b in zip(dils, dils[1:]))
    return pl.pallas_call(
        _inproj_cm_kernel,
        out_shape=tuple([jax.ShapeDtypeStruct((bsz, dl, seq // dl, na), BF16) for dl in dils]
                        + [jax.ShapeDtypeStruct((bsz, seq, n - na), F32)]),
        grid=(bsz, seq // tm),
        in_specs=[pl.BlockSpec((1, tm, d), lambda b, i: (b, i, 0)),
                  _const_spec((1, d)), vec, vec, _const_spec((d, n))],
        out_specs=tuple([pl.BlockSpec((1, dl, tm // dl, na), lambda b, i: (b, 0, i, 0)) for dl in dils]
                        + [pl.BlockSpec((1, tm, n - na), lambda b, i: (b, i, 0))]),
        scratch_shapes=[pltpu.VMEM((na // LANES, tm, LANES), F32),
                        pltpu.VMEM((na // LANES, tm, LANES), F32)],
        compiler_params=_cparams("parallel", "parallel"),
        name="inproj_cm",
    )(x, gain.reshape(1, d), sc.reshape(bsz, 1, d), sh.reshape(bsz, 1, d), w_bf16)


def _t5_buckets(rel):
    half = N_BUCKETS // 2
    max_exact = half // 2
    n = np.abs(rel)
    large = max_exact + (np.log(np.maximum(n, 1) / max_exact) / np.log(T5_MAX_DIST / max_exact)
                         * (half - max_exact)).astype(np.int32)
    large = np.minimum(large, half - 1)
    return (np.where(rel > 0, half, 0) + np.where(n < max_exact, n, large)).astype(np.int32)


def _dil_bias(t5_bias, window, dil, tq):
    half = (window // 2) // dil
    assert half == tq // 2
    rel = np.arange(2 * tq)[None, :] - half - np.arange(tq)[:, None]
    inside = np.abs(rel) <= half
    buckets = _t5_buckets(np.where(inside, rel, 0) * dil)
    onehot =jnp.asarray(np.eye(N_BUCKETS, dtype=np.float32)[buckets])
    bias = jnp.einsum("qkn,nh->hqk", onehot, t5_bias.astype(F32), precision=lax.Precision.HIGHEST)
    return jnp.where(jnp.asarray(inside)[None], bias * LOG2E, NEG_INF)


def _dil_kernel(q_ref, kp_ref, kc_ref, kn_ref, vp_ref, vc_ref, vn_ref, bias_ref, o_ref, lse_ref,
                *, class_len):
    i = pl.program_id(2)
    sub, hq, tile = DIL_TQ, DIL_TQ // 2, q_ref.shape[2]
    kwin = jnp.concatenate([kp_ref[0, 0], kc_ref[0, 0], kn_ref[0, 0]], axis=0)
    vwin = jnp.concatenate([vp_ref[0, 0], vc_ref[0, 0], vn_ref[0, 0]], axis=0)
    lane = lax.broadcasted_iota(jnp.int32, (1, LANES), 1)
    lo = lane < A_HEAD_DIM
    nblk = A_HEADS // 2
    units = [(jt, j) for jt in range(tile // sub) for j in range(nblk)]
    logits = []
    for jt, j in units:
        cols = slice(j * LANES, (j + 1) * LANES)
        qj = q_ref[0, 0, jt * sub:(jt + 1) * sub, cols]
        zero = jnp.zeros_like(qj)
        q2 = jnp.concatenate([jnp.where(lo, qj, zero), jnp.where(lo, zero, qj)], axis=0)
        s = lax.dot_general(q2, kwin[jt * sub:jt * sub + 2 * sub, cols], (((1,), (1,)), ((), ())),
                            preferred_element_type=F32)
        kpos = i * tile + jt * sub - hq + lax.broadcasted_iota(jnp.int32, (1, 2 * sub), 1)
        valid = jnp.logical_and(kpos >= 0, kpos < class_len)
        logits.append(jnp.where(valid, s + bias_ref[j], NEG_INF))
    s_all = jnp.concatenate(logits, axis=0)
    m = jnp.max(s_all, axis=-1, keepdims=True)
    p32 = jnp.exp2(s_all - m)
    l = jnp.sum(p32, axis=-1, keepdims=True)
    p = p32.astype(BF16)
    rinv = 1.0 / l
    lse = m + jnp.log2(l)
    for jt in range(tile // sub):
        lse_all = jnp.zeros((sub, LANES), F32)
        for j in range(nblk):
            cols = slice(j * LANES, (j + 1) * LANES)
            r0 = (jt * nblk + j) * 2 * sub
            o2 = jnp.dot(p[r0:r0 + 2 * sub], vwin[jt * sub:jt * sub + 2 * sub, cols],
                         preferred_element_type=F32) * rinv[r0:r0 + 2 * sub]
            lse_all = jnp.where(lane == 2 * j, lse[r0:r0 + sub], lse_all)
            lse_all = jnp.where(lane == 2 * j + 1, lse[r0 + sub:r0 + 2 * sub], lse_all)
            o_ref[0, 0, jt * sub:(jt + 1) * sub, cols] = (
                jnp.where(lo, o2[:sub], o2[sub:]).astype(o_ref.dtype))
        lse_ref[0, 0, jt * sub:(jt + 1) * sub, :] = lse_all


def _dilated_branch(qkv_cm, t5_bias, window, dil):
    bsz, _, cl, width = qkv_cm.shape
    aw = A_HEADS * A_HEAD_DIM
    tile, hq = min(DIL_TILE, cl), DIL_TQ // 2
    nt = cl // tile
    per = tile // hq
    nh = cl // hq

    def cur(col):
        return pl.BlockSpec((1, 1, tile, aw), lambda b, r, i: (b, r, i, col))

    def prev(col):
        return pl.BlockSpec((1, 1, hq, aw), lambda b, r, i: (b, r, jnp.maximum(i * per - 1, 0), col))

    def nxt(col):
        return pl.BlockSpec((1, 1, hq, aw),
                            lambda b, r, i: (b, r, jnp.minimum((i + 1) * per, nh - 1), col))

    return pl.pallas_call(
        functools.partial(_dil_kernel, class_len=cl),
        out_shape=(jax.ShapeDtypeStruct((bsz, dil, cl, aw), BF16),
                   jax.ShapeDtypeStruct((bsz, dil, cl, LANES), F32)),
        grid=(bsz, dil, nt),
        in_specs=[cur(0), prev(1), cur(1), nxt(1), prev(2), cur(2), nxt(2),
                  _const_spec((A_HEADS // 2, 2 * DIL_TQ, 2 * DIL_TQ))],
        out_specs=(pl.BlockSpec((1, 1, tile, aw), lambda b, r, i: (b, r, i, 0)),
                   pl.BlockSpec((1, 1, tile, LANES), lambda b, r, i: (b, r, i, 0))),
        compiler_params=_cparams("parallel", "parallel", "parallel"),
        name=f"dilated_d{dil}",
    )(*([qkv_cm] * 7),
      _dil_bias(t5_bias, window, dil, DIL_TQ).reshape(A_HEADS // 2, 2 * DIL_TQ, 2 * DIL_TQ))


def _merge_branches(a_refs, l_refs, e_ref, a_scs, l_scs, sub):
    tm = l_scs[0].shape[0]
    accs, lses = [], []
    for (_, dil), a_ref, l_ref, a_sc, l_sc in zip(DIL_CFG, a_refs, l_refs, a_scs, l_scs):
        nblk = a_sc.shape[0]
        for r in range(dil):
            rows = pl.ds(r, tm // dil, stride=dil)
            src = pl.ds(sub * (tm // dil), tm // dil)
            for c in range(nblk):
                a_sc[c, rows, :] = a_ref[0, r, src, c * LANES:(c + 1) * LANES].astype(F32)
            l_sc[rows, :] = l_ref[0, r, src, :]
        accs.append(jnp.concatenate([a_sc[c] for c in range(nblk)], axis=1))
        lses.append(l_sc[...])
    mx = functools.reduce(jnp.maximum, lses)
    ws = [jnp.exp2(x - mx) for x in lses]
    tot = functools.reduce(jnp.add, ws)
    out = None
    for w, a in zip(ws, accs):
        w_hi, w_lo = _split2(w / tot)
        wide = (jnp.dot(w_hi, e_ref[...], preferred_element_type=F32)
                + jnp.dot(w_lo, e_ref[...], preferred_element_type=F32))
        out = wide * a if out is None else out + wide * a
    return out


def _merge_operands(branch_outs, tm):
    aw = branch_outs[0][0].shape[-1]
    expand = np.zeros((LANES, aw), np.float32)
    for h in range(A_HEADS):
        expand[h, h * A_HEAD_DIM:(h + 1) * A_HEAD_DIM] = 1.0
    dils = [d for _, d in DIL_CFG]
    specs = ([pl.BlockSpec((1, d, tm // d, aw), lambda b, i: (b, 0, i, 0)) for d in dils]
             + [pl.BlockSpec((1, d, tm // d, LANES), lambda b, i: (b, 0, i, 0)) for d in dils]
             + [_const_spec(expand.shape)])
    arrays = [o for o, _ in branch_outs] + [l for _, l in branch_outs] + [jnp.asarray(expand, BF16)]
    scratch = ([pltpu.VMEM((aw // LANES, TAIL_SUB, LANES), F32) for _ in dils]
               + [pltpu.VMEM((TAIL_SUB, LANES), F32) for _ in dils])
    return arrays, specs, scratch


def _hgrn_consts(reverse):
    t = HGRN_SUB
    r = np.arange(t)
    u = r[None, :]
    row = r[:, None]
    nmats = [(u >= row) if reverse else (u <= row)]
    masks = []
    m = t // 2
    while m >= 1:
        grp = r // (2 * m)
        in_first = (r % (2 * m)) < m
        same = grp[:, None] == grp[None, :]
        if reverse:
            beta = (grp * 2 * m + m)[:, None]
            n = np.where(in_first[:, None], (u >= row) & (u < beta), (u >= beta) & (u < row))
            mask = same & in_first[:, None] & ~in_first[None, :]
        else:
            beta = (grp * 2 * m + m - 1)[:, None]
            n = np.where(in_first[:, None], (u > row) & (u <= beta), (u > beta) & (u <= row))
            mask = same & ~in_first[:, None] & in_first[None, :]
        if m < HGRN_BCAST_MIN:
            nmats.append(n)
        masks.append(mask)
        m //= 2
    masks.append(np.eye(t, dtype=bool))
    nmat = jnp.asarray(np.concatenate(nmats, axis=0), F32).astype(BF16)
    return nmat, jnp.asarray(np.stack(masks), F32)


def _hgrn_block(q, f, v, lb, st, nmat_ref, mask_ref, reverse):
    t = HGRN_SUB
    nlev = mask_ref.shape[0] - 1
    qs = q * (B_DK ** -0.5)
    fa = lb + (1.0 - lb) * _sigmoid(f)
    kk = 1.0 - fa
    g_hi, g_lo = _split2(jnp.log2(fa))
    ex = jnp.dot(nmat_ref[...], jnp.concatenate([g_hi, g_lo], axis=1), preferred_element_type=F32)
    ex = ex[:, :B_DK] + ex[:, B_DK:]
    b = ex[:t]
    btot = b[0:1] if reverse else b[t - 1:t]
    a = mask_ref[nlev] * _dot_nt(qs, kk)
    fine = 1
    for l in range(nlev):
        m = t >> (l + 1)
        if m >= HGRN_BCAST_MIN:
            ref = jnp.concatenate(
                [jnp.broadcast_to(b[beta:beta + 1], (2 * m, B_DK))
                 for beta in range(m if reverse else m - 1, t, 2 * m)], axis=0)
            e = jnp.exp2(-jnp.abs(b - ref))
        else:
            e = jnp.exp2(ex[fine * t:(fine + 1) * t])
            fine += 1
        if l == 0:
            qr, kr = (slice(0, t // 2), slice(t // 2, t)) if reverse else (slice(t // 2, t), slice(0, t // 2))
            top = _dot(_dot_nt(qs[qr] * e[qr], kk[kr] * e[kr]), v[kr])
        else:
            a = a + mask_ref[l] * _dot_nt(qs * e, kk * e)
    zero = jnp.zeros_like(top)
    out = (_dot(a, v) + jnp.concatenate([top, zero] if reverse else [zero, top], axis=0)
           + _dot_nt(qs * jnp.exp2(b), st))
    khat = (kk * jnp.exp2(btot - b)).astype(BF16)
    st_new = st * jnp.exp2(btot) + jnp.dot(v.T.astype(BF16), khat, preferred_element_type=F32)
    return out, st_new


def _hgrn_lb(lg_ref, layer):
    lg = [lg_ref[l, 0] for l in range(lg_ref.shape[0])]
    mx = functools.reduce(jnp.maximum, lg)
    e = [jnp.exp(x - mx) for x in lg]
    return functools.reduce(jnp.add, e[:layer + 1]) / functools.reduce(jnp.add, e)


def _hgrn_kernel(qf_ref, ff_ref, vf_ref, qb_ref, fb_ref, vb_ref, lgf_ref, lgb_ref,
                 nf_ref, mf_ref, nb_ref, mb_ref, of_ref, ob_ref, sf_sc, sb_sc, *, layer):
    @pl.when(pl.program_id(2) == 0)
    def _():
        sf_sc[...] = jnp.zeros_like(sf_sc)
        sb_sc[...] = jnp.zeros_like(sb_sc)

    nsub = HGRN_BLOCK // HGRN_SUB
    chains = ((qf_ref, ff_ref, vf_ref, lgf_ref, sf_sc, nf_ref, mf_ref, of_ref, False),
              (qb_ref, fb_ref, vb_ref, lgb_ref, sb_sc, nb_ref, mb_ref, ob_ref, True))
    for q_ref, f_ref, v_ref, lg_ref, st_sc, n_ref, m_ref, o_ref, reverse in chains:
        lb = _hgrn_lb(lg_ref, layer)
        st = st_sc[...]
        for sub in (reversed(range(nsub)) if reverse else range(nsub)):
            rows = pl.ds(sub * HGRN_SUB, HGRN_SUB)
            o, st = _hgrn_block(q_ref[0, rows, :], f_ref[0, rows, :], v_ref[0, rows, :], lb, st,
                                n_ref, m_ref, reverse)
            o_ref[0, rows, :] = o.astype(o_ref.dtype)
        st_sc[...] = st


def _hgrn(z, lb_logits, layer, col0):
    bsz, seq, _ = z.shape
    t = HGRN_BLOCK
    nb = seq // t
    c0 = col0 // LANES
    hw = B_HEADS

    def zspec(group, rev):
        return pl.BlockSpec(
            (1, t, LANES),
            lambda b, h, j: (b, (nb - 1 - j) if rev else j, c0 + group * hw + h))

    def lgspec(direction):
        return pl.BlockSpec((lb_logits.shape[0], 1, 1, LANES),
                            lambda b, h, j: (0, direction * hw + h, 0, 0))

    nf, mf = _hgrn_consts(False)
    nbw, mbw = _hgrn_consts(True)
    lg = lb_logits.astype(F32).reshape(lb_logits.shape[0], 2 * hw, 1, LANES)
    o_shape = jax.ShapeDtypeStruct((bsz, seq, hw * B_DV), BF16)
    return pl.pallas_call(
        functools.partial(_hgrn_kernel, layer=layer),
        out_shape=(o_shape, o_shape),
        grid=(bsz, hw, nb),
        in_specs=[zspec(0, False), zspec(1, False), zspec(3, False),
                  zspec(0, True), zspec(2, True), zspec(3, True),
                  lgspec(0), lgspec(1),
                  _const_spec(nf.shape), _const_spec(mf.shape),
                  _const_spec(nbw.shape), _const_spec(mbw.shape)],
        out_specs=(pl.BlockSpec((1, t, LANES), lambda b, h, j: (b, j, h)),
                   pl.BlockSpec((1, t, LANES), lambda b, h, j: (b, nb - 1 - j, h))),
        scratch_shapes=[pltpu.VMEM((B_DV, B_DK), F32), pltpu.VMEM((B_DV, B_DK), F32)],
        compiler_params=_cparams("parallel", "parallel", "arbitrary"),
        name="hgrn",
    )(z, z, z, z, z, z, lg, lg, nf, mf, nbw, mbw)


def _mix0_y(sub, *refs):
    nbr = len(DIL_CFG)
    a_refs, l_refs, e_ref = refs[:nbr], refs[nbr:2 * nbr], refs[2 * nbr]
    of_ref, ob_ref, g_ref, on_ref, w_ref = refs[2 * nbr + 1:2 * nbr + 6]
    scratch = refs[2 * nbr + 6:]
    a = _merge_branches(a_refs, l_refs, e_ref, scratch[:nbr], scratch[nbr:], sub)
    rows = pl.ds(sub * TAIL_SUB, TAIL_SUB)
    o = of_ref[0, rows, :].astype(F32) + ob_ref[0, rows, :].astype(F32)
    g = g_ref[0, rows, :]
    parts = [_rms(o[:, h * B_DV:(h + 1) * B_DV], on_ref[...]) for h in range(B_HEADS)]
    bn = jnp.concatenate(parts, axis=-1) * (g * _sigmoid(g))
    na = a.shape[-1]
    return _dot(a, w_ref[:na, :]) + _dot(bn, w_ref[na:, :])


def _mix1_y(sub, c_ref, d_ref, w_ref):
    rows = pl.ds(sub * TAIL_SUB, TAIL_SUB)
    c = jnp.concatenate([c_ref[0, g, rows, :] for g in range(c_ref.shape[1])], axis=1)
    nc = c.shape[-1]
    return _dot(c, w_ref[:nc, :]) + _dot(d_ref[0, rows, :], w_ref[nc:, :])


def _row_spec(tm, width, col=0):
    return pl.BlockSpec((1, tm, width), lambda b, i: (b, i, col))


def _tail_kernel(*refs, nmix, mix_fn, bounds):
    (x_ref, gm_ref, gatem_ref, g1_ref, sc_ref, sh_ref, wi_ref, wo_ref, g2_ref, gatef_ref,
     o_ref) = refs[nmix:nmix + 11]
    hidden = wo_ref.shape[0]
    for sub in range(x_ref.shape[1] // TAIL_SUB):
        rows = pl.ds(sub * TAIL_SUB, TAIL_SUB)
        y_mix = mix_fn(sub, *refs[:nmix], *refs[nmix + 11:])
        x1 = x_ref[0, rows, :] + gatem_ref[0] * _rms(y_mix, gm_ref[...])
        h = (_rms(x1, g1_ref[...]) * (1.0 + sc_ref[0]) + sh_ref[0]).astype(BF16)
        y = None
        for c0, c1 in zip(bounds, bounds[1:]):
            gt = jnp.dot(h, wi_ref[:, c0:c1], preferred_element_type=F32)
            up = jnp.dot(h, wi_ref[:, hidden + c0:hidden + c1], preferred_element_type=F32)
            part = _dot(gt * _sigmoid(gt) * up, wo_ref[c0:c1, :])
            y = part if y is None else y + part
        o_ref[0, rows, :] = x1 + gatef_ref[0] * _rms(y, g2_ref[...])


def _layer_tail(mix_fn, mix_args, mix_specs, x, gain_m, gate_m, g1, sc, sh, wi_bf16, wo_bf16, layer,
                g2, gate_f, tm, name, mix_scratch=()):
    bsz, seq, d = x.shape
    vec = pl.BlockSpec((1, 1, d), lambda b, i: (b, 0, 0))
    hidden = wo_bf16.shape[1]

    def layer_slab(w):
        return pl.BlockSpec((None,) + w.shape[1:], lambda b, i: (layer, 0, 0),
                            pipeline_mode=pl.Buffered(1))

    ntile = hidden // MXU_WIDTH
    assert ntile * MXU_WIDTH == hidden
    bounds = (0, (ntile + 1) // 2 * MXU_WIDTH, hidden)
    row = lambda v: v.reshape(1, d)
    per_batch = lambda v: v.reshape(bsz, 1, d)
    return pl.pallas_call(
        functools.partial(_tail_kernel, nmix=len(mix_args), mix_fn=mix_fn, bounds=bounds),
        out_shape=jax.ShapeDtypeStruct(x.shape, F32),
        grid=(bsz, seq // tm),
        in_specs=list(mix_specs) + [_row_spec(tm, d), _const_spec((1, d)), vec, _const_spec((1, d)),
                                    vec, vec, layer_slab(wi_bf16), layer_slab(wo_bf16),
                                    _const_spec((1, d)), vec],
        out_specs=_row_spec(tm, d),
        scratch_shapes=list(mix_scratch),
        compiler_params=_cparams("parallel", "parallel"),
        name=name,
    )(*mix_args, x, row(gain_m), per_batch(gate_m), row(g1), per_batch(sc), per_batch(sh),
      wi_bf16, wo_bf16, row(g2), per_batch(gate_f))


def _tail0(branches, o_f, o_b, z, g_col, out_norm, w_bf16, tm, **kw):
    wv = B_HEADS * B_DV
    m_arrays, m_specs, m_scratch = _merge_operands(branches, tm)
    specs = m_specs + [_row_spec(tm, wv), _row_spec(tm, wv), _row_spec(tm, wv, g_col // wv),
                       _const_spec((1, B_DV)), _const_spec(w_bf16.shape)]
    args = m_arrays + [o_f, o_b, z, out_norm.reshape(1, B_DV), w_bf16]
    return _layer_tail(_mix0_y, args, specs, tm=tm, name="tail0", mix_scratch=m_scratch, **kw)


def _tail1(c_out, d_out, w_bf16, tm, **kw):
    specs = [pl.BlockSpec((1, c_out.shape[1], tm, c_out.shape[3]), lambda b, i: (b, 0, i, 0)),
             _row_spec(tm, d_out.shape[-1]), _const_spec(w_bf16.shape)]
    return _layer_tail(_mix1_y, (c_out, d_out, w_bf16), specs, tm=tm, name="tail1", **kw)


def _fft_kernel(u_ref, f1_ref, twc_ref, tws_ref, f2_ref, fw_ref, o_ref, u_sc, p_sc, y_sc,
                *, scale, n1, n2):
    pu = n2 + FFT_PAD
    pp = 2 * n1 + FFT_PAD
    py = n1 + FFT_PAD
    f1 = f1_ref[...].astype(BF16)
    f2 = f2_ref[...].astype(BF16)
    fw = fw_ref[...].astype(BF16)
    for i1 in range(n1):
        u_sc[i1 * pu:i1 * pu + n2, :] = u_ref[0, 0, i1 * n2:(i1 + 1) * n2, :]

    nb = FFT_BATCH

    def stage1(blk, carry):
        i2s = [blk * nb + j for j in range(nb)]
        x = jnp.concatenate([u_sc[pl.ds(i2, n1, stride=pu), :] for i2 in i2s], axis=1)
        p = jnp.dot(f1, x.astype(BF16), preferred_element_type=F32)
        for j, i2 in enumerate(i2s):
            p_sc[pl.ds(pl.multiple_of(i2 * pp, 8), 2 * n1), :] = p[:, j * C_WIDTH:(j + 1) * C_WIDTH]
        return carry

    lax.fori_loop(0, n2 // nb, stage1, 0, unroll=2)

    def stage2(blk, carry):
        k1s = [blk * nb + j for j in range(nb)]
        qr, qi = [], []
        for k1 in k1s:
            tc = twc_ref[k1]
            ts = tws_ref[k1]
            pr = p_sc[pl.ds(k1, n2, stride=pp), :]
            pim = p_sc[pl.ds(n1 + k1, n2, stride=pp), :]
            qr.append(pr * tc + pim * ts)
            qi.append(pim * tc - pr * ts)
        q = jnp.concatenate([jnp.concatenate(qr, axis=1), jnp.concatenate(qi, axis=1)], axis=0)
        xx = jnp.dot(f2, q.astype(BF16), preferred_element_type=F32)
        xg = jnp.concatenate(
            [jnp.concatenate([xx[:n2, j * C_WIDTH:(j + 1) * C_WIDTH],
                              xx[n2:, j * C_WIDTH:(j + 1) * C_WIDTH]], axis=1) for j in range(nb)],
            axis=0)
        y = jnp.dot(xg.astype(BF16), fw, preferred_element_type=F32) * scale
        for j, k1 in enumerate(k1s):
            y_sc[pl.ds(k1, n2, stride=py), :] = y[j * n2:(j + 1) * n2]
        return carry

    lax.fori_loop(0, n1 // nb, stage2, 0, unroll=2)
    for k2 in range(n2):
        o_ref[0, 0, k2 * n1:(k2 + 1) * n1, :] = y_sc[k2 * py:k2 * py + n1, :]


def _fourier_mixer(u):
    bsz, ngroups, seq, width = u.shape
    n2 = FFT_N2
    n1 = seq // n2
    assert n1 * n2 == seq and width == C_WIDTH and n1 % 8 == 0
    a1 = 2.0 * np.pi * np.outer(np.arange(n1), np.arange(n1)) / n1
    f1 = np.concatenate([np.cos(a1), -np.sin(a1)], axis=0)
    a2 = 2.0 * np.pi * np.outer(np.arange(n2), np.arange(n2)) / n2
    c2, s2 = np.cos(a2), np.sin(a2)
    f2 = np.block([[c2, s2], [-s2, c2]])
    aw = 2.0 * np.pi * np.outer(np.arange(C_WIDTH), np.arange(C_WIDTH)) / C_WIDTH
    fw = np.concatenate([np.cos(aw), np.sin(aw)], axis=0)
    at = np.repeat((2.0 * np.pi * np.outer(np.arange(n1), np.arange(n2)) / seq)[:, :, None],
                   C_WIDTH, axis=2)
    consts = (jnp.asarray(f1, F32), jnp.asarray(np.cos(at), F32), jnp.asarray(np.sin(at), F32),
              jnp.asarray(f2, F32), jnp.asarray(fw, F32))
    blk = pl.BlockSpec((1, 1, seq, C_WIDTH), lambda b, g: (b, g, 0, 0))
    return pl.pallas_call(
        functools.partial(_fft_kernel, scale=float(1.0 / np.sqrt(seq * C_WIDTH)), n1=n1, n2=n2),
        out_shape=jax.ShapeDtypeStruct(u.shape, F32),
        grid=(bsz, ngroups),
        in_specs=[blk] + [_const_spec(c.shape) for c in consts],
        out_specs=blk,
        scratch_shapes=[pltpu.VMEM((n1 * (n2 + FFT_PAD), C_WIDTH), F32),
                        pltpu.VMEM((n2 * (2 * n1 + FFT_PAD), C_WIDTH), F32),
                        pltpu.VMEM((n2 * (n1 + FFT_PAD), C_WIDTH), F32)],
        compiler_params=_cparams("parallel", "parallel"),
        name="fft",
    )(u, *consts)


def _head_perm():
    rep = D_Q_HEADS // D_KV_HEADS
    cols = []
    for j in range(rep):
        for g in range(D_KV_HEADS):
            h = g * rep + j
            cols.extend(range(h * D_HEAD_DIM, (h + 1) * D_HEAD_DIM))
    return np.asarray(cols, np.int32)


def _rope_tables(seq):
    rows = seq // GRID_W
    row = jnp.repeat(jnp.arange(rows, dtype=F32), GRID_W)
    col = jnp.tile(jnp.arange(GRID_W, dtype=F32), rows)
    axis_dim = D_HEAD_DIM // 2
    inv_freq = jnp.power(ROPE_THETA, -jnp.arange(0, axis_dim, 2, dtype=F32) / axis_dim)
    ang_r = row[:, None] * inv_freq[None, :]
    ang_c = col[:, None] * inv_freq[None, :]
    cr, sr, cc, sc = jnp.cos(ang_r), jnp.sin(ang_r), jnp.cos(ang_c), jnp.sin(ang_c)
    cos = jnp.concatenate([cr, cr, cc, cc], axis=1)
    sin = jnp.concatenate([-sr, sr, -sc, sc], axis=1)
    reps = LANES // D_HEAD_DIM
    return jnp.tile(cos, (1, reps)), jnp.tile(sin, (1, reps))


def _inproj_qk_kernel(x_ref, gain_ref, sc_ref, sh_ref, w_ref, cos_ref, sin_ref, bd_h_ref, bd_l_ref,
                      gq_ref, gk_ref, u_ref, qo_ref, ko_ref, vo_ref):
    h = _rms(x_ref[0], gain_ref[...]) * (1.0 + sc_ref[0]) + sh_ref[0]
    z = _dot(h, w_ref[...])
    ngrp = u_ref.shape[1]
    for g in range(ngrp):
        u_ref[0, g] = z[:, g * LANES:(g + 1) * LANES]
    cos = cos_ref[...]
    sin = sin_ref[...]
    quarter = D_HEAD_DIM // 4
    lane = lax.broadcasted_iota(jnp.int32, (1, LANES), 1)
    first_of_pair = (lane // quarter) % 2 == 0

    def norm_rope(x, gain, scale):
        ms = _dot_tab(bd_h_ref[...], bd_l_ref[...], x * x, tab_left=False)
        xn = x * lax.rsqrt(ms + EPS) * gain
        partner = jnp.where(first_of_pair, pltpu.roll(xn, LANES - quarter, 1),
                            pltpu.roll(xn, quarter, 1))
        return ((xn * cos + partner * sin) * scale).astype(BF16)

    nq = qo_ref.shape[-1] // LANES
    for j in range(nq):
        qo_ref[0, :, j * LANES:(j + 1) * LANES] = norm_rope(
            z[:, (ngrp + j) * LANES:(ngrp + j + 1) * LANES], gq_ref[...], D_HEAD_DIM ** -0.5 * LOG2E)
    ko_ref[0] = norm_rope(z[:, (ngrp + nq) * LANES:(ngrp + nq + 1) * LANES], gk_ref[...], 1.0)
    vt = z[:, (ngrp + nq + 1) * LANES:(ngrp + nq + 2) * LANES].T
    ones = jnp.ones((FLASH_ONES, vt.shape[1]), F32)
    vo_ref[0] = jnp.concatenate(
        [piece for g in range(D_KV_HEADS)
         for piece in (vt[g * D_HEAD_DIM:(g + 1) * D_HEAD_DIM], ones)], axis=0).astype(BF16)


def _inproj_qk(x, gain, sc, sh, w_bf16, qk_norm_j, tm):
    bsz, seq, d = x.shape
    n = w_bf16.shape[1]
    qw = D_Q_HEADS * D_HEAD_DIM
    kw = D_KV_HEADS * D_HEAD_DIM
    assert kw == LANES and n == C_GROUPS * C_WIDTH + qw + 2 * kw
    vrows = D_KV_HEADS * (D_HEAD_DIM + FLASH_ONES)
    cos, sin = _rope_tables(seq)
    bd = np.kron(np.eye(LANES // D_HEAD_DIM), np.full((D_HEAD_DIM, D_HEAD_DIM), 1.0 / D_HEAD_DIM))
    bd_h, bd_l = _np_split2(bd)
    reps = LANES // D_HEAD_DIM
    gq = jnp.tile(qk_norm_j[0].astype(F32), reps).reshape(1, LANES)
    gk = jnp.tile(qk_norm_j[1].astype(F32), reps).reshape(1, LANES)
    tab = pl.BlockSpec((tm, LANES), lambda b, i: (i, 0))
    vec = pl.BlockSpec((1, 1, d), lambda b, i: (b, 0, 0))
    return pl.pallas_call(
        _inproj_qk_kernel,
        out_shape=(jax.ShapeDtypeStruct((bsz, C_GROUPS, seq, C_WIDTH), F32),
                   jax.ShapeDtypeStruct((bsz, seq, qw), BF16),
                   jax.ShapeDtypeStruct((bsz, seq, kw), BF16),
                   jax.ShapeDtypeStruct((bsz, vrows, seq), BF16)),
        grid=(bsz, seq // tm),
        in_specs=[pl.BlockSpec((1, tm, d), lambda b, i: (b, i, 0)),
                  _const_spec((1, d)), vec, vec, _const_spec((d, n)),
                  tab, tab, _const_spec(bd_h.shape), _const_spec(bd_l.shape),
                  _const_spec((1, LANES)), _const_spec((1, LANES))],
        out_specs=(pl.BlockSpec((1, C_GROUPS, tm, C_WIDTH), lambda b, i: (b, 0, i, 0)),
                   _row_spec(tm, qw), _row_spec(tm, kw),
                   pl.BlockSpec((1, vrows, tm), lambda b, i: (b, 0, i))),
        compiler_params=_cparams("parallel", "parallel"),
        name="inproj_qk",
    )(x, gain.reshape(1, d), sc.reshape(bsz, 1, d), sh.reshape(bsz, 1, d), w_bf16,
      cos, sin, bd_h, bd_l, gq, gk)


def _flash_kernel(q_ref, k_ref, vt_ref, o_ref, m_sc, acc_sc, s_sc):
    kv = pl.program_id(2)

    @pl.when(kv == 0)
    def _():
        m_sc[...] = jnp.full_like(m_sc, -jnp.inf)
        acc_sc[...] = jnp.zeros_like(acc_sc)

    lane = lax.broadcasted_iota(jnp.int32, (1, LANES), 1)
    lo = lane < D_HEAD_DIM
    nblk = q_ref.shape[-1] // LANES
    tq, tk = q_ref.shape[1], k_ref.shape[1]
    ku, qu = FLASH_KEY_UNIT, min(FLASH_QUERY_UNIT, tq)
    grows = D_HEAD_DIM + FLASH_ONES
    nheads = nblk * D_KV_HEADS
    k = k_ref[0]

    nchunk = tq // qu

    def logits_chunk(idx, c):
        j, g = divmod(idx, D_KV_HEADS)
        qj = q_ref[0, c * qu:(c + 1) * qu, j * LANES:(j + 1) * LANES]
        sel = lo if g == 0 else jnp.logical_not(lo)
        s = lax.dot_general(k, jnp.where(sel, qj, jnp.zeros_like(qj)), (((1,), (1,)), ((), ())),
                            preferred_element_type=F32)
        s_sc[idx % s_sc.shape[0], :, c * qu:(c + 1) * qu] = s
        return jnp.max(s, axis=0, keepdims=True)

    def finish_logits(idx, mcs):
        m_prev = m_sc[idx, 0:1, :]
        m_new = jnp.maximum(m_prev, jnp.concatenate(mcs, axis=1))
        m_sc[idx, 0:1, :] = m_new
        return m_new, jnp.exp2(m_prev - m_new)

    def value_chunk(idx, c, m_new, alpha):
        j, g = divmod(idx, D_KV_HEADS)
        rows = slice(g * grows, (g + 1) * grows)
        qcols = slice(c * qu, (c + 1) * qu)
        pv = None
        for u in range(tk // ku):
            keys = slice(u * ku, (u + 1) * ku)
            p = jnp.exp2(s_sc[idx % s_sc.shape[0], keys, qcols] - m_new[:, qcols])
            d = jnp.dot(vt_ref[0, rows, keys], p.astype(BF16), preferred_element_type=F32)
            pv = d if pv is None else pv + d
        acc_sc[j, rows, qcols] = alpha[:, qcols] * acc_sc[j, rows, qcols] + pv

    def logits_pass(idx):
        return finish_logits(idx, [logits_chunk(idx, c) for c in range(nchunk)])

    pending = [logits_pass(i) for i in range(min(FLASH_AHEAD, nheads))]
    for idx in range(nheads):
        if idx + FLASH_AHEAD < nheads:
            pending.append(logits_pass(idx + FLASH_AHEAD))
        stats = pending.pop(0)
        for c in range(nchunk):
            value_chunk(idx, c, *stats)

    @pl.when(kv == pl.num_programs(2) - 1)
    def _():
        for j in range(nblk):
            parts = []
            for g in range(D_KV_HEADS):
                num = acc_sc[j, g * grows:g * grows + D_HEAD_DIM, :]
                den = acc_sc[j, g * grows + D_HEAD_DIM:g * grows + D_HEAD_DIM + 1, :]
                parts.append(num / den)
            o_ref[0, :, j * LANES:(j + 1) * LANES] = (
                jnp.concatenate(parts, axis=0).T.astype(o_ref.dtype))


def _flash(q, k, vt, tq, tk):
    bsz, seq, qw = q.shape
    kw = k.shape[-1]
    vrows = vt.shape[1]
    assert FLASH_LOGIT_BUFS > FLASH_AHEAD
    return pl.pallas_call(
        _flash_kernel,
        out_shape=jax.ShapeDtypeStruct((bsz, seq, qw), BF16),
        grid=(bsz, seq // tq, seq // tk),
        in_specs=[pl.BlockSpec((1, tq, qw), lambda b, i, j: (b, i, 0)),
                  pl.BlockSpec((1, tk, kw), lambda b, i, j: (b, j, 0)),
                  pl.BlockSpec((1, vrows, tk), lambda b, i, j: (b, 0, j))],
        out_specs=pl.BlockSpec((1, tq, qw), lambda b, i, j: (b, i, 0)),
        scratch_shapes=[pltpu.VMEM((D_Q_HEADS, 8, tq), F32),
                        pltpu.VMEM((qw // LANES, vrows, tq), F32),
                        pltpu.VMEM((FLASH_LOGIT_BUFS, tk, tq), F32)],
        compiler_params=_cparams("parallel", "parallel", "arbitrary"),
        name="flash",
    )(q, k, vt)


def kernel(x, c, t5_bias, hgrn_lb_logits, ada_w, ada_b, norm_gains, ab_w_in, ab_w_out,
           hgrn_out_norm, cd_w_in, cd_w_out, qk_norm, ffn_w_in, ffn_w_out):
    bsz, seq, d = x.shape
    depth = ada_w.shape[0]
    mod = _ada_mod(c.astype(F32), ada_w, ada_b)
    perm = _head_perm()
    aw = A_HEADS * A_HEAD_DIM
    cw = C_GROUPS * C_WIDTH
    qw = D_Q_HEADS * D_HEAD_DIM
    ffn_wi = ffn_w_in.astype(BF16)
    ffn_wo = ffn_w_out.astype(BF16)
    for layer in range(depth):
        sh_m, sc_m, g_m, sh_f, sc_f, g_f = [mod[layer, :, i * d:(i + 1) * d] for i in range(6)]
        gains = norm_gains[layer]
        j = layer // 2
        tail = dict(x=x, gain_m=gains[1], gate_m=g_m, g1=gains[2], sc=sc_f, sh=sh_f,
                    wi_bf16=ffn_wi, wo_bf16=ffn_wo, layer=layer, g2=gains[3], gate_f=g_f)
        if layer % 2 == 0:
            w_in = ab_w_in[j].astype(BF16)
            *qkv_cm, z = _inproj_cm(x, gains[0], sc_m, sh_m, w_in, 3 * aw, min(INPROJ_CM_ROWS, seq))
            branches = [_dilated_branch(cm, t5_bias, window, dil)
                        for cm, (window, dil) in zip(qkv_cm, DIL_CFG)]
            o_f, o_b = _hgrn(z, hgrn_lb_logits, layer, 0)
            g_col = 3 * B_HEADS * B_DK + B_HEADS * B_DV
            x = _tail0(branches, o_f, o_b, z, g_col, hgrn_out_norm[j], ab_w_out[j].astype(BF16),
                       tm=min(TAIL_SUB, seq), **tail)
        else:
            w_full = cd_w_in[j]
            w_in = jnp.concatenate([w_full[:, :cw], w_full[:, cw:cw + qw][:, perm],
                                    w_full[:, cw + qw:]], axis=1).astype(BF16)
            u, qn, kn, vn = _inproj_qk(x, gains[0], sc_m, sh_m, w_in, qk_norm[j],
                                       min(INPROJ_QK_ROWS, seq))
            c_out = _fourier_mixer(u)
            d_out = _flash(qn, kn, vn, min(FLASH_TQ, seq), min(FLASH_TK, seq))
            w_out_full = cd_w_out[j]
            w_out = jnp.concatenate([w_out_full[:cw], w_out_full[cw:][perm]], axis=0).astype(BF16)
            x = _tail1(c_out, d_out, w_out, tm=min(TAIL_BLOCK, seq), **tail)
    return x
```

```python
import functools

import numpy as np
import jax
import jax.numpy as jnp
from jax import lax
from jax.experimental import pallas as pl
from jax.experimental.pallas import tpu as pltpu

F32 = jnp.float32
BF16 = jnp.bfloat16
LANES = 128
MXU_WIDTH = 256
VMEM_LIMIT_BYTES = 56 * 2**20
NEG_INF = -1e30
EPS = 1e-6

GRID_W = 64
A_HEADS = 8
A_HEAD_DIM = 64
DIL_CFG = ((128, 1), (512, 4), (2048, 16))
N_BUCKETS = 32
T5_MAX_DIST = 1024
B_HEADS = 4
B_DK = 128
B_DV = 128
C_GROUPS = 4
C_WIDTH = 128
D_Q_HEADS = 8
D_KV_HEADS = 2
D_HEAD_DIM = 64
ROPE_THETA = 10000.0

DIL_TQ = 128
DIL_TILE = 512
HGRN_BLOCK = 1024
HGRN_SUB = 256
HGRN_BCAST_MIN = 8
FLASH_TQ = 1024
FLASH_TK = 4096
FLASH_LOGIT_BUFS = 2
FLASH_KEY_UNIT = 256
FLASH_QUERY_UNIT = 1024
FLASH_AHEAD = 1
FLASH_ONES = 16
TAIL_BLOCK = 512
TAIL_SUB = 512
FFT_N2 = 128
FFT_BATCH = 8
FFT_PAD = 8
LOG2E = 1.4426950408889634


def _cparams(*sem):
    return pltpu.CompilerParams(dimension_semantics=sem, vmem_limit_bytes=VMEM_LIMIT_BYTES)


def _const_spec(shape):
    nd = len(shape)
    return pl.BlockSpec(shape, lambda *_: (0,) * nd, pipeline_mode=pl.Buffered(1))


def _sigmoid(x):
    return 1.0 / (1.0 + jnp.exp(-x))


def _dot(a, b):
    return jnp.dot(a.astype(BF16), b.astype(BF16), preferred_element_type=F32)


def _dot_nt(a, b):
    return lax.dot_general(a.astype(BF16), b.astype(BF16), (((1,), (1,)), ((), ())),
                           preferred_element_type=F32)


def _split2(a):
    hi = a.astype(BF16)
    lo = (a - hi.astype(F32)).astype(BF16)
    return hi, lo


def _dot_tab(tab_hi, tab_lo, x, *, tab_left):
    x_hi, x_lo = _split2(x)
    if tab_left:
        d = lambda t, v: jnp.dot(t, v, preferred_element_type=F32)
    else:
        d = lambda t, v: jnp.dot(v, t, preferred_element_type=F32)
    return d(tab_hi, x_hi) + (d(tab_hi, x_lo) + d(tab_lo, x_hi))


def _rms(x, gain):
    ms = jnp.mean(x * x, axis=-1, keepdims=True)
    return x * lax.rsqrt(ms + EPS) * gain


def _np_split2(t):
    t = np.asarray(t, np.float32)
    hi = jnp.asarray(t, F32).astype(BF16)
    lo = (jnp.asarray(t, F32) - hi.astype(F32)).astype(BF16)
    return hi, lo


def _mod_kernel(c_ref, w_ref, b_ref, o_ref):
    c = c_ref[...]
    o_ref[0] = _dot(c * _sigmoid(c), w_ref[0]) + b_ref[0]


def _ada_mod(c, ada_w, ada_b):
    depth, d, n6 = ada_w.shape
    bsz = c.shape[0]
    rows = 8
    cp = jnp.zeros((rows, d), F32).at[:bsz].set(c)
    tn = n6 // 4
    out = pl.pallas_call(
        _mod_kernel,
        out_shape=jax.ShapeDtypeStruct((depth, rows, n6), F32),
        grid=(depth, n6 // tn),
        in_specs=[pl.BlockSpec((rows, d), lambda l, j: (0, 0)),
                  pl.BlockSpec((1, d, tn), lambda l, j: (l, 0, j)),
                  pl.BlockSpec((1, 1, tn), lambda l, j: (l, 0, j))],
        out_specs=pl.BlockSpec((1, rows, tn), lambda l, j: (l, 0, j)),
        compiler_params=_cparams("parallel", "parallel"),
        name="ada_mod",
    )(cp, ada_w, ada_b.reshape(depth, 1, n6))
    return out[:, :bsz]


def _inproj_cm_kernel(x_ref, gain_ref, sc_ref, sh_ref, w_ref, *refs):
    cm_refs, rest_ref, zs_sc, zc_sc = refs[:-3], refs[-3], refs[-2], refs[-1]
    h = _rms(x_ref[0], gain_ref[...]) * (1.0 + sc_ref[0]) + sh_ref[0]
    z = _dot(h, w_ref[...])
    nblk, tm, _ = zs_sc.shape
    rest_ref[0] = z[:, nblk * LANES:]
    nq = A_HEADS * A_HEAD_DIM // LANES
    for c in range(nblk):
        blk = z[:, c * LANES:(c + 1) * LANES]
        zs_sc[c] = blk * (A_HEAD_DIM ** -0.5 * LOG2E) if c < nq else blk
    src, sd = zs_sc, 1
    for level, (cm_ref, (_, dil)) in enumerate(zip(cm_refs, DIL_CFG)):
        step, n = dil // sd, tm // dil
        keep = dil > 1 and level + 1 < len(DIL_CFG)
        for rs in range(sd):
            for cc in range(step):
                r = rs + sd * cc
                for c in range(nblk):
                    rows = src[c, pl.ds(rs * (tm // sd) + cc, n, stride=step), :]
                    cm_ref[0, r, :, c * LANES:(c + 1) * LANES] = rows.astype(BF16)
                    if keep:
                        zc_sc[c, r * n:(r + 1) * n, :] = rows
        if keep:
            src, sd = zc_sc, dil


def _inproj_cm(x, gain, sc, sh, w_bf16, na, tm):
    bsz, seq, d = x.shape
    n = w_bf16.shape[1]
    vec = pl.BlockSpec((1, 1, d), lambda b, i: (b, 0, 0))
    dils = [dl for _, dl in DIL_CFG]
    assert dils[0] == 1 and all(b % a == 0 for a, b in zip(dils, dils[1:]))
    return pl.pallas_call(
        _inproj_cm_kernel,
        out_shape=tuple([jax.ShapeDtypeStruct((bsz, dl, seq // dl, na), BF16) for dl in dils]
                        + [jax.ShapeDtypeStruct((bsz, seq, n - na), F32)]),
        grid=(bsz, seq // tm),
        in_specs=[pl.BlockSpec((1, tm, d), lambda b, i: (b, i, 0)),
                  _const_spec((1, d)), vec, vec, _const_spec((d, n))],
        out_specs=tuple([pl.BlockSpec((1, dl, tm // dl, na), lambda b, i: (b, 0, i, 0)) for dl in dils]
                        + [pl.BlockSpec((1, tm, n - na), lambda b, i: (b, i, 0))]),
        scratch_shapes=[pltpu.VMEM((na // LANES, tm, LANES), F32),
                        pltpu.VMEM((na // LANES, tm, LANES), F32)],
        compiler_params=_cparams("parallel", "parallel"),
        name="inproj_cm",
    )(x, gain.reshape(1, d), sc.reshape(bsz, 1, d), sh.reshape(bsz, 1, d), w_bf16)


def _t5_buckets(rel):
    half = N_BUCKETS // 2
    max_exact = half // 2
    n = np.abs(rel)
    large = max_exact + (np.log(np.maximum(n, 1) / max_exact) / np.log(T5_MAX_DIST / max_exact)
                         * (half - max_exact)).astype(np.int32)
    large = np.minimum(large, half - 1)
    return (np.where(rel > 0, half, 0) + np.where(n < max_exact, n, large)).astype(np.int32)


def _dil_bias(t5_bias, window, dil, tq):
    half = (window // 2) // dil
    assert half == tq // 2
    rel = np.arange(2 * tq)[None, :] - half - np.arange(tq)[:, None]
    inside = np.abs(rel) <= half
    buckets = _t5_buckets(np.where(inside, rel, 0) * dil)
    onehot =jnp.asarray(np.eye(N_BUCKETS, dtype=np.float32)[buckets])
    bias = jnp.einsum("qkn,nh->hqk", onehot, t5_bias.astype(F32), precision=lax.Precision.HIGHEST)
    return jnp.where(jnp.asarray(inside)[None], bias * LOG2E, NEG_INF)


def _dil_kernel(q_ref, kp_ref, kc_ref, kn_ref, vp_ref, vc_ref, vn_ref, bias_ref, o_ref, lse_ref,
                *, class_len):
    i = pl.program_id(2)
    sub, hq, tile = DIL_TQ, DIL_TQ // 2, q_ref.shape[2]
    kwin = jnp.concatenate([kp_ref[0, 0], kc_ref[0, 0], kn_ref[0, 0]], axis=0)
    vwin = jnp.concatenate([vp_ref[0, 0], vc_ref[0, 0], vn_ref[0, 0]], axis=0)
    lane = lax.broadcasted_iota(jnp.int32, (1, LANES), 1)
    lo = lane < A_HEAD_DIM
    nblk = A_HEADS // 2
    units = [(jt, j) for jt in range(tile // sub) for j in range(nblk)]
    logits = []
    for jt, j in units:
        cols = slice(j * LANES, (j + 1) * LANES)
        qj = q_ref[0, 0, jt * sub:(jt + 1) * sub, cols]
        zero = jnp.zeros_like(qj)
        q2 = jnp.concatenate([jnp.where(lo, qj, zero), jnp.where(lo, zero, qj)], axis=0)
        s = lax.dot_general(q2, kwin[jt * sub:jt * sub + 2 * sub, cols], (((1,), (1,)), ((), ())),
                            preferred_element_type=F32)
        kpos = i * tile + jt * sub - hq + lax.broadcasted_iota(jnp.int32, (1, 2 * sub), 1)
        valid = jnp.logical_and(kpos >= 0, kpos < class_len)
        logits.append(jnp.where(valid, s + bias_ref[j], NEG_INF))
    s_all = jnp.concatenate(logits, axis=0)
    m = jnp.max(s_all, axis=-1, keepdims=True)
    p32 = jnp.exp2(s_all - m)
    l = jnp.sum(p32, axis=-1, keepdims=True)
    p = p32.astype(BF16)
    rinv = 1.0 / l
    lse = m + jnp.log2(l)
    for jt in range(tile // sub):
        lse_all = jnp.zeros((sub, LANES), F32)
        for j in range(nblk):
            cols = slice(j * LANES, (j + 1) * LANES)
            r0 = (jt * nblk + j) * 2 * sub
            o2 = jnp.dot(p[r0:r0 + 2 * sub], vwin[jt * sub:jt * sub + 2 * sub, cols],
                         preferred_element_type=F32) * rinv[r0:r0 + 2 * sub]
            lse_all = jnp.where(lane == 2 * j, lse[r0:r0 + sub], lse_all)
            lse_all = jnp.where(lane == 2 * j + 1, lse[r0 + sub:r0 + 2 * sub], lse_all)
            o_ref[0, 0, jt * sub:(jt + 1) * sub, cols] = (
                jnp.where(lo, o2[:sub], o2[sub:]).astype(o_ref.dtype))
        lse_ref[0, 0, jt * sub:(jt + 1) * sub, :] = lse_all


def _dilated_branch(qkv_cm, t5_bias, window, dil):
    bsz, _, cl, width = qkv_cm.shape
    aw = A_HEADS * A_HEAD_DIM
    tile, hq = min(DIL_TILE, cl), DIL_TQ // 2
    nt = cl // tile
    per = tile // hq
    nh = cl // hq

    def cur(col):
        return pl.BlockSpec((1, 1, tile, aw), lambda b, r, i: (b, r, i, col))

    def prev(col):
        return pl.BlockSpec((1, 1, hq, aw), lambda b, r, i: (b, r, jnp.maximum(i * per - 1, 0), col))

    def nxt(col):
        return pl.BlockSpec((1, 1, hq, aw),
                            lambda b, r, i: (b, r, jnp.minimum((i + 1) * per, nh - 1), col))

    return pl.pallas_call(
        functools.partial(_dil_kernel, class_len=cl),
        out_shape=(jax.ShapeDtypeStruct((bsz, dil, cl, aw), BF16),
                   jax.ShapeDtypeStruct((bsz, dil, cl, LANES), F32)),
        grid=(bsz, dil, nt),
        in_specs=[cur(0), prev(1), cur(1), nxt(1), prev(2), cur(2), nxt(2),
                  _const_spec((A_HEADS // 2, 2 * DIL_TQ, 2 * DIL_TQ))],
        out_specs=(pl.BlockSpec((1, 1, tile, aw), lambda b, r, i: (b, r, i, 0)),
                   pl.BlockSpec((1, 1, tile, LANES), lambda b, r, i: (b, r, i, 0))),
        compiler_params=_cparams("parallel", "parallel", "parallel"),
        name=f"dilated_d{dil}",
    )(*([qkv_cm] * 7),
      _dil_bias(t5_bias, window, dil, DIL_TQ).reshape(A_HEADS // 2, 2 * DIL_TQ, 2 * DIL_TQ))


def _merge_branches(a_refs, l_refs, e_ref, a_scs, l_scs, sub):
    tm = l_scs[0].shape[0]
    accs, lses = [], []
    for (_, dil), a_ref, l_ref, a_sc, l_sc in zip(DIL_CFG, a_refs, l_refs, a_scs, l_scs):
        nblk = a_sc.shape[0]
        for r in range(dil):
            rows = pl.ds(r, tm // dil, stride=dil)
            src = pl.ds(sub * (tm // dil), tm // dil)
            for c in range(nblk):
                a_sc[c, rows, :] = a_ref[0, r, src, c * LANES:(c + 1) * LANES].astype(F32)
            l_sc[rows, :] = l_ref[0, r, src, :]
        accs.append(jnp.concatenate([a_sc[c] for c in range(nblk)], axis=1))
        lses.append(l_sc[...])
    mx = functools.reduce(jnp.maximum, lses)
    ws = [jnp.exp2(x - mx) for x in lses]
    tot = functools.reduce(jnp.add, ws)
    out = None
    for w, a in zip(ws, accs):
        w_hi, w_lo = _split2(w / tot)
        wide = (jnp.dot(w_hi, e_ref[...], preferred_element_type=F32)
                + jnp.dot(w_lo, e_ref[...], preferred_element_type=F32))
        out = wide * a if out is None else out + wide * a
    return out


def _merge_operands(branch_outs, tm):
    aw = branch_outs[0][0].shape[-1]
    expand = np.zeros((LANES, aw), np.float32)
    for h in range(A_HEADS):
        expand[h, h * A_HEAD_DIM:(h + 1) * A_HEAD_DIM] = 1.0
    dils = [d for _, d in DIL_CFG]
    specs = ([pl.BlockSpec((1, d, tm // d, aw), lambda b, i: (b, 0, i, 0)) for d in dils]
             + [pl.BlockSpec((1, d, tm // d, LANES), lambda b, i: (b, 0, i, 0)) for d in dils]
             + [_const_spec(expand.shape)])
    arrays = [o for o, _ in branch_outs] + [l for _, l in branch_outs] + [jnp.asarray(expand, BF16)]
    scratch = ([pltpu.VMEM((aw // LANES, TAIL_SUB, LANES), F32) for _ in dils]
               + [pltpu.VMEM((TAIL_SUB, LANES), F32) for _ in dils])
    return arrays, specs, scratch


def _hgrn_consts(reverse):
    t = HGRN_SUB
    r = np.arange(t)
    u = r[None, :]
    row = r[:, None]
    nmats = [(u >= row) if reverse else (u <= row)]
    masks = []
    m = t // 2
    while m >= 1:
        grp = r // (2 * m)
        in_first = (r % (2 * m)) < m
        same = grp[:, None] == grp[None, :]
        if reverse:
            beta = (grp * 2 * m + m)[:, None]
            n = np.where(in_first[:, None], (u >= row) & (u < beta), (u >= beta) & (u < row))
            mask = same & in_first[:, None] & ~in_first[None, :]
        else:
            beta = (grp * 2 * m + m - 1)[:, None]
            n = np.where(in_first[:, None], (u > row) & (u <= beta), (u > beta) & (u <= row))
            mask = same & ~in_first[:, None] & in_first[None, :]
        if m < HGRN_BCAST_MIN:
            nmats.append(n)
        masks.append(mask)
        m //= 2
    masks.append(np.eye(t, dtype=bool))
    nmat = jnp.asarray(np.concatenate(nmats, axis=0), F32).astype(BF16)
    return nmat, jnp.asarray(np.stack(masks), F32)


def _hgrn_block(q, f, v, lb, st, nmat_ref, mask_ref, reverse):
    t = HGRN_SUB
    nlev = mask_ref.shape[0] - 1
    qs = q * (B_DK ** -0.5)
    fa = lb + (1.0 - lb) * _sigmoid(f)
    kk = 1.0 - fa
    g_hi, g_lo = _split2(jnp.log2(fa))
    ex = jnp.dot(nmat_ref[...], jnp.concatenate([g_hi, g_lo], axis=1), preferred_element_type=F32)
    ex = ex[:, :B_DK] + ex[:, B_DK:]
    b = ex[:t]
    btot = b[0:1] if reverse else b[t - 1:t]
    a = mask_ref[nlev] * _dot_nt(qs, kk)
    fine = 1
    for l in range(nlev):
        m = t >> (l + 1)
        if m >= HGRN_BCAST_MIN:
            ref = jnp.concatenate(
                [jnp.broadcast_to(b[beta:beta + 1], (2 * m, B_DK))
                 for beta in range(m if reverse else m - 1, t, 2 * m)], axis=0)
            e = jnp.exp2(-jnp.abs(b - ref))
        else:
            e = jnp.exp2(ex[fine * t:(fine + 1) * t])
            fine += 1
        a = a + mask_ref[l] * _dot_nt(qs * e, kk * e)
    out = _dot(a, v) + _dot_nt(qs * jnp.exp2(b), st)
    khat = (kk * jnp.exp2(btot - b)).astype(BF16)
    st_new = st * jnp.exp2(btot) + jnp.dot(v.T.astype(BF16), khat, preferred_element_type=F32)
    return out, st_new


def _hgrn_lb(lg_ref, layer):
    lg = [lg_ref[l, 0] for l in range(lg_ref.shape[0])]
    mx = functools.reduce(jnp.maximum, lg)
    e = [jnp.exp(x - mx) for x in lg]
    return functools.reduce(jnp.add, e[:layer + 1]) / functools.reduce(jnp.add, e)


def _hgrn_kernel(qf_ref, ff_ref, vf_ref, qb_ref, fb_ref, vb_ref, lgf_ref, lgb_ref,
                 nf_ref, mf_ref, nb_ref, mb_ref, of_ref, ob_ref, sf_sc, sb_sc, *, layer):
    @pl.when(pl.program_id(2) == 0)
    def _():
        sf_sc[...] = jnp.zeros_like(sf_sc)
        sb_sc[...] = jnp.zeros_like(sb_sc)

    nsub = HGRN_BLOCK // HGRN_SUB
    chains = ((qf_ref, ff_ref, vf_ref, lgf_ref, sf_sc, nf_ref, mf_ref, of_ref, False),
              (qb_ref, fb_ref, vb_ref, lgb_ref, sb_sc, nb_ref, mb_ref, ob_ref, True))
    for q_ref, f_ref, v_ref, lg_ref, st_sc, n_ref, m_ref, o_ref, reverse in chains:
        lb = _hgrn_lb(lg_ref, layer)
        st = st_sc[...]
        for sub in (reversed(range(nsub)) if reverse else range(nsub)):
            rows = pl.ds(sub * HGRN_SUB, HGRN_SUB)
            o, st = _hgrn_block(q_ref[0, rows, :], f_ref[0, rows, :], v_ref[0, rows, :], lb, st,
                                n_ref, m_ref, reverse)
            o_ref[0, rows, :] = o.astype(o_ref.dtype)
        st_sc[...] = st


def _hgrn(z, lb_logits, layer, col0):
    bsz, seq, _ = z.shape
    t = HGRN_BLOCK
    nb = seq // t
    c0 = col0 // LANES
    hw = B_HEADS

    def zspec(group, rev):
        return pl.BlockSpec(
            (1, t, LANES),
            lambda b, h, j: (b, (nb - 1 - j) if rev else j, c0 + group * hw + h))

    def lgspec(direction):
        return pl.BlockSpec((lb_logits.shape[0], 1, 1, LANES),
                            lambda b, h, j: (0, direction * hw + h, 0, 0))

    nf, mf = _hgrn_consts(False)
    nbw, mbw = _hgrn_consts(True)
    lg = lb_logits.astype(F32).reshape(lb_logits.shape[0], 2 * hw, 1, LANES)
    o_shape = jax.ShapeDtypeStruct((bsz, seq, hw * B_DV), BF16)
    return pl.pallas_call(
        functools.partial(_hgrn_kernel, layer=layer),
        out_shape=(o_shape, o_shape),
        grid=(bsz, hw, nb),
        in_specs=[zspec(0, False), zspec(1, False), zspec(3, False),
                  zspec(0, True), zspec(2, True), zspec(3, True),
                  lgspec(0), lgspec(1),
                  _const_spec(nf.shape), _const_spec(mf.shape),
                  _const_spec(nbw.shape), _const_spec(mbw.shape)],
        out_specs=(pl.BlockSpec((1, t, LANES), lambda b, h, j: (b, j, h)),
                   pl.BlockSpec((1, t, LANES), lambda b, h, j: (b, nb - 1 - j, h))),
        scratch_shapes=[pltpu.VMEM((B_DV, B_DK), F32), pltpu.VMEM((B_DV, B_DK), F32)],
        compiler_params=_cparams("parallel", "parallel", "arbitrary"),
        name="hgrn",
    )(z, z, z, z, z, z, lg, lg, nf, mf, nbw, mbw)


def _mix0_y(sub, *refs):
    nbr = len(DIL_CFG)
    a_refs, l_refs, e_ref = refs[:nbr], refs[nbr:2 * nbr], refs[2 * nbr]
    of_ref, ob_ref, g_ref, on_ref, w_ref = refs[2 * nbr + 1:2 * nbr + 6]
    scratch = refs[2 * nbr + 6:]
    a = _merge_branches(a_refs, l_refs, e_ref, scratch[:nbr], scratch[nbr:], sub)
    rows = pl.ds(sub * TAIL_SUB, TAIL_SUB)
    o = of_ref[0, rows, :].astype(F32) + ob_ref[0, rows, :].astype(F32)
    g = g_ref[0, rows, :]
    parts = [_rms(o[:, h * B_DV:(h + 1) * B_DV], on_ref[...]) for h in range(B_HEADS)]
    bn = jnp.concatenate(parts, axis=-1) * (g * _sigmoid(g))
    na = a.shape[-1]
    return _dot(a, w_ref[:na, :]) + _dot(bn, w_ref[na:, :])


def _mix1_y(sub, c_ref, d_ref, w_ref):
    rows = pl.ds(sub * TAIL_SUB, TAIL_SUB)
    c = jnp.concatenate([c_ref[0, g, rows, :] for g in range(c_ref.shape[1])], axis=1)
    nc = c.shape[-1]
    return _dot(c, w_ref[:nc, :]) + _dot(d_ref[0, rows, :], w_ref[nc:, :])


def _row_spec(tm, width, col=0):
    return pl.BlockSpec((1, tm, width), lambda b, i: (b, i, col))


def _tail_kernel(*refs, nmix, mix_fn, bounds):
    (x_ref, gm_ref, gatem_ref, g1_ref, sc_ref, sh_ref, wi_ref, wo_ref, g2_ref, gatef_ref,
     o_ref) = refs[nmix:nmix + 11]
    hidden = wo_ref.shape[0]
    for sub in range(x_ref.shape[1] // TAIL_SUB):
        rows = pl.ds(sub * TAIL_SUB, TAIL_SUB)
        y_mix = mix_fn(sub, *refs[:nmix], *refs[nmix + 11:])
        x1 = x_ref[0, rows, :] + gatem_ref[0] * _rms(y_mix, gm_ref[...])
        h = (_rms(x1, g1_ref[...]) * (1.0 + sc_ref[0]) + sh_ref[0]).astype(BF16)
        y = None
        for c0, c1 in zip(bounds, bounds[1:]):
            gt = jnp.dot(h, wi_ref[:, c0:c1], preferred_element_type=F32)
            up = jnp.dot(h, wi_ref[:, hidden + c0:hidden + c1], preferred_element_type=F32)
            part = _dot(gt * _sigmoid(gt) * up, wo_ref[c0:c1, :])
            y = part if y is None else y + part
        o_ref[0, rows, :] = x1 + gatef_ref[0] * _rms(y, g2_ref[...])


def _layer_tail(mix_fn, mix_args, mix_specs, x, gain_m, gate_m, g1, sc, sh, wi_bf16, wo_bf16, layer,
                g2, gate_f, tm, name, mix_scratch=()):
    bsz, seq, d = x.shape
    vec = pl.BlockSpec((1, 1, d), lambda b, i: (b, 0, 0))
    hidden = wo_bf16.shape[1]

    def layer_slab(w):
        return pl.BlockSpec((None,) + w.shape[1:], lambda b, i: (layer, 0, 0),
                            pipeline_mode=pl.Buffered(1))

    ntile = hidden // MXU_WIDTH
    assert ntile * MXU_WIDTH == hidden
    bounds = (0, hidden)
    row = lambda v: v.reshape(1, d)
    per_batch = lambda v: v.reshape(bsz, 1, d)
    return pl.pallas_call(
        functools.partial(_tail_kernel, nmix=len(mix_args), mix_fn=mix_fn, bounds=bounds),
        out_shape=jax.ShapeDtypeStruct(x.shape, F32),
        grid=(bsz, seq // tm),
        in_specs=list(mix_specs) + [_row_spec(tm, d), _const_spec((1, d)), vec, _const_spec((1, d)),
                                    vec, vec, layer_slab(wi_bf16), layer_slab(wo_bf16),
                                    _const_spec((1, d)), vec],
        out_specs=_row_spec(tm, d),
        scratch_shapes=list(mix_scratch),
        compiler_params=_cparams("parallel", "parallel"),
        name=name,
    )(*mix_args, x, row(gain_m), per_batch(gate_m), row(g1), per_batch(sc), per_batch(sh),
      wi_bf16, wo_bf16, row(g2), per_batch(gate_f))


def _tail0(branches, o_f, o_b, z, g_col, out_norm, w_bf16, tm, **kw):
    wv = B_HEADS * B_DV
    m_arrays, m_specs, m_scratch = _merge_operands(branches, tm)
    specs = m_specs + [_row_spec(tm, wv), _row_spec(tm, wv), _row_spec(tm, wv, g_col // wv),
                       _const_spec((1, B_DV)), _const_spec(w_bf16.shape)]
    args = m_arrays + [o_f, o_b, z, out_norm.reshape(1, B_DV), w_bf16]
    return _layer_tail(_mix0_y, args, specs, tm=tm, name="tail0", mix_scratch=m_scratch, **kw)


def _tail1(c_out, d_out, w_bf16, tm, **kw):
    specs = [pl.BlockSpec((1, c_out.shape[1], tm, c_out.shape[3]), lambda b, i: (b, 0, i, 0)),
             _row_spec(tm, d_out.shape[-1]), _const_spec(w_bf16.shape)]
    return _layer_tail(_mix1_y, (c_out, d_out, w_bf16), specs, tm=tm, name="tail1", **kw)


def _fft_kernel(u_ref, f1_ref, twc_ref, tws_ref, f2_ref, fw_ref, o_ref, u_sc, p_sc, y_sc,
                *, scale, n1, n2):
    pu = n2 + FFT_PAD
    pp = 2 * n1 + FFT_PAD
    py = n1 + FFT_PAD
    f1 = f1_ref[...].astype(BF16)
    f2 = f2_ref[...].astype(BF16)
    fw = fw_ref[...].astype(BF16)
    for i1 in range(n1):
        u_sc[i1 * pu:i1 * pu + n2, :] = u_ref[0, 0, i1 * n2:(i1 + 1) * n2, :]

    nb = FFT_BATCH

    def stage1(blk, carry):
        i2s = [blk * nb + j for j in range(nb)]
        x = jnp.concatenate([u_sc[pl.ds(i2, n1, stride=pu), :] for i2 in i2s], axis=1)
        p = jnp.dot(f1, x.astype(BF16), preferred_element_type=F32)
        for j, i2 in enumerate(i2s):
            p_sc[pl.ds(pl.multiple_of(i2 * pp, 8), 2 * n1), :] = p[:, j * C_WIDTH:(j + 1) * C_WIDTH]
        return carry

    lax.fori_loop(0, n2 // nb, stage1, 0, unroll=2)

    def stage2(blk, carry):
        k1s = [blk * nb + j for j in range(nb)]
        qr, qi = [], []
        for k1 in k1s:
            tc = twc_ref[k1]
            ts = tws_ref[k1]
            pr = p_sc[pl.ds(k1, n2, stride=pp), :]
            pim = p_sc[pl.ds(n1 + k1, n2, stride=pp), :]
            qr.append(pr * tc + pim * ts)
            qi.append(pim * tc - pr * ts)
        q = jnp.concatenate([jnp.concatenate(qr, axis=1), jnp.concatenate(qi, axis=1)], axis=0)
        xx = jnp.dot(f2, q.astype(BF16), preferred_element_type=F32)
        xg = jnp.concatenate(
            [jnp.concatenate([xx[:n2, j * C_WIDTH:(j + 1) * C_WIDTH],
                              xx[n2:, j * C_WIDTH:(j + 1) * C_WIDTH]], axis=1) for j in range(nb)],
            axis=0)
        y = jnp.dot(xg.astype(BF16), fw, preferred_element_type=F32) * scale
        for j, k1 in enumerate(k1s):
            y_sc[pl.ds(k1, n2, stride=py), :] = y[j * n2:(j + 1) * n2]
        return carry

    lax.fori_loop(0, n1 // nb, stage2, 0, unroll=2)
    for k2 in range(n2):
        o_ref[0, 0, k2 * n1:(k2 + 1) * n1, :] = y_sc[k2 * py:k2 * py + n1, :]


def _fourier_mixer(u):
    bsz, ngroups, seq, width = u.shape
    n2 = FFT_N2
    n1 = seq // n2
    assert n1 * n2 == seq and width == C_WIDTH and n1 % 8 == 0
    a1 = 2.0 * np.pi * np.outer(np.arange(n1), np.arange(n1)) / n1
    f1 = np.concatenate([np.cos(a1), -np.sin(a1)], axis=0)
    a2 = 2.0 * np.pi * np.outer(np.arange(n2), np.arange(n2)) / n2
    c2, s2 = np.cos(a2), np.sin(a2)
    f2 = np.block([[c2, s2], [-s2, c2]])
    aw = 2.0 * np.pi * np.outer(np.arange(C_WIDTH), np.arange(C_WIDTH)) / C_WIDTH
    fw = np.concatenate([np.cos(aw), np.sin(aw)], axis=0)
    at = np.repeat((2.0 * np.pi * np.outer(np.arange(n1), np.arange(n2)) / seq)[:, :, None],
                   C_WIDTH, axis=2)
    consts = (jnp.asarray(f1, F32), jnp.asarray(np.cos(at), F32), jnp.asarray(np.sin(at), F32),
              jnp.asarray(f2, F32), jnp.asarray(fw, F32))
    blk = pl.BlockSpec((1, 1, seq, C_WIDTH), lambda b, g: (b, g, 0, 0))
    return pl.pallas_call(
        functools.partial(_fft_kernel, scale=float(1.0 / np.sqrt(seq * C_WIDTH)), n1=n1, n2=n2),
        out_shape=jax.ShapeDtypeStruct(u.shape, F32),
        grid=(bsz, ngroups),
        in_specs=[blk] + [_const_spec(c.shape) for c in consts],
        out_specs=blk,
        scratch_shapes=[pltpu.VMEM((n1 * (n2 + FFT_PAD), C_WIDTH), F32),
                        pltpu.VMEM((n2 * (2 * n1 + FFT_PAD), C_WIDTH), F32),
                        pltpu.VMEM((n2 * (n1 + FFT_PAD), C_WIDTH), F32)],
        compiler_params=_cparams("parallel", "parallel"),
        name="fft",
    )(u, *consts)


def _head_perm():
    rep = D_Q_HEADS // D_KV_HEADS
    cols = []
    for j in range(rep):
        for g in range(D_KV_HEADS):
            h = g * rep + j
            cols.extend(range(h * D_HEAD_DIM, (h + 1) * D_HEAD_DIM))
    return np.asarray(cols, np.int32)


def _rope_tables(seq):
    rows = seq // GRID_W
    row = jnp.repeat(jnp.arange(rows, dtype=F32), GRID_W)
    col = jnp.tile(jnp.arange(GRID_W, dtype=F32), rows)
    axis_dim = D_HEAD_DIM // 2
    inv_freq = jnp.power(ROPE_THETA, -jnp.arange(0, axis_dim, 2, dtype=F32) / axis_dim)
    ang_r = row[:, None] * inv_freq[None, :]
    ang_c = col[:, None] * inv_freq[None, :]
    cr, sr, cc, sc = jnp.cos(ang_r), jnp.sin(ang_r), jnp.cos(ang_c), jnp.sin(ang_c)
    cos = jnp.concatenate([cr, cr, cc, cc], axis=1)
    sin = jnp.concatenate([-sr, sr, -sc, sc], axis=1)
    reps = LANES // D_HEAD_DIM
    return jnp.tile(cos, (1, reps)), jnp.tile(sin, (1, reps))


def _inproj_qk_kernel(x_ref, gain_ref, sc_ref, sh_ref, w_ref, cos_ref, sin_ref, bd_h_ref, bd_l_ref,
                      gq_ref, gk_ref, u_ref, qo_ref, ko_ref, vo_ref):
    h = _rms(x_ref[0], gain_ref[...]) * (1.0 + sc_ref[0]) + sh_ref[0]
    z = _dot(h, w_ref[...])
    ngrp = u_ref.shape[1]
    for g in range(ngrp):
        u_ref[0, g] = z[:, g * LANES:(g + 1) * LANES]
    cos = cos_ref[...]
    sin = sin_ref[...]
    quarter = D_HEAD_DIM // 4
    lane = lax.broadcasted_iota(jnp.int32, (1, LANES), 1)
    first_of_pair = (lane // quarter) % 2 == 0

    def norm_rope(x, gain, scale):
        ms = _dot_tab(bd_h_ref[...], bd_l_ref[...], x * x, tab_left=False)
        xn = x * lax.rsqrt(ms + EPS) * gain
        partner = jnp.where(first_of_pair, pltpu.roll(xn, LANES - quarter, 1),
                            pltpu.roll(xn, quarter, 1))
        return ((xn * cos + partner * sin) * scale).astype(BF16)

    nq = qo_ref.shape[-1] // LANES
    for j in range(nq):
        qo_ref[0, :, j * LANES:(j + 1) * LANES] = norm_rope(
            z[:, (ngrp + j) * LANES:(ngrp + j + 1) * LANES], gq_ref[...], D_HEAD_DIM ** -0.5 * LOG2E)
    ko_ref[0] = norm_rope(z[:, (ngrp + nq) * LANES:(ngrp + nq + 1) * LANES], gk_ref[...], 1.0)
    vt = z[:, (ngrp + nq + 1) * LANES:(ngrp + nq + 2) * LANES].T
    ones = jnp.ones((FLASH_ONES, vt.shape[1]), F32)
    vo_ref[0] = jnp.concatenate(
        [piece for g in range(D_KV_HEADS)
         for piece in (vt[g * D_HEAD_DIM:(g + 1) * D_HEAD_DIM], ones)], axis=0).astype(BF16)


def _inproj_qk(x, gain, sc, sh, w_bf16, qk_norm_j, tm):
    bsz, seq, d = x.shape
    n = w_bf16.shape[1]
    qw = D_Q_HEADS * D_HEAD_DIM
    kw = D_KV_HEADS * D_HEAD_DIM
    assert kw == LANES and n == C_GROUPS * C_WIDTH + qw + 2 * kw
    vrows = D_KV_HEADS * (D_HEAD_DIM + FLASH_ONES)
    cos, sin = _rope_tables(seq)
    bd = np.kron(np.eye(LANES // D_HEAD_DIM), np.full((D_HEAD_DIM, D_HEAD_DIM), 1.0 / D_HEAD_DIM))
    bd_h, bd_l = _np_split2(bd)
    reps = LANES // D_HEAD_DIM
    gq = jnp.tile(qk_norm_j[0].astype(F32), reps).reshape(1, LANES)
    gk = jnp.tile(qk_norm_j[1].astype(F32), reps).reshape(1, LANES)
    tab = pl.BlockSpec((tm, LANES), lambda b, i: (i, 0))
    vec = pl.BlockSpec((1, 1, d), lambda b, i: (b, 0, 0))
    return pl.pallas_call(
        _inproj_qk_kernel,
        out_shape=(jax.ShapeDtypeStruct((bsz, C_GROUPS, seq, C_WIDTH), F32),
                   jax.ShapeDtypeStruct((bsz, seq, qw), BF16),
                   jax.ShapeDtypeStruct((bsz, seq, kw), BF16),
                   jax.ShapeDtypeStruct((bsz, vrows, seq), BF16)),
        grid=(bsz, seq // tm),
        in_specs=[pl.BlockSpec((1, tm, d), lambda b, i: (b, i, 0)),
                  _const_spec((1, d)), vec, vec, _const_spec((d, n)),
                  tab, tab, _const_spec(bd_h.shape), _const_spec(bd_l.shape),
                  _const_spec((1, LANES)), _const_spec((1, LANES))],
        out_specs=(pl.BlockSpec((1, C_GROUPS, tm, C_WIDTH), lambda b, i: (b, 0, i, 0)),
                   _row_spec(tm, qw), _row_spec(tm, kw),
                   pl.BlockSpec((1, vrows, tm), lambda b, i: (b, 0, i))),
        compiler_params=_cparams("parallel", "parallel"),
        name="inproj_qk",
    )(x, gain.reshape(1, d), sc.reshape(bsz, 1, d), sh.reshape(bsz, 1, d), w_bf16,
      cos, sin, bd_h, bd_l, gq, gk)


def _flash_kernel(q_ref, k_ref, vt_ref, o_ref, m_sc, acc_sc, s_sc):
    kv = pl.program_id(2)

    @pl.when(kv == 0)
    def _():
        m_sc[...] = jnp.full_like(m_sc, -jnp.inf)
        acc_sc[...] = jnp.zeros_like(acc_sc)

    lane = lax.broadcasted_iota(jnp.int32, (1, LANES), 1)
    lo = lane < D_HEAD_DIM
    nblk = q_ref.shape[-1] // LANES
    tq, tk = q_ref.shape[1], k_ref.shape[1]
    ku, qu = FLASH_KEY_UNIT, min(FLASH_QUERY_UNIT, tq)
    grows = D_HEAD_DIM + FLASH_ONES
    nheads = nblk * D_KV_HEADS
    k = k_ref[0]

    nchunk = tq // qu

    def logits_chunk(idx, c):
        j, g = divmod(idx, D_KV_HEADS)
        qj = q_ref[0, c * qu:(c + 1) * qu, j * LANES:(j + 1) * LANES]
        sel = lo if g == 0 else jnp.logical_not(lo)
        s = lax.dot_general(k, jnp.where(sel, qj, jnp.zeros_like(qj)), (((1,), (1,)), ((), ())),
                            preferred_element_type=F32)
        s_sc[idx % s_sc.shape[0], :, c * qu:(c + 1) * qu] = s
        return jnp.max(s, axis=0, keepdims=True)

    def finish_logits(idx, mcs):
        m_prev = m_sc[idx, 0:1, :]
        m_new = jnp.maximum(m_prev, jnp.concatenate(mcs, axis=1))
        m_sc[idx, 0:1, :] = m_new
        return m_new, jnp.exp2(m_prev - m_new)

    def value_chunk(idx, c, m_new, alpha):
        j, g = divmod(idx, D_KV_HEADS)
        rows = slice(g * grows, (g + 1) * grows)
        qcols = slice(c * qu, (c + 1) * qu)
        pv = None
        for u in range(tk // ku):
            keys = slice(u * ku, (u + 1) * ku)
            p = jnp.exp2(s_sc[idx % s_sc.shape[0], keys, qcols] - m_new[:, qcols])
            d = jnp.dot(vt_ref[0, rows, keys], p.astype(BF16), preferred_element_type=F32)
            pv = d if pv is None else pv + d
        acc_sc[j, rows, qcols] = alpha[:, qcols] * acc_sc[j, rows, qcols] + pv

    def logits_pass(idx):
        return finish_logits(idx, [logits_chunk(idx, c) for c in range(nchunk)])

    pending = [logits_pass(i) for i in range(min(FLASH_AHEAD, nheads))]
    for idx in range(nheads):
        if idx + FLASH_AHEAD < nheads:
            pending.append(logits_pass(idx + FLASH_AHEAD))
        stats = pending.pop(0)
        for c in range(nchunk):
            value_chunk(idx, c, *stats)

    @pl.when(kv == pl.num_programs(2) - 1)
    def _():
        for j in range(nblk):
            parts = []
            for g in range(D_KV_HEADS):
                num = acc_sc[j, g * grows:g * grows + D_HEAD_DIM, :]
                den = acc_sc[j, g * grows + D_HEAD_DIM:g * grows + D_HEAD_DIM + 1, :]
                parts.append(num / den)
            o_ref[0, :, j * LANES:(j + 1) * LANES] = (
                jnp.concatenate(parts, axis=0).T.astype(o_ref.dtype))


def _flash(q, k, vt, tq, tk):
    bsz, seq, qw = q.shape
    kw = k.shape[-1]
    vrows = vt.shape[1]
    assert FLASH_LOGIT_BUFS > FLASH_AHEAD
    return pl.pallas_call(
        _flash_kernel,
        out_shape=jax.ShapeDtypeStruct((bsz, seq, qw), BF16),
        grid=(bsz, seq // tq, seq // tk),
        in_specs=[pl.BlockSpec((1, tq, qw), lambda b, i, j: (b, i, 0)),
                  pl.BlockSpec((1, tk, kw), lambda b, i, j: (b, j, 0)),
                  pl.BlockSpec((1, vrows, tk), lambda b, i, j: (b, 0, j))],
        out_specs=pl.BlockSpec((1, tq, qw), lambda b, i, j: (b, i, 0)),
        scratch_shapes=[pltpu.VMEM((D_Q_HEADS, 8, tq), F32),
                        pltpu.VMEM((qw // LANES, vrows, tq), F32),
                        pltpu.VMEM((FLASH_LOGIT_BUFS, tk, tq), F32)],
        compiler_params=_cparams("parallel", "parallel", "arbitrary"),
        name="flash",
    )(q, k, vt)


def kernel(x, c, t5_bias, hgrn_lb_logits, ada_w, ada_b, norm_gains, ab_w_in, ab_w_out,
           hgrn_out_norm, cd_w_in, cd_w_out, qk_norm, ffn_w_in, ffn_w_out):
    bsz, seq, d = x.shape
    depth = ada_w.shape[0]
    mod = _ada_mod(c.astype(F32), ada_w, ada_b)
    perm = _head_perm()
    aw = A_HEADS * A_HEAD_DIM
    cw = C_GROUPS * C_WIDTH
    qw = D_Q_HEADS * D_HEAD_DIM
    tm_in = min(512, seq)
    ffn_wi = ffn_w_in.astype(BF16)
    ffn_wo = ffn_w_out.astype(BF16)
    for layer in range(depth):
        sh_m, sc_m, g_m, sh_f, sc_f, g_f = [mod[layer, :, i * d:(i + 1) * d] for i in range(6)]
        gains = norm_gains[layer]
        j = layer // 2
        tail = dict(x=x, gain_m=gains[1], gate_m=g_m, g1=gains[2], sc=sc_f, sh=sh_f,
                    wi_bf16=ffn_wi, wo_bf16=ffn_wo, layer=layer, g2=gains[3], gate_f=g_f)
        if layer % 2 == 0:
            w_in = ab_w_in[j].astype(BF16)
            *qkv_cm, z = _inproj_cm(x, gains[0], sc_m, sh_m, w_in, 3 * aw, tm_in)
            branches = [_dilated_branch(cm, t5_bias, window, dil)
                        for cm, (window, dil) in zip(qkv_cm, DIL_CFG)]
            o_f, o_b = _hgrn(z, hgrn_lb_logits, layer, 0)
            g_col = 3 * B_HEADS * B_DK + B_HEADS * B_DV
            x = _tail0(branches, o_f, o_b, z, g_col, hgrn_out_norm[j], ab_w_out[j].astype(BF16),
                       tm=min(TAIL_SUB, seq), **tail)
        else:
            w_full = cd_w_in[j]
            w_in = jnp.concatenate([w_full[:, :cw], w_full[:, cw:cw + qw][:, perm],
                                    w_full[:, cw + qw:]], axis=1).astype(BF16)
            u, qn, kn, vn = _inproj_qk(x, gains[0], sc_m, sh_m, w_in, qk_norm[j], min(1024, seq))
            c_out = _fourier_mixer(u)
            d_out = _flash(qn, kn, vn, min(FLASH_TQ, seq), min(FLASH_TK, seq))
            w_out_full = cd_w_out[j]
            w_out = jnp.concatenate([w_out_full[:cw], w_out_full[cw:][perm]], axis=0).astype(BF16)
            x = _tail1(c_out, d_out, w_out, tm=min(TAIL_BLOCK, seq), **tail)
    return x
```

```python
import functools

import numpy as np
import jax
import jax.numpy as jnp
from jax import lax
from jax.experimental import pallas as pl
from jax.experimental.pallas import tpu as pltpu

F32 = jnp.float32
BF16 = jnp.bfloat16
LANES = 128
MXU_WIDTH = 256
VMEM_LIMIT_BYTES = 56 * 2**20
NEG_INF = -1e30
EPS = 1e-6

GRID_W = 64
A_HEADS = 8
A_HEAD_DIM = 64
DIL_CFG = ((128, 1), (512, 4), (2048, 16))
N_BUCKETS = 32
T5_MAX_DIST = 1024
B_HEADS = 4
B_DK = 128
B_DV = 128
C_GROUPS = 4
C_WIDTH = 128
D_Q_HEADS = 8
D_KV_HEADS = 2
D_HEAD_DIM = 64
ROPE_THETA = 10000.0

DIL_TQ = 128
DIL_TILE = 512
HGRN_BLOCK = 1024
HGRN_SUB = 256
HGRN_BCAST_MIN = 8
FLASH_TQ = 1024
FLASH_TK = 8192
FLASH_LOGIT_BUFS = 1
FLASH_KEY_UNIT = 256
FLASH_QUERY_UNIT = 1024
FLASH_AHEAD = 0
FLASH_ONES = 16
TAIL_BLOCK = 512
TAIL_SUB = 512
FFT_N2 = 128
FFT_BATCH = 8
FFT_PAD = 8
LOG2E = 1.4426950408889634


def _cparams(*sem):
    return pltpu.CompilerParams(dimension_semantics=sem, vmem_limit_bytes=VMEM_LIMIT_BYTES)


def _const_spec(shape):
    nd = len(shape)
    return pl.BlockSpec(shape, lambda *_: (0,) * nd, pipeline_mode=pl.Buffered(1))


def _sigmoid(x):
    return 1.0 / (1.0 + jnp.exp(-x))


def _dot(a, b):
    return jnp.dot(a.astype(BF16), b.astype(BF16), preferred_element_type=F32)


def _dot_nt(a, b):
    return lax.dot_general(a.astype(BF16), b.astype(BF16), (((1,), (1,)), ((), ())),
                           preferred_element_type=F32)


def _split2(a):
    hi = a.astype(BF16)
    lo = (a - hi.astype(F32)).astype(BF16)
    return hi, lo


def _dot_tab(tab_hi, tab_lo, x, *, tab_left):
    x_hi, x_lo = _split2(x)
    if tab_left:
        d = lambda t, v: jnp.dot(t, v, preferred_element_type=F32)
    else:
        d = lambda t, v: jnp.dot(v, t, preferred_element_type=F32)
    return d(tab_hi, x_hi) + (d(tab_hi, x_lo) + d(tab_lo, x_hi))


def _rms(x, gain):
    ms = jnp.mean(x * x, axis=-1, keepdims=True)
    return x * lax.rsqrt(ms + EPS) * gain


def _np_split2(t):
    t = np.asarray(t, np.float32)
    hi = jnp.asarray(t, F32).astype(BF16)
    lo = (jnp.asarray(t, F32) - hi.astype(F32)).astype(BF16)
    return hi, lo


def _mod_kernel(c_ref, w_ref, b_ref, o_ref):
    c = c_ref[...]
    o_ref[0] = _dot(c * _sigmoid(c), w_ref[0]) + b_ref[0]


def _ada_mod(c, ada_w, ada_b):
    depth, d, n6 = ada_w.shape
    bsz = c.shape[0]
    rows = 8
    cp = jnp.zeros((rows, d), F32).at[:bsz].set(c)
    tn = n6 // 4
    out = pl.pallas_call(
        _mod_kernel,
        out_shape=jax.ShapeDtypeStruct((depth, rows, n6), F32),
        grid=(depth, n6 // tn),
        in_specs=[pl.BlockSpec((rows, d), lambda l, j: (0, 0)),
                  pl.BlockSpec((1, d, tn), lambda l, j: (l, 0, j)),
                  pl.BlockSpec((1, 1, tn), lambda l, j: (l, 0, j))],
        out_specs=pl.BlockSpec((1, rows, tn), lambda l, j: (l, 0, j)),
        compiler_params=_cparams("parallel", "parallel"),
        name="ada_mod",
    )(cp, ada_w, ada_b.reshape(depth, 1, n6))
    return out[:, :bsz]


def _inproj_cm_kernel(x_ref, gain_ref, sc_ref, sh_ref, w_ref, *refs):
    cm_refs, rest_ref, zs_sc, zc_sc = refs[:-3], refs[-3], refs[-2], refs[-1]
    h = _rms(x_ref[0], gain_ref[...]) * (1.0 + sc_ref[0]) + sh_ref[0]
    z = _dot(h, w_ref[...])
    nblk, tm, _ = zs_sc.shape
    rest_ref[0] = z[:, nblk * LANES:]
    nq = A_HEADS * A_HEAD_DIM // LANES
    for c in range(nblk):
        blk = z[:, c * LANES:(c + 1) * LANES]
        zs_sc[c] = blk * (A_HEAD_DIM ** -0.5 * LOG2E) if c < nq else blk
    src, sd = zs_sc, 1
    for level, (cm_ref, (_, dil)) in enumerate(zip(cm_refs, DIL_CFG)):
        step, n = dil // sd, tm // dil
        keep = dil > 1 and level + 1 < len(DIL_CFG)
        for rs in range(sd):
            for cc in range(step):
                r = rs + sd * cc
                for c in range(nblk):
                    rows = src[c, pl.ds(rs * (tm // sd) + cc, n, stride=step), :]
                    cm_ref[0, r, :, c * LANES:(c + 1) * LANES] = rows.astype(BF16)
                    if keep:
                        zc_sc[c, r * n:(r + 1) * n, :] = rows
        if keep:
            src, sd = zc_sc, dil


def _inproj_cm(x, gain, sc, sh, w_bf16, na, tm):
    bsz, seq, d = x.shape
    n = w_bf16.shape[1]
    vec = pl.BlockSpec((1, 1, d), lambda b, i: (b, 0, 0))
    dils = [dl for _, dl in DIL_CFG]
    assert dils[0] == 1 and all(b % a == 0 for a, b in zip(dils, dils[1:]))
    return pl.pallas_call(
        _inproj_cm_kernel,
        out_shape=tuple([jax.ShapeDtypeStruct((bsz, dl, seq // dl, na), BF16) for dl in dils]
                        + [jax.ShapeDtypeStruct((bsz, seq, n - na), F32)]),
        grid=(bsz, seq // tm),
        in_specs=[pl.BlockSpec((1, tm, d), lambda b, i: (b, i, 0)),
                  _const_spec((1, d)), vec, vec, _const_spec((d, n))],
        out_specs=tuple([pl.BlockSpec((1, dl, tm // dl, na), lambda b, i: (b, 0, i, 0)) for dl in dils]
                        + [pl.BlockSpec((1, tm, n - na), lambda b, i: (b, i, 0))]),
        scratch_shapes=[pltpu.VMEM((na // LANES, tm, LANES), F32),
                        pltpu.VMEM((na // LANES, tm, LANES), F32)],
        compiler_params=_cparams("parallel", "parallel"),
        name="inproj_cm",
    )(x, gain.reshape(1, d), sc.reshape(bsz, 1, d), sh.reshape(bsz, 1, d), w_bf16)


def _t5_buckets(rel):
    half = N_BUCKETS // 2
    max_exact = half // 2
    n = np.abs(rel)
    large = max_exact + (np.log(np.maximum(n, 1) / max_exact) / np.log(T5_MAX_DIST / max_exact)
                         * (half - max_exact)).astype(np.int32)
    large = np.minimum(large, half - 1)
    return (np.where(rel > 0, half, 0) + np.where(n < max_exact, n, large)).astype(np.int32)


def _dil_bias(t5_bias, window, dil, tq):
    half = (window // 2) // dil
    assert half == tq // 2
    rel = np.arange(2 * tq)[None, :] - half - np.arange(tq)[:, None]
    inside = np.abs(rel) <= half
    buckets = _t5_buckets(np.where(inside, rel, 0) * dil)
    onehot =jnp.asarray(np.eye(N_BUCKETS, dtype=np.float32)[buckets])
    bias = jnp.einsum("qkn,nh->hqk", onehot, t5_bias.astype(F32), precision=lax.Precision.HIGHEST)
    return jnp.where(jnp.asarray(inside)[None], bias * LOG2E, NEG_INF)


def _dil_kernel(q_ref, kp_ref, kc_ref, kn_ref, vp_ref, vc_ref, vn_ref, bias_ref, o_ref, lse_ref,
                *, class_len):
    i = pl.program_id(2)
    sub, hq, tile = DIL_TQ, DIL_TQ // 2, q_ref.shape[2]
    kwin = jnp.concatenate([kp_ref[0, 0], kc_ref[0, 0], kn_ref[0, 0]], axis=0)
    vwin = jnp.concatenate([vp_ref[0, 0], vc_ref[0, 0], vn_ref[0, 0]], axis=0)
    lane = lax.broadcasted_iota(jnp.int32, (1, LANES), 1)
    lo = lane < A_HEAD_DIM
    nblk = A_HEADS // 2
    units = [(jt, j) for jt in range(tile // sub) for j in range(nblk)]
    logits = []
    for jt, j in units:
        cols = slice(j * LANES, (j + 1) * LANES)
        qj = q_ref[0, 0, jt * sub:(jt + 1) * sub, cols]
        zero = jnp.zeros_like(qj)
        q2 = jnp.concatenate([jnp.where(lo, qj, zero), jnp.where(lo, zero, qj)], axis=0)
        s = lax.dot_general(q2, kwin[jt * sub:jt * sub + 2 * sub, cols], (((1,), (1,)), ((), ())),
                            preferred_element_type=F32)
        kpos = i * tile + jt * sub - hq + lax.broadcasted_iota(jnp.int32, (1, 2 * sub), 1)
        valid = jnp.logical_and(kpos >= 0, kpos < class_len)
        logits.append(jnp.where(valid, s + bias_ref[j], NEG_INF))
    s_all = jnp.concatenate(logits, axis=0)
    m = jnp.max(s_all, axis=-1, keepdims=True)
    p32 = jnp.exp2(s_all - m)
    l = jnp.sum(p32, axis=-1, keepdims=True)
    p = p32.astype(BF16)
    rinv = 1.0 / l
    lse = m + jnp.log2(l)
    for jt in range(tile // sub):
        lse_all = jnp.zeros((sub, LANES), F32)
        for j in range(nblk):
            cols = slice(j * LANES, (j + 1) * LANES)
            r0 = (jt * nblk + j) * 2 * sub
            o2 = jnp.dot(p[r0:r0 + 2 * sub], vwin[jt * sub:jt * sub + 2 * sub, cols],
                         preferred_element_type=F32) * rinv[r0:r0 + 2 * sub]
            lse_all = jnp.where(lane == 2 * j, lse[r0:r0 + sub], lse_all)
            lse_all = jnp.where(lane == 2 * j + 1, lse[r0 + sub:r0 + 2 * sub], lse_all)
            o_ref[0, 0, jt * sub:(jt + 1) * sub, cols] = (
                jnp.where(lo, o2[:sub], o2[sub:]).astype(o_ref.dtype))
        lse_ref[0, 0, jt * sub:(jt + 1) * sub, :] = lse_all


def _dilated_branch(qkv_cm, t5_bias, window, dil):
    bsz, _, cl, width = qkv_cm.shape
    aw = A_HEADS * A_HEAD_DIM
    tile, hq = min(DIL_TILE, cl), DIL_TQ // 2
    nt = cl // tile
    per = tile // hq
    nh = cl // hq

    def cur(col):
        return pl.BlockSpec((1, 1, tile, aw), lambda b, r, i: (b, r, i, col))

    def prev(col):
        return pl.BlockSpec((1, 1, hq, aw), lambda b, r, i: (b, r, jnp.maximum(i * per - 1, 0), col))

    def nxt(col):
        return pl.BlockSpec((1, 1, hq, aw),
                            lambda b, r, i: (b, r, jnp.minimum((i + 1) * per, nh - 1), col))

    return pl.pallas_call(
        functools.partial(_dil_kernel, class_len=cl),
        out_shape=(jax.ShapeDtypeStruct((bsz, dil, cl, aw), BF16),
                   jax.ShapeDtypeStruct((bsz, dil, cl, LANES), F32)),
        grid=(bsz, dil, nt),
        in_specs=[cur(0), prev(1), cur(1), nxt(1), prev(2), cur(2), nxt(2),
                  _const_spec((A_HEADS // 2, 2 * DIL_TQ, 2 * DIL_TQ))],
        out_specs=(pl.BlockSpec((1, 1, tile, aw), lambda b, r, i: (b, r, i, 0)),
                   pl.BlockSpec((1, 1, tile, LANES), lambda b, r, i: (b, r, i, 0))),
        compiler_params=_cparams("parallel", "parallel", "parallel"),
        name=f"dilated_d{dil}",
    )(*([qkv_cm] * 7),
      _dil_bias(t5_bias, window, dil, DIL_TQ).reshape(A_HEADS // 2, 2 * DIL_TQ, 2 * DIL_TQ))


def _merge_branches(a_refs, l_refs, e_ref, a_scs, l_scs, sub):
    tm = l_scs[0].shape[0]
    accs, lses = [], []
    for (_, dil), a_ref, l_ref, a_sc, l_sc in zip(DIL_CFG, a_refs, l_refs, a_scs, l_scs):
        nblk = a_sc.shape[0]
        for r in range(dil):
            rows = pl.ds(r, tm // dil, stride=dil)
            src = pl.ds(sub * (tm // dil), tm // dil)
            for c in range(nblk):
                a_sc[c, rows, :] = a_ref[0, r, src, c * LANES:(c + 1) * LANES].astype(F32)
            l_sc[rows, :] = l_ref[0, r, src, :]
        accs.append(jnp.concatenate([a_sc[c] for c in range(nblk)], axis=1))
        lses.append(l_sc[...])
    mx = functools.reduce(jnp.maximum, lses)
    ws = [jnp.exp2(x - mx) for x in lses]
    tot = functools.reduce(jnp.add, ws)
    out = None
    for w, a in zip(ws, accs):
        w_hi, w_lo = _split2(w / tot)
        wide = (jnp.dot(w_hi, e_ref[...], preferred_element_type=F32)
                + jnp.dot(w_lo, e_ref[...], preferred_element_type=F32))
        out = wide * a if out is None else out + wide * a
    return out


def _merge_operands(branch_outs, tm):
    aw = branch_outs[0][0].shape[-1]
    expand = np.zeros((LANES, aw), np.float32)
    for h in range(A_HEADS):
        expand[h, h * A_HEAD_DIM:(h + 1) * A_HEAD_DIM] = 1.0
    dils = [d for _, d in DIL_CFG]
    specs = ([pl.BlockSpec((1, d, tm // d, aw), lambda b, i: (b, 0, i, 0)) for d in dils]
             + [pl.BlockSpec((1, d, tm // d, LANES), lambda b, i: (b, 0, i, 0)) for d in dils]
             + [_const_spec(expand.shape)])
    arrays = [o for o, _ in branch_outs] + [l for _, l in branch_outs] + [jnp.asarray(expand, BF16)]
    scratch = ([pltpu.VMEM((aw // LANES, TAIL_SUB, LANES), F32) for _ in dils]
               + [pltpu.VMEM((TAIL_SUB, LANES), F32) for _ in dils])
    return arrays, specs, scratch


def _hgrn_consts(reverse):
    t = HGRN_SUB
    r = np.arange(t)
    u = r[None, :]
    row = r[:, None]
    nmats = [(u >= row) if reverse else (u <= row)]
    masks = []
    m = t // 2
    while m >= 1:
        grp = r // (2 * m)
        in_first = (r % (2 * m)) < m
        same = grp[:, None] == grp[None, :]
        if reverse:
            beta = (grp * 2 * m + m)[:, None]
            n = np.where(in_first[:, None], (u >= row) & (u < beta), (u >= beta) & (u < row))
            mask = same & in_first[:, None] & ~in_first[None, :]
        else:
            beta = (grp * 2 * m + m - 1)[:, None]
            n = np.where(in_first[:, None], (u > row) & (u <= beta), (u > beta) & (u <= row))
            mask = same & ~in_first[:, None] & in_first[None, :]
        if m < HGRN_BCAST_MIN:
            nmats.append(n)
        masks.append(mask)
        m //= 2
    masks.append(np.eye(t, dtype=bool))
    nmat = jnp.asarray(np.concatenate(nmats, axis=0), F32).astype(BF16)
    return nmat, jnp.asarray(np.stack(masks), F32)


def _hgrn_block(q, f, v, lb, st, nmat_ref, mask_ref, reverse):
    t = HGRN_SUB
    nlev = mask_ref.shape[0] - 1
    qs = q * (B_DK ** -0.5)
    fa = lb + (1.0 - lb) * _sigmoid(f)
    kk = 1.0 - fa
    g_hi, g_lo = _split2(jnp.log2(fa))
    ex = jnp.dot(nmat_ref[...], jnp.concatenate([g_hi, g_lo], axis=1), preferred_element_type=F32)
    ex = ex[:, :B_DK] + ex[:, B_DK:]
    b = ex[:t]
    btot = b[0:1] if reverse else b[t - 1:t]
    a = mask_ref[nlev] * _dot_nt(qs, kk)
    fine = 1
    for l in range(nlev):
        m = t >> (l + 1)
        if m >= HGRN_BCAST_MIN:
            ref = jnp.concatenate(
                [jnp.broadcast_to(b[beta:beta + 1], (2 * m, B_DK))
                 for beta in range(m if reverse else m - 1, t, 2 * m)], axis=0)
            e = jnp.exp2(-jnp.abs(b - ref))
        else:
            e = jnp.exp2(ex[fine * t:(fine + 1) * t])
            fine += 1
        a = a + mask_ref[l] * _dot_nt(qs * e, kk * e)
    out = _dot(a, v) + _dot_nt(qs * jnp.exp2(b), st)
    khat = (kk * jnp.exp2(btot - b)).astype(BF16)
    st_new = st * jnp.exp2(btot) + jnp.dot(v.T.astype(BF16), khat, preferred_element_type=F32)
    return out, st_new


def _hgrn_lb(lg_ref, layer):
    lg = [lg_ref[l, 0] for l in range(lg_ref.shape[0])]
    mx = functools.reduce(jnp.maximum, lg)
    e = [jnp.exp(x - mx) for x in lg]
    return functools.reduce(jnp.add, e[:layer + 1]) / functools.reduce(jnp.add, e)


def _hgrn_kernel(qf_ref, ff_ref, vf_ref, qb_ref, fb_ref, vb_ref, lgf_ref, lgb_ref,
                 nf_ref, mf_ref, nb_ref, mb_ref, of_ref, ob_ref, sf_sc, sb_sc, *, layer):
    @pl.when(pl.program_id(2) == 0)
    def _():
        sf_sc[...] = jnp.zeros_like(sf_sc)
        sb_sc[...] = jnp.zeros_like(sb_sc)

    nsub = HGRN_BLOCK // HGRN_SUB
    chains = ((qf_ref, ff_ref, vf_ref, lgf_ref, sf_sc, nf_ref, mf_ref, of_ref, False),
              (qb_ref, fb_ref, vb_ref, lgb_ref, sb_sc, nb_ref, mb_ref, ob_ref, True))
    for q_ref, f_ref, v_ref, lg_ref, st_sc, n_ref, m_ref, o_ref, reverse in chains:
        lb = _hgrn_lb(lg_ref, layer)
        st = st_sc[...]
        for sub in (reversed(range(nsub)) if reverse else range(nsub)):
            rows = pl.ds(sub * HGRN_SUB, HGRN_SUB)
            o, st = _hgrn_block(q_ref[0, rows, :], f_ref[0, rows, :], v_ref[0, rows, :], lb, st,
                                n_ref, m_ref, reverse)
            o_ref[0, rows, :] = o.astype(o_ref.dtype)
        st_sc[...] = st


def _hgrn(z, lb_logits, layer, col0):
    bsz, seq, _ = z.shape
    t = HGRN_BLOCK
    nb = seq // t
    c0 = col0 // LANES
    hw = B_HEADS

    def zspec(group, rev):
        return pl.BlockSpec(
            (1, t, LANES),
            lambda b, h, j: (b, (nb - 1 - j) if rev else j, c0 + group * hw + h))

    def lgspec(direction):
        return pl.BlockSpec((lb_logits.shape[0], 1, 1, LANES),
                            lambda b, h, j: (0, direction * hw + h, 0, 0))

    nf, mf = _hgrn_consts(False)
    nbw, mbw = _hgrn_consts(True)
    lg = lb_logits.astype(F32).reshape(lb_logits.shape[0], 2 * hw, 1, LANES)
    o_shape = jax.ShapeDtypeStruct((bsz, seq, hw * B_DV), BF16)
    return pl.pallas_call(
        functools.partial(_hgrn_kernel, layer=layer),
        out_shape=(o_shape, o_shape),
        grid=(bsz, hw, nb),
        in_specs=[zspec(0, False), zspec(1, False), zspec(3, False),
                  zspec(0, True), zspec(2, True), zspec(3, True),
                  lgspec(0), lgspec(1),
                  _const_spec(nf.shape), _const_spec(mf.shape),
                  _const_spec(nbw.shape), _const_spec(mbw.shape)],
        out_specs=(pl.BlockSpec((1, t, LANES), lambda b, h, j: (b, j, h)),
                   pl.BlockSpec((1, t, LANES), lambda b, h, j: (b, nb - 1 - j, h))),
        scratch_shapes=[pltpu.VMEM((B_DV, B_DK), F32), pltpu.VMEM((B_DV, B_DK), F32)],
        compiler_params=_cparams("parallel", "parallel", "arbitrary"),
        name="hgrn",
    )(z, z, z, z, z, z, lg, lg, nf, mf, nbw, mbw)


def _mix0_y(sub, *refs):
    nbr = len(DIL_CFG)
    a_refs, l_refs, e_ref = refs[:nbr], refs[nbr:2 * nbr], refs[2 * nbr]
    of_ref, ob_ref, g_ref, on_ref, w_ref = refs[2 * nbr + 1:2 * nbr + 6]
    scratch = refs[2 * nbr + 6:]
    a = _merge_branches(a_refs, l_refs, e_ref, scratch[:nbr], scratch[nbr:], sub)
    rows = pl.ds(sub * TAIL_SUB, TAIL_SUB)
    o = of_ref[0, rows, :].astype(F32) + ob_ref[0, rows, :].astype(F32)
    g = g_ref[0, rows, :]
    parts = [_rms(o[:, h * B_DV:(h + 1) * B_DV], on_ref[...]) for h in range(B_HEADS)]
    bn = jnp.concatenate(parts, axis=-1) * (g * _sigmoid(g))
    na = a.shape[-1]
    return _dot(a, w_ref[:na, :]) + _dot(bn, w_ref[na:, :])


def _mix1_y(sub, c_ref, d_ref, w_ref):
    rows = pl.ds(sub * TAIL_SUB, TAIL_SUB)
    c = jnp.concatenate([c_ref[0, g, rows, :] for g in range(c_ref.shape[1])], axis=1)
    nc = c.shape[-1]
    return _dot(c, w_ref[:nc, :]) + _dot(d_ref[0, rows, :], w_ref[nc:, :])


def _row_spec(tm, width, col=0):
    return pl.BlockSpec((1, tm, width), lambda b, i: (b, i, col))


def _tail_kernel(*refs, nmix, mix_fn, bounds):
    (x_ref, gm_ref, gatem_ref, g1_ref, sc_ref, sh_ref, wi_ref, wo_ref, g2_ref, gatef_ref,
     o_ref) = refs[nmix:nmix + 11]
    hidden = wo_ref.shape[0]
    for sub in range(x_ref.shape[1] // TAIL_SUB):
        rows = pl.ds(sub * TAIL_SUB, TAIL_SUB)
        y_mix = mix_fn(sub, *refs[:nmix], *refs[nmix + 11:])
        x1 = x_ref[0, rows, :] + gatem_ref[0] * _rms(y_mix, gm_ref[...])
        h = (_rms(x1, g1_ref[...]) * (1.0 + sc_ref[0]) + sh_ref[0]).astype(BF16)
        y = None
        for c0, c1 in zip(bounds, bounds[1:]):
            gt = jnp.dot(h, wi_ref[:, c0:c1], preferred_element_type=F32)
            up = jnp.dot(h, wi_ref[:, hidden + c0:hidden + c1], preferred_element_type=F32)
            part = _dot(gt * _sigmoid(gt) * up, wo_ref[c0:c1, :])
            y = part if y is None else y + part
        o_ref[0, rows, :] = x1 + gatef_ref[0] * _rms(y, g2_ref[...])


def _layer_tail(mix_fn, mix_args, mix_specs, x, gain_m, gate_m, g1, sc, sh, wi_bf16, wo_bf16, layer,
                g2, gate_f, tm, name, mix_scratch=()):
    bsz, seq, d = x.shape
    vec = pl.BlockSpec((1, 1, d), lambda b, i: (b, 0, 0))
    hidden = wo_bf16.shape[1]

    def layer_slab(w):
        return pl.BlockSpec((None,) + w.shape[1:], lambda b, i: (layer, 0, 0),
                            pipeline_mode=pl.Buffered(1))

    ntile = hidden // MXU_WIDTH
    assert ntile * MXU_WIDTH == hidden
    bounds = (0, (ntile + 1) // 2 * MXU_WIDTH, hidden)
    row = lambda v: v.reshape(1, d)
    per_batch = lambda v: v.reshape(bsz, 1, d)
    return pl.pallas_call(
        functools.partial(_tail_kernel, nmix=len(mix_args), mix_fn=mix_fn, bounds=bounds),
        out_shape=jax.ShapeDtypeStruct(x.shape, F32),
        grid=(bsz, seq // tm),
        in_specs=list(mix_specs) + [_row_spec(tm, d), _const_spec((1, d)), vec, _const_spec((1, d)),
                                    vec, vec, layer_slab(wi_bf16), layer_slab(wo_bf16),
                                    _const_spec((1, d)), vec],
        out_specs=_row_spec(tm, d),
        scratch_shapes=list(mix_scratch),
        compiler_params=_cparams("parallel", "parallel"),
        name=name,
    )(*mix_args, x, row(gain_m), per_batch(gate_m), row(g1), per_batch(sc), per_batch(sh),
      wi_bf16, wo_bf16, row(g2), per_batch(gate_f))


def _tail0(branches, o_f, o_b, z, g_col, out_norm, w_bf16, tm, **kw):
    wv = B_HEADS * B_DV
    m_arrays, m_specs, m_scratch = _merge_operands(branches, tm)
    specs = m_specs + [_row_spec(tm, wv), _row_spec(tm, wv), _row_spec(tm, wv, g_col // wv),
                       _const_spec((1, B_DV)), _const_spec(w_bf16.shape)]
    args = m_arrays + [o_f, o_b, z, out_norm.reshape(1, B_DV), w_bf16]
    return _layer_tail(_mix0_y, args, specs, tm=tm, name="tail0", mix_scratch=m_scratch, **kw)


def _tail1(c_out, d_out, w_bf16, tm, **kw):
    specs = [pl.BlockSpec((1, c_out.shape[1], tm, c_out.shape[3]), lambda b, i: (b, 0, i, 0)),
             _row_spec(tm, d_out.shape[-1]), _const_spec(w_bf16.shape)]
    return _layer_tail(_mix1_y, (c_out, d_out, w_bf16), specs, tm=tm, name="tail1", **kw)


def _fft_kernel(u_ref, f1_ref, twc_ref, tws_ref, f2_ref, fw_ref, o_ref, u_sc, p_sc, y_sc,
                *, scale, n1, n2):
    pu = n2 + FFT_PAD
    pp = 2 * n1 + FFT_PAD
    py = n1 + FFT_PAD
    f1 = f1_ref[...].astype(BF16)
    f2 = f2_ref[...].astype(BF16)
    fw = fw_ref[...].astype(BF16)
    for i1 in range(n1):
        u_sc[i1 * pu:i1 * pu + n2, :] = u_ref[0, 0, i1 * n2:(i1 + 1) * n2, :]

    nb = FFT_BATCH

    def stage1(blk, carry):
        i2s = [blk * nb + j for j in range(nb)]
        x = jnp.concatenate([u_sc[pl.ds(i2, n1, stride=pu), :] for i2 in i2s], axis=1)
        p = jnp.dot(f1, x.astype(BF16), preferred_element_type=F32)
        for j, i2 in enumerate(i2s):
            p_sc[pl.ds(pl.multiple_of(i2 * pp, 8), 2 * n1), :] = p[:, j * C_WIDTH:(j + 1) * C_WIDTH]
        return carry

    lax.fori_loop(0, n2 // nb, stage1, 0, unroll=2)

    def stage2(blk, carry):
        k1s = [blk * nb + j for j in range(nb)]
        qr, qi = [], []
        for k1 in k1s:
            tc = twc_ref[k1]
            ts = tws_ref[k1]
            pr = p_sc[pl.ds(k1, n2, stride=pp), :]
            pim = p_sc[pl.ds(n1 + k1, n2, stride=pp), :]
            qr.append(pr * tc + pim * ts)
            qi.append(pim * tc - pr * ts)
        q = jnp.concatenate([jnp.concatenate(qr, axis=1), jnp.concatenate(qi, axis=1)], axis=0)
        xx = jnp.dot(f2, q.astype(BF16), preferred_element_type=F32)
        xg = jnp.concatenate(
            [jnp.concatenate([xx[:n2, j * C_WIDTH:(j + 1) * C_WIDTH],
                              xx[n2:, j * C_WIDTH:(j + 1) * C_WIDTH]], axis=1) for j in range(nb)],
            axis=0)
        y = jnp.dot(xg.astype(BF16), fw, preferred_element_type=F32) * scale
        for j, k1 in enumerate(k1s):
            y_sc[pl.ds(k1, n2, stride=py), :] = y[j * n2:(j + 1) * n2]
        return carry

    lax.fori_loop(0, n1 // nb, stage2, 0, unroll=2)
    for k2 in range(n2):
        o_ref[0, 0, k2 * n1:(k2 + 1) * n1, :] = y_sc[k2 * py:k2 * py + n1, :]


def _fourier_mixer(u):
    bsz, ngroups, seq, width = u.shape
    n2 = FFT_N2
    n1 = seq // n2
    assert n1 * n2 == seq and width == C_WIDTH and n1 % 8 == 0
    a1 = 2.0 * np.pi * np.outer(np.arange(n1), np.arange(n1)) / n1
    f1 = np.concatenate([np.cos(a1), -np.sin(a1)], axis=0)
    a2 = 2.0 * np.pi * np.outer(np.arange(n2), np.arange(n2)) / n2
    c2, s2 = np.cos(a2), np.sin(a2)
    f2 = np.block([[c2, s2], [-s2, c2]])
    aw = 2.0 * np.pi * np.outer(np.arange(C_WIDTH), np.arange(C_WIDTH)) / C_WIDTH
    fw = np.concatenate([np.cos(aw), np.sin(aw)], axis=0)
    at = np.repeat((2.0 * np.pi * np.outer(np.arange(n1), np.arange(n2)) / seq)[:, :, None],
                   C_WIDTH, axis=2)
    consts = (jnp.asarray(f1, F32), jnp.asarray(np.cos(at), F32), jnp.asarray(np.sin(at), F32),
              jnp.asarray(f2, F32), jnp.asarray(fw, F32))
    blk = pl.BlockSpec((1, 1, seq, C_WIDTH), lambda b, g: (b, g, 0, 0))
    return pl.pallas_call(
        functools.partial(_fft_kernel, scale=float(1.0 / np.sqrt(seq * C_WIDTH)), n1=n1, n2=n2),
        out_shape=jax.ShapeDtypeStruct(u.shape, F32),
        grid=(bsz, ngroups),
        in_specs=[blk] + [_const_spec(c.shape) for c in consts],
        out_specs=blk,
        scratch_shapes=[pltpu.VMEM((n1 * (n2 + FFT_PAD), C_WIDTH), F32),
                        pltpu.VMEM((n2 * (2 * n1 + FFT_PAD), C_WIDTH), F32),
                        pltpu.VMEM((n2 * (n1 + FFT_PAD), C_WIDTH), F32)],
        compiler_params=_cparams("parallel", "parallel"),
        name="fft",
    )(u, *consts)


def _head_perm():
    rep = D_Q_HEADS // D_KV_HEADS
    cols = []
    for j in range(rep):
        for g in range(D_KV_HEADS):
            h = g * rep + j
            cols.extend(range(h * D_HEAD_DIM, (h + 1) * D_HEAD_DIM))
    return np.asarray(cols, np.int32)


def _rope_tables(seq):
    rows = seq // GRID_W
    row = jnp.repeat(jnp.arange(rows, dtype=F32), GRID_W)
    col = jnp.tile(jnp.arange(GRID_W, dtype=F32), rows)
    axis_dim = D_HEAD_DIM // 2
    inv_freq = jnp.power(ROPE_THETA, -jnp.arange(0, axis_dim, 2, dtype=F32) / axis_dim)
    ang_r = row[:, None] * inv_freq[None, :]
    ang_c = col[:, None] * inv_freq[None, :]
    cr, sr, cc, sc = jnp.cos(ang_r), jnp.sin(ang_r), jnp.cos(ang_c), jnp.sin(ang_c)
    cos = jnp.concatenate([cr, cr, cc, cc], axis=1)
    sin = jnp.concatenate([-sr, sr, -sc, sc], axis=1)
    reps = LANES // D_HEAD_DIM
    return jnp.tile(cos, (1, reps)), jnp.tile(sin, (1, reps))


def _inproj_qk_kernel(x_ref, gain_ref, sc_ref, sh_ref, w_ref, cos_ref, sin_ref, bd_h_ref, bd_l_ref,
                      gq_ref, gk_ref, u_ref, qo_ref, ko_ref, vo_ref):
    h = _rms(x_ref[0], gain_ref[...]) * (1.0 + sc_ref[0]) + sh_ref[0]
    z = _dot(h, w_ref[...])
    ngrp = u_ref.shape[1]
    for g in range(ngrp):
        u_ref[0, g] = z[:, g * LANES:(g + 1) * LANES]
    cos = cos_ref[...]
    sin = sin_ref[...]
    quarter = D_HEAD_DIM // 4
    lane = lax.broadcasted_iota(jnp.int32, (1, LANES), 1)
    first_of_pair = (lane // quarter) % 2 == 0

    def norm_rope(x, gain, scale):
        ms = _dot_tab(bd_h_ref[...], bd_l_ref[...], x * x, tab_left=False)
        xn = x * lax.rsqrt(ms + EPS) * gain
        partner = jnp.where(first_of_pair, pltpu.roll(xn, LANES - quarter, 1),
                            pltpu.roll(xn, quarter, 1))
        return ((xn * cos + partner * sin) * scale).astype(BF16)

    nq = qo_ref.shape[-1] // LANES
    for j in range(nq):
        qo_ref[0, :, j * LANES:(j + 1) * LANES] = norm_rope(
            z[:, (ngrp + j) * LANES:(ngrp + j + 1) * LANES], gq_ref[...], D_HEAD_DIM ** -0.5 * LOG2E)
    ko_ref[0] = norm_rope(z[:, (ngrp + nq) * LANES:(ngrp + nq + 1) * LANES], gk_ref[...], 1.0)
    vt = z[:, (ngrp + nq + 1) * LANES:(ngrp + nq + 2) * LANES].T
    ones = jnp.ones((FLASH_ONES, vt.shape[1]), F32)
    vo_ref[0] = jnp.concatenate(
        [piece for g in range(D_KV_HEADS)
         for piece in (vt[g * D_HEAD_DIM:(g + 1) * D_HEAD_DIM], ones)], axis=0).astype(BF16)


def _inproj_qk(x, gain, sc, sh, w_bf16, qk_norm_j, tm):
    bsz, seq, d = x.shape
    n = w_bf16.shape[1]
    qw = D_Q_HEADS * D_HEAD_DIM
    kw = D_KV_HEADS * D_HEAD_DIM
    assert kw == LANES and n == C_GROUPS * C_WIDTH + qw + 2 * kw
    vrows = D_KV_HEADS * (D_HEAD_DIM + FLASH_ONES)
    cos, sin = _rope_tables(seq)
    bd = np.kron(np.eye(LANES // D_HEAD_DIM), np.full((D_HEAD_DIM, D_HEAD_DIM), 1.0 / D_HEAD_DIM))
    bd_h, bd_l = _np_split2(bd)
    reps = LANES // D_HEAD_DIM
    gq = jnp.tile(qk_norm_j[0].astype(F32), reps).reshape(1, LANES)
    gk = jnp.tile(qk_norm_j[1].astype(F32), reps).reshape(1, LANES)
    tab = pl.BlockSpec((tm, LANES), lambda b, i: (i, 0))
    vec = pl.BlockSpec((1, 1, d), lambda b, i: (b, 0, 0))
    return pl.pallas_call(
        _inproj_qk_kernel,
        out_shape=(jax.ShapeDtypeStruct((bsz, C_GROUPS, seq, C_WIDTH), F32),
                   jax.ShapeDtypeStruct((bsz, seq, qw), BF16),
                   jax.ShapeDtypeStruct((bsz, seq, kw), BF16),
                   jax.ShapeDtypeStruct((bsz, vrows, seq), BF16)),
        grid=(bsz, seq // tm),
        in_specs=[pl.BlockSpec((1, tm, d), lambda b, i: (b, i, 0)),
                  _const_spec((1, d)), vec, vec, _const_spec((d, n)),
                  tab, tab, _const_spec(bd_h.shape), _const_spec(bd_l.shape),
                  _const_spec((1, LANES)), _const_spec((1, LANES))],
        out_specs=(pl.BlockSpec((1, C_GROUPS, tm, C_WIDTH), lambda b, i: (b, 0, i, 0)),
                   _row_spec(tm, qw), _row_spec(tm, kw),
                   pl.BlockSpec((1, vrows, tm), lambda b, i: (b, 0, i))),
        compiler_params=_cparams("parallel", "parallel"),
        name="inproj_qk",
    )(x, gain.reshape(1, d), sc.reshape(bsz, 1, d), sh.reshape(bsz, 1, d), w_bf16,
      cos, sin, bd_h, bd_l, gq, gk)


def _flash_kernel(q_ref, k_ref, vt_ref, o_ref, m_sc, acc_sc, s_sc):
    kv = pl.program_id(2)

    @pl.when(kv == 0)
    def _():
        m_sc[...] = jnp.full_like(m_sc, -jnp.inf)
        acc_sc[...] = jnp.zeros_like(acc_sc)

    lane = lax.broadcasted_iota(jnp.int32, (1, LANES), 1)
    lo = lane < D_HEAD_DIM
    nblk = q_ref.shape[-1] // LANES
    tq, tk = q_ref.shape[1], k_ref.shape[1]
    ku, qu = FLASH_KEY_UNIT, min(FLASH_QUERY_UNIT, tq)
    grows = D_HEAD_DIM + FLASH_ONES
    nheads = nblk * D_KV_HEADS
    k = k_ref[0]

    nchunk = tq // qu

    def logits_chunk(idx, c):
        j, g = divmod(idx, D_KV_HEADS)
        qj = q_ref[0, c * qu:(c + 1) * qu, j * LANES:(j + 1) * LANES]
        sel = lo if g == 0 else jnp.logical_not(lo)
        s = lax.dot_general(k, jnp.where(sel, qj, jnp.zeros_like(qj)), (((1,), (1,)), ((), ())),
                            preferred_element_type=F32)
        s_sc[idx % s_sc.shape[0], :, c * qu:(c + 1) * qu] = s
        return jnp.max(s, axis=0, keepdims=True)

    def finish_logits(idx, mcs):
        m_prev = m_sc[idx, 0:1, :]
        m_new = jnp.maximum(m_prev, jnp.concatenate(mcs, axis=1))
        m_sc[idx, 0:1, :] = m_new
        return m_new, jnp.exp2(m_prev - m_new)

    def value_chunk(idx, c, m_new, alpha):
        j, g = divmod(idx, D_KV_HEADS)
        rows = slice(g * grows, (g + 1) * grows)
        qcols = slice(c * qu, (c + 1) * qu)
        pv = None
        for u in range(tk // ku):
            keys = slice(u * ku, (u + 1) * ku)
            p = jnp.exp2(s_sc[idx % s_sc.shape[0], keys, qcols] - m_new[:, qcols])
            d = jnp.dot(vt_ref[0, rows, keys], p.astype(BF16), preferred_element_type=F32)
            pv = d if pv is None else pv + d
        acc_sc[j, rows, qcols] = alpha[:, qcols] * acc_sc[j, rows, qcols] + pv

    def logits_pass(idx):
        return finish_logits(idx, [logits_chunk(idx, c) for c in range(nchunk)])

    pending = [logits_pass(i) for i in range(min(FLASH_AHEAD, nheads))]
    for idx in range(nheads):
        if idx + FLASH_AHEAD < nheads:
            pending.append(logits_pass(idx + FLASH_AHEAD))
        stats = pending.pop(0)
        for c in range(nchunk):
            value_chunk(idx, c, *stats)

    @pl.when(kv == pl.num_programs(2) - 1)
    def _():
        for j in range(nblk):
            parts = []
            for g in range(D_KV_HEADS):
                num = acc_sc[j, g * grows:g * grows + D_HEAD_DIM, :]
                den = acc_sc[j, g * grows + D_HEAD_DIM:g * grows + D_HEAD_DIM + 1, :]
                parts.append(num / den)
            o_ref[0, :, j * LANES:(j + 1) * LANES] = (
                jnp.concatenate(parts, axis=0).T.astype(o_ref.dtype))


def _flash(q, k, vt, tq, tk):
    bsz, seq, qw = q.shape
    kw = k.shape[-1]
    vrows = vt.shape[1]
    assert FLASH_LOGIT_BUFS > FLASH_AHEAD
    return pl.pallas_call(
        _flash_kernel,
        out_shape=jax.ShapeDtypeStruct((bsz, seq, qw), BF16),
        grid=(bsz, seq // tq, seq // tk),
        in_specs=[pl.BlockSpec((1, tq, qw), lambda b, i, j: (b, i, 0)),
                  pl.BlockSpec((1, tk, kw), lambda b, i, j: (b, j, 0)),
                  pl.BlockSpec((1, vrows, tk), lambda b, i, j: (b, 0, j))],
        out_specs=pl.BlockSpec((1, tq, qw), lambda b, i, j: (b, i, 0)),
        scratch_shapes=[pltpu.VMEM((D_Q_HEADS, 8, tq), F32),
                        pltpu.VMEM((qw // LANES, vrows, tq), F32),
                        pltpu.VMEM((FLASH_LOGIT_BUFS, tk, tq), F32)],
        compiler_params=_cparams("parallel", "parallel", "arbitrary"),
        name="flash",
    )(q, k, vt)


def kernel(x, c, t5_bias, hgrn_lb_logits, ada_w, ada_b, norm_gains, ab_w_in, ab_w_out,
           hgrn_out_norm, cd_w_in, cd_w_out, qk_norm, ffn_w_in, ffn_w_out):
    bsz, seq, d = x.shape
    depth = ada_w.shape[0]
    mod = _ada_mod(c.astype(F32), ada_w, ada_b)
    perm = _head_perm()
    aw = A_HEADS * A_HEAD_DIM
    cw = C_GROUPS * C_WIDTH
    qw = D_Q_HEADS * D_HEAD_DIM
    tm_in = min(512, seq)
    ffn_wi = ffn_w_in.astype(BF16)
    ffn_wo = ffn_w_out.astype(BF16)
    for layer in range(depth):
        sh_m, sc_m, g_m, sh_f, sc_f, g_f = [mod[layer, :, i * d:(i + 1) * d] for i in range(6)]
        gains = norm_gains[layer]
        j = layer // 2
        tail = dict(x=x, gain_m=gains[1], gate_m=g_m, g1=gains[2], sc=sc_f, sh=sh_f,
                    wi_bf16=ffn_wi, wo_bf16=ffn_wo, layer=layer, g2=gains[3], gate_f=g_f)
        if layer % 2 == 0:
            w_in = ab_w_in[j].astype(BF16)
            *qkv_cm, z = _inproj_cm(x, gains[0], sc_m, sh_m, w_in, 3 * aw, tm_in)
            branches = [_dilated_branch(cm, t5_bias, window, dil)
                        for cm, (window, dil) in zip(qkv_cm, DIL_CFG)]
            o_f, o_b = _hgrn(z, hgrn_lb_logits, layer, 0)
            g_col = 3 * B_HEADS * B_DK + B_HEADS * B_DV
            x = _tail0(branches, o_f, o_b, z, g_col, hgrn_out_norm[j], ab_w_out[j].astype(BF16),
                       tm=min(TAIL_SUB, seq), **tail)
        else:
            w_full = cd_w_in[j]
            w_in = jnp.concatenate([w_full[:, :cw], w_full[:, cw:cw + qw][:, perm],
                                    w_full[:, cw + qw:]], axis=1).astype(BF16)
            u, qn, kn, vn = _inproj_qk(x, gains[0], sc_m, sh_m, w_in, qk_norm[j], min(1024, seq))
            c_out = _fourier_mixer(u)
            d_out = _flash(qn, kn, vn, min(FLASH_TQ, seq), min(FLASH_TK, seq))
            w_out_full = cd_w_out[j]
            w_out = jnp.concatenate([w_out_full[:cw], w_out_full[cw:][perm]], axis=0).astype(BF16)
            x = _tail1(c_out, d_out, w_out, tm=min(TAIL_BLOCK, seq), **tail)
    return x
```

```python
import functools

import numpy as np
import jax
import jax.numpy as jnp
from jax import lax
from jax.experimental import pallas as pl
from jax.experimental.pallas import tpu as pltpu

F32 = jnp.float32
BF16 = jnp.bfloat16
LANES = 128
MXU_WIDTH = 256
VMEM_LIMIT_BYTES = 56 * 2**20
NEG_INF = -1e30
EPS = 1e-6

GRID_W = 64
A_HEADS = 8
A_HEAD_DIM = 64
DIL_CFG = ((128, 1), (512, 4), (2048, 16))
N_BUCKETS = 32
T5_MAX_DIST = 1024
B_HEADS = 4
B_DK = 128
B_DV = 128
C_GROUPS = 4
C_WIDTH = 128
D_Q_HEADS = 8
D_KV_HEADS = 2
D_HEAD_DIM = 64
ROPE_THETA = 10000.0

DIL_TQ = 128
DIL_TILE = 512
HGRN_BLOCK = 1024
HGRN_SUB = 256
HGRN_BCAST_MIN = 8
FLASH_TQ = 1024
FLASH_TK = 4096
FLASH_LOGIT_BUFS = 2
FLASH_KEY_UNIT = 512
FLASH_QUERY_UNIT = 1024
FLASH_AHEAD = 1
FLASH_ONES = 16
TAIL_BLOCK = 512
TAIL_SUB = 512
FFT_N2 = 128
FFT_BATCH = 8
FFT_PAD = 8
LOG2E = 1.4426950408889634


def _cparams(*sem):
    return pltpu.CompilerParams(dimension_semantics=sem, vmem_limit_bytes=VMEM_LIMIT_BYTES)


def _const_spec(shape):
    nd = len(shape)
    return pl.BlockSpec(shape, lambda *_: (0,) * nd, pipeline_mode=pl.Buffered(1))


def _sigmoid(x):
    return 1.0 / (1.0 + jnp.exp(-x))


def _dot(a, b):
    return jnp.dot(a.astype(BF16), b.astype(BF16), preferred_element_type=F32)


def _dot_nt(a, b):
    return lax.dot_general(a.astype(BF16), b.astype(BF16), (((1,), (1,)), ((), ())),
                           preferred_element_type=F32)


def _split2(a):
    hi = a.astype(BF16)
    lo = (a - hi.astype(F32)).astype(BF16)
    return hi, lo


def _dot_tab(tab_hi, tab_lo, x, *, tab_left):
    x_hi, x_lo = _split2(x)
    if tab_left:
        d = lambda t, v: jnp.dot(t, v, preferred_element_type=F32)
    else:
        d = lambda t, v: jnp.dot(v, t, preferred_element_type=F32)
    return d(tab_hi, x_hi) + (d(tab_hi, x_lo) + d(tab_lo, x_hi))


def _rms(x, gain):
    ms = jnp.mean(x * x, axis=-1, keepdims=True)
    return x * lax.rsqrt(ms + EPS) * gain


def _np_split2(t):
    t = np.asarray(t, np.float32)
    hi = jnp.asarray(t, F32).astype(BF16)
    lo = (jnp.asarray(t, F32) - hi.astype(F32)).astype(BF16)
    return hi, lo


def _mod_kernel(c_ref, w_ref, b_ref, o_ref):
    c = c_ref[...]
    o_ref[0] = _dot(c * _sigmoid(c), w_ref[0]) + b_ref[0]


def _ada_mod(c, ada_w, ada_b):
    depth, d, n6 = ada_w.shape
    bsz = c.shape[0]
    rows = 8
    cp = jnp.zeros((rows, d), F32).at[:bsz].set(c)
    tn = n6 // 4
    out = pl.pallas_call(
        _mod_kernel,
        out_shape=jax.ShapeDtypeStruct((depth, rows, n6), F32),
        grid=(depth, n6 // tn),
        in_specs=[pl.BlockSpec((rows, d), lambda l, j: (0, 0)),
                  pl.BlockSpec((1, d, tn), lambda l, j: (l, 0, j)),
                  pl.BlockSpec((1, 1, tn), lambda l, j: (l, 0, j))],
        out_specs=pl.BlockSpec((1, rows, tn), lambda l, j: (l, 0, j)),
        compiler_params=_cparams("parallel", "parallel"),
        name="ada_mod",
    )(cp, ada_w, ada_b.reshape(depth, 1, n6))
    return out[:, :bsz]


def _inproj_cm_kernel(x_ref, gain_ref, sc_ref, sh_ref, w_ref, *refs):
    cm_refs, rest_ref, zs_sc, zc_sc = refs[:-3], refs[-3], refs[-2], refs[-1]
    h = _rms(x_ref[0], gain_ref[...]) * (1.0 + sc_ref[0]) + sh_ref[0]
    z = _dot(h, w_ref[...])
    nblk, tm, _ = zs_sc.shape
    rest_ref[0] = z[:, nblk * LANES:]
    nq = A_HEADS * A_HEAD_DIM // LANES
    for c in range(nblk):
        blk = z[:, c * LANES:(c + 1) * LANES]
        zs_sc[c] = blk * (A_HEAD_DIM ** -0.5 * LOG2E) if c < nq else blk
    src, sd = zs_sc, 1
    for level, (cm_ref, (_, dil)) in enumerate(zip(cm_refs, DIL_CFG)):
        step, n = dil // sd, tm // dil
        keep = dil > 1 and level + 1 < len(DIL_CFG)
        for rs in range(sd):
            for cc in range(step):
                r = rs + sd * cc
                for c in range(nblk):
                    rows = src[c, pl.ds(rs * (tm // sd) + cc, n, stride=step), :]
                    cm_ref[0, r, :, c * LANES:(c + 1) * LANES] = rows.astype(BF16)
                    if keep:
                        zc_sc[c, r * n:(r + 1) * n, :] = rows
        if keep:
            src, sd = zc_sc, dil


def _inproj_cm(x, gain, sc, sh, w_bf16, na, tm):
    bsz, seq, d = x.shape
    n = w_bf16.shape[1]
    vec = pl.BlockSpec((1, 1, d), lambda b, i: (b, 0, 0))
    dils = [dl for _, dl in DIL_CFG]
    assert dils[0] == 1 and all(b % a == 0 for a, b in zip(dils, dils[1:]))
    return pl.pallas_call(
        _inproj_cm_kernel,
        out_shape=tuple([jax.ShapeDtypeStruct((bsz, dl, seq // dl, na), BF16) for dl in dils]
                        + [jax.ShapeDtypeStruct((bsz, seq, n - na), F32)]),
        grid=(bsz, seq // tm),
        in_specs=[pl.BlockSpec((1, tm, d), lambda b, i: (b, i, 0)),
                  _const_spec((1, d)), vec, vec, _const_spec((d, n))],
        out_specs=tuple([pl.BlockSpec((1, dl, tm // dl, na), lambda b, i: (b, 0, i, 0)) for dl in dils]
                        + [pl.BlockSpec((1, tm, n - na), lambda b, i: (b, i, 0))]),
        scratch_shapes=[pltpu.VMEM((na // LANES, tm, LANES), F32),
                        pltpu.VMEM((na // LANES, tm, LANES), F32)],
        compiler_params=_cparams("parallel", "parallel"),
        name="inproj_cm",
    )(x, gain.reshape(1, d), sc.reshape(bsz, 1, d), sh.reshape(bsz, 1, d), w_bf16)


def _t5_buckets(rel):
    half = N_BUCKETS // 2
    max_exact = half // 2
    n = np.abs(rel)
    large = max_exact + (np.log(np.maximum(n, 1) / max_exact) / np.log(T5_MAX_DIST / max_exact)
                         * (half - max_exact)).astype(np.int32)
    large = np.minimum(large, half - 1)
    return (np.where(rel > 0, half, 0) + np.where(n < max_exact, n, large)).astype(np.int32)


def _dil_bias(t5_bias, window, dil, tq):
    half = (window // 2) // dil
    assert half == tq // 2
    rel = np.arange(2 * tq)[None, :] - half - np.arange(tq)[:, None]
    inside = np.abs(rel) <= half
    buckets = _t5_buckets(np.where(inside, rel, 0) * dil)
    onehot =jnp.asarray(np.eye(N_BUCKETS, dtype=np.float32)[buckets])
    bias = jnp.einsum("qkn,nh->hqk", onehot, t5_bias.astype(F32), precision=lax.Precision.HIGHEST)
    return jnp.where(jnp.asarray(inside)[None], bias * LOG2E, NEG_INF)


def _dil_kernel(q_ref, kp_ref, kc_ref, kn_ref, vp_ref, vc_ref, vn_ref, bias_ref, o_ref, lse_ref,
                *, class_len):
    i = pl.program_id(2)
    sub, hq, tile = DIL_TQ, DIL_TQ // 2, q_ref.shape[2]
    kwin = jnp.concatenate([kp_ref[0, 0], kc_ref[0, 0], kn_ref[0, 0]], axis=0)
    vwin = jnp.concatenate([vp_ref[0, 0], vc_ref[0, 0], vn_ref[0, 0]], axis=0)
    lane = lax.broadcasted_iota(jnp.int32, (1, LANES), 1)
    lo = lane < A_HEAD_DIM
    nblk = A_HEADS // 2
    units = [(jt, j) for jt in range(tile // sub) for j in range(nblk)]
    logits = []
    for jt, j in units:
        cols = slice(j * LANES, (j + 1) * LANES)
        qj = q_ref[0, 0, jt * sub:(jt + 1) * sub, cols]
        zero = jnp.zeros_like(qj)
        q2 = jnp.concatenate([jnp.where(lo, qj, zero), jnp.where(lo, zero, qj)], axis=0)
        s = lax.dot_general(q2, kwin[jt * sub:jt * sub + 2 * sub, cols], (((1,), (1,)), ((), ())),
                            preferred_element_type=F32)
        kpos = i * tile + jt * sub - hq + lax.broadcasted_iota(jnp.int32, (1, 2 * sub), 1)
        valid = jnp.logical_and(kpos >= 0, kpos < class_len)
        logits.append(jnp.where(valid, s + bias_ref[j], NEG_INF))
    s_all = jnp.concatenate(logits, axis=0)
    m = jnp.max(s_all, axis=-1, keepdims=True)
    p32 = jnp.exp2(s_all - m)
    l = jnp.sum(p32, axis=-1, keepdims=True)
    p = p32.astype(BF16)
    rinv = 1.0 / l
    lse = m + jnp.log2(l)
    for jt in range(tile // sub):
        lse_all = jnp.zeros((sub, LANES), F32)
        for j in range(nblk):
            cols = slice(j * LANES, (j + 1) * LANES)
            r0 = (jt * nblk + j) * 2 * sub
            o2 = jnp.dot(p[r0:r0 + 2 * sub], vwin[jt * sub:jt * sub + 2 * sub, cols],
                         preferred_element_type=F32) * rinv[r0:r0 + 2 * sub]
            lse_all = jnp.where(lane == 2 * j, lse[r0:r0 + sub], lse_all)
            lse_all = jnp.where(lane == 2 * j + 1, lse[r0 + sub:r0 + 2 * sub], lse_all)
            o_ref[0, 0, jt * sub:(jt + 1) * sub, cols] = (
                jnp.where(lo, o2[:sub], o2[sub:]).astype(o_ref.dtype))
        lse_ref[0, 0, jt * sub:(jt + 1) * sub, :] = lse_all


def _dilated_branch(qkv_cm, t5_bias, window, dil):
    bsz, _, cl, width = qkv_cm.shape
    aw = A_HEADS * A_HEAD_DIM
    tile, hq = min(DIL_TILE, cl), DIL_TQ // 2
    nt = cl // tile
    per = tile // hq
    nh = cl // hq

    def cur(col):
        return pl.BlockSpec((1, 1, tile, aw), lambda b, r, i: (b, r, i, col))

    def prev(col):
        return pl.BlockSpec((1, 1, hq, aw), lambda b, r, i: (b, r, jnp.maximum(i * per - 1, 0), col))

    def nxt(col):
        return pl.BlockSpec((1, 1, hq, aw),
                            lambda b, r, i: (b, r, jnp.minimum((i + 1) * per, nh - 1), col))

    return pl.pallas_call(
        functools.partial(_dil_kernel, class_len=cl),
        out_shape=(jax.ShapeDtypeStruct((bsz, dil, cl, aw), BF16),
                   jax.ShapeDtypeStruct((bsz, dil, cl, LANES), F32)),
        grid=(bsz, dil, nt),
        in_specs=[cur(0), prev(1), cur(1), nxt(1), prev(2), cur(2), nxt(2),
                  _const_spec((A_HEADS // 2, 2 * DIL_TQ, 2 * DIL_TQ))],
        out_specs=(pl.BlockSpec((1, 1, tile, aw), lambda b, r, i: (b, r, i, 0)),
                   pl.BlockSpec((1, 1, tile, LANES), lambda b, r, i: (b, r, i, 0))),
        compiler_params=_cparams("parallel", "parallel", "parallel"),
        name=f"dilated_d{dil}",
    )(*([qkv_cm] * 7),
      _dil_bias(t5_bias, window, dil, DIL_TQ).reshape(A_HEADS // 2, 2 * DIL_TQ, 2 * DIL_TQ))


def _merge_branches(a_refs, l_refs, e_ref, a_scs, l_scs, sub):
    tm = l_scs[0].shape[0]
    accs, lses = [], []
    for (_, dil), a_ref, l_ref, a_sc, l_sc in zip(DIL_CFG, a_refs, l_refs, a_scs, l_scs):
        nblk = a_sc.shape[0]
        for r in range(dil):
            rows = pl.ds(r, tm // dil, stride=dil)
            src = pl.ds(sub * (tm // dil), tm // dil)
            for c in range(nblk):
                a_sc[c, rows, :] = a_ref[0, r, src, c * LANES:(c + 1) * LANES].astype(F32)
            l_sc[rows, :] = l_ref[0, r, src, :]
        accs.append(jnp.concatenate([a_sc[c] for c in range(nblk)], axis=1))
        lses.append(l_sc[...])
    mx = functools.reduce(jnp.maximum, lses)
    ws = [jnp.exp2(x - mx) for x in lses]
    tot = functools.reduce(jnp.add, ws)
    out = None
    for w, a in zip(ws, accs):
        w_hi, w_lo = _split2(w / tot)
        wide = (jnp.dot(w_hi, e_ref[...], preferred_element_type=F32)
                + jnp.dot(w_lo, e_ref[...], preferred_element_type=F32))
        out = wide * a if out is None else out + wide * a
    return out


def _merge_operands(branch_outs, tm):
    aw = branch_outs[0][0].shape[-1]
    expand = np.zeros((LANES, aw), np.float32)
    for h in range(A_HEADS):
        expand[h, h * A_HEAD_DIM:(h + 1) * A_HEAD_DIM] = 1.0
    dils = [d for _, d in DIL_CFG]
    specs = ([pl.BlockSpec((1, d, tm // d, aw), lambda b, i: (b, 0, i, 0)) for d in dils]
             + [pl.BlockSpec((1, d, tm // d, LANES), lambda b, i: (b, 0, i, 0)) for d in dils]
             + [_const_spec(expand.shape)])
    arrays = [o for o, _ in branch_outs] + [l for _, l in branch_outs] + [jnp.asarray(expand, BF16)]
    scratch = ([pltpu.VMEM((aw // LANES, TAIL_SUB, LANES), F32) for _ in dils]
               + [pltpu.VMEM((TAIL_SUB, LANES), F32) for _ in dils])
    return arrays, specs, scratch


def _hgrn_consts(reverse):
    t = HGRN_SUB
    r = np.arange(t)
    u = r[None, :]
    row = r[:, None]
    nmats = [(u >= row) if reverse else (u <= row)]
    masks = []
    m = t // 2
    while m >= 1:
        grp = r // (2 * m)
        in_first = (r % (2 * m)) < m
        same = grp[:, None] == grp[None, :]
        if reverse:
            beta = (grp * 2 * m + m)[:, None]
            n = np.where(in_first[:, None], (u >= row) & (u < beta), (u >= beta) & (u < row))
            mask = same & in_first[:, None] & ~in_first[None, :]
        else:
            beta = (grp * 2 * m + m - 1)[:, None]
            n = np.where(in_first[:, None], (u > row) & (u <= beta), (u > beta) & (u <= row))
            mask = same & ~in_first[:, None] & in_first[None, :]
        if m < HGRN_BCAST_MIN:
            nmats.append(n)
        masks.append(mask)
        m //= 2
    masks.append(np.eye(t, dtype=bool))
    nmat = jnp.asarray(np.concatenate(nmats, axis=0), F32).astype(BF16)
    return nmat, jnp.asarray(np.stack(masks), F32)


def _hgrn_block(q, f, v, lb, st, nmat_ref, mask_ref, reverse):
    t = HGRN_SUB
    nlev = mask_ref.shape[0] - 1
    qs = q * (B_DK ** -0.5)
    fa = lb + (1.0 - lb) * _sigmoid(f)
    kk = 1.0 - fa
    g_hi, g_lo = _split2(jnp.log2(fa))
    ex = jnp.dot(nmat_ref[...], jnp.concatenate([g_hi, g_lo], axis=1), preferred_element_type=F32)
    ex = ex[:, :B_DK] + ex[:, B_DK:]
    b = ex[:t]
    btot = b[0:1] if reverse else b[t - 1:t]
    a = mask_ref[nlev] * _dot_nt(qs, kk)
    fine = 1
    for l in range(nlev):
        m = t >> (l + 1)
        if m >= HGRN_BCAST_MIN:
            ref = jnp.concatenate(
                [jnp.broadcast_to(b[beta:beta + 1], (2 * m, B_DK))
                 for beta in range(m if reverse else m - 1, t, 2 * m)], axis=0)
            e = jnp.exp2(-jnp.abs(b - ref))
        else:
            e = jnp.exp2(ex[fine * t:(fine + 1) * t])
            fine += 1
        a = a + mask_ref[l] * _dot_nt(qs * e, kk * e)
    out = _dot(a, v) + _dot_nt(qs * jnp.exp2(b), st)
    khat = (kk * jnp.exp2(btot - b)).astype(BF16)
    st_new = st * jnp.exp2(btot) + jnp.dot(v.T.astype(BF16), khat, preferred_element_type=F32)
    return out, st_new


def _hgrn_lb(lg_ref, layer):
    lg = [lg_ref[l, 0] for l in range(lg_ref.shape[0])]
    mx = functools.reduce(jnp.maximum, lg)
    e = [jnp.exp(x - mx) for x in lg]
    return functools.reduce(jnp.add, e[:layer + 1]) / functools.reduce(jnp.add, e)


def _hgrn_kernel(qf_ref, ff_ref, vf_ref, qb_ref, fb_ref, vb_ref, lgf_ref, lgb_ref,
                 nf_ref, mf_ref, nb_ref, mb_ref, of_ref, ob_ref, sf_sc, sb_sc, *, layer):
    @pl.when(pl.program_id(2) == 0)
    def _():
        sf_sc[...] = jnp.zeros_like(sf_sc)
        sb_sc[...] = jnp.zeros_like(sb_sc)

    nsub = HGRN_BLOCK // HGRN_SUB
    chains = ((qf_ref, ff_ref, vf_ref, lgf_ref, sf_sc, nf_ref, mf_ref, of_ref, False),
              (qb_ref, fb_ref, vb_ref, lgb_ref, sb_sc, nb_ref, mb_ref, ob_ref, True))
    for q_ref, f_ref, v_ref, lg_ref, st_sc, n_ref, m_ref, o_ref, reverse in chains:
        lb = _hgrn_lb(lg_ref, layer)
        st = st_sc[...]
        for sub in (reversed(range(nsub)) if reverse else range(nsub)):
            rows = pl.ds(sub * HGRN_SUB, HGRN_SUB)
            o, st = _hgrn_block(q_ref[0, rows, :], f_ref[0, rows, :], v_ref[0, rows, :], lb, st,
                                n_ref, m_ref, reverse)
            o_ref[0, rows, :] = o.astype(o_ref.dtype)
        st_sc[...] = st


def _hgrn(z, lb_logits, layer, col0):
    bsz, seq, _ = z.shape
    t = HGRN_BLOCK
    nb = seq // t
    c0 = col0 // LANES
    hw = B_HEADS

    def zspec(group, rev):
        return pl.BlockSpec(
            (1, t, LANES),
            lambda b, h, j: (b, (nb - 1 - j) if rev else j, c0 + group * hw + h))

    def lgspec(direction):
        return pl.BlockSpec((lb_logits.shape[0], 1, 1, LANES),
                            lambda b, h, j: (0, direction * hw + h, 0, 0))

    nf, mf = _hgrn_consts(False)
    nbw, mbw = _hgrn_consts(True)
    lg = lb_logits.astype(F32).reshape(lb_logits.shape[0], 2 * hw, 1, LANES)
    o_shape = jax.ShapeDtypeStruct((bsz, seq, hw * B_DV), BF16)
    return pl.pallas_call(
        functools.partial(_hgrn_kernel, layer=layer),
        out_shape=(o_shape, o_shape),
        grid=(bsz, hw, nb),
        in_specs=[zspec(0, False), zspec(1, False), zspec(3, False),
                  zspec(0, True), zspec(2, True), zspec(3, True),
                  lgspec(0), lgspec(1),
                  _const_spec(nf.shape), _const_spec(mf.shape),
                  _const_spec(nbw.shape), _const_spec(mbw.shape)],
        out_specs=(pl.BlockSpec((1, t, LANES), lambda b, h, j: (b, j, h)),
                   pl.BlockSpec((1, t, LANES), lambda b, h, j: (b, nb - 1 - j, h))),
        scratch_shapes=[pltpu.VMEM((B_DV, B_DK), F32), pltpu.VMEM((B_DV, B_DK), F32)],
        compiler_params=_cparams("parallel", "parallel", "arbitrary"),
        name="hgrn",
    )(z, z, z, z, z, z, lg, lg, nf, mf, nbw, mbw)


def _mix0_y(sub, *refs):
    nbr = len(DIL_CFG)
    a_refs, l_refs, e_ref = refs[:nbr], refs[nbr:2 * nbr], refs[2 * nbr]
    of_ref, ob_ref, g_ref, on_ref, w_ref = refs[2 * nbr + 1:2 * nbr + 6]
    scratch = refs[2 * nbr + 6:]
    a = _merge_branches(a_refs, l_refs, e_ref, scratch[:nbr], scratch[nbr:], sub)
    rows = pl.ds(sub * TAIL_SUB, TAIL_SUB)
    o = of_ref[0, rows, :].astype(F32) + ob_ref[0, rows, :].astype(F32)
    g = g_ref[0, rows, :]
    parts = [_rms(o[:, h * B_DV:(h + 1) * B_DV], on_ref[...]) for h in range(B_HEADS)]
    bn = jnp.concatenate(parts, axis=-1) * (g * _sigmoid(g))
    na = a.shape[-1]
    return _dot(a, w_ref[:na, :]) + _dot(bn, w_ref[na:, :])


def _mix1_y(sub, c_ref, d_ref, w_ref):
    rows = pl.ds(sub * TAIL_SUB, TAIL_SUB)
    c = jnp.concatenate([c_ref[0, g, rows, :] for g in range(c_ref.shape[1])], axis=1)
    nc = c.shape[-1]
    return _dot(c, w_ref[:nc, :]) + _dot(d_ref[0, rows, :], w_ref[nc:, :])


def _row_spec(tm, width, col=0):
    return pl.BlockSpec((1, tm, width), lambda b, i: (b, i, col))


def _tail_kernel(*refs, nmix, mix_fn, bounds):
    (x_ref, gm_ref, gatem_ref, g1_ref, sc_ref, sh_ref, wi_ref, wo_ref, g2_ref, gatef_ref,
     o_ref) = refs[nmix:nmix + 11]
    hidden = wo_ref.shape[0]
    for sub in range(x_ref.shape[1] // TAIL_SUB):
        rows = pl.ds(sub * TAIL_SUB, TAIL_SUB)
        y_mix = mix_fn(sub, *refs[:nmix], *refs[nmix + 11:])
        x1 = x_ref[0, rows, :] + gatem_ref[0] * _rms(y_mix, gm_ref[...])
        h = (_rms(x1, g1_ref[...]) * (1.0 + sc_ref[0]) + sh_ref[0]).astype(BF16)
        y = None
        for c0, c1 in zip(bounds, bounds[1:]):
            gt = jnp.dot(h, wi_ref[:, c0:c1], preferred_element_type=F32)
            up = jnp.dot(h, wi_ref[:, hidden + c0:hidden + c1], preferred_element_type=F32)
            part = _dot(gt * _sigmoid(gt) * up, wo_ref[c0:c1, :])
            y = part if y is None else y + part
        o_ref[0, rows, :] = x1 + gatef_ref[0] * _rms(y, g2_ref[...])


def _layer_tail(mix_fn, mix_args, mix_specs, x, gain_m, gate_m, g1, sc, sh, wi_bf16, wo_bf16, layer,
                g2, gate_f, tm, name, mix_scratch=()):
    bsz, seq, d = x.shape
    vec = pl.BlockSpec((1, 1, d), lambda b, i: (b, 0, 0))
    hidden = wo_bf16.shape[1]

    def layer_slab(w):
        return pl.BlockSpec((None,) + w.shape[1:], lambda b, i: (layer, 0, 0),
                            pipeline_mode=pl.Buffered(1))

    ntile = hidden // MXU_WIDTH
    assert ntile * MXU_WIDTH == hidden
    bounds = (0, (ntile + 1) // 2 * MXU_WIDTH, hidden)
    row = lambda v: v.reshape(1, d)
    per_batch = lambda v: v.reshape(bsz, 1, d)
    return pl.pallas_call(
        functools.partial(_tail_kernel, nmix=len(mix_args), mix_fn=mix_fn, bounds=bounds),
        out_shape=jax.ShapeDtypeStruct(x.shape, F32),
        grid=(bsz, seq // tm),
        in_specs=list(mix_specs) + [_row_spec(tm, d), _const_spec((1, d)), vec, _const_spec((1, d)),
                                    vec, vec, layer_slab(wi_bf16), layer_slab(wo_bf16),
                                    _const_spec((1, d)), vec],
        out_specs=_row_spec(tm, d),
        scratch_shapes=list(mix_scratch),
        compiler_params=_cparams("parallel", "parallel"),
        name=name,
    )(*mix_args, x, row(gain_m), per_batch(gate_m), row(g1), per_batch(sc), per_batch(sh),
      wi_bf16, wo_bf16, row(g2), per_batch(gate_f))


def _tail0(branches, o_f, o_b, z, g_col, out_norm, w_bf16, tm, **kw):
    wv = B_HEADS * B_DV
    m_arrays, m_specs, m_scratch = _merge_operands(branches, tm)
    specs = m_specs + [_row_spec(tm, wv), _row_spec(tm, wv), _row_spec(tm, wv, g_col // wv),
                       _const_spec((1, B_DV)), _const_spec(w_bf16.shape)]
    args = m_arrays + [o_f, o_b, z, out_norm.reshape(1, B_DV), w_bf16]
    return _layer_tail(_mix0_y, args, specs, tm=tm, name="tail0", mix_scratch=m_scratch, **kw)


def _tail1(c_out, d_out, w_bf16, tm, **kw):
    specs = [pl.BlockSpec((1, c_out.shape[1], tm, c_out.shape[3]), lambda b, i: (b, 0, i, 0)),
             _row_spec(tm, d_out.shape[-1]), _const_spec(w_bf16.shape)]
    return _layer_tail(_mix1_y, (c_out, d_out, w_bf16), specs, tm=tm, name="tail1", **kw)


def _fft_kernel(u_ref, f1_ref, twc_ref, tws_ref, f2_ref, fw_ref, o_ref, u_sc, p_sc, y_sc,
                *, scale, n1, n2):
    pu = n2 + FFT_PAD
    pp = 2 * n1 + FFT_PAD
    py = n1 + FFT_PAD
    f1 = f1_ref[...].astype(BF16)
    f2 = f2_ref[...].astype(BF16)
    fw = fw_ref[...].astype(BF16)
    for i1 in range(n1):
        u_sc[i1 * pu:i1 * pu + n2, :] = u_ref[0, 0, i1 * n2:(i1 + 1) * n2, :]

    nb = FFT_BATCH

    def stage1(blk, carry):
        i2s = [blk * nb + j for j in range(nb)]
        x = jnp.concatenate([u_sc[pl.ds(i2, n1, stride=pu), :] for i2 in i2s], axis=1)
        p = jnp.dot(f1, x.astype(BF16), preferred_element_type=F32)
        for j, i2 in enumerate(i2s):
            p_sc[pl.ds(pl.multiple_of(i2 * pp, 8), 2 * n1), :] = p[:, j * C_WIDTH:(j + 1) * C_WIDTH]
        return carry

    lax.fori_loop(0, n2 // nb, stage1, 0, unroll=2)

    def stage2(blk, carry):
        k1s = [blk * nb + j for j in range(nb)]
        qr, qi = [], []
        for k1 in k1s:
            tc = twc_ref[k1]
            ts = tws_ref[k1]
            pr = p_sc[pl.ds(k1, n2, stride=pp), :]
            pim = p_sc[pl.ds(n1 + k1, n2, stride=pp), :]
            qr.append(pr * tc + pim * ts)
            qi.append(pim * tc - pr * ts)
        q = jnp.concatenate([jnp.concatenate(qr, axis=1), jnp.concatenate(qi, axis=1)], axis=0)
        xx = jnp.dot(f2, q.astype(BF16), preferred_element_type=F32)
        xg = jnp.concatenate(
            [jnp.concatenate([xx[:n2, j * C_WIDTH:(j + 1) * C_WIDTH],
                              xx[n2:, j * C_WIDTH:(j + 1) * C_WIDTH]], axis=1) for j in range(nb)],
            axis=0)
        y = jnp.dot(xg.astype(BF16), fw, preferred_element_type=F32) * scale
        for j, k1 in enumerate(k1s):
            y_sc[pl.ds(k1, n2, stride=py), :] = y[j * n2:(j + 1) * n2]
        return carry

    lax.fori_loop(0, n1 // nb, stage2, 0, unroll=2)
    for k2 in range(n2):
        o_ref[0, 0, k2 * n1:(k2 + 1) * n1, :] = y_sc[k2 * py:k2 * py + n1, :]


def _fourier_mixer(u):
    bsz, ngroups, seq, width = u.shape
    n2 = FFT_N2
    n1 = seq // n2
    assert n1 * n2 == seq and width == C_WIDTH and n1 % 8 == 0
    a1 = 2.0 * np.pi * np.outer(np.arange(n1), np.arange(n1)) / n1
    f1 = np.concatenate([np.cos(a1), -np.sin(a1)], axis=0)
    a2 = 2.0 * np.pi * np.outer(np.arange(n2), np.arange(n2)) / n2
    c2, s2 = np.cos(a2), np.sin(a2)
    f2 = np.block([[c2, s2], [-s2, c2]])
    aw = 2.0 * np.pi * np.outer(np.arange(C_WIDTH), np.arange(C_WIDTH)) / C_WIDTH
    fw = np.concatenate([np.cos(aw), np.sin(aw)], axis=0)
    at = np.repeat((2.0 * np.pi * np.outer(np.arange(n1), np.arange(n2)) / seq)[:, :, None],
                   C_WIDTH, axis=2)
    consts = (jnp.asarray(f1, F32), jnp.asarray(np.cos(at), F32), jnp.asarray(np.sin(at), F32),
              jnp.asarray(f2, F32), jnp.asarray(fw, F32))
    blk = pl.BlockSpec((1, 1, seq, C_WIDTH), lambda b, g: (b, g, 0, 0))
    return pl.pallas_call(
        functools.partial(_fft_kernel, scale=float(1.0 / np.sqrt(seq * C_WIDTH)), n1=n1, n2=n2),
        out_shape=jax.ShapeDtypeStruct(u.shape, F32),
        grid=(bsz, ngroups),
        in_specs=[blk] + [_const_spec(c.shape) for c in consts],
        out_specs=blk,
        scratch_shapes=[pltpu.VMEM((n1 * (n2 + FFT_PAD), C_WIDTH), F32),
                        pltpu.VMEM((n2 * (2 * n1 + FFT_PAD), C_WIDTH), F32),
                        pltpu.VMEM((n2 * (n1 + FFT_PAD), C_WIDTH), F32)],
        compiler_params=_cparams("parallel", "parallel"),
        name="fft",
    )(u, *consts)


def _head_perm():
    rep = D_Q_HEADS // D_KV_HEADS
    cols = []
    for j in range(rep):
        for g in range(D_KV_HEADS):
            h = g * rep + j
            cols.extend(range(h * D_HEAD_DIM, (h + 1) * D_HEAD_DIM))
    return np.asarray(cols, np.int32)


def _rope_tables(seq):
    rows = seq // GRID_W
    row = jnp.repeat(jnp.arange(rows, dtype=F32), GRID_W)
    col = jnp.tile(jnp.arange(GRID_W, dtype=F32), rows)
    axis_dim = D_HEAD_DIM // 2
    inv_freq = jnp.power(ROPE_THETA, -jnp.arange(0, axis_dim, 2, dtype=F32) / axis_dim)
    ang_r = row[:, None] * inv_freq[None, :]
    ang_c = col[:, None] * inv_freq[None, :]
    cr, sr, cc, sc = jnp.cos(ang_r), jnp.sin(ang_r), jnp.cos(ang_c), jnp.sin(ang_c)
    cos = jnp.concatenate([cr, cr, cc, cc], axis=1)
    sin = jnp.concatenate([-sr, sr, -sc, sc], axis=1)
    reps = LANES // D_HEAD_DIM
    return jnp.tile(cos, (1, reps)), jnp.tile(sin, (1, reps))


def _inproj_qk_kernel(x_ref, gain_ref, sc_ref, sh_ref, w_ref, cos_ref, sin_ref, bd_h_ref, bd_l_ref,
                      gq_ref, gk_ref, u_ref, qo_ref, ko_ref, vo_ref):
    h = _rms(x_ref[0], gain_ref[...]) * (1.0 + sc_ref[0]) + sh_ref[0]
    z = _dot(h, w_ref[...])
    ngrp = u_ref.shape[1]
    for g in range(ngrp):
        u_ref[0, g] = z[:, g * LANES:(g + 1) * LANES]
    cos = cos_ref[...]
    sin = sin_ref[...]
    quarter = D_HEAD_DIM // 4
    lane = lax.broadcasted_iota(jnp.int32, (1, LANES), 1)
    first_of_pair = (lane // quarter) % 2 == 0

    def norm_rope(x, gain, scale):
        ms = _dot_tab(bd_h_ref[...], bd_l_ref[...], x * x, tab_left=False)
        xn = x * lax.rsqrt(ms + EPS) * gain
        partner = jnp.where(first_of_pair, pltpu.roll(xn, LANES - quarter, 1),
                            pltpu.roll(xn, quarter, 1))
        return ((xn * cos + partner * sin) * scale).astype(BF16)

    nq = qo_ref.shape[-1] // LANES
    for j in range(nq):
        qo_ref[0, :, j * LANES:(j + 1) * LANES] = norm_rope(
            z[:, (ngrp + j) * LANES:(ngrp + j + 1) * LANES], gq_ref[...], D_HEAD_DIM ** -0.5 * LOG2E)
    ko_ref[0] = norm_rope(z[:, (ngrp + nq) * LANES:(ngrp + nq + 1) * LANES], gk_ref[...], 1.0)
    vt = z[:, (ngrp + nq + 1) * LANES:(ngrp + nq + 2) * LANES].T
    ones = jnp.ones((FLASH_ONES, vt.shape[1]), F32)
    vo_ref[0] = jnp.concatenate(
        [piece for g in range(D_KV_HEADS)
         for piece in (vt[g * D_HEAD_DIM:(g + 1) * D_HEAD_DIM], ones)], axis=0).astype(BF16)


def _inproj_qk(x, gain, sc, sh, w_bf16, qk_norm_j, tm):
    bsz, seq, d = x.shape
    n = w_bf16.shape[1]
    qw = D_Q_HEADS * D_HEAD_DIM
    kw = D_KV_HEADS * D_HEAD_DIM
    assert kw == LANES and n == C_GROUPS * C_WIDTH + qw + 2 * kw
    vrows = D_KV_HEADS * (D_HEAD_DIM + FLASH_ONES)
    cos, sin = _rope_tables(seq)
    bd = np.kron(np.eye(LANES // D_HEAD_DIM), np.full((D_HEAD_DIM, D_HEAD_DIM), 1.0 / D_HEAD_DIM))
    bd_h, bd_l = _np_split2(bd)
    reps = LANES // D_HEAD_DIM
    gq = jnp.tile(qk_norm_j[0].astype(F32), reps).reshape(1, LANES)
    gk = jnp.tile(qk_norm_j[1].astype(F32), reps).reshape(1, LANES)
    tab = pl.BlockSpec((tm, LANES), lambda b, i: (i, 0))
    vec = pl.BlockSpec((1, 1, d), lambda b, i: (b, 0, 0))
    return pl.pallas_call(
        _inproj_qk_kernel,
        out_shape=(jax.ShapeDtypeStruct((bsz, C_GROUPS, seq, C_WIDTH), F32),
                   jax.ShapeDtypeStruct((bsz, seq, qw), BF16),
                   jax.ShapeDtypeStruct((bsz, seq, kw), BF16),
                   jax.ShapeDtypeStruct((bsz, vrows, seq), BF16)),
        grid=(bsz, seq // tm),
        in_specs=[pl.BlockSpec((1, tm, d), lambda b, i: (b, i, 0)),
                  _const_spec((1, d)), vec, vec, _const_spec((d, n)),
                  tab, tab, _const_spec(bd_h.shape), _const_spec(bd_l.shape),
                  _const_spec((1, LANES)), _const_spec((1, LANES))],
        out_specs=(pl.BlockSpec((1, C_GROUPS, tm, C_WIDTH), lambda b, i: (b, 0, i, 0)),
                   _row_spec(tm, qw), _row_spec(tm, kw),
                   pl.BlockSpec((1, vrows, tm), lambda b, i: (b, 0, i))),
        compiler_params=_cparams("parallel", "parallel"),
        name="inproj_qk",
    )(x, gain.reshape(1, d), sc.reshape(bsz, 1, d), sh.reshape(bsz, 1, d), w_bf16,
      cos, sin, bd_h, bd_l, gq, gk)


def _flash_kernel(q_ref, k_ref, vt_ref, o_ref, m_sc, acc_sc, s_sc):
    kv = pl.program_id(2)

    @pl.when(kv == 0)
    def _():
        m_sc[...] = jnp.full_like(m_sc, -jnp.inf)
        acc_sc[...] = jnp.zeros_like(acc_sc)

    lane = lax.broadcasted_iota(jnp.int32, (1, LANES), 1)
    lo = lane < D_HEAD_DIM
    nblk = q_ref.shape[-1] // LANES
    tq, tk = q_ref.shape[1], k_ref.shape[1]
    ku, qu = FLASH_KEY_UNIT, min(FLASH_QUERY_UNIT, tq)
    grows = D_HEAD_DIM + FLASH_ONES
    nheads = nblk * D_KV_HEADS
    k = k_ref[0]

    nchunk = tq // qu

    def logits_chunk(idx, c):
        j, g = divmod(idx, D_KV_HEADS)
        qj = q_ref[0, c * qu:(c + 1) * qu, j * LANES:(j + 1) * LANES]
        sel = lo if g == 0 else jnp.logical_not(lo)
        s = lax.dot_general(k, jnp.where(sel, qj, jnp.zeros_like(qj)), (((1,), (1,)), ((), ())),
                            preferred_element_type=F32)
        s_sc[idx % s_sc.shape[0], :, c * qu:(c + 1) * qu] = s
        return jnp.max(s, axis=0, keepdims=True)

    def finish_logits(idx, mcs):
        m_prev = m_sc[idx, 0:1, :]
        m_new = jnp.maximum(m_prev, jnp.concatenate(mcs, axis=1))
        m_sc[idx, 0:1, :] = m_new
        return m_new, jnp.exp2(m_prev - m_new)

    def value_chunk(idx, c, m_new, alpha):
        j, g = divmod(idx, D_KV_HEADS)
        rows = slice(g * grows, (g + 1) * grows)
        qcols = slice(c * qu, (c + 1) * qu)
        pv = None
        for u in range(tk // ku):
            keys = slice(u * ku, (u + 1) * ku)
            p = jnp.exp2(s_sc[idx % s_sc.shape[0], keys, qcols] - m_new[:, qcols])
            d = jnp.dot(vt_ref[0, rows, keys], p.astype(BF16), preferred_element_type=F32)
            pv = d if pv is None else pv + d
        acc_sc[j, rows, qcols] = alpha[:, qcols] * acc_sc[j, rows, qcols] + pv

    def logits_pass(idx):
        return finish_logits(idx, [logits_chunk(idx, c) for c in range(nchunk)])

    pending = [logits_pass(i) for i in range(min(FLASH_AHEAD, nheads))]
    for idx in range(nheads):
        if idx + FLASH_AHEAD < nheads:
            pending.append(logits_pass(idx + FLASH_AHEAD))
        stats = pending.pop(0)
        for c in range(nchunk):
            value_chunk(idx, c, *stats)

    @pl.when(kv == pl.num_programs(2) - 1)
    def _():
        for j in range(nblk):
            parts = []
            for g in range(D_KV_HEADS):
                num = acc_sc[j, g * grows:g * grows + D_HEAD_DIM, :]
                den = acc_sc[j, g * grows + D_HEAD_DIM:g * grows + D_HEAD_DIM + 1, :]
                parts.append(num / den)
            o_ref[0, :, j * LANES:(j + 1) * LANES] = (
                jnp.concatenate(parts, axis=0).T.astype(o_ref.dtype))


def _flash(q, k, vt, tq, tk):
    bsz, seq, qw = q.shape
    kw = k.shape[-1]
    vrows = vt.shape[1]
    assert FLASH_LOGIT_BUFS > FLASH_AHEAD
    return pl.pallas_call(
        _flash_kernel,
        out_shape=jax.ShapeDtypeStruct((bsz, seq, qw), BF16),
        grid=(bsz, seq // tq, seq // tk),
        in_specs=[pl.BlockSpec((1, tq, qw), lambda b, i, j: (b, i, 0)),
                  pl.BlockSpec((1, tk, kw), lambda b, i, j: (b, j, 0)),
                  pl.BlockSpec((1, vrows, tk), lambda b, i, j: (b, 0, j))],
        out_specs=pl.BlockSpec((1, tq, qw), lambda b, i, j: (b, i, 0)),
        scratch_shapes=[pltpu.VMEM((D_Q_HEADS, 8, tq), F32),
                        pltpu.VMEM((qw // LANES, vrows, tq), F32),
                        pltpu.VMEM((FLASH_LOGIT_BUFS, tk, tq), F32)],
        compiler_params=_cparams("parallel", "parallel", "arbitrary"),
        name="flash",
    )(q, k, vt)


def kernel(x, c, t5_bias, hgrn_lb_logits, ada_w, ada_b, norm_gains, ab_w_in, ab_w_out,
           hgrn_out_norm, cd_w_in, cd_w_out, qk_norm, ffn_w_in, ffn_w_out):
    bsz, seq, d = x.shape
    depth = ada_w.shape[0]
    mod = _ada_mod(c.astype(F32), ada_w, ada_b)
    perm = _head_perm()
    aw = A_HEADS * A_HEAD_DIM
    cw = C_GROUPS * C_WIDTH
    qw = D_Q_HEADS * D_HEAD_DIM
    tm_in = min(512, seq)
    ffn_wi = ffn_w_in.astype(BF16)
    ffn_wo = ffn_w_out.astype(BF16)
    for layer in range(depth):
        sh_m, sc_m, g_m, sh_f, sc_f, g_f = [mod[layer, :, i * d:(i + 1) * d] for i in range(6)]
        gains = norm_gains[layer]
        j = layer // 2
        tail = dict(x=x, gain_m=gains[1], gate_m=g_m, g1=gains[2], sc=sc_f, sh=sh_f,
                    wi_bf16=ffn_wi, wo_bf16=ffn_wo, layer=layer, g2=gains[3], gate_f=g_f)
        if layer % 2 == 0:
            w_in = ab_w_in[j].astype(BF16)
            *qkv_cm, z = _inproj_cm(x, gains[0], sc_m, sh_m, w_in, 3 * aw, tm_in)
            branches = [_dilated_branch(cm, t5_bias, window, dil)
                        for cm, (window, dil) in zip(qkv_cm, DIL_CFG)]
            o_f, o_b = _hgrn(z, hgrn_lb_logits, layer, 0)
            g_col = 3 * B_HEADS * B_DK + B_HEADS * B_DV
            x = _tail0(branches, o_f, o_b, z, g_col, hgrn_out_norm[j], ab_w_out[j].astype(BF16),
                       tm=min(TAIL_SUB, seq), **tail)
        else:
            w_full = cd_w_in[j]
            w_in = jnp.concatenate([w_full[:, :cw], w_full[:, cw:cw + qw][:, perm],
                                    w_full[:, cw + qw:]], axis=1).astype(BF16)
            u, qn, kn, vn = _inproj_qk(x, gains[0], sc_m, sh_m, w_in, qk_norm[j], min(1024, seq))
            c_out = _fourier_mixer(u)
            d_out = _flash(qn, kn, vn, min(FLASH_TQ, seq), min(FLASH_TK, seq))
            w_out_full = cd_w_out[j]
            w_out = jnp.concatenate([w_out_full[:cw], w_out_full[cw:][perm]], axis=0).astype(BF16)
            x = _tail1(c_out, d_out, w_out, tm=min(TAIL_BLOCK, seq), **tail)
    return x
```
